```python
import jax, jax.numpy as jnp
from jax import lax
import numpy as np

D_MODEL = 2048
BATCH = 8
SEQ = 4096
DEPTH = 1

D_MIX = D_MODEL
HEAD_DIM = 128
RET_HEADS = (D_MIX // 2) // HEAD_DIM
SB_HEADS = (D_MIX // 2) // HEAD_DIM
RET_WIDTH = RET_HEADS * HEAD_DIM
SB_WIDTH = SB_HEADS * HEAD_DIM
IN_COLS = 4 * RET_WIDTH + 4 * SB_WIDTH
CHUNK = 128
Q_BLOCK = 128
ROPE_THETA = 10000.0
EPS = 1e-6

kernel_name = "hybrid_retention_stickbreaking_block"


def rmsnorm(x, g):
    xf = x.astype(jnp.float32)
    y = xf * lax.rsqrt(jnp.mean(xf * xf, axis=-1, keepdims=True) + EPS)
    return (y * g.astype(jnp.float32)).astype(x.dtype)


def rotary(x):
    S, d = x.shape[1], x.shape[-1]
    half = d // 2
    inv = ROPE_THETA ** (-jnp.arange(half, dtype=jnp.float32) / half)
    ang = jnp.arange(S, dtype=jnp.float32)[:, None] * inv[None, :]
    cos = jnp.cos(ang)[None, :, None, :]
    sin = jnp.sin(ang)[None, :, None, :]
    xf = x.astype(jnp.float32)
    x1, x2 = xf[..., :half], xf[..., half:]
    out = jnp.concatenate([x1 * cos - x2 * sin, x1 * sin + x2 * cos], axis=-1)
    return out.astype(x.dtype)


def retention(q, k, v):
    B, S, H, d = q.shape
    nc = S // CHUNK
    dt = q.dtype
    lg = jnp.log1p(-jnp.exp2(-5.0 - jnp.arange(H, dtype=jnp.float32)))

    def to_chunks(t):
        return t.reshape(B, nc, CHUNK, H, t.shape[-1]).transpose(0, 3, 1, 2, 4)

    qc, kc, vc = to_chunks(q), to_chunks(k) * (d ** -0.5), to_chunks(v)
    n = jnp.arange(CHUNK, dtype=jnp.float32)
    rel = n[:, None] - n[None, :]
    decay = jnp.where(rel >= 0, jnp.exp(lg[:, None, None] * jnp.maximum(rel, 0.0)), 0.0)
    xi = jnp.exp(lg[:, None] * (n + 1.0))
    zeta = jnp.exp(lg[:, None] * (CHUNK - 1.0 - n))
    gamma_c = jnp.exp(lg * CHUNK).astype(dt)

    scores = jnp.einsum('bhcnd,bhcmd->bhcnm', qc, kc) * decay[None, :, None].astype(dt)
    intra = jnp.einsum('bhcnm,bhcme->bhcne', scores, vc)

    kv = jnp.einsum('bhcmd,bhcme->cbhde', kc * zeta[None, :, None, :, None].astype(dt), vc)

    def step(R, kv_c):
        return gamma_c[None, :, None, None] * R + kv_c, R

    _, R_prev = lax.scan(step, jnp.zeros_like(kv[0]), kv)
    cross = jnp.einsum('bhcnd,cbhde->bhcne', qc * xi[None, :, None, :, None].astype(dt), R_prev)
    out = intra + cross
    return out.transpose(0, 2, 3, 1, 4).reshape(B, S, H, -1)


def stick_breaking(q, k, v):
    S, d = q.shape[1], q.shape[-1]
    scale = d ** -0.5
    qh, kh, vh = (t.transpose(0, 2, 1, 3) for t in (q, k, v))
    outs = []
    for start in range(0, S, Q_BLOCK):
        end = start + Q_BLOCK
        qb, kb, vb = qh[:, :, start:end], kh[:, :, :end], vh[:, :, :end]
        z = jnp.einsum('bhtd,bhsd->bhts', qb, kb).astype(jnp.float32) * scale
        t_idx = start + jnp.arange(Q_BLOCK)[:, None]
        s_idx = jnp.arange(end)[None, :]
        causal = s_idx < t_idx
        log_keep = jnp.where(causal, jax.nn.log_sigmoid(-z), 0.0)
        log_a = jax.nn.log_sigmoid(z) + lax.cumsum(log_keep, axis=3, reverse=True) - log_keep
        a = jnp.where(causal, jnp.exp(log_a), 0.0).astype(vb.dtype)
        outs.append(jnp.einsum('bhts,bhse->bhte', a, vb))
    return jnp.concatenate(outs, axis=2).transpose(0, 2, 1, 3)


def head_groupnorm(y, g, b):
    B, S, H, e = y.shape
    yf = y.astype(jnp.float32)
    mu = jnp.mean(yf, axis=-1, keepdims=True)
    var = jnp.mean(jnp.square(yf - mu), axis=-1, keepdims=True)
    yn = ((yf - mu) * lax.rsqrt(var + EPS)).reshape(B, S, H * e)
    return (yn * g.astype(jnp.float32) + b.astype(jnp.float32)).astype(y.dtype)


def head_rmsnorm(y, g):
    B, S, H, e = y.shape
    yf = y.astype(jnp.float32)
    yn = (yf * lax.rsqrt(jnp.mean(yf * yf, axis=-1, keepdims=True) + EPS)).reshape(B, S, H * e)
    return (yn * g.astype(jnp.float32)).astype(y.dtype)


def _fwd_setup_inputs(seed: int = 0) -> dict:
    key = jax.random.key(seed)
    ks = jax.random.split(key, 9)
    f32 = jnp.float32
    x = jax.random.normal(ks[0], (BATCH, SEQ, D_MODEL), f32)
    norm_gain = 1.0 + 0.02 * jax.random.normal(ks[1], (DEPTH, D_MODEL), f32)
    w_in = jax.random.normal(ks[2], (DEPTH, D_MODEL, IN_COLS), f32) * D_MODEL ** -0.5
    ret_gn_gain = 1.0 + 0.02 * jax.random.normal(ks[3], (DEPTH, RET_WIDTH), f32)
    ret_gn_bias = 0.02 * jax.random.normal(ks[4], (DEPTH, RET_WIDTH), f32)
    sb_norm_gain = 1.0 + 0.02 * jax.random.normal(ks[5], (DEPTH, SB_WIDTH), f32)
    w_out = jax.random.normal(ks[6], (DEPTH, D_MIX, D_MODEL), f32) * D_MIX ** -0.5
    final_norm_gain = 1.0 + 0.02 * jax.random.normal(ks[7], (D_MODEL,), f32)
    return {"x": x, "norm_gain": norm_gain, "w_in": w_in, "ret_gn_gain": ret_gn_gain,
            "ret_gn_bias": ret_gn_bias, "sb_norm_gain": sb_norm_gain, "w_out": w_out,
            "final_norm_gain": final_norm_gain}


def _fwd_reference(x, norm_gain, w_in, ret_gn_gain, ret_gn_bias, sb_norm_gain, w_out, final_norm_gain):
    B, S, _ = x.shape
    splits = [RET_WIDTH * i for i in range(1, 5)] + [4 * RET_WIDTH + SB_WIDTH * i for i in range(1, 4)]
    for l in range(DEPTH):
        h = rmsnorm(x, norm_gain[l])
        proj = jnp.einsum('bsd,dc->bsc', h, w_in[l])
        rq, rk, rv, rg, sq, sk, sv, sg = jnp.split(proj, splits, axis=-1)
        heads = lambda t, H: t.reshape(B, S, H, HEAD_DIM)
        ret = retention(rotary(heads(rq, RET_HEADS)), rotary(heads(rk, RET_HEADS)), heads(rv, RET_HEADS))
        ret = jax.nn.silu(rg) * head_groupnorm(ret, ret_gn_gain[l], ret_gn_bias[l])
        sb = stick_breaking(heads(sq, SB_HEADS), heads(sk, SB_HEADS), heads(sv, SB_HEADS))
        sb = jax.nn.silu(sg) * head_rmsnorm(sb, sb_norm_gain[l])
        mix = jnp.concatenate([ret, sb], axis=-1)
        x = x + jnp.einsum('bsc,cd->bsd', mix, w_out[l])
    return rmsnorm(x, final_norm_gain)


import jax as _jax
import jax.numpy as _jnp

TWIN_FORMAT = 'train_step'
FWD_PARAMS = ['x', 'norm_gain', 'w_in', 'ret_gn_gain', 'ret_gn_bias', 'sb_norm_gain', 'w_out', 'final_norm_gain']
TWIN_WEIGHTS = ['norm_gain', 'w_in', 'ret_gn_gain', 'ret_gn_bias', 'sb_norm_gain', 'w_out', 'final_norm_gain']
TWIN_DIFF_INPUT = 'x'
TWIN_INPUTS = ['x', 'norm_gain', 'w_in', 'ret_gn_gain', 'ret_gn_bias', 'sb_norm_gain', 'w_out', 'final_norm_gain', 'loss_target', 'm_norm_gain', 'm_w_in', 'm_ret_gn_gain', 'm_ret_gn_bias', 'm_sb_norm_gain', 'm_w_out', 'm_final_norm_gain', 'v_norm_gain', 'v_w_in', 'v_ret_gn_gain', 'v_ret_gn_bias', 'v_sb_norm_gain', 'v_w_out', 'v_final_norm_gain']
TWIN_OUTPUTS = ['loss', 'grad_x', 'grad_norm_gain', 'grad_w_in', 'grad_ret_gn_gain', 'grad_ret_gn_bias', 'grad_sb_norm_gain', 'grad_w_out', 'grad_final_norm_gain', 'delta_norm_gain', 'delta_w_in', 'delta_ret_gn_gain', 'delta_ret_gn_bias', 'delta_sb_norm_gain', 'delta_w_out', 'delta_final_norm_gain', 'new_m_norm_gain', 'new_m_w_in', 'new_m_ret_gn_gain', 'new_m_ret_gn_bias', 'new_m_sb_norm_gain', 'new_m_w_out', 'new_m_final_norm_gain', 'new_v_norm_gain', 'new_v_w_in', 'new_v_ret_gn_gain', 'new_v_ret_gn_bias', 'new_v_sb_norm_gain', 'new_v_w_out', 'new_v_final_norm_gain']
TWIN_LEAF_KINDS = {'loss': 'loss', 'grad_x': 'grad_x', 'grad_norm_gain': 'grad_w', 'grad_w_in': 'grad_w', 'grad_ret_gn_gain': 'grad_w', 'grad_ret_gn_bias': 'grad_w', 'grad_sb_norm_gain': 'grad_w', 'grad_w_out': 'grad_w', 'grad_final_norm_gain': 'grad_w', 'delta_norm_gain': 'delta_w', 'delta_w_in': 'delta_w', 'delta_ret_gn_gain': 'delta_w', 'delta_ret_gn_bias': 'delta_w', 'delta_sb_norm_gain': 'delta_w', 'delta_w_out': 'delta_w', 'delta_final_norm_gain': 'delta_w', 'new_m_norm_gain': 'new_m', 'new_m_w_in': 'new_m', 'new_m_ret_gn_gain': 'new_m', 'new_m_ret_gn_bias': 'new_m', 'new_m_sb_norm_gain': 'new_m', 'new_m_w_out': 'new_m', 'new_m_final_norm_gain': 'new_m', 'new_v_norm_gain': 'new_v', 'new_v_w_in': 'new_v', 'new_v_ret_gn_gain': 'new_v', 'new_v_ret_gn_bias': 'new_v', 'new_v_sb_norm_gain': 'new_v', 'new_v_w_out': 'new_v', 'new_v_final_norm_gain': 'new_v'}


def _forward(args):
    return _fwd_reference(*[args[k] for k in FWD_PARAMS])


def _output_shape():
    out = _jax.eval_shape(lambda: _forward(_fwd_setup_inputs(0)))
    return out.shape, out.dtype

N_MICROBATCH = 1
ADAM_LR = 0.001
ADAM_B1 = 0.9
ADAM_B2 = 0.999
ADAM_EPS = 1e-08
ADAM_WD = 0.01
ADAM_STEP = 10
PER_EXAMPLE_BATCH_AXIS = {'x': 0, 'loss_target': 0}
SHARED_INPUTS = []
_WEIGHT_DTYPES = {'norm_gain': _jnp.float32, 'w_in': _jnp.float32, 'ret_gn_gain': _jnp.float32, 'ret_gn_bias': _jnp.float32, 'sb_norm_gain': _jnp.float32, 'w_out': _jnp.float32, 'final_norm_gain': _jnp.float32}
MOMENT_SCALE = {'norm_gain': 8.662403e-02, 'w_in': 4.144631e-02, 'ret_gn_gain': 4.686681e-02, 'ret_gn_bias': 4.724950e-02, 'sb_norm_gain': 4.415329e-02, 'w_out': 4.535685e-02, 'final_norm_gain': 1.598684e+01}


def _to_microbatches(a, axis):
    t = _jnp.moveaxis(a, axis, 0)
    t = t.reshape((N_MICROBATCH, t.shape[0] // N_MICROBATCH) + t.shape[1:])
    return _jnp.moveaxis(t, 1, axis + 1)


def setup_inputs(seed: int = 0) -> dict:
    inp = _fwd_setup_inputs(seed)
    key = _jax.random.fold_in(_jax.random.key(seed), 7919)
    shape, _ = _output_shape()
    out = dict(inp)
    out["loss_target"] = _jax.random.normal(_jax.random.fold_in(key, 0), shape, _jnp.float32)
    for i, name in enumerate(TWIN_WEIGHTS):
        w = inp[name].astype(_jnp.float32)
        if MOMENT_SCALE is None:
            s = _jnp.sqrt(_jnp.mean(_jnp.square(w)) + 1e-30)
        else:
            s = MOMENT_SCALE[name]
        km, kv = _jax.random.split(_jax.random.fold_in(key, i + 1))
        out[name] = w
        out["m_" + name] = s * _jax.random.normal(km, w.shape, _jnp.float32)
        out["v_" + name] = (s * s) * _jax.random.uniform(kv, w.shape, _jnp.float32, 0.5, 1.5)
    if N_MICROBATCH > 1:
        for name, axis in PER_EXAMPLE_BATCH_AXIS.items():
            out[name] = _to_microbatches(out[name], axis)
    return {'x': out['x'], 'norm_gain': out['norm_gain'], 'w_in': out['w_in'], 'ret_gn_gain': out['ret_gn_gain'], 'ret_gn_bias': out['ret_gn_bias'], 'sb_norm_gain': out['sb_norm_gain'], 'w_out': out['w_out'], 'final_norm_gain': out['final_norm_gain'], 'loss_target': out['loss_target'], 'm_norm_gain': out['m_norm_gain'], 'm_w_in': out['m_w_in'], 'm_ret_gn_gain': out['m_ret_gn_gain'], 'm_ret_gn_bias': out['m_ret_gn_bias'], 'm_sb_norm_gain': out['m_sb_norm_gain'], 'm_w_out': out['m_w_out'], 'm_final_norm_gain': out['m_final_norm_gain'], 'v_norm_gain': out['v_norm_gain'], 'v_w_in': out['v_w_in'], 'v_ret_gn_gain': out['v_ret_gn_gain'], 'v_ret_gn_bias': out['v_ret_gn_bias'], 'v_sb_norm_gain': out['v_sb_norm_gain'], 'v_w_out': out['v_w_out'], 'v_final_norm_gain': out['v_final_norm_gain']}


def _loss(weights, diff, rest, loss_target):
    with _jax.named_scope("forward"):
        args = {**rest, TWIN_DIFF_INPUT: diff, **{k: w.astype(_WEIGHT_DTYPES[k]) for k, w in weights.items()}}
        y = _forward(args)
    with _jax.named_scope("loss_head"):
        err = _jnp.square(y.astype(_jnp.float32) - loss_target)
        return 0.5 * _jnp.sum(_jnp.mean(err, axis=-1)) if err.ndim else 0.5 * err


def _adamw(w, g, m, v):
    m = ADAM_B1 * m + (1.0 - ADAM_B1) * g
    v = ADAM_B2 * v + (1.0 - ADAM_B2) * _jnp.square(g)
    m_hat = m / (1.0 - ADAM_B1 ** ADAM_STEP)
    v_hat = v / (1.0 - ADAM_B2 ** ADAM_STEP)
    delta = -ADAM_LR * (m_hat / (_jnp.sqrt(v_hat) + ADAM_EPS) + ADAM_WD * w)
    return delta, m, v


def reference(x, norm_gain, w_in, ret_gn_gain, ret_gn_bias, sb_norm_gain, w_out, final_norm_gain, loss_target, m_norm_gain, m_w_in, m_ret_gn_gain, m_ret_gn_bias, m_sb_norm_gain, m_w_out, m_final_norm_gain, v_norm_gain, v_w_in, v_ret_gn_gain, v_ret_gn_bias, v_sb_norm_gain, v_w_out, v_final_norm_gain):
    given = dict(x=x, norm_gain=norm_gain, w_in=w_in, ret_gn_gain=ret_gn_gain, ret_gn_bias=ret_gn_bias, sb_norm_gain=sb_norm_gain, w_out=w_out, final_norm_gain=final_norm_gain, loss_target=loss_target, m_norm_gain=m_norm_gain, m_w_in=m_w_in, m_ret_gn_gain=m_ret_gn_gain, m_ret_gn_bias=m_ret_gn_bias, m_sb_norm_gain=m_sb_norm_gain, m_w_out=m_w_out, m_final_norm_gain=m_final_norm_gain, v_norm_gain=v_norm_gain, v_w_in=v_w_in, v_ret_gn_gain=v_ret_gn_gain, v_ret_gn_bias=v_ret_gn_bias, v_sb_norm_gain=v_sb_norm_gain, v_w_out=v_w_out, v_final_norm_gain=v_final_norm_gain)
    weights = {n: given[n] for n in TWIN_WEIGHTS}
    shared = {n: given[n] for n in SHARED_INPUTS}
    per_example = {n: given[n] for n in ['x']}
    grad_fn = _jax.value_and_grad(_loss, argnums=(0, 1))

    def one_microbatch(ex, loss_target):
        ex = dict(ex)
        diff = ex.pop(TWIN_DIFF_INPUT)
        return grad_fn(weights, diff, {**shared, **ex}, loss_target)

    if N_MICROBATCH == 1:
        loss, (grad_w, grad_x) = one_microbatch(per_example, given["loss_target"])
    else:
        def body(carry, xs):
            loss_sum, grad_sum = carry
            l_k, (gw_k, gx_k) = one_microbatch(xs[0], xs[1])
            with _jax.named_scope("update"):
                return (loss_sum + l_k, _jax.tree.map(_jnp.add, grad_sum, gw_k)), gx_k

        init = (_jnp.zeros((), _jnp.float32), _jax.tree.map(_jnp.zeros_like, weights))
        (loss, grad_w), grad_x = _jax.lax.scan(body, init, (per_example, given["loss_target"]))
    with _jax.named_scope("update"):
        delta_w, new_m, new_v = {}, {}, {}
        for n in TWIN_WEIGHTS:
            delta_w[n], new_m[n], new_v[n] = _adamw(weights[n], grad_w[n], given["m_" + n], given["v_" + n])
    return (loss, grad_x, *[grad_w[n] for n in TWIN_WEIGHTS], *[delta_w[n] for n in TWIN_WEIGHTS],
            *[new_m[n] for n in TWIN_WEIGHTS], *[new_v[n] for n in TWIN_WEIGHTS])
```

```python
import functools

import jax
import jax.numpy as jnp
from jax import lax
from jax.experimental import pallas as pl
from jax.experimental.pallas import tpu as pltpu

F32 = jnp.float32
BF16 = jnp.bfloat16

HEAD_DIM = 128
CHUNK = 128
ROPE_THETA = 10000.0
EPS = 1e-6
ADAM_LR = 0.001
ADAM_B1 = 0.9
ADAM_B2 = 0.999
ADAM_EPS = 1e-08
ADAM_WD = 0.01
ADAM_STEP = 10

N_DEV = 8
LANES = 128
SUBLANES = 8
VMEM_LIMIT = 56 * 1024 * 1024
SB_BLOCK = 256
SMALL_N = 8192
MESH = pl.DeviceIdType.MESH

NT = (((1,), (1,)), ((), ()))
TN = (((0,), (0,)), ((), ()))


def _params(*sem):
    return pltpu.CompilerParams(dimension_semantics=sem if sem else None, vmem_limit_bytes=VMEM_LIMIT)


def _dot(a, b):
    return jnp.dot(a, b, preferred_element_type=F32)


def _dot_nt(a, b):
    return lax.dot_general(a, b, NT, preferred_element_type=F32)


def _dot_tn(a, b):
    return lax.dot_general(a, b, TN, preferred_element_type=F32)


def _sigmoid(g):
    return 1.0 / (1.0 + jnp.exp(-g))


def _rot(a, cos, sin_signed):
    return a * cos + pltpu.roll(a, HEAD_DIM // 2, 1) * sin_signed


def _mesh_pos():
    return lax.axis_index("x"), lax.axis_index("y"), lax.axis_index("c")


def _all_gather_weights(win_b, wout_b):
    n_arr = 2

    def body(a_ref, b_ref, oa_ref, ob_ref, send_sems, recv_sems, local_sems):
        x, y, c = _mesh_pos()
        me, sibling = (x, y, c), (x, y, 1 - c)
        chips = [(1 - x, y), (x, 1 - y), (1 - x, 1 - y)]
        srcs, outs = (a_ref, b_ref), (oa_ref, ob_ref)

        def slot(arr, dev):
            px, py, pc = dev
            return outs[arr].at[4 * px + 2 * py + pc]

        def copy(arr, k, block, to, src=None):
            dst = slot(arr, block)
            return pltpu.make_async_remote_copy(
                src_ref=dst if src is None else src, dst_ref=dst,
                send_sem=send_sems.at[arr, k], recv_sem=recv_sems.at[arr, k],
                device_id=to, device_id_type=MESH)

        mine = [pltpu.make_async_copy(srcs[arr], slot(arr, me), local_sems.at[arr]) for arr in range(n_arr)]
        for cp in mine:
            cp.start()
        first = []
        for arr in range(n_arr):
            first.append(copy(arr, 0, me, sibling, src=srcs[arr]))
            first += [copy(arr, 1 + j, me, (*chip, c), src=srcs[arr]) for j, chip in enumerate(chips)]
        for cp in first:
            cp.start()
        passed = []
        for j, chip in enumerate(chips):
            for arr in range(n_arr):
                copy(arr, 1 + j, (*chip, c), me).wait_recv()
                fwd = copy(arr, 4 + j, (*chip, c), sibling)
                fwd.start()
                passed.append(fwd)
        for arr in range(n_arr):
            copy(arr, 0, sibling, me).wait_recv()
            for j, chip in enumerate(chips):
                copy(arr, 4 + j, (*chip, 1 - c), me).wait_recv()
        for cp in first + passed:
            cp.wait_send()
        for cp in mine:
            cp.wait()

    hbm = pl.BlockSpec(memory_space=pltpu.HBM)
    return pl.pallas_call(
        body, name="all_gather_weights",
        out_shape=(jax.ShapeDtypeStruct((N_DEV,) + win_b.shape, win_b.dtype),
                   jax.ShapeDtypeStruct((N_DEV,) + wout_b.shape, wout_b.dtype)),
        in_specs=[hbm, hbm], out_specs=(hbm, hbm),
        scratch_shapes=[pltpu.SemaphoreType.DMA((n_arr, 7)), pltpu.SemaphoreType.DMA((n_arr, 7)),
                        pltpu.SemaphoreType.DMA((n_arr,))],
    )(win_b, wout_b)


def _rs_sibling(gw, gwo):
    n_arr = 2

    def body(a_ref, b_ref, oa_ref, ob_ref, send_sems, recv_sems):
        x, y, c = _mesh_pos()
        sibling = (x, y, 1 - c)
        srcs, outs = (a_ref, b_ref), (oa_ref, ob_ref)
        copies = []
        for arr in range(n_arr):
            for q in range(4):
                copies.append(pltpu.make_async_remote_copy(
                    src_ref=srcs[arr].at[2 * q + (1 - c)], dst_ref=outs[arr].at[q],
                    send_sem=send_sems.at[arr, q], recv_sem=recv_sems.at[arr, q],
                    device_id=sibling, device_id_type=MESH))
        for cp in copies:
            cp.start()
        for cp in copies:
            cp.wait_recv()
        for cp in copies:
            cp.wait_send()

    hbm = pl.BlockSpec(memory_space=pltpu.HBM)
    return pl.pallas_call(
        body, name="rs_sibling",
        out_shape=(jax.ShapeDtypeStruct((4,) + gw.shape[1:], gw.dtype),
                   jax.ShapeDtypeStruct((4,) + gwo.shape[1:], gwo.dtype)),
        in_specs=[hbm, hbm], out_specs=(hbm, hbm),
        scratch_shapes=[pltpu.SemaphoreType.DMA((n_arr, 4)), pltpu.SemaphoreType.DMA((n_arr, 4))],
    )(gw, gwo)


def _rs_chips(sb, sbo):
    n_arr = 2

    def body(a_ref, b_ref, oa_ref, ob_ref, send_sems, recv_sems):
        x, y, c = _mesh_pos()
        srcs, outs = (a_ref, b_ref), (oa_ref, ob_ref)
        copies = []
        for arr in range(n_arr):
            for k in range(1, 4):
                kx, ky = k >> 1, k & 1
                px = 1 - x if kx else x
                py = 1 - y if ky else y
                copies.append(pltpu.make_async_remote_copy(
                    src_ref=srcs[arr].at[2 * px + py], dst_ref=outs[arr].at[k - 1],
                    send_sem=send_sems.at[arr, k - 1], recv_sem=recv_sems.at[arr, k - 1],
                    device_id=(px, py, c), device_id_type=MESH))
        for cp in copies:
            cp.start()
        for cp in copies:
            cp.wait_recv()
        for cp in copies:
            cp.wait_send()

    hbm = pl.BlockSpec(memory_space=pltpu.HBM)
    return pl.pallas_call(
        body, name="rs_chips",
        out_shape=(jax.ShapeDtypeStruct((3,) + sb.shape[1:], sb.dtype),
                   jax.ShapeDtypeStruct((3,) + sbo.shape[1:], sbo.dtype)),
        in_specs=[hbm, hbm], out_specs=(hbm, hbm),
        scratch_shapes=[pltpu.SemaphoreType.DMA((n_arr, 3)), pltpu.SemaphoreType.DMA((n_arr, 3))],
    )(sb, sbo)


def _small_all_gather(small):
    rows, n = small.shape

    def body(s_ref, o_ref, send_sems, recv_sems, local_sem):
        x, y, c = _mesh_pos()
        mine = pltpu.make_async_copy(s_ref, o_ref.at[4 * x + 2 * y + c], local_sem)
        mine.start()
        copies = []
        for k in range(1, N_DEV):
            px = 1 - x if (k >> 2) & 1 else x
            py = 1 - y if (k >> 1) & 1 else y
            pc = 1 - c if k & 1 else c
            copies.append((pltpu.make_async_remote_copy(
                src_ref=s_ref, dst_ref=o_ref.at[4 * x + 2 * y + c],
                send_sem=send_sems.at[k - 1], recv_sem=recv_sems.at[k - 1],
                device_id=(px, py, pc), device_id_type=MESH), (px, py, pc)))
        for cp, _ in copies:
            cp.start()
        for k, (cp, (px, py, pc)) in enumerate(copies):
            pltpu.make_async_remote_copy(
                src_ref=s_ref, dst_ref=o_ref.at[4 * px + 2 * py + pc],
                send_sem=send_sems.at[k], recv_sem=recv_sems.at[k],
                device_id=(px, py, pc), device_id_type=MESH).wait_recv()
        for cp, _ in copies:
            cp.wait_send()
        mine.wait()

    vmem = pl.BlockSpec(memory_space=pltpu.VMEM)
    return pl.pallas_call(
        body, name="small_all_gather",
        out_shape=jax.ShapeDtypeStruct((N_DEV, rows, n), small.dtype),
        in_specs=[vmem], out_specs=vmem,
        scratch_shapes=[pltpu.SemaphoreType.DMA((N_DEV - 1,)), pltpu.SemaphoreType.DMA((N_DEV - 1,)),
                        pltpu.SemaphoreType.DMA],
    )(small)


def _rmsnorm_fwd(x, gain):
    S, D = x.shape
    tm = min(512, S)

    def body(x_ref, g_ref, h_ref):
        xv = x_ref[...]
        r = lax.rsqrt(jnp.mean(xv * xv, axis=-1, keepdims=True) + EPS)
        h_ref[...] = (xv * r * g_ref[...]).astype(BF16)

    return pl.pallas_call(
        body, name="rmsnorm_fwd", out_shape=jax.ShapeDtypeStruct((S, D), BF16), grid=(S // tm,),
        in_specs=[pl.BlockSpec((tm, D), lambda i: (i, 0)), pl.BlockSpec((1, D), lambda i: (0, 0))],
        out_specs=pl.BlockSpec((tm, D), lambda i: (i, 0)),
        compiler_params=_params("parallel"),
    )(x, gain)


def _in_proj(h, w_all, cos, sin):
    S, D = h.shape
    n_proj, _, W = w_all.shape
    tm = min(512, S)

    def body(h_ref, w_ref, cos_ref, sin_ref, o_ref):
        j = pl.program_id(0)
        acc = _dot(h_ref[...], w_ref[...])

        @pl.when(j >= 2)
        def _():
            o_ref[...] = acc

        @pl.when(j < 2)
        def _():
            scale = jnp.where(j == 1, HEAD_DIM ** -0.5, 1.0).astype(F32)
            cs, sn = cos_ref[...], sin_ref[...]
            for hh in range(W // HEAD_DIM):
                cols = slice(hh * HEAD_DIM, (hh + 1) * HEAD_DIM)
                o_ref[:, cols] = _rot(acc[:, cols], cs, sn) * scale

    return pl.pallas_call(
        body, name="in_proj", out_shape=jax.ShapeDtypeStruct((n_proj, S, W), F32), grid=(n_proj, S // tm),
        in_specs=[pl.BlockSpec((tm, D), lambda j, i: (i, 0)),
                  pl.BlockSpec((None, D, W), lambda j, i: (j, 0, 0)),
                  pl.BlockSpec((tm, HEAD_DIM), lambda j, i: (i, 0)),
                  pl.BlockSpec((tm, HEAD_DIM), lambda j, i: (i, 0))],
        out_specs=pl.BlockSpec((None, tm, W), lambda j, i: (j, i, 0)),
        compiler_params=_params("parallel", "parallel"),
    )(h, w_all, cos, sin)


def _head_spec(S, j):
    return pl.BlockSpec((None, S, HEAD_DIM), lambda h, *_: (j, 0, h))


def _ret_chunk(q, k, vb, r_prev, dec, xi, ze):
    qb, kb = q.astype(BF16), k.astype(BF16)
    sb = (_dot_nt(qb, kb) * dec).astype(BF16)
    qx = (q * xi).astype(BF16)
    kz = (k * ze).astype(BF16)
    out = _dot(sb, vb) + _dot(qx, r_prev.astype(BF16))
    return out, _dot_tn(kz, vb), (qb, kb, sb, qx, kz)


def _table_specs():
    return [pl.BlockSpec((None, CHUNK, HEAD_DIM), lambda h, *_: (h, 0, 0))] * 4


def _ret_fwd(proj, tabs, gn_gain, gn_bias):
    _, S, W = proj.shape
    H, nc = W // HEAD_DIM, S // CHUNK

    def body(q_ref, k_ref, v_ref, g_ref, dec_ref, xi_ref, ze_ref, gam_ref, gain_ref, bias_ref, o_ref, r_ref):
        r_ref[...] = jnp.zeros_like(r_ref)
        dec, xi, ze, gam = dec_ref[...], xi_ref[...], ze_ref[...], gam_ref[...]
        gain, bias = gain_ref[...], bias_ref[...]

        def step(c, carry):
            rows = pl.ds(pl.multiple_of(c * CHUNK, CHUNK), CHUNK)
            out, kv, _ = _ret_chunk(q_ref[rows, :], k_ref[rows, :], v_ref[rows, :].astype(BF16), r_ref[...],
                                    dec, xi, ze)
            r_ref[...] = gam * r_ref[...] + kv
            mu = jnp.mean(out, axis=-1, keepdims=True)
            d = out - mu
            yn = d * lax.rsqrt(jnp.mean(d * d, axis=-1, keepdims=True) + EPS)
            g = g_ref[rows, :]
            o_ref[rows, :] = (g * _sigmoid(g) * (yn * gain + bias)).astype(BF16)
            return carry

        lax.fori_loop(0, nc, step, 0)

    vec = pl.BlockSpec((1, HEAD_DIM), lambda h: (0, h))
    return pl.pallas_call(
        body, name="ret_fwd", out_shape=jax.ShapeDtypeStruct((S, W), BF16), grid=(H,),
        in_specs=[_head_spec(S, 0), _head_spec(S, 1), _head_spec(S, 2), _head_spec(S, 3)] + _table_specs() + [vec, vec],
        out_specs=pl.BlockSpec((S, HEAD_DIM), lambda h: (0, h)),
        scratch_shapes=[pltpu.VMEM((HEAD_DIM, HEAD_DIM), F32)],
        compiler_params=_params("parallel"),
    )(proj, proj, proj, proj, *tabs, gn_gain, gn_bias)


def _sb_scores(qb, kk, masked, causal, upper):
    z = _dot_nt(qb, kk) * (HEAD_DIM ** -0.5)
    e = jnp.exp(-jnp.abs(z))
    l1p = jnp.log1p(e)
    log_beta = jnp.minimum(z, 0.0) - l1p
    lk = jnp.minimum(-z, 0.0) - l1p
    if masked:
        lk = jnp.where(causal, lk, 0.0)
    hi = lk.astype(BF16)
    lo = (lk - hi.astype(F32)).astype(BF16)
    cs = _dot(hi, upper) + _dot(lo, upper)
    return z, e, log_beta, lk, cs


def _tri(B, kind):
    r = lax.broadcasted_iota(jnp.int32, (B, B), 0)
    c = lax.broadcasted_iota(jnp.int32, (B, B), 1)
    return {"gt": r > c, "lt": r < c}[kind]


def _ones_where(mask):
    return jnp.where(mask, 1.0, 0.0).astype(BF16)


def _sb_fwd(proj, gain):
    _, S, W = proj.shape
    H = W // HEAD_DIM
    B = min(SB_BLOCK, S)
    nq = S // B
    assert nq <= HEAD_DIM

    def body(q_ref, k_ref, v_ref, g_ref, gain_ref, mix_ref, raw_ref, car_ref, kb_ref, vb_ref):
        qi = pl.program_id(1)

        @pl.when(qi == 0)
        def _():
            kb_ref[...] = k_ref[...].astype(BF16)
            vb_ref[...] = v_ref[...].astype(BF16)

        qb = q_ref[...].astype(BF16)
        causal = _tri(B, "gt")
        upper = _ones_where(causal)
        lane = lax.broadcasted_iota(jnp.int32, (B, HEAD_DIM), 1)

        def block(kb, carry, acc, saved, masked):
            rows = pl.ds(pl.multiple_of(kb * B, B), B)
            _, _, log_beta, lk, cs = _sb_scores(qb, kb_ref[rows, :], masked, causal, upper)
            a = jnp.exp(log_beta + cs + carry)
            if masked:
                a = jnp.where(causal, a, 0.0)
            acc = acc + _dot(a.astype(BF16), vb_ref[rows, :])
            return carry + jnp.sum(lk, axis=1, keepdims=True), acc, jnp.where(lane == kb, carry, saved)

        zeros = jnp.zeros((B, HEAD_DIM), F32)
        state = block(qi, jnp.zeros((B, 1), F32), zeros, zeros, True)
        _, acc, saved = lax.fori_loop(0, qi, lambda i, st: block(qi - 1 - i, *st, False), state)
        raw_ref[...] = acc
        car_ref[...] = saved
        yn = acc * lax.rsqrt(jnp.mean(acc * acc, axis=-1, keepdims=True) + EPS)
        g = g_ref[...]
        mix_ref[...] = (g * _sigmoid(g) * (yn * gain_ref[...])).astype(BF16)

    tile = lambda j: pl.BlockSpec((None, B, HEAD_DIM), lambda h, i: (j, i, h))
    out_tile = pl.BlockSpec((B, HEAD_DIM), lambda h, i: (i, h))
    return pl.pallas_call(
        body, name="sb_fwd",
        out_shape=(jax.ShapeDtypeStruct((S, W), BF16), jax.ShapeDtypeStruct((S, W), F32),
                   jax.ShapeDtypeStruct((S, W), F32)), grid=(H, nq),
        in_specs=[tile(4), _head_spec(S, 5), _head_spec(S, 6), tile(7), pl.BlockSpec((1, HEAD_DIM), lambda h, i: (0, h))],
        out_specs=(out_tile, out_tile, out_tile),
        scratch_shapes=[pltpu.VMEM((S, HEAD_DIM), BF16), pltpu.VMEM((S, HEAD_DIM), BF16)],
        compiler_params=_params("arbitrary", "arbitrary"),
    )(proj, proj, proj, proj, gain)


def _out_proj_loss(mix_r, mix_s, w_out, x, tgt, gf):
    S, W = mix_r.shape
    D = x.shape[1]
    tm = min(256, S)

    def body(mr_ref, ms_ref, wo_ref, x_ref, t_ref, gf_ref, dx2_ref, dx2b_ref, dmix_ref, loss_ref, gfn_ref):
        @pl.when(pl.program_id(0) == 0)
        def _():
            loss_ref[...] = jnp.zeros_like(loss_ref)
            gfn_ref[...] = jnp.zeros_like(gfn_ref)

        gfv = gf_ref[...]
        x2 = x_ref[...] + (_dot(mr_ref[...], wo_ref[:W, :]) + _dot(ms_ref[...], wo_ref[W:, :]))
        r2 = lax.rsqrt(jnp.mean(x2 * x2, axis=-1, keepdims=True) + EPS)
        n = x2 * r2
        err = n * gfv - t_ref[...]
        loss_ref[...] += 0.5 * jnp.sum(jnp.mean(err * err, axis=-1, keepdims=True))
        dy = err * (1.0 / D)
        gfn_ref[...] += jnp.sum(dy * n, axis=0, keepdims=True)
        dn = dy * gfv
        dx2 = r2 * (dn - n * jnp.mean(dn * n, axis=-1, keepdims=True))
        dx2_ref[...] = dx2
        b = dx2.astype(BF16)
        dx2b_ref[...] = b
        dmix_ref[:, :W] = _dot_nt(b, wo_ref[:W, :])
        dmix_ref[:, W:] = _dot_nt(b, wo_ref[W:, :])

    row = lambda width: pl.BlockSpec((tm, width), lambda i: (i, 0))
    return pl.pallas_call(
        body, name="out_proj_loss",
        out_shape=(jax.ShapeDtypeStruct((S, D), F32), jax.ShapeDtypeStruct((S, D), BF16),
                   jax.ShapeDtypeStruct((S, 2 * W), F32), jax.ShapeDtypeStruct((SUBLANES, LANES), F32),
                   jax.ShapeDtypeStruct((1, D), F32)),
        grid=(S // tm,),
        in_specs=[row(W), row(W), pl.BlockSpec((2 * W, D), lambda i: (0, 0)), row(D), row(D),
                  pl.BlockSpec((1, D), lambda i: (0, 0))],
        out_specs=(row(D), row(D), row(2 * W), pl.BlockSpec((SUBLANES, LANES), lambda i: (0, 0)),
                   pl.BlockSpec((1, D), lambda i: (0, 0))),
        compiler_params=_params("arbitrary"),
    )(mix_r, mix_s, w_out, x, tgt, gf)


def _silu_bwd(g, dm, normed):
    sig = _sigmoid(g)
    return dm * (g * sig), dm * normed * (sig * (1.0 + g * (1.0 - sig)))


def _ret_bwd(proj, dmix, tabs, gn_gain, gn_bias, cos, sin):
    _, S, W = proj.shape
    H, nc = W // HEAD_DIM, S // CHUNK

    def body(q_ref, k_ref, v_ref, g_ref, dm_ref, dec_ref, xi_ref, ze_ref, gam_ref, gain_ref, bias_ref, cos_ref,
             sin_ref, dp_ref, dgain_ref, dbias_ref, rs_ref, r_ref, dr_ref):
        dec, xi, ze, gam = dec_ref[...], xi_ref[...], ze_ref[...], gam_ref[...]
        gain, bias = gain_ref[...], bias_ref[...]

        r_ref[...] = jnp.zeros_like(r_ref)

        def fwd_step(c, carry):
            rows = pl.ds(pl.multiple_of(c * CHUNK, CHUNK), CHUNK)
            rs_ref[c] = r_ref[...]
            kz = (k_ref[rows, :] * ze).astype(BF16)
            r_ref[...] = gam * r_ref[...] + _dot_tn(kz, v_ref[rows, :].astype(BF16))
            return carry

        lax.fori_loop(0, nc, fwd_step, 0)

        dr_ref[...] = jnp.zeros_like(dr_ref)

        def bwd_step(i, carry):
            dgain, dbias = carry
            c = nc - 1 - i
            rows = pl.ds(pl.multiple_of(c * CHUNK, CHUNK), CHUNK)
            q, k, g = q_ref[rows, :], k_ref[rows, :], g_ref[rows, :]
            vb = v_ref[rows, :].astype(BF16)
            rb = rs_ref[c].astype(BF16)
            out, _, (qb, kb, sb, qx, kz) = _ret_chunk(q, k, vb, rs_ref[c], dec, xi, ze)
            mu = jnp.mean(out, axis=-1, keepdims=True)
            d = out - mu
            rstd = lax.rsqrt(jnp.mean(d * d, axis=-1, keepdims=True) + EPS)
            yn = d * rstd
            dgn, dg = _silu_bwd(g, dm_ref[rows, :], yn * gain + bias)
            dgain = dgain + jnp.sum(dgn * yn, axis=0, keepdims=True)
            dbias = dbias + jnp.sum(dgn, axis=0, keepdims=True)
            dyn = dgn * gain
            do = rstd * (dyn - jnp.mean(dyn, axis=-1, keepdims=True)
                         - yn * jnp.mean(dyn * yn, axis=-1, keepdims=True))
            dob = do.astype(BF16)
            drb = dr_ref[...].astype(BF16)
            dv = _dot_tn(sb, dob) + _dot(kz, drb)
            dsb = (_dot_nt(dob, vb) * dec).astype(BF16)
            dq = _dot(dsb, kb) + _dot_nt(dob, rb) * xi
            dk = _dot_tn(dsb, qb) + _dot_nt(vb, drb) * ze
            dr_ref[...] = gam * dr_ref[...] + _dot_tn(qx, dob)
            cs, sn = cos_ref[rows, :], -sin_ref[rows, :]
            dp_ref[0, rows, :] = _rot(dq, cs, sn).astype(BF16)
            dp_ref[1, rows, :] = (_rot(dk, cs, sn) * (HEAD_DIM ** -0.5)).astype(BF16)
            dp_ref[2, rows, :] = dv.astype(BF16)
            dp_ref[3, rows, :] = dg.astype(BF16)
            return dgain, dbias

        zero = jnp.zeros((1, HEAD_DIM), F32)
        dgain, dbias = lax.fori_loop(0, nc, bwd_step, (zero, zero))
        dgain_ref[...] = dgain
        dbias_ref[...] = dbias

    vec = pl.BlockSpec((1, HEAD_DIM), lambda h: (0, h))
    full = pl.BlockSpec((S, HEAD_DIM), lambda h: (0, 0))
    return pl.pallas_call(
        body, name="ret_bwd",
        out_shape=(jax.ShapeDtypeStruct((4, S, W), BF16), jax.ShapeDtypeStruct((1, W), F32),
                   jax.ShapeDtypeStruct((1, W), F32)),
        grid=(H,),
        in_specs=[_head_spec(S, 0), _head_spec(S, 1), _head_spec(S, 2), _head_spec(S, 3),
                  pl.BlockSpec((S, HEAD_DIM), lambda h: (0, h))] + _table_specs() + [vec, vec, full, full],
        out_specs=(pl.BlockSpec((4, S, HEAD_DIM), lambda h: (0, 0, h)), vec, vec),
        scratch_shapes=[pltpu.VMEM((nc, HEAD_DIM, HEAD_DIM), F32), pltpu.VMEM((HEAD_DIM, HEAD_DIM), F32),
                        pltpu.VMEM((HEAD_DIM, HEAD_DIM), F32)],
        compiler_params=_params("parallel"),
    )(proj, proj, proj, proj, dmix, *tabs, gn_gain, gn_bias, cos, sin)


def _sb_bwd(proj, raw, carries, dmix, gain):
    _, S, W = proj.shape
    H = W // HEAD_DIM
    B = min(SB_BLOCK, S)
    nq = S // B

    def body(q_ref, k_ref, v_ref, g_ref, raw_ref, car_ref, dm_ref, gain_ref, dp_ref, dgain_ref,
             kb_ref, vb_ref, dk_ref, dv_ref):
        qi = pl.program_id(1)

        @pl.when(qi == 0)
        def _():
            kb_ref[...] = k_ref[...].astype(BF16)
            vb_ref[...] = v_ref[...].astype(BF16)
            dk_ref[...] = jnp.zeros_like(dk_ref)
            dv_ref[...] = jnp.zeros_like(dv_ref)
            dgain_ref[...] = jnp.zeros_like(dgain_ref)

        q_rows = pl.ds(pl.multiple_of(qi * B, B), B)
        o = raw_ref[...]
        rstd = lax.rsqrt(jnp.mean(o * o, axis=-1, keepdims=True) + EPS)
        yn = o * rstd
        gain_v = gain_ref[...]
        dnrm, dg = _silu_bwd(g_ref[...], dm_ref[...], yn * gain_v)
        dp_ref[3, q_rows, :] = dg.astype(BF16)
        dgain_ref[...] += jnp.sum(dnrm * yn, axis=0, keepdims=True)
        dyn = dnrm * gain_v
        do = rstd * (dyn - yn * jnp.mean(dyn * yn, axis=-1, keepdims=True))
        dob = do.astype(BF16)
        qb = q_ref[...].astype(BF16)
        causal = _tri(B, "gt")
        upper = _ones_where(causal)
        before = _ones_where(_tri(B, "lt"))
        lane = lax.broadcasted_iota(jnp.int32, (B, HEAD_DIM), 1)
        saved = car_ref[...]

        def block(kb, carry_g, dq, masked):
            rows = pl.ds(pl.multiple_of(kb * B, B), B)
            kk, vv = kb_ref[rows, :], vb_ref[rows, :]
            z, e, log_beta, _, cs = _sb_scores(qb, kk, masked, causal, upper)
            carry_lk = jnp.sum(jnp.where(lane == kb, saved, 0.0), axis=1, keepdims=True)
            a = jnp.exp(log_beta + cs + carry_lk)
            if masked:
                a = jnp.where(causal, a, 0.0)
            gmat = _dot_nt(dob, vv) * a
            dv_ref[rows, :] += _dot_tn(a.astype(BF16), dob)
            hi = gmat.astype(BF16)
            lo = (gmat - hi.astype(F32)).astype(BF16)
            dlk = carry_g + (_dot(hi, before) + _dot(lo, before))
            r = 1.0 / (1.0 + e)
            er = e * r
            pos = z >= 0.0
            dz = (gmat * jnp.where(pos, er, r) - dlk * jnp.where(pos, r, er)) * (HEAD_DIM ** -0.5)
            if masked:
                dz = jnp.where(causal, dz, 0.0)
            dzb = dz.astype(BF16)
            dk_ref[rows, :] += _dot_tn(dzb, qb)
            return carry_g + jnp.sum(gmat, axis=1, keepdims=True), dq + _dot(dzb, kk)

        state = (jnp.zeros((B, 1), F32), jnp.zeros((B, HEAD_DIM), F32))
        state = lax.fori_loop(0, qi, lambda i, st: block(i, st[0], st[1], False), state)
        state = block(qi, state[0], state[1], True)
        dp_ref[0, q_rows, :] = state[1].astype(BF16)

        @pl.when(qi == nq - 1)
        def _():
            dp_ref[1] = dk_ref[...].astype(BF16)
            dp_ref[2] = dv_ref[...].astype(BF16)

    tile = lambda j: pl.BlockSpec((None, B, HEAD_DIM), lambda h, i: (j, i, h))
    vec = pl.BlockSpec((1, HEAD_DIM), lambda h, i: (0, h))
    return pl.pallas_call(
        body, name="sb_bwd",
        out_shape=(jax.ShapeDtypeStruct((4, S, W), BF16), jax.ShapeDtypeStruct((1, W), F32)),
        grid=(H, nq),
        in_specs=[tile(4), _head_spec(S, 5), _head_spec(S, 6), tile(7),
                  pl.BlockSpec((B, HEAD_DIM), lambda h, i: (i, h)),
                  pl.BlockSpec((B, HEAD_DIM), lambda h, i: (i, h)),
                  pl.BlockSpec((B, HEAD_DIM), lambda h, i: (i, H + h)), vec],
        out_specs=(pl.BlockSpec((4, S, HEAD_DIM), lambda h, i: (0, 0, h)), vec),
        scratch_shapes=[pltpu.VMEM((S, HEAD_DIM), BF16), pltpu.VMEM((S, HEAD_DIM), BF16),
                        pltpu.VMEM((S, HEAD_DIM), F32), pltpu.VMEM((S, HEAD_DIM), F32)],
        compiler_params=_params("arbitrary", "arbitrary"),
    )(proj, proj, proj, proj, raw, carries, dmix, gain)


def _grad_w_in(h, dpr, dps):
    S, D = h.shape
    _, _, W = dpr.shape
    tmm = min(512, D)
    tk = min(2048, S)

    def body(h_ref, r_ref, s_ref, o_ref):
        j, kk = pl.program_id(0), pl.program_id(2)

        def acc(b_ref):
            part = _dot_tn(h_ref[...], b_ref[...])

            @pl.when(kk == 0)
            def _():
                o_ref[...] = part

            @pl.when(kk > 0)
            def _():
                o_ref[...] += part

        pl.when(j < 4)(lambda: acc(r_ref))
        pl.when(j >= 4)(lambda: acc(s_ref))

    return pl.pallas_call(
        body, name="grad_w_in", out_shape=jax.ShapeDtypeStruct((8, D, W), F32), grid=(8, D // tmm, S // tk),
        in_specs=[pl.BlockSpec((tk, tmm), lambda j, m, k: (k, m)),
                  pl.BlockSpec((None, tk, W), lambda j, m, k: (jnp.minimum(j, 3), k, 0)),
                  pl.BlockSpec((None, tk, W), lambda j, m, k: (jnp.maximum(j - 4, 0), k, 0))],
        out_specs=pl.BlockSpec((None, tmm, W), lambda j, m, k: (j, m, 0)),
        compiler_params=_params("parallel", "parallel", "arbitrary"),
    )(h, dpr, dps)


def _grad_w_out(mix_r, mix_s, dx2b):
    S, W = mix_r.shape
    D = dx2b.shape[1]
    tmm = min(512, W)
    tk = min(1024, S)

    def body(r_ref, s_ref, b_ref, o_ref):
        j, kk = pl.program_id(0), pl.program_id(2)

        def acc(a_ref):
            part = _dot_tn(a_ref[...], b_ref[...])

            @pl.when(kk == 0)
            def _():
                o_ref[...] = part

            @pl.when(kk > 0)
            def _():
                o_ref[...] += part

        pl.when(j == 0)(lambda: acc(r_ref))
        pl.when(j == 1)(lambda: acc(s_ref))

    return pl.pallas_call(
        body, name="grad_w_out", out_shape=jax.ShapeDtypeStruct((2, W, D), F32), grid=(2, W // tmm, S // tk),
        in_specs=[pl.BlockSpec((tk, tmm), lambda j, m, k: (k, m)),
                  pl.BlockSpec((tk, tmm), lambda j, m, k: (k, m)),
                  pl.BlockSpec((tk, D), lambda j, m, k: (k, 0))],
        out_specs=pl.BlockSpec((None, tmm, D), lambda j, m, k: (j, m, 0)),
        compiler_params=_params("parallel", "parallel", "arbitrary"),
    )(mix_r, mix_s, dx2b)


def _dh_norm_bwd(dpr, dps, w_all, x, dx2, gain):
    _, S, W = dpr.shape
    D = x.shape[1]
    tm = min(512, S)

    def body(r_ref, s_ref, w_ref, x_ref, dx2_ref, g_ref, gx_ref, dgain_ref, acc_ref):
        i, j = pl.program_id(0), pl.program_id(1)

        @pl.when((i == 0) & (j == 0))
        def _():
            dgain_ref[...] = jnp.zeros_like(dgain_ref)

        def acc(b_ref):
            part = _dot_nt(b_ref[...], w_ref[...])

            @pl.when(j == 0)
            def _():
                acc_ref[...] = part

            @pl.when(j > 0)
            def _():
                acc_ref[...] += part

        pl.when(j < 4)(lambda: acc(r_ref))
        pl.when(j >= 4)(lambda: acc(s_ref))

        @pl.when(j == 7)
        def _():
            xv, dh, gv = x_ref[...], acc_ref[...], g_ref[...]
            r1 = lax.rsqrt(jnp.mean(xv * xv, axis=-1, keepdims=True) + EPS)
            n = xv * r1
            dgain_ref[...] += jnp.sum(dh * n, axis=0, keepdims=True)
            dn = dh * gv
            gx_ref[...] = dx2_ref[...] + r1 * (dn - n * jnp.mean(dn * n, axis=-1, keepdims=True))

    row = pl.BlockSpec((tm, D), lambda i, j: (i, 0))
    one = pl.BlockSpec((1, D), lambda i, j: (0, 0))
    return pl.pallas_call(
        body, name="dh_norm_bwd",
        out_shape=(jax.ShapeDtypeStruct((S, D), F32), jax.ShapeDtypeStruct((1, D), F32)), grid=(S // tm, 8),
        in_specs=[pl.BlockSpec((None, tm, W), lambda i, j: (jnp.minimum(j, 3), i, 0)),
                  pl.BlockSpec((None, tm, W), lambda i, j: (jnp.maximum(j - 4, 0), i, 0)),
                  pl.BlockSpec((None, D, W), lambda i, j: (j, 0, 0)), row, row, one],
        out_specs=(row, one),
        scratch_shapes=[pltpu.VMEM((tm, D), F32)],
        compiler_params=_params("arbitrary", "arbitrary"),
    )(dpr, dps, w_all, x, dx2, gain)


def _rs_local_sum(gw, rin, pos):
    _, R, C = gw.shape
    tr = min(256, R)

    def body(pos_ref, a_ref, b_ref, o_ref):
        o_ref[...] = (a_ref[...] + b_ref[...]).astype(BF16)

    return pl.pallas_call(
        body, name="rs_local_sum", out_shape=jax.ShapeDtypeStruct((4, R, C), BF16),
        grid_spec=pltpu.PrefetchScalarGridSpec(
            num_scalar_prefetch=1, grid=(4, R // tr),
            in_specs=[pl.BlockSpec((None, tr, C), lambda q, i, pos: (2 * q + pos[0], i, 0)),
                      pl.BlockSpec((None, tr, C), lambda q, i, pos: (q, i, 0))],
            out_specs=pl.BlockSpec((None, tr, C), lambda q, i, pos: (q, i, 0))),
        compiler_params=_params("parallel", "parallel"),
    )(pos, gw, rin)


def _adamw(w, g, m, v):
    m2 = ADAM_B1 * m + (1.0 - ADAM_B1) * g
    v2 = ADAM_B2 * v + (1.0 - ADAM_B2) * (g * g)
    m_hat = m2 / (1.0 - ADAM_B1 ** ADAM_STEP)
    v_hat = v2 / (1.0 - ADAM_B2 ** ADAM_STEP)
    delta = -ADAM_LR * (m_hat / (jnp.sqrt(v_hat) + ADAM_EPS) + ADAM_WD * w)
    return delta, m2, v2


def _adamw_shard(gw, rin, rb, w, m, v, pos):
    _, R, C = gw.shape
    tr = min(256, R)

    def body(pos_ref, a_ref, b_ref, rb_ref, w_ref, m_ref, v_ref, g_ref, d_ref, m2_ref, v2_ref):
        g = a_ref[...] + b_ref[...]
        for k in range(3):
            g = g + rb_ref[k].astype(F32)
        g_ref[...] = g
        d_ref[...], m2_ref[...], v2_ref[...] = _adamw(w_ref[...], g, m_ref[...], v_ref[...])

    plain = pl.BlockSpec((tr, C), lambda i, pos: (i, 0))
    shape = jax.ShapeDtypeStruct((R, C), F32)
    return pl.pallas_call(
        body, name="adamw_shard", out_shape=(shape,) * 4,
        grid_spec=pltpu.PrefetchScalarGridSpec(
            num_scalar_prefetch=1, grid=(R // tr,),
            in_specs=[pl.BlockSpec((None, tr, C), lambda i, pos: (2 * pos[1] + pos[0], i, 0)),
                      pl.BlockSpec((None, tr, C), lambda i, pos: (pos[1], i, 0)),
                      pl.BlockSpec((3, tr, C), lambda i, pos: (0, i, 0)), plain, plain, plain],
            out_specs=(plain,) * 4),
        compiler_params=_params("parallel"),
    )(pos, gw, rin, rb, w, m, v)


def _adamw_small(parts, w, m, v):
    _, rows, n = parts.shape

    def body(p_ref, w_ref, m_ref, v_ref, g_ref, d_ref, m2_ref, v2_ref):
        g = p_ref[0]
        for d in range(1, N_DEV):
            g = g + p_ref[d]
        g_ref[...] = g
        d_ref[...], m2_ref[...], v2_ref[...] = _adamw(w_ref[...], g, m_ref[...], v_ref[...])

    shape = jax.ShapeDtypeStruct((rows, n), F32)
    return pl.pallas_call(body, name="adamw_small", out_shape=(shape,) * 4)(parts, w, m, v)


def _rope_tables(S):
    half = HEAD_DIM // 2
    inv = ROPE_THETA ** (-jnp.arange(half, dtype=F32) / half)
    ang = jnp.arange(S, dtype=F32)[:, None] * inv[None, :]
    cos, sin = jnp.cos(ang), jnp.sin(ang)
    return jnp.concatenate([cos, cos], axis=1), jnp.concatenate([-sin, sin], axis=1)


def _retention_tables(H):
    lg = jnp.log1p(-jnp.exp2(-5.0 - jnp.arange(H, dtype=F32)))
    n = jnp.arange(CHUNK, dtype=F32)
    rel = n[:, None] - n[None, :]
    decay = jnp.where(rel >= 0, jnp.exp(lg[:, None, None] * jnp.maximum(rel, 0.0)), 0.0)
    shape = (H, CHUNK, HEAD_DIM)
    xi = jnp.broadcast_to(jnp.exp(lg[:, None] * (n + 1.0))[:, :, None], shape)
    zeta = jnp.broadcast_to(jnp.exp(lg[:, None] * (CHUNK - 1.0 - n))[:, :, None], shape)
    gamma_c = jnp.broadcast_to(jnp.exp(lg * CHUNK)[:, None, None], shape)
    return decay, xi, zeta, gamma_c


def _pack_small(parts):
    flat = []
    for p in parts:
        p = p.reshape(-1)
        flat.append(jnp.pad(p, (0, -p.shape[0] % LANES)))
    flat = jnp.concatenate(flat)
    return jnp.pad(flat, (0, SMALL_N - flat.shape[0])).reshape(SUBLANES, SMALL_N // SUBLANES)


def _unpack_small(packed, shapes):
    flat = packed.reshape(-1)
    out, at = [], 0
    for shp in shapes:
        size = 1
        for s in shp:
            size *= s
        out.append(flat[at:at + size].reshape(shp))
        at += size + (-size % LANES)
    return out


def kernel(x, norm_gain, w_in, ret_gn_gain, ret_gn_bias, sb_norm_gain, w_out, final_norm_gain, loss_target, m_norm_gain, m_w_in, m_ret_gn_gain, m_ret_gn_bias, m_sb_norm_gain, m_w_out, m_final_norm_gain, v_norm_gain, v_w_in, v_ret_gn_gain, v_ret_gn_bias, v_sb_norm_gain, v_w_out, v_final_norm_gain):
    S, D = x.shape[1], x.shape[2]
    W = w_in.shape[2]
    wo_rows = w_out.shape[1]
    H = W // HEAD_DIM
    xs, tgt = x[0], loss_target[0]
    mx, my, mc = _mesh_pos()
    pos = jnp.stack([mc, 2 * mx + my]).astype(jnp.int32)

    w_all, wo_all = _all_gather_weights(w_in[0].astype(BF16), w_out[0].astype(BF16))
    wo_full = wo_all.reshape(N_DEV * wo_rows, D)

    cos, sin = _rope_tables(S)
    tabs = _retention_tables(H)

    h = _rmsnorm_fwd(xs, norm_gain)
    proj = _in_proj(h, w_all, cos, sin)
    mix_r = _ret_fwd(proj, tabs, ret_gn_gain, ret_gn_bias)
    mix_s, raw_s, carries = _sb_fwd(proj, sb_norm_gain)
    dx2, dx2b, dmix, loss_p, d_gf = _out_proj_loss(mix_r, mix_s, wo_full, xs, tgt, final_norm_gain[None])

    dpr, d_rgain, d_rbias = _ret_bwd(proj, dmix, tabs, ret_gn_gain, ret_gn_bias, cos, sin)
    dps, d_sgain = _sb_bwd(proj, raw_s, carries, dmix, sb_norm_gain)
    gw = _grad_w_in(h, dpr, dps)
    gwo = _grad_w_out(mix_r, mix_s, dx2b).reshape(N_DEV, wo_rows, D)
    grad_x, d_gain = _dh_norm_bwd(dpr, dps, w_all, xs, dx2, norm_gain)

    rin, rino = _rs_sibling(gw, gwo)
    rb, rbo = _rs_chips(_rs_local_sum(gw, rin, pos), _rs_local_sum(gwo, rino, pos))
    g_in, d_in, m_in, v_in = _adamw_shard(gw, rin, rb, w_in[0], m_w_in[0], v_w_in[0], pos)
    g_out, d_out, m_out, v_out = _adamw_shard(gwo, rino, rbo, w_out[0], m_w_out[0], v_w_out[0], pos)

    small_w = [norm_gain, ret_gn_gain, ret_gn_bias, sb_norm_gain, final_norm_gain]
    small_m = [m_norm_gain, m_ret_gn_gain, m_ret_gn_bias, m_sb_norm_gain, m_final_norm_gain]
    small_v = [v_norm_gain, v_ret_gn_gain, v_ret_gn_bias, v_sb_norm_gain, v_final_norm_gain]
    shapes = [()] + [w.shape for w in small_w]
    zero = jnp.zeros((), F32)
    parts = _small_all_gather(_pack_small([loss_p[0, 0], d_gain, d_rgain, d_rbias, d_sgain, d_gf]))
    packed = _adamw_small(parts, _pack_small([zero] + small_w), _pack_small([zero] + small_m),
                          _pack_small([zero] + small_v))
    g_s, d_s, m_s, v_s = (_unpack_small(p, shapes) for p in packed)

    grads = [g_s[1], g_in[None], g_s[2], g_s[3], g_s[4], g_out[None], g_s[5]]
    deltas = [d_s[1], d_in[None], d_s[2], d_s[3], d_s[4], d_out[None], d_s[5]]
    new_m = [m_s[1], m_in[None], m_s[2], m_s[3], m_s[4], m_out[None], m_s[5]]
    new_v = [v_s[1], v_in[None], v_s[2], v_s[3], v_s[4], v_out[None], v_s[5]]
    return (g_s[0], grad_x[None], *grads, *deltas, *new_m, *new_v)
```

```python
import functools

import jax
import jax.numpy as jnp
from jax import lax
from jax.experimental import pallas as pl
from jax.experimental.pallas import tpu as pltpu

F32 = jnp.float32
BF16 = jnp.bfloat16

HEAD_DIM = 128
CHUNK = 128
ROPE_THETA = 10000.0
EPS = 1e-6
ADAM_LR = 0.001
ADAM_B1 = 0.9
ADAM_B2 = 0.999
ADAM_EPS = 1e-08
ADAM_WD = 0.01
ADAM_STEP = 10

N_DEV = 8
LANES = 128
SUBLANES = 8
VMEM_LIMIT = 56 * 1024 * 1024
SB_BLOCK = 256
SMALL_N = 8192
EXP_IS_ZERO_BELOW = -104.0
NOT_VISITED = -1e30
MESH = pl.DeviceIdType.MESH

NT = (((1,), (1,)), ((), ()))
TN = (((0,), (0,)), ((), ()))


def _params(*sem):
    return pltpu.CompilerParams(dimension_semantics=sem if sem else None, vmem_limit_bytes=VMEM_LIMIT)


def _dot(a, b):
    return jnp.dot(a, b, preferred_element_type=F32)


def _dot_nt(a, b):
    return lax.dot_general(a, b, NT, preferred_element_type=F32)


def _dot_tn(a, b):
    return lax.dot_general(a, b, TN, preferred_element_type=F32)


def _sigmoid(g):
    return 1.0 / (1.0 + jnp.exp(-g))


def _rot(a, cos, sin_signed):
    return a * cos + pltpu.roll(a, HEAD_DIM // 2, 1) * sin_signed


def _mesh_pos():
    return lax.axis_index("x"), lax.axis_index("y"), lax.axis_index("c")


def _all_gather_weights(win_b, wout_b):
    n_arr = 2

    def body(a_ref, b_ref, oa_ref, ob_ref, send_sems, recv_sems, local_sems):
        x, y, c = _mesh_pos()
        me, sibling = (x, y, c), (x, y, 1 - c)
        chips = [(1 - x, y), (x, 1 - y), (1 - x, 1 - y)]
        srcs, outs = (a_ref, b_ref), (oa_ref, ob_ref)

        def slot(arr, dev):
            px, py, pc = dev
            return outs[arr].at[4 * px + 2 * py + pc]

        def copy(arr, k, block, to, src=None):
            dst = slot(arr, block)
            return pltpu.make_async_remote_copy(
                src_ref=dst if src is None else src, dst_ref=dst,
                send_sem=send_sems.at[arr, k], recv_sem=recv_sems.at[arr, k],
                device_id=to, device_id_type=MESH)

        mine = [pltpu.make_async_copy(srcs[arr], slot(arr, me), local_sems.at[arr]) for arr in range(n_arr)]
        for cp in mine:
            cp.start()
        first = []
        for arr in range(n_arr):
            first.append(copy(arr, 0, me, sibling, src=srcs[arr]))
            first += [copy(arr, 1 + j, me, (*chip, c), src=srcs[arr]) for j, chip in enumerate(chips)]
        for cp in first:
            cp.start()
        passed = []
        for j, chip in enumerate(chips):
            for arr in range(n_arr):
                copy(arr, 1 + j, (*chip, c), me).wait_recv()
                fwd = copy(arr, 4 + j, (*chip, c), sibling)
                fwd.start()
                passed.append(fwd)
        for arr in range(n_arr):
            copy(arr, 0, sibling, me).wait_recv()
            for j, chip in enumerate(chips):
                copy(arr, 4 + j, (*chip, 1 - c), me).wait_recv()
        for cp in first + passed:
            cp.wait_send()
        for cp in mine:
            cp.wait()

    hbm = pl.BlockSpec(memory_space=pltpu.HBM)
    return pl.pallas_call(
        body, name="all_gather_weights",
        out_shape=(jax.ShapeDtypeStruct((N_DEV,) + win_b.shape, win_b.dtype),
                   jax.ShapeDtypeStruct((N_DEV,) + wout_b.shape, wout_b.dtype)),
        in_specs=[hbm, hbm], out_specs=(hbm, hbm),
        scratch_shapes=[pltpu.SemaphoreType.DMA((n_arr, 7)), pltpu.SemaphoreType.DMA((n_arr, 7)),
                        pltpu.SemaphoreType.DMA((n_arr,))],
    )(win_b, wout_b)


def _rs_sibling(gw, gwo):
    n_arr = 2

    def body(a_ref, b_ref, oa_ref, ob_ref, send_sems, recv_sems):
        x, y, c = _mesh_pos()
        sibling = (x, y, 1 - c)
        srcs, outs = (a_ref, b_ref), (oa_ref, ob_ref)
        copies = []
        for arr in range(n_arr):
            for q in range(4):
                copies.append(pltpu.make_async_remote_copy(
                    src_ref=srcs[arr].at[2 * q + (1 - c)], dst_ref=outs[arr].at[q],
                    send_sem=send_sems.at[arr, q], recv_sem=recv_sems.at[arr, q],
                    device_id=sibling, device_id_type=MESH))
        for cp in copies:
            cp.start()
        for cp in copies:
            cp.wait_recv()
        for cp in copies:
            cp.wait_send()

    hbm = pl.BlockSpec(memory_space=pltpu.HBM)
    return pl.pallas_call(
        body, name="rs_sibling",
        out_shape=(jax.ShapeDtypeStruct((4,) + gw.shape[1:], gw.dtype),
                   jax.ShapeDtypeStruct((4,) + gwo.shape[1:], gwo.dtype)),
        in_specs=[hbm, hbm], out_specs=(hbm, hbm),
        scratch_shapes=[pltpu.SemaphoreType.DMA((n_arr, 4)), pltpu.SemaphoreType.DMA((n_arr, 4))],
    )(gw, gwo)


def _rs_chips(sb, sbo):
    n_arr = 2

    def body(a_ref, b_ref, oa_ref, ob_ref, send_sems, recv_sems):
        x, y, c = _mesh_pos()
        srcs, outs = (a_ref, b_ref), (oa_ref, ob_ref)
        copies = []
        for arr in range(n_arr):
            for k in range(1, 4):
                kx, ky = k >> 1, k & 1
                px = 1 - x if kx else x
                py = 1 - y if ky else y
                copies.append(pltpu.make_async_remote_copy(
                    src_ref=srcs[arr].at[2 * px + py], dst_ref=outs[arr].at[k - 1],
                    send_sem=send_sems.at[arr, k - 1], recv_sem=recv_sems.at[arr, k - 1],
                    device_id=(px, py, c), device_id_type=MESH))
        for cp in copies:
            cp.start()
        for cp in copies:
            cp.wait_recv()
        for cp in copies:
            cp.wait_send()

    hbm = pl.BlockSpec(memory_space=pltpu.HBM)
    return pl.pallas_call(
        body, name="rs_chips",
        out_shape=(jax.ShapeDtypeStruct((3,) + sb.shape[1:], sb.dtype),
                   jax.ShapeDtypeStruct((3,) + sbo.shape[1:], sbo.dtype)),
        in_specs=[hbm, hbm], out_specs=(hbm, hbm),
        scratch_shapes=[pltpu.SemaphoreType.DMA((n_arr, 3)), pltpu.SemaphoreType.DMA((n_arr, 3))],
    )(sb, sbo)


def _small_all_gather(small):
    rows, n = small.shape

    def body(s_ref, o_ref, send_sems, recv_sems, local_sem):
        x, y, c = _mesh_pos()
        mine = pltpu.make_async_copy(s_ref, o_ref.at[4 * x + 2 * y + c], local_sem)
        mine.start()
        copies = []
        for k in range(1, N_DEV):
            px = 1 - x if (k >> 2) & 1 else x
            py = 1 - y if (k >> 1) & 1 else y
            pc = 1 - c if k & 1 else c
            copies.append((pltpu.make_async_remote_copy(
                src_ref=s_ref, dst_ref=o_ref.at[4 * x + 2 * y + c],
                send_sem=send_sems.at[k - 1], recv_sem=recv_sems.at[k - 1],
                device_id=(px, py, pc), device_id_type=MESH), (px, py, pc)))
        for cp, _ in copies:
            cp.start()
        for k, (cp, (px, py, pc)) in enumerate(copies):
            pltpu.make_async_remote_copy(
                src_ref=s_ref, dst_ref=o_ref.at[4 * px + 2 * py + pc],
                send_sem=send_sems.at[k], recv_sem=recv_sems.at[k],
                device_id=(px, py, pc), device_id_type=MESH).wait_recv()
        for cp, _ in copies:
            cp.wait_send()
        mine.wait()

    vmem = pl.BlockSpec(memory_space=pltpu.VMEM)
    return pl.pallas_call(
        body, name="small_all_gather",
        out_shape=jax.ShapeDtypeStruct((N_DEV, rows, n), small.dtype),
        in_specs=[vmem], out_specs=vmem,
        scratch_shapes=[pltpu.SemaphoreType.DMA((N_DEV - 1,)), pltpu.SemaphoreType.DMA((N_DEV - 1,)),
                        pltpu.SemaphoreType.DMA],
    )(small)


def _rmsnorm_fwd(x, gain):
    S, D = x.shape
    tm = min(512, S)

    def body(x_ref, g_ref, h_ref):
        xv = x_ref[...]
        r = lax.rsqrt(jnp.mean(xv * xv, axis=-1, keepdims=True) + EPS)
        h_ref[...] = (xv * r * g_ref[...]).astype(BF16)

    return pl.pallas_call(
        body, name="rmsnorm_fwd", out_shape=jax.ShapeDtypeStruct((S, D), BF16), grid=(S // tm,),
        in_specs=[pl.BlockSpec((tm, D), lambda i: (i, 0)), pl.BlockSpec((1, D), lambda i: (0, 0))],
        out_specs=pl.BlockSpec((tm, D), lambda i: (i, 0)),
        compiler_params=_params("parallel"),
    )(x, gain)


def _in_proj(h, w_all, cos, sin):
    S, D = h.shape
    n_proj, _, W = w_all.shape
    tm = min(512, S)

    def body(h_ref, w_ref, cos_ref, sin_ref, o_ref):
        j = pl.program_id(0)
        acc = _dot(h_ref[...], w_ref[...])

        @pl.when(j >= 2)
        def _():
            o_ref[...] = acc

        @pl.when(j < 2)
        def _():
            scale = jnp.where(j == 1, HEAD_DIM ** -0.5, 1.0).astype(F32)
            cs, sn = cos_ref[...], sin_ref[...]
            for hh in range(W // HEAD_DIM):
                cols = slice(hh * HEAD_DIM, (hh + 1) * HEAD_DIM)
                o_ref[:, cols] = _rot(acc[:, cols], cs, sn) * scale

    return pl.pallas_call(
        body, name="in_proj", out_shape=jax.ShapeDtypeStruct((n_proj, S, W), F32), grid=(n_proj, S // tm),
        in_specs=[pl.BlockSpec((tm, D), lambda j, i: (i, 0)),
                  pl.BlockSpec((None, D, W), lambda j, i: (j, 0, 0)),
                  pl.BlockSpec((tm, HEAD_DIM), lambda j, i: (i, 0)),
                  pl.BlockSpec((tm, HEAD_DIM), lambda j, i: (i, 0))],
        out_specs=pl.BlockSpec((None, tm, W), lambda j, i: (j, i, 0)),
        compiler_params=_params("parallel", "parallel"),
    )(h, w_all, cos, sin)


def _head_spec(S, j):
    return pl.BlockSpec((None, S, HEAD_DIM), lambda h, *_: (j, 0, h))


def _ret_chunk(q, k, vb, r_prev, dec, xi, ze):
    qb, kb = q.astype(BF16), k.astype(BF16)
    sb = (_dot_nt(qb, kb) * dec).astype(BF16)
    qx = (q * xi).astype(BF16)
    kz = (k * ze).astype(BF16)
    out = _dot(sb, vb) + _dot(qx, r_prev.astype(BF16))
    return out, _dot_tn(kz, vb), (qb, kb, sb, qx, kz)


def _table_specs():
    return [pl.BlockSpec((None, CHUNK, HEAD_DIM), lambda h, *_: (h, 0, 0))] * 4


def _ret_fwd(proj, tabs, gn_gain, gn_bias):
    _, S, W = proj.shape
    H, nc = W // HEAD_DIM, S // CHUNK

    def body(q_ref, k_ref, v_ref, g_ref, dec_ref, xi_ref, ze_ref, gam_ref, gain_ref, bias_ref, o_ref, r_ref):
        r_ref[...] = jnp.zeros_like(r_ref)
        dec, xi, ze, gam = dec_ref[...], xi_ref[...], ze_ref[...], gam_ref[...]
        gain, bias = gain_ref[...], bias_ref[...]

        def step(c, carry):
            rows = pl.ds(pl.multiple_of(c * CHUNK, CHUNK), CHUNK)
            out, kv, _ = _ret_chunk(q_ref[rows, :], k_ref[rows, :], v_ref[rows, :].astype(BF16), r_ref[...],
                                    dec, xi, ze)
            r_ref[...] = gam * r_ref[...] + kv
            mu = jnp.mean(out, axis=-1, keepdims=True)
            d = out - mu
            yn = d * lax.rsqrt(jnp.mean(d * d, axis=-1, keepdims=True) + EPS)
            g = g_ref[rows, :]
            o_ref[rows, :] = (g * _sigmoid(g) * (yn * gain + bias)).astype(BF16)
            return carry

        lax.fori_loop(0, nc, step, 0)

    vec = pl.BlockSpec((1, HEAD_DIM), lambda h: (0, h))
    return pl.pallas_call(
        body, name="ret_fwd", out_shape=jax.ShapeDtypeStruct((S, W), BF16), grid=(H,),
        in_specs=[_head_spec(S, 0), _head_spec(S, 1), _head_spec(S, 2), _head_spec(S, 3)] + _table_specs() + [vec, vec],
        out_specs=pl.BlockSpec((S, HEAD_DIM), lambda h: (0, h)),
        scratch_shapes=[pltpu.VMEM((HEAD_DIM, HEAD_DIM), F32)],
        compiler_params=_params("parallel"),
    )(proj, proj, proj, proj, *tabs, gn_gain, gn_bias)


def _sb_scores(qb, kk, masked, causal, upper):
    z = _dot_nt(qb, kk) * (HEAD_DIM ** -0.5)
    e = jnp.exp(-jnp.abs(z))
    l1p = jnp.log1p(e)
    log_beta = jnp.minimum(z, 0.0) - l1p
    lk = jnp.minimum(-z, 0.0) - l1p
    if masked:
        lk = jnp.where(causal, lk, 0.0)
    hi = lk.astype(BF16)
    lo = (lk - hi.astype(F32)).astype(BF16)
    cs = _dot(hi, upper) + _dot(lo, upper)
    return z, e, log_beta, lk, cs


def _tri(B, kind):
    r = lax.broadcasted_iota(jnp.int32, (B, B), 0)
    c = lax.broadcasted_iota(jnp.int32, (B, B), 1)
    return {"gt": r > c, "lt": r < c}[kind]


def _ones_where(mask):
    return jnp.where(mask, 1.0, 0.0).astype(BF16)


def _sb_fwd(proj, gain):
    _, S, W = proj.shape
    H = W // HEAD_DIM
    B = min(SB_BLOCK, S)
    nq = S // B
    assert nq <= HEAD_DIM

    def body(q_ref, k_ref, v_ref, g_ref, gain_ref, mix_ref, raw_ref, car_ref, kb_ref, vb_ref):
        qi = pl.program_id(1)

        @pl.when(qi == 0)
        def _():
            kb_ref[...] = k_ref[...].astype(BF16)
            vb_ref[...] = v_ref[...].astype(BF16)

        qb = q_ref[...].astype(BF16)
        causal = _tri(B, "gt")
        upper = _ones_where(causal)
        lane = lax.broadcasted_iota(jnp.int32, (B, HEAD_DIM), 1)

        def block(kb, carry, acc, saved, masked):
            rows = pl.ds(pl.multiple_of(kb * B, B), B)
            _, _, log_beta, lk, cs = _sb_scores(qb, kb_ref[rows, :], masked, causal, upper)
            a = jnp.exp(log_beta + cs + carry)
            if masked:
                a = jnp.where(causal, a, 0.0)
            acc = acc + _dot(a.astype(BF16), vb_ref[rows, :])
            return carry + jnp.sum(lk, axis=1, keepdims=True), acc, jnp.where(lane == kb, carry, saved)

        state = block(qi, jnp.zeros((B, 1), F32), jnp.zeros((B, HEAD_DIM), F32),
                      jnp.full((B, HEAD_DIM), NOT_VISITED, F32), True)

        def live(st):
            return (st[0] >= 0) & (jnp.max(st[1]) >= EXP_IS_ZERO_BELOW)

        def step(st):
            return (st[0] - 1,) + block(st[0], st[1], st[2], st[3], False)

        _, _, acc, saved = lax.while_loop(live, step, (qi - 1,) + state)
        raw_ref[...] = acc
        car_ref[...] = saved
        yn = acc * lax.rsqrt(jnp.mean(acc * acc, axis=-1, keepdims=True) + EPS)
        g = g_ref[...]
        mix_ref[...] = (g * _sigmoid(g) * (yn * gain_ref[...])).astype(BF16)

    tile = lambda j: pl.BlockSpec((None, B, HEAD_DIM), lambda h, i: (j, i, h))
    out_tile = pl.BlockSpec((B, HEAD_DIM), lambda h, i: (i, h))
    return pl.pallas_call(
        body, name="sb_fwd",
        out_shape=(jax.ShapeDtypeStruct((S, W), BF16), jax.ShapeDtypeStruct((S, W), F32),
                   jax.ShapeDtypeStruct((S, W), F32)), grid=(H, nq),
        in_specs=[tile(4), _head_spec(S, 5), _head_spec(S, 6), tile(7), pl.BlockSpec((1, HEAD_DIM), lambda h, i: (0, h))],
        out_specs=(out_tile, out_tile, out_tile),
        scratch_shapes=[pltpu.VMEM((S, HEAD_DIM), BF16), pltpu.VMEM((S, HEAD_DIM), BF16)],
        compiler_params=_params("arbitrary", "arbitrary"),
    )(proj, proj, proj, proj, gain)


def _out_proj_loss(mix_r, mix_s, w_out, x, tgt, gf):
    S, W = mix_r.shape
    D = x.shape[1]
    tm = min(256, S)

    def body(mr_ref, ms_ref, wo_ref, x_ref, t_ref, gf_ref, dx2_ref, dx2b_ref, dmix_ref, loss_ref, gfn_ref):
        @pl.when(pl.program_id(0) == 0)
        def _():
            loss_ref[...] = jnp.zeros_like(loss_ref)
            gfn_ref[...] = jnp.zeros_like(gfn_ref)

        gfv = gf_ref[...]
        x2 = x_ref[...] + (_dot(mr_ref[...], wo_ref[:W, :]) + _dot(ms_ref[...], wo_ref[W:, :]))
        r2 = lax.rsqrt(jnp.mean(x2 * x2, axis=-1, keepdims=True) + EPS)
        n = x2 * r2
        err = n * gfv - t_ref[...]
        loss_ref[...] += 0.5 * jnp.sum(jnp.mean(err * err, axis=-1, keepdims=True))
        dy = err * (1.0 / D)
        gfn_ref[...] += jnp.sum(dy * n, axis=0, keepdims=True)
        dn = dy * gfv
        dx2 = r2 * (dn - n * jnp.mean(dn * n, axis=-1, keepdims=True))
        dx2_ref[...] = dx2
        b = dx2.astype(BF16)
        dx2b_ref[...] = b
        dmix_ref[:, :W] = _dot_nt(b, wo_ref[:W, :])
        dmix_ref[:, W:] = _dot_nt(b, wo_ref[W:, :])

    row = lambda width: pl.BlockSpec((tm, width), lambda i: (i, 0))
    return pl.pallas_call(
        body, name="out_proj_loss",
        out_shape=(jax.ShapeDtypeStruct((S, D), F32), jax.ShapeDtypeStruct((S, D), BF16),
                   jax.ShapeDtypeStruct((S, 2 * W), F32), jax.ShapeDtypeStruct((SUBLANES, LANES), F32),
                   jax.ShapeDtypeStruct((1, D), F32)),
        grid=(S // tm,),
        in_specs=[row(W), row(W), pl.BlockSpec((2 * W, D), lambda i: (0, 0)), row(D), row(D),
                  pl.BlockSpec((1, D), lambda i: (0, 0))],
        out_specs=(row(D), row(D), row(2 * W), pl.BlockSpec((SUBLANES, LANES), lambda i: (0, 0)),
                   pl.BlockSpec((1, D), lambda i: (0, 0))),
        compiler_params=_params("arbitrary"),
    )(mix_r, mix_s, w_out, x, tgt, gf)


def _silu_bwd(g, dm, normed):
    sig = _sigmoid(g)
    return dm * (g * sig), dm * normed * (sig * (1.0 + g * (1.0 - sig)))


def _ret_bwd(proj, dmix, tabs, gn_gain, gn_bias, cos, sin):
    _, S, W = proj.shape
    H, nc = W // HEAD_DIM, S // CHUNK

    def body(q_ref, k_ref, v_ref, g_ref, dm_ref, dec_ref, xi_ref, ze_ref, gam_ref, gain_ref, bias_ref, cos_ref,
             sin_ref, dp_ref, dgain_ref, dbias_ref, rs_ref, r_ref, dr_ref):
        dec, xi, ze, gam = dec_ref[...], xi_ref[...], ze_ref[...], gam_ref[...]
        gain, bias = gain_ref[...], bias_ref[...]

        r_ref[...] = jnp.zeros_like(r_ref)

        def fwd_step(c, carry):
            rows = pl.ds(pl.multiple_of(c * CHUNK, CHUNK), CHUNK)
            rs_ref[c] = r_ref[...]
            kz = (k_ref[rows, :] * ze).astype(BF16)
            r_ref[...] = gam * r_ref[...] + _dot_tn(kz, v_ref[rows, :].astype(BF16))
            return carry

        lax.fori_loop(0, nc, fwd_step, 0)

        dr_ref[...] = jnp.zeros_like(dr_ref)

        def bwd_step(i, carry):
            dgain, dbias = carry
            c = nc - 1 - i
            rows = pl.ds(pl.multiple_of(c * CHUNK, CHUNK), CHUNK)
            q, k, g = q_ref[rows, :], k_ref[rows, :], g_ref[rows, :]
            vb = v_ref[rows, :].astype(BF16)
            rb = rs_ref[c].astype(BF16)
            out, _, (qb, kb, sb, qx, kz) = _ret_chunk(q, k, vb, rs_ref[c], dec, xi, ze)
            mu = jnp.mean(out, axis=-1, keepdims=True)
            d = out - mu
            rstd = lax.rsqrt(jnp.mean(d * d, axis=-1, keepdims=True) + EPS)
            yn = d * rstd
            dgn, dg = _silu_bwd(g, dm_ref[rows, :], yn * gain + bias)
            dgain = dgain + jnp.sum(dgn * yn, axis=0, keepdims=True)
            dbias = dbias + jnp.sum(dgn, axis=0, keepdims=True)
            dyn = dgn * gain
            do = rstd * (dyn - jnp.mean(dyn, axis=-1, keepdims=True)
                         - yn * jnp.mean(dyn * yn, axis=-1, keepdims=True))
            dob = do.astype(BF16)
            drb = dr_ref[...].astype(BF16)
            dv = _dot_tn(sb, dob) + _dot(kz, drb)
            dsb = (_dot_nt(dob, vb) * dec).astype(BF16)
            dq = _dot(dsb, kb) + _dot_nt(dob, rb) * xi
            dk = _dot_tn(dsb, qb) + _dot_nt(vb, drb) * ze
            dr_ref[...] = gam * dr_ref[...] + _dot_tn(qx, dob)
            cs, sn = cos_ref[rows, :], -sin_ref[rows, :]
            dp_ref[0, rows, :] = _rot(dq, cs, sn).astype(BF16)
            dp_ref[1, rows, :] = (_rot(dk, cs, sn) * (HEAD_DIM ** -0.5)).astype(BF16)
            dp_ref[2, rows, :] = dv.astype(BF16)
            dp_ref[3, rows, :] = dg.astype(BF16)
            return dgain, dbias

        zero = jnp.zeros((1, HEAD_DIM), F32)
        dgain, dbias = lax.fori_loop(0, nc, bwd_step, (zero, zero))
        dgain_ref[...] = dgain
        dbias_ref[...] = dbias

    vec = pl.BlockSpec((1, HEAD_DIM), lambda h: (0, h))
    full = pl.BlockSpec((S, HEAD_DIM), lambda h: (0, 0))
    return pl.pallas_call(
        body, name="ret_bwd",
        out_shape=(jax.ShapeDtypeStruct((4, S, W), BF16), jax.ShapeDtypeStruct((1, W), F32),
                   jax.ShapeDtypeStruct((1, W), F32)),
        grid=(H,),
        in_specs=[_head_spec(S, 0), _head_spec(S, 1), _head_spec(S, 2), _head_spec(S, 3),
                  pl.BlockSpec((S, HEAD_DIM), lambda h: (0, h))] + _table_specs() + [vec, vec, full, full],
        out_specs=(pl.BlockSpec((4, S, HEAD_DIM), lambda h: (0, 0, h)), vec, vec),
        scratch_shapes=[pltpu.VMEM((nc, HEAD_DIM, HEAD_DIM), F32), pltpu.VMEM((HEAD_DIM, HEAD_DIM), F32),
                        pltpu.VMEM((HEAD_DIM, HEAD_DIM), F32)],
        compiler_params=_params("parallel"),
    )(proj, proj, proj, proj, dmix, *tabs, gn_gain, gn_bias, cos, sin)


def _sb_bwd(proj, raw, carries, dmix, gain):
    _, S, W = proj.shape
    H = W // HEAD_DIM
    B = min(SB_BLOCK, S)
    nq = S // B

    def body(q_ref, k_ref, v_ref, g_ref, raw_ref, car_ref, dm_ref, gain_ref, dp_ref, dgain_ref,
             kb_ref, vb_ref, dk_ref, dv_ref):
        qi = pl.program_id(1)

        @pl.when(qi == 0)
        def _():
            kb_ref[...] = k_ref[...].astype(BF16)
            vb_ref[...] = v_ref[...].astype(BF16)
            dk_ref[...] = jnp.zeros_like(dk_ref)
            dv_ref[...] = jnp.zeros_like(dv_ref)
            dgain_ref[...] = jnp.zeros_like(dgain_ref)

        q_rows = pl.ds(pl.multiple_of(qi * B, B), B)
        o = raw_ref[...]
        rstd = lax.rsqrt(jnp.mean(o * o, axis=-1, keepdims=True) + EPS)
        yn = o * rstd
        gain_v = gain_ref[...]
        dnrm, dg = _silu_bwd(g_ref[...], dm_ref[...], yn * gain_v)
        dp_ref[3, q_rows, :] = dg.astype(BF16)
        dgain_ref[...] += jnp.sum(dnrm * yn, axis=0, keepdims=True)
        dyn = dnrm * gain_v
        do = rstd * (dyn - yn * jnp.mean(dyn * yn, axis=-1, keepdims=True))
        dob = do.astype(BF16)
        qb = q_ref[...].astype(BF16)
        causal = _tri(B, "gt")
        upper = _ones_where(causal)
        before = _ones_where(_tri(B, "lt"))
        lane = lax.broadcasted_iota(jnp.int32, (B, HEAD_DIM), 1)
        saved = car_ref[...]

        def block(kb, carry_g, dq, masked):
            rows = pl.ds(pl.multiple_of(kb * B, B), B)
            kk, vv = kb_ref[rows, :], vb_ref[rows, :]
            z, e, log_beta, _, cs = _sb_scores(qb, kk, masked, causal, upper)
            carry_lk = jnp.sum(jnp.where(lane == kb, saved, 0.0), axis=1, keepdims=True)
            a = jnp.exp(log_beta + cs + carry_lk)
            if masked:
                a = jnp.where(causal, a, 0.0)
            gmat = _dot_nt(dob, vv) * a
            dv_ref[rows, :] += _dot_tn(a.astype(BF16), dob)
            hi = gmat.astype(BF16)
            lo = (gmat - hi.astype(F32)).astype(BF16)
            dlk = carry_g + (_dot(hi, before) + _dot(lo, before))
            r = 1.0 / (1.0 + e)
            er = e * r
            pos = z >= 0.0
            dz = (gmat * jnp.where(pos, er, r) - dlk * jnp.where(pos, r, er)) * (HEAD_DIM ** -0.5)
            if masked:
                dz = jnp.where(causal, dz, 0.0)
            dzb = dz.astype(BF16)
            dk_ref[rows, :] += _dot_tn(dzb, qb)
            return carry_g + jnp.sum(gmat, axis=1, keepdims=True), dq + _dot(dzb, kk)

        visited = jnp.max(saved, axis=0, keepdims=True) >= EXP_IS_ZERO_BELOW
        first = jnp.min(jnp.where(visited, lane[:1, :], qi))
        state = (jnp.zeros((B, 1), F32), jnp.zeros((B, HEAD_DIM), F32))
        state = lax.fori_loop(first, qi, lambda i, st: block(i, st[0], st[1], False), state)
        state = block(qi, state[0], state[1], True)
        dp_ref[0, q_rows, :] = state[1].astype(BF16)

        @pl.when(qi == nq - 1)
        def _():
            dp_ref[1] = dk_ref[...].astype(BF16)
            dp_ref[2] = dv_ref[...].astype(BF16)

    tile = lambda j: pl.BlockSpec((None, B, HEAD_DIM), lambda h, i: (j, i, h))
    vec = pl.BlockSpec((1, HEAD_DIM), lambda h, i: (0, h))
    return pl.pallas_call(
        body, name="sb_bwd",
        out_shape=(jax.ShapeDtypeStruct((4, S, W), BF16), jax.ShapeDtypeStruct((1, W), F32)),
        grid=(H, nq),
        in_specs=[tile(4), _head_spec(S, 5), _head_spec(S, 6), tile(7),
                  pl.BlockSpec((B, HEAD_DIM), lambda h, i: (i, h)),
                  pl.BlockSpec((B, HEAD_DIM), lambda h, i: (i, h)),
                  pl.BlockSpec((B, HEAD_DIM), lambda h, i: (i, H + h)), vec],
        out_specs=(pl.BlockSpec((4, S, HEAD_DIM), lambda h, i: (0, 0, h)), vec),
        scratch_shapes=[pltpu.VMEM((S, HEAD_DIM), BF16), pltpu.VMEM((S, HEAD_DIM), BF16),
                        pltpu.VMEM((S, HEAD_DIM), F32), pltpu.VMEM((S, HEAD_DIM), F32)],
        compiler_params=_params("arbitrary", "arbitrary"),
    )(proj, proj, proj, proj, raw, carries, dmix, gain)


def _grad_w_in(h, dpr, dps):
    S, D = h.shape
    _, _, W = dpr.shape
    tmm = min(512, D)
    tk = min(2048, S)

    def body(h_ref, r_ref, s_ref, o_ref):
        j, kk = pl.program_id(0), pl.program_id(2)

        def acc(b_ref):
            part = _dot_tn(h_ref[...], b_ref[...])

            @pl.when(kk == 0)
            def _():
                o_ref[...] = part

            @pl.when(kk > 0)
            def _():
                o_ref[...] += part

        pl.when(j < 4)(lambda: acc(r_ref))
        pl.when(j >= 4)(lambda: acc(s_ref))

    return pl.pallas_call(
        body, name="grad_w_in", out_shape=jax.ShapeDtypeStruct((8, D, W), F32), grid=(8, D // tmm, S // tk),
        in_specs=[pl.BlockSpec((tk, tmm), lambda j, m, k: (k, m)),
                  pl.BlockSpec((None, tk, W), lambda j, m, k: (jnp.minimum(j, 3), k, 0)),
                  pl.BlockSpec((None, tk, W), lambda j, m, k: (jnp.maximum(j - 4, 0), k, 0))],
        out_specs=pl.BlockSpec((None, tmm, W), lambda j, m, k: (j, m, 0)),
        compiler_params=_params("parallel", "parallel", "arbitrary"),
    )(h, dpr, dps)


def _grad_w_out(mix_r, mix_s, dx2b):
    S, W = mix_r.shape
    D = dx2b.shape[1]
    tmm = min(512, W)
    tk = min(1024, S)

    def body(r_ref, s_ref, b_ref, o_ref):
        j, kk = pl.program_id(0), pl.program_id(2)

        def acc(a_ref):
            part = _dot_tn(a_ref[...], b_ref[...])

            @pl.when(kk == 0)
            def _():
                o_ref[...] = part

            @pl.when(kk > 0)
            def _():
                o_ref[...] += part

        pl.when(j == 0)(lambda: acc(r_ref))
        pl.when(j == 1)(lambda: acc(s_ref))

    return pl.pallas_call(
        body, name="grad_w_out", out_shape=jax.ShapeDtypeStruct((2, W, D), F32), grid=(2, W // tmm, S // tk),
        in_specs=[pl.BlockSpec((tk, tmm), lambda j, m, k: (k, m)),
                  pl.BlockSpec((tk, tmm), lambda j, m, k: (k, m)),
                  pl.BlockSpec((tk, D), lambda j, m, k: (k, 0))],
        out_specs=pl.BlockSpec((None, tmm, D), lambda j, m, k: (j, m, 0)),
        compiler_params=_params("parallel", "parallel", "arbitrary"),
    )(mix_r, mix_s, dx2b)


def _dh_norm_bwd(dpr, dps, w_all, x, dx2, gain):
    _, S, W = dpr.shape
    D = x.shape[1]
    tm = min(512, S)

    def body(r_ref, s_ref, w_ref, x_ref, dx2_ref, g_ref, gx_ref, dgain_ref, acc_ref):
        i, j = pl.program_id(0), pl.program_id(1)

        @pl.when((i == 0) & (j == 0))
        def _():
            dgain_ref[...] = jnp.zeros_like(dgain_ref)

        def acc(b_ref):
            part = _dot_nt(b_ref[...], w_ref[...])

            @pl.when(j == 0)
            def _():
                acc_ref[...] = part

            @pl.when(j > 0)
            def _():
                acc_ref[...] += part

        pl.when(j < 4)(lambda: acc(r_ref))
        pl.when(j >= 4)(lambda: acc(s_ref))

        @pl.when(j == 7)
        def _():
            xv, dh, gv = x_ref[...], acc_ref[...], g_ref[...]
            r1 = lax.rsqrt(jnp.mean(xv * xv, axis=-1, keepdims=True) + EPS)
            n = xv * r1
            dgain_ref[...] += jnp.sum(dh * n, axis=0, keepdims=True)
            dn = dh * gv
            gx_ref[...] = dx2_ref[...] + r1 * (dn - n * jnp.mean(dn * n, axis=-1, keepdims=True))

    row = pl.BlockSpec((tm, D), lambda i, j: (i, 0))
    one = pl.BlockSpec((1, D), lambda i, j: (0, 0))
    return pl.pallas_call(
        body, name="dh_norm_bwd",
        out_shape=(jax.ShapeDtypeStruct((S, D), F32), jax.ShapeDtypeStruct((1, D), F32)), grid=(S // tm, 8),
        in_specs=[pl.BlockSpec((None, tm, W), lambda i, j: (jnp.minimum(j, 3), i, 0)),
                  pl.BlockSpec((None, tm, W), lambda i, j: (jnp.maximum(j - 4, 0), i, 0)),
                  pl.BlockSpec((None, D, W), lambda i, j: (j, 0, 0)), row, row, one],
        out_specs=(row, one),
        scratch_shapes=[pltpu.VMEM((tm, D), F32)],
        compiler_params=_params("arbitrary", "arbitrary"),
    )(dpr, dps, w_all, x, dx2, gain)


def _rs_local_sum(gw, rin, pos):
    _, R, C = gw.shape
    tr = min(256, R)

    def body(pos_ref, a_ref, b_ref, o_ref):
        o_ref[...] = (a_ref[...] + b_ref[...]).astype(BF16)

    return pl.pallas_call(
        body, name="rs_local_sum", out_shape=jax.ShapeDtypeStruct((4, R, C), BF16),
        grid_spec=pltpu.PrefetchScalarGridSpec(
            num_scalar_prefetch=1, grid=(4, R // tr),
            in_specs=[pl.BlockSpec((None, tr, C), lambda q, i, pos: (2 * q + pos[0], i, 0)),
                      pl.BlockSpec((None, tr, C), lambda q, i, pos: (q, i, 0))],
            out_specs=pl.BlockSpec((None, tr, C), lambda q, i, pos: (q, i, 0))),
        compiler_params=_params("parallel", "parallel"),
    )(pos, gw, rin)


def _adamw(w, g, m, v):
    m2 = ADAM_B1 * m + (1.0 - ADAM_B1) * g
    v2 = ADAM_B2 * v + (1.0 - ADAM_B2) * (g * g)
    m_hat = m2 / (1.0 - ADAM_B1 ** ADAM_STEP)
    v_hat = v2 / (1.0 - ADAM_B2 ** ADAM_STEP)
    delta = -ADAM_LR * (m_hat / (jnp.sqrt(v_hat) + ADAM_EPS) + ADAM_WD * w)
    return delta, m2, v2


def _adamw_shard(gw, rin, rb, w, m, v, pos):
    _, R, C = gw.shape
    tr = min(256, R)

    def body(pos_ref, a_ref, b_ref, rb_ref, w_ref, m_ref, v_ref, g_ref, d_ref, m2_ref, v2_ref):
        g = a_ref[...] + b_ref[...]
        for k in range(3):
            g = g + rb_ref[k].astype(F32)
        g_ref[...] = g
        d_ref[...], m2_ref[...], v2_ref[...] = _adamw(w_ref[...], g, m_ref[...], v_ref[...])

    plain = pl.BlockSpec((tr, C), lambda i, pos: (i, 0))
    shape = jax.ShapeDtypeStruct((R, C), F32)
    return pl.pallas_call(
        body, name="adamw_shard", out_shape=(shape,) * 4,
        grid_spec=pltpu.PrefetchScalarGridSpec(
            num_scalar_prefetch=1, grid=(R // tr,),
            in_specs=[pl.BlockSpec((None, tr, C), lambda i, pos: (2 * pos[1] + pos[0], i, 0)),
                      pl.BlockSpec((None, tr, C), lambda i, pos: (pos[1], i, 0)),
                      pl.BlockSpec((3, tr, C), lambda i, pos: (0, i, 0)), plain, plain, plain],
            out_specs=(plain,) * 4),
        compiler_params=_params("parallel"),
    )(pos, gw, rin, rb, w, m, v)


def _adamw_small(parts, w, m, v):
    _, rows, n = parts.shape

    def body(p_ref, w_ref, m_ref, v_ref, g_ref, d_ref, m2_ref, v2_ref):
        g = p_ref[0]
        for d in range(1, N_DEV):
            g = g + p_ref[d]
        g_ref[...] = g
        d_ref[...], m2_ref[...], v2_ref[...] = _adamw(w_ref[...], g, m_ref[...], v_ref[...])

    shape = jax.ShapeDtypeStruct((rows, n), F32)
    return pl.pallas_call(body, name="adamw_small", out_shape=(shape,) * 4)(parts, w, m, v)


def _rope_tables(S):
    half = HEAD_DIM // 2
    inv = ROPE_THETA ** (-jnp.arange(half, dtype=F32) / half)
    ang = jnp.arange(S, dtype=F32)[:, None] * inv[None, :]
    cos, sin = jnp.cos(ang), jnp.sin(ang)
    return jnp.concatenate([cos, cos], axis=1), jnp.concatenate([-sin, sin], axis=1)


def _retention_tables(H):
    lg = jnp.log1p(-jnp.exp2(-5.0 - jnp.arange(H, dtype=F32)))
    n = jnp.arange(CHUNK, dtype=F32)
    rel = n[:, None] - n[None, :]
    decay = jnp.where(rel >= 0, jnp.exp(lg[:, None, None] * jnp.maximum(rel, 0.0)), 0.0)
    shape = (H, CHUNK, HEAD_DIM)
    xi = jnp.broadcast_to(jnp.exp(lg[:, None] * (n + 1.0))[:, :, None], shape)
    zeta = jnp.broadcast_to(jnp.exp(lg[:, None] * (CHUNK - 1.0 - n))[:, :, None], shape)
    gamma_c = jnp.broadcast_to(jnp.exp(lg * CHUNK)[:, None, None], shape)
    return decay, xi, zeta, gamma_c


def _pack_small(parts):
    flat = []
    for p in parts:
        p = p.reshape(-1)
        flat.append(jnp.pad(p, (0, -p.shape[0] % LANES)))
    flat = jnp.concatenate(flat)
    return jnp.pad(flat, (0, SMALL_N - flat.shape[0])).reshape(SUBLANES, SMALL_N // SUBLANES)


def _unpack_small(packed, shapes):
    flat = packed.reshape(-1)
    out, at = [], 0
    for shp in shapes:
        size = 1
        for s in shp:
            size *= s
        out.append(flat[at:at + size].reshape(shp))
        at += size + (-size % LANES)
    return out


def kernel(x, norm_gain, w_in, ret_gn_gain, ret_gn_bias, sb_norm_gain, w_out, final_norm_gain, loss_target, m_norm_gain, m_w_in, m_ret_gn_gain, m_ret_gn_bias, m_sb_norm_gain, m_w_out, m_final_norm_gain, v_norm_gain, v_w_in, v_ret_gn_gain, v_ret_gn_bias, v_sb_norm_gain, v_w_out, v_final_norm_gain):
    S, D = x.shape[1], x.shape[2]
    W = w_in.shape[2]
    wo_rows = w_out.shape[1]
    H = W // HEAD_DIM
    xs, tgt = x[0], loss_target[0]
    mx, my, mc = _mesh_pos()
    pos = jnp.stack([mc, 2 * mx + my]).astype(jnp.int32)

    w_all, wo_all = _all_gather_weights(w_in[0].astype(BF16), w_out[0].astype(BF16))
    wo_full = wo_all.reshape(N_DEV * wo_rows, D)

    cos, sin = _rope_tables(S)
    tabs = _retention_tables(H)

    h = _rmsnorm_fwd(xs, norm_gain)
    proj = _in_proj(h, w_all, cos, sin)
    mix_r = _ret_fwd(proj, tabs, ret_gn_gain, ret_gn_bias)
    mix_s, raw_s, carries = _sb_fwd(proj, sb_norm_gain)
    dx2, dx2b, dmix, loss_p, d_gf = _out_proj_loss(mix_r, mix_s, wo_full, xs, tgt, final_norm_gain[None])

    dpr, d_rgain, d_rbias = _ret_bwd(proj, dmix, tabs, ret_gn_gain, ret_gn_bias, cos, sin)
    dps, d_sgain = _sb_bwd(proj, raw_s, carries, dmix, sb_norm_gain)
    gw = _grad_w_in(h, dpr, dps)
    gwo = _grad_w_out(mix_r, mix_s, dx2b).reshape(N_DEV, wo_rows, D)
    grad_x, d_gain = _dh_norm_bwd(dpr, dps, w_all, xs, dx2, norm_gain)

    rin, rino = _rs_sibling(gw, gwo)
    rb, rbo = _rs_chips(_rs_local_sum(gw, rin, pos), _rs_local_sum(gwo, rino, pos))
    g_in, d_in, m_in, v_in = _adamw_shard(gw, rin, rb, w_in[0], m_w_in[0], v_w_in[0], pos)
    g_out, d_out, m_out, v_out = _adamw_shard(gwo, rino, rbo, w_out[0], m_w_out[0], v_w_out[0], pos)

    small_w = [norm_gain, ret_gn_gain, ret_gn_bias, sb_norm_gain, final_norm_gain]
    small_m = [m_norm_gain, m_ret_gn_gain, m_ret_gn_bias, m_sb_norm_gain, m_final_norm_gain]
    small_v = [v_norm_gain, v_ret_gn_gain, v_ret_gn_bias, v_sb_norm_gain, v_final_norm_gain]
    shapes = [()] + [w.shape for w in small_w]
    zero = jnp.zeros((), F32)
    parts = _small_all_gather(_pack_small([loss_p[0, 0], d_gain, d_rgain, d_rbias, d_sgain, d_gf]))
    packed = _adamw_small(parts, _pack_small([zero] + small_w), _pack_small([zero] + small_m),
                          _pack_small([zero] + small_v))
    g_s, d_s, m_s, v_s = (_unpack_small(p, shapes) for p in packed)

    grads = [g_s[1], g_in[None], g_s[2], g_s[3], g_s[4], g_out[None], g_s[5]]
    deltas = [d_s[1], d_in[None], d_s[2], d_s[3], d_s[4], d_out[None], d_s[5]]
    new_m = [m_s[1], m_in[None], m_s[2], m_s[3], m_s[4], m_out[None], m_s[5]]
    new_v = [v_s[1], v_in[None], v_s[2], v_s[3], v_s[4], v_out[None], v_s[5]]
    return (g_s[0], grad_x[None], *grads, *deltas, *new_m, *new_v)
```

```python
import functools

import jax
import jax.numpy as jnp
from jax import lax
from jax.experimental import pallas as pl
from jax.experimental.pallas import tpu as pltpu

F32 = jnp.float32
BF16 = jnp.bfloat16

HEAD_DIM = 128
CHUNK = 128
ROPE_THETA = 10000.0
EPS = 1e-6
ADAM_LR = 0.001
ADAM_B1 = 0.9
ADAM_B2 = 0.999
ADAM_EPS = 1e-08
ADAM_WD = 0.01
ADAM_STEP = 10

N_DEV = 8
LANES = 128
SUBLANES = 8
VMEM_LIMIT = 56 * 1024 * 1024
SB_BLOCK = 256
SMALL_N = 8192
EXP_IS_ZERO_BELOW = -104.0
NOT_VISITED = -1e30
MESH = pl.DeviceIdType.MESH

NT = (((1,), (1,)), ((), ()))
TN = (((0,), (0,)), ((), ()))


def _params(*sem):
    return pltpu.CompilerParams(dimension_semantics=sem if sem else None, vmem_limit_bytes=VMEM_LIMIT)


def _dot(a, b):
    return jnp.dot(a, b, preferred_element_type=F32)


def _dot_nt(a, b):
    return lax.dot_general(a, b, NT, preferred_element_type=F32)


def _dot_tn(a, b):
    return lax.dot_general(a, b, TN, preferred_element_type=F32)


def _sigmoid(g):
    return 1.0 / (1.0 + jnp.exp(-g))


def _rot(a, cos, sin_signed):
    return a * cos + pltpu.roll(a, HEAD_DIM // 2, 1) * sin_signed


def _mesh_pos():
    return lax.axis_index("x"), lax.axis_index("y"), lax.axis_index("c")


def _rs_sibling(gw, gwo):
    n_arr = 2

    def body(a_ref, b_ref, oa_ref, ob_ref, send_sems, recv_sems):
        x, y, c = _mesh_pos()
        sibling = (x, y, 1 - c)
        srcs, outs = (a_ref, b_ref), (oa_ref, ob_ref)
        copies = []
        for arr in range(n_arr):
            for q in range(4):
                copies.append(pltpu.make_async_remote_copy(
                    src_ref=srcs[arr].at[2 * q + (1 - c)], dst_ref=outs[arr].at[q],
                    send_sem=send_sems.at[arr, q], recv_sem=recv_sems.at[arr, q],
                    device_id=sibling, device_id_type=MESH))
        for cp in copies:
            cp.start()
        for cp in copies:
            cp.wait_recv()
        for cp in copies:
            cp.wait_send()

    hbm = pl.BlockSpec(memory_space=pltpu.HBM)
    return pl.pallas_call(
        body, name="rs_sibling",
        out_shape=(jax.ShapeDtypeStruct((4,) + gw.shape[1:], gw.dtype),
                   jax.ShapeDtypeStruct((4,) + gwo.shape[1:], gwo.dtype)),
        in_specs=[hbm, hbm], out_specs=(hbm, hbm),
        scratch_shapes=[pltpu.SemaphoreType.DMA((n_arr, 4)), pltpu.SemaphoreType.DMA((n_arr, 4))],
    )(gw, gwo)


def _exchange_chip_sums(srcs, outs, send_sems, recv_sems):
    x, y, c = _mesh_pos()
    copies = []
    for arr, (src, out) in enumerate(zip(srcs, outs)):
        for k in range(1, 4):
            px = 1 - x if k & 2 else x
            py = 1 - y if k & 1 else y
            copies.append(pltpu.make_async_remote_copy(
                src_ref=src.at[2 * px + py], dst_ref=out.at[k - 1],
                send_sem=send_sems.at[arr, k - 1], recv_sem=recv_sems.at[arr, k - 1],
                device_id=(px, py, c), device_id_type=MESH))

    def start():
        for cp in copies:
            cp.start()

    def wait():
        for cp in copies:
            cp.wait_recv()
        for cp in copies:
            cp.wait_send()

    return start, wait


def _small_all_gather(small):
    rows, n = small.shape

    def body(s_ref, o_ref, send_sems, recv_sems, local_sem):
        start, wait = _exchange_with_all(s_ref, o_ref, send_sems, recv_sems, local_sem)
        start()
        wait()

    vmem = pl.BlockSpec(memory_space=pltpu.VMEM)
    return pl.pallas_call(
        body, name="small_all_gather",
        out_shape=jax.ShapeDtypeStruct((N_DEV, rows, n), small.dtype),
        in_specs=[vmem], out_specs=vmem,
        scratch_shapes=[pltpu.SemaphoreType.DMA((N_DEV - 1,)), pltpu.SemaphoreType.DMA((N_DEV - 1,)),
                        pltpu.SemaphoreType.DMA],
    )(small)


def _rmsnorm_fwd(x, gain):
    S, D = x.shape
    tm = min(512, S)

    def body(x_ref, g_ref, h_ref):
        xv = x_ref[...]
        r = lax.rsqrt(jnp.mean(xv * xv, axis=-1, keepdims=True) + EPS)
        h_ref[...] = (xv * r * g_ref[...]).astype(BF16)

    return pl.pallas_call(
        body, name="rmsnorm_fwd", out_shape=jax.ShapeDtypeStruct((S, D), BF16), grid=(S // tm,),
        in_specs=[pl.BlockSpec((tm, D), lambda i: (i, 0)), pl.BlockSpec((1, D), lambda i: (0, 0))],
        out_specs=pl.BlockSpec((tm, D), lambda i: (i, 0)),
        compiler_params=_params("parallel"),
    )(x, gain)


def _gather_order():
    x, y, c = _mesh_pos()
    chips = [(1 - x, y), (x, 1 - y), (1 - x, 1 - y)]
    devs = [(x, y, c), (x, y, 1 - c)] + [(*chip, c) for chip in chips] + [(*chip, 1 - c) for chip in chips]
    return jnp.stack([4 * px + 2 * py + pc for px, py, pc in devs]).astype(jnp.int32)


def _in_proj_gather(h, w_shard, cos, sin, order):
    S, D = h.shape
    W = w_shard.shape[1]
    tm = min(512, S)
    ni = S // tm

    def body(order_ref, h_ref, w_ref, cos_ref, sin_ref, o_ref, wall_ref, wbuf, send_sems, recv_sems, local_sem,
             load_sem):
        step, i = pl.program_id(0), pl.program_id(1)
        x, y, c = _mesh_pos()
        me, sibling = (x, y, c), (x, y, 1 - c)
        chips = [(1 - x, y), (x, 1 - y), (1 - x, 1 - y)]

        def slot(dev):
            px, py, pc = dev
            return wall_ref.at[4 * px + 2 * py + pc]

        def copy(k, block, to, src=None):
            dst = slot(block)
            return pltpu.make_async_remote_copy(
                src_ref=dst if src is None else src, dst_ref=dst, send_sem=send_sems.at[k], recv_sem=recv_sems.at[k],
                device_id=to, device_id_type=MESH)

        def load(src):
            cp = pltpu.make_async_copy(src, wbuf, load_sem)
            cp.start()
            cp.wait()

        first = [copy(0, me, sibling, src=w_ref)] + [copy(1 + j, me, (*chip, c), src=w_ref)
                                                     for j, chip in enumerate(chips)]
        passed = [copy(4 + j, (*chip, c), sibling) for j, chip in enumerate(chips)]
        mine = pltpu.make_async_copy(w_ref, slot(me), local_sem)

        @pl.when(i == 0)
        def _():
            @pl.when(step == 0)
            def _():
                for cp in first:
                    cp.start()
                mine.start()
                load(w_ref)

            @pl.when(step == 1)
            def _():
                copy(0, sibling, me).wait_recv()
                load(slot(sibling))

            for j, chip in enumerate(chips):
                @pl.when(step == 2 + j)
                def _(j=j, chip=chip):
                    copy(1 + j, (*chip, c), me).wait_recv()
                    passed[j].start()
                    load(slot((*chip, c)))

                @pl.when(step == 5 + j)
                def _(j=j, chip=chip):
                    copy(4 + j, (*chip, 1 - c), me).wait_recv()
                    load(slot((*chip, 1 - c)))

        acc = _dot(h_ref[...], wbuf[...])
        b = order_ref[step]

        @pl.when(b >= 2)
        def _():
            o_ref[...] = acc

        @pl.when(b < 2)
        def _():
            scale = jnp.where(b == 1, HEAD_DIM ** -0.5, 1.0).astype(F32)
            cs, sn = cos_ref[...], sin_ref[...]
            for hh in range(W // HEAD_DIM):
                cols = slice(hh * HEAD_DIM, (hh + 1) * HEAD_DIM)
                o_ref[:, cols] = _rot(acc[:, cols], cs, sn) * scale

        @pl.when((step == N_DEV - 1) & (i == ni - 1))
        def _():
            for cp in first + passed:
                cp.wait_send()
            mine.wait()

    hbm = pl.BlockSpec(memory_space=pltpu.HBM)
    rope = pl.BlockSpec((tm, HEAD_DIM), lambda s, i, order: (i, 0))
    return pl.pallas_call(
        body, name="in_proj_gather",
        out_shape=(jax.ShapeDtypeStruct((N_DEV, S, W), F32), jax.ShapeDtypeStruct((N_DEV, D, W), BF16)),
        grid_spec=pltpu.PrefetchScalarGridSpec(
            num_scalar_prefetch=1, grid=(N_DEV, ni),
            in_specs=[pl.BlockSpec((tm, D), lambda s, i, order: (i, 0)), hbm, rope, rope],
            out_specs=(pl.BlockSpec((None, tm, W), lambda s, i, order: (order[s], i, 0)), hbm),
            scratch_shapes=[pltpu.VMEM((D, W), BF16), pltpu.SemaphoreType.DMA((7,)), pltpu.SemaphoreType.DMA((7,)),
                            pltpu.SemaphoreType.DMA, pltpu.SemaphoreType.DMA]),
        compiler_params=_params("arbitrary", "arbitrary"),
    )(order, h, w_shard, cos, sin)


def _head_spec(S, j):
    return pl.BlockSpec((None, S, HEAD_DIM), lambda h, *_: (j, 0, h))


def _ret_chunk(q, k, vb, r_prev, dec, xi, ze):
    qb, kb = q.astype(BF16), k.astype(BF16)
    sb = (_dot_nt(qb, kb) * dec).astype(BF16)
    qx = (q * xi).astype(BF16)
    kz = (k * ze).astype(BF16)
    out = _dot(sb, vb) + _dot(qx, r_prev.astype(BF16))
    return out, _dot_tn(kz, vb), (qb, kb, sb, qx, kz)


def _table_specs():
    return [pl.BlockSpec((None, CHUNK, HEAD_DIM), lambda h, *_: (h, 0, 0))] * 4


def _ret_fwd(proj, tabs, gn_gain, gn_bias):
    _, S, W = proj.shape
    H, nc = W // HEAD_DIM, S // CHUNK

    def body(q_ref, k_ref, v_ref, g_ref, dec_ref, xi_ref, ze_ref, gam_ref, gain_ref, bias_ref, o_ref, r_ref):
        r_ref[...] = jnp.zeros_like(r_ref)
        dec, xi, ze, gam = dec_ref[...], xi_ref[...], ze_ref[...], gam_ref[...]
        gain, bias = gain_ref[...], bias_ref[...]

        def step(c, carry):
            rows = pl.ds(pl.multiple_of(c * CHUNK, CHUNK), CHUNK)
            out, kv, _ = _ret_chunk(q_ref[rows, :], k_ref[rows, :], v_ref[rows, :].astype(BF16), r_ref[...],
                                    dec, xi, ze)
            r_ref[...] = gam * r_ref[...] + kv
            mu = jnp.mean(out, axis=-1, keepdims=True)
            d = out - mu
            yn = d * lax.rsqrt(jnp.mean(d * d, axis=-1, keepdims=True) + EPS)
            g = g_ref[rows, :]
            o_ref[rows, :] = (g * _sigmoid(g) * (yn * gain + bias)).astype(BF16)
            return carry

        lax.fori_loop(0, nc, step, 0)

    vec = pl.BlockSpec((1, HEAD_DIM), lambda h: (0, h))
    return pl.pallas_call(
        body, name="ret_fwd", out_shape=jax.ShapeDtypeStruct((S, W), BF16), grid=(H,),
        in_specs=[_head_spec(S, 0), _head_spec(S, 1), _head_spec(S, 2), _head_spec(S, 3)] + _table_specs() + [vec, vec],
        out_specs=pl.BlockSpec((S, HEAD_DIM), lambda h: (0, h)),
        scratch_shapes=[pltpu.VMEM((HEAD_DIM, HEAD_DIM), F32)],
        compiler_params=_params("parallel"),
    )(proj, proj, proj, proj, *tabs, gn_gain, gn_bias)


def _sb_scores(qb, kk, masked, causal, upper):
    z = _dot_nt(qb, kk) * (HEAD_DIM ** -0.5)
    e = jnp.exp(-jnp.abs(z))
    l1p = jnp.log1p(e)
    log_beta = jnp.minimum(z, 0.0) - l1p
    lk = jnp.minimum(-z, 0.0) - l1p
    if masked:
        lk = jnp.where(causal, lk, 0.0)
    hi = lk.astype(BF16)
    lo = (lk - hi.astype(F32)).astype(BF16)
    cs = _dot(hi, upper) + _dot(lo, upper)
    return z, e, log_beta, lk, cs


def _tri(B, kind):
    r = lax.broadcasted_iota(jnp.int32, (B, B), 0)
    c = lax.broadcasted_iota(jnp.int32, (B, B), 1)
    return {"gt": r > c, "lt": r < c}[kind]


def _ones_where(mask):
    return jnp.where(mask, 1.0, 0.0).astype(BF16)


def _exchange_with_all(src_ref, out_ref, send_sems, recv_sems, local_sem):
    x, y, c = _mesh_pos()
    peers = [(1 - x if k & 4 else x, 1 - y if k & 2 else y, 1 - c if k & 1 else c) for k in range(1, N_DEV)]

    def copy(k, owner, to):
        px, py, pc = owner
        return pltpu.make_async_remote_copy(
            src_ref=src_ref, dst_ref=out_ref.at[4 * px + 2 * py + pc], send_sem=send_sems.at[k],
            recv_sem=recv_sems.at[k], device_id=to, device_id_type=MESH)

    sends = [copy(k, (x, y, c), p) for k, p in enumerate(peers)]
    mine = pltpu.make_async_copy(src_ref, out_ref.at[4 * x + 2 * y + c], local_sem)

    def start():
        for cp in sends:
            cp.start()
        mine.start()

    def wait():
        for k, p in enumerate(peers):
            copy(k, p, p).wait_recv()
        for cp in sends:
            cp.wait_send()
        mine.wait()

    return start, wait


def _sb_fwd(proj, gain, wo_shard):
    _, S, W = proj.shape
    H = W // HEAD_DIM
    B = min(SB_BLOCK, S)
    nq = S // B
    assert nq <= HEAD_DIM

    def body(q_ref, k_ref, v_ref, g_ref, gain_ref, wo_ref, mix_ref, raw_ref, car_ref, woall_ref, kb_ref, vb_ref,
             send_sems, recv_sems, local_sem):
        hd, qi = pl.program_id(0), pl.program_id(1)
        start_gather, wait_gather = _exchange_with_all(wo_ref, woall_ref, send_sems, recv_sems, local_sem)
        pl.when((hd == 0) & (qi == 0))(start_gather)

        @pl.when(qi == 0)
        def _():
            kb_ref[...] = k_ref[...].astype(BF16)
            vb_ref[...] = v_ref[...].astype(BF16)

        qb = q_ref[...].astype(BF16)
        causal = _tri(B, "gt")
        upper = _ones_where(causal)
        lane = lax.broadcasted_iota(jnp.int32, (B, HEAD_DIM), 1)

        def block(kb, carry, acc, saved, masked):
            rows = pl.ds(pl.multiple_of(kb * B, B), B)
            _, _, log_beta, lk, cs = _sb_scores(qb, kb_ref[rows, :], masked, causal, upper)
            a = jnp.exp(log_beta + cs + carry)
            if masked:
                a = jnp.where(causal, a, 0.0)
            acc = acc + _dot(a.astype(BF16), vb_ref[rows, :])
            return carry + jnp.sum(lk, axis=1, keepdims=True), acc, jnp.where(lane == kb, carry, saved)

        state = block(qi, jnp.zeros((B, 1), F32), jnp.zeros((B, HEAD_DIM), F32),
                      jnp.full((B, HEAD_DIM), NOT_VISITED, F32), True)

        def live(st):
            return (st[0] >= 0) & (jnp.max(st[1]) >= EXP_IS_ZERO_BELOW)

        def step(st):
            return (st[0] - 1,) + block(st[0], st[1], st[2], st[3], False)

        _, _, acc, saved = lax.while_loop(live, step, (qi - 1,) + state)
        raw_ref[...] = acc
        car_ref[...] = saved
        yn = acc * lax.rsqrt(jnp.mean(acc * acc, axis=-1, keepdims=True) + EPS)
        g = g_ref[...]
        mix_ref[...] = (g * _sigmoid(g) * (yn * gain_ref[...])).astype(BF16)
        pl.when((hd == H - 1) & (qi == nq - 1))(wait_gather)

    tile = lambda j: pl.BlockSpec((None, B, HEAD_DIM), lambda h, i: (j, i, h))
    out_tile = pl.BlockSpec((B, HEAD_DIM), lambda h, i: (i, h))
    hbm = pl.BlockSpec(memory_space=pltpu.HBM)
    return pl.pallas_call(
        body, name="sb_fwd",
        out_shape=(jax.ShapeDtypeStruct((S, W), BF16), jax.ShapeDtypeStruct((S, W), F32),
                   jax.ShapeDtypeStruct((S, W), F32), jax.ShapeDtypeStruct((N_DEV,) + wo_shard.shape, BF16)),
        grid=(H, nq),
        in_specs=[tile(4), _head_spec(S, 5), _head_spec(S, 6), tile(7),
                  pl.BlockSpec((1, HEAD_DIM), lambda h, i: (0, h)), hbm],
        out_specs=(out_tile, out_tile, out_tile, hbm),
        scratch_shapes=[pltpu.VMEM((S, HEAD_DIM), BF16), pltpu.VMEM((S, HEAD_DIM), BF16),
                        pltpu.SemaphoreType.DMA((N_DEV - 1,)), pltpu.SemaphoreType.DMA((N_DEV - 1,)),
                        pltpu.SemaphoreType.DMA],
        compiler_params=_params("arbitrary", "arbitrary"),
    )(proj, proj, proj, proj, gain, wo_shard)


def _out_proj_loss(mix_r, mix_s, w_out, x, tgt, gf):
    S, W = mix_r.shape
    D = x.shape[1]
    tm = min(256, S)

    def body(mr_ref, ms_ref, wo_ref, x_ref, t_ref, gf_ref, dx2_ref, dx2b_ref, dmix_ref, loss_ref, gfn_ref):
        @pl.when(pl.program_id(0) == 0)
        def _():
            loss_ref[...] = jnp.zeros_like(loss_ref)
            gfn_ref[...] = jnp.zeros_like(gfn_ref)

        gfv = gf_ref[...]
        x2 = x_ref[...] + (_dot(mr_ref[...], wo_ref[:W, :]) + _dot(ms_ref[...], wo_ref[W:, :]))
        r2 = lax.rsqrt(jnp.mean(x2 * x2, axis=-1, keepdims=True) + EPS)
        n = x2 * r2
        err = n * gfv - t_ref[...]
        loss_ref[...] += 0.5 * jnp.sum(jnp.mean(err * err, axis=-1, keepdims=True))
        dy = err * (1.0 / D)
        gfn_ref[...] += jnp.sum(dy * n, axis=0, keepdims=True)
        dn = dy * gfv
        dx2 = r2 * (dn - n * jnp.mean(dn * n, axis=-1, keepdims=True))
        dx2_ref[...] = dx2
        b = dx2.astype(BF16)
        dx2b_ref[...] = b
        dmix_ref[:, :W] = _dot_nt(b, wo_ref[:W, :])
        dmix_ref[:, W:] = _dot_nt(b, wo_ref[W:, :])

    row = lambda width: pl.BlockSpec((tm, width), lambda i: (i, 0))
    return pl.pallas_call(
        body, name="out_proj_loss",
        out_shape=(jax.ShapeDtypeStruct((S, D), F32), jax.ShapeDtypeStruct((S, D), BF16),
                   jax.ShapeDtypeStruct((S, 2 * W), F32), jax.ShapeDtypeStruct((SUBLANES, LANES), F32),
                   jax.ShapeDtypeStruct((1, D), F32)),
        grid=(S // tm,),
        in_specs=[row(W), row(W), pl.BlockSpec((2 * W, D), lambda i: (0, 0)), row(D), row(D),
                  pl.BlockSpec((1, D), lambda i: (0, 0))],
        out_specs=(row(D), row(D), row(2 * W), pl.BlockSpec((SUBLANES, LANES), lambda i: (0, 0)),
                   pl.BlockSpec((1, D), lambda i: (0, 0))),
        compiler_params=_params("arbitrary"),
    )(mix_r, mix_s, w_out, x, tgt, gf)


def _silu_bwd(g, dm, normed):
    sig = _sigmoid(g)
    return dm * (g * sig), dm * normed * (sig * (1.0 + g * (1.0 - sig)))


def _ret_bwd(proj, dmix, tabs, gn_gain, gn_bias, cos, sin):
    _, S, W = proj.shape
    H, nc = W // HEAD_DIM, S // CHUNK

    def body(q_ref, k_ref, v_ref, g_ref, dm_ref, dec_ref, xi_ref, ze_ref, gam_ref, gain_ref, bias_ref, cos_ref,
             sin_ref, dp_ref, dgain_ref, dbias_ref, rs_ref, r_ref, dr_ref):
        dec, xi, ze, gam = dec_ref[...], xi_ref[...], ze_ref[...], gam_ref[...]
        gain, bias = gain_ref[...], bias_ref[...]

        r_ref[...] = jnp.zeros_like(r_ref)

        def fwd_step(c, carry):
            rows = pl.ds(pl.multiple_of(c * CHUNK, CHUNK), CHUNK)
            rs_ref[c] = r_ref[...]
            kz = (k_ref[rows, :] * ze).astype(BF16)
            r_ref[...] = gam * r_ref[...] + _dot_tn(kz, v_ref[rows, :].astype(BF16))
            return carry

        lax.fori_loop(0, nc, fwd_step, 0)

        dr_ref[...] = jnp.zeros_like(dr_ref)

        def bwd_step(i, carry):
            dgain, dbias = carry
            c = nc - 1 - i
            rows = pl.ds(pl.multiple_of(c * CHUNK, CHUNK), CHUNK)
            q, k, g = q_ref[rows, :], k_ref[rows, :], g_ref[rows, :]
            vb = v_ref[rows, :].astype(BF16)
            rb = rs_ref[c].astype(BF16)
            out, _, (qb, kb, sb, qx, kz) = _ret_chunk(q, k, vb, rs_ref[c], dec, xi, ze)
            mu = jnp.mean(out, axis=-1, keepdims=True)
            d = out - mu
            rstd = lax.rsqrt(jnp.mean(d * d, axis=-1, keepdims=True) + EPS)
            yn = d * rstd
            dgn, dg = _silu_bwd(g, dm_ref[rows, :], yn * gain + bias)
            dgain = dgain + jnp.sum(dgn * yn, axis=0, keepdims=True)
            dbias = dbias + jnp.sum(dgn, axis=0, keepdims=True)
            dyn = dgn * gain
            do = rstd * (dyn - jnp.mean(dyn, axis=-1, keepdims=True)
                         - yn * jnp.mean(dyn * yn, axis=-1, keepdims=True))
            dob = do.astype(BF16)
            drb = dr_ref[...].astype(BF16)
            dv = _dot_tn(sb, dob) + _dot(kz, drb)
            dsb = (_dot_nt(dob, vb) * dec).astype(BF16)
            dq = _dot(dsb, kb) + _dot_nt(dob, rb) * xi
            dk = _dot_tn(dsb, qb) + _dot_nt(vb, drb) * ze
            dr_ref[...] = gam * dr_ref[...] + _dot_tn(qx, dob)
            cs, sn = cos_ref[rows, :], -sin_ref[rows, :]
            dp_ref[0, rows, :] = _rot(dq, cs, sn).astype(BF16)
            dp_ref[1, rows, :] = (_rot(dk, cs, sn) * (HEAD_DIM ** -0.5)).astype(BF16)
            dp_ref[2, rows, :] = dv.astype(BF16)
            dp_ref[3, rows, :] = dg.astype(BF16)
            return dgain, dbias

        zero = jnp.zeros((1, HEAD_DIM), F32)
        dgain, dbias = lax.fori_loop(0, nc, bwd_step, (zero, zero))
        dgain_ref[...] = dgain
        dbias_ref[...] = dbias

    vec = pl.BlockSpec((1, HEAD_DIM), lambda h: (0, h))
    full = pl.BlockSpec((S, HEAD_DIM), lambda h: (0, 0))
    return pl.pallas_call(
        body, name="ret_bwd",
        out_shape=(jax.ShapeDtypeStruct((4, S, W), BF16), jax.ShapeDtypeStruct((1, W), F32),
                   jax.ShapeDtypeStruct((1, W), F32)),
        grid=(H,),
        in_specs=[_head_spec(S, 0), _head_spec(S, 1), _head_spec(S, 2), _head_spec(S, 3),
                  pl.BlockSpec((S, HEAD_DIM), lambda h: (0, h))] + _table_specs() + [vec, vec, full, full],
        out_specs=(pl.BlockSpec((4, S, HEAD_DIM), lambda h: (0, 0, h)), vec, vec),
        scratch_shapes=[pltpu.VMEM((nc, HEAD_DIM, HEAD_DIM), F32), pltpu.VMEM((HEAD_DIM, HEAD_DIM), F32),
                        pltpu.VMEM((HEAD_DIM, HEAD_DIM), F32)],
        compiler_params=_params("parallel"),
    )(proj, proj, proj, proj, dmix, *tabs, gn_gain, gn_bias, cos, sin)


def _sb_bwd(proj, raw, carries, dmix, gain):
    _, S, W = proj.shape
    H = W // HEAD_DIM
    B = min(SB_BLOCK, S)
    nq = S // B

    def body(q_ref, k_ref, v_ref, g_ref, raw_ref, car_ref, dm_ref, gain_ref, dp_ref, dgain_ref,
             kb_ref, vb_ref, dk_ref, dv_ref):
        qi = pl.program_id(1)

        @pl.when(qi == 0)
        def _():
            kb_ref[...] = k_ref[...].astype(BF16)
            vb_ref[...] = v_ref[...].astype(BF16)
            dk_ref[...] = jnp.zeros_like(dk_ref)
            dv_ref[...] = jnp.zeros_like(dv_ref)
            dgain_ref[...] = jnp.zeros_like(dgain_ref)

        q_rows = pl.ds(pl.multiple_of(qi * B, B), B)
        o = raw_ref[...]
        rstd = lax.rsqrt(jnp.mean(o * o, axis=-1, keepdims=True) + EPS)
        yn = o * rstd
        gain_v = gain_ref[...]
        dnrm, dg = _silu_bwd(g_ref[...], dm_ref[...], yn * gain_v)
        dp_ref[3, q_rows, :] = dg.astype(BF16)
        dgain_ref[...] += jnp.sum(dnrm * yn, axis=0, keepdims=True)
        dyn = dnrm * gain_v
        do = rstd * (dyn - yn * jnp.mean(dyn * yn, axis=-1, keepdims=True))
        dob = do.astype(BF16)
        qb = q_ref[...].astype(BF16)
        causal = _tri(B, "gt")
        upper = _ones_where(causal)
        before = _ones_where(_tri(B, "lt"))
        lane = lax.broadcasted_iota(jnp.int32, (B, HEAD_DIM), 1)
        saved = car_ref[...]

        def block(kb, carry_g, dq, masked):
            rows = pl.ds(pl.multiple_of(kb * B, B), B)
            kk, vv = kb_ref[rows, :], vb_ref[rows, :]
            z, e, log_beta, _, cs = _sb_scores(qb, kk, masked, causal, upper)
            carry_lk = jnp.sum(jnp.where(lane == kb, saved, 0.0), axis=1, keepdims=True)
            a = jnp.exp(log_beta + cs + carry_lk)
            if masked:
                a = jnp.where(causal, a, 0.0)
            gmat = _dot_nt(dob, vv) * a
            dv_ref[rows, :] += _dot_tn(a.astype(BF16), dob)
            hi = gmat.astype(BF16)
            lo = (gmat - hi.astype(F32)).astype(BF16)
            dlk = carry_g + (_dot(hi, before) + _dot(lo, before))
            r = 1.0 / (1.0 + e)
            er = e * r
            pos = z >= 0.0
            dz = (gmat * jnp.where(pos, er, r) - dlk * jnp.where(pos, r, er)) * (HEAD_DIM ** -0.5)
            if masked:
                dz = jnp.where(causal, dz, 0.0)
            dzb = dz.astype(BF16)
            dk_ref[rows, :] += _dot_tn(dzb, qb)
            return carry_g + jnp.sum(gmat, axis=1, keepdims=True), dq + _dot(dzb, kk)

        visited = jnp.max(saved, axis=0, keepdims=True) >= EXP_IS_ZERO_BELOW
        first = jnp.min(jnp.where(visited, lane[:1, :], qi))
        state = (jnp.zeros((B, 1), F32), jnp.zeros((B, HEAD_DIM), F32))
        state = lax.fori_loop(first, qi, lambda i, st: block(i, st[0], st[1], False), state)
        state = block(qi, state[0], state[1], True)
        dp_ref[0, q_rows, :] = state[1].astype(BF16)

        @pl.when(qi == nq - 1)
        def _():
            dp_ref[1] = dk_ref[...].astype(BF16)
            dp_ref[2] = dv_ref[...].astype(BF16)

    tile = lambda j: pl.BlockSpec((None, B, HEAD_DIM), lambda h, i: (j, i, h))
    vec = pl.BlockSpec((1, HEAD_DIM), lambda h, i: (0, h))
    return pl.pallas_call(
        body, name="sb_bwd",
        out_shape=(jax.ShapeDtypeStruct((4, S, W), BF16), jax.ShapeDtypeStruct((1, W), F32)),
        grid=(H, nq),
        in_specs=[tile(4), _head_spec(S, 5), _head_spec(S, 6), tile(7),
                  pl.BlockSpec((B, HEAD_DIM), lambda h, i: (i, h)),
                  pl.BlockSpec((B, HEAD_DIM), lambda h, i: (i, h)),
                  pl.BlockSpec((B, HEAD_DIM), lambda h, i: (i, H + h)), vec],
        out_specs=(pl.BlockSpec((4, S, HEAD_DIM), lambda h, i: (0, 0, h)), vec),
        scratch_shapes=[pltpu.VMEM((S, HEAD_DIM), BF16), pltpu.VMEM((S, HEAD_DIM), BF16),
                        pltpu.VMEM((S, HEAD_DIM), F32), pltpu.VMEM((S, HEAD_DIM), F32)],
        compiler_params=_params("arbitrary", "arbitrary"),
    )(proj, proj, proj, proj, raw, carries, dmix, gain)


def _grad_w_in(h, dpr, dps):
    S, D = h.shape
    _, _, W = dpr.shape
    tmm = min(512, D)
    tk = min(2048, S)

    def body(h_ref, r_ref, s_ref, o_ref):
        j, kk = pl.program_id(0), pl.program_id(2)

        def acc(b_ref):
            part = _dot_tn(h_ref[...], b_ref[...])

            @pl.when(kk == 0)
            def _():
                o_ref[...] = part

            @pl.when(kk > 0)
            def _():
                o_ref[...] += part

        pl.when(j < 4)(lambda: acc(r_ref))
        pl.when(j >= 4)(lambda: acc(s_ref))

    return pl.pallas_call(
        body, name="grad_w_in", out_shape=jax.ShapeDtypeStruct((8, D, W), F32), grid=(8, D // tmm, S // tk),
        in_specs=[pl.BlockSpec((tk, tmm), lambda j, m, k: (k, m)),
                  pl.BlockSpec((None, tk, W), lambda j, m, k: (jnp.minimum(j, 3), k, 0)),
                  pl.BlockSpec((None, tk, W), lambda j, m, k: (jnp.maximum(j - 4, 0), k, 0))],
        out_specs=pl.BlockSpec((None, tmm, W), lambda j, m, k: (j, m, 0)),
        compiler_params=_params("parallel", "parallel", "arbitrary"),
    )(h, dpr, dps)


def _grad_w_out(mix_r, mix_s, dx2b):
    S, W = mix_r.shape
    D = dx2b.shape[1]
    tmm = min(512, W)
    tk = min(1024, S)

    def body(r_ref, s_ref, b_ref, o_ref):
        j, kk = pl.program_id(0), pl.program_id(2)

        def acc(a_ref):
            part = _dot_tn(a_ref[...], b_ref[...])

            @pl.when(kk == 0)
            def _():
                o_ref[...] = part

            @pl.when(kk > 0)
            def _():
                o_ref[...] += part

        pl.when(j == 0)(lambda: acc(r_ref))
        pl.when(j == 1)(lambda: acc(s_ref))

    return pl.pallas_call(
        body, name="grad_w_out", out_shape=jax.ShapeDtypeStruct((2, W, D), F32), grid=(2, W // tmm, S // tk),
        in_specs=[pl.BlockSpec((tk, tmm), lambda j, m, k: (k, m)),
                  pl.BlockSpec((tk, tmm), lambda j, m, k: (k, m)),
                  pl.BlockSpec((tk, D), lambda j, m, k: (k, 0))],
        out_specs=pl.BlockSpec((None, tmm, D), lambda j, m, k: (j, m, 0)),
        compiler_params=_params("parallel", "parallel", "arbitrary"),
    )(mix_r, mix_s, dx2b)


def _dh_norm_bwd(dpr, dps, w_all, x, dx2, gain, chip_sums):
    _, S, W = dpr.shape
    D = x.shape[1]
    tm = min(512, S)
    ni = S // tm

    def body(r_ref, s_ref, w_ref, x_ref, dx2_ref, g_ref, sa_ref, sb_ref, gx_ref, dgain_ref, ra_ref, rb_ref,
             acc_ref, send_sems, recv_sems):
        i, j = pl.program_id(0), pl.program_id(1)
        start_exchange, wait_exchange = _exchange_chip_sums((sa_ref, sb_ref), (ra_ref, rb_ref), send_sems, recv_sems)

        @pl.when((i == 0) & (j == 0))
        def _():
            start_exchange()
            dgain_ref[...] = jnp.zeros_like(dgain_ref)

        def acc(b_ref):
            part = _dot_nt(b_ref[...], w_ref[...])

            @pl.when(j == 0)
            def _():
                acc_ref[...] = part

            @pl.when(j > 0)
            def _():
                acc_ref[...] += part

        pl.when(j < 4)(lambda: acc(r_ref))
        pl.when(j >= 4)(lambda: acc(s_ref))

        @pl.when(j == 7)
        def _():
            xv, dh, gv = x_ref[...], acc_ref[...], g_ref[...]
            r1 = lax.rsqrt(jnp.mean(xv * xv, axis=-1, keepdims=True) + EPS)
            n = xv * r1
            dgain_ref[...] += jnp.sum(dh * n, axis=0, keepdims=True)
            dn = dh * gv
            gx_ref[...] = dx2_ref[...] + r1 * (dn - n * jnp.mean(dn * n, axis=-1, keepdims=True))

        pl.when((i == ni - 1) & (j == 7))(wait_exchange)

    row = pl.BlockSpec((tm, D), lambda i, j: (i, 0))
    one = pl.BlockSpec((1, D), lambda i, j: (0, 0))
    hbm = pl.BlockSpec(memory_space=pltpu.HBM)
    return pl.pallas_call(
        body, name="dh_norm_bwd",
        out_shape=(jax.ShapeDtypeStruct((S, D), F32), jax.ShapeDtypeStruct((1, D), F32))
        + tuple(jax.ShapeDtypeStruct((3,) + s.shape[1:], s.dtype) for s in chip_sums),
        grid=(ni, 8),
        in_specs=[pl.BlockSpec((None, tm, W), lambda i, j: (jnp.minimum(j, 3), i, 0)),
                  pl.BlockSpec((None, tm, W), lambda i, j: (jnp.maximum(j - 4, 0), i, 0)),
                  pl.BlockSpec((None, D, W), lambda i, j: (j, 0, 0)), row, row, one, hbm, hbm],
        out_specs=(row, one, hbm, hbm),
        scratch_shapes=[pltpu.VMEM((tm, D), F32), pltpu.SemaphoreType.DMA((2, 3)), pltpu.SemaphoreType.DMA((2, 3))],
        compiler_params=_params("arbitrary", "arbitrary"),
    )(dpr, dps, w_all, x, dx2, gain, *chip_sums)


def _rs_local_sum(gw, rin, pos):
    _, R, C = gw.shape
    tr = min(256, R)

    def body(pos_ref, a_ref, b_ref, o_ref):
        o_ref[...] = (a_ref[...] + b_ref[...]).astype(BF16)

    return pl.pallas_call(
        body, name="rs_local_sum", out_shape=jax.ShapeDtypeStruct((4, R, C), BF16),
        grid_spec=pltpu.PrefetchScalarGridSpec(
            num_scalar_prefetch=1, grid=(4, R // tr),
            in_specs=[pl.BlockSpec((None, tr, C), lambda q, i, pos: (2 * q + pos[0], i, 0)),
                      pl.BlockSpec((None, tr, C), lambda q, i, pos: (q, i, 0))],
            out_specs=pl.BlockSpec((None, tr, C), lambda q, i, pos: (q, i, 0))),
        compiler_params=_params("parallel", "parallel"),
    )(pos, gw, rin)


def _adamw(w, g, m, v):
    m2 = ADAM_B1 * m + (1.0 - ADAM_B1) * g
    v2 = ADAM_B2 * v + (1.0 - ADAM_B2) * (g * g)
    m_hat = m2 / (1.0 - ADAM_B1 ** ADAM_STEP)
    v_hat = v2 / (1.0 - ADAM_B2 ** ADAM_STEP)
    delta = -ADAM_LR * (m_hat / (jnp.sqrt(v_hat) + ADAM_EPS) + ADAM_WD * w)
    return delta, m2, v2


def _adamw_shard(gw, rin, rb, w, m, v, pos):
    _, R, C = gw.shape
    tr = min(256, R)

    def body(pos_ref, a_ref, b_ref, rb_ref, w_ref, m_ref, v_ref, g_ref, d_ref, m2_ref, v2_ref):
        g = a_ref[...] + b_ref[...]
        for k in range(3):
            g = g + rb_ref[k].astype(F32)
        g_ref[...] = g
        d_ref[...], m2_ref[...], v2_ref[...] = _adamw(w_ref[...], g, m_ref[...], v_ref[...])

    plain = pl.BlockSpec((tr, C), lambda i, pos: (i, 0))
    shape = jax.ShapeDtypeStruct((R, C), F32)
    return pl.pallas_call(
        body, name="adamw_shard", out_shape=(shape,) * 4,
        grid_spec=pltpu.PrefetchScalarGridSpec(
            num_scalar_prefetch=1, grid=(R // tr,),
            in_specs=[pl.BlockSpec((None, tr, C), lambda i, pos: (2 * pos[1] + pos[0], i, 0)),
                      pl.BlockSpec((None, tr, C), lambda i, pos: (pos[1], i, 0)),
                      pl.BlockSpec((3, tr, C), lambda i, pos: (0, i, 0)), plain, plain, plain],
            out_specs=(plain,) * 4),
        compiler_params=_params("parallel"),
    )(pos, gw, rin, rb, w, m, v)


def _adamw_small(parts, w, m, v):
    _, rows, n = parts.shape

    def body(p_ref, w_ref, m_ref, v_ref, g_ref, d_ref, m2_ref, v2_ref):
        g = p_ref[0]
        for d in range(1, N_DEV):
            g = g + p_ref[d]
        g_ref[...] = g
        d_ref[...], m2_ref[...], v2_ref[...] = _adamw(w_ref[...], g, m_ref[...], v_ref[...])

    shape = jax.ShapeDtypeStruct((rows, n), F32)
    return pl.pallas_call(body, name="adamw_small", out_shape=(shape,) * 4)(parts, w, m, v)


def _rope_tables(S):
    half = HEAD_DIM // 2
    inv = ROPE_THETA ** (-jnp.arange(half, dtype=F32) / half)
    ang = jnp.arange(S, dtype=F32)[:, None] * inv[None, :]
    cos, sin = jnp.cos(ang), jnp.sin(ang)
    return jnp.concatenate([cos, cos], axis=1), jnp.concatenate([-sin, sin], axis=1)


def _retention_tables(H):
    lg = jnp.log1p(-jnp.exp2(-5.0 - jnp.arange(H, dtype=F32)))
    n = jnp.arange(CHUNK, dtype=F32)
    rel = n[:, None] - n[None, :]
    decay = jnp.where(rel >= 0, jnp.exp(lg[:, None, None] * jnp.maximum(rel, 0.0)), 0.0)
    shape = (H, CHUNK, HEAD_DIM)
    xi = jnp.broadcast_to(jnp.exp(lg[:, None] * (n + 1.0))[:, :, None], shape)
    zeta = jnp.broadcast_to(jnp.exp(lg[:, None] * (CHUNK - 1.0 - n))[:, :, None], shape)
    gamma_c = jnp.broadcast_to(jnp.exp(lg * CHUNK)[:, None, None], shape)
    return decay, xi, zeta, gamma_c


def _pack_small(parts):
    flat = []
    for p in parts:
        p = p.reshape(-1)
        flat.append(jnp.pad(p, (0, -p.shape[0] % LANES)))
    flat = jnp.concatenate(flat)
    return jnp.pad(flat, (0, SMALL_N - flat.shape[0])).reshape(SUBLANES, SMALL_N // SUBLANES)


def _unpack_small(packed, shapes):
    flat = packed.reshape(-1)
    out, at = [], 0
    for shp in shapes:
        size = 1
        for s in shp:
            size *= s
        out.append(flat[at:at + size].reshape(shp))
        at += size + (-size % LANES)
    return out


def kernel(x, norm_gain, w_in, ret_gn_gain, ret_gn_bias, sb_norm_gain, w_out, final_norm_gain, loss_target, m_norm_gain, m_w_in, m_ret_gn_gain, m_ret_gn_bias, m_sb_norm_gain, m_w_out, m_final_norm_gain, v_norm_gain, v_w_in, v_ret_gn_gain, v_ret_gn_bias, v_sb_norm_gain, v_w_out, v_final_norm_gain):
    S, D = x.shape[1], x.shape[2]
    W = w_in.shape[2]
    wo_rows = w_out.shape[1]
    H = W // HEAD_DIM
    xs, tgt = x[0], loss_target[0]
    mx, my, mc = _mesh_pos()
    pos = jnp.stack([mc, 2 * mx + my]).astype(jnp.int32)

    cos, sin = _rope_tables(S)
    tabs = _retention_tables(H)

    h = _rmsnorm_fwd(xs, norm_gain)
    proj, w_all = _in_proj_gather(h, w_in[0].astype(BF16), cos, sin, _gather_order())
    mix_r = _ret_fwd(proj, tabs, ret_gn_gain, ret_gn_bias)
    mix_s, raw_s, carries, wo_all = _sb_fwd(proj, sb_norm_gain, w_out[0].astype(BF16))
    wo_full = wo_all.reshape(N_DEV * wo_rows, D)
    dx2, dx2b, dmix, loss_p, d_gf = _out_proj_loss(mix_r, mix_s, wo_full, xs, tgt, final_norm_gain[None])

    dpr, d_rgain, d_rbias = _ret_bwd(proj, dmix, tabs, ret_gn_gain, ret_gn_bias, cos, sin)
    dps, d_sgain = _sb_bwd(proj, raw_s, carries, dmix, sb_norm_gain)
    gw = _grad_w_in(h, dpr, dps)
    gwo = _grad_w_out(mix_r, mix_s, dx2b).reshape(N_DEV, wo_rows, D)
    rin, rino = _rs_sibling(gw, gwo)
    chip_sums = (_rs_local_sum(gw, rin, pos), _rs_local_sum(gwo, rino, pos))
    grad_x, d_gain, rb, rbo = _dh_norm_bwd(dpr, dps, w_all, xs, dx2, norm_gain, chip_sums)
    g_in, d_in, m_in, v_in = _adamw_shard(gw, rin, rb, w_in[0], m_w_in[0], v_w_in[0], pos)
    g_out, d_out, m_out, v_out = _adamw_shard(gwo, rino, rbo, w_out[0], m_w_out[0], v_w_out[0], pos)

    small_w = [norm_gain, ret_gn_gain, ret_gn_bias, sb_norm_gain, final_norm_gain]
    small_m = [m_norm_gain, m_ret_gn_gain, m_ret_gn_bias, m_sb_norm_gain, m_final_norm_gain]
    small_v = [v_norm_gain, v_ret_gn_gain, v_ret_gn_bias, v_sb_norm_gain, v_final_norm_gain]
    shapes = [()] + [w.shape for w in small_w]
    zero = jnp.zeros((), F32)
    parts = _small_all_gather(_pack_small([loss_p[0, 0], d_gain, d_rgain, d_rbias, d_sgain, d_gf]))
    packed = _adamw_small(parts, _pack_small([zero] + small_w), _pack_small([zero] + small_m),
                          _pack_small([zero] + small_v))
    g_s, d_s, m_s, v_s = (_unpack_small(p, shapes) for p in packed)

    grads = [g_s[1], g_in[None], g_s[2], g_s[3], g_s[4], g_out[None], g_s[5]]
    deltas = [d_s[1], d_in[None], d_s[2], d_s[3], d_s[4], d_out[None], d_s[5]]
    new_m = [m_s[1], m_in[None], m_s[2], m_s[3], m_s[4], m_out[None], m_s[5]]
    new_v = [v_s[1], v_in[None], v_s[2], v_s[3], v_s[4], v_out[None], v_s[5]]
    return (g_s[0], grad_x[None], *grads, *deltas, *new_m, *new_v)
```

```python
import functools

import jax
import jax.numpy as jnp
from jax import lax
from jax.experimental import pallas as pl
from jax.experimental.pallas import tpu as pltpu

F32 = jnp.float32
BF16 = jnp.bfloat16

HEAD_DIM = 128
CHUNK = 128
RET_UNROLL = 2
ROPE_THETA = 10000.0
EPS = 1e-6
ADAM_LR = 0.001
ADAM_B1 = 0.9
ADAM_B2 = 0.999
ADAM_EPS = 1e-08
ADAM_WD = 0.01
ADAM_STEP = 10

N_DEV = 8
LANES = 128
SUBLANES = 8
VMEM_LIMIT = 56 * 1024 * 1024
SB_BLOCK = 256
SMALL_N = 8192
EXP_IS_ZERO_BELOW = -104.0
NOT_VISITED = -1e30
MESH = pl.DeviceIdType.MESH

NT = (((1,), (1,)), ((), ()))
TN = (((0,), (0,)), ((), ()))


def _params(*sem):
    return pltpu.CompilerParams(dimension_semantics=sem if sem else None, vmem_limit_bytes=VMEM_LIMIT)


def _dot(a, b):
    return jnp.dot(a, b, preferred_element_type=F32)


def _dot_nt(a, b):
    return lax.dot_general(a, b, NT, preferred_element_type=F32)


def _dot_tn(a, b):
    return lax.dot_general(a, b, TN, preferred_element_type=F32)


def _sigmoid(g):
    return 1.0 / (1.0 + jnp.exp(-g))


def _rot(a, cos, sin_signed):
    return a * cos + pltpu.roll(a, HEAD_DIM // 2, 1) * sin_signed


def _mesh_pos():
    return lax.axis_index("x"), lax.axis_index("y"), lax.axis_index("c")


def _exchange_chip_sums(srcs, outs, send_sems, recv_sems):
    x, y, c = _mesh_pos()
    copies = []
    for arr, (src, out) in enumerate(zip(srcs, outs)):
        for k in range(1, 4):
            px = 1 - x if k & 2 else x
            py = 1 - y if k & 1 else y
            copies.append(pltpu.make_async_remote_copy(
                src_ref=src.at[2 * px + py], dst_ref=out.at[k - 1],
                send_sem=send_sems.at[arr, k - 1], recv_sem=recv_sems.at[arr, k - 1],
                device_id=(px, py, c), device_id_type=MESH))

    def start():
        for cp in copies:
            cp.start()

    def wait():
        for cp in copies:
            cp.wait_recv()
        for cp in copies:
            cp.wait_send()

    return start, wait


def _small_all_gather(small):
    rows, n = small.shape

    def body(s_ref, o_ref, send_sems, recv_sems, local_sem):
        start, wait = _exchange_with_all(s_ref, o_ref, send_sems, recv_sems, local_sem)
        start()
        wait()

    vmem = pl.BlockSpec(memory_space=pltpu.VMEM)
    return pl.pallas_call(
        body, name="small_all_gather",
        out_shape=jax.ShapeDtypeStruct((N_DEV, rows, n), small.dtype),
        in_specs=[vmem], out_specs=vmem,
        scratch_shapes=[pltpu.SemaphoreType.DMA((N_DEV - 1,)), pltpu.SemaphoreType.DMA((N_DEV - 1,)),
                        pltpu.SemaphoreType.DMA],
    )(small)


def _rmsnorm_fwd(x, gain):
    S, D = x.shape
    tm = min(512, S)

    def body(x_ref, g_ref, h_ref, ht_ref):
        xv = x_ref[...]
        r = lax.rsqrt(jnp.mean(xv * xv, axis=-1, keepdims=True) + EPS)
        hv = xv * r * g_ref[...]
        h_ref[...] = hv.astype(BF16)
        ht_ref[...] = hv.T.astype(BF16)

    return pl.pallas_call(
        body, name="rmsnorm_fwd",
        out_shape=(jax.ShapeDtypeStruct((S, D), BF16), jax.ShapeDtypeStruct((D, S), BF16)), grid=(S // tm,),
        in_specs=[pl.BlockSpec((tm, D), lambda i: (i, 0)), pl.BlockSpec((1, D), lambda i: (0, 0))],
        out_specs=(pl.BlockSpec((tm, D), lambda i: (i, 0)), pl.BlockSpec((D, tm), lambda i: (0, i))),
        compiler_params=_params("parallel"),
    )(x, gain)


def _gather_order():
    x, y, c = _mesh_pos()
    chips = [(1 - x, y), (x, 1 - y), (1 - x, 1 - y)]
    devs = [(x, y, c), (x, y, 1 - c)] + [(*chip, c) for chip in chips] + [(*chip, 1 - c) for chip in chips]
    return jnp.stack([4 * px + 2 * py + pc for px, py, pc in devs]).astype(jnp.int32)


def _in_proj_gather(h, w_shard, cos, sin, order):
    S, D = h.shape
    W = w_shard.shape[1]
    tm = min(512, S)
    ni = S // tm

    def body(order_ref, h_ref, w_ref, cos_ref, sin_ref, o_ref, wall_ref, wbuf, send_sems, recv_sems, local_sem,
             load_sem):
        step, i = pl.program_id(0), pl.program_id(1)
        x, y, c = _mesh_pos()
        me, sibling = (x, y, c), (x, y, 1 - c)
        chips = [(1 - x, y), (x, 1 - y), (1 - x, 1 - y)]

        def slot(dev):
            px, py, pc = dev
            return wall_ref.at[4 * px + 2 * py + pc]

        def copy(k, block, to, src=None):
            dst = slot(block)
            return pltpu.make_async_remote_copy(
                src_ref=dst if src is None else src, dst_ref=dst, send_sem=send_sems.at[k], recv_sem=recv_sems.at[k],
                device_id=to, device_id_type=MESH)

        def load(src):
            cp = pltpu.make_async_copy(src, wbuf, load_sem)
            cp.start()
            cp.wait()

        first = [copy(0, me, sibling, src=w_ref)] + [copy(1 + j, me, (*chip, c), src=w_ref)
                                                     for j, chip in enumerate(chips)]
        passed = [copy(4 + j, (*chip, c), sibling) for j, chip in enumerate(chips)]
        mine = pltpu.make_async_copy(w_ref, slot(me), local_sem)

        @pl.when(i == 0)
        def _():
            @pl.when(step == 0)
            def _():
                for cp in first:
                    cp.start()
                mine.start()
                load(w_ref)

            @pl.when(step == 1)
            def _():
                copy(0, sibling, me).wait_recv()
                load(slot(sibling))

            for j, chip in enumerate(chips):
                @pl.when(step == 2 + j)
                def _(j=j, chip=chip):
                    copy(1 + j, (*chip, c), me).wait_recv()
                    passed[j].start()
                    load(slot((*chip, c)))

                @pl.when(step == 5 + j)
                def _(j=j, chip=chip):
                    copy(4 + j, (*chip, 1 - c), me).wait_recv()
                    load(slot((*chip, 1 - c)))

        acc = _dot(h_ref[...], wbuf[...])
        b = order_ref[step]

        @pl.when(b >= 2)
        def _():
            o_ref[...] = acc

        @pl.when(b < 2)
        def _():
            scale = jnp.where(b == 1, HEAD_DIM ** -0.5, 1.0).astype(F32)
            cs, sn = cos_ref[...], sin_ref[...]
            for hh in range(W // HEAD_DIM):
                cols = slice(hh * HEAD_DIM, (hh + 1) * HEAD_DIM)
                o_ref[:, cols] = _rot(acc[:, cols], cs, sn) * scale

        @pl.when((step == N_DEV - 1) & (i == ni - 1))
        def _():
            for cp in first + passed:
                cp.wait_send()
            mine.wait()

    hbm = pl.BlockSpec(memory_space=pltpu.HBM)
    rope = pl.BlockSpec((tm, HEAD_DIM), lambda s, i, order: (i, 0))
    return pl.pallas_call(
        body, name="in_proj_gather",
        out_shape=(jax.ShapeDtypeStruct((N_DEV, S, W), F32), jax.ShapeDtypeStruct((N_DEV, D, W), BF16)),
        grid_spec=pltpu.PrefetchScalarGridSpec(
            num_scalar_prefetch=1, grid=(N_DEV, ni),
            in_specs=[pl.BlockSpec((tm, D), lambda s, i, order: (i, 0)), hbm, rope, rope],
            out_specs=(pl.BlockSpec((None, tm, W), lambda s, i, order: (order[s], i, 0)), hbm),
            scratch_shapes=[pltpu.VMEM((D, W), BF16), pltpu.SemaphoreType.DMA((7,)), pltpu.SemaphoreType.DMA((7,)),
                            pltpu.SemaphoreType.DMA, pltpu.SemaphoreType.DMA]),
        compiler_params=_params("arbitrary", "arbitrary"),
    )(order, h, w_shard, cos, sin)


def _head_spec(S, j):
    return pl.BlockSpec((None, S, HEAD_DIM), lambda h, *_: (j, 0, h))


def _ret_chunk(q, k, vb, r_prev, dec, xi, ze):
    qb, kb = q.astype(BF16), k.astype(BF16)
    sb = (_dot_nt(qb, kb) * dec).astype(BF16)
    qx = (q * xi).astype(BF16)
    kz = (k * ze).astype(BF16)
    out = _dot(sb, vb) + _dot(qx, r_prev.astype(BF16))
    return out, _dot_tn(kz, vb), (qb, kb, sb, qx, kz)


def _table_specs():
    return [pl.BlockSpec((None, CHUNK, HEAD_DIM), lambda h, *_: (h, 0, 0))] * 4


def _ret_fwd(proj, tabs, gn_gain, gn_bias):
    _, S, W = proj.shape
    H, nc = W // HEAD_DIM, S // CHUNK

    def body(q_ref, k_ref, v_ref, g_ref, dec_ref, xi_ref, ze_ref, gam_ref, gain_ref, bias_ref, o_ref):
        dec, xi, ze, gam = dec_ref[...], xi_ref[...], ze_ref[...], gam_ref[...]
        gain, bias = gain_ref[...], bias_ref[...]

        def step(c, state):
            rows = pl.ds(pl.multiple_of(c * CHUNK, CHUNK), CHUNK)
            out, kv, _ = _ret_chunk(q_ref[rows, :], k_ref[rows, :], v_ref[rows, :].astype(BF16), state,
                                    dec, xi, ze)
            mu = jnp.mean(out, axis=-1, keepdims=True)
            d = out - mu
            yn = d * lax.rsqrt(jnp.mean(d * d, axis=-1, keepdims=True) + EPS)
            g = g_ref[rows, :]
            o_ref[rows, :] = (g * _sigmoid(g) * (yn * gain + bias)).astype(BF16)
            return gam * state + kv

        lax.fori_loop(0, nc, step, jnp.zeros((HEAD_DIM, HEAD_DIM), F32), unroll=RET_UNROLL)

    vec = pl.BlockSpec((1, HEAD_DIM), lambda h: (0, h))
    return pl.pallas_call(
        body, name="ret_fwd", out_shape=jax.ShapeDtypeStruct((S, W), BF16), grid=(H,),
        in_specs=[_head_spec(S, 0), _head_spec(S, 1), _head_spec(S, 2), _head_spec(S, 3)] + _table_specs() + [vec, vec],
        out_specs=pl.BlockSpec((S, HEAD_DIM), lambda h: (0, h)),
        compiler_params=_params("parallel"),
    )(proj, proj, proj, proj, *tabs, gn_gain, gn_bias)


def _sb_scores(qb, kk, masked, causal, upper):
    z = _dot_nt(qb, kk) * (HEAD_DIM ** -0.5)
    e = jnp.exp(-jnp.abs(z))
    l1p = jnp.log1p(e)
    log_beta = jnp.minimum(z, 0.0) - l1p
    lk = jnp.minimum(-z, 0.0) - l1p
    if masked:
        lk = jnp.where(causal, lk, 0.0)
    hi = lk.astype(BF16)
    lo = (lk - hi.astype(F32)).astype(BF16)
    cs = _dot(hi, upper) + _dot(lo, upper)
    return z, e, log_beta, lk, cs


def _tri(B, kind):
    r = lax.broadcasted_iota(jnp.int32, (B, B), 0)
    c = lax.broadcasted_iota(jnp.int32, (B, B), 1)
    return {"gt": r > c, "lt": r < c}[kind]


def _ones_where(mask):
    return jnp.where(mask, 1.0, 0.0).astype(BF16)


def _exchange_with_all(src_ref, out_ref, send_sems, recv_sems, local_sem):
    x, y, c = _mesh_pos()
    peers = [(1 - x if k & 4 else x, 1 - y if k & 2 else y, 1 - c if k & 1 else c) for k in range(1, N_DEV)]

    def copy(k, owner, to):
        px, py, pc = owner
        return pltpu.make_async_remote_copy(
            src_ref=src_ref, dst_ref=out_ref.at[4 * px + 2 * py + pc], send_sem=send_sems.at[k],
            recv_sem=recv_sems.at[k], device_id=to, device_id_type=MESH)

    sends = [copy(k, (x, y, c), p) for k, p in enumerate(peers)]
    mine = pltpu.make_async_copy(src_ref, out_ref.at[4 * x + 2 * y + c], local_sem)

    def start():
        for cp in sends:
            cp.start()
        mine.start()

    def wait():
        for k, p in enumerate(peers):
            copy(k, p, p).wait_recv()
        for cp in sends:
            cp.wait_send()
        mine.wait()

    return start, wait


def _sb_fwd(proj, gain, wo_shard):
    _, S, W = proj.shape
    H = W // HEAD_DIM
    B = min(SB_BLOCK, S)
    nq = S // B
    assert nq <= HEAD_DIM

    def body(q_ref, k_ref, v_ref, g_ref, gain_ref, wo_ref, mix_ref, raw_ref, car_ref, woall_ref, kb_ref, vb_ref,
             send_sems, recv_sems, local_sem):
        hd, qi = pl.program_id(0), pl.program_id(1)
        start_gather, wait_gather = _exchange_with_all(wo_ref, woall_ref, send_sems, recv_sems, local_sem)
        pl.when((hd == 0) & (qi == 0))(start_gather)

        @pl.when(qi == 0)
        def _():
            kb_ref[...] = k_ref[...].astype(BF16)
            vb_ref[...] = v_ref[...].astype(BF16)

        qb = q_ref[...].astype(BF16)
        causal = _tri(B, "gt")
        upper = _ones_where(causal)
        lane = lax.broadcasted_iota(jnp.int32, (B, HEAD_DIM), 1)

        def block(kb, carry, acc, saved, masked):
            rows = pl.ds(pl.multiple_of(kb * B, B), B)
            _, _, log_beta, lk, cs = _sb_scores(qb, kb_ref[rows, :], masked, causal, upper)
            a = jnp.exp(log_beta + cs + carry)
            if masked:
                a = jnp.where(causal, a, 0.0)
            acc = acc + _dot(a.astype(BF16), vb_ref[rows, :])
            return carry + jnp.sum(lk, axis=1, keepdims=True), acc, jnp.where(lane == kb, carry, saved)

        init = (jnp.zeros((B, 1), F32), jnp.zeros((B, HEAD_DIM), F32), jnp.full((B, HEAD_DIM), NOT_VISITED, F32))

        def live(st):
            return (st[0] >= 0) & (jnp.max(st[1]) >= EXP_IS_ZERO_BELOW)

        def step(st):
            return (st[0] - 1,) + block(st[0], st[1], st[2], st[3], False)

        def finish(acc, saved):
            raw_ref[...] = acc
            car_ref[...] = saved
            yn = acc * lax.rsqrt(jnp.mean(acc * acc, axis=-1, keepdims=True) + EPS)
            g = g_ref[...]
            mix_ref[...] = (g * _sigmoid(g) * (yn * gain_ref[...])).astype(BF16)

        @pl.when(qi == 0)
        def _():
            _, acc, saved = block(qi, *init, True)
            finish(acc, saved)

        @pl.when(qi > 0)
        def _():
            state = block(qi - 1, *block(qi, *init, True), False)
            _, _, acc, saved = lax.while_loop(live, step, (qi - 2,) + state)
            finish(acc, saved)

        pl.when((hd == H - 1) & (qi == nq - 1))(wait_gather)

    tile = lambda j: pl.BlockSpec((None, B, HEAD_DIM), lambda h, i: (j, i, h))
    out_tile = pl.BlockSpec((B, HEAD_DIM), lambda h, i: (i, h))
    hbm = pl.BlockSpec(memory_space=pltpu.HBM)
    return pl.pallas_call(
        body, name="sb_fwd",
        out_shape=(jax.ShapeDtypeStruct((S, W), BF16), jax.ShapeDtypeStruct((S, W), F32),
                   jax.ShapeDtypeStruct((S, W), F32), jax.ShapeDtypeStruct((N_DEV,) + wo_shard.shape, BF16)),
        grid=(H, nq),
        in_specs=[tile(4), _head_spec(S, 5), _head_spec(S, 6), tile(7),
                  pl.BlockSpec((1, HEAD_DIM), lambda h, i: (0, h)), hbm],
        out_specs=(out_tile, out_tile, out_tile, hbm),
        scratch_shapes=[pltpu.VMEM((S, HEAD_DIM), BF16), pltpu.VMEM((S, HEAD_DIM), BF16),
                        pltpu.SemaphoreType.DMA((N_DEV - 1,)), pltpu.SemaphoreType.DMA((N_DEV - 1,)),
                        pltpu.SemaphoreType.DMA],
        compiler_params=_params("arbitrary", "arbitrary"),
    )(proj, proj, proj, proj, gain, wo_shard)


def _out_proj_loss(mix_r, mix_s, w_out, x, tgt, gf):
    S, W = mix_r.shape
    D = x.shape[1]
    tm = min(256, S)

    def body(mr_ref, ms_ref, wo_ref, x_ref, t_ref, gf_ref, dx2_ref, dx2b_ref, dmix_ref, loss_ref, gfn_ref):
        @pl.when(pl.program_id(0) == 0)
        def _():
            loss_ref[...] = jnp.zeros_like(loss_ref)
            gfn_ref[...] = jnp.zeros_like(gfn_ref)

        gfv = gf_ref[...]
        x2 = x_ref[...] + (_dot(mr_ref[...], wo_ref[:W, :]) + _dot(ms_ref[...], wo_ref[W:, :]))
        r2 = lax.rsqrt(jnp.mean(x2 * x2, axis=-1, keepdims=True) + EPS)
        n = x2 * r2
        err = n * gfv - t_ref[...]
        loss_ref[...] += 0.5 * jnp.sum(jnp.mean(err * err, axis=-1, keepdims=True))
        dy = err * (1.0 / D)
        gfn_ref[...] += jnp.sum(dy * n, axis=0, keepdims=True)
        dn = dy * gfv
        dx2 = r2 * (dn - n * jnp.mean(dn * n, axis=-1, keepdims=True))
        dx2_ref[...] = dx2
        b = dx2.astype(BF16)
        dx2b_ref[...] = b
        dmix_ref[:, :W] = _dot_nt(b, wo_ref[:W, :])
        dmix_ref[:, W:] = _dot_nt(b, wo_ref[W:, :])

    row = lambda width: pl.BlockSpec((tm, width), lambda i: (i, 0))
    return pl.pallas_call(
        body, name="out_proj_loss",
        out_shape=(jax.ShapeDtypeStruct((S, D), F32), jax.ShapeDtypeStruct((S, D), BF16),
                   jax.ShapeDtypeStruct((S, 2 * W), F32), jax.ShapeDtypeStruct((SUBLANES, LANES), F32),
                   jax.ShapeDtypeStruct((1, D), F32)),
        grid=(S // tm,),
        in_specs=[row(W), row(W), pl.BlockSpec((2 * W, D), lambda i: (0, 0)), row(D), row(D),
                  pl.BlockSpec((1, D), lambda i: (0, 0))],
        out_specs=(row(D), row(D), row(2 * W), pl.BlockSpec((SUBLANES, LANES), lambda i: (0, 0)),
                   pl.BlockSpec((1, D), lambda i: (0, 0))),
        compiler_params=_params("arbitrary"),
    )(mix_r, mix_s, w_out, x, tgt, gf)


def _silu_bwd(g, dm, normed):
    sig = _sigmoid(g)
    return dm * (g * sig), dm * normed * (sig * (1.0 + g * (1.0 - sig)))


def _ret_bwd(proj, dmix, tabs, gn_gain, gn_bias, cos, sin):
    _, S, W = proj.shape
    H, nc = W // HEAD_DIM, S // CHUNK

    def body(q_ref, k_ref, v_ref, g_ref, dm_ref, dec_ref, xi_ref, ze_ref, gam_ref, gain_ref, bias_ref, cos_ref,
             sin_ref, dp_ref, dgain_ref, dbias_ref, rs_ref):
        dec, xi, ze, gam = dec_ref[...], xi_ref[...], ze_ref[...], gam_ref[...]
        gain, bias = gain_ref[...], bias_ref[...]

        def fwd_step(c, state):
            rows = pl.ds(pl.multiple_of(c * CHUNK, CHUNK), CHUNK)
            rs_ref[c] = state
            kz = (k_ref[rows, :] * ze).astype(BF16)
            return gam * state + _dot_tn(kz, v_ref[rows, :].astype(BF16))

        lax.fori_loop(0, nc, fwd_step, jnp.zeros((HEAD_DIM, HEAD_DIM), F32), unroll=RET_UNROLL)

        def bwd_step(i, carry):
            dgain, dbias, dstate = carry
            c = nc - 1 - i
            rows = pl.ds(pl.multiple_of(c * CHUNK, CHUNK), CHUNK)
            q, k, g = q_ref[rows, :], k_ref[rows, :], g_ref[rows, :]
            vb = v_ref[rows, :].astype(BF16)
            rb = rs_ref[c].astype(BF16)
            out, _, (qb, kb, sb, qx, kz) = _ret_chunk(q, k, vb, rs_ref[c], dec, xi, ze)
            mu = jnp.mean(out, axis=-1, keepdims=True)
            d = out - mu
            rstd = lax.rsqrt(jnp.mean(d * d, axis=-1, keepdims=True) + EPS)
            yn = d * rstd
            dgn, dg = _silu_bwd(g, dm_ref[rows, :], yn * gain + bias)
            dgain = dgain + jnp.sum(dgn * yn, axis=0, keepdims=True)
            dbias = dbias + jnp.sum(dgn, axis=0, keepdims=True)
            dyn = dgn * gain
            do = rstd * (dyn - jnp.mean(dyn, axis=-1, keepdims=True)
                         - yn * jnp.mean(dyn * yn, axis=-1, keepdims=True))
            dob = do.astype(BF16)
            drb = dstate.astype(BF16)
            dv = _dot_tn(sb, dob) + _dot(kz, drb)
            dsb = (_dot_nt(dob, vb) * dec).astype(BF16)
            dq = _dot(dsb, kb) + _dot_nt(dob, rb) * xi
            dk = _dot_tn(dsb, qb) + _dot_nt(vb, drb) * ze
            cs, sn = cos_ref[rows, :], -sin_ref[rows, :]
            dp_ref[0, rows, :] = _rot(dq, cs, sn).astype(BF16)
            dp_ref[1, rows, :] = (_rot(dk, cs, sn) * (HEAD_DIM ** -0.5)).astype(BF16)
            dp_ref[2, rows, :] = dv.astype(BF16)
            dp_ref[3, rows, :] = dg.astype(BF16)
            return dgain, dbias, gam * dstate + _dot_tn(qx, dob)

        zero = jnp.zeros((1, HEAD_DIM), F32)
        dgain, dbias, _ = lax.fori_loop(0, nc, bwd_step, (zero, zero, jnp.zeros((HEAD_DIM, HEAD_DIM), F32)),
                                        unroll=RET_UNROLL)
        dgain_ref[...] = dgain
        dbias_ref[...] = dbias

    vec = pl.BlockSpec((1, HEAD_DIM), lambda h: (0, h))
    full = pl.BlockSpec((S, HEAD_DIM), lambda h: (0, 0))
    return pl.pallas_call(
        body, name="ret_bwd",
        out_shape=(jax.ShapeDtypeStruct((4, S, W), BF16), jax.ShapeDtypeStruct((1, W), F32),
                   jax.ShapeDtypeStruct((1, W), F32)),
        grid=(H,),
        in_specs=[_head_spec(S, 0), _head_spec(S, 1), _head_spec(S, 2), _head_spec(S, 3),
                  pl.BlockSpec((S, HEAD_DIM), lambda h: (0, h))] + _table_specs() + [vec, vec, full, full],
        out_specs=(pl.BlockSpec((4, S, HEAD_DIM), lambda h: (0, 0, h)), vec, vec),
        scratch_shapes=[pltpu.VMEM((nc, HEAD_DIM, HEAD_DIM), F32)],
        compiler_params=_params("parallel"),
    )(proj, proj, proj, proj, dmix, *tabs, gn_gain, gn_bias, cos, sin)


def _sb_bwd(proj, raw, carries, dmix, gain):
    _, S, W = proj.shape
    H = W // HEAD_DIM
    B = min(SB_BLOCK, S)
    nq = S // B

    def body(q_ref, k_ref, v_ref, g_ref, raw_ref, car_ref, dm_ref, gain_ref, dp_ref, dgain_ref,
             kb_ref, vb_ref, dk_ref, dv_ref):
        qi = pl.program_id(1)

        @pl.when(qi == 0)
        def _():
            kb_ref[...] = k_ref[...].astype(BF16)
            vb_ref[...] = v_ref[...].astype(BF16)
            dk_ref[...] = jnp.zeros_like(dk_ref)
            dv_ref[...] = jnp.zeros_like(dv_ref)
            dgain_ref[...] = jnp.zeros_like(dgain_ref)

        q_rows = pl.ds(pl.multiple_of(qi * B, B), B)
        o = raw_ref[...]
        rstd = lax.rsqrt(jnp.mean(o * o, axis=-1, keepdims=True) + EPS)
        yn = o * rstd
        gain_v = gain_ref[...]
        dnrm, dg = _silu_bwd(g_ref[...], dm_ref[...], yn * gain_v)
        dp_ref[3, q_rows, :] = dg.astype(BF16)
        dgain_ref[...] += jnp.sum(dnrm * yn, axis=0, keepdims=True)
        dyn = dnrm * gain_v
        do = rstd * (dyn - yn * jnp.mean(dyn * yn, axis=-1, keepdims=True))
        dob = do.astype(BF16)
        qb = q_ref[...].astype(BF16)
        causal = _tri(B, "gt")
        upper = _ones_where(causal)
        before = _ones_where(_tri(B, "lt"))
        lane = lax.broadcasted_iota(jnp.int32, (B, HEAD_DIM), 1)
        saved = car_ref[...]

        def block(kb, carry_g, dq, masked):
            rows = pl.ds(pl.multiple_of(kb * B, B), B)
            kk, vv = kb_ref[rows, :], vb_ref[rows, :]
            z, e, log_beta, _, cs = _sb_scores(qb, kk, masked, causal, upper)
            carry_lk = jnp.sum(jnp.where(lane == kb, saved, 0.0), axis=1, keepdims=True)
            a = jnp.exp(log_beta + cs + carry_lk)
            if masked:
                a = jnp.where(causal, a, 0.0)
            gmat = _dot_nt(dob, vv) * a
            dv_ref[rows, :] += _dot_tn(a.astype(BF16), dob)
            hi = gmat.astype(BF16)
            lo = (gmat - hi.astype(F32)).astype(BF16)
            dlk = carry_g + (_dot(hi, before) + _dot(lo, before))
            r = 1.0 / (1.0 + e)
            er = e * r
            pos = z >= 0.0
            dz = (gmat * jnp.where(pos, er, r) - dlk * jnp.where(pos, r, er)) * (HEAD_DIM ** -0.5)
            if masked:
                dz = jnp.where(causal, dz, 0.0)
            dzb = dz.astype(BF16)
            dk_ref[rows, :] += _dot_tn(dzb, qb)
            return carry_g + jnp.sum(gmat, axis=1, keepdims=True), dq + _dot(dzb, kk)

        visited = jnp.max(saved, axis=0, keepdims=True) >= EXP_IS_ZERO_BELOW
        first = jnp.min(jnp.where(visited, lane[:1, :], qi))
        init = (jnp.zeros((B, 1), F32), jnp.zeros((B, HEAD_DIM), F32))

        @pl.when(qi == 0)
        def _():
            dp_ref[0, q_rows, :] = block(qi, *init, True)[1].astype(BF16)

        @pl.when(qi > 0)
        def _():
            state = lax.fori_loop(first, qi - 1, lambda i, st: block(i, st[0], st[1], False), init)
            state = block(qi, *block(qi - 1, *state, False), True)
            dp_ref[0, q_rows, :] = state[1].astype(BF16)

        @pl.when(qi == nq - 1)
        def _():
            dp_ref[1] = dk_ref[...].astype(BF16)
            dp_ref[2] = dv_ref[...].astype(BF16)

    tile = lambda j: pl.BlockSpec((None, B, HEAD_DIM), lambda h, i: (j, i, h))
    vec = pl.BlockSpec((1, HEAD_DIM), lambda h, i: (0, h))
    return pl.pallas_call(
        body, name="sb_bwd",
        out_shape=(jax.ShapeDtypeStruct((4, S, W), BF16), jax.ShapeDtypeStruct((1, W), F32)),
        grid=(H, nq),
        in_specs=[tile(4), _head_spec(S, 5), _head_spec(S, 6), tile(7),
                  pl.BlockSpec((B, HEAD_DIM), lambda h, i: (i, h)),
                  pl.BlockSpec((B, HEAD_DIM), lambda h, i: (i, h)),
                  pl.BlockSpec((B, HEAD_DIM), lambda h, i: (i, H + h)), vec],
        out_specs=(pl.BlockSpec((4, S, HEAD_DIM), lambda h, i: (0, 0, h)), vec),
        scratch_shapes=[pltpu.VMEM((S, HEAD_DIM), BF16), pltpu.VMEM((S, HEAD_DIM), BF16),
                        pltpu.VMEM((S, HEAD_DIM), F32), pltpu.VMEM((S, HEAD_DIM), F32)],
        compiler_params=_params("arbitrary", "arbitrary"),
    )(proj, proj, proj, proj, raw, carries, dmix, gain)


def _grad_w_in_half(ht, dpr, dps, core, name, to_sibling=None):
    D, S = ht.shape
    _, _, W = dpr.shape
    tmm = min(512, D)
    nm = D // tmm

    def body(core_ref, ht_ref, r_ref, s_ref, *rest):
        o_ref = rest[2] if to_sibling else rest[0]
        q, m = pl.program_id(0), pl.program_id(1)
        if to_sibling:
            ga_ref, gwo_ref, _, rin_ref, rino_ref, send_sems, recv_sems = rest
            x, y, c = _mesh_pos()
            copies = []
            for k in range(4):
                for arr, (src, dst) in enumerate(((ga_ref.at[k], rin_ref.at[k]),
                                                  (gwo_ref.at[2 * k + (1 - c)], rino_ref.at[k]))):
                    copies.append(pltpu.make_async_remote_copy(
                        src_ref=src, dst_ref=dst, send_sem=send_sems.at[arr, k], recv_sem=recv_sems.at[arr, k],
                        device_id=(x, y, 1 - c), device_id_type=MESH))

            @pl.when((q == 0) & (m == 0))
            def _():
                for cp in copies:
                    cp.start()

        @pl.when(q < 2)
        def _():
            o_ref[...] = _dot(ht_ref[...], r_ref[...])

        @pl.when(q >= 2)
        def _():
            o_ref[...] = _dot(ht_ref[...], s_ref[...])

        if to_sibling:
            @pl.when((q == 3) & (m == nm - 1))
            def _():
                for cp in copies:
                    cp.wait_recv()
                for cp in copies:
                    cp.wait_send()

    hbm = pl.BlockSpec(memory_space=pltpu.HBM)
    gw_shape = jax.ShapeDtypeStruct((4, D, W), F32)
    out_shape, out_specs, extra_in, scratch = (gw_shape,), (pl.BlockSpec((None, tmm, W), lambda q, m, core: (q, m, 0)),), [], []
    if to_sibling:
        out_shape += (gw_shape, jax.ShapeDtypeStruct((4,) + to_sibling[1].shape[1:], F32))
        out_specs += (hbm, hbm)
        extra_in = [hbm, hbm]
        scratch = [pltpu.SemaphoreType.DMA((2, 4)), pltpu.SemaphoreType.DMA((2, 4))]
    return pl.pallas_call(
        body, name=name, out_shape=out_shape,
        grid_spec=pltpu.PrefetchScalarGridSpec(
            num_scalar_prefetch=1, grid=(4, nm),
            in_specs=[pl.BlockSpec((tmm, S), lambda q, m, core: (m, 0)),
                      pl.BlockSpec((None, S, W), lambda q, m, core: (jnp.minimum(2 * q + core[0], 3), 0, 0)),
                      pl.BlockSpec((None, S, W), lambda q, m, core: (jnp.maximum(2 * q + core[0] - 4, 0), 0, 0))]
            + extra_in,
            out_specs=out_specs, scratch_shapes=scratch),
        compiler_params=_params("arbitrary", "arbitrary"),
    )(core, ht, dpr, dps, *(to_sibling or ()))


def _grad_w_out(mix_r, mix_s, dx2b):
    S, W = mix_r.shape
    D = dx2b.shape[1]
    tmm = min(512, W)
    tk = min(1024, S)

    def body(r_ref, s_ref, b_ref, o_ref):
        j, kk = pl.program_id(0), pl.program_id(2)

        def acc(a_ref):
            part = _dot_tn(a_ref[...], b_ref[...])

            @pl.when(kk == 0)
            def _():
                o_ref[...] = part

            @pl.when(kk > 0)
            def _():
                o_ref[...] += part

        pl.when(j == 0)(lambda: acc(r_ref))
        pl.when(j == 1)(lambda: acc(s_ref))

    return pl.pallas_call(
        body, name="grad_w_out", out_shape=jax.ShapeDtypeStruct((2, W, D), F32), grid=(2, W // tmm, S // tk),
        in_specs=[pl.BlockSpec((tk, tmm), lambda j, m, k: (k, m)),
                  pl.BlockSpec((tk, tmm), lambda j, m, k: (k, m)),
                  pl.BlockSpec((tk, D), lambda j, m, k: (k, 0))],
        out_specs=pl.BlockSpec((None, tmm, D), lambda j, m, k: (j, m, 0)),
        compiler_params=_params("parallel", "parallel", "arbitrary"),
    )(mix_r, mix_s, dx2b)


def _dh_norm_bwd(dpr, dps, w_all, x, dx2, gain, chip_sums):
    _, S, W = dpr.shape
    D = x.shape[1]
    tm = min(512, S)
    ni = S // tm

    def body(r_ref, s_ref, w_ref, x_ref, dx2_ref, g_ref, sa_ref, sb_ref, gx_ref, dgain_ref, ra_ref, rb_ref,
             acc_ref, send_sems, recv_sems):
        i, j = pl.program_id(0), pl.program_id(1)
        start_exchange, wait_exchange = _exchange_chip_sums((sa_ref, sb_ref), (ra_ref, rb_ref), send_sems, recv_sems)

        @pl.when((i == 0) & (j == 0))
        def _():
            start_exchange()
            dgain_ref[...] = jnp.zeros_like(dgain_ref)

        def acc(b_ref):
            part = _dot_nt(b_ref[...], w_ref[...])

            @pl.when(j == 0)
            def _():
                acc_ref[...] = part

            @pl.when(j > 0)
            def _():
                acc_ref[...] += part

        pl.when(j < 4)(lambda: acc(r_ref))
        pl.when(j >= 4)(lambda: acc(s_ref))

        @pl.when(j == 7)
        def _():
            xv, dh, gv = x_ref[...], acc_ref[...], g_ref[...]
            r1 = lax.rsqrt(jnp.mean(xv * xv, axis=-1, keepdims=True) + EPS)
            n = xv * r1
            dgain_ref[...] += jnp.sum(dh * n, axis=0, keepdims=True)
            dn = dh * gv
            gx_ref[...] = dx2_ref[...] + r1 * (dn - n * jnp.mean(dn * n, axis=-1, keepdims=True))

        pl.when((i == ni - 1) & (j == 7))(wait_exchange)

    row = pl.BlockSpec((tm, D), lambda i, j: (i, 0))
    one = pl.BlockSpec((1, D), lambda i, j: (0, 0))
    hbm = pl.BlockSpec(memory_space=pltpu.HBM)
    return pl.pallas_call(
        body, name="dh_norm_bwd",
        out_shape=(jax.ShapeDtypeStruct((S, D), F32), jax.ShapeDtypeStruct((1, D), F32))
        + tuple(jax.ShapeDtypeStruct((3,) + s.shape[1:], s.dtype) for s in chip_sums),
        grid=(ni, 8),
        in_specs=[pl.BlockSpec((None, tm, W), lambda i, j: (jnp.minimum(j, 3), i, 0)),
                  pl.BlockSpec((None, tm, W), lambda i, j: (jnp.maximum(j - 4, 0), i, 0)),
                  pl.BlockSpec((None, D, W), lambda i, j: (j, 0, 0)), row, row, one, hbm, hbm],
        out_specs=(row, one, hbm, hbm),
        scratch_shapes=[pltpu.VMEM((tm, D), F32), pltpu.SemaphoreType.DMA((2, 3)), pltpu.SemaphoreType.DMA((2, 3))],
        compiler_params=_params("arbitrary", "arbitrary"),
    )(dpr, dps, w_all, x, dx2, gain, *chip_sums)


def _own_block(gw, pos, q):
    return q if gw.shape[0] == 4 else 2 * q + pos[0]


def _rs_local_sum(gw, rin, pos):
    _, R, C = gw.shape
    tr = min(256, R)

    def body(pos_ref, a_ref, b_ref, o_ref):
        o_ref[...] = (a_ref[...] + b_ref[...]).astype(BF16)

    return pl.pallas_call(
        body, name="rs_local_sum", out_shape=jax.ShapeDtypeStruct((4, R, C), BF16),
        grid_spec=pltpu.PrefetchScalarGridSpec(
            num_scalar_prefetch=1, grid=(4, R // tr),
            in_specs=[pl.BlockSpec((None, tr, C), lambda q, i, pos: (_own_block(gw, pos, q), i, 0)),
                      pl.BlockSpec((None, tr, C), lambda q, i, pos: (q, i, 0))],
            out_specs=pl.BlockSpec((None, tr, C), lambda q, i, pos: (q, i, 0))),
        compiler_params=_params("parallel", "parallel"),
    )(pos, gw, rin)


def _adamw(w, g, m, v):
    m2 = ADAM_B1 * m + (1.0 - ADAM_B1) * g
    v2 = ADAM_B2 * v + (1.0 - ADAM_B2) * (g * g)
    m_hat = m2 / (1.0 - ADAM_B1 ** ADAM_STEP)
    v_hat = v2 / (1.0 - ADAM_B2 ** ADAM_STEP)
    delta = -ADAM_LR * (m_hat / (jnp.sqrt(v_hat) + ADAM_EPS) + ADAM_WD * w)
    return delta, m2, v2


def _adamw_shard(gw, rin, rb, w, m, v, pos):
    _, R, C = gw.shape
    tr = min(256, R)

    def body(pos_ref, a_ref, b_ref, rb_ref, w_ref, m_ref, v_ref, g_ref, d_ref, m2_ref, v2_ref):
        g = a_ref[...] + b_ref[...]
        for k in range(3):
            g = g + rb_ref[k].astype(F32)
        g_ref[...] = g
        d_ref[...], m2_ref[...], v2_ref[...] = _adamw(w_ref[...], g, m_ref[...], v_ref[...])

    plain = pl.BlockSpec((tr, C), lambda i, pos: (i, 0))
    shape = jax.ShapeDtypeStruct((R, C), F32)
    return pl.pallas_call(
        body, name="adamw_shard", out_shape=(shape,) * 4,
        grid_spec=pltpu.PrefetchScalarGridSpec(
            num_scalar_prefetch=1, grid=(R // tr,),
            in_specs=[pl.BlockSpec((None, tr, C), lambda i, pos: (_own_block(gw, pos, pos[1]), i, 0)),
                      pl.BlockSpec((None, tr, C), lambda i, pos: (pos[1], i, 0)),
                      pl.BlockSpec((3, tr, C), lambda i, pos: (0, i, 0)), plain, plain, plain],
            out_specs=(plain,) * 4),
        compiler_params=_params("parallel"),
    )(pos, gw, rin, rb, w, m, v)


def _adamw_small(parts, w, m, v):
    _, rows, n = parts.shape

    def body(p_ref, w_ref, m_ref, v_ref, g_ref, d_ref, m2_ref, v2_ref):
        g = p_ref[0]
        for d in range(1, N_DEV):
            g = g + p_ref[d]
        g_ref[...] = g
        d_ref[...], m2_ref[...], v2_ref[...] = _adamw(w_ref[...], g, m_ref[...], v_ref[...])

    shape = jax.ShapeDtypeStruct((rows, n), F32)
    return pl.pallas_call(body, name="adamw_small", out_shape=(shape,) * 4)(parts, w, m, v)


def _rope_tables(S):
    half = HEAD_DIM // 2
    inv = ROPE_THETA ** (-jnp.arange(half, dtype=F32) / half)
    ang = jnp.arange(S, dtype=F32)[:, None] * inv[None, :]
    cos, sin = jnp.cos(ang), jnp.sin(ang)
    return jnp.concatenate([cos, cos], axis=1), jnp.concatenate([-sin, sin], axis=1)


def _retention_tables(H):
    lg = jnp.log1p(-jnp.exp2(-5.0 - jnp.arange(H, dtype=F32)))
    n = jnp.arange(CHUNK, dtype=F32)
    rel = n[:, None] - n[None, :]
    decay = jnp.where(rel >= 0, jnp.exp(lg[:, None, None] * jnp.maximum(rel, 0.0)), 0.0)
    shape = (H, CHUNK, HEAD_DIM)
    xi = jnp.broadcast_to(jnp.exp(lg[:, None] * (n + 1.0))[:, :, None], shape)
    zeta = jnp.broadcast_to(jnp.exp(lg[:, None] * (CHUNK - 1.0 - n))[:, :, None], shape)
    gamma_c = jnp.broadcast_to(jnp.exp(lg * CHUNK)[:, None, None], shape)
    return decay, xi, zeta, gamma_c


def _pack_small(parts):
    flat = []
    for p in parts:
        p = p.reshape(-1)
        flat.append(jnp.pad(p, (0, -p.shape[0] % LANES)))
    flat = jnp.concatenate(flat)
    return jnp.pad(flat, (0, SMALL_N - flat.shape[0])).reshape(SUBLANES, SMALL_N // SUBLANES)


def _unpack_small(packed, shapes):
    flat = packed.reshape(-1)
    out, at = [], 0
    for shp in shapes:
        size = 1
        for s in shp:
            size *= s
        out.append(flat[at:at + size].reshape(shp))
        at += size + (-size % LANES)
    return out


def kernel(x, norm_gain, w_in, ret_gn_gain, ret_gn_bias, sb_norm_gain, w_out, final_norm_gain, loss_target, m_norm_gain, m_w_in, m_ret_gn_gain, m_ret_gn_bias, m_sb_norm_gain, m_w_out, m_final_norm_gain, v_norm_gain, v_w_in, v_ret_gn_gain, v_ret_gn_bias, v_sb_norm_gain, v_w_out, v_final_norm_gain):
    S, D = x.shape[1], x.shape[2]
    W = w_in.shape[2]
    wo_rows = w_out.shape[1]
    H = W // HEAD_DIM
    xs, tgt = x[0], loss_target[0]
    mx, my, mc = _mesh_pos()
    pos = jnp.stack([mc, 2 * mx + my]).astype(jnp.int32)

    cos, sin = _rope_tables(S)
    tabs = _retention_tables(H)

    h, ht = _rmsnorm_fwd(xs, norm_gain)
    proj, w_all = _in_proj_gather(h, w_in[0].astype(BF16), cos, sin, _gather_order())
    mix_r = _ret_fwd(proj, tabs, ret_gn_gain, ret_gn_bias)
    mix_s, raw_s, carries, wo_all = _sb_fwd(proj, sb_norm_gain, w_out[0].astype(BF16))
    wo_full = wo_all.reshape(N_DEV * wo_rows, D)
    dx2, dx2b, dmix, loss_p, d_gf = _out_proj_loss(mix_r, mix_s, wo_full, xs, tgt, final_norm_gain[None])

    dpr, d_rgain, d_rbias = _ret_bwd(proj, dmix, tabs, ret_gn_gain, ret_gn_bias, cos, sin)
    dps, d_sgain = _sb_bwd(proj, raw_s, carries, dmix, sb_norm_gain)
    gwo = _grad_w_out(mix_r, mix_s, dx2b).reshape(N_DEV, wo_rows, D)
    gw_sibling, = _grad_w_in_half(ht, dpr, dps, (1 - mc).reshape(1).astype(jnp.int32), "grad_w_in_sibling")
    gw, rin, rino = _grad_w_in_half(ht, dpr, dps, mc.reshape(1).astype(jnp.int32), "grad_w_in_own",
                                    to_sibling=(gw_sibling, gwo))
    chip_sums = (_rs_local_sum(gw, rin, pos), _rs_local_sum(gwo, rino, pos))
    grad_x, d_gain, rb, rbo = _dh_norm_bwd(dpr, dps, w_all, xs, dx2, norm_gain, chip_sums)
    g_in, d_in, m_in, v_in = _adamw_shard(gw, rin, rb, w_in[0], m_w_in[0], v_w_in[0], pos)
    g_out, d_out, m_out, v_out = _adamw_shard(gwo, rino, rbo, w_out[0], m_w_out[0], v_w_out[0], pos)

    small_w = [norm_gain, ret_gn_gain, ret_gn_bias, sb_norm_gain, final_norm_gain]
    small_m = [m_norm_gain, m_ret_gn_gain, m_ret_gn_bias, m_sb_norm_gain, m_final_norm_gain]
    small_v = [v_norm_gain, v_ret_gn_gain, v_ret_gn_bias, v_sb_norm_gain, v_final_norm_gain]
    shapes = [()] + [w.shape for w in small_w]
    zero = jnp.zeros((), F32)
    parts = _small_all_gather(_pack_small([loss_p[0, 0], d_gain, d_rgain, d_rbias, d_sgain, d_gf]))
    packed = _adamw_small(parts, _pack_small([zero] + small_w), _pack_small([zero] + small_m),
                          _pack_small([zero] + small_v))
    g_s, d_s, m_s, v_s = (_unpack_small(p, shapes) for p in packed)

    grads = [g_s[1], g_in[None], g_s[2], g_s[3], g_s[4], g_out[None], g_s[5]]
    deltas = [d_s[1], d_in[None], d_s[2], d_s[3], d_s[4], d_out[None], d_s[5]]
    new_m = [m_s[1], m_in[None], m_s[2], m_s[3], m_s[4], m_out[None], m_s[5]]
    new_v = [v_s[1], v_in[None], v_s[2], v_s[3], v_s[4], v_out[None], v_s[5]]
    return (g_s[0], grad_x[None], *grads, *deltas, *new_m, *new_v)
```

```python
import functools

import jax
import jax.numpy as jnp
from jax import lax
from jax.experimental import pallas as pl
from jax.experimental.pallas import tpu as pltpu

F32 = jnp.float32
BF16 = jnp.bfloat16

HEAD_DIM = 128
CHUNK = 128
RET_UNROLL = 4
ROPE_THETA = 10000.0
EPS = 1e-6
ADAM_LR = 0.001
ADAM_B1 = 0.9
ADAM_B2 = 0.999
ADAM_EPS = 1e-08
ADAM_WD = 0.01
ADAM_STEP = 10

N_DEV = 8
LANES = 128
SUBLANES = 8
VMEM_LIMIT = 56 * 1024 * 1024
SB_BLOCK = 256
SMALL_N = 8192
EXP_IS_ZERO_BELOW = -104.0
NOT_VISITED = -1e30
MESH = pl.DeviceIdType.MESH

NT = (((1,), (1,)), ((), ()))
TN = (((0,), (0,)), ((), ()))


def _params(*sem):
    return pltpu.CompilerParams(dimension_semantics=sem if sem else None, vmem_limit_bytes=VMEM_LIMIT)


def _dot(a, b):
    return jnp.dot(a, b, preferred_element_type=F32)


def _dot_nt(a, b):
    return lax.dot_general(a, b, NT, preferred_element_type=F32)


def _dot_tn(a, b):
    return lax.dot_general(a, b, TN, preferred_element_type=F32)


def _sigmoid(g):
    return 1.0 / (1.0 + jnp.exp(-g))


def _rot(a, cos, sin_signed):
    return a * cos + pltpu.roll(a, HEAD_DIM // 2, 1) * sin_signed


def _mesh_pos():
    return lax.axis_index("x"), lax.axis_index("y"), lax.axis_index("c")


def _exchange_chip_sums(srcs, outs, send_sems, recv_sems):
    x, y, c = _mesh_pos()
    copies = []
    for arr, (src, out) in enumerate(zip(srcs, outs)):
        for k in range(1, 4):
            px = 1 - x if k & 2 else x
            py = 1 - y if k & 1 else y
            copies.append(pltpu.make_async_remote_copy(
                src_ref=src.at[2 * px + py], dst_ref=out.at[k - 1],
                send_sem=send_sems.at[arr, k - 1], recv_sem=recv_sems.at[arr, k - 1],
                device_id=(px, py, c), device_id_type=MESH))

    def start():
        for cp in copies:
            cp.start()

    def wait():
        for cp in copies:
            cp.wait_recv()
        for cp in copies:
            cp.wait_send()

    return start, wait


def _small_all_gather(small):
    rows, n = small.shape

    def body(s_ref, o_ref, send_sems, recv_sems, local_sem):
        start, wait = _exchange_with_all(s_ref, o_ref, send_sems, recv_sems, local_sem)
        start()
        wait()

    vmem = pl.BlockSpec(memory_space=pltpu.VMEM)
    return pl.pallas_call(
        body, name="small_all_gather",
        out_shape=jax.ShapeDtypeStruct((N_DEV, rows, n), small.dtype),
        in_specs=[vmem], out_specs=vmem,
        scratch_shapes=[pltpu.SemaphoreType.DMA((N_DEV - 1,)), pltpu.SemaphoreType.DMA((N_DEV - 1,)),
                        pltpu.SemaphoreType.DMA],
    )(small)


def _rmsnorm_fwd(x, gain):
    S, D = x.shape
    tm = min(512, S)

    def body(x_ref, g_ref, h_ref, ht_ref):
        xv = x_ref[...]
        r = lax.rsqrt(jnp.mean(xv * xv, axis=-1, keepdims=True) + EPS)
        hv = xv * r * g_ref[...]
        h_ref[...] = hv.astype(BF16)
        ht_ref[...] = hv.T.astype(BF16)

    return pl.pallas_call(
        body, name="rmsnorm_fwd",
        out_shape=(jax.ShapeDtypeStruct((S, D), BF16), jax.ShapeDtypeStruct((D, S), BF16)), grid=(S // tm,),
        in_specs=[pl.BlockSpec((tm, D), lambda i: (i, 0)), pl.BlockSpec((1, D), lambda i: (0, 0))],
        out_specs=(pl.BlockSpec((tm, D), lambda i: (i, 0)), pl.BlockSpec((D, tm), lambda i: (0, i))),
        compiler_params=_params("parallel"),
    )(x, gain)


def _gather_order():
    x, y, c = _mesh_pos()
    chips = [(1 - x, y), (x, 1 - y), (1 - x, 1 - y)]
    devs = [(x, y, c), (x, y, 1 - c)] + [(*chip, c) for chip in chips] + [(*chip, 1 - c) for chip in chips]
    return jnp.stack([4 * px + 2 * py + pc for px, py, pc in devs]).astype(jnp.int32)


def _in_proj_gather(h, w_shard, cos, sin, order):
    S, D = h.shape
    W = w_shard.shape[1]
    tm = min(512, S)
    ni = S // tm

    def body(order_ref, h_ref, w_ref, cos_ref, sin_ref, o_ref, wall_ref, wbuf, send_sems, recv_sems, local_sem,
             load_sem):
        step, i = pl.program_id(0), pl.program_id(1)
        x, y, c = _mesh_pos()
        me, sibling = (x, y, c), (x, y, 1 - c)
        chips = [(1 - x, y), (x, 1 - y), (1 - x, 1 - y)]

        def slot(dev):
            px, py, pc = dev
            return wall_ref.at[4 * px + 2 * py + pc]

        def copy(k, block, to, src=None):
            dst = slot(block)
            return pltpu.make_async_remote_copy(
                src_ref=dst if src is None else src, dst_ref=dst, send_sem=send_sems.at[k], recv_sem=recv_sems.at[k],
                device_id=to, device_id_type=MESH)

        def load(src):
            cp = pltpu.make_async_copy(src, wbuf, load_sem)
            cp.start()
            cp.wait()

        first = [copy(0, me, sibling, src=w_ref)] + [copy(1 + j, me, (*chip, c), src=w_ref)
                                                     for j, chip in enumerate(chips)]
        passed = [copy(4 + j, (*chip, c), sibling) for j, chip in enumerate(chips)]
        mine = pltpu.make_async_copy(w_ref, slot(me), local_sem)

        @pl.when(i == 0)
        def _():
            @pl.when(step == 0)
            def _():
                for cp in first:
                    cp.start()
                mine.start()
                load(w_ref)

            @pl.when(step == 1)
            def _():
                copy(0, sibling, me).wait_recv()
                load(slot(sibling))

            for j, chip in enumerate(chips):
                @pl.when(step == 2 + j)
                def _(j=j, chip=chip):
                    copy(1 + j, (*chip, c), me).wait_recv()
                    passed[j].start()
                    load(slot((*chip, c)))

                @pl.when(step == 5 + j)
                def _(j=j, chip=chip):
                    copy(4 + j, (*chip, 1 - c), me).wait_recv()
                    load(slot((*chip, 1 - c)))

        acc = _dot(h_ref[...], wbuf[...])
        b = order_ref[step]

        @pl.when(b >= 2)
        def _():
            o_ref[...] = acc

        @pl.when(b < 2)
        def _():
            scale = jnp.where(b == 1, HEAD_DIM ** -0.5, 1.0).astype(F32)
            cs, sn = cos_ref[...], sin_ref[...]
            for hh in range(W // HEAD_DIM):
                cols = slice(hh * HEAD_DIM, (hh + 1) * HEAD_DIM)
                o_ref[:, cols] = _rot(acc[:, cols], cs, sn) * scale

        @pl.when((step == N_DEV - 1) & (i == ni - 1))
        def _():
            for cp in first + passed:
                cp.wait_send()
            mine.wait()

    hbm = pl.BlockSpec(memory_space=pltpu.HBM)
    rope = pl.BlockSpec((tm, HEAD_DIM), lambda s, i, order: (i, 0))
    return pl.pallas_call(
        body, name="in_proj_gather",
        out_shape=(jax.ShapeDtypeStruct((N_DEV, S, W), F32), jax.ShapeDtypeStruct((N_DEV, D, W), BF16)),
        grid_spec=pltpu.PrefetchScalarGridSpec(
            num_scalar_prefetch=1, grid=(N_DEV, ni),
            in_specs=[pl.BlockSpec((tm, D), lambda s, i, order: (i, 0)), hbm, rope, rope],
            out_specs=(pl.BlockSpec((None, tm, W), lambda s, i, order: (order[s], i, 0)), hbm),
            scratch_shapes=[pltpu.VMEM((D, W), BF16), pltpu.SemaphoreType.DMA((7,)), pltpu.SemaphoreType.DMA((7,)),
                            pltpu.SemaphoreType.DMA, pltpu.SemaphoreType.DMA]),
        compiler_params=_params("arbitrary", "arbitrary"),
    )(order, h, w_shard, cos, sin)


def _head_spec(S, j):
    return pl.BlockSpec((None, S, HEAD_DIM), lambda h, *_: (j, 0, h))


def _ret_chunk(q, k, vb, r_prev, dec, xi, ze):
    qb, kb = q.astype(BF16), k.astype(BF16)
    sb = (_dot_nt(qb, kb) * dec).astype(BF16)
    qx = (q * xi).astype(BF16)
    kz = (k * ze).astype(BF16)
    out = _dot(sb, vb) + _dot(qx, r_prev.astype(BF16))
    return out, _dot_tn(kz, vb), (qb, kb, sb, qx, kz)


def _chunk_loop(nc, step, init, unroll=RET_UNROLL):
    def trip(i, state):
        for u in range(unroll):
            state = step(i * unroll + u, state)
        return state

    assert nc % unroll == 0
    return lax.fori_loop(0, nc // unroll, trip, init)


def _table_specs():
    return [pl.BlockSpec((None, CHUNK, HEAD_DIM), lambda h, *_: (h, 0, 0))] * 4


def _ret_fwd(proj, tabs, gn_gain, gn_bias):
    _, S, W = proj.shape
    H, nc = W // HEAD_DIM, S // CHUNK

    def body(q_ref, k_ref, v_ref, g_ref, dec_ref, xi_ref, ze_ref, gam_ref, gain_ref, bias_ref, o_ref):
        dec, xi, ze, gam = dec_ref[...], xi_ref[...], ze_ref[...], gam_ref[...]
        gain, bias = gain_ref[...], bias_ref[...]

        def step(c, state):
            rows = pl.ds(pl.multiple_of(c * CHUNK, CHUNK), CHUNK)
            out, kv, _ = _ret_chunk(q_ref[rows, :], k_ref[rows, :], v_ref[rows, :].astype(BF16), state,
                                    dec, xi, ze)
            mu = jnp.mean(out, axis=-1, keepdims=True)
            d = out - mu
            yn = d * lax.rsqrt(jnp.mean(d * d, axis=-1, keepdims=True) + EPS)
            g = g_ref[rows, :]
            o_ref[rows, :] = (g * _sigmoid(g) * (yn * gain + bias)).astype(BF16)
            return gam * state + kv

        _chunk_loop(nc, step, jnp.zeros((HEAD_DIM, HEAD_DIM), F32))

    vec = pl.BlockSpec((1, HEAD_DIM), lambda h: (0, h))
    return pl.pallas_call(
        body, name="ret_fwd", out_shape=jax.ShapeDtypeStruct((S, W), BF16), grid=(H,),
        in_specs=[_head_spec(S, 0), _head_spec(S, 1), _head_spec(S, 2), _head_spec(S, 3)] + _table_specs() + [vec, vec],
        out_specs=pl.BlockSpec((S, HEAD_DIM), lambda h: (0, h)),
        compiler_params=_params("parallel"),
    )(proj, proj, proj, proj, *tabs, gn_gain, gn_bias)


def _sb_scores(qb, kk, masked, causal, upper):
    z = _dot_nt(qb, kk) * (HEAD_DIM ** -0.5)
    e = jnp.exp(-jnp.abs(z))
    l1p = jnp.log1p(e)
    log_beta = jnp.minimum(z, 0.0) - l1p
    lk = jnp.minimum(-z, 0.0) - l1p
    if masked:
        lk = jnp.where(causal, lk, 0.0)
    hi = lk.astype(BF16)
    lo = (lk - hi.astype(F32)).astype(BF16)
    cs = _dot(hi, upper) + _dot(lo, upper)
    return z, e, log_beta, lk, cs


def _tri(B, kind):
    r = lax.broadcasted_iota(jnp.int32, (B, B), 0)
    c = lax.broadcasted_iota(jnp.int32, (B, B), 1)
    return {"gt": r > c, "lt": r < c}[kind]


def _ones_where(mask):
    return jnp.where(mask, 1.0, 0.0).astype(BF16)


def _exchange_with_all(src_ref, out_ref, send_sems, recv_sems, local_sem):
    x, y, c = _mesh_pos()
    peers = [(1 - x if k & 4 else x, 1 - y if k & 2 else y, 1 - c if k & 1 else c) for k in range(1, N_DEV)]

    def copy(k, owner, to):
        px, py, pc = owner
        return pltpu.make_async_remote_copy(
            src_ref=src_ref, dst_ref=out_ref.at[4 * px + 2 * py + pc], send_sem=send_sems.at[k],
            recv_sem=recv_sems.at[k], device_id=to, device_id_type=MESH)

    sends = [copy(k, (x, y, c), p) for k, p in enumerate(peers)]
    mine = pltpu.make_async_copy(src_ref, out_ref.at[4 * x + 2 * y + c], local_sem)

    def start():
        for cp in sends:
            cp.start()
        mine.start()

    def wait():
        for k, p in enumerate(peers):
            copy(k, p, p).wait_recv()
        for cp in sends:
            cp.wait_send()
        mine.wait()

    return start, wait


def _sb_fwd(proj, gain, wo_shard):
    _, S, W = proj.shape
    H = W // HEAD_DIM
    B = min(SB_BLOCK, S)
    nq = S // B
    assert nq <= HEAD_DIM

    def body(q_ref, k_ref, v_ref, g_ref, gain_ref, wo_ref, mix_ref, raw_ref, car_ref, woall_ref, kb_ref, vb_ref,
             send_sems, recv_sems, local_sem):
        hd, qi = pl.program_id(0), pl.program_id(1)
        start_gather, wait_gather = _exchange_with_all(wo_ref, woall_ref, send_sems, recv_sems, local_sem)
        pl.when((hd == 0) & (qi == 0))(start_gather)

        @pl.when(qi == 0)
        def _():
            kb_ref[...] = k_ref[...].astype(BF16)
            vb_ref[...] = v_ref[...].astype(BF16)

        qb = q_ref[...].astype(BF16)
        causal = _tri(B, "gt")
        upper = _ones_where(causal)
        lane = lax.broadcasted_iota(jnp.int32, (B, HEAD_DIM), 1)

        def block(kb, carry, acc, saved, masked):
            rows = pl.ds(pl.multiple_of(kb * B, B), B)
            _, _, log_beta, lk, cs = _sb_scores(qb, kb_ref[rows, :], masked, causal, upper)
            a = jnp.exp(log_beta + cs + carry)
            if masked:
                a = jnp.where(causal, a, 0.0)
            acc = acc + _dot(a.astype(BF16), vb_ref[rows, :])
            return carry + jnp.sum(lk, axis=1, keepdims=True), acc, jnp.where(lane == kb, carry, saved)

        init = (jnp.zeros((B, 1), F32), jnp.zeros((B, HEAD_DIM), F32), jnp.full((B, HEAD_DIM), NOT_VISITED, F32))

        def live(st):
            return (st[0] >= 0) & (jnp.max(st[1]) >= EXP_IS_ZERO_BELOW)

        def step(st):
            return (st[0] - 1,) + block(st[0], st[1], st[2], st[3], False)

        def finish(acc, saved):
            raw_ref[...] = acc
            car_ref[...] = saved
            yn = acc * lax.rsqrt(jnp.mean(acc * acc, axis=-1, keepdims=True) + EPS)
            g = g_ref[...]
            mix_ref[...] = (g * _sigmoid(g) * (yn * gain_ref[...])).astype(BF16)

        @pl.when(qi == 0)
        def _():
            _, acc, saved = block(qi, *init, True)
            finish(acc, saved)

        @pl.when(qi > 0)
        def _():
            state = block(qi - 1, *block(qi, *init, True), False)
            _, _, acc, saved = lax.while_loop(live, step, (qi - 2,) + state)
            finish(acc, saved)

        pl.when((hd == H - 1) & (qi == nq - 1))(wait_gather)

    tile = lambda j: pl.BlockSpec((None, B, HEAD_DIM), lambda h, i: (j, i, h))
    out_tile = pl.BlockSpec((B, HEAD_DIM), lambda h, i: (i, h))
    hbm = pl.BlockSpec(memory_space=pltpu.HBM)
    return pl.pallas_call(
        body, name="sb_fwd",
        out_shape=(jax.ShapeDtypeStruct((S, W), BF16), jax.ShapeDtypeStruct((S, W), F32),
                   jax.ShapeDtypeStruct((S, W), F32), jax.ShapeDtypeStruct((N_DEV,) + wo_shard.shape, BF16)),
        grid=(H, nq),
        in_specs=[tile(4), _head_spec(S, 5), _head_spec(S, 6), tile(7),
                  pl.BlockSpec((1, HEAD_DIM), lambda h, i: (0, h)), hbm],
        out_specs=(out_tile, out_tile, out_tile, hbm),
        scratch_shapes=[pltpu.VMEM((S, HEAD_DIM), BF16), pltpu.VMEM((S, HEAD_DIM), BF16),
                        pltpu.SemaphoreType.DMA((N_DEV - 1,)), pltpu.SemaphoreType.DMA((N_DEV - 1,)),
                        pltpu.SemaphoreType.DMA],
        compiler_params=_params("arbitrary", "arbitrary"),
    )(proj, proj, proj, proj, gain, wo_shard)


def _out_proj_loss(mix_r, mix_s, w_out, x, tgt, gf):
    S, W = mix_r.shape
    D = x.shape[1]
    tm = min(256, S)

    def body(mr_ref, ms_ref, wo_ref, x_ref, t_ref, gf_ref, dx2_ref, dx2b_ref, dmix_ref, loss_ref, gfn_ref):
        @pl.when(pl.program_id(0) == 0)
        def _():
            loss_ref[...] = jnp.zeros_like(loss_ref)
            gfn_ref[...] = jnp.zeros_like(gfn_ref)

        gfv = gf_ref[...]
        x2 = x_ref[...] + (_dot(mr_ref[...], wo_ref[:W, :]) + _dot(ms_ref[...], wo_ref[W:, :]))
        r2 = lax.rsqrt(jnp.mean(x2 * x2, axis=-1, keepdims=True) + EPS)
        n = x2 * r2
        err = n * gfv - t_ref[...]
        loss_ref[...] += 0.5 * jnp.sum(jnp.mean(err * err, axis=-1, keepdims=True))
        dy = err * (1.0 / D)
        gfn_ref[...] += jnp.sum(dy * n, axis=0, keepdims=True)
        dn = dy * gfv
        dx2 = r2 * (dn - n * jnp.mean(dn * n, axis=-1, keepdims=True))
        dx2_ref[...] = dx2
        b = dx2.astype(BF16)
        dx2b_ref[...] = b
        dmix_ref[:, :W] = _dot_nt(b, wo_ref[:W, :])
        dmix_ref[:, W:] = _dot_nt(b, wo_ref[W:, :])

    row = lambda width: pl.BlockSpec((tm, width), lambda i: (i, 0))
    return pl.pallas_call(
        body, name="out_proj_loss",
        out_shape=(jax.ShapeDtypeStruct((S, D), F32), jax.ShapeDtypeStruct((S, D), BF16),
                   jax.ShapeDtypeStruct((S, 2 * W), F32), jax.ShapeDtypeStruct((SUBLANES, LANES), F32),
                   jax.ShapeDtypeStruct((1, D), F32)),
        grid=(S // tm,),
        in_specs=[row(W), row(W), pl.BlockSpec((2 * W, D), lambda i: (0, 0)), row(D), row(D),
                  pl.BlockSpec((1, D), lambda i: (0, 0))],
        out_specs=(row(D), row(D), row(2 * W), pl.BlockSpec((SUBLANES, LANES), lambda i: (0, 0)),
                   pl.BlockSpec((1, D), lambda i: (0, 0))),
        compiler_params=_params("arbitrary"),
    )(mix_r, mix_s, w_out, x, tgt, gf)


def _silu_bwd(g, dm, normed):
    sig = _sigmoid(g)
    return dm * (g * sig), dm * normed * (sig * (1.0 + g * (1.0 - sig)))


def _ret_bwd(proj, dmix, tabs, gn_gain, gn_bias, cos, sin):
    _, S, W = proj.shape
    H, nc = W // HEAD_DIM, S // CHUNK

    def body(q_ref, k_ref, v_ref, g_ref, dm_ref, dec_ref, xi_ref, ze_ref, gam_ref, gain_ref, bias_ref, cos_ref,
             sin_ref, dp_ref, dgain_ref, dbias_ref, rs_ref):
        dec, xi, ze, gam = dec_ref[...], xi_ref[...], ze_ref[...], gam_ref[...]
        gain, bias = gain_ref[...], bias_ref[...]

        def fwd_step(c, state):
            rows = pl.ds(pl.multiple_of(c * CHUNK, CHUNK), CHUNK)
            rs_ref[c] = state
            kz = (k_ref[rows, :] * ze).astype(BF16)
            return gam * state + _dot_tn(kz, v_ref[rows, :].astype(BF16))

        _chunk_loop(nc, fwd_step, jnp.zeros((HEAD_DIM, HEAD_DIM), F32))

        def bwd_step(i, carry):
            dgain, dbias, dstate = carry
            c = nc - 1 - i
            rows = pl.ds(pl.multiple_of(c * CHUNK, CHUNK), CHUNK)
            q, k, g = q_ref[rows, :], k_ref[rows, :], g_ref[rows, :]
            vb = v_ref[rows, :].astype(BF16)
            rb = rs_ref[c].astype(BF16)
            out, _, (qb, kb, sb, qx, kz) = _ret_chunk(q, k, vb, rs_ref[c], dec, xi, ze)
            mu = jnp.mean(out, axis=-1, keepdims=True)
            d = out - mu
            rstd = lax.rsqrt(jnp.mean(d * d, axis=-1, keepdims=True) + EPS)
            yn = d * rstd
            dgn, dg = _silu_bwd(g, dm_ref[rows, :], yn * gain + bias)
            dgain = dgain + jnp.sum(dgn * yn, axis=0, keepdims=True)
            dbias = dbias + jnp.sum(dgn, axis=0, keepdims=True)
            dyn = dgn * gain
            do = rstd * (dyn - jnp.mean(dyn, axis=-1, keepdims=True)
                         - yn * jnp.mean(dyn * yn, axis=-1, keepdims=True))
            dob = do.astype(BF16)
            drb = dstate.astype(BF16)
            dv = _dot_tn(sb, dob) + _dot(kz, drb)
            dsb = (_dot_nt(dob, vb) * dec).astype(BF16)
            dq = _dot(dsb, kb) + _dot_nt(dob, rb) * xi
            dk = _dot_tn(dsb, qb) + _dot_nt(vb, drb) * ze
            cs, sn = cos_ref[rows, :], -sin_ref[rows, :]
            dp_ref[0, rows, :] = _rot(dq, cs, sn).astype(BF16)
            dp_ref[1, rows, :] = (_rot(dk, cs, sn) * (HEAD_DIM ** -0.5)).astype(BF16)
            dp_ref[2, rows, :] = dv.astype(BF16)
            dp_ref[3, rows, :] = dg.astype(BF16)
            return dgain, dbias, gam * dstate + _dot_tn(qx, dob)

        zero = jnp.zeros((1, HEAD_DIM), F32)
        dgain, dbias, _ = _chunk_loop(nc, bwd_step, (zero, zero, jnp.zeros((HEAD_DIM, HEAD_DIM), F32)))
        dgain_ref[...] = dgain
        dbias_ref[...] = dbias

    vec = pl.BlockSpec((1, HEAD_DIM), lambda h: (0, h))
    full = pl.BlockSpec((S, HEAD_DIM), lambda h: (0, 0))
    return pl.pallas_call(
        body, name="ret_bwd",
        out_shape=(jax.ShapeDtypeStruct((4, S, W), BF16), jax.ShapeDtypeStruct((1, W), F32),
                   jax.ShapeDtypeStruct((1, W), F32)),
        grid=(H,),
        in_specs=[_head_spec(S, 0), _head_spec(S, 1), _head_spec(S, 2), _head_spec(S, 3),
                  pl.BlockSpec((S, HEAD_DIM), lambda h: (0, h))] + _table_specs() + [vec, vec, full, full],
        out_specs=(pl.BlockSpec((4, S, HEAD_DIM), lambda h: (0, 0, h)), vec, vec),
        scratch_shapes=[pltpu.VMEM((nc, HEAD_DIM, HEAD_DIM), F32)],
        compiler_params=_params("parallel"),
    )(proj, proj, proj, proj, dmix, *tabs, gn_gain, gn_bias, cos, sin)


def _sb_bwd(proj, raw, carries, dmix, gain):
    _, S, W = proj.shape
    H = W // HEAD_DIM
    B = min(SB_BLOCK, S)
    nq = S // B

    def body(q_ref, k_ref, v_ref, g_ref, raw_ref, car_ref, dm_ref, gain_ref, dp_ref, dgain_ref,
             kb_ref, vb_ref, dk_ref, dv_ref):
        qi = pl.program_id(1)

        @pl.when(qi == 0)
        def _():
            kb_ref[...] = k_ref[...].astype(BF16)
            vb_ref[...] = v_ref[...].astype(BF16)
            dk_ref[...] = jnp.zeros_like(dk_ref)
            dv_ref[...] = jnp.zeros_like(dv_ref)
            dgain_ref[...] = jnp.zeros_like(dgain_ref)

        q_rows = pl.ds(pl.multiple_of(qi * B, B), B)
        o = raw_ref[...]
        rstd = lax.rsqrt(jnp.mean(o * o, axis=-1, keepdims=True) + EPS)
        yn = o * rstd
        gain_v = gain_ref[...]
        dnrm, dg = _silu_bwd(g_ref[...], dm_ref[...], yn * gain_v)
        dp_ref[3, q_rows, :] = dg.astype(BF16)
        dgain_ref[...] += jnp.sum(dnrm * yn, axis=0, keepdims=True)
        dyn = dnrm * gain_v
        do = rstd * (dyn - yn * jnp.mean(dyn * yn, axis=-1, keepdims=True))
        dob = do.astype(BF16)
        qb = q_ref[...].astype(BF16)
        causal = _tri(B, "gt")
        upper = _ones_where(causal)
        before = _ones_where(_tri(B, "lt"))
        lane = lax.broadcasted_iota(jnp.int32, (B, HEAD_DIM), 1)
        saved = car_ref[...]

        def block(kb, carry_g, dq, masked):
            rows = pl.ds(pl.multiple_of(kb * B, B), B)
            kk, vv = kb_ref[rows, :], vb_ref[rows, :]
            z, e, log_beta, _, cs = _sb_scores(qb, kk, masked, causal, upper)
            carry_lk = jnp.sum(jnp.where(lane == kb, saved, 0.0), axis=1, keepdims=True)
            a = jnp.exp(log_beta + cs + carry_lk)
            if masked:
                a = jnp.where(causal, a, 0.0)
            gmat = _dot_nt(dob, vv) * a
            dv_ref[rows, :] += _dot_tn(a.astype(BF16), dob)
            hi = gmat.astype(BF16)
            lo = (gmat - hi.astype(F32)).astype(BF16)
            dlk = carry_g + (_dot(hi, before) + _dot(lo, before))
            r = 1.0 / (1.0 + e)
            er = e * r
            pos = z >= 0.0
            dz = (gmat * jnp.where(pos, er, r) - dlk * jnp.where(pos, r, er)) * (HEAD_DIM ** -0.5)
            if masked:
                dz = jnp.where(causal, dz, 0.0)
            dzb = dz.astype(BF16)
            dk_ref[rows, :] += _dot_tn(dzb, qb)
            return carry_g + jnp.sum(gmat, axis=1, keepdims=True), dq + _dot(dzb, kk)

        visited = jnp.max(saved, axis=0, keepdims=True) >= EXP_IS_ZERO_BELOW
        first = jnp.min(jnp.where(visited, lane[:1, :], qi))
        init = (jnp.zeros((B, 1), F32), jnp.zeros((B, HEAD_DIM), F32))

        @pl.when(qi == 0)
        def _():
            dp_ref[0, q_rows, :] = block(qi, *init, True)[1].astype(BF16)

        @pl.when(qi > 0)
        def _():
            state = lax.fori_loop(first, qi - 1, lambda i, st: block(i, st[0], st[1], False), init)
            state = block(qi, *block(qi - 1, *state, False), True)
            dp_ref[0, q_rows, :] = state[1].astype(BF16)

        @pl.when(qi == nq - 1)
        def _():
            dp_ref[1] = dk_ref[...].astype(BF16)
            dp_ref[2] = dv_ref[...].astype(BF16)

    tile = lambda j: pl.BlockSpec((None, B, HEAD_DIM), lambda h, i: (j, i, h))
    vec = pl.BlockSpec((1, HEAD_DIM), lambda h, i: (0, h))
    return pl.pallas_call(
        body, name="sb_bwd",
        out_shape=(jax.ShapeDtypeStruct((4, S, W), BF16), jax.ShapeDtypeStruct((1, W), F32)),
        grid=(H, nq),
        in_specs=[tile(4), _head_spec(S, 5), _head_spec(S, 6), tile(7),
                  pl.BlockSpec((B, HEAD_DIM), lambda h, i: (i, h)),
                  pl.BlockSpec((B, HEAD_DIM), lambda h, i: (i, h)),
                  pl.BlockSpec((B, HEAD_DIM), lambda h, i: (i, H + h)), vec],
        out_specs=(pl.BlockSpec((4, S, HEAD_DIM), lambda h, i: (0, 0, h)), vec),
        scratch_shapes=[pltpu.VMEM((S, HEAD_DIM), BF16), pltpu.VMEM((S, HEAD_DIM), BF16),
                        pltpu.VMEM((S, HEAD_DIM), F32), pltpu.VMEM((S, HEAD_DIM), F32)],
        compiler_params=_params("arbitrary", "arbitrary"),
    )(proj, proj, proj, proj, raw, carries, dmix, gain)


def _grad_w_in_half(ht, dpr, dps, core, name, to_sibling=None):
    D, S = ht.shape
    _, _, W = dpr.shape
    tmm = min(512, D)
    nm = D // tmm

    def body(core_ref, ht_ref, r_ref, s_ref, *rest):
        o_ref = rest[2] if to_sibling else rest[0]
        q, m = pl.program_id(0), pl.program_id(1)
        if to_sibling:
            ga_ref, gwo_ref, _, rin_ref, rino_ref, send_sems, recv_sems = rest
            x, y, c = _mesh_pos()
            copies = []
            for k in range(4):
                for arr, (src, dst) in enumerate(((ga_ref.at[k], rin_ref.at[k]),
                                                  (gwo_ref.at[2 * k + (1 - c)], rino_ref.at[k]))):
                    copies.append(pltpu.make_async_remote_copy(
                        src_ref=src, dst_ref=dst, send_sem=send_sems.at[arr, k], recv_sem=recv_sems.at[arr, k],
                        device_id=(x, y, 1 - c), device_id_type=MESH))

            @pl.when((q == 0) & (m == 0))
            def _():
                for cp in copies:
                    cp.start()

        @pl.when(q < 2)
        def _():
            o_ref[...] = _dot(ht_ref[...], r_ref[...])

        @pl.when(q >= 2)
        def _():
            o_ref[...] = _dot(ht_ref[...], s_ref[...])

        if to_sibling:
            @pl.when((q == 3) & (m == nm - 1))
            def _():
                for cp in copies:
                    cp.wait_recv()
                for cp in copies:
                    cp.wait_send()

    hbm = pl.BlockSpec(memory_space=pltpu.HBM)
    gw_shape = jax.ShapeDtypeStruct((4, D, W), F32)
    out_shape, out_specs, extra_in, scratch = (gw_shape,), (pl.BlockSpec((None, tmm, W), lambda q, m, core: (q, m, 0)),), [], []
    if to_sibling:
        out_shape += (gw_shape, jax.ShapeDtypeStruct((4,) + to_sibling[1].shape[1:], F32))
        out_specs += (hbm, hbm)
        extra_in = [hbm, hbm]
        scratch = [pltpu.SemaphoreType.DMA((2, 4)), pltpu.SemaphoreType.DMA((2, 4))]
    return pl.pallas_call(
        body, name=name, out_shape=out_shape,
        grid_spec=pltpu.PrefetchScalarGridSpec(
            num_scalar_prefetch=1, grid=(4, nm),
            in_specs=[pl.BlockSpec((tmm, S), lambda q, m, core: (m, 0)),
                      pl.BlockSpec((None, S, W), lambda q, m, core: (jnp.minimum(2 * q + core[0], 3), 0, 0)),
                      pl.BlockSpec((None, S, W), lambda q, m, core: (jnp.maximum(2 * q + core[0] - 4, 0), 0, 0))]
            + extra_in,
            out_specs=out_specs, scratch_shapes=scratch),
        compiler_params=_params("arbitrary", "arbitrary"),
    )(core, ht, dpr, dps, *(to_sibling or ()))


def _grad_w_out(mix_r, mix_s, dx2b):
    S, W = mix_r.shape
    D = dx2b.shape[1]
    tmm = min(512, W)
    tk = min(1024, S)

    def body(r_ref, s_ref, b_ref, o_ref):
        j, kk = pl.program_id(0), pl.program_id(2)

        def acc(a_ref):
            part = _dot_tn(a_ref[...], b_ref[...])

            @pl.when(kk == 0)
            def _():
                o_ref[...] = part

            @pl.when(kk > 0)
            def _():
                o_ref[...] += part

        pl.when(j == 0)(lambda: acc(r_ref))
        pl.when(j == 1)(lambda: acc(s_ref))

    return pl.pallas_call(
        body, name="grad_w_out", out_shape=jax.ShapeDtypeStruct((2, W, D), F32), grid=(2, W // tmm, S // tk),
        in_specs=[pl.BlockSpec((tk, tmm), lambda j, m, k: (k, m)),
                  pl.BlockSpec((tk, tmm), lambda j, m, k: (k, m)),
                  pl.BlockSpec((tk, D), lambda j, m, k: (k, 0))],
        out_specs=pl.BlockSpec((None, tmm, D), lambda j, m, k: (j, m, 0)),
        compiler_params=_params("parallel", "parallel", "arbitrary"),
    )(mix_r, mix_s, dx2b)


def _dh_norm_bwd(dpr, dps, w_all, x, dx2, gain, chip_sums):
    _, S, W = dpr.shape
    D = x.shape[1]
    tm = min(512, S)
    ni = S // tm

    def body(r_ref, s_ref, w_ref, x_ref, dx2_ref, g_ref, sa_ref, sb_ref, gx_ref, dgain_ref, ra_ref, rb_ref,
             acc_ref, send_sems, recv_sems):
        i, j = pl.program_id(0), pl.program_id(1)
        start_exchange, wait_exchange = _exchange_chip_sums((sa_ref, sb_ref), (ra_ref, rb_ref), send_sems, recv_sems)

        @pl.when((i == 0) & (j == 0))
        def _():
            start_exchange()
            dgain_ref[...] = jnp.zeros_like(dgain_ref)

        def acc(b_ref):
            part = _dot_nt(b_ref[...], w_ref[...])

            @pl.when(j == 0)
            def _():
                acc_ref[...] = part

            @pl.when(j > 0)
            def _():
                acc_ref[...] += part

        pl.when(j < 4)(lambda: acc(r_ref))
        pl.when(j >= 4)(lambda: acc(s_ref))

        @pl.when(j == 7)
        def _():
            xv, dh, gv = x_ref[...], acc_ref[...], g_ref[...]
            r1 = lax.rsqrt(jnp.mean(xv * xv, axis=-1, keepdims=True) + EPS)
            n = xv * r1
            dgain_ref[...] += jnp.sum(dh * n, axis=0, keepdims=True)
            dn = dh * gv
            gx_ref[...] = dx2_ref[...] + r1 * (dn - n * jnp.mean(dn * n, axis=-1, keepdims=True))

        pl.when((i == ni - 1) & (j == 7))(wait_exchange)

    row = pl.BlockSpec((tm, D), lambda i, j: (i, 0))
    one = pl.BlockSpec((1, D), lambda i, j: (0, 0))
    hbm = pl.BlockSpec(memory_space=pltpu.HBM)
    return pl.pallas_call(
        body, name="dh_norm_bwd",
        out_shape=(jax.ShapeDtypeStruct((S, D), F32), jax.ShapeDtypeStruct((1, D), F32))
        + tuple(jax.ShapeDtypeStruct((3,) + s.shape[1:], s.dtype) for s in chip_sums),
        grid=(ni, 8),
        in_specs=[pl.BlockSpec((None, tm, W), lambda i, j: (jnp.minimum(j, 3), i, 0)),
                  pl.BlockSpec((None, tm, W), lambda i, j: (jnp.maximum(j - 4, 0), i, 0)),
                  pl.BlockSpec((None, D, W), lambda i, j: (j, 0, 0)), row, row, one, hbm, hbm],
        out_specs=(row, one, hbm, hbm),
        scratch_shapes=[pltpu.VMEM((tm, D), F32), pltpu.SemaphoreType.DMA((2, 3)), pltpu.SemaphoreType.DMA((2, 3))],
        compiler_params=_params("arbitrary", "arbitrary"),
    )(dpr, dps, w_all, x, dx2, gain, *chip_sums)


def _own_block(gw, pos, q):
    return q if gw.shape[0] == 4 else 2 * q + pos[0]


def _rs_local_sum(gw, rin, pos):
    _, R, C = gw.shape
    tr = min(256, R)

    def body(pos_ref, a_ref, b_ref, o_ref):
        o_ref[...] = (a_ref[...] + b_ref[...]).astype(BF16)

    return pl.pallas_call(
        body, name="rs_local_sum", out_shape=jax.ShapeDtypeStruct((4, R, C), BF16),
        grid_spec=pltpu.PrefetchScalarGridSpec(
            num_scalar_prefetch=1, grid=(4, R // tr),
            in_specs=[pl.BlockSpec((None, tr, C), lambda q, i, pos: (_own_block(gw, pos, q), i, 0)),
                      pl.BlockSpec((None, tr, C), lambda q, i, pos: (q, i, 0))],
            out_specs=pl.BlockSpec((None, tr, C), lambda q, i, pos: (q, i, 0))),
        compiler_params=_params("parallel", "parallel"),
    )(pos, gw, rin)


def _adamw(w, g, m, v):
    m2 = ADAM_B1 * m + (1.0 - ADAM_B1) * g
    v2 = ADAM_B2 * v + (1.0 - ADAM_B2) * (g * g)
    m_hat = m2 / (1.0 - ADAM_B1 ** ADAM_STEP)
    v_hat = v2 / (1.0 - ADAM_B2 ** ADAM_STEP)
    delta = -ADAM_LR * (m_hat / (jnp.sqrt(v_hat) + ADAM_EPS) + ADAM_WD * w)
    return delta, m2, v2


def _adamw_shard(gw, rin, rb, w, m, v, pos):
    _, R, C = gw.shape
    tr = min(256, R)

    def body(pos_ref, a_ref, b_ref, rb_ref, w_ref, m_ref, v_ref, g_ref, d_ref, m2_ref, v2_ref):
        g = a_ref[...] + b_ref[...]
        for k in range(3):
            g = g + rb_ref[k].astype(F32)
        g_ref[...] = g
        d_ref[...], m2_ref[...], v2_ref[...] = _adamw(w_ref[...], g, m_ref[...], v_ref[...])

    plain = pl.BlockSpec((tr, C), lambda i, pos: (i, 0))
    shape = jax.ShapeDtypeStruct((R, C), F32)
    return pl.pallas_call(
        body, name="adamw_shard", out_shape=(shape,) * 4,
        grid_spec=pltpu.PrefetchScalarGridSpec(
            num_scalar_prefetch=1, grid=(R // tr,),
            in_specs=[pl.BlockSpec((None, tr, C), lambda i, pos: (_own_block(gw, pos, pos[1]), i, 0)),
                      pl.BlockSpec((None, tr, C), lambda i, pos: (pos[1], i, 0)),
                      pl.BlockSpec((3, tr, C), lambda i, pos: (0, i, 0)), plain, plain, plain],
            out_specs=(plain,) * 4),
        compiler_params=_params("parallel"),
    )(pos, gw, rin, rb, w, m, v)


def _adamw_small(parts, w, m, v):
    _, rows, n = parts.shape

    def body(p_ref, w_ref, m_ref, v_ref, g_ref, d_ref, m2_ref, v2_ref):
        g = p_ref[0]
        for d in range(1, N_DEV):
            g = g + p_ref[d]
        g_ref[...] = g
        d_ref[...], m2_ref[...], v2_ref[...] = _adamw(w_ref[...], g, m_ref[...], v_ref[...])

    shape = jax.ShapeDtypeStruct((rows, n), F32)
    return pl.pallas_call(body, name="adamw_small", out_shape=(shape,) * 4)(parts, w, m, v)


def _rope_tables(S):
    half = HEAD_DIM // 2
    inv = ROPE_THETA ** (-jnp.arange(half, dtype=F32) / half)
    ang = jnp.arange(S, dtype=F32)[:, None] * inv[None, :]
    cos, sin = jnp.cos(ang), jnp.sin(ang)
    return jnp.concatenate([cos, cos], axis=1), jnp.concatenate([-sin, sin], axis=1)


def _retention_tables(H):
    lg = jnp.log1p(-jnp.exp2(-5.0 - jnp.arange(H, dtype=F32)))
    n = jnp.arange(CHUNK, dtype=F32)
    rel = n[:, None] - n[None, :]
    decay = jnp.where(rel >= 0, jnp.exp(lg[:, None, None] * jnp.maximum(rel, 0.0)), 0.0)
    shape = (H, CHUNK, HEAD_DIM)
    xi = jnp.broadcast_to(jnp.exp(lg[:, None] * (n + 1.0))[:, :, None], shape)
    zeta = jnp.broadcast_to(jnp.exp(lg[:, None] * (CHUNK - 1.0 - n))[:, :, None], shape)
    gamma_c = jnp.broadcast_to(jnp.exp(lg * CHUNK)[:, None, None], shape)
    return decay, xi, zeta, gamma_c


def _pack_small(parts):
    flat = []
    for p in parts:
        p = p.reshape(-1)
        flat.append(jnp.pad(p, (0, -p.shape[0] % LANES)))
    flat = jnp.concatenate(flat)
    return jnp.pad(flat, (0, SMALL_N - flat.shape[0])).reshape(SUBLANES, SMALL_N // SUBLANES)


def _unpack_small(packed, shapes):
    flat = packed.reshape(-1)
    out, at = [], 0
    for shp in shapes:
        size = 1
        for s in shp:
            size *= s
        out.append(flat[at:at + size].reshape(shp))
        at += size + (-size % LANES)
    return out


def kernel(x, norm_gain, w_in, ret_gn_gain, ret_gn_bias, sb_norm_gain, w_out, final_norm_gain, loss_target, m_norm_gain, m_w_in, m_ret_gn_gain, m_ret_gn_bias, m_sb_norm_gain, m_w_out, m_final_norm_gain, v_norm_gain, v_w_in, v_ret_gn_gain, v_ret_gn_bias, v_sb_norm_gain, v_w_out, v_final_norm_gain):
    S, D = x.shape[1], x.shape[2]
    W = w_in.shape[2]
    wo_rows = w_out.shape[1]
    H = W // HEAD_DIM
    xs, tgt = x[0], loss_target[0]
    mx, my, mc = _mesh_pos()
    pos = jnp.stack([mc, 2 * mx + my]).astype(jnp.int32)

    cos, sin = _rope_tables(S)
    tabs = _retention_tables(H)

    h, ht = _rmsnorm_fwd(xs, norm_gain)
    proj, w_all = _in_proj_gather(h, w_in[0].astype(BF16), cos, sin, _gather_order())
    mix_r = _ret_fwd(proj, tabs, ret_gn_gain, ret_gn_bias)
    mix_s, raw_s, carries, wo_all = _sb_fwd(proj, sb_norm_gain, w_out[0].astype(BF16))
    wo_full = wo_all.reshape(N_DEV * wo_rows, D)
    dx2, dx2b, dmix, loss_p, d_gf = _out_proj_loss(mix_r, mix_s, wo_full, xs, tgt, final_norm_gain[None])

    dpr, d_rgain, d_rbias = _ret_bwd(proj, dmix, tabs, ret_gn_gain, ret_gn_bias, cos, sin)
    dps, d_sgain = _sb_bwd(proj, raw_s, carries, dmix, sb_norm_gain)
    gwo = _grad_w_out(mix_r, mix_s, dx2b).reshape(N_DEV, wo_rows, D)
    gw_sibling, = _grad_w_in_half(ht, dpr, dps, (1 - mc).reshape(1).astype(jnp.int32), "grad_w_in_sibling")
    gw, rin, rino = _grad_w_in_half(ht, dpr, dps, mc.reshape(1).astype(jnp.int32), "grad_w_in_own",
                                    to_sibling=(gw_sibling, gwo))
    chip_sums = (_rs_local_sum(gw, rin, pos), _rs_local_sum(gwo, rino, pos))
    grad_x, d_gain, rb, rbo = _dh_norm_bwd(dpr, dps, w_all, xs, dx2, norm_gain, chip_sums)
    g_in, d_in, m_in, v_in = _adamw_shard(gw, rin, rb, w_in[0], m_w_in[0], v_w_in[0], pos)
    g_out, d_out, m_out, v_out = _adamw_shard(gwo, rino, rbo, w_out[0], m_w_out[0], v_w_out[0], pos)

    small_w = [norm_gain, ret_gn_gain, ret_gn_bias, sb_norm_gain, final_norm_gain]
    small_m = [m_norm_gain, m_ret_gn_gain, m_ret_gn_bias, m_sb_norm_gain, m_final_norm_gain]
    small_v = [v_norm_gain, v_ret_gn_gain, v_ret_gn_bias, v_sb_norm_gain, v_final_norm_gain]
    shapes = [()] + [w.shape for w in small_w]
    zero = jnp.zeros((), F32)
    parts = _small_all_gather(_pack_small([loss_p[0, 0], d_gain, d_rgain, d_rbias, d_sgain, d_gf]))
    packed = _adamw_small(parts, _pack_small([zero] + small_w), _pack_small([zero] + small_m),
                          _pack_small([zero] + small_v))
    g_s, d_s, m_s, v_s = (_unpack_small(p, shapes) for p in packed)

    grads = [g_s[1], g_in[None], g_s[2], g_s[3], g_s[4], g_out[None], g_s[5]]
    deltas = [d_s[1], d_in[None], d_s[2], d_s[3], d_s[4], d_out[None], d_s[5]]
    new_m = [m_s[1], m_in[None], m_s[2], m_s[3], m_s[4], m_out[None], m_s[5]]
    new_v = [v_s[1], v_in[None], v_s[2], v_s[3], v_s[4], v_out[None], v_s[5]]
    return (g_s[0], grad_x[None], *grads, *deltas, *new_m, *new_v)
```

```python
import functools

import jax
import jax.numpy as jnp
from jax import lax
from jax.experimental import pallas as pl
from jax.experimental.pallas import tpu as pltpu

F32 = jnp.float32
BF16 = jnp.bfloat16

HEAD_DIM = 128
CHUNK = 128
RET_UNROLL = 4
ROPE_THETA = 10000.0
EPS = 1e-6
ADAM_LR = 0.001
ADAM_B1 = 0.9
ADAM_B2 = 0.999
ADAM_EPS = 1e-08
ADAM_WD = 0.01
ADAM_STEP = 10

N_DEV = 8
LANES = 128
SUBLANES = 8
VMEM_LIMIT = 56 * 1024 * 1024
SB_BLOCK = 256
SB_PER_STEP = 2
SMALL_N = 8192
EXP_IS_ZERO_BELOW = -104.0
NOT_VISITED = -1e30
MESH = pl.DeviceIdType.MESH

NT = (((1,), (1,)), ((), ()))
TN = (((0,), (0,)), ((), ()))


def _params(*sem):
    return pltpu.CompilerParams(dimension_semantics=sem if sem else None, vmem_limit_bytes=VMEM_LIMIT)


def _dot(a, b):
    return jnp.dot(a, b, preferred_element_type=F32)


def _dot_nt(a, b):
    return lax.dot_general(a, b, NT, preferred_element_type=F32)


def _dot_tn(a, b):
    return lax.dot_general(a, b, TN, preferred_element_type=F32)


def _sigmoid(g):
    return 1.0 / (1.0 + jnp.exp(-g))


def _rot(a, cos, sin_signed):
    return a * cos + pltpu.roll(a, HEAD_DIM // 2, 1) * sin_signed


def _mesh_pos():
    return lax.axis_index("x"), lax.axis_index("y"), lax.axis_index("c")


def _to_sibling_copies(blocks, out_ref, send_sems, recv_sems):
    x, y, c = _mesh_pos()
    return [pltpu.make_async_remote_copy(
        src_ref=block, dst_ref=out_ref.at[k], send_sem=send_sems.at[k], recv_sem=recv_sems.at[k],
        device_id=(x, y, 1 - c), device_id_type=MESH) for k, block in enumerate(blocks)]


def _exchange_chip_sums(srcs, outs, send_sems, recv_sems):
    x, y, c = _mesh_pos()
    copies = []
    for arr, (src, out) in enumerate(zip(srcs, outs)):
        for k in range(1, 4):
            px = 1 - x if k & 2 else x
            py = 1 - y if k & 1 else y
            copies.append(pltpu.make_async_remote_copy(
                src_ref=src.at[2 * px + py], dst_ref=out.at[k - 1],
                send_sem=send_sems.at[arr, k - 1], recv_sem=recv_sems.at[arr, k - 1],
                device_id=(px, py, c), device_id_type=MESH))

    def start():
        for cp in copies:
            cp.start()

    def wait():
        for cp in copies:
            cp.wait_recv()
        for cp in copies:
            cp.wait_send()

    return start, wait


def _small_all_gather(small):
    rows, n = small.shape

    def body(s_ref, o_ref, send_sems, recv_sems, local_sem):
        start, wait = _exchange_with_all(s_ref, o_ref, send_sems, recv_sems, local_sem)
        start()
        wait()

    vmem = pl.BlockSpec(memory_space=pltpu.VMEM)
    return pl.pallas_call(
        body, name="small_all_gather",
        out_shape=jax.ShapeDtypeStruct((N_DEV, rows, n), small.dtype),
        in_specs=[vmem], out_specs=vmem,
        scratch_shapes=[pltpu.SemaphoreType.DMA((N_DEV - 1,)), pltpu.SemaphoreType.DMA((N_DEV - 1,)),
                        pltpu.SemaphoreType.DMA],
    )(small)


def _rmsnorm_fwd(x, gain):
    S, D = x.shape
    tm = min(512, S)

    def body(x_ref, g_ref, h_ref, ht_ref):
        xv = x_ref[...]
        r = lax.rsqrt(jnp.mean(xv * xv, axis=-1, keepdims=True) + EPS)
        hv = xv * r * g_ref[...]
        h_ref[...] = hv.astype(BF16)
        ht_ref[...] = hv.T.astype(BF16)

    return pl.pallas_call(
        body, name="rmsnorm_fwd",
        out_shape=(jax.ShapeDtypeStruct((S, D), BF16), jax.ShapeDtypeStruct((D, S), BF16)), grid=(S // tm,),
        in_specs=[pl.BlockSpec((tm, D), lambda i: (i, 0)), pl.BlockSpec((1, D), lambda i: (0, 0))],
        out_specs=(pl.BlockSpec((tm, D), lambda i: (i, 0)), pl.BlockSpec((D, tm), lambda i: (0, i))),
        compiler_params=_params("parallel"),
    )(x, gain)


def _gather_order():
    x, y, c = _mesh_pos()
    chips = [(1 - x, y), (x, 1 - y), (1 - x, 1 - y)]
    devs = [(x, y, c), (x, y, 1 - c)] + [(*chip, c) for chip in chips] + [(*chip, 1 - c) for chip in chips]
    return jnp.stack([4 * px + 2 * py + pc for px, py, pc in devs]).astype(jnp.int32)


def _in_proj_gather(h, w_shard, cos, sin, order):
    S, D = h.shape
    W = w_shard.shape[1]
    tm = min(512, S)
    ni = S // tm

    def body(order_ref, h_ref, w_ref, cos_ref, sin_ref, o_ref, wall_ref, wbuf, send_sems, recv_sems, local_sem,
             load_sem):
        step, i = pl.program_id(0), pl.program_id(1)
        x, y, c = _mesh_pos()
        me, sibling = (x, y, c), (x, y, 1 - c)
        chips = [(1 - x, y), (x, 1 - y), (1 - x, 1 - y)]

        def slot(dev):
            px, py, pc = dev
            return wall_ref.at[4 * px + 2 * py + pc]

        def copy(k, block, to, src=None):
            dst = slot(block)
            return pltpu.make_async_remote_copy(
                src_ref=dst if src is None else src, dst_ref=dst, send_sem=send_sems.at[k], recv_sem=recv_sems.at[k],
                device_id=to, device_id_type=MESH)

        def load(src):
            cp = pltpu.make_async_copy(src, wbuf, load_sem)
            cp.start()
            cp.wait()

        first = [copy(0, me, sibling, src=w_ref)] + [copy(1 + j, me, (*chip, c), src=w_ref)
                                                     for j, chip in enumerate(chips)]
        passed = [copy(4 + j, (*chip, c), sibling) for j, chip in enumerate(chips)]
        mine = pltpu.make_async_copy(w_ref, slot(me), local_sem)

        @pl.when(i == 0)
        def _():
            @pl.when(step == 0)
            def _():
                for cp in first:
                    cp.start()
                mine.start()
                load(w_ref)

            @pl.when(step == 1)
            def _():
                copy(0, sibling, me).wait_recv()
                load(slot(sibling))

            for j, chip in enumerate(chips):
                @pl.when(step == 2 + j)
                def _(j=j, chip=chip):
                    copy(1 + j, (*chip, c), me).wait_recv()
                    passed[j].start()
                    load(slot((*chip, c)))

                @pl.when(step == 5 + j)
                def _(j=j, chip=chip):
                    copy(4 + j, (*chip, 1 - c), me).wait_recv()
                    load(slot((*chip, 1 - c)))

        acc = _dot(h_ref[...], wbuf[...])
        b = order_ref[step]

        @pl.when(b >= 2)
        def _():
            o_ref[...] = acc

        @pl.when(b < 2)
        def _():
            scale = jnp.where(b == 1, HEAD_DIM ** -0.5, 1.0).astype(F32)
            cs, sn = cos_ref[...], sin_ref[...]
            for hh in range(W // HEAD_DIM):
                cols = slice(hh * HEAD_DIM, (hh + 1) * HEAD_DIM)
                o_ref[:, cols] = _rot(acc[:, cols], cs, sn) * scale

        @pl.when((step == N_DEV - 1) & (i == ni - 1))
        def _():
            for cp in first + passed:
                cp.wait_send()
            mine.wait()

    hbm = pl.BlockSpec(memory_space=pltpu.HBM)
    rope = pl.BlockSpec((tm, HEAD_DIM), lambda s, i, order: (i, 0))
    return pl.pallas_call(
        body, name="in_proj_gather",
        out_shape=(jax.ShapeDtypeStruct((N_DEV, S, W), F32), jax.ShapeDtypeStruct((N_DEV, D, W), BF16)),
        grid_spec=pltpu.PrefetchScalarGridSpec(
            num_scalar_prefetch=1, grid=(N_DEV, ni),
            in_specs=[pl.BlockSpec((tm, D), lambda s, i, order: (i, 0)), hbm, rope, rope],
            out_specs=(pl.BlockSpec((None, tm, W), lambda s, i, order: (order[s], i, 0)), hbm),
            scratch_shapes=[pltpu.VMEM((D, W), BF16), pltpu.SemaphoreType.DMA((7,)), pltpu.SemaphoreType.DMA((7,)),
                            pltpu.SemaphoreType.DMA, pltpu.SemaphoreType.DMA]),
        compiler_params=_params("arbitrary", "arbitrary"),
    )(order, h, w_shard, cos, sin)


def _head_spec(S, j):
    return pl.BlockSpec((None, S, HEAD_DIM), lambda h, *_: (j, 0, h))


def _ret_chunk(q, k, vb, r_prev, dec, xi, ze):
    qb, kb = q.astype(BF16), k.astype(BF16)
    sb = (_dot_nt(qb, kb) * dec).astype(BF16)
    qx = (q * xi).astype(BF16)
    kz = (k * ze).astype(BF16)
    out = _dot(sb, vb) + _dot(qx, r_prev.astype(BF16))
    return out, _dot_tn(kz, vb), (qb, kb, sb, qx, kz)


def _chunk_loop(nc, step, init, unroll=RET_UNROLL):
    def trip(i, state):
        for u in range(unroll):
            state = step(i * unroll + u, state)
        return state

    assert nc % unroll == 0
    return lax.fori_loop(0, nc // unroll, trip, init)


def _table_specs():
    return [pl.BlockSpec((None, CHUNK, HEAD_DIM), lambda h, *_: (h, 0, 0))] * 4


def _ret_fwd(proj, tabs, gn_gain, gn_bias):
    _, S, W = proj.shape
    H, nc = W // HEAD_DIM, S // CHUNK

    def body(q_ref, k_ref, v_ref, g_ref, dec_ref, xi_ref, ze_ref, gam_ref, gain_ref, bias_ref, o_ref):
        dec, xi, ze, gam = dec_ref[...], xi_ref[...], ze_ref[...], gam_ref[...]
        gain, bias = gain_ref[...], bias_ref[...]

        def step(c, state):
            rows = pl.ds(pl.multiple_of(c * CHUNK, CHUNK), CHUNK)
            out, kv, _ = _ret_chunk(q_ref[rows, :], k_ref[rows, :], v_ref[rows, :].astype(BF16), state,
                                    dec, xi, ze)
            mu = jnp.mean(out, axis=-1, keepdims=True)
            d = out - mu
            yn = d * lax.rsqrt(jnp.mean(d * d, axis=-1, keepdims=True) + EPS)
            g = g_ref[rows, :]
            o_ref[rows, :] = (g * _sigmoid(g) * (yn * gain + bias)).astype(BF16)
            return gam * state + kv

        _chunk_loop(nc, step, jnp.zeros((HEAD_DIM, HEAD_DIM), F32))

    vec = pl.BlockSpec((1, HEAD_DIM), lambda h: (0, h))
    return pl.pallas_call(
        body, name="ret_fwd", out_shape=jax.ShapeDtypeStruct((S, W), BF16), grid=(H,),
        in_specs=[_head_spec(S, 0), _head_spec(S, 1), _head_spec(S, 2), _head_spec(S, 3)] + _table_specs() + [vec, vec],
        out_specs=pl.BlockSpec((S, HEAD_DIM), lambda h: (0, h)),
        compiler_params=_params("parallel"),
    )(proj, proj, proj, proj, *tabs, gn_gain, gn_bias)


def _sb_scores(qb, kk, masked, causal, upper):
    z = _dot_nt(qb, kk) * (HEAD_DIM ** -0.5)
    e = jnp.exp(-jnp.abs(z))
    l1p = jnp.log1p(e)
    log_beta = jnp.minimum(z, 0.0) - l1p
    lk = jnp.minimum(-z, 0.0) - l1p
    if masked:
        lk = jnp.where(causal, lk, 0.0)
    hi = lk.astype(BF16)
    lo = (lk - hi.astype(F32)).astype(BF16)
    cs = _dot(hi, upper) + _dot(lo, upper)
    return z, e, log_beta, lk, cs


def _tri(B, kind):
    r = lax.broadcasted_iota(jnp.int32, (B, B), 0)
    c = lax.broadcasted_iota(jnp.int32, (B, B), 1)
    return {"gt": r > c, "lt": r < c}[kind]


def _ones_where(mask):
    return jnp.where(mask, 1.0, 0.0).astype(BF16)


def _exchange_with_all(src_ref, out_ref, send_sems, recv_sems, local_sem):
    x, y, c = _mesh_pos()
    peers = [(1 - x if k & 4 else x, 1 - y if k & 2 else y, 1 - c if k & 1 else c) for k in range(1, N_DEV)]

    def copy(k, owner, to):
        px, py, pc = owner
        return pltpu.make_async_remote_copy(
            src_ref=src_ref, dst_ref=out_ref.at[4 * px + 2 * py + pc], send_sem=send_sems.at[k],
            recv_sem=recv_sems.at[k], device_id=to, device_id_type=MESH)

    sends = [copy(k, (x, y, c), p) for k, p in enumerate(peers)]
    mine = pltpu.make_async_copy(src_ref, out_ref.at[4 * x + 2 * y + c], local_sem)

    def start():
        for cp in sends:
            cp.start()
        mine.start()

    def wait():
        for k, p in enumerate(peers):
            copy(k, p, p).wait_recv()
        for cp in sends:
            cp.wait_send()
        mine.wait()

    return start, wait


def _sb_fwd(proj, gain, wo_shard):
    _, S, W = proj.shape
    H = W // HEAD_DIM
    B = min(SB_BLOCK, S)
    nq = S // B
    assert nq <= HEAD_DIM and nq % SB_PER_STEP == 0
    ns = nq // SB_PER_STEP

    def body(q_ref, k_ref, v_ref, g_ref, gain_ref, wo_ref, mix_ref, raw_ref, car_ref, woall_ref, kb_ref, vb_ref,
             send_sems, recv_sems, local_sem):
        hd, si = pl.program_id(0), pl.program_id(1)
        start_gather, wait_gather = _exchange_with_all(wo_ref, woall_ref, send_sems, recv_sems, local_sem)
        pl.when((hd == 0) & (si == 0))(start_gather)

        @pl.when(si == 0)
        def _():
            kb_ref[...] = k_ref[...].astype(BF16)
            vb_ref[...] = v_ref[...].astype(BF16)

        causal = _tri(B, "gt")
        upper = _ones_where(causal)
        lane = lax.broadcasted_iota(jnp.int32, (B, HEAD_DIM), 1)

        def block(qb, kb, carry, acc, saved, masked):
            rows = pl.ds(pl.multiple_of(kb * B, B), B)
            _, _, log_beta, lk, cs = _sb_scores(qb, kb_ref[rows, :], masked, causal, upper)
            a = jnp.exp(log_beta + cs + carry)
            if masked:
                a = jnp.where(causal, a, 0.0)
            acc = acc + _dot(a.astype(BF16), vb_ref[rows, :])
            return carry + jnp.sum(lk, axis=1, keepdims=True), acc, jnp.where(lane == kb, carry, saved)

        init = (jnp.zeros((B, 1), F32), jnp.zeros((B, HEAD_DIM), F32), jnp.full((B, HEAD_DIM), NOT_VISITED, F32))

        def live(st):
            return (st[0] >= 0) & (jnp.max(st[1]) >= EXP_IS_ZERO_BELOW)

        def finish(u, acc, saved):
            rows = slice(u * B, (u + 1) * B)
            raw_ref[rows, :] = acc
            car_ref[rows, :] = saved
            yn = acc * lax.rsqrt(jnp.mean(acc * acc, axis=-1, keepdims=True) + EPS)
            g = g_ref[rows, :]
            mix_ref[rows, :] = (g * _sigmoid(g) * (yn * gain_ref[...])).astype(BF16)

        def whole(first_step):
            heads = []
            for u in range(SB_PER_STEP):
                qi = si * SB_PER_STEP + u
                qb = q_ref[u * B:(u + 1) * B, :].astype(BF16)
                state = block(qb, qi, *init, True)
                if not (first_step and u == 0):
                    state = block(qb, qi - 1, *state, False)
                heads.append((qi, qb, state))
            for u, (qi, qb, state) in enumerate(heads):
                if not (first_step and u == 0):
                    state = lax.while_loop(
                        live, lambda st, qb=qb: (st[0] - 1,) + block(qb, st[0], st[1], st[2], st[3], False),
                        (qi - 2,) + state)[1:]
                finish(u, state[1], state[2])

        pl.when(si == 0)(lambda: whole(True))
        pl.when(si > 0)(lambda: whole(False))
        pl.when((hd == H - 1) & (si == ns - 1))(wait_gather)

    tq = SB_PER_STEP * B
    tile = lambda j: pl.BlockSpec((None, tq, HEAD_DIM), lambda h, i: (j, i, h))
    out_tile = pl.BlockSpec((tq, HEAD_DIM), lambda h, i: (i, h))
    hbm = pl.BlockSpec(memory_space=pltpu.HBM)
    return pl.pallas_call(
        body, name="sb_fwd",
        out_shape=(jax.ShapeDtypeStruct((S, W), BF16), jax.ShapeDtypeStruct((S, W), F32),
                   jax.ShapeDtypeStruct((S, W), F32), jax.ShapeDtypeStruct((N_DEV,) + wo_shard.shape, BF16)),
        grid=(H, ns),
        in_specs=[tile(4), _head_spec(S, 5), _head_spec(S, 6), tile(7),
                  pl.BlockSpec((1, HEAD_DIM), lambda h, i: (0, h)), hbm],
        out_specs=(out_tile, out_tile, out_tile, hbm),
        scratch_shapes=[pltpu.VMEM((S, HEAD_DIM), BF16), pltpu.VMEM((S, HEAD_DIM), BF16),
                        pltpu.SemaphoreType.DMA((N_DEV - 1,)), pltpu.SemaphoreType.DMA((N_DEV - 1,)),
                        pltpu.SemaphoreType.DMA],
        compiler_params=_params("arbitrary", "arbitrary"),
    )(proj, proj, proj, proj, gain, wo_shard)


def _out_proj_loss(mix_r, mix_s, w_out, x, tgt, gf):
    S, W = mix_r.shape
    D = x.shape[1]
    tm = min(256, S)

    def body(mr_ref, ms_ref, wo_ref, x_ref, t_ref, gf_ref, dx2_ref, dx2b_ref, dmix_ref, loss_ref, gfn_ref):
        @pl.when(pl.program_id(0) == 0)
        def _():
            loss_ref[...] = jnp.zeros_like(loss_ref)
            gfn_ref[...] = jnp.zeros_like(gfn_ref)

        gfv = gf_ref[...]
        x2 = x_ref[...] + (_dot(mr_ref[...], wo_ref[:W, :]) + _dot(ms_ref[...], wo_ref[W:, :]))
        r2 = lax.rsqrt(jnp.mean(x2 * x2, axis=-1, keepdims=True) + EPS)
        n = x2 * r2
        err = n * gfv - t_ref[...]
        loss_ref[...] += 0.5 * jnp.sum(jnp.mean(err * err, axis=-1, keepdims=True))
        dy = err * (1.0 / D)
        gfn_ref[...] += jnp.sum(dy * n, axis=0, keepdims=True)
        dn = dy * gfv
        dx2 = r2 * (dn - n * jnp.mean(dn * n, axis=-1, keepdims=True))
        dx2_ref[...] = dx2
        b = dx2.astype(BF16)
        dx2b_ref[...] = b
        dmix_ref[:, :W] = _dot_nt(b, wo_ref[:W, :])
        dmix_ref[:, W:] = _dot_nt(b, wo_ref[W:, :])

    row = lambda width: pl.BlockSpec((tm, width), lambda i: (i, 0))
    return pl.pallas_call(
        body, name="out_proj_loss",
        out_shape=(jax.ShapeDtypeStruct((S, D), F32), jax.ShapeDtypeStruct((S, D), BF16),
                   jax.ShapeDtypeStruct((S, 2 * W), F32), jax.ShapeDtypeStruct((SUBLANES, LANES), F32),
                   jax.ShapeDtypeStruct((1, D), F32)),
        grid=(S // tm,),
        in_specs=[row(W), row(W), pl.BlockSpec((2 * W, D), lambda i: (0, 0)), row(D), row(D),
                  pl.BlockSpec((1, D), lambda i: (0, 0))],
        out_specs=(row(D), row(D), row(2 * W), pl.BlockSpec((SUBLANES, LANES), lambda i: (0, 0)),
                   pl.BlockSpec((1, D), lambda i: (0, 0))),
        compiler_params=_params("arbitrary"),
    )(mix_r, mix_s, w_out, x, tgt, gf)


def _silu_bwd(g, dm, normed):
    sig = _sigmoid(g)
    return dm * (g * sig), dm * normed * (sig * (1.0 + g * (1.0 - sig)))


def _ret_bwd(proj, dmix, tabs, gn_gain, gn_bias, cos, sin, gwo):
    _, S, W = proj.shape
    H, nc = W // HEAD_DIM, S // CHUNK

    def body(q_ref, k_ref, v_ref, g_ref, dm_ref, dec_ref, xi_ref, ze_ref, gam_ref, gain_ref, bias_ref, cos_ref,
             sin_ref, gwo_ref, dp_ref, dgain_ref, dbias_ref, rino_ref, rs_ref, send_sems, recv_sems):
        dec, xi, ze, gam = dec_ref[...], xi_ref[...], ze_ref[...], gam_ref[...]
        gain, bias = gain_ref[...], bias_ref[...]
        hd = pl.program_id(0)
        other_core = 1 - lax.axis_index("c")
        copies = _to_sibling_copies([gwo_ref.at[2 * k + other_core] for k in range(4)], rino_ref, send_sems, recv_sems)

        @pl.when(hd == 0)
        def _():
            for cp in copies:
                cp.start()

        def fwd_step(c, state):
            rows = pl.ds(pl.multiple_of(c * CHUNK, CHUNK), CHUNK)
            rs_ref[c] = state
            kz = (k_ref[rows, :] * ze).astype(BF16)
            return gam * state + _dot_tn(kz, v_ref[rows, :].astype(BF16))

        _chunk_loop(nc, fwd_step, jnp.zeros((HEAD_DIM, HEAD_DIM), F32))

        def bwd_step(i, carry):
            dgain, dbias, dstate = carry
            c = nc - 1 - i
            rows = pl.ds(pl.multiple_of(c * CHUNK, CHUNK), CHUNK)
            q, k, g = q_ref[rows, :], k_ref[rows, :], g_ref[rows, :]
            vb = v_ref[rows, :].astype(BF16)
            rb = rs_ref[c].astype(BF16)
            out, _, (qb, kb, sb, qx, kz) = _ret_chunk(q, k, vb, rs_ref[c], dec, xi, ze)
            mu = jnp.mean(out, axis=-1, keepdims=True)
            d = out - mu
            rstd = lax.rsqrt(jnp.mean(d * d, axis=-1, keepdims=True) + EPS)
            yn = d * rstd
            dgn, dg = _silu_bwd(g, dm_ref[rows, :], yn * gain + bias)
            dgain = dgain + jnp.sum(dgn * yn, axis=0, keepdims=True)
            dbias = dbias + jnp.sum(dgn, axis=0, keepdims=True)
            dyn = dgn * gain
            do = rstd * (dyn - jnp.mean(dyn, axis=-1, keepdims=True)
                         - yn * jnp.mean(dyn * yn, axis=-1, keepdims=True))
            dob = do.astype(BF16)
            drb = dstate.astype(BF16)
            dv = _dot_tn(sb, dob) + _dot(kz, drb)
            dsb = (_dot_nt(dob, vb) * dec).astype(BF16)
            dq = _dot(dsb, kb) + _dot_nt(dob, rb) * xi
            dk = _dot_tn(dsb, qb) + _dot_nt(vb, drb) * ze
            cs, sn = cos_ref[rows, :], -sin_ref[rows, :]
            dp_ref[0, rows, :] = _rot(dq, cs, sn).astype(BF16)
            dp_ref[1, rows, :] = (_rot(dk, cs, sn) * (HEAD_DIM ** -0.5)).astype(BF16)
            dp_ref[2, rows, :] = dv.astype(BF16)
            dp_ref[3, rows, :] = dg.astype(BF16)
            return dgain, dbias, gam * dstate + _dot_tn(qx, dob)

        zero = jnp.zeros((1, HEAD_DIM), F32)
        dgain, dbias, _ = _chunk_loop(nc, bwd_step, (zero, zero, jnp.zeros((HEAD_DIM, HEAD_DIM), F32)))
        dgain_ref[...] = dgain
        dbias_ref[...] = dbias

        @pl.when(hd == H - 1)
        def _():
            for cp in copies:
                cp.wait_recv()
            for cp in copies:
                cp.wait_send()

    vec = pl.BlockSpec((1, HEAD_DIM), lambda h: (0, h))
    full = pl.BlockSpec((S, HEAD_DIM), lambda h: (0, 0))
    hbm = pl.BlockSpec(memory_space=pltpu.HBM)
    return pl.pallas_call(
        body, name="ret_bwd",
        out_shape=(jax.ShapeDtypeStruct((4, S, W), BF16), jax.ShapeDtypeStruct((1, W), F32),
                   jax.ShapeDtypeStruct((1, W), F32), jax.ShapeDtypeStruct((4,) + gwo.shape[1:], gwo.dtype)),
        grid=(H,),
        in_specs=[_head_spec(S, 0), _head_spec(S, 1), _head_spec(S, 2), _head_spec(S, 3),
                  pl.BlockSpec((S, HEAD_DIM), lambda h: (0, h))] + _table_specs() + [vec, vec, full, full, hbm],
        out_specs=(pl.BlockSpec((4, S, HEAD_DIM), lambda h: (0, 0, h)), vec, vec, hbm),
        scratch_shapes=[pltpu.VMEM((nc, HEAD_DIM, HEAD_DIM), F32), pltpu.SemaphoreType.DMA((4,)),
                        pltpu.SemaphoreType.DMA((4,))],
        compiler_params=_params("arbitrary"),
    )(proj, proj, proj, proj, dmix, *tabs, gn_gain, gn_bias, cos, sin, gwo)


def _sb_bwd(proj, raw, carries, dmix, gain, chip_sums_o):
    _, S, W = proj.shape
    H = W // HEAD_DIM
    B = min(SB_BLOCK, S)
    nq = S // B
    ns = nq // SB_PER_STEP

    def body(q_ref, k_ref, v_ref, g_ref, raw_ref, car_ref, dm_ref, gain_ref, so_ref, dp_ref, dgain_ref, ro_ref,
             kb_ref, vb_ref, dk_ref, dv_ref, send_sems, recv_sems):
        hd, si = pl.program_id(0), pl.program_id(1)
        start_exchange, wait_exchange = _exchange_chip_sums((so_ref,), (ro_ref,), send_sems, recv_sems)
        pl.when((hd == 0) & (si == 0))(start_exchange)

        @pl.when(si == 0)
        def _():
            kb_ref[...] = k_ref[...].astype(BF16)
            vb_ref[...] = v_ref[...].astype(BF16)
            dk_ref[...] = jnp.zeros_like(dk_ref)
            dv_ref[...] = jnp.zeros_like(dv_ref)
            dgain_ref[...] = jnp.zeros_like(dgain_ref)

        causal = _tri(B, "gt")
        upper = _ones_where(causal)
        before = _ones_where(_tri(B, "lt"))
        lane = lax.broadcasted_iota(jnp.int32, (B, HEAD_DIM), 1)
        gain_v = gain_ref[...]

        def prologue(u):
            qi = si * SB_PER_STEP + u
            rows = slice(u * B, (u + 1) * B)
            o = raw_ref[rows, :]
            rstd = lax.rsqrt(jnp.mean(o * o, axis=-1, keepdims=True) + EPS)
            yn = o * rstd
            dnrm, dg = _silu_bwd(g_ref[rows, :], dm_ref[rows, :], yn * gain_v)
            dp_ref[3, pl.ds(pl.multiple_of(qi * B, B), B), :] = dg.astype(BF16)
            dgain_ref[...] += jnp.sum(dnrm * yn, axis=0, keepdims=True)
            dyn = dnrm * gain_v
            do = rstd * (dyn - yn * jnp.mean(dyn * yn, axis=-1, keepdims=True))
            return qi, q_ref[rows, :].astype(BF16), do.astype(BF16), car_ref[rows, :]

        def block(ctx, kb, carry_g, dq, masked):
            _, qb, dob, saved = ctx
            rows = pl.ds(pl.multiple_of(kb * B, B), B)
            kk, vv = kb_ref[rows, :], vb_ref[rows, :]
            z, e, log_beta, _, cs = _sb_scores(qb, kk, masked, causal, upper)
            carry_lk = jnp.sum(jnp.where(lane == kb, saved, 0.0), axis=1, keepdims=True)
            a = jnp.exp(log_beta + cs + carry_lk)
            if masked:
                a = jnp.where(causal, a, 0.0)
            gmat = _dot_nt(dob, vv) * a
            dv_ref[rows, :] += _dot_tn(a.astype(BF16), dob)
            hi = gmat.astype(BF16)
            lo = (gmat - hi.astype(F32)).astype(BF16)
            dlk = carry_g + (_dot(hi, before) + _dot(lo, before))
            r = 1.0 / (1.0 + e)
            er = e * r
            pos = z >= 0.0
            dz = (gmat * jnp.where(pos, er, r) - dlk * jnp.where(pos, r, er)) * (HEAD_DIM ** -0.5)
            if masked:
                dz = jnp.where(causal, dz, 0.0)
            dzb = dz.astype(BF16)
            dk_ref[rows, :] += _dot_tn(dzb, qb)
            return carry_g + jnp.sum(gmat, axis=1, keepdims=True), dq + _dot(dzb, kk)

        init = (jnp.zeros((B, 1), F32), jnp.zeros((B, HEAD_DIM), F32))

        def whole(first_step):
            ctxs = [prologue(u) for u in range(SB_PER_STEP)]
            states = []
            for u, ctx in enumerate(ctxs):
                state = init
                if not (first_step and u == 0):
                    visited = jnp.max(ctx[3], axis=0, keepdims=True) >= EXP_IS_ZERO_BELOW
                    first = jnp.min(jnp.where(visited, lane[:1, :], ctx[0]))
                    state = lax.fori_loop(first, ctx[0] - 1,
                                          lambda i, st, ctx=ctx: block(ctx, i, st[0], st[1], False), state)
                states.append(state)
            for u, (ctx, state) in enumerate(zip(ctxs, states)):
                if not (first_step and u == 0):
                    state = block(ctx, ctx[0] - 1, *state, False)
                state = block(ctx, ctx[0], *state, True)
                dp_ref[0, pl.ds(pl.multiple_of(ctx[0] * B, B), B), :] = state[1].astype(BF16)

        pl.when(si == 0)(lambda: whole(True))
        pl.when(si > 0)(lambda: whole(False))

        @pl.when(si == ns - 1)
        def _():
            dp_ref[1] = dk_ref[...].astype(BF16)
            dp_ref[2] = dv_ref[...].astype(BF16)

        pl.when((hd == H - 1) & (si == ns - 1))(wait_exchange)

    tq = SB_PER_STEP * B
    tile = lambda j: pl.BlockSpec((None, tq, HEAD_DIM), lambda h, i: (j, i, h))
    vec = pl.BlockSpec((1, HEAD_DIM), lambda h, i: (0, h))
    hbm = pl.BlockSpec(memory_space=pltpu.HBM)
    return pl.pallas_call(
        body, name="sb_bwd",
        out_shape=(jax.ShapeDtypeStruct((4, S, W), BF16), jax.ShapeDtypeStruct((1, W), F32),
                   jax.ShapeDtypeStruct((3,) + chip_sums_o.shape[1:], chip_sums_o.dtype)),
        grid=(H, ns),
        in_specs=[tile(4), _head_spec(S, 5), _head_spec(S, 6), tile(7),
                  pl.BlockSpec((tq, HEAD_DIM), lambda h, i: (i, h)),
                  pl.BlockSpec((tq, HEAD_DIM), lambda h, i: (i, h)),
                  pl.BlockSpec((tq, HEAD_DIM), lambda h, i: (i, H + h)), vec, hbm],
        out_specs=(pl.BlockSpec((4, S, HEAD_DIM), lambda h, i: (0, 0, h)), vec, hbm),
        scratch_shapes=[pltpu.VMEM((S, HEAD_DIM), BF16), pltpu.VMEM((S, HEAD_DIM), BF16),
                        pltpu.VMEM((S, HEAD_DIM), F32), pltpu.VMEM((S, HEAD_DIM), F32),
                        pltpu.SemaphoreType.DMA((1, 3)), pltpu.SemaphoreType.DMA((1, 3))],
        compiler_params=_params("arbitrary", "arbitrary"),
    )(proj, proj, proj, proj, raw, carries, dmix, gain, chip_sums_o)


def _grad_w_in_half(ht, dpr, dps, core, name, to_sibling=None):
    D, S = ht.shape
    _, _, W = dpr.shape
    tmm = min(512, D)
    nm = D // tmm

    def body(core_ref, ht_ref, r_ref, s_ref, *rest):
        o_ref = rest[1] if to_sibling is not None else rest[0]
        q, m = pl.program_id(0), pl.program_id(1)
        if to_sibling is not None:
            ga_ref, _, rin_ref, send_sems, recv_sems = rest
            copies = _to_sibling_copies([ga_ref.at[k] for k in range(4)], rin_ref, send_sems, recv_sems)

            @pl.when((q == 0) & (m == 0))
            def _():
                for cp in copies:
                    cp.start()

        @pl.when(q < 2)
        def _():
            o_ref[...] = _dot(ht_ref[...], r_ref[...])

        @pl.when(q >= 2)
        def _():
            o_ref[...] = _dot(ht_ref[...], s_ref[...])

        if to_sibling is not None:
            @pl.when((q == 3) & (m == nm - 1))
            def _():
                for cp in copies:
                    cp.wait_recv()
                for cp in copies:
                    cp.wait_send()

    hbm = pl.BlockSpec(memory_space=pltpu.HBM)
    gw_shape = jax.ShapeDtypeStruct((4, D, W), F32)
    out_shape, out_specs, extra_in, scratch = (gw_shape,), (pl.BlockSpec((None, tmm, W), lambda q, m, core: (q, m, 0)),), [], []
    if to_sibling is not None:
        out_shape += (gw_shape,)
        out_specs += (hbm,)
        extra_in = [hbm]
        scratch = [pltpu.SemaphoreType.DMA((4,)), pltpu.SemaphoreType.DMA((4,))]
    return pl.pallas_call(
        body, name=name, out_shape=out_shape,
        grid_spec=pltpu.PrefetchScalarGridSpec(
            num_scalar_prefetch=1, grid=(4, nm),
            in_specs=[pl.BlockSpec((tmm, S), lambda q, m, core: (m, 0)),
                      pl.BlockSpec((None, S, W), lambda q, m, core: (jnp.minimum(2 * q + core[0], 3), 0, 0)),
                      pl.BlockSpec((None, S, W), lambda q, m, core: (jnp.maximum(2 * q + core[0] - 4, 0), 0, 0))]
            + extra_in,
            out_specs=out_specs, scratch_shapes=scratch),
        compiler_params=_params("arbitrary", "arbitrary"),
    )(core, ht, dpr, dps, *(() if to_sibling is None else (to_sibling,)))


def _grad_w_out(mix_r, mix_s, dx2b):
    S, W = mix_r.shape
    D = dx2b.shape[1]
    tmm = min(512, W)
    tk = min(1024, S)

    def body(r_ref, s_ref, b_ref, o_ref):
        j, kk = pl.program_id(0), pl.program_id(2)

        def acc(a_ref):
            part = _dot_tn(a_ref[...], b_ref[...])

            @pl.when(kk == 0)
            def _():
                o_ref[...] = part

            @pl.when(kk > 0)
            def _():
                o_ref[...] += part

        pl.when(j == 0)(lambda: acc(r_ref))
        pl.when(j == 1)(lambda: acc(s_ref))

    return pl.pallas_call(
        body, name="grad_w_out", out_shape=jax.ShapeDtypeStruct((2, W, D), F32), grid=(2, W // tmm, S // tk),
        in_specs=[pl.BlockSpec((tk, tmm), lambda j, m, k: (k, m)),
                  pl.BlockSpec((tk, tmm), lambda j, m, k: (k, m)),
                  pl.BlockSpec((tk, D), lambda j, m, k: (k, 0))],
        out_specs=pl.BlockSpec((None, tmm, D), lambda j, m, k: (j, m, 0)),
        compiler_params=_params("parallel", "parallel", "arbitrary"),
    )(mix_r, mix_s, dx2b)


def _dh_norm_bwd(dpr, dps, w_all, x, dx2, gain, chip_sums):
    _, S, W = dpr.shape
    D = x.shape[1]
    tm = min(512, S)
    ni = S // tm

    def body(r_ref, s_ref, w_ref, x_ref, dx2_ref, g_ref, sa_ref, gx_ref, dgain_ref, ra_ref,
             acc_ref, send_sems, recv_sems):
        i, j = pl.program_id(0), pl.program_id(1)
        start_exchange, wait_exchange = _exchange_chip_sums((sa_ref,), (ra_ref,), send_sems, recv_sems)

        @pl.when((i == 0) & (j == 0))
        def _():
            start_exchange()
            dgain_ref[...] = jnp.zeros_like(dgain_ref)

        def acc(b_ref):
            part = _dot_nt(b_ref[...], w_ref[...])

            @pl.when(j == 0)
            def _():
                acc_ref[...] = part

            @pl.when(j > 0)
            def _():
                acc_ref[...] += part

        pl.when(j < 4)(lambda: acc(r_ref))
        pl.when(j >= 4)(lambda: acc(s_ref))

        @pl.when(j == 7)
        def _():
            xv, dh, gv = x_ref[...], acc_ref[...], g_ref[...]
            r1 = lax.rsqrt(jnp.mean(xv * xv, axis=-1, keepdims=True) + EPS)
            n = xv * r1
            dgain_ref[...] += jnp.sum(dh * n, axis=0, keepdims=True)
            dn = dh * gv
            gx_ref[...] = dx2_ref[...] + r1 * (dn - n * jnp.mean(dn * n, axis=-1, keepdims=True))

        pl.when((i == ni - 1) & (j == 7))(wait_exchange)

    row = pl.BlockSpec((tm, D), lambda i, j: (i, 0))
    one = pl.BlockSpec((1, D), lambda i, j: (0, 0))
    hbm = pl.BlockSpec(memory_space=pltpu.HBM)
    return pl.pallas_call(
        body, name="dh_norm_bwd",
        out_shape=(jax.ShapeDtypeStruct((S, D), F32), jax.ShapeDtypeStruct((1, D), F32),
                   jax.ShapeDtypeStruct((3,) + chip_sums.shape[1:], chip_sums.dtype)),
        grid=(ni, 8),
        in_specs=[pl.BlockSpec((None, tm, W), lambda i, j: (jnp.minimum(j, 3), i, 0)),
                  pl.BlockSpec((None, tm, W), lambda i, j: (jnp.maximum(j - 4, 0), i, 0)),
                  pl.BlockSpec((None, D, W), lambda i, j: (j, 0, 0)), row, row, one, hbm],
        out_specs=(row, one, hbm),
        scratch_shapes=[pltpu.VMEM((tm, D), F32), pltpu.SemaphoreType.DMA((1, 3)), pltpu.SemaphoreType.DMA((1, 3))],
        compiler_params=_params("arbitrary", "arbitrary"),
    )(dpr, dps, w_all, x, dx2, gain, chip_sums)


def _own_block(gw, pos, q):
    return q if gw.shape[0] == 4 else 2 * q + pos[0]


def _rs_local_sum(gw, rin, pos):
    _, R, C = gw.shape
    tr = min(256, R)

    def body(pos_ref, a_ref, b_ref, o_ref):
        o_ref[...] = (a_ref[...] + b_ref[...]).astype(BF16)

    return pl.pallas_call(
        body, name="rs_local_sum", out_shape=jax.ShapeDtypeStruct((4, R, C), BF16),
        grid_spec=pltpu.PrefetchScalarGridSpec(
            num_scalar_prefetch=1, grid=(4, R // tr),
            in_specs=[pl.BlockSpec((None, tr, C), lambda q, i, pos: (_own_block(gw, pos, q), i, 0)),
                      pl.BlockSpec((None, tr, C), lambda q, i, pos: (q, i, 0))],
            out_specs=pl.BlockSpec((None, tr, C), lambda q, i, pos: (q, i, 0))),
        compiler_params=_params("parallel", "parallel"),
    )(pos, gw, rin)


def _adamw(w, g, m, v):
    m2 = ADAM_B1 * m + (1.0 - ADAM_B1) * g
    v2 = ADAM_B2 * v + (1.0 - ADAM_B2) * (g * g)
    m_hat = m2 / (1.0 - ADAM_B1 ** ADAM_STEP)
    v_hat = v2 / (1.0 - ADAM_B2 ** ADAM_STEP)
    delta = -ADAM_LR * (m_hat / (jnp.sqrt(v_hat) + ADAM_EPS) + ADAM_WD * w)
    return delta, m2, v2


def _adamw_shard(gw, rin, rb, w, m, v, pos):
    _, R, C = gw.shape
    tr = min(256, R)

    def body(pos_ref, a_ref, b_ref, rb_ref, w_ref, m_ref, v_ref, g_ref, d_ref, m2_ref, v2_ref):
        g = a_ref[...] + b_ref[...]
        for k in range(3):
            g = g + rb_ref[k].astype(F32)
        g_ref[...] = g
        d_ref[...], m2_ref[...], v2_ref[...] = _adamw(w_ref[...], g, m_ref[...], v_ref[...])

    plain = pl.BlockSpec((tr, C), lambda i, pos: (i, 0))
    shape = jax.ShapeDtypeStruct((R, C), F32)
    return pl.pallas_call(
        body, name="adamw_shard", out_shape=(shape,) * 4,
        grid_spec=pltpu.PrefetchScalarGridSpec(
            num_scalar_prefetch=1, grid=(R // tr,),
            in_specs=[pl.BlockSpec((None, tr, C), lambda i, pos: (_own_block(gw, pos, pos[1]), i, 0)),
                      pl.BlockSpec((None, tr, C), lambda i, pos: (pos[1], i, 0)),
                      pl.BlockSpec((3, tr, C), lambda i, pos: (0, i, 0)), plain, plain, plain],
            out_specs=(plain,) * 4),
        compiler_params=_params("parallel"),
    )(pos, gw, rin, rb, w, m, v)


def _adamw_small(parts, w, m, v):
    _, rows, n = parts.shape

    def body(p_ref, w_ref, m_ref, v_ref, g_ref, d_ref, m2_ref, v2_ref):
        g = p_ref[0]
        for d in range(1, N_DEV):
            g = g + p_ref[d]
        g_ref[...] = g
        d_ref[...], m2_ref[...], v2_ref[...] = _adamw(w_ref[...], g, m_ref[...], v_ref[...])

    shape = jax.ShapeDtypeStruct((rows, n), F32)
    return pl.pallas_call(body, name="adamw_small", out_shape=(shape,) * 4)(parts, w, m, v)


def _rope_tables(S):
    half = HEAD_DIM // 2
    inv = ROPE_THETA ** (-jnp.arange(half, dtype=F32) / half)
    ang = jnp.arange(S, dtype=F32)[:, None] * inv[None, :]
    cos, sin = jnp.cos(ang), jnp.sin(ang)
    return jnp.concatenate([cos, cos], axis=1), jnp.concatenate([-sin, sin], axis=1)


def _retention_tables(H):
    lg = jnp.log1p(-jnp.exp2(-5.0 - jnp.arange(H, dtype=F32)))
    n = jnp.arange(CHUNK, dtype=F32)
    rel = n[:, None] - n[None, :]
    decay = jnp.where(rel >= 0, jnp.exp(lg[:, None, None] * jnp.maximum(rel, 0.0)), 0.0)
    shape = (H, CHUNK, HEAD_DIM)
    xi = jnp.broadcast_to(jnp.exp(lg[:, None] * (n + 1.0))[:, :, None], shape)
    zeta = jnp.broadcast_to(jnp.exp(lg[:, None] * (CHUNK - 1.0 - n))[:, :, None], shape)
    gamma_c = jnp.broadcast_to(jnp.exp(lg * CHUNK)[:, None, None], shape)
    return decay, xi, zeta, gamma_c


def _pack_small(parts):
    flat = []
    for p in parts:
        p = p.reshape(-1)
        flat.append(jnp.pad(p, (0, -p.shape[0] % LANES)))
    flat = jnp.concatenate(flat)
    return jnp.pad(flat, (0, SMALL_N - flat.shape[0])).reshape(SUBLANES, SMALL_N // SUBLANES)


def _unpack_small(packed, shapes):
    flat = packed.reshape(-1)
    out, at = [], 0
    for shp in shapes:
        size = 1
        for s in shp:
            size *= s
        out.append(flat[at:at + size].reshape(shp))
        at += size + (-size % LANES)
    return out


def kernel(x, norm_gain, w_in, ret_gn_gain, ret_gn_bias, sb_norm_gain, w_out, final_norm_gain, loss_target, m_norm_gain, m_w_in, m_ret_gn_gain, m_ret_gn_bias, m_sb_norm_gain, m_w_out, m_final_norm_gain, v_norm_gain, v_w_in, v_ret_gn_gain, v_ret_gn_bias, v_sb_norm_gain, v_w_out, v_final_norm_gain):
    S, D = x.shape[1], x.shape[2]
    W = w_in.shape[2]
    wo_rows = w_out.shape[1]
    H = W // HEAD_DIM
    xs, tgt = x[0], loss_target[0]
    mx, my, mc = _mesh_pos()
    pos = jnp.stack([mc, 2 * mx + my]).astype(jnp.int32)

    cos, sin = _rope_tables(S)
    tabs = _retention_tables(H)

    h, ht = _rmsnorm_fwd(xs, norm_gain)
    proj, w_all = _in_proj_gather(h, w_in[0].astype(BF16), cos, sin, _gather_order())
    mix_r = _ret_fwd(proj, tabs, ret_gn_gain, ret_gn_bias)
    mix_s, raw_s, carries, wo_all = _sb_fwd(proj, sb_norm_gain, w_out[0].astype(BF16))
    wo_full = wo_all.reshape(N_DEV * wo_rows, D)
    dx2, dx2b, dmix, loss_p, d_gf = _out_proj_loss(mix_r, mix_s, wo_full, xs, tgt, final_norm_gain[None])

    gwo = _grad_w_out(mix_r, mix_s, dx2b).reshape(N_DEV, wo_rows, D)
    dpr, d_rgain, d_rbias, rino = _ret_bwd(proj, dmix, tabs, ret_gn_gain, ret_gn_bias, cos, sin, gwo)
    dps, d_sgain, rbo = _sb_bwd(proj, raw_s, carries, dmix, sb_norm_gain, _rs_local_sum(gwo, rino, pos))
    gw_sibling, = _grad_w_in_half(ht, dpr, dps, (1 - mc).reshape(1).astype(jnp.int32), "grad_w_in_sibling")
    gw, rin = _grad_w_in_half(ht, dpr, dps, mc.reshape(1).astype(jnp.int32), "grad_w_in_own", to_sibling=gw_sibling)
    grad_x, d_gain, rb = _dh_norm_bwd(dpr, dps, w_all, xs, dx2, norm_gain, _rs_local_sum(gw, rin, pos))
    g_in, d_in, m_in, v_in = _adamw_shard(gw, rin, rb, w_in[0], m_w_in[0], v_w_in[0], pos)
    g_out, d_out, m_out, v_out = _adamw_shard(gwo, rino, rbo, w_out[0], m_w_out[0], v_w_out[0], pos)

    small_w = [norm_gain, ret_gn_gain, ret_gn_bias, sb_norm_gain, final_norm_gain]
    small_m = [m_norm_gain, m_ret_gn_gain, m_ret_gn_bias, m_sb_norm_gain, m_final_norm_gain]
    small_v = [v_norm_gain, v_ret_gn_gain, v_ret_gn_bias, v_sb_norm_gain, v_final_norm_gain]
    shapes = [()] + [w.shape for w in small_w]
    zero = jnp.zeros((), F32)
    parts = _small_all_gather(_pack_small([loss_p[0, 0], d_gain, d_rgain, d_rbias, d_sgain, d_gf]))
    packed = _adamw_small(parts, _pack_small([zero] + small_w), _pack_small([zero] + small_m),
                          _pack_small([zero] + small_v))
    g_s, d_s, m_s, v_s = (_unpack_small(p, shapes) for p in packed)

    grads = [g_s[1], g_in[None], g_s[2], g_s[3], g_s[4], g_out[None], g_s[5]]
    deltas = [d_s[1], d_in[None], d_s[2], d_s[3], d_s[4], d_out[None], d_s[5]]
    new_m = [m_s[1], m_in[None], m_s[2], m_s[3], m_s[4], m_out[None], m_s[5]]
    new_v = [v_s[1], v_in[None], v_s[2], v_s[3], v_s[4], v_out[None], v_s[5]]
    return (g_s[0], grad_x[None], *grads, *deltas, *new_m, *new_v)
```

```python
import functools

import jax
import jax.numpy as jnp
from jax import lax
from jax.experimental import pallas as pl
from jax.experimental.pallas import tpu as pltpu

F32 = jnp.float32
BF16 = jnp.bfloat16

HEAD_DIM = 128
CHUNK = 128
RET_UNROLL = 4
ROPE_THETA = 10000.0
EPS = 1e-6
ADAM_LR = 0.001
ADAM_B1 = 0.9
ADAM_B2 = 0.999
ADAM_EPS = 1e-08
ADAM_WD = 0.01
ADAM_STEP = 10

N_DEV = 8
LANES = 128
SUBLANES = 8
VMEM_LIMIT = 56 * 1024 * 1024
SB_BLOCK = 256
SB_PER_STEP = 2
SMALL_N = 8192
EXP_IS_ZERO_BELOW = -104.0
NOT_VISITED = -1e30
MESH = pl.DeviceIdType.MESH

NT = (((1,), (1,)), ((), ()))
TN = (((0,), (0,)), ((), ()))


def _params(*sem):
    return pltpu.CompilerParams(dimension_semantics=sem if sem else None, vmem_limit_bytes=VMEM_LIMIT)


def _dot(a, b):
    return jnp.dot(a, b, preferred_element_type=F32)


def _dot_nt(a, b):
    return lax.dot_general(a, b, NT, preferred_element_type=F32)


def _dot_tn(a, b):
    return lax.dot_general(a, b, TN, preferred_element_type=F32)


def _sigmoid(g):
    return 1.0 / (1.0 + jnp.exp(-g))


def _rot(a, cos, sin_signed):
    return a * cos + pltpu.roll(a, HEAD_DIM // 2, 1) * sin_signed


def _mesh_pos():
    return lax.axis_index("x"), lax.axis_index("y"), lax.axis_index("c")


def _to_sibling_copies(blocks, out_ref, send_sems, recv_sems):
    x, y, c = _mesh_pos()
    return [pltpu.make_async_remote_copy(
        src_ref=block, dst_ref=out_ref.at[k], send_sem=send_sems.at[k], recv_sem=recv_sems.at[k],
        device_id=(x, y, 1 - c), device_id_type=MESH) for k, block in enumerate(blocks)]


def _exchange_chip_sums(srcs, outs, send_sems, recv_sems):
    x, y, c = _mesh_pos()
    copies = []
    for arr, (src, out) in enumerate(zip(srcs, outs)):
        for k in range(1, 4):
            px = 1 - x if k & 2 else x
            py = 1 - y if k & 1 else y
            copies.append(pltpu.make_async_remote_copy(
                src_ref=src.at[2 * px + py], dst_ref=out.at[k - 1],
                send_sem=send_sems.at[arr, k - 1], recv_sem=recv_sems.at[arr, k - 1],
                device_id=(px, py, c), device_id_type=MESH))

    def start():
        for cp in copies:
            cp.start()

    def wait():
        for cp in copies:
            cp.wait_recv()
        for cp in copies:
            cp.wait_send()

    return start, wait


def _small_all_gather(small):
    rows, n = small.shape

    def body(s_ref, o_ref, send_sems, recv_sems, local_sem):
        start, wait = _exchange_with_all(s_ref, o_ref, send_sems, recv_sems, local_sem)
        start()
        wait()

    vmem = pl.BlockSpec(memory_space=pltpu.VMEM)
    return pl.pallas_call(
        body, name="small_all_gather",
        out_shape=jax.ShapeDtypeStruct((N_DEV, rows, n), small.dtype),
        in_specs=[vmem], out_specs=vmem,
        scratch_shapes=[pltpu.SemaphoreType.DMA((N_DEV - 1,)), pltpu.SemaphoreType.DMA((N_DEV - 1,)),
                        pltpu.SemaphoreType.DMA],
    )(small)


GATHER_SPLIT = 2
GATHER_STEPS = ([("own", 0, p) for p in range(GATHER_SPLIT)] + [("sibling", 0, p) for p in range(GATHER_SPLIT)]
                + [(kind, j, p) for p in range(GATHER_SPLIT) for kind in ("ici", "passed") for j in range(3)])


def _gather_order():
    x, y, c = _mesh_pos()
    chips = [(1 - x, y), (x, 1 - y), (1 - x, 1 - y)]
    owner = {"own": lambda j: (x, y, c), "sibling": lambda j: (x, y, 1 - c),
             "ici": lambda j: (*chips[j], c), "passed": lambda j: (*chips[j], 1 - c)}
    blocks = [4 * px + 2 * py + pc for px, py, pc in (owner[kind](j) for kind, j, _ in GATHER_STEPS)]
    return (jnp.stack(blocks).astype(jnp.int32), jnp.array([p for _, _, p in GATHER_STEPS], jnp.int32))


def _in_proj_gather(x, gain, w_shard, cos, sin, order):
    S, D = x.shape
    W = w_shard.shape[1]
    wp = W // GATHER_SPLIT
    tm = min(1024, S)
    ni = S // tm
    n_steps = len(GATHER_STEPS)

    def body(blk_ref, piece_ref, x_ref, g_ref, w_ref, cos_ref, sin_ref, o_ref, wall_ref, ht_ref, h_scr, wbuf,
             send_sems, recv_sems, local_sem, load_sem):
        step, i = pl.program_id(0), pl.program_id(1)
        mx, my, c = _mesh_pos()
        me, sibling = (mx, my, c), (mx, my, 1 - c)
        chips = [(1 - mx, my), (mx, 1 - my), (1 - mx, 1 - my)]

        def piece_of(dev, p):
            px, py, pc = dev
            return wall_ref.at[4 * px + 2 * py + pc, :, pl.ds(p * wp, wp)]

        def copy(k, p, block, to, own=False):
            dst = piece_of(block, p)
            return pltpu.make_async_remote_copy(
                src_ref=w_ref.at[:, pl.ds(p * wp, wp)] if own else dst, dst_ref=dst,
                send_sem=send_sems.at[k, p], recv_sem=recv_sems.at[k, p], device_id=to, device_id_type=MESH)

        def load(src):
            cp = pltpu.make_async_copy(src, wbuf, load_sem)
            cp.start()
            cp.wait()

        pieces = range(GATHER_SPLIT)
        first = [cp for p in pieces for cp in
                 [copy(0, p, me, sibling, own=True)] + [copy(1 + j, p, me, (*chip, c), own=True)
                                                        for j, chip in enumerate(chips)]]
        passed = {(j, p): copy(4 + j, p, (*chip, c), sibling) for j, chip in enumerate(chips) for p in pieces}
        mine = pltpu.make_async_copy(w_ref, wall_ref.at[4 * mx + 2 * my + c], local_sem)

        @pl.when(i == 0)
        def _():
            for s, (kind, j, p) in enumerate(GATHER_STEPS):
                @pl.when(step == s)
                def _(s=s, kind=kind, j=j, p=p):
                    if s == 0:
                        for cp in first:
                            cp.start()
                        mine.start()
                    if kind == "own":
                        load(w_ref.at[:, pl.ds(p * wp, wp)])
                    elif kind == "sibling":
                        copy(0, p, sibling, me).wait_recv()
                        load(piece_of(sibling, p))
                    elif kind == "ici":
                        copy(1 + j, p, (*chips[j], c), me).wait_recv()
                        passed[j, p].start()
                        load(piece_of((*chips[j], c), p))
                    else:
                        copy(4 + j, p, (*chips[j], 1 - c), me).wait_recv()
                        load(piece_of((*chips[j], 1 - c), p))

        rows = pl.ds(pl.multiple_of(i * tm, tm), tm)

        @pl.when(step == 0)
        def _():
            xv = x_ref[...]
            r = lax.rsqrt(jnp.mean(xv * xv, axis=-1, keepdims=True) + EPS)
            hv = xv * r * g_ref[...]
            h_scr[rows, :] = hv.astype(BF16)
            ht_ref[...] = hv.T.astype(BF16)

        acc = _dot(h_scr[rows, :], wbuf[...])
        b = blk_ref[step]

        @pl.when(b >= 2)
        def _():
            o_ref[...] = acc

        @pl.when(b < 2)
        def _():
            scale = jnp.where(b == 1, HEAD_DIM ** -0.5, 1.0).astype(F32)
            cs, sn = cos_ref[...], sin_ref[...]
            for hh in range(wp // HEAD_DIM):
                cols = slice(hh * HEAD_DIM, (hh + 1) * HEAD_DIM)
                o_ref[:, cols] = _rot(acc[:, cols], cs, sn) * scale

        @pl.when((step == n_steps - 1) & (i == ni - 1))
        def _():
            for cp in first + list(passed.values()):
                cp.wait_send()
            mine.wait()

    hbm = pl.BlockSpec(memory_space=pltpu.HBM)
    rope = pl.BlockSpec((tm, HEAD_DIM), lambda s, i, blk, piece: (i, 0))
    first_pass = lambda s, i: jnp.where(s == 0, i, ni - 1)
    return pl.pallas_call(
        body, name="in_proj_gather",
        out_shape=(jax.ShapeDtypeStruct((N_DEV, S, W), F32), jax.ShapeDtypeStruct((N_DEV, D, W), BF16),
                   jax.ShapeDtypeStruct((D, S), BF16)),
        grid_spec=pltpu.PrefetchScalarGridSpec(
            num_scalar_prefetch=2, grid=(n_steps, ni),
            in_specs=[pl.BlockSpec((tm, D), lambda s, i, blk, piece: (first_pass(s, i), 0)),
                      pl.BlockSpec((1, D), lambda s, i, blk, piece: (0, 0)), hbm, rope, rope],
            out_specs=(pl.BlockSpec((None, tm, wp), lambda s, i, blk, piece: (blk[s], i, piece[s])), hbm,
                       pl.BlockSpec((D, tm), lambda s, i, blk, piece: (0, first_pass(s, i)))),
            scratch_shapes=[pltpu.VMEM((S, D), BF16), pltpu.VMEM((D, wp), BF16),
                            pltpu.SemaphoreType.DMA((7, GATHER_SPLIT)), pltpu.SemaphoreType.DMA((7, GATHER_SPLIT)),
                            pltpu.SemaphoreType.DMA, pltpu.SemaphoreType.DMA]),
        compiler_params=_params("arbitrary", "arbitrary"),
    )(*order, x, gain, w_shard, cos, sin)


def _head_spec(S, j):
    return pl.BlockSpec((None, S, HEAD_DIM), lambda h, *_: (j, 0, h))


def _ret_chunk(q, k, vb, r_prev, dec, xi, ze):
    qb, kb = q.astype(BF16), k.astype(BF16)
    sb = (_dot_nt(qb, kb) * dec).astype(BF16)
    qx = (q * xi).astype(BF16)
    kz = (k * ze).astype(BF16)
    out = _dot(sb, vb) + _dot(qx, r_prev.astype(BF16))
    return out, _dot_tn(kz, vb), (qb, kb, sb, qx, kz)


def _chunk_loop(nc, step, init, unroll=RET_UNROLL):
    def trip(i, state):
        for u in range(unroll):
            state = step(i * unroll + u, state)
        return state

    assert nc % unroll == 0
    return lax.fori_loop(0, nc // unroll, trip, init)


def _table_specs():
    return [pl.BlockSpec((None, CHUNK, HEAD_DIM), lambda h, *_: (h, 0, 0))] * 4


def _ret_fwd(proj, tabs, gn_gain, gn_bias):
    _, S, W = proj.shape
    H, nc = W // HEAD_DIM, S // CHUNK

    def body(q_ref, k_ref, v_ref, g_ref, dec_ref, xi_ref, ze_ref, gam_ref, gain_ref, bias_ref, o_ref):
        dec, xi, ze, gam = dec_ref[...], xi_ref[...], ze_ref[...], gam_ref[...]
        gain, bias = gain_ref[...], bias_ref[...]

        def step(c, state):
            rows = pl.ds(pl.multiple_of(c * CHUNK, CHUNK), CHUNK)
            out, kv, _ = _ret_chunk(q_ref[rows, :], k_ref[rows, :], v_ref[rows, :].astype(BF16), state,
                                    dec, xi, ze)
            mu = jnp.mean(out, axis=-1, keepdims=True)
            d = out - mu
            yn = d * lax.rsqrt(jnp.mean(d * d, axis=-1, keepdims=True) + EPS)
            g = g_ref[rows, :]
            o_ref[rows, :] = (g * _sigmoid(g) * (yn * gain + bias)).astype(BF16)
            return gam * state + kv

        _chunk_loop(nc, step, jnp.zeros((HEAD_DIM, HEAD_DIM), F32))

    vec = pl.BlockSpec((1, HEAD_DIM), lambda h: (0, h))
    return pl.pallas_call(
        body, name="ret_fwd", out_shape=jax.ShapeDtypeStruct((S, W), BF16), grid=(H,),
        in_specs=[_head_spec(S, 0), _head_spec(S, 1), _head_spec(S, 2), _head_spec(S, 3)] + _table_specs() + [vec, vec],
        out_specs=pl.BlockSpec((S, HEAD_DIM), lambda h: (0, h)),
        compiler_params=_params("parallel"),
    )(proj, proj, proj, proj, *tabs, gn_gain, gn_bias)


def _sb_scores(qb, kk, masked, causal, upper):
    z = _dot_nt(qb, kk) * (HEAD_DIM ** -0.5)
    e = jnp.exp(-jnp.abs(z))
    l1p = jnp.log1p(e)
    log_beta = jnp.minimum(z, 0.0) - l1p
    lk = jnp.minimum(-z, 0.0) - l1p
    if masked:
        lk = jnp.where(causal, lk, 0.0)
    hi = lk.astype(BF16)
    lo = (lk - hi.astype(F32)).astype(BF16)
    cs = _dot(hi, upper) + _dot(lo, upper)
    return z, e, log_beta, lk, cs


def _tri(B, kind):
    r = lax.broadcasted_iota(jnp.int32, (B, B), 0)
    c = lax.broadcasted_iota(jnp.int32, (B, B), 1)
    return {"gt": r > c, "lt": r < c}[kind]


def _ones_where(mask):
    return jnp.where(mask, 1.0, 0.0).astype(BF16)


def _exchange_with_all(src_ref, out_ref, send_sems, recv_sems, local_sem):
    x, y, c = _mesh_pos()
    peers = [(1 - x if k & 4 else x, 1 - y if k & 2 else y, 1 - c if k & 1 else c) for k in range(1, N_DEV)]

    def copy(k, owner, to):
        px, py, pc = owner
        return pltpu.make_async_remote_copy(
            src_ref=src_ref, dst_ref=out_ref.at[4 * px + 2 * py + pc], send_sem=send_sems.at[k],
            recv_sem=recv_sems.at[k], device_id=to, device_id_type=MESH)

    sends = [copy(k, (x, y, c), p) for k, p in enumerate(peers)]
    mine = pltpu.make_async_copy(src_ref, out_ref.at[4 * x + 2 * y + c], local_sem)

    def start():
        for cp in sends:
            cp.start()
        mine.start()

    def wait():
        for k, p in enumerate(peers):
            copy(k, p, p).wait_recv()
        for cp in sends:
            cp.wait_send()
        mine.wait()

    return start, wait


def _sb_fwd(proj, gain, wo_shard):
    _, S, W = proj.shape
    H = W // HEAD_DIM
    B = min(SB_BLOCK, S)
    nq = S // B
    assert nq <= HEAD_DIM and nq % SB_PER_STEP == 0
    ns = nq // SB_PER_STEP

    def body(q_ref, k_ref, v_ref, g_ref, gain_ref, wo_ref, mix_ref, raw_ref, car_ref, woall_ref, kb_ref, vb_ref,
             send_sems, recv_sems, local_sem):
        hd, si = pl.program_id(0), pl.program_id(1)
        start_gather, wait_gather = _exchange_with_all(wo_ref, woall_ref, send_sems, recv_sems, local_sem)
        pl.when((hd == 0) & (si == 0))(start_gather)

        @pl.when(si == 0)
        def _():
            kb_ref[...] = k_ref[...].astype(BF16)
            vb_ref[...] = v_ref[...].astype(BF16)

        causal = _tri(B, "gt")
        upper = _ones_where(causal)
        lane = lax.broadcasted_iota(jnp.int32, (B, HEAD_DIM), 1)

        def block(qb, kb, carry, acc, saved, masked):
            rows = pl.ds(pl.multiple_of(kb * B, B), B)
            _, _, log_beta, lk, cs = _sb_scores(qb, kb_ref[rows, :], masked, causal, upper)
            a = jnp.exp(log_beta + cs + carry)
            if masked:
                a = jnp.where(causal, a, 0.0)
            acc = acc + _dot(a.astype(BF16), vb_ref[rows, :])
            return carry + jnp.sum(lk, axis=1, keepdims=True), acc, jnp.where(lane == kb, carry, saved)

        init = (jnp.zeros((B, 1), F32), jnp.zeros((B, HEAD_DIM), F32), jnp.full((B, HEAD_DIM), NOT_VISITED, F32))

        def live(st):
            return (st[0] >= 0) & (jnp.max(st[1]) >= EXP_IS_ZERO_BELOW)

        def finish(u, acc, saved):
            rows = slice(u * B, (u + 1) * B)
            raw_ref[rows, :] = acc
            car_ref[rows, :] = saved
            yn = acc * lax.rsqrt(jnp.mean(acc * acc, axis=-1, keepdims=True) + EPS)
            g = g_ref[rows, :]
            mix_ref[rows, :] = (g * _sigmoid(g) * (yn * gain_ref[...])).astype(BF16)

        def whole(first_step):
            heads = []
            for u in range(SB_PER_STEP):
                qi = si * SB_PER_STEP + u
                qb = q_ref[u * B:(u + 1) * B, :].astype(BF16)
                state = block(qb, qi, *init, True)
                if not (first_step and u == 0):
                    state = block(qb, qi - 1, *state, False)
                heads.append((qi, qb, state))
            for u, (qi, qb, state) in enumerate(heads):
                if not (first_step and u == 0):
                    state = lax.while_loop(
                        live, lambda st, qb=qb: (st[0] - 1,) + block(qb, st[0], st[1], st[2], st[3], False),
                        (qi - 2,) + state)[1:]
                finish(u, state[1], state[2])

        pl.when(si == 0)(lambda: whole(True))
        pl.when(si > 0)(lambda: whole(False))
        pl.when((hd == H - 1) & (si == ns - 1))(wait_gather)

    tq = SB_PER_STEP * B
    tile = lambda j: pl.BlockSpec((None, tq, HEAD_DIM), lambda h, i: (j, i, h))
    out_tile = pl.BlockSpec((tq, HEAD_DIM), lambda h, i: (i, h))
    hbm = pl.BlockSpec(memory_space=pltpu.HBM)
    return pl.pallas_call(
        body, name="sb_fwd",
        out_shape=(jax.ShapeDtypeStruct((S, W), BF16), jax.ShapeDtypeStruct((S, W), F32),
                   jax.ShapeDtypeStruct((S, W), F32), jax.ShapeDtypeStruct((N_DEV,) + wo_shard.shape, BF16)),
        grid=(H, ns),
        in_specs=[tile(4), _head_spec(S, 5), _head_spec(S, 6), tile(7),
                  pl.BlockSpec((1, HEAD_DIM), lambda h, i: (0, h)), hbm],
        out_specs=(out_tile, out_tile, out_tile, hbm),
        scratch_shapes=[pltpu.VMEM((S, HEAD_DIM), BF16), pltpu.VMEM((S, HEAD_DIM), BF16),
                        pltpu.SemaphoreType.DMA((N_DEV - 1,)), pltpu.SemaphoreType.DMA((N_DEV - 1,)),
                        pltpu.SemaphoreType.DMA],
        compiler_params=_params("arbitrary", "arbitrary"),
    )(proj, proj, proj, proj, gain, wo_shard)


def _out_proj_loss(mix_r, mix_s, w_out, x, tgt, gf):
    S, W = mix_r.shape
    D = x.shape[1]
    tm = min(256, S)

    def body(mr_ref, ms_ref, wo_ref, x_ref, t_ref, gf_ref, dx2_ref, dx2b_ref, dmix_ref, loss_ref, gfn_ref):
        @pl.when(pl.program_id(0) == 0)
        def _():
            loss_ref[...] = jnp.zeros_like(loss_ref)
            gfn_ref[...] = jnp.zeros_like(gfn_ref)

        gfv = gf_ref[...]
        x2 = x_ref[...] + (_dot(mr_ref[...], wo_ref[:W, :]) + _dot(ms_ref[...], wo_ref[W:, :]))
        r2 = lax.rsqrt(jnp.mean(x2 * x2, axis=-1, keepdims=True) + EPS)
        n = x2 * r2
        err = n * gfv - t_ref[...]
        loss_ref[...] += 0.5 * jnp.sum(jnp.mean(err * err, axis=-1, keepdims=True))
        dy = err * (1.0 / D)
        gfn_ref[...] += jnp.sum(dy * n, axis=0, keepdims=True)
        dn = dy * gfv
        dx2 = r2 * (dn - n * jnp.mean(dn * n, axis=-1, keepdims=True))
        dx2_ref[...] = dx2
        b = dx2.astype(BF16)
        dx2b_ref[...] = b
        dmix_ref[:, :W] = _dot_nt(b, wo_ref[:W, :])
        dmix_ref[:, W:] = _dot_nt(b, wo_ref[W:, :])

    row = lambda width: pl.BlockSpec((tm, width), lambda i: (i, 0))
    return pl.pallas_call(
        body, name="out_proj_loss",
        out_shape=(jax.ShapeDtypeStruct((S, D), F32), jax.ShapeDtypeStruct((S, D), BF16),
                   jax.ShapeDtypeStruct((S, 2 * W), F32), jax.ShapeDtypeStruct((SUBLANES, LANES), F32),
                   jax.ShapeDtypeStruct((1, D), F32)),
        grid=(S // tm,),
        in_specs=[row(W), row(W), pl.BlockSpec((2 * W, D), lambda i: (0, 0)), row(D), row(D),
                  pl.BlockSpec((1, D), lambda i: (0, 0))],
        out_specs=(row(D), row(D), row(2 * W), pl.BlockSpec((SUBLANES, LANES), lambda i: (0, 0)),
                   pl.BlockSpec((1, D), lambda i: (0, 0))),
        compiler_params=_params("arbitrary"),
    )(mix_r, mix_s, w_out, x, tgt, gf)


def _silu_bwd(g, dm, normed):
    sig = _sigmoid(g)
    return dm * (g * sig), dm * normed * (sig * (1.0 + g * (1.0 - sig)))


def _ret_bwd(proj, dmix, tabs, gn_gain, gn_bias, cos, sin, gwo):
    _, S, W = proj.shape
    H, nc = W // HEAD_DIM, S // CHUNK

    def body(q_ref, k_ref, v_ref, g_ref, dm_ref, dec_ref, xi_ref, ze_ref, gam_ref, gain_ref, bias_ref, cos_ref,
             sin_ref, gwo_ref, dp_ref, dgain_ref, dbias_ref, rino_ref, rs_ref, send_sems, recv_sems):
        dec, xi, ze, gam = dec_ref[...], xi_ref[...], ze_ref[...], gam_ref[...]
        gain, bias = gain_ref[...], bias_ref[...]
        hd = pl.program_id(0)
        other_core = 1 - lax.axis_index("c")
        copies = _to_sibling_copies([gwo_ref.at[2 * k + other_core] for k in range(4)], rino_ref, send_sems, recv_sems)

        @pl.when(hd == 0)
        def _():
            for cp in copies:
                cp.start()

        def fwd_step(c, state):
            rows = pl.ds(pl.multiple_of(c * CHUNK, CHUNK), CHUNK)
            rs_ref[c] = state
            kz = (k_ref[rows, :] * ze).astype(BF16)
            return gam * state + _dot_tn(kz, v_ref[rows, :].astype(BF16))

        _chunk_loop(nc, fwd_step, jnp.zeros((HEAD_DIM, HEAD_DIM), F32))

        def bwd_step(i, carry):
            dgain, dbias, dstate = carry
            c = nc - 1 - i
            rows = pl.ds(pl.multiple_of(c * CHUNK, CHUNK), CHUNK)
            q, k, g = q_ref[rows, :], k_ref[rows, :], g_ref[rows, :]
            vb = v_ref[rows, :].astype(BF16)
            rb = rs_ref[c].astype(BF16)
            out, _, (qb, kb, sb, qx, kz) = _ret_chunk(q, k, vb, rs_ref[c], dec, xi, ze)
            mu = jnp.mean(out, axis=-1, keepdims=True)
            d = out - mu
            rstd = lax.rsqrt(jnp.mean(d * d, axis=-1, keepdims=True) + EPS)
            yn = d * rstd
            dgn, dg = _silu_bwd(g, dm_ref[rows, :], yn * gain + bias)
            dgain = dgain + jnp.sum(dgn * yn, axis=0, keepdims=True)
            dbias = dbias + jnp.sum(dgn, axis=0, keepdims=True)
            dyn = dgn * gain
            do = rstd * (dyn - jnp.mean(dyn, axis=-1, keepdims=True)
                         - yn * jnp.mean(dyn * yn, axis=-1, keepdims=True))
            dob = do.astype(BF16)
            drb = dstate.astype(BF16)
            dv = _dot_tn(sb, dob) + _dot(kz, drb)
            dsb = (_dot_nt(dob, vb) * dec).astype(BF16)
            dq = _dot(dsb, kb) + _dot_nt(dob, rb) * xi
            dk = _dot_tn(dsb, qb) + _dot_nt(vb, drb) * ze
            cs, sn = cos_ref[rows, :], -sin_ref[rows, :]
            dp_ref[0, rows, :] = _rot(dq, cs, sn).astype(BF16)
            dp_ref[1, rows, :] = (_rot(dk, cs, sn) * (HEAD_DIM ** -0.5)).astype(BF16)
            dp_ref[2, rows, :] = dv.astype(BF16)
            dp_ref[3, rows, :] = dg.astype(BF16)
            return dgain, dbias, gam * dstate + _dot_tn(qx, dob)

        zero = jnp.zeros((1, HEAD_DIM), F32)
        dgain, dbias, _ = _chunk_loop(nc, bwd_step, (zero, zero, jnp.zeros((HEAD_DIM, HEAD_DIM), F32)))
        dgain_ref[...] = dgain
        dbias_ref[...] = dbias

        @pl.when(hd == H - 1)
        def _():
            for cp in copies:
                cp.wait_recv()
            for cp in copies:
                cp.wait_send()

    vec = pl.BlockSpec((1, HEAD_DIM), lambda h: (0, h))
    full = pl.BlockSpec((S, HEAD_DIM), lambda h: (0, 0))
    hbm = pl.BlockSpec(memory_space=pltpu.HBM)
    return pl.pallas_call(
        body, name="ret_bwd",
        out_shape=(jax.ShapeDtypeStruct((4, S, W), BF16), jax.ShapeDtypeStruct((1, W), F32),
                   jax.ShapeDtypeStruct((1, W), F32), jax.ShapeDtypeStruct((4,) + gwo.shape[1:], gwo.dtype)),
        grid=(H,),
        in_specs=[_head_spec(S, 0), _head_spec(S, 1), _head_spec(S, 2), _head_spec(S, 3),
                  pl.BlockSpec((S, HEAD_DIM), lambda h: (0, h))] + _table_specs() + [vec, vec, full, full, hbm],
        out_specs=(pl.BlockSpec((4, S, HEAD_DIM), lambda h: (0, 0, h)), vec, vec, hbm),
        scratch_shapes=[pltpu.VMEM((nc, HEAD_DIM, HEAD_DIM), F32), pltpu.SemaphoreType.DMA((4,)),
                        pltpu.SemaphoreType.DMA((4,))],
        compiler_params=_params("arbitrary"),
    )(proj, proj, proj, proj, dmix, *tabs, gn_gain, gn_bias, cos, sin, gwo)


def _sb_bwd(proj, raw, carries, dmix, gain, chip_sums_o):
    _, S, W = proj.shape
    H = W // HEAD_DIM
    B = min(SB_BLOCK, S)
    nq = S // B
    ns = nq // SB_PER_STEP

    def body(q_ref, k_ref, v_ref, g_ref, raw_ref, car_ref, dm_ref, gain_ref, so_ref, dp_ref, dgain_ref, ro_ref,
             kb_ref, vb_ref, dk_ref, dv_ref, send_sems, recv_sems):
        hd, si = pl.program_id(0), pl.program_id(1)
        start_exchange, wait_exchange = _exchange_chip_sums((so_ref,), (ro_ref,), send_sems, recv_sems)
        pl.when((hd == 0) & (si == 0))(start_exchange)

        @pl.when(si == 0)
        def _():
            kb_ref[...] = k_ref[...].astype(BF16)
            vb_ref[...] = v_ref[...].astype(BF16)
            dk_ref[...] = jnp.zeros_like(dk_ref)
            dv_ref[...] = jnp.zeros_like(dv_ref)
            dgain_ref[...] = jnp.zeros_like(dgain_ref)

        causal = _tri(B, "gt")
        upper = _ones_where(causal)
        before = _ones_where(_tri(B, "lt"))
        lane = lax.broadcasted_iota(jnp.int32, (B, HEAD_DIM), 1)
        gain_v = gain_ref[...]

        def prologue(u):
            qi = si * SB_PER_STEP + u
            rows = slice(u * B, (u + 1) * B)
            o = raw_ref[rows, :]
            rstd = lax.rsqrt(jnp.mean(o * o, axis=-1, keepdims=True) + EPS)
            yn = o * rstd
            dnrm, dg = _silu_bwd(g_ref[rows, :], dm_ref[rows, :], yn * gain_v)
            dp_ref[3, pl.ds(pl.multiple_of(qi * B, B), B), :] = dg.astype(BF16)
            dgain_ref[...] += jnp.sum(dnrm * yn, axis=0, keepdims=True)
            dyn = dnrm * gain_v
            do = rstd * (dyn - yn * jnp.mean(dyn * yn, axis=-1, keepdims=True))
            return qi, q_ref[rows, :].astype(BF16), do.astype(BF16), car_ref[rows, :]

        def block(ctx, kb, carry_g, dq, masked):
            _, qb, dob, saved = ctx
            rows = pl.ds(pl.multiple_of(kb * B, B), B)
            kk, vv = kb_ref[rows, :], vb_ref[rows, :]
            z, e, log_beta, _, cs = _sb_scores(qb, kk, masked, causal, upper)
            carry_lk = jnp.sum(jnp.where(lane == kb, saved, 0.0), axis=1, keepdims=True)
            a = jnp.exp(log_beta + cs + carry_lk)
            if masked:
                a = jnp.where(causal, a, 0.0)
            gmat = _dot_nt(dob, vv) * a
            dv_ref[rows, :] += _dot_tn(a.astype(BF16), dob)
            hi = gmat.astype(BF16)
            lo = (gmat - hi.astype(F32)).astype(BF16)
            dlk = carry_g + (_dot(hi, before) + _dot(lo, before))
            r = 1.0 / (1.0 + e)
            er = e * r
            pos = z >= 0.0
            dz = (gmat * jnp.where(pos, er, r) - dlk * jnp.where(pos, r, er)) * (HEAD_DIM ** -0.5)
            if masked:
                dz = jnp.where(causal, dz, 0.0)
            dzb = dz.astype(BF16)
            dk_ref[rows, :] += _dot_tn(dzb, qb)
            return carry_g + jnp.sum(gmat, axis=1, keepdims=True), dq + _dot(dzb, kk)

        init = (jnp.zeros((B, 1), F32), jnp.zeros((B, HEAD_DIM), F32))

        def whole(first_step):
            ctxs = [prologue(u) for u in range(SB_PER_STEP)]
            states = []
            for u, ctx in enumerate(ctxs):
                state = init
                if not (first_step and u == 0):
                    visited = jnp.max(ctx[3], axis=0, keepdims=True) >= EXP_IS_ZERO_BELOW
                    first = jnp.min(jnp.where(visited, lane[:1, :], ctx[0]))
                    state = lax.fori_loop(first, ctx[0] - 1,
                                          lambda i, st, ctx=ctx: block(ctx, i, st[0], st[1], False), state)
                states.append(state)
            for u, (ctx, state) in enumerate(zip(ctxs, states)):
                if not (first_step and u == 0):
                    state = block(ctx, ctx[0] - 1, *state, False)
                state = block(ctx, ctx[0], *state, True)
                dp_ref[0, pl.ds(pl.multiple_of(ctx[0] * B, B), B), :] = state[1].astype(BF16)

        pl.when(si == 0)(lambda: whole(True))
        pl.when(si > 0)(lambda: whole(False))

        @pl.when(si == ns - 1)
        def _():
            dp_ref[1] = dk_ref[...].astype(BF16)
            dp_ref[2] = dv_ref[...].astype(BF16)

        pl.when((hd == H - 1) & (si == ns - 1))(wait_exchange)

    tq = SB_PER_STEP * B
    tile = lambda j: pl.BlockSpec((None, tq, HEAD_DIM), lambda h, i: (j, i, h))
    vec = pl.BlockSpec((1, HEAD_DIM), lambda h, i: (0, h))
    hbm = pl.BlockSpec(memory_space=pltpu.HBM)
    return pl.pallas_call(
        body, name="sb_bwd",
        out_shape=(jax.ShapeDtypeStruct((4, S, W), BF16), jax.ShapeDtypeStruct((1, W), F32),
                   jax.ShapeDtypeStruct((3,) + chip_sums_o.shape[1:], chip_sums_o.dtype)),
        grid=(H, ns),
        in_specs=[tile(4), _head_spec(S, 5), _head_spec(S, 6), tile(7),
                  pl.BlockSpec((tq, HEAD_DIM), lambda h, i: (i, h)),
                  pl.BlockSpec((tq, HEAD_DIM), lambda h, i: (i, h)),
                  pl.BlockSpec((tq, HEAD_DIM), lambda h, i: (i, H + h)), vec, hbm],
        out_specs=(pl.BlockSpec((4, S, HEAD_DIM), lambda h, i: (0, 0, h)), vec, hbm),
        scratch_shapes=[pltpu.VMEM((S, HEAD_DIM), BF16), pltpu.VMEM((S, HEAD_DIM), BF16),
                        pltpu.VMEM((S, HEAD_DIM), F32), pltpu.VMEM((S, HEAD_DIM), F32),
                        pltpu.SemaphoreType.DMA((1, 3)), pltpu.SemaphoreType.DMA((1, 3))],
        compiler_params=_params("arbitrary", "arbitrary"),
    )(proj, proj, proj, proj, raw, carries, dmix, gain, chip_sums_o)


def _grad_w_in_half(ht, dpr, dps, core, name, to_sibling=None):
    D, S = ht.shape
    _, _, W = dpr.shape
    tmm = min(512, D)
    nm = D // tmm

    def body(core_ref, ht_ref, r_ref, s_ref, *rest):
        o_ref = rest[1] if to_sibling is not None else rest[0]
        q, m = pl.program_id(0), pl.program_id(1)
        if to_sibling is not None:
            ga_ref, _, rin_ref, send_sems, recv_sems = rest
            copies = _to_sibling_copies([ga_ref.at[k] for k in range(4)], rin_ref, send_sems, recv_sems)

            @pl.when((q == 0) & (m == 0))
            def _():
                for cp in copies:
                    cp.start()

        @pl.when(q < 2)
        def _():
            o_ref[...] = _dot(ht_ref[...], r_ref[...])

        @pl.when(q >= 2)
        def _():
            o_ref[...] = _dot(ht_ref[...], s_ref[...])

        if to_sibling is not None:
            @pl.when((q == 3) & (m == nm - 1))
            def _():
                for cp in copies:
                    cp.wait_recv()
                for cp in copies:
                    cp.wait_send()

    hbm = pl.BlockSpec(memory_space=pltpu.HBM)
    gw_shape = jax.ShapeDtypeStruct((4, D, W), F32)
    out_shape, out_specs, extra_in, scratch = (gw_shape,), (pl.BlockSpec((None, tmm, W), lambda q, m, core: (q, m, 0)),), [], []
    if to_sibling is not None:
        out_shape += (gw_shape,)
        out_specs += (hbm,)
        extra_in = [hbm]
        scratch = [pltpu.SemaphoreType.DMA((4,)), pltpu.SemaphoreType.DMA((4,))]
    return pl.pallas_call(
        body, name=name, out_shape=out_shape,
        grid_spec=pltpu.PrefetchScalarGridSpec(
            num_scalar_prefetch=1, grid=(4, nm),
            in_specs=[pl.BlockSpec((tmm, S), lambda q, m, core: (m, 0)),
                      pl.BlockSpec((None, S, W), lambda q, m, core: (jnp.minimum(2 * q + core[0], 3), 0, 0)),
                      pl.BlockSpec((None, S, W), lambda q, m, core: (jnp.maximum(2 * q + core[0] - 4, 0), 0, 0))]
            + extra_in,
            out_specs=out_specs, scratch_shapes=scratch),
        compiler_params=_params("arbitrary", "arbitrary"),
    )(core, ht, dpr, dps, *(() if to_sibling is None else (to_sibling,)))


def _grad_w_out(mix_r, mix_s, dx2b):
    S, W = mix_r.shape
    D = dx2b.shape[1]
    tmm = min(512, W)
    tk = min(1024, S)

    def body(r_ref, s_ref, b_ref, o_ref):
        j, kk = pl.program_id(0), pl.program_id(2)

        def acc(a_ref):
            part = _dot_tn(a_ref[...], b_ref[...])

            @pl.when(kk == 0)
            def _():
                o_ref[...] = part

            @pl.when(kk > 0)
            def _():
                o_ref[...] += part

        pl.when(j == 0)(lambda: acc(r_ref))
        pl.when(j == 1)(lambda: acc(s_ref))

    return pl.pallas_call(
        body, name="grad_w_out", out_shape=jax.ShapeDtypeStruct((2, W, D), F32), grid=(2, W // tmm, S // tk),
        in_specs=[pl.BlockSpec((tk, tmm), lambda j, m, k: (k, m)),
                  pl.BlockSpec((tk, tmm), lambda j, m, k: (k, m)),
                  pl.BlockSpec((tk, D), lambda j, m, k: (k, 0))],
        out_specs=pl.BlockSpec((None, tmm, D), lambda j, m, k: (j, m, 0)),
        compiler_params=_params("parallel", "parallel", "arbitrary"),
    )(mix_r, mix_s, dx2b)


def _dh_norm_bwd(dpr, dps, w_all, x, dx2, gain, chip_sums):
    _, S, W = dpr.shape
    D = x.shape[1]
    tm = min(512, S)
    ni = S // tm

    def body(r_ref, s_ref, w_ref, x_ref, dx2_ref, g_ref, sa_ref, gx_ref, dgain_ref, ra_ref,
             acc_ref, send_sems, recv_sems):
        i, j = pl.program_id(0), pl.program_id(1)
        start_exchange, wait_exchange = _exchange_chip_sums((sa_ref,), (ra_ref,), send_sems, recv_sems)

        @pl.when((i == 0) & (j == 0))
        def _():
            start_exchange()
            dgain_ref[...] = jnp.zeros_like(dgain_ref)

        def acc(b_ref):
            part = _dot_nt(b_ref[...], w_ref[...])

            @pl.when(j == 0)
            def _():
                acc_ref[...] = part

            @pl.when(j > 0)
            def _():
                acc_ref[...] += part

        pl.when(j < 4)(lambda: acc(r_ref))
        pl.when(j >= 4)(lambda: acc(s_ref))

        @pl.when(j == 7)
        def _():
            xv, dh, gv = x_ref[...], acc_ref[...], g_ref[...]
            r1 = lax.rsqrt(jnp.mean(xv * xv, axis=-1, keepdims=True) + EPS)
            n = xv * r1
            dgain_ref[...] += jnp.sum(dh * n, axis=0, keepdims=True)
            dn = dh * gv
            gx_ref[...] = dx2_ref[...] + r1 * (dn - n * jnp.mean(dn * n, axis=-1, keepdims=True))

        pl.when((i == ni - 1) & (j == 7))(wait_exchange)

    row = pl.BlockSpec((tm, D), lambda i, j: (i, 0))
    one = pl.BlockSpec((1, D), lambda i, j: (0, 0))
    hbm = pl.BlockSpec(memory_space=pltpu.HBM)
    return pl.pallas_call(
        body, name="dh_norm_bwd",
        out_shape=(jax.ShapeDtypeStruct((S, D), F32), jax.ShapeDtypeStruct((1, D), F32),
                   jax.ShapeDtypeStruct((3,) + chip_sums.shape[1:], chip_sums.dtype)),
        grid=(ni, 8),
        in_specs=[pl.BlockSpec((None, tm, W), lambda i, j: (jnp.minimum(j, 3), i, 0)),
                  pl.BlockSpec((None, tm, W), lambda i, j: (jnp.maximum(j - 4, 0), i, 0)),
                  pl.BlockSpec((None, D, W), lambda i, j: (j, 0, 0)), row, row, one, hbm],
        out_specs=(row, one, hbm),
        scratch_shapes=[pltpu.VMEM((tm, D), F32), pltpu.SemaphoreType.DMA((1, 3)), pltpu.SemaphoreType.DMA((1, 3))],
        compiler_params=_params("arbitrary", "arbitrary"),
    )(dpr, dps, w_all, x, dx2, gain, chip_sums)


def _own_block(gw, pos, q):
    return q if gw.shape[0] == 4 else 2 * q + pos[0]


def _rs_local_sum(gw, rin, pos):
    _, R, C = gw.shape
    tr = min(256, R)

    def body(pos_ref, a_ref, b_ref, o_ref):
        o_ref[...] = (a_ref[...] + b_ref[...]).astype(BF16)

    return pl.pallas_call(
        body, name="rs_local_sum", out_shape=jax.ShapeDtypeStruct((4, R, C), BF16),
        grid_spec=pltpu.PrefetchScalarGridSpec(
            num_scalar_prefetch=1, grid=(4, R // tr),
            in_specs=[pl.BlockSpec((None, tr, C), lambda q, i, pos: (_own_block(gw, pos, q), i, 0)),
                      pl.BlockSpec((None, tr, C), lambda q, i, pos: (q, i, 0))],
            out_specs=pl.BlockSpec((None, tr, C), lambda q, i, pos: (q, i, 0))),
        compiler_params=_params("parallel", "parallel"),
    )(pos, gw, rin)


def _adamw(w, g, m, v):
    m2 = ADAM_B1 * m + (1.0 - ADAM_B1) * g
    v2 = ADAM_B2 * v + (1.0 - ADAM_B2) * (g * g)
    m_hat = m2 / (1.0 - ADAM_B1 ** ADAM_STEP)
    v_hat = v2 / (1.0 - ADAM_B2 ** ADAM_STEP)
    delta = -ADAM_LR * (m_hat / (jnp.sqrt(v_hat) + ADAM_EPS) + ADAM_WD * w)
    return delta, m2, v2


def _adamw_shard(gw, rin, rb, w, m, v, pos):
    _, R, C = gw.shape
    tr = min(256, R)

    def body(pos_ref, a_ref, b_ref, rb_ref, w_ref, m_ref, v_ref, g_ref, d_ref, m2_ref, v2_ref):
        g = a_ref[...] + b_ref[...]
        for k in range(3):
            g = g + rb_ref[k].astype(F32)
        g_ref[...] = g
        d_ref[...], m2_ref[...], v2_ref[...] = _adamw(w_ref[...], g, m_ref[...], v_ref[...])

    plain = pl.BlockSpec((tr, C), lambda i, pos: (i, 0))
    shape = jax.ShapeDtypeStruct((R, C), F32)
    return pl.pallas_call(
        body, name="adamw_shard", out_shape=(shape,) * 4,
        grid_spec=pltpu.PrefetchScalarGridSpec(
            num_scalar_prefetch=1, grid=(R // tr,),
            in_specs=[pl.BlockSpec((None, tr, C), lambda i, pos: (_own_block(gw, pos, pos[1]), i, 0)),
                      pl.BlockSpec((None, tr, C), lambda i, pos: (pos[1], i, 0)),
                      pl.BlockSpec((3, tr, C), lambda i, pos: (0, i, 0)), plain, plain, plain],
            out_specs=(plain,) * 4),
        compiler_params=_params("parallel"),
    )(pos, gw, rin, rb, w, m, v)


def _adamw_small(parts, w, m, v):
    _, rows, n = parts.shape

    def body(p_ref, w_ref, m_ref, v_ref, g_ref, d_ref, m2_ref, v2_ref):
        g = p_ref[0]
        for d in range(1, N_DEV):
            g = g + p_ref[d]
        g_ref[...] = g
        d_ref[...], m2_ref[...], v2_ref[...] = _adamw(w_ref[...], g, m_ref[...], v_ref[...])

    shape = jax.ShapeDtypeStruct((rows, n), F32)
    return pl.pallas_call(body, name="adamw_small", out_shape=(shape,) * 4)(parts, w, m, v)


def _rope_tables(S):
    half = HEAD_DIM // 2
    inv = ROPE_THETA ** (-jnp.arange(half, dtype=F32) / half)
    ang = jnp.arange(S, dtype=F32)[:, None] * inv[None, :]
    cos, sin = jnp.cos(ang), jnp.sin(ang)
    return jnp.concatenate([cos, cos], axis=1), jnp.concatenate([-sin, sin], axis=1)


def _retention_tables(H):
    lg = jnp.log1p(-jnp.exp2(-5.0 - jnp.arange(H, dtype=F32)))
    n = jnp.arange(CHUNK, dtype=F32)
    rel = n[:, None] - n[None, :]
    decay = jnp.where(rel >= 0, jnp.exp(lg[:, None, None] * jnp.maximum(rel, 0.0)), 0.0)
    shape = (H, CHUNK, HEAD_DIM)
    xi = jnp.broadcast_to(jnp.exp(lg[:, None] * (n + 1.0))[:, :, None], shape)
    zeta = jnp.broadcast_to(jnp.exp(lg[:, None] * (CHUNK - 1.0 - n))[:, :, None], shape)
    gamma_c = jnp.broadcast_to(jnp.exp(lg * CHUNK)[:, None, None], shape)
    return decay, xi, zeta, gamma_c


def _pack_small(parts):
    flat = []
    for p in parts:
        p = p.reshape(-1)
        flat.append(jnp.pad(p, (0, -p.shape[0] % LANES)))
    flat = jnp.concatenate(flat)
    return jnp.pad(flat, (0, SMALL_N - flat.shape[0])).reshape(SUBLANES, SMALL_N // SUBLANES)


def _unpack_small(packed, shapes):
    flat = packed.reshape(-1)
    out, at = [], 0
    for shp in shapes:
        size = 1
        for s in shp:
            size *= s
        out.append(flat[at:at + size].reshape(shp))
        at += size + (-size % LANES)
    return out


def kernel(x, norm_gain, w_in, ret_gn_gain, ret_gn_bias, sb_norm_gain, w_out, final_norm_gain, loss_target, m_norm_gain, m_w_in, m_ret_gn_gain, m_ret_gn_bias, m_sb_norm_gain, m_w_out, m_final_norm_gain, v_norm_gain, v_w_in, v_ret_gn_gain, v_ret_gn_bias, v_sb_norm_gain, v_w_out, v_final_norm_gain):
    S, D = x.shape[1], x.shape[2]
    W = w_in.shape[2]
    wo_rows = w_out.shape[1]
    H = W // HEAD_DIM
    xs, tgt = x[0], loss_target[0]
    mx, my, mc = _mesh_pos()
    pos = jnp.stack([mc, 2 * mx + my]).astype(jnp.int32)

    cos, sin = _rope_tables(S)
    tabs = _retention_tables(H)

    proj, w_all, ht = _in_proj_gather(xs, norm_gain, w_in[0].astype(BF16), cos, sin, _gather_order())
    mix_r = _ret_fwd(proj, tabs, ret_gn_gain, ret_gn_bias)
    mix_s, raw_s, carries, wo_all = _sb_fwd(proj, sb_norm_gain, w_out[0].astype(BF16))
    wo_full = wo_all.reshape(N_DEV * wo_rows, D)
    dx2, dx2b, dmix, loss_p, d_gf = _out_proj_loss(mix_r, mix_s, wo_full, xs, tgt, final_norm_gain[None])

    gwo = _grad_w_out(mix_r, mix_s, dx2b).reshape(N_DEV, wo_rows, D)
    dpr, d_rgain, d_rbias, rino = _ret_bwd(proj, dmix, tabs, ret_gn_gain, ret_gn_bias, cos, sin, gwo)
    dps, d_sgain, rbo = _sb_bwd(proj, raw_s, carries, dmix, sb_norm_gain, _rs_local_sum(gwo, rino, pos))
    gw_sibling, = _grad_w_in_half(ht, dpr, dps, (1 - mc).reshape(1).astype(jnp.int32), "grad_w_in_sibling")
    gw, rin = _grad_w_in_half(ht, dpr, dps, mc.reshape(1).astype(jnp.int32), "grad_w_in_own", to_sibling=gw_sibling)
    grad_x, d_gain, rb = _dh_norm_bwd(dpr, dps, w_all, xs, dx2, norm_gain, _rs_local_sum(gw, rin, pos))
    g_in, d_in, m_in, v_in = _adamw_shard(gw, rin, rb, w_in[0], m_w_in[0], v_w_in[0], pos)
    g_out, d_out, m_out, v_out = _adamw_shard(gwo, rino, rbo, w_out[0], m_w_out[0], v_w_out[0], pos)

    small_w = [norm_gain, ret_gn_gain, ret_gn_bias, sb_norm_gain, final_norm_gain]
    small_m = [m_norm_gain, m_ret_gn_gain, m_ret_gn_bias, m_sb_norm_gain, m_final_norm_gain]
    small_v = [v_norm_gain, v_ret_gn_gain, v_ret_gn_bias, v_sb_norm_gain, v_final_norm_gain]
    shapes = [()] + [w.shape for w in small_w]
    zero = jnp.zeros((), F32)
    parts = _small_all_gather(_pack_small([loss_p[0, 0], d_gain, d_rgain, d_rbias, d_sgain, d_gf]))
    packed = _adamw_small(parts, _pack_small([zero] + small_w), _pack_small([zero] + small_m),
                          _pack_small([zero] + small_v))
    g_s, d_s, m_s, v_s = (_unpack_small(p, shapes) for p in packed)

    grads = [g_s[1], g_in[None], g_s[2], g_s[3], g_s[4], g_out[None], g_s[5]]
    deltas = [d_s[1], d_in[None], d_s[2], d_s[3], d_s[4], d_out[None], d_s[5]]
    new_m = [m_s[1], m_in[None], m_s[2], m_s[3], m_s[4], m_out[None], m_s[5]]
    new_v = [v_s[1], v_in[None], v_s[2], v_s[3], v_s[4], v_out[None], v_s[5]]
    return (g_s[0], grad_x[None], *grads, *deltas, *new_m, *new_v)
```

```python
import functools

import jax
import jax.numpy as jnp
from jax import lax
from jax.experimental import pallas as pl
from jax.experimental.pallas import tpu as pltpu

F32 = jnp.float32
BF16 = jnp.bfloat16

HEAD_DIM = 128
CHUNK = 128
RET_UNROLL = 4
ROPE_THETA = 10000.0
EPS = 1e-6
ADAM_LR = 0.001
ADAM_B1 = 0.9
ADAM_B2 = 0.999
ADAM_EPS = 1e-08
ADAM_WD = 0.01
ADAM_STEP = 10

N_DEV = 8
LANES = 128
SUBLANES = 8
VMEM_LIMIT = 56 * 1024 * 1024
SB_BLOCK = 256
SB_PER_STEP = 2
SMALL_N = 8192
EXP_IS_ZERO_BELOW = -104.0
NOT_VISITED = -1e30
MESH = pl.DeviceIdType.MESH

NT = (((1,), (1,)), ((), ()))
TN = (((0,), (0,)), ((), ()))


def _params(*sem):
    return pltpu.CompilerParams(dimension_semantics=sem if sem else None, vmem_limit_bytes=VMEM_LIMIT)


def _dot(a, b):
    return jnp.dot(a, b, preferred_element_type=F32)


def _dot_nt(a, b):
    return lax.dot_general(a, b, NT, preferred_element_type=F32)


def _dot_tn(a, b):
    return lax.dot_general(a, b, TN, preferred_element_type=F32)


def _sigmoid(g):
    return 1.0 / (1.0 + jnp.exp(-g))


def _rot(a, cos, sin_signed):
    return a * cos + pltpu.roll(a, HEAD_DIM // 2, 1) * sin_signed


def _mesh_pos():
    return lax.axis_index("x"), lax.axis_index("y"), lax.axis_index("c")


def _to_sibling_copies(blocks, out_ref, send_sems, recv_sems):
    x, y, c = _mesh_pos()
    return [pltpu.make_async_remote_copy(
        src_ref=block, dst_ref=out_ref.at[k], send_sem=send_sems.at[k], recv_sem=recv_sems.at[k],
        device_id=(x, y, 1 - c), device_id_type=MESH) for k, block in enumerate(blocks)]


def _exchange_chip_sums(srcs, outs, send_sems, recv_sems):
    x, y, c = _mesh_pos()
    copies = []
    for arr, (src, out) in enumerate(zip(srcs, outs)):
        for k in range(1, 4):
            px = 1 - x if k & 2 else x
            py = 1 - y if k & 1 else y
            copies.append(pltpu.make_async_remote_copy(
                src_ref=src.at[2 * px + py], dst_ref=out.at[k - 1],
                send_sem=send_sems.at[arr, k - 1], recv_sem=recv_sems.at[arr, k - 1],
                device_id=(px, py, c), device_id_type=MESH))

    def start():
        for cp in copies:
            cp.start()

    def wait():
        for cp in copies:
            cp.wait_recv()
        for cp in copies:
            cp.wait_send()

    return start, wait


def _small_all_gather(small):
    rows, n = small.shape

    def body(s_ref, o_ref, send_sems, recv_sems, local_sem):
        start, wait = _exchange_with_all(s_ref, o_ref, send_sems, recv_sems, local_sem)
        start()
        wait()

    vmem = pl.BlockSpec(memory_space=pltpu.VMEM)
    return pl.pallas_call(
        body, name="small_all_gather",
        out_shape=jax.ShapeDtypeStruct((N_DEV, rows, n), small.dtype),
        in_specs=[vmem], out_specs=vmem,
        scratch_shapes=[pltpu.SemaphoreType.DMA((N_DEV - 1,)), pltpu.SemaphoreType.DMA((N_DEV - 1,)),
                        pltpu.SemaphoreType.DMA],
    )(small)


GATHER_SPLIT = 2
GATHER_STEPS = ([("own", 0, p) for p in range(GATHER_SPLIT)] + [("sibling", 0, p) for p in range(GATHER_SPLIT)]
                + [(kind, j, p) for p in range(GATHER_SPLIT) for kind in ("ici", "passed") for j in range(3)])


def _gather_order():
    x, y, c = _mesh_pos()
    chips = [(1 - x, y), (x, 1 - y), (1 - x, 1 - y)]
    owner = {"own": lambda j: (x, y, c), "sibling": lambda j: (x, y, 1 - c),
             "ici": lambda j: (*chips[j], c), "passed": lambda j: (*chips[j], 1 - c)}
    blocks = [4 * px + 2 * py + pc for px, py, pc in (owner[kind](j) for kind, j, _ in GATHER_STEPS)]
    return (jnp.stack(blocks).astype(jnp.int32), jnp.array([p for _, _, p in GATHER_STEPS], jnp.int32))


def _in_proj_gather(x, gain, w_shard, cos, sin, order):
    S, D = x.shape
    W = w_shard.shape[1]
    wp = W // GATHER_SPLIT
    tm = min(1024, S)
    ni = S // tm
    n_steps = len(GATHER_STEPS)

    def body(blk_ref, piece_ref, x_ref, g_ref, w_ref, cos_ref, sin_ref, o_ref, wall_ref, ht_ref, h_scr, wbuf,
             send_sems, recv_sems, local_sem, load_sem):
        step, i = pl.program_id(0), pl.program_id(1)
        mx, my, c = _mesh_pos()
        me, sibling = (mx, my, c), (mx, my, 1 - c)
        chips = [(1 - mx, my), (mx, 1 - my), (1 - mx, 1 - my)]

        def piece_of(dev, p):
            px, py, pc = dev
            return wall_ref.at[4 * px + 2 * py + pc, :, pl.ds(p * wp, wp)]

        def copy(k, p, block, to, own=False):
            dst = piece_of(block, p)
            return pltpu.make_async_remote_copy(
                src_ref=w_ref.at[:, pl.ds(p * wp, wp)] if own else dst, dst_ref=dst,
                send_sem=send_sems.at[k, p], recv_sem=recv_sems.at[k, p], device_id=to, device_id_type=MESH)

        def load(src):
            cp = pltpu.make_async_copy(src, wbuf, load_sem)
            cp.start()
            cp.wait()

        pieces = range(GATHER_SPLIT)
        first = [cp for p in pieces for cp in
                 [copy(0, p, me, sibling, own=True)] + [copy(1 + j, p, me, (*chip, c), own=True)
                                                        for j, chip in enumerate(chips)]]
        passed = {(j, p): copy(4 + j, p, (*chip, c), sibling) for j, chip in enumerate(chips) for p in pieces}
        mine = pltpu.make_async_copy(w_ref, wall_ref.at[4 * mx + 2 * my + c], local_sem)

        @pl.when(i == 0)
        def _():
            for s, (kind, j, p) in enumerate(GATHER_STEPS):
                @pl.when(step == s)
                def _(s=s, kind=kind, j=j, p=p):
                    if s == 0:
                        for cp in first:
                            cp.start()
                        mine.start()
                    if kind == "own":
                        load(w_ref.at[:, pl.ds(p * wp, wp)])
                    elif kind == "sibling":
                        copy(0, p, sibling, me).wait_recv()
                        load(piece_of(sibling, p))
                    elif kind == "ici":
                        copy(1 + j, p, (*chips[j], c), me).wait_recv()
                        passed[j, p].start()
                        load(piece_of((*chips[j], c), p))
                    else:
                        copy(4 + j, p, (*chips[j], 1 - c), me).wait_recv()
                        load(piece_of((*chips[j], 1 - c), p))

        rows = pl.ds(pl.multiple_of(i * tm, tm), tm)

        @pl.when(step == 0)
        def _():
            xv = x_ref[...]
            r = lax.rsqrt(jnp.mean(xv * xv, axis=-1, keepdims=True) + EPS)
            hv = xv * r * g_ref[...]
            h_scr[rows, :] = hv.astype(BF16)
            ht_ref[...] = hv.T.astype(BF16)

        acc = _dot(h_scr[rows, :], wbuf[...])
        b = blk_ref[step]

        @pl.when(b >= 2)
        def _():
            o_ref[...] = acc

        @pl.when(b < 2)
        def _():
            scale = jnp.where(b == 1, HEAD_DIM ** -0.5, 1.0).astype(F32)
            cs, sn = cos_ref[...], sin_ref[...]
            for hh in range(wp // HEAD_DIM):
                cols = slice(hh * HEAD_DIM, (hh + 1) * HEAD_DIM)
                o_ref[:, cols] = _rot(acc[:, cols], cs, sn) * scale

        @pl.when((step == n_steps - 1) & (i == ni - 1))
        def _():
            for cp in first + list(passed.values()):
                cp.wait_send()
            mine.wait()

    hbm = pl.BlockSpec(memory_space=pltpu.HBM)
    rope = pl.BlockSpec((tm, HEAD_DIM), lambda s, i, blk, piece: (i, 0))
    first_pass = lambda s, i: jnp.where(s == 0, i, ni - 1)
    return pl.pallas_call(
        body, name="in_proj_gather",
        out_shape=(jax.ShapeDtypeStruct((N_DEV, S, W), F32), jax.ShapeDtypeStruct((N_DEV, D, W), BF16),
                   jax.ShapeDtypeStruct((D, S), BF16)),
        grid_spec=pltpu.PrefetchScalarGridSpec(
            num_scalar_prefetch=2, grid=(n_steps, ni),
            in_specs=[pl.BlockSpec((tm, D), lambda s, i, blk, piece: (first_pass(s, i), 0)),
                      pl.BlockSpec((1, D), lambda s, i, blk, piece: (0, 0)), hbm, rope, rope],
            out_specs=(pl.BlockSpec((None, tm, wp), lambda s, i, blk, piece: (blk[s], i, piece[s])), hbm,
                       pl.BlockSpec((D, tm), lambda s, i, blk, piece: (0, first_pass(s, i)))),
            scratch_shapes=[pltpu.VMEM((S, D), BF16), pltpu.VMEM((D, wp), BF16),
                            pltpu.SemaphoreType.DMA((7, GATHER_SPLIT)), pltpu.SemaphoreType.DMA((7, GATHER_SPLIT)),
                            pltpu.SemaphoreType.DMA, pltpu.SemaphoreType.DMA]),
        compiler_params=_params("arbitrary", "arbitrary"),
    )(*order, x, gain, w_shard, cos, sin)


def _head_spec(S, j):
    return pl.BlockSpec((None, S, HEAD_DIM), lambda h, *_: (j, 0, h))


def _ret_chunk(q, k, vb, r_prev, dec, xi, ze):
    qb, kb = q.astype(BF16), k.astype(BF16)
    sb = (_dot_nt(qb, kb) * dec).astype(BF16)
    qx = (q * xi).astype(BF16)
    kz = (k * ze).astype(BF16)
    out = _dot(sb, vb) + _dot(qx, r_prev.astype(BF16))
    return out, _dot_tn(kz, vb), (qb, kb, sb, qx, kz)


def _chunk_loop(nc, step, init, unroll=RET_UNROLL):
    def trip(i, state):
        for u in range(unroll):
            state = step(i * unroll + u, state)
        return state

    assert nc % unroll == 0
    return lax.fori_loop(0, nc // unroll, trip, init)


def _table_specs():
    return [pl.BlockSpec((None, CHUNK, HEAD_DIM), lambda h, *_: (h, 0, 0))] * 4


def _ret_fwd(proj, tabs, gn_gain, gn_bias):
    _, S, W = proj.shape
    H, nc = W // HEAD_DIM, S // CHUNK

    def body(q_ref, k_ref, v_ref, g_ref, dec_ref, xi_ref, ze_ref, gam_ref, gain_ref, bias_ref, o_ref):
        dec, xi, ze, gam = dec_ref[...], xi_ref[...], ze_ref[...], gam_ref[...]
        gain, bias = gain_ref[...], bias_ref[...]

        def step(c, state):
            rows = pl.ds(pl.multiple_of(c * CHUNK, CHUNK), CHUNK)
            out, kv, _ = _ret_chunk(q_ref[rows, :], k_ref[rows, :], v_ref[rows, :].astype(BF16), state,
                                    dec, xi, ze)
            mu = jnp.mean(out, axis=-1, keepdims=True)
            d = out - mu
            yn = d * lax.rsqrt(jnp.mean(d * d, axis=-1, keepdims=True) + EPS)
            g = g_ref[rows, :]
            o_ref[rows, :] = (g * _sigmoid(g) * (yn * gain + bias)).astype(BF16)
            return gam * state + kv

        _chunk_loop(nc, step, jnp.zeros((HEAD_DIM, HEAD_DIM), F32))

    vec = pl.BlockSpec((1, HEAD_DIM), lambda h: (0, h))
    return pl.pallas_call(
        body, name="ret_fwd", out_shape=jax.ShapeDtypeStruct((S, W), BF16), grid=(H,),
        in_specs=[_head_spec(S, 0), _head_spec(S, 1), _head_spec(S, 2), _head_spec(S, 3)] + _table_specs() + [vec, vec],
        out_specs=pl.BlockSpec((S, HEAD_DIM), lambda h: (0, h)),
        compiler_params=_params("parallel"),
    )(proj, proj, proj, proj, *tabs, gn_gain, gn_bias)


def _sb_scores(qb, kk, masked, causal, upper):
    z = _dot_nt(qb, kk) * (HEAD_DIM ** -0.5)
    e = jnp.exp(-jnp.abs(z))
    l1p = jnp.log(1.0 + e)
    log_beta = jnp.minimum(z, 0.0) - l1p
    lk = jnp.minimum(-z, 0.0) - l1p
    if masked:
        lk = jnp.where(causal, lk, 0.0)
    hi = lk.astype(BF16)
    lo = (lk - hi.astype(F32)).astype(BF16)
    cs = _dot(hi, upper) + _dot(lo, upper)
    return log_beta, lk, cs


def _tri(B, kind):
    r = lax.broadcasted_iota(jnp.int32, (B, B), 0)
    c = lax.broadcasted_iota(jnp.int32, (B, B), 1)
    return {"gt": r > c, "lt": r < c}[kind]


def _ones_where(mask):
    return jnp.where(mask, 1.0, 0.0).astype(BF16)


def _exchange_with_all(src_ref, out_ref, send_sems, recv_sems, local_sem):
    x, y, c = _mesh_pos()
    peers = [(1 - x if k & 4 else x, 1 - y if k & 2 else y, 1 - c if k & 1 else c) for k in range(1, N_DEV)]

    def copy(k, owner, to):
        px, py, pc = owner
        return pltpu.make_async_remote_copy(
            src_ref=src_ref, dst_ref=out_ref.at[4 * px + 2 * py + pc], send_sem=send_sems.at[k],
            recv_sem=recv_sems.at[k], device_id=to, device_id_type=MESH)

    sends = [copy(k, (x, y, c), p) for k, p in enumerate(peers)]
    mine = pltpu.make_async_copy(src_ref, out_ref.at[4 * x + 2 * y + c], local_sem)

    def start():
        for cp in sends:
            cp.start()
        mine.start()

    def wait():
        for k, p in enumerate(peers):
            copy(k, p, p).wait_recv()
        for cp in sends:
            cp.wait_send()
        mine.wait()

    return start, wait


def _sb_fwd(proj, gain, wo_shard):
    _, S, W = proj.shape
    H = W // HEAD_DIM
    B = min(SB_BLOCK, S)
    nq = S // B
    assert nq <= HEAD_DIM and nq % SB_PER_STEP == 0
    ns = nq // SB_PER_STEP

    def body(q_ref, k_ref, v_ref, g_ref, gain_ref, wo_ref, mix_ref, raw_ref, car_ref, woall_ref, kb_ref, vb_ref,
             send_sems, recv_sems, local_sem):
        hd, si = pl.program_id(0), pl.program_id(1)
        start_gather, wait_gather = _exchange_with_all(wo_ref, woall_ref, send_sems, recv_sems, local_sem)
        pl.when((hd == 0) & (si == 0))(start_gather)

        @pl.when(si == 0)
        def _():
            kb_ref[...] = k_ref[...].astype(BF16)
            vb_ref[...] = v_ref[...].astype(BF16)

        causal = _tri(B, "gt")
        upper = _ones_where(causal)
        lane = lax.broadcasted_iota(jnp.int32, (B, HEAD_DIM), 1)

        def block(qb, kb, carry, acc, saved, masked):
            rows = pl.ds(pl.multiple_of(kb * B, B), B)
            log_beta, lk, cs = _sb_scores(qb, kb_ref[rows, :], masked, causal, upper)
            a = jnp.exp(log_beta + cs + carry)
            if masked:
                a = jnp.where(causal, a, 0.0)
            acc = acc + _dot(a.astype(BF16), vb_ref[rows, :])
            return carry + jnp.sum(lk, axis=1, keepdims=True), acc, jnp.where(lane == kb, carry, saved)

        init = (jnp.zeros((B, 1), F32), jnp.zeros((B, HEAD_DIM), F32), jnp.full((B, HEAD_DIM), NOT_VISITED, F32))

        def live(st):
            return (st[0] >= 0) & (jnp.max(st[1]) >= EXP_IS_ZERO_BELOW)

        def finish(u, acc, saved):
            rows = slice(u * B, (u + 1) * B)
            raw_ref[rows, :] = acc
            car_ref[rows, :] = saved
            yn = acc * lax.rsqrt(jnp.mean(acc * acc, axis=-1, keepdims=True) + EPS)
            g = g_ref[rows, :]
            mix_ref[rows, :] = (g * _sigmoid(g) * (yn * gain_ref[...])).astype(BF16)

        def whole(first_step):
            heads = []
            for u in range(SB_PER_STEP):
                qi = si * SB_PER_STEP + u
                qb = q_ref[u * B:(u + 1) * B, :].astype(BF16)
                state = block(qb, qi, *init, True)
                if not (first_step and u == 0):
                    state = block(qb, qi - 1, *state, False)
                heads.append((qi, qb, state))
            for u, (qi, qb, state) in enumerate(heads):
                if not (first_step and u == 0):
                    state = lax.while_loop(
                        live, lambda st, qb=qb: (st[0] - 1,) + block(qb, st[0], st[1], st[2], st[3], False),
                        (qi - 2,) + state)[1:]
                finish(u, state[1], state[2])

        pl.when(si == 0)(lambda: whole(True))
        pl.when(si > 0)(lambda: whole(False))
        pl.when((hd == H - 1) & (si == ns - 1))(wait_gather)

    tq = SB_PER_STEP * B
    tile = lambda j: pl.BlockSpec((None, tq, HEAD_DIM), lambda h, i: (j, i, h))
    out_tile = pl.BlockSpec((tq, HEAD_DIM), lambda h, i: (i, h))
    hbm = pl.BlockSpec(memory_space=pltpu.HBM)
    return pl.pallas_call(
        body, name="sb_fwd",
        out_shape=(jax.ShapeDtypeStruct((S, W), BF16), jax.ShapeDtypeStruct((S, W), F32),
                   jax.ShapeDtypeStruct((S, W), F32), jax.ShapeDtypeStruct((N_DEV,) + wo_shard.shape, BF16)),
        grid=(H, ns),
        in_specs=[tile(4), _head_spec(S, 5), _head_spec(S, 6), tile(7),
                  pl.BlockSpec((1, HEAD_DIM), lambda h, i: (0, h)), hbm],
        out_specs=(out_tile, out_tile, out_tile, hbm),
        scratch_shapes=[pltpu.VMEM((S, HEAD_DIM), BF16), pltpu.VMEM((S, HEAD_DIM), BF16),
                        pltpu.SemaphoreType.DMA((N_DEV - 1,)), pltpu.SemaphoreType.DMA((N_DEV - 1,)),
                        pltpu.SemaphoreType.DMA],
        compiler_params=_params("arbitrary", "arbitrary"),
    )(proj, proj, proj, proj, gain, wo_shard)


def _out_proj_loss(mix_r, mix_s, w_out, x, tgt, gf):
    S, W = mix_r.shape
    D = x.shape[1]
    tm = min(256, S)

    def body(mr_ref, ms_ref, wo_ref, x_ref, t_ref, gf_ref, dx2_ref, dx2b_ref, dmix_ref, loss_ref, gfn_ref):
        @pl.when(pl.program_id(0) == 0)
        def _():
            loss_ref[...] = jnp.zeros_like(loss_ref)
            gfn_ref[...] = jnp.zeros_like(gfn_ref)

        gfv = gf_ref[...]
        x2 = x_ref[...] + (_dot(mr_ref[...], wo_ref[:W, :]) + _dot(ms_ref[...], wo_ref[W:, :]))
        r2 = lax.rsqrt(jnp.mean(x2 * x2, axis=-1, keepdims=True) + EPS)
        n = x2 * r2
        err = n * gfv - t_ref[...]
        loss_ref[...] += 0.5 * jnp.sum(jnp.mean(err * err, axis=-1, keepdims=True))
        dy = err * (1.0 / D)
        gfn_ref[...] += jnp.sum(dy * n, axis=0, keepdims=True)
        dn = dy * gfv
        dx2 = r2 * (dn - n * jnp.mean(dn * n, axis=-1, keepdims=True))
        dx2_ref[...] = dx2
        b = dx2.astype(BF16)
        dx2b_ref[...] = b
        dmix_ref[:, :W] = _dot_nt(b, wo_ref[:W, :])
        dmix_ref[:, W:] = _dot_nt(b, wo_ref[W:, :])

    row = lambda width: pl.BlockSpec((tm, width), lambda i: (i, 0))
    return pl.pallas_call(
        body, name="out_proj_loss",
        out_shape=(jax.ShapeDtypeStruct((S, D), F32), jax.ShapeDtypeStruct((S, D), BF16),
                   jax.ShapeDtypeStruct((S, 2 * W), F32), jax.ShapeDtypeStruct((SUBLANES, LANES), F32),
                   jax.ShapeDtypeStruct((1, D), F32)),
        grid=(S // tm,),
        in_specs=[row(W), row(W), pl.BlockSpec((2 * W, D), lambda i: (0, 0)), row(D), row(D),
                  pl.BlockSpec((1, D), lambda i: (0, 0))],
        out_specs=(row(D), row(D), row(2 * W), pl.BlockSpec((SUBLANES, LANES), lambda i: (0, 0)),
                   pl.BlockSpec((1, D), lambda i: (0, 0))),
        compiler_params=_params("arbitrary"),
    )(mix_r, mix_s, w_out, x, tgt, gf)


def _silu_bwd(g, dm, normed):
    sig = _sigmoid(g)
    return dm * (g * sig), dm * normed * (sig * (1.0 + g * (1.0 - sig)))


def _ret_bwd(proj, dmix, tabs, gn_gain, gn_bias, cos, sin, gwo):
    _, S, W = proj.shape
    H, nc = W // HEAD_DIM, S // CHUNK

    def body(q_ref, k_ref, v_ref, g_ref, dm_ref, dec_ref, xi_ref, ze_ref, gam_ref, gain_ref, bias_ref, cos_ref,
             sin_ref, gwo_ref, dp_ref, dgain_ref, dbias_ref, rino_ref, rs_ref, send_sems, recv_sems):
        dec, xi, ze, gam = dec_ref[...], xi_ref[...], ze_ref[...], gam_ref[...]
        gain, bias = gain_ref[...], bias_ref[...]
        hd = pl.program_id(0)
        other_core = 1 - lax.axis_index("c")
        copies = _to_sibling_copies([gwo_ref.at[2 * k + other_core] for k in range(4)], rino_ref, send_sems, recv_sems)

        @pl.when(hd == 0)
        def _():
            for cp in copies:
                cp.start()

        def fwd_step(c, state):
            rows = pl.ds(pl.multiple_of(c * CHUNK, CHUNK), CHUNK)
            rs_ref[c] = state
            kz = (k_ref[rows, :] * ze).astype(BF16)
            return gam * state + _dot_tn(kz, v_ref[rows, :].astype(BF16))

        _chunk_loop(nc, fwd_step, jnp.zeros((HEAD_DIM, HEAD_DIM), F32))

        def bwd_step(i, carry):
            dgain, dbias, dstate = carry
            c = nc - 1 - i
            rows = pl.ds(pl.multiple_of(c * CHUNK, CHUNK), CHUNK)
            q, k, g = q_ref[rows, :], k_ref[rows, :], g_ref[rows, :]
            vb = v_ref[rows, :].astype(BF16)
            rb = rs_ref[c].astype(BF16)
            out, _, (qb, kb, sb, qx, kz) = _ret_chunk(q, k, vb, rs_ref[c], dec, xi, ze)
            mu = jnp.mean(out, axis=-1, keepdims=True)
            d = out - mu
            rstd = lax.rsqrt(jnp.mean(d * d, axis=-1, keepdims=True) + EPS)
            yn = d * rstd
            dgn, dg = _silu_bwd(g, dm_ref[rows, :], yn * gain + bias)
            dgain = dgain + jnp.sum(dgn * yn, axis=0, keepdims=True)
            dbias = dbias + jnp.sum(dgn, axis=0, keepdims=True)
            dyn = dgn * gain
            do = rstd * (dyn - jnp.mean(dyn, axis=-1, keepdims=True)
                         - yn * jnp.mean(dyn * yn, axis=-1, keepdims=True))
            dob = do.astype(BF16)
            drb = dstate.astype(BF16)
            dv = _dot_tn(sb, dob) + _dot(kz, drb)
            dsb = (_dot_nt(dob, vb) * dec).astype(BF16)
            dq = _dot(dsb, kb) + _dot_nt(dob, rb) * xi
            dk = _dot_tn(dsb, qb) + _dot_nt(vb, drb) * ze
            cs, sn = cos_ref[rows, :], -sin_ref[rows, :]
            dp_ref[0, rows, :] = _rot(dq, cs, sn).astype(BF16)
            dp_ref[1, rows, :] = (_rot(dk, cs, sn) * (HEAD_DIM ** -0.5)).astype(BF16)
            dp_ref[2, rows, :] = dv.astype(BF16)
            dp_ref[3, rows, :] = dg.astype(BF16)
            return dgain, dbias, gam * dstate + _dot_tn(qx, dob)

        zero = jnp.zeros((1, HEAD_DIM), F32)
        dgain, dbias, _ = _chunk_loop(nc, bwd_step, (zero, zero, jnp.zeros((HEAD_DIM, HEAD_DIM), F32)))
        dgain_ref[...] = dgain
        dbias_ref[...] = dbias

        @pl.when(hd == H - 1)
        def _():
            for cp in copies:
                cp.wait_recv()
            for cp in copies:
                cp.wait_send()

    vec = pl.BlockSpec((1, HEAD_DIM), lambda h: (0, h))
    full = pl.BlockSpec((S, HEAD_DIM), lambda h: (0, 0))
    hbm = pl.BlockSpec(memory_space=pltpu.HBM)
    return pl.pallas_call(
        body, name="ret_bwd",
        out_shape=(jax.ShapeDtypeStruct((4, S, W), BF16), jax.ShapeDtypeStruct((1, W), F32),
                   jax.ShapeDtypeStruct((1, W), F32), jax.ShapeDtypeStruct((4,) + gwo.shape[1:], gwo.dtype)),
        grid=(H,),
        in_specs=[_head_spec(S, 0), _head_spec(S, 1), _head_spec(S, 2), _head_spec(S, 3),
                  pl.BlockSpec((S, HEAD_DIM), lambda h: (0, h))] + _table_specs() + [vec, vec, full, full, hbm],
        out_specs=(pl.BlockSpec((4, S, HEAD_DIM), lambda h: (0, 0, h)), vec, vec, hbm),
        scratch_shapes=[pltpu.VMEM((nc, HEAD_DIM, HEAD_DIM), F32), pltpu.SemaphoreType.DMA((4,)),
                        pltpu.SemaphoreType.DMA((4,))],
        compiler_params=_params("arbitrary"),
    )(proj, proj, proj, proj, dmix, *tabs, gn_gain, gn_bias, cos, sin, gwo)


def _sb_bwd(proj, raw, carries, dmix, gain, chip_sums_o):
    _, S, W = proj.shape
    H = W // HEAD_DIM
    B = min(SB_BLOCK, S)
    nq = S // B
    ns = nq // SB_PER_STEP

    def body(q_ref, k_ref, v_ref, g_ref, raw_ref, car_ref, dm_ref, gain_ref, so_ref, dp_ref, dgain_ref, ro_ref,
             kb_ref, vb_ref, dk_ref, dv_ref, send_sems, recv_sems):
        hd, si = pl.program_id(0), pl.program_id(1)
        start_exchange, wait_exchange = _exchange_chip_sums((so_ref,), (ro_ref,), send_sems, recv_sems)
        pl.when((hd == 0) & (si == 0))(start_exchange)

        @pl.when(si == 0)
        def _():
            kb_ref[...] = k_ref[...].astype(BF16)
            vb_ref[...] = v_ref[...].astype(BF16)
            dk_ref[...] = jnp.zeros_like(dk_ref)
            dv_ref[...] = jnp.zeros_like(dv_ref)
            dgain_ref[...] = jnp.zeros_like(dgain_ref)

        causal = _tri(B, "gt")
        upper = _ones_where(causal)
        before = _ones_where(_tri(B, "lt"))
        lane = lax.broadcasted_iota(jnp.int32, (B, HEAD_DIM), 1)
        gain_v = gain_ref[...]

        def prologue(u):
            qi = si * SB_PER_STEP + u
            rows = slice(u * B, (u + 1) * B)
            o = raw_ref[rows, :]
            rstd = lax.rsqrt(jnp.mean(o * o, axis=-1, keepdims=True) + EPS)
            yn = o * rstd
            dnrm, dg = _silu_bwd(g_ref[rows, :], dm_ref[rows, :], yn * gain_v)
            dp_ref[3, pl.ds(pl.multiple_of(qi * B, B), B), :] = dg.astype(BF16)
            dgain_ref[...] += jnp.sum(dnrm * yn, axis=0, keepdims=True)
            dyn = dnrm * gain_v
            do = rstd * (dyn - yn * jnp.mean(dyn * yn, axis=-1, keepdims=True))
            return qi, q_ref[rows, :].astype(BF16), do.astype(BF16), car_ref[rows, :]

        def block(ctx, kb, carry_g, dq, masked):
            _, qb, dob, saved = ctx
            rows = pl.ds(pl.multiple_of(kb * B, B), B)
            kk, vv = kb_ref[rows, :], vb_ref[rows, :]
            log_beta, _, cs = _sb_scores(qb, kk, masked, causal, upper)
            carry_lk = jnp.sum(jnp.where(lane == kb, saved, 0.0), axis=1, keepdims=True)
            a = jnp.exp(log_beta + cs + carry_lk)
            if masked:
                a = jnp.where(causal, a, 0.0)
            gmat = _dot_nt(dob, vv) * a
            dv_ref[rows, :] += _dot_tn(a.astype(BF16), dob)
            hi = gmat.astype(BF16)
            lo = (gmat - hi.astype(F32)).astype(BF16)
            dlk = carry_g + (_dot(hi, before) + _dot(lo, before))
            beta = jnp.exp(log_beta)
            dz = (gmat * (1.0 - beta) - dlk * beta) * (HEAD_DIM ** -0.5)
            if masked:
                dz = jnp.where(causal, dz, 0.0)
            dzb = dz.astype(BF16)
            dk_ref[rows, :] += _dot_tn(dzb, qb)
            return carry_g + jnp.sum(gmat, axis=1, keepdims=True), dq + _dot(dzb, kk)

        init = (jnp.zeros((B, 1), F32), jnp.zeros((B, HEAD_DIM), F32))

        def whole(first_step):
            ctxs = [prologue(u) for u in range(SB_PER_STEP)]
            states = []
            for u, ctx in enumerate(ctxs):
                state = init
                if not (first_step and u == 0):
                    visited = jnp.max(ctx[3], axis=0, keepdims=True) >= EXP_IS_ZERO_BELOW
                    first = jnp.min(jnp.where(visited, lane[:1, :], ctx[0]))
                    state = lax.fori_loop(first, ctx[0] - 1,
                                          lambda i, st, ctx=ctx: block(ctx, i, st[0], st[1], False), state)
                states.append(state)
            for u, (ctx, state) in enumerate(zip(ctxs, states)):
                if not (first_step and u == 0):
                    state = block(ctx, ctx[0] - 1, *state, False)
                state = block(ctx, ctx[0], *state, True)
                dp_ref[0, pl.ds(pl.multiple_of(ctx[0] * B, B), B), :] = state[1].astype(BF16)

        pl.when(si == 0)(lambda: whole(True))
        pl.when(si > 0)(lambda: whole(False))

        @pl.when(si == ns - 1)
        def _():
            dp_ref[1] = dk_ref[...].astype(BF16)
            dp_ref[2] = dv_ref[...].astype(BF16)

        pl.when((hd == H - 1) & (si == ns - 1))(wait_exchange)

    tq = SB_PER_STEP * B
    tile = lambda j: pl.BlockSpec((None, tq, HEAD_DIM), lambda h, i: (j, i, h))
    vec = pl.BlockSpec((1, HEAD_DIM), lambda h, i: (0, h))
    hbm = pl.BlockSpec(memory_space=pltpu.HBM)
    return pl.pallas_call(
        body, name="sb_bwd",
        out_shape=(jax.ShapeDtypeStruct((4, S, W), BF16), jax.ShapeDtypeStruct((1, W), F32),
                   jax.ShapeDtypeStruct((3,) + chip_sums_o.shape[1:], chip_sums_o.dtype)),
        grid=(H, ns),
        in_specs=[tile(4), _head_spec(S, 5), _head_spec(S, 6), tile(7),
                  pl.BlockSpec((tq, HEAD_DIM), lambda h, i: (i, h)),
                  pl.BlockSpec((tq, HEAD_DIM), lambda h, i: (i, h)),
                  pl.BlockSpec((tq, HEAD_DIM), lambda h, i: (i, H + h)), vec, hbm],
        out_specs=(pl.BlockSpec((4, S, HEAD_DIM), lambda h, i: (0, 0, h)), vec, hbm),
        scratch_shapes=[pltpu.VMEM((S, HEAD_DIM), BF16), pltpu.VMEM((S, HEAD_DIM), BF16),
                        pltpu.VMEM((S, HEAD_DIM), F32), pltpu.VMEM((S, HEAD_DIM), F32),
                        pltpu.SemaphoreType.DMA((1, 3)), pltpu.SemaphoreType.DMA((1, 3))],
        compiler_params=_params("arbitrary", "arbitrary"),
    )(proj, proj, proj, proj, raw, carries, dmix, gain, chip_sums_o)


def _grad_w_in_half(ht, dpr, dps, core, name, to_sibling=None):
    D, S = ht.shape
    _, _, W = dpr.shape
    tmm = min(512, D)
    nm = D // tmm

    def body(core_ref, ht_ref, r_ref, s_ref, *rest):
        o_ref = rest[1] if to_sibling is not None else rest[0]
        q, m = pl.program_id(0), pl.program_id(1)
        if to_sibling is not None:
            ga_ref, _, rin_ref, send_sems, recv_sems = rest
            copies = _to_sibling_copies([ga_ref.at[k] for k in range(4)], rin_ref, send_sems, recv_sems)

            @pl.when((q == 0) & (m == 0))
            def _():
                for cp in copies:
                    cp.start()

        @pl.when(q < 2)
        def _():
            o_ref[...] = _dot(ht_ref[...], r_ref[...])

        @pl.when(q >= 2)
        def _():
            o_ref[...] = _dot(ht_ref[...], s_ref[...])

        if to_sibling is not None:
            @pl.when((q == 3) & (m == nm - 1))
            def _():
                for cp in copies:
                    cp.wait_recv()
                for cp in copies:
                    cp.wait_send()

    hbm = pl.BlockSpec(memory_space=pltpu.HBM)
    gw_shape = jax.ShapeDtypeStruct((4, D, W), F32)
    out_shape, out_specs, extra_in, scratch = (gw_shape,), (pl.BlockSpec((None, tmm, W), lambda q, m, core: (q, m, 0)),), [], []
    if to_sibling is not None:
        out_shape += (gw_shape,)
        out_specs += (hbm,)
        extra_in = [hbm]
        scratch = [pltpu.SemaphoreType.DMA((4,)), pltpu.SemaphoreType.DMA((4,))]
    return pl.pallas_call(
        body, name=name, out_shape=out_shape,
        grid_spec=pltpu.PrefetchScalarGridSpec(
            num_scalar_prefetch=1, grid=(4, nm),
            in_specs=[pl.BlockSpec((tmm, S), lambda q, m, core: (m, 0)),
                      pl.BlockSpec((None, S, W), lambda q, m, core: (jnp.minimum(2 * q + core[0], 3), 0, 0)),
                      pl.BlockSpec((None, S, W), lambda q, m, core: (jnp.maximum(2 * q + core[0] - 4, 0), 0, 0))]
            + extra_in,
            out_specs=out_specs, scratch_shapes=scratch),
        compiler_params=_params("arbitrary", "arbitrary"),
    )(core, ht, dpr, dps, *(() if to_sibling is None else (to_sibling,)))


def _grad_w_out(mix_r, mix_s, dx2b):
    S, W = mix_r.shape
    D = dx2b.shape[1]
    tmm = min(512, W)
    tk = min(1024, S)

    def body(r_ref, s_ref, b_ref, o_ref):
        j, kk = pl.program_id(0), pl.program_id(2)

        def acc(a_ref):
            part = _dot_tn(a_ref[...], b_ref[...])

            @pl.when(kk == 0)
            def _():
                o_ref[...] = part

            @pl.when(kk > 0)
            def _():
                o_ref[...] += part

        pl.when(j == 0)(lambda: acc(r_ref))
        pl.when(j == 1)(lambda: acc(s_ref))

    return pl.pallas_call(
        body, name="grad_w_out", out_shape=jax.ShapeDtypeStruct((2, W, D), F32), grid=(2, W // tmm, S // tk),
        in_specs=[pl.BlockSpec((tk, tmm), lambda j, m, k: (k, m)),
                  pl.BlockSpec((tk, tmm), lambda j, m, k: (k, m)),
                  pl.BlockSpec((tk, D), lambda j, m, k: (k, 0))],
        out_specs=pl.BlockSpec((None, tmm, D), lambda j, m, k: (j, m, 0)),
        compiler_params=_params("parallel", "parallel", "arbitrary"),
    )(mix_r, mix_s, dx2b)


def _dh_matmul(dpr, dps, w_all, chip_sums):
    _, S, W = dpr.shape
    D = w_all.shape[1]
    tm = min(1024, S)
    ni = S // tm

    def body(r_ref, s_ref, w_ref, sa_ref, dh_ref, ra_ref, send_sems, recv_sems):
        i, j = pl.program_id(0), pl.program_id(1)
        start_exchange, wait_exchange = _exchange_chip_sums((sa_ref,), (ra_ref,), send_sems, recv_sems)
        pl.when((i == 0) & (j == 0))(start_exchange)

        def acc(b_ref):
            part = _dot_nt(b_ref[...], w_ref[...])

            @pl.when(j == 0)
            def _():
                dh_ref[...] = part

            @pl.when(j > 0)
            def _():
                dh_ref[...] += part

        pl.when(j < 4)(lambda: acc(r_ref))
        pl.when(j >= 4)(lambda: acc(s_ref))
        pl.when((i == ni - 1) & (j == 7))(wait_exchange)

    hbm = pl.BlockSpec(memory_space=pltpu.HBM)
    return pl.pallas_call(
        body, name="dh_matmul",
        out_shape=(jax.ShapeDtypeStruct((S, D), F32), jax.ShapeDtypeStruct((3,) + chip_sums.shape[1:], chip_sums.dtype)),
        grid=(ni, 8),
        in_specs=[pl.BlockSpec((None, tm, W), lambda i, j: (jnp.minimum(j, 3), i, 0)),
                  pl.BlockSpec((None, tm, W), lambda i, j: (jnp.maximum(j - 4, 0), i, 0)),
                  pl.BlockSpec((None, D, W), lambda i, j: (j, 0, 0)), hbm],
        out_specs=(pl.BlockSpec((tm, D), lambda i, j: (i, 0)), hbm),
        scratch_shapes=[pltpu.SemaphoreType.DMA((1, 3)), pltpu.SemaphoreType.DMA((1, 3))],
        compiler_params=_params("arbitrary", "arbitrary"),
    )(dpr, dps, w_all, chip_sums)


def _norm_bwd(x, dx2, dh, gain):
    S, D = x.shape
    tm = min(256, S)

    def body(x_ref, dx2_ref, dh_ref, g_ref, gx_ref, dgain_ref):
        @pl.when(pl.program_id(0) == 0)
        def _():
            dgain_ref[...] = jnp.zeros_like(dgain_ref)

        xv, dh_v = x_ref[...], dh_ref[...]
        r1 = lax.rsqrt(jnp.mean(xv * xv, axis=-1, keepdims=True) + EPS)
        n = xv * r1
        dgain_ref[...] += jnp.sum(dh_v * n, axis=0, keepdims=True)
        dn = dh_v * g_ref[...]
        gx_ref[...] = dx2_ref[...] + r1 * (dn - n * jnp.mean(dn * n, axis=-1, keepdims=True))

    row = pl.BlockSpec((tm, D), lambda i: (i, 0))
    one = pl.BlockSpec((1, D), lambda i: (0, 0))
    return pl.pallas_call(
        body, name="norm_bwd", out_shape=(jax.ShapeDtypeStruct((S, D), F32), jax.ShapeDtypeStruct((1, D), F32)),
        grid=(S // tm,), in_specs=[row, row, row, one], out_specs=(row, one),
        compiler_params=_params("arbitrary"),
    )(x, dx2, dh, gain)


def _own_block(gw, pos, q):
    return q if gw.shape[0] == 4 else 2 * q + pos[0]


def _rs_local_sum(gw, rin, pos):
    _, R, C = gw.shape
    tr = min(256, R)
    other = lambda k, pos: (pos[1] + 1 + k) % 4

    def body(pos_ref, a_ref, b_ref, o_ref):
        o_ref[...] = (a_ref[...] + b_ref[...]).astype(BF16)

    return pl.pallas_call(
        body, name="rs_local_sum", out_shape=jax.ShapeDtypeStruct((4, R, C), BF16),
        grid_spec=pltpu.PrefetchScalarGridSpec(
            num_scalar_prefetch=1, grid=(3, R // tr),
            in_specs=[pl.BlockSpec((None, tr, C), lambda k, i, pos: (_own_block(gw, pos, other(k, pos)), i, 0)),
                      pl.BlockSpec((None, tr, C), lambda k, i, pos: (other(k, pos), i, 0))],
            out_specs=pl.BlockSpec((None, tr, C), lambda k, i, pos: (other(k, pos), i, 0))),
        compiler_params=_params("parallel", "parallel"),
    )(pos, gw, rin)


def _adamw(w, g, m, v):
    m2 = ADAM_B1 * m + (1.0 - ADAM_B1) * g
    v2 = ADAM_B2 * v + (1.0 - ADAM_B2) * (g * g)
    m_hat = m2 / (1.0 - ADAM_B1 ** ADAM_STEP)
    v_hat = v2 / (1.0 - ADAM_B2 ** ADAM_STEP)
    delta = -ADAM_LR * (m_hat / (jnp.sqrt(v_hat) + ADAM_EPS) + ADAM_WD * w)
    return delta, m2, v2


def _adamw_shard(gw, rin, rb, w, m, v, pos):
    _, R, C = gw.shape
    tr = min(256, R)

    def body(pos_ref, a_ref, b_ref, rb_ref, w_ref, m_ref, v_ref, g_ref, d_ref, m2_ref, v2_ref):
        g = a_ref[...] + b_ref[...]
        for k in range(3):
            g = g + rb_ref[k].astype(F32)
        g_ref[...] = g
        d_ref[...], m2_ref[...], v2_ref[...] = _adamw(w_ref[...], g, m_ref[...], v_ref[...])

    plain = pl.BlockSpec((tr, C), lambda i, pos: (i, 0))
    shape = jax.ShapeDtypeStruct((R, C), F32)
    return pl.pallas_call(
        body, name="adamw_shard", out_shape=(shape,) * 4,
        grid_spec=pltpu.PrefetchScalarGridSpec(
            num_scalar_prefetch=1, grid=(R // tr,),
            in_specs=[pl.BlockSpec((None, tr, C), lambda i, pos: (_own_block(gw, pos, pos[1]), i, 0)),
                      pl.BlockSpec((None, tr, C), lambda i, pos: (pos[1], i, 0)),
                      pl.BlockSpec((3, tr, C), lambda i, pos: (0, i, 0)), plain, plain, plain],
            out_specs=(plain,) * 4),
        compiler_params=_params("parallel"),
    )(pos, gw, rin, rb, w, m, v)


def _adamw_small(parts, w, m, v):
    _, rows, n = parts.shape

    def body(p_ref, w_ref, m_ref, v_ref, g_ref, d_ref, m2_ref, v2_ref):
        g = p_ref[0]
        for d in range(1, N_DEV):
            g = g + p_ref[d]
        g_ref[...] = g
        d_ref[...], m2_ref[...], v2_ref[...] = _adamw(w_ref[...], g, m_ref[...], v_ref[...])

    shape = jax.ShapeDtypeStruct((rows, n), F32)
    return pl.pallas_call(body, name="adamw_small", out_shape=(shape,) * 4)(parts, w, m, v)


def _rope_tables(S):
    half = HEAD_DIM // 2
    inv = ROPE_THETA ** (-jnp.arange(half, dtype=F32) / half)
    ang = jnp.arange(S, dtype=F32)[:, None] * inv[None, :]
    cos, sin = jnp.cos(ang), jnp.sin(ang)
    return jnp.concatenate([cos, cos], axis=1), jnp.concatenate([-sin, sin], axis=1)


def _retention_tables(H):
    lg = jnp.log1p(-jnp.exp2(-5.0 - jnp.arange(H, dtype=F32)))
    n = jnp.arange(CHUNK, dtype=F32)
    rel = n[:, None] - n[None, :]
    decay = jnp.where(rel >= 0, jnp.exp(lg[:, None, None] * jnp.maximum(rel, 0.0)), 0.0)
    shape = (H, CHUNK, HEAD_DIM)
    xi = jnp.broadcast_to(jnp.exp(lg[:, None] * (n + 1.0))[:, :, None], shape)
    zeta = jnp.broadcast_to(jnp.exp(lg[:, None] * (CHUNK - 1.0 - n))[:, :, None], shape)
    gamma_c = jnp.broadcast_to(jnp.exp(lg * CHUNK)[:, None, None], shape)
    return decay, xi, zeta, gamma_c


def _pack_small(parts):
    flat = []
    for p in parts:
        p = p.reshape(-1)
        flat.append(jnp.pad(p, (0, -p.shape[0] % LANES)))
    flat = jnp.concatenate(flat)
    return jnp.pad(flat, (0, SMALL_N - flat.shape[0])).reshape(SUBLANES, SMALL_N // SUBLANES)


def _unpack_small(packed, shapes):
    flat = packed.reshape(-1)
    out, at = [], 0
    for shp in shapes:
        size = 1
        for s in shp:
            size *= s
        out.append(flat[at:at + size].reshape(shp))
        at += size + (-size % LANES)
    return out


def kernel(x, norm_gain, w_in, ret_gn_gain, ret_gn_bias, sb_norm_gain, w_out, final_norm_gain, loss_target, m_norm_gain, m_w_in, m_ret_gn_gain, m_ret_gn_bias, m_sb_norm_gain, m_w_out, m_final_norm_gain, v_norm_gain, v_w_in, v_ret_gn_gain, v_ret_gn_bias, v_sb_norm_gain, v_w_out, v_final_norm_gain):
    S, D = x.shape[1], x.shape[2]
    W = w_in.shape[2]
    wo_rows = w_out.shape[1]
    H = W // HEAD_DIM
    xs, tgt = x[0], loss_target[0]
    mx, my, mc = _mesh_pos()
    pos = jnp.stack([mc, 2 * mx + my]).astype(jnp.int32)

    cos, sin = _rope_tables(S)
    tabs = _retention_tables(H)

    proj, w_all, ht = _in_proj_gather(xs, norm_gain, w_in[0].astype(BF16), cos, sin, _gather_order())
    mix_r = _ret_fwd(proj, tabs, ret_gn_gain, ret_gn_bias)
    mix_s, raw_s, carries, wo_all = _sb_fwd(proj, sb_norm_gain, w_out[0].astype(BF16))
    wo_full = wo_all.reshape(N_DEV * wo_rows, D)
    dx2, dx2b, dmix, loss_p, d_gf = _out_proj_loss(mix_r, mix_s, wo_full, xs, tgt, final_norm_gain[None])

    gwo = _grad_w_out(mix_r, mix_s, dx2b).reshape(N_DEV, wo_rows, D)
    dpr, d_rgain, d_rbias, rino = _ret_bwd(proj, dmix, tabs, ret_gn_gain, ret_gn_bias, cos, sin, gwo)
    dps, d_sgain, rbo = _sb_bwd(proj, raw_s, carries, dmix, sb_norm_gain, _rs_local_sum(gwo, rino, pos))
    gw_sibling, = _grad_w_in_half(ht, dpr, dps, (1 - mc).reshape(1).astype(jnp.int32), "grad_w_in_sibling")
    gw, rin = _grad_w_in_half(ht, dpr, dps, mc.reshape(1).astype(jnp.int32), "grad_w_in_own", to_sibling=gw_sibling)
    dh, rb = _dh_matmul(dpr, dps, w_all, _rs_local_sum(gw, rin, pos))
    grad_x, d_gain = _norm_bwd(xs, dx2, dh, norm_gain)
    g_in, d_in, m_in, v_in = _adamw_shard(gw, rin, rb, w_in[0], m_w_in[0], v_w_in[0], pos)
    g_out, d_out, m_out, v_out = _adamw_shard(gwo, rino, rbo, w_out[0], m_w_out[0], v_w_out[0], pos)

    small_w = [norm_gain, ret_gn_gain, ret_gn_bias, sb_norm_gain, final_norm_gain]
    small_m = [m_norm_gain, m_ret_gn_gain, m_ret_gn_bias, m_sb_norm_gain, m_final_norm_gain]
    small_v = [v_norm_gain, v_ret_gn_gain, v_ret_gn_bias, v_sb_norm_gain, v_final_norm_gain]
    shapes = [()] + [w.shape for w in small_w]
    zero = jnp.zeros((), F32)
    parts = _small_all_gather(_pack_small([loss_p[0, 0], d_gain, d_rgain, d_rbias, d_sgain, d_gf]))
    packed = _adamw_small(parts, _pack_small([zero] + small_w), _pack_small([zero] + small_m),
                          _pack_small([zero] + small_v))
    g_s, d_s, m_s, v_s = (_unpack_small(p, shapes) for p in packed)

    grads = [g_s[1], g_in[None], g_s[2], g_s[3], g_s[4], g_out[None], g_s[5]]
    deltas = [d_s[1], d_in[None], d_s[2], d_s[3], d_s[4], d_out[None], d_s[5]]
    new_m = [m_s[1], m_in[None], m_s[2], m_s[3], m_s[4], m_out[None], m_s[5]]
    new_v = [v_s[1], v_in[None], v_s[2], v_s[3], v_s[4], v_out[None], v_s[5]]
    return (g_s[0], grad_x[None], *grads, *deltas, *new_m, *new_v)
```

```python
import functools

import jax
import jax.numpy as jnp
from jax import lax
from jax.experimental import pallas as pl
from jax.experimental.pallas import tpu as pltpu

F32 = jnp.float32
BF16 = jnp.bfloat16

HEAD_DIM = 128
CHUNK = 128
RET_GROUP = 4
ROPE_THETA = 10000.0
EPS = 1e-6
ADAM_LR = 0.001
ADAM_B1 = 0.9
ADAM_B2 = 0.999
ADAM_EPS = 1e-08
ADAM_WD = 0.01
ADAM_STEP = 10

N_DEV = 8
LANES = 128
SUBLANES = 8
VMEM_LIMIT = 56 * 1024 * 1024
SB_BLOCK = 256
SB_PER_STEP = 2
SMALL_N = 8192
EXP_IS_ZERO_BELOW = -104.0
NOT_VISITED = -1e30
MESH = pl.DeviceIdType.MESH

NT = (((1,), (1,)), ((), ()))
TN = (((0,), (0,)), ((), ()))


def _params(*sem):
    return pltpu.CompilerParams(dimension_semantics=sem if sem else None, vmem_limit_bytes=VMEM_LIMIT)


def _dot(a, b):
    return jnp.dot(a, b, preferred_element_type=F32)


def _dot_nt(a, b):
    return lax.dot_general(a, b, NT, preferred_element_type=F32)


def _dot_tn(a, b):
    return lax.dot_general(a, b, TN, preferred_element_type=F32)


def _sigmoid(g):
    return 1.0 / (1.0 + jnp.exp(-g))


def _rot(a, cos, sin_signed):
    return a * cos + pltpu.roll(a, HEAD_DIM // 2, 1) * sin_signed


def _mesh_pos():
    return lax.axis_index("x"), lax.axis_index("y"), lax.axis_index("c")


def _to_sibling_copies(blocks, out_ref, send_sems, recv_sems):
    x, y, c = _mesh_pos()
    return [pltpu.make_async_remote_copy(
        src_ref=block, dst_ref=out_ref.at[k], send_sem=send_sems.at[k], recv_sem=recv_sems.at[k],
        device_id=(x, y, 1 - c), device_id_type=MESH) for k, block in enumerate(blocks)]


def _exchange_chip_sums(srcs, outs, send_sems, recv_sems):
    x, y, c = _mesh_pos()
    copies = []
    for arr, (src, out) in enumerate(zip(srcs, outs)):
        for k in range(1, 4):
            px = 1 - x if k & 2 else x
            py = 1 - y if k & 1 else y
            copies.append(pltpu.make_async_remote_copy(
                src_ref=src.at[2 * px + py], dst_ref=out.at[k - 1],
                send_sem=send_sems.at[arr, k - 1], recv_sem=recv_sems.at[arr, k - 1],
                device_id=(px, py, c), device_id_type=MESH))

    def start():
        for cp in copies:
            cp.start()

    def wait():
        for cp in copies:
            cp.wait_recv()
        for cp in copies:
            cp.wait_send()

    return start, wait


def _small_all_gather(small):
    rows, n = small.shape

    def body(s_ref, o_ref, send_sems, recv_sems, local_sem):
        start, wait = _exchange_with_all(s_ref, o_ref, send_sems, recv_sems, local_sem)
        start()
        wait()

    vmem = pl.BlockSpec(memory_space=pltpu.VMEM)
    return pl.pallas_call(
        body, name="small_all_gather",
        out_shape=jax.ShapeDtypeStruct((N_DEV, rows, n), small.dtype),
        in_specs=[vmem], out_specs=vmem,
        scratch_shapes=[pltpu.SemaphoreType.DMA((N_DEV - 1,)), pltpu.SemaphoreType.DMA((N_DEV - 1,)),
                        pltpu.SemaphoreType.DMA],
    )(small)


GATHER_SPLIT = 2
GATHER_STEPS = ([("own", 0, p) for p in range(GATHER_SPLIT)] + [("sibling", 0, p) for p in range(GATHER_SPLIT)]
                + [(kind, j, p) for p in range(GATHER_SPLIT) for kind in ("ici", "passed") for j in range(3)])


def _gather_order():
    x, y, c = _mesh_pos()
    chips = [(1 - x, y), (x, 1 - y), (1 - x, 1 - y)]
    owner = {"own": lambda j: (x, y, c), "sibling": lambda j: (x, y, 1 - c),
             "ici": lambda j: (*chips[j], c), "passed": lambda j: (*chips[j], 1 - c)}
    blocks = [4 * px + 2 * py + pc for px, py, pc in (owner[kind](j) for kind, j, _ in GATHER_STEPS)]
    return (jnp.stack(blocks).astype(jnp.int32), jnp.array([p for _, _, p in GATHER_STEPS], jnp.int32))


def _in_proj_gather(x, gain, w_shard, cos, sin, order):
    S, D = x.shape
    W = w_shard.shape[1]
    wp = W // GATHER_SPLIT
    tm = min(1024, S)
    ni = S // tm
    n_steps = len(GATHER_STEPS)

    def body(blk_ref, piece_ref, x_ref, g_ref, w_ref, cos_ref, sin_ref, o_ref, wall_ref, ht_ref, h_scr, wbuf,
             send_sems, recv_sems, local_sem, load_sem):
        step, i = pl.program_id(0), pl.program_id(1)
        mx, my, c = _mesh_pos()
        me, sibling = (mx, my, c), (mx, my, 1 - c)
        chips = [(1 - mx, my), (mx, 1 - my), (1 - mx, 1 - my)]

        def piece_of(dev, p):
            px, py, pc = dev
            return wall_ref.at[4 * px + 2 * py + pc, :, pl.ds(p * wp, wp)]

        def copy(k, p, block, to, own=False):
            dst = piece_of(block, p)
            return pltpu.make_async_remote_copy(
                src_ref=w_ref.at[:, pl.ds(p * wp, wp)] if own else dst, dst_ref=dst,
                send_sem=send_sems.at[k, p], recv_sem=recv_sems.at[k, p], device_id=to, device_id_type=MESH)

        def load(src):
            cp = pltpu.make_async_copy(src, wbuf, load_sem)
            cp.start()
            cp.wait()

        pieces = range(GATHER_SPLIT)
        first = [cp for p in pieces for cp in
                 [copy(0, p, me, sibling, own=True)] + [copy(1 + j, p, me, (*chip, c), own=True)
                                                        for j, chip in enumerate(chips)]]
        passed = {(j, p): copy(4 + j, p, (*chip, c), sibling) for j, chip in enumerate(chips) for p in pieces}
        mine = pltpu.make_async_copy(w_ref, wall_ref.at[4 * mx + 2 * my + c], local_sem)

        @pl.when(i == 0)
        def _():
            for s, (kind, j, p) in enumerate(GATHER_STEPS):
                @pl.when(step == s)
                def _(s=s, kind=kind, j=j, p=p):
                    if s == 0:
                        for cp in first:
                            cp.start()
                        mine.start()
                    if kind == "own":
                        load(w_ref.at[:, pl.ds(p * wp, wp)])
                    elif kind == "sibling":
                        copy(0, p, sibling, me).wait_recv()
                        load(piece_of(sibling, p))
                    elif kind == "ici":
                        copy(1 + j, p, (*chips[j], c), me).wait_recv()
                        passed[j, p].start()
                        load(piece_of((*chips[j], c), p))
                    else:
                        copy(4 + j, p, (*chips[j], 1 - c), me).wait_recv()
                        load(piece_of((*chips[j], 1 - c), p))

        rows = pl.ds(pl.multiple_of(i * tm, tm), tm)

        @pl.when(step == 0)
        def _():
            xv = x_ref[...]
            r = lax.rsqrt(jnp.mean(xv * xv, axis=-1, keepdims=True) + EPS)
            hv = xv * r * g_ref[...]
            h_scr[rows, :] = hv.astype(BF16)
            ht_ref[...] = hv.T.astype(BF16)

        acc = _dot(h_scr[rows, :], wbuf[...])
        b = blk_ref[step]

        @pl.when(b >= 2)
        def _():
            o_ref[...] = acc

        @pl.when(b < 2)
        def _():
            scale = jnp.where(b == 1, HEAD_DIM ** -0.5, 1.0).astype(F32)
            cs, sn = cos_ref[...], sin_ref[...]
            for hh in range(wp // HEAD_DIM):
                cols = slice(hh * HEAD_DIM, (hh + 1) * HEAD_DIM)
                o_ref[:, cols] = _rot(acc[:, cols], cs, sn) * scale

        @pl.when((step == n_steps - 1) & (i == ni - 1))
        def _():
            for cp in first + list(passed.values()):
                cp.wait_send()
            mine.wait()

    hbm = pl.BlockSpec(memory_space=pltpu.HBM)
    rope = pl.BlockSpec((tm, HEAD_DIM), lambda s, i, blk, piece: (i, 0))
    first_pass = lambda s, i: jnp.where(s == 0, i, ni - 1)
    return pl.pallas_call(
        body, name="in_proj_gather",
        out_shape=(jax.ShapeDtypeStruct((N_DEV, S, W), F32), jax.ShapeDtypeStruct((N_DEV, D, W), BF16),
                   jax.ShapeDtypeStruct((D, S), BF16)),
        grid_spec=pltpu.PrefetchScalarGridSpec(
            num_scalar_prefetch=2, grid=(n_steps, ni),
            in_specs=[pl.BlockSpec((tm, D), lambda s, i, blk, piece: (first_pass(s, i), 0)),
                      pl.BlockSpec((1, D), lambda s, i, blk, piece: (0, 0)), hbm, rope, rope],
            out_specs=(pl.BlockSpec((None, tm, wp), lambda s, i, blk, piece: (blk[s], i, piece[s])), hbm,
                       pl.BlockSpec((D, tm), lambda s, i, blk, piece: (0, first_pass(s, i)))),
            scratch_shapes=[pltpu.VMEM((S, D), BF16), pltpu.VMEM((D, wp), BF16),
                            pltpu.SemaphoreType.DMA((7, GATHER_SPLIT)), pltpu.SemaphoreType.DMA((7, GATHER_SPLIT)),
                            pltpu.SemaphoreType.DMA, pltpu.SemaphoreType.DMA]),
        compiler_params=_params("arbitrary", "arbitrary"),
    )(*order, x, gain, w_shard, cos, sin)


def _head_spec(S, j):
    return pl.BlockSpec((None, S, HEAD_DIM), lambda h, *_: (j, 0, h))


def _bdot(a, b):
    return lax.dot_general(a, b, (((2,), (1,)), ((0,), (0,))), preferred_element_type=F32)


def _bdot_nt(a, b):
    return lax.dot_general(a, b, (((2,), (2,)), ((0,), (0,))), preferred_element_type=F32)


def _bdot_tn(a, b):
    return lax.dot_general(a, b, (((1,), (1,)), ((0,), (0,))), preferred_element_type=F32)


def _chunks(a):
    return a.reshape(a.shape[0] // CHUNK, CHUNK, a.shape[1])


def _ret_group(q, k, vb, states_b, dec, xi, ze):
    qb, kb = q.astype(BF16), k.astype(BF16)
    sb = (_bdot_nt(qb, kb) * dec).astype(BF16)
    qx = (q * xi).astype(BF16)
    out = _bdot(sb, vb) + _bdot(qx, states_b)
    return out, (qb, kb, sb, qx)


def _ret_states(kz, vb, gam, state, states_ref):
    kv = _bdot_tn(kz, vb)
    for u in range(RET_GROUP):
        states_ref[u] = state
        state = gam * state + kv[u]
    return state


def _table_specs():
    return [pl.BlockSpec((None, CHUNK, HEAD_DIM), lambda h, *_: (h, 0, 0))] * 4


def _ret_fwd(proj, tabs, gn_gain, gn_bias):
    _, S, W = proj.shape
    H, nc = W // HEAD_DIM, S // CHUNK
    assert nc % RET_GROUP == 0
    rows_per_group = RET_GROUP * CHUNK

    def body(q_ref, k_ref, v_ref, g_ref, dec_ref, xi_ref, ze_ref, gam_ref, gain_ref, bias_ref, o_ref, states_ref):
        dec, xi, ze, gam = dec_ref[...], xi_ref[...], ze_ref[...], gam_ref[...]
        gain, bias = gain_ref[...], bias_ref[...]

        def group(i, state):
            rows = pl.ds(pl.multiple_of(i * rows_per_group, rows_per_group), rows_per_group)
            q, k, vb = _chunks(q_ref[rows, :]), _chunks(k_ref[rows, :]), _chunks(v_ref[rows, :]).astype(BF16)
            state = _ret_states((k * ze).astype(BF16), vb, gam, state, states_ref)
            out, _ = _ret_group(q, k, vb, states_ref[...].astype(BF16), dec, xi, ze)
            mu = jnp.mean(out, axis=-1, keepdims=True)
            d = out - mu
            yn = d * lax.rsqrt(jnp.mean(d * d, axis=-1, keepdims=True) + EPS)
            g = _chunks(g_ref[rows, :])
            mix = g * _sigmoid(g) * (yn * gain + bias)
            o_ref[rows, :] = mix.reshape(rows_per_group, HEAD_DIM).astype(BF16)
            return state

        lax.fori_loop(0, nc // RET_GROUP, group, jnp.zeros((HEAD_DIM, HEAD_DIM), F32))

    vec = pl.BlockSpec((1, HEAD_DIM), lambda h: (0, h))
    return pl.pallas_call(
        body, name="ret_fwd", out_shape=jax.ShapeDtypeStruct((S, W), BF16), grid=(H,),
        in_specs=[_head_spec(S, 0), _head_spec(S, 1), _head_spec(S, 2), _head_spec(S, 3)] + _table_specs() + [vec, vec],
        out_specs=pl.BlockSpec((S, HEAD_DIM), lambda h: (0, h)),
        scratch_shapes=[pltpu.VMEM((RET_GROUP, HEAD_DIM, HEAD_DIM), F32)],
        compiler_params=_params("parallel"),
    )(proj, proj, proj, proj, *tabs, gn_gain, gn_bias)


def _sb_scores(qb, kk, masked, causal, upper):
    z = _dot_nt(qb, kk) * (HEAD_DIM ** -0.5)
    e = jnp.exp(-jnp.abs(z))
    l1p = jnp.log(1.0 + e)
    log_beta = jnp.minimum(z, 0.0) - l1p
    lk = jnp.minimum(-z, 0.0) - l1p
    if masked:
        lk = jnp.where(causal, lk, 0.0)
    hi = lk.astype(BF16)
    lo = (lk - hi.astype(F32)).astype(BF16)
    cs = _dot(hi, upper) + _dot(lo, upper)
    return log_beta, lk, cs


def _tri(B, kind):
    r = lax.broadcasted_iota(jnp.int32, (B, B), 0)
    c = lax.broadcasted_iota(jnp.int32, (B, B), 1)
    return {"gt": r > c, "lt": r < c}[kind]


def _ones_where(mask):
    return jnp.where(mask, 1.0, 0.0).astype(BF16)


def _exchange_with_all(src_ref, out_ref, send_sems, recv_sems, local_sem):
    x, y, c = _mesh_pos()
    peers = [(1 - x if k & 4 else x, 1 - y if k & 2 else y, 1 - c if k & 1 else c) for k in range(1, N_DEV)]

    def copy(k, owner, to):
        px, py, pc = owner
        return pltpu.make_async_remote_copy(
            src_ref=src_ref, dst_ref=out_ref.at[4 * px + 2 * py + pc], send_sem=send_sems.at[k],
            recv_sem=recv_sems.at[k], device_id=to, device_id_type=MESH)

    sends = [copy(k, (x, y, c), p) for k, p in enumerate(peers)]
    mine = pltpu.make_async_copy(src_ref, out_ref.at[4 * x + 2 * y + c], local_sem)

    def start():
        for cp in sends:
            cp.start()
        mine.start()

    def wait():
        for k, p in enumerate(peers):
            copy(k, p, p).wait_recv()
        for cp in sends:
            cp.wait_send()
        mine.wait()

    return start, wait


def _sb_fwd(proj, gain, wo_shard):
    _, S, W = proj.shape
    H = W // HEAD_DIM
    B = min(SB_BLOCK, S)
    nq = S // B
    assert nq <= HEAD_DIM and nq % SB_PER_STEP == 0
    ns = nq // SB_PER_STEP

    def body(q_ref, k_ref, v_ref, g_ref, gain_ref, wo_ref, mix_ref, raw_ref, car_ref, woall_ref, kb_ref, vb_ref,
             send_sems, recv_sems, local_sem):
        hd, si = pl.program_id(0), pl.program_id(1)
        start_gather, wait_gather = _exchange_with_all(wo_ref, woall_ref, send_sems, recv_sems, local_sem)
        pl.when((hd == 0) & (si == 0))(start_gather)

        @pl.when(si == 0)
        def _():
            kb_ref[...] = k_ref[...].astype(BF16)
            vb_ref[...] = v_ref[...].astype(BF16)

        causal = _tri(B, "gt")
        upper = _ones_where(causal)
        lane = lax.broadcasted_iota(jnp.int32, (B, HEAD_DIM), 1)

        def block(qb, kb, carry, acc, saved, masked):
            rows = pl.ds(pl.multiple_of(kb * B, B), B)
            log_beta, lk, cs = _sb_scores(qb, kb_ref[rows, :], masked, causal, upper)
            a = jnp.exp(log_beta + cs + carry)
            if masked:
                a = jnp.where(causal, a, 0.0)
            acc = acc + _dot(a.astype(BF16), vb_ref[rows, :])
            return carry + jnp.sum(lk, axis=1, keepdims=True), acc, jnp.where(lane == kb, carry, saved)

        init = (jnp.zeros((B, 1), F32), jnp.zeros((B, HEAD_DIM), F32), jnp.full((B, HEAD_DIM), NOT_VISITED, F32))

        def live(st):
            return (st[0] >= 0) & (jnp.max(st[1]) >= EXP_IS_ZERO_BELOW)

        def finish(u, acc, saved):
            rows = slice(u * B, (u + 1) * B)
            raw_ref[rows, :] = acc
            car_ref[rows, :] = saved
            yn = acc * lax.rsqrt(jnp.mean(acc * acc, axis=-1, keepdims=True) + EPS)
            g = g_ref[rows, :]
            mix_ref[rows, :] = (g * _sigmoid(g) * (yn * gain_ref[...])).astype(BF16)

        def whole(first_step):
            heads = []
            for u in range(SB_PER_STEP):
                qi = si * SB_PER_STEP + u
                qb = q_ref[u * B:(u + 1) * B, :].astype(BF16)
                state = block(qb, qi, *init, True)
                if not (first_step and u == 0):
                    state = block(qb, qi - 1, *state, False)
                heads.append((qi, qb, state))
            for u, (qi, qb, state) in enumerate(heads):
                if not (first_step and u == 0):
                    state = lax.while_loop(
                        live, lambda st, qb=qb: (st[0] - 1,) + block(qb, st[0], st[1], st[2], st[3], False),
                        (qi - 2,) + state)[1:]
                finish(u, state[1], state[2])

        pl.when(si == 0)(lambda: whole(True))
        pl.when(si > 0)(lambda: whole(False))
        pl.when((hd == H - 1) & (si == ns - 1))(wait_gather)

    tq = SB_PER_STEP * B
    tile = lambda j: pl.BlockSpec((None, tq, HEAD_DIM), lambda h, i: (j, i, h))
    out_tile = pl.BlockSpec((tq, HEAD_DIM), lambda h, i: (i, h))
    hbm = pl.BlockSpec(memory_space=pltpu.HBM)
    return pl.pallas_call(
        body, name="sb_fwd",
        out_shape=(jax.ShapeDtypeStruct((S, W), BF16), jax.ShapeDtypeStruct((S, W), F32),
                   jax.ShapeDtypeStruct((S, W), F32), jax.ShapeDtypeStruct((N_DEV,) + wo_shard.shape, BF16)),
        grid=(H, ns),
        in_specs=[tile(4), _head_spec(S, 5), _head_spec(S, 6), tile(7),
                  pl.BlockSpec((1, HEAD_DIM), lambda h, i: (0, h)), hbm],
        out_specs=(out_tile, out_tile, out_tile, hbm),
        scratch_shapes=[pltpu.VMEM((S, HEAD_DIM), BF16), pltpu.VMEM((S, HEAD_DIM), BF16),
                        pltpu.SemaphoreType.DMA((N_DEV - 1,)), pltpu.SemaphoreType.DMA((N_DEV - 1,)),
                        pltpu.SemaphoreType.DMA],
        compiler_params=_params("arbitrary", "arbitrary"),
    )(proj, proj, proj, proj, gain, wo_shard)


def _out_proj_loss(mix_r, mix_s, w_out, x, tgt, gf):
    S, W = mix_r.shape
    D = x.shape[1]
    tm = min(256, S)

    def body(mr_ref, ms_ref, wo_ref, x_ref, t_ref, gf_ref, dx2_ref, dx2b_ref, dmix_ref, loss_ref, gfn_ref):
        @pl.when(pl.program_id(0) == 0)
        def _():
            loss_ref[...] = jnp.zeros_like(loss_ref)
            gfn_ref[...] = jnp.zeros_like(gfn_ref)

        gfv = gf_ref[...]
        x2 = x_ref[...] + (_dot(mr_ref[...], wo_ref[:W, :]) + _dot(ms_ref[...], wo_ref[W:, :]))
        r2 = lax.rsqrt(jnp.mean(x2 * x2, axis=-1, keepdims=True) + EPS)
        n = x2 * r2
        err = n * gfv - t_ref[...]
        loss_ref[...] += 0.5 * jnp.sum(jnp.mean(err * err, axis=-1, keepdims=True))
        dy = err * (1.0 / D)
        gfn_ref[...] += jnp.sum(dy * n, axis=0, keepdims=True)
        dn = dy * gfv
        dx2 = r2 * (dn - n * jnp.mean(dn * n, axis=-1, keepdims=True))
        dx2_ref[...] = dx2
        b = dx2.astype(BF16)
        dx2b_ref[...] = b
        dmix_ref[:, :W] = _dot_nt(b, wo_ref[:W, :])
        dmix_ref[:, W:] = _dot_nt(b, wo_ref[W:, :])

    row = lambda width: pl.BlockSpec((tm, width), lambda i: (i, 0))
    return pl.pallas_call(
        body, name="out_proj_loss",
        out_shape=(jax.ShapeDtypeStruct((S, D), F32), jax.ShapeDtypeStruct((S, D), BF16),
                   jax.ShapeDtypeStruct((S, 2 * W), F32), jax.ShapeDtypeStruct((SUBLANES, LANES), F32),
                   jax.ShapeDtypeStruct((1, D), F32)),
        grid=(S // tm,),
        in_specs=[row(W), row(W), pl.BlockSpec((2 * W, D), lambda i: (0, 0)), row(D), row(D),
                  pl.BlockSpec((1, D), lambda i: (0, 0))],
        out_specs=(row(D), row(D), row(2 * W), pl.BlockSpec((SUBLANES, LANES), lambda i: (0, 0)),
                   pl.BlockSpec((1, D), lambda i: (0, 0))),
        compiler_params=_params("arbitrary"),
    )(mix_r, mix_s, w_out, x, tgt, gf)


def _silu_bwd(g, dm, normed):
    sig = _sigmoid(g)
    return dm * (g * sig), dm * normed * (sig * (1.0 + g * (1.0 - sig)))


def _ret_bwd(proj, dmix, tabs, gn_gain, gn_bias, cos, sin, gwo):
    _, S, W = proj.shape
    H, nc = W // HEAD_DIM, S // CHUNK
    assert nc % RET_GROUP == 0
    ng = nc // RET_GROUP
    rows_per_group = RET_GROUP * CHUNK

    def body(q_ref, k_ref, v_ref, g_ref, dm_ref, dec_ref, xi_ref, ze_ref, gam_ref, gain_ref, bias_ref, cos_ref,
             sin_ref, gwo_ref, dp_ref, dgain_ref, dbias_ref, rino_ref, rs_ref, dstates_ref, send_sems, recv_sems):
        dec, xi, ze, gam = dec_ref[...], xi_ref[...], ze_ref[...], gam_ref[...]
        gain, bias = gain_ref[...], bias_ref[...]
        hd = pl.program_id(0)
        other_core = 1 - lax.axis_index("c")
        copies = _to_sibling_copies([gwo_ref.at[2 * k + other_core] for k in range(4)], rino_ref, send_sems, recv_sems)

        @pl.when(hd == 0)
        def _():
            for cp in copies:
                cp.start()

        def group_rows(i):
            return pl.ds(pl.multiple_of(i * rows_per_group, rows_per_group), rows_per_group)

        def fwd_group(i, state):
            rows = group_rows(i)
            kz = (_chunks(k_ref[rows, :]) * ze).astype(BF16)
            return _ret_states(kz, _chunks(v_ref[rows, :]).astype(BF16), gam, state,
                               rs_ref.at[pl.ds(i * RET_GROUP, RET_GROUP)])

        lax.fori_loop(0, ng, fwd_group, jnp.zeros((HEAD_DIM, HEAD_DIM), F32))

        flat = lambda a: a.reshape(rows_per_group, HEAD_DIM)

        def bwd_group(t, carry):
            dgain, dbias, dstate = carry
            i = ng - 1 - t
            rows = group_rows(i)
            q, k, g = _chunks(q_ref[rows, :]), _chunks(k_ref[rows, :]), _chunks(g_ref[rows, :])
            vb = _chunks(v_ref[rows, :]).astype(BF16)
            rb = rs_ref[pl.ds(i * RET_GROUP, RET_GROUP)].astype(BF16)
            out, (qb, kb, sb, qx) = _ret_group(q, k, vb, rb, dec, xi, ze)
            kz = (k * ze).astype(BF16)
            mu = jnp.mean(out, axis=-1, keepdims=True)
            d = out - mu
            rstd = lax.rsqrt(jnp.mean(d * d, axis=-1, keepdims=True) + EPS)
            yn = d * rstd
            dgn, dg = _silu_bwd(g, _chunks(dm_ref[rows, :]), yn * gain + bias)
            dgain = dgain + jnp.sum(flat(dgn * yn), axis=0, keepdims=True)
            dbias = dbias + jnp.sum(flat(dgn), axis=0, keepdims=True)
            dyn = dgn * gain
            do = rstd * (dyn - jnp.mean(dyn, axis=-1, keepdims=True)
                         - yn * jnp.mean(dyn * yn, axis=-1, keepdims=True))
            dob = do.astype(BF16)
            dkv = _bdot_tn(qx, dob)
            for u in reversed(range(RET_GROUP)):
                dstates_ref[u] = dstate
                dstate = gam * dstate + dkv[u]
            drb = dstates_ref[...].astype(BF16)
            dv = _bdot_tn(sb, dob) + _bdot(kz, drb)
            dsb = (_bdot_nt(dob, vb) * dec).astype(BF16)
            dq = _bdot(dsb, kb) + _bdot_nt(dob, rb) * xi
            dk = _bdot_tn(dsb, qb) + _bdot_nt(vb, drb) * ze
            cs, sn = cos_ref[rows, :], -sin_ref[rows, :]
            dp_ref[0, rows, :] = _rot(flat(dq), cs, sn).astype(BF16)
            dp_ref[1, rows, :] = (_rot(flat(dk), cs, sn) * (HEAD_DIM ** -0.5)).astype(BF16)
            dp_ref[2, rows, :] = flat(dv).astype(BF16)
            dp_ref[3, rows, :] = flat(dg).astype(BF16)
            return dgain, dbias, dstate

        zero = jnp.zeros((1, HEAD_DIM), F32)
        dgain, dbias, _ = lax.fori_loop(0, ng, bwd_group, (zero, zero, jnp.zeros((HEAD_DIM, HEAD_DIM), F32)))
        dgain_ref[...] = dgain
        dbias_ref[...] = dbias

        @pl.when(hd == H - 1)
        def _():
            for cp in copies:
                cp.wait_recv()
            for cp in copies:
                cp.wait_send()

    vec = pl.BlockSpec((1, HEAD_DIM), lambda h: (0, h))
    full = pl.BlockSpec((S, HEAD_DIM), lambda h: (0, 0))
    hbm = pl.BlockSpec(memory_space=pltpu.HBM)
    return pl.pallas_call(
        body, name="ret_bwd",
        out_shape=(jax.ShapeDtypeStruct((4, S, W), BF16), jax.ShapeDtypeStruct((1, W), F32),
                   jax.ShapeDtypeStruct((1, W), F32), jax.ShapeDtypeStruct((4,) + gwo.shape[1:], gwo.dtype)),
        grid=(H,),
        in_specs=[_head_spec(S, 0), _head_spec(S, 1), _head_spec(S, 2), _head_spec(S, 3),
                  pl.BlockSpec((S, HEAD_DIM), lambda h: (0, h))] + _table_specs() + [vec, vec, full, full, hbm],
        out_specs=(pl.BlockSpec((4, S, HEAD_DIM), lambda h: (0, 0, h)), vec, vec, hbm),
        scratch_shapes=[pltpu.VMEM((nc, HEAD_DIM, HEAD_DIM), F32), pltpu.VMEM((RET_GROUP, HEAD_DIM, HEAD_DIM), F32),
                        pltpu.SemaphoreType.DMA((4,)), pltpu.SemaphoreType.DMA((4,))],
        compiler_params=_params("arbitrary"),
    )(proj, proj, proj, proj, dmix, *tabs, gn_gain, gn_bias, cos, sin, gwo)


def _sb_bwd(proj, raw, carries, dmix, gain, chip_sums_o):
    _, S, W = proj.shape
    H = W // HEAD_DIM
    B = min(SB_BLOCK, S)
    nq = S // B
    ns = nq // SB_PER_STEP

    def body(q_ref, k_ref, v_ref, g_ref, raw_ref, car_ref, dm_ref, gain_ref, so_ref, dp_ref, dgain_ref, ro_ref,
             kb_ref, vb_ref, dk_ref, dv_ref, send_sems, recv_sems):
        hd, si = pl.program_id(0), pl.program_id(1)
        start_exchange, wait_exchange = _exchange_chip_sums((so_ref,), (ro_ref,), send_sems, recv_sems)
        pl.when((hd == 0) & (si == 0))(start_exchange)

        @pl.when(si == 0)
        def _():
            kb_ref[...] = k_ref[...].astype(BF16)
            vb_ref[...] = v_ref[...].astype(BF16)
            dk_ref[...] = jnp.zeros_like(dk_ref)
            dv_ref[...] = jnp.zeros_like(dv_ref)
            dgain_ref[...] = jnp.zeros_like(dgain_ref)

        causal = _tri(B, "gt")
        upper = _ones_where(causal)
        before = _ones_where(_tri(B, "lt"))
        lane = lax.broadcasted_iota(jnp.int32, (B, HEAD_DIM), 1)
        gain_v = gain_ref[...]

        def prologue(u):
            qi = si * SB_PER_STEP + u
            rows = slice(u * B, (u + 1) * B)
            o = raw_ref[rows, :]
            rstd = lax.rsqrt(jnp.mean(o * o, axis=-1, keepdims=True) + EPS)
            yn = o * rstd
            dnrm, dg = _silu_bwd(g_ref[rows, :], dm_ref[rows, :], yn * gain_v)
            dp_ref[3, pl.ds(pl.multiple_of(qi * B, B), B), :] = dg.astype(BF16)
            dgain_ref[...] += jnp.sum(dnrm * yn, axis=0, keepdims=True)
            dyn = dnrm * gain_v
            do = rstd * (dyn - yn * jnp.mean(dyn * yn, axis=-1, keepdims=True))
            return qi, q_ref[rows, :].astype(BF16), do.astype(BF16), car_ref[rows, :]

        def block(ctx, kb, carry_g, dq, masked):
            _, qb, dob, saved = ctx
            rows = pl.ds(pl.multiple_of(kb * B, B), B)
            kk, vv = kb_ref[rows, :], vb_ref[rows, :]
            log_beta, _, cs = _sb_scores(qb, kk, masked, causal, upper)
            carry_lk = jnp.sum(jnp.where(lane == kb, saved, 0.0), axis=1, keepdims=True)
            a = jnp.exp(log_beta + cs + carry_lk)
            if masked:
                a = jnp.where(causal, a, 0.0)
            gmat = _dot_nt(dob, vv) * a
            dv_ref[rows, :] += _dot_tn(a.astype(BF16), dob)
            hi = gmat.astype(BF16)
            lo = (gmat - hi.astype(F32)).astype(BF16)
            dlk = carry_g + (_dot(hi, before) + _dot(lo, before))
            beta = jnp.exp(log_beta)
            dz = (gmat * (1.0 - beta) - dlk * beta) * (HEAD_DIM ** -0.5)
            if masked:
                dz = jnp.where(causal, dz, 0.0)
            dzb = dz.astype(BF16)
            dk_ref[rows, :] += _dot_tn(dzb, qb)
            return carry_g + jnp.sum(gmat, axis=1, keepdims=True), dq + _dot(dzb, kk)

        init = (jnp.zeros((B, 1), F32), jnp.zeros((B, HEAD_DIM), F32))

        def whole(first_step):
            ctxs = [prologue(u) for u in range(SB_PER_STEP)]
            states = []
            for u, ctx in enumerate(ctxs):
                state = init
                if not (first_step and u == 0):
                    visited = jnp.max(ctx[3], axis=0, keepdims=True) >= EXP_IS_ZERO_BELOW
                    first = jnp.min(jnp.where(visited, lane[:1, :], ctx[0]))
                    state = lax.fori_loop(first, ctx[0] - 1,
                                          lambda i, st, ctx=ctx: block(ctx, i, st[0], st[1], False), state)
                states.append(state)
            for u, (ctx, state) in enumerate(zip(ctxs, states)):
                if not (first_step and u == 0):
                    state = block(ctx, ctx[0] - 1, *state, False)
                state = block(ctx, ctx[0], *state, True)
                dp_ref[0, pl.ds(pl.multiple_of(ctx[0] * B, B), B), :] = state[1].astype(BF16)

        pl.when(si == 0)(lambda: whole(True))
        pl.when(si > 0)(lambda: whole(False))

        @pl.when(si == ns - 1)
        def _():
            dp_ref[1] = dk_ref[...].astype(BF16)
            dp_ref[2] = dv_ref[...].astype(BF16)

        pl.when((hd == H - 1) & (si == ns - 1))(wait_exchange)

    tq = SB_PER_STEP * B
    tile = lambda j: pl.BlockSpec((None, tq, HEAD_DIM), lambda h, i: (j, i, h))
    vec = pl.BlockSpec((1, HEAD_DIM), lambda h, i: (0, h))
    hbm = pl.BlockSpec(memory_space=pltpu.HBM)
    return pl.pallas_call(
        body, name="sb_bwd",
        out_shape=(jax.ShapeDtypeStruct((4, S, W), BF16), jax.ShapeDtypeStruct((1, W), F32),
                   jax.ShapeDtypeStruct((3,) + chip_sums_o.shape[1:], chip_sums_o.dtype)),
        grid=(H, ns),
        in_specs=[tile(4), _head_spec(S, 5), _head_spec(S, 6), tile(7),
                  pl.BlockSpec((tq, HEAD_DIM), lambda h, i: (i, h)),
                  pl.BlockSpec((tq, HEAD_DIM), lambda h, i: (i, h)),
                  pl.BlockSpec((tq, HEAD_DIM), lambda h, i: (i, H + h)), vec, hbm],
        out_specs=(pl.BlockSpec((4, S, HEAD_DIM), lambda h, i: (0, 0, h)), vec, hbm),
        scratch_shapes=[pltpu.VMEM((S, HEAD_DIM), BF16), pltpu.VMEM((S, HEAD_DIM), BF16),
                        pltpu.VMEM((S, HEAD_DIM), F32), pltpu.VMEM((S, HEAD_DIM), F32),
                        pltpu.SemaphoreType.DMA((1, 3)), pltpu.SemaphoreType.DMA((1, 3))],
        compiler_params=_params("arbitrary", "arbitrary"),
    )(proj, proj, proj, proj, raw, carries, dmix, gain, chip_sums_o)


def _grad_w_in_half(ht, dpr, dps, core, name, to_sibling=None):
    D, S = ht.shape
    _, _, W = dpr.shape
    tmm = min(512, D)
    nm = D // tmm

    def body(core_ref, ht_ref, r_ref, s_ref, *rest):
        o_ref = rest[1] if to_sibling is not None else rest[0]
        q, m = pl.program_id(0), pl.program_id(1)
        if to_sibling is not None:
            ga_ref, _, rin_ref, send_sems, recv_sems = rest
            copies = _to_sibling_copies([ga_ref.at[k] for k in range(4)], rin_ref, send_sems, recv_sems)

            @pl.when((q == 0) & (m == 0))
            def _():
                for cp in copies:
                    cp.start()

        @pl.when(q < 2)
        def _():
            o_ref[...] = _dot(ht_ref[...], r_ref[...])

        @pl.when(q >= 2)
        def _():
            o_ref[...] = _dot(ht_ref[...], s_ref[...])

        if to_sibling is not None:
            @pl.when((q == 3) & (m == nm - 1))
            def _():
                for cp in copies:
                    cp.wait_recv()
                for cp in copies:
                    cp.wait_send()

    hbm = pl.BlockSpec(memory_space=pltpu.HBM)
    gw_shape = jax.ShapeDtypeStruct((4, D, W), F32)
    out_shape, out_specs, extra_in, scratch = (gw_shape,), (pl.BlockSpec((None, tmm, W), lambda q, m, core: (q, m, 0)),), [], []
    if to_sibling is not None:
        out_shape += (gw_shape,)
        out_specs += (hbm,)
        extra_in = [hbm]
        scratch = [pltpu.SemaphoreType.DMA((4,)), pltpu.SemaphoreType.DMA((4,))]
    return pl.pallas_call(
        body, name=name, out_shape=out_shape,
        grid_spec=pltpu.PrefetchScalarGridSpec(
            num_scalar_prefetch=1, grid=(4, nm),
            in_specs=[pl.BlockSpec((tmm, S), lambda q, m, core: (m, 0)),
                      pl.BlockSpec((None, S, W), lambda q, m, core: (jnp.minimum(2 * q + core[0], 3), 0, 0)),
                      pl.BlockSpec((None, S, W), lambda q, m, core: (jnp.maximum(2 * q + core[0] - 4, 0), 0, 0))]
            + extra_in,
            out_specs=out_specs, scratch_shapes=scratch),
        compiler_params=_params("arbitrary", "arbitrary"),
    )(core, ht, dpr, dps, *(() if to_sibling is None else (to_sibling,)))


def _grad_w_out(mix_r, mix_s, dx2b):
    S, W = mix_r.shape
    D = dx2b.shape[1]
    tmm = min(512, W)
    tk = min(1024, S)

    def body(r_ref, s_ref, b_ref, o_ref):
        j, kk = pl.program_id(0), pl.program_id(2)

        def acc(a_ref):
            part = _dot_tn(a_ref[...], b_ref[...])

            @pl.when(kk == 0)
            def _():
                o_ref[...] = part

            @pl.when(kk > 0)
            def _():
                o_ref[...] += part

        pl.when(j == 0)(lambda: acc(r_ref))
        pl.when(j == 1)(lambda: acc(s_ref))

    return pl.pallas_call(
        body, name="grad_w_out", out_shape=jax.ShapeDtypeStruct((2, W, D), F32), grid=(2, W // tmm, S // tk),
        in_specs=[pl.BlockSpec((tk, tmm), lambda j, m, k: (k, m)),
                  pl.BlockSpec((tk, tmm), lambda j, m, k: (k, m)),
                  pl.BlockSpec((tk, D), lambda j, m, k: (k, 0))],
        out_specs=pl.BlockSpec((None, tmm, D), lambda j, m, k: (j, m, 0)),
        compiler_params=_params("parallel", "parallel", "arbitrary"),
    )(mix_r, mix_s, dx2b)


def _dh_matmul(dpr, dps, w_all, chip_sums):
    _, S, W = dpr.shape
    D = w_all.shape[1]
    tm = min(1024, S)
    ni = S // tm

    def body(r_ref, s_ref, w_ref, sa_ref, dh_ref, ra_ref, send_sems, recv_sems):
        i, j = pl.program_id(0), pl.program_id(1)
        start_exchange, wait_exchange = _exchange_chip_sums((sa_ref,), (ra_ref,), send_sems, recv_sems)
        pl.when((i == 0) & (j == 0))(start_exchange)

        def acc(b_ref):
            part = _dot_nt(b_ref[...], w_ref[...])

            @pl.when(j == 0)
            def _():
                dh_ref[...] = part

            @pl.when(j > 0)
            def _():
                dh_ref[...] += part

        pl.when(j < 4)(lambda: acc(r_ref))
        pl.when(j >= 4)(lambda: acc(s_ref))
        pl.when((i == ni - 1) & (j == 7))(wait_exchange)

    hbm = pl.BlockSpec(memory_space=pltpu.HBM)
    return pl.pallas_call(
        body, name="dh_matmul",
        out_shape=(jax.ShapeDtypeStruct((S, D), F32), jax.ShapeDtypeStruct((3,) + chip_sums.shape[1:], chip_sums.dtype)),
        grid=(ni, 8),
        in_specs=[pl.BlockSpec((None, tm, W), lambda i, j: (jnp.minimum(j, 3), i, 0)),
                  pl.BlockSpec((None, tm, W), lambda i, j: (jnp.maximum(j - 4, 0), i, 0)),
                  pl.BlockSpec((None, D, W), lambda i, j: (j, 0, 0)), hbm],
        out_specs=(pl.BlockSpec((tm, D), lambda i, j: (i, 0)), hbm),
        scratch_shapes=[pltpu.SemaphoreType.DMA((1, 3)), pltpu.SemaphoreType.DMA((1, 3))],
        compiler_params=_params("arbitrary", "arbitrary"),
    )(dpr, dps, w_all, chip_sums)


def _norm_bwd(x, dx2, dh, gain):
    S, D = x.shape
    tm = min(256, S)

    def body(x_ref, dx2_ref, dh_ref, g_ref, gx_ref, dgain_ref):
        @pl.when(pl.program_id(0) == 0)
        def _():
            dgain_ref[...] = jnp.zeros_like(dgain_ref)

        xv, dh_v = x_ref[...], dh_ref[...]
        r1 = lax.rsqrt(jnp.mean(xv * xv, axis=-1, keepdims=True) + EPS)
        n = xv * r1
        dgain_ref[...] += jnp.sum(dh_v * n, axis=0, keepdims=True)
        dn = dh_v * g_ref[...]
        gx_ref[...] = dx2_ref[...] + r1 * (dn - n * jnp.mean(dn * n, axis=-1, keepdims=True))

    row = pl.BlockSpec((tm, D), lambda i: (i, 0))
    one = pl.BlockSpec((1, D), lambda i: (0, 0))
    return pl.pallas_call(
        body, name="norm_bwd", out_shape=(jax.ShapeDtypeStruct((S, D), F32), jax.ShapeDtypeStruct((1, D), F32)),
        grid=(S // tm,), in_specs=[row, row, row, one], out_specs=(row, one),
        compiler_params=_params("arbitrary"),
    )(x, dx2, dh, gain)


def _own_block(gw, pos, q):
    return q if gw.shape[0] == 4 else 2 * q + pos[0]


def _rs_local_sum(gw, rin, pos):
    _, R, C = gw.shape
    tr = min(256, R)
    other = lambda k, pos: (pos[1] + 1 + k) % 4

    def body(pos_ref, a_ref, b_ref, o_ref):
        o_ref[...] = (a_ref[...] + b_ref[...]).astype(BF16)

    return pl.pallas_call(
        body, name="rs_local_sum", out_shape=jax.ShapeDtypeStruct((4, R, C), BF16),
        grid_spec=pltpu.PrefetchScalarGridSpec(
            num_scalar_prefetch=1, grid=(3, R // tr),
            in_specs=[pl.BlockSpec((None, tr, C), lambda k, i, pos: (_own_block(gw, pos, other(k, pos)), i, 0)),
                      pl.BlockSpec((None, tr, C), lambda k, i, pos: (other(k, pos), i, 0))],
            out_specs=pl.BlockSpec((None, tr, C), lambda k, i, pos: (other(k, pos), i, 0))),
        compiler_params=_params("parallel", "parallel"),
    )(pos, gw, rin)


def _adamw(w, g, m, v):
    m2 = ADAM_B1 * m + (1.0 - ADAM_B1) * g
    v2 = ADAM_B2 * v + (1.0 - ADAM_B2) * (g * g)
    m_hat = m2 / (1.0 - ADAM_B1 ** ADAM_STEP)
    v_hat = v2 / (1.0 - ADAM_B2 ** ADAM_STEP)
    delta = -ADAM_LR * (m_hat / (jnp.sqrt(v_hat) + ADAM_EPS) + ADAM_WD * w)
    return delta, m2, v2


def _adamw_shard(gw, rin, rb, w, m, v, pos):
    _, R, C = gw.shape
    tr = min(256, R)

    def body(pos_ref, a_ref, b_ref, rb_ref, w_ref, m_ref, v_ref, g_ref, d_ref, m2_ref, v2_ref):
        g = a_ref[...] + b_ref[...]
        for k in range(3):
            g = g + rb_ref[k].astype(F32)
        g_ref[...] = g
        d_ref[...], m2_ref[...], v2_ref[...] = _adamw(w_ref[...], g, m_ref[...], v_ref[...])

    plain = pl.BlockSpec((tr, C), lambda i, pos: (i, 0))
    shape = jax.ShapeDtypeStruct((R, C), F32)
    return pl.pallas_call(
        body, name="adamw_shard", out_shape=(shape,) * 4,
        grid_spec=pltpu.PrefetchScalarGridSpec(
            num_scalar_prefetch=1, grid=(R // tr,),
            in_specs=[pl.BlockSpec((None, tr, C), lambda i, pos: (_own_block(gw, pos, pos[1]), i, 0)),
                      pl.BlockSpec((None, tr, C), lambda i, pos: (pos[1], i, 0)),
                      pl.BlockSpec((3, tr, C), lambda i, pos: (0, i, 0)), plain, plain, plain],
            out_specs=(plain,) * 4),
        compiler_params=_params("parallel"),
    )(pos, gw, rin, rb, w, m, v)


def _adamw_small(parts, w, m, v):
    _, rows, n = parts.shape

    def body(p_ref, w_ref, m_ref, v_ref, g_ref, d_ref, m2_ref, v2_ref):
        g = p_ref[0]
        for d in range(1, N_DEV):
            g = g + p_ref[d]
        g_ref[...] = g
        d_ref[...], m2_ref[...], v2_ref[...] = _adamw(w_ref[...], g, m_ref[...], v_ref[...])

    shape = jax.ShapeDtypeStruct((rows, n), F32)
    return pl.pallas_call(body, name="adamw_small", out_shape=(shape,) * 4)(parts, w, m, v)


def _rope_tables(S):
    half = HEAD_DIM // 2
    inv = ROPE_THETA ** (-jnp.arange(half, dtype=F32) / half)
    ang = jnp.arange(S, dtype=F32)[:, None] * inv[None, :]
    cos, sin = jnp.cos(ang), jnp.sin(ang)
    return jnp.concatenate([cos, cos], axis=1), jnp.concatenate([-sin, sin], axis=1)


def _retention_tables(H):
    lg = jnp.log1p(-jnp.exp2(-5.0 - jnp.arange(H, dtype=F32)))
    n = jnp.arange(CHUNK, dtype=F32)
    rel = n[:, None] - n[None, :]
    decay = jnp.where(rel >= 0, jnp.exp(lg[:, None, None] * jnp.maximum(rel, 0.0)), 0.0)
    shape = (H, CHUNK, HEAD_DIM)
    xi = jnp.broadcast_to(jnp.exp(lg[:, None] * (n + 1.0))[:, :, None], shape)
    zeta = jnp.broadcast_to(jnp.exp(lg[:, None] * (CHUNK - 1.0 - n))[:, :, None], shape)
    gamma_c = jnp.broadcast_to(jnp.exp(lg * CHUNK)[:, None, None], shape)
    return decay, xi, zeta, gamma_c


def _pack_small(parts):
    flat = []
    for p in parts:
        p = p.reshape(-1)
        flat.append(jnp.pad(p, (0, -p.shape[0] % LANES)))
    flat = jnp.concatenate(flat)
    return jnp.pad(flat, (0, SMALL_N - flat.shape[0])).reshape(SUBLANES, SMALL_N // SUBLANES)


def _unpack_small(packed, shapes):
    flat = packed.reshape(-1)
    out, at = [], 0
    for shp in shapes:
        size = 1
        for s in shp:
            size *= s
        out.append(flat[at:at + size].reshape(shp))
        at += size + (-size % LANES)
    return out


def kernel(x, norm_gain, w_in, ret_gn_gain, ret_gn_bias, sb_norm_gain, w_out, final_norm_gain, loss_target, m_norm_gain, m_w_in, m_ret_gn_gain, m_ret_gn_bias, m_sb_norm_gain, m_w_out, m_final_norm_gain, v_norm_gain, v_w_in, v_ret_gn_gain, v_ret_gn_bias, v_sb_norm_gain, v_w_out, v_final_norm_gain):
    S, D = x.shape[1], x.shape[2]
    W = w_in.shape[2]
    wo_rows = w_out.shape[1]
    H = W // HEAD_DIM
    xs, tgt = x[0], loss_target[0]
    mx, my, mc = _mesh_pos()
    pos = jnp.stack([mc, 2 * mx + my]).astype(jnp.int32)

    cos, sin = _rope_tables(S)
    tabs = _retention_tables(H)

    proj, w_all, ht = _in_proj_gather(xs, norm_gain, w_in[0].astype(BF16), cos, sin, _gather_order())
    mix_r = _ret_fwd(proj, tabs, ret_gn_gain, ret_gn_bias)
    mix_s, raw_s, carries, wo_all = _sb_fwd(proj, sb_norm_gain, w_out[0].astype(BF16))
    wo_full = wo_all.reshape(N_DEV * wo_rows, D)
    dx2, dx2b, dmix, loss_p, d_gf = _out_proj_loss(mix_r, mix_s, wo_full, xs, tgt, final_norm_gain[None])

    gwo = _grad_w_out(mix_r, mix_s, dx2b).reshape(N_DEV, wo_rows, D)
    dpr, d_rgain, d_rbias, rino = _ret_bwd(proj, dmix, tabs, ret_gn_gain, ret_gn_bias, cos, sin, gwo)
    dps, d_sgain, rbo = _sb_bwd(proj, raw_s, carries, dmix, sb_norm_gain, _rs_local_sum(gwo, rino, pos))
    gw_sibling, = _grad_w_in_half(ht, dpr, dps, (1 - mc).reshape(1).astype(jnp.int32), "grad_w_in_sibling")
    gw, rin = _grad_w_in_half(ht, dpr, dps, mc.reshape(1).astype(jnp.int32), "grad_w_in_own", to_sibling=gw_sibling)
    dh, rb = _dh_matmul(dpr, dps, w_all, _rs_local_sum(gw, rin, pos))
    grad_x, d_gain = _norm_bwd(xs, dx2, dh, norm_gain)
    g_in, d_in, m_in, v_in = _adamw_shard(gw, rin, rb, w_in[0], m_w_in[0], v_w_in[0], pos)
    g_out, d_out, m_out, v_out = _adamw_shard(gwo, rino, rbo, w_out[0], m_w_out[0], v_w_out[0], pos)

    small_w = [norm_gain, ret_gn_gain, ret_gn_bias, sb_norm_gain, final_norm_gain]
    small_m = [m_norm_gain, m_ret_gn_gain, m_ret_gn_bias, m_sb_norm_gain, m_final_norm_gain]
    small_v = [v_norm_gain, v_ret_gn_gain, v_ret_gn_bias, v_sb_norm_gain, v_final_norm_gain]
    shapes = [()] + [w.shape for w in small_w]
    zero = jnp.zeros((), F32)
    parts = _small_all_gather(_pack_small([loss_p[0, 0], d_gain, d_rgain, d_rbias, d_sgain, d_gf]))
    packed = _adamw_small(parts, _pack_small([zero] + small_w), _pack_small([zero] + small_m),
                          _pack_small([zero] + small_v))
    g_s, d_s, m_s, v_s = (_unpack_small(p, shapes) for p in packed)

    grads = [g_s[1], g_in[None], g_s[2], g_s[3], g_s[4], g_out[None], g_s[5]]
    deltas = [d_s[1], d_in[None], d_s[2], d_s[3], d_s[4], d_out[None], d_s[5]]
    new_m = [m_s[1], m_in[None], m_s[2], m_s[3], m_s[4], m_out[None], m_s[5]]
    new_v = [v_s[1], v_in[None], v_s[2], v_s[3], v_s[4], v_out[None], v_s[5]]
    return (g_s[0], grad_x[None], *grads, *deltas, *new_m, *new_v)
```

```python
import functools

import jax
import jax.numpy as jnp
from jax import lax
from jax.experimental import pallas as pl
from jax.experimental.pallas import tpu as pltpu

F32 = jnp.float32
BF16 = jnp.bfloat16

HEAD_DIM = 128
CHUNK = 128
RET_GROUP = 8
ROPE_THETA = 10000.0
EPS = 1e-6
ADAM_LR = 0.001
ADAM_B1 = 0.9
ADAM_B2 = 0.999
ADAM_EPS = 1e-08
ADAM_WD = 0.01
ADAM_STEP = 10

N_DEV = 8
LANES = 128
SUBLANES = 8
VMEM_LIMIT = 56 * 1024 * 1024
SB_BLOCK = 256
SB_PER_STEP = 4
SMALL_N = 8192
EXP_IS_ZERO_BELOW = -104.0
NOT_VISITED = -1e30
MESH = pl.DeviceIdType.MESH

NT = (((1,), (1,)), ((), ()))
TN = (((0,), (0,)), ((), ()))


def _params(*sem):
    return pltpu.CompilerParams(dimension_semantics=sem if sem else None, vmem_limit_bytes=VMEM_LIMIT)


def _dot(a, b):
    return jnp.dot(a, b, preferred_element_type=F32)


def _dot_nt(a, b):
    return lax.dot_general(a, b, NT, preferred_element_type=F32)


def _dot_tn(a, b):
    return lax.dot_general(a, b, TN, preferred_element_type=F32)


def _sigmoid(g):
    return 1.0 / (1.0 + jnp.exp(-g))


def _rot(a, cos, sin_signed):
    return a * cos + pltpu.roll(a, HEAD_DIM // 2, 1) * sin_signed


def _mesh_pos():
    return lax.axis_index("x"), lax.axis_index("y"), lax.axis_index("c")


def _to_sibling_copies(blocks, out_ref, send_sems, recv_sems):
    x, y, c = _mesh_pos()
    return [pltpu.make_async_remote_copy(
        src_ref=block, dst_ref=out_ref.at[k], send_sem=send_sems.at[k], recv_sem=recv_sems.at[k],
        device_id=(x, y, 1 - c), device_id_type=MESH) for k, block in enumerate(blocks)]


def _exchange_chip_sums(srcs, outs, send_sems, recv_sems):
    x, y, c = _mesh_pos()
    copies = []
    for arr, (src, out) in enumerate(zip(srcs, outs)):
        for k in range(1, 4):
            px = 1 - x if k & 2 else x
            py = 1 - y if k & 1 else y
            copies.append(pltpu.make_async_remote_copy(
                src_ref=src.at[2 * px + py], dst_ref=out.at[k - 1],
                send_sem=send_sems.at[arr, k - 1], recv_sem=recv_sems.at[arr, k - 1],
                device_id=(px, py, c), device_id_type=MESH))

    def start():
        for cp in copies:
            cp.start()

    def wait():
        for cp in copies:
            cp.wait_recv()
        for cp in copies:
            cp.wait_send()

    return start, wait


def _small_all_gather(small):
    rows, n = small.shape

    def body(s_ref, o_ref, send_sems, recv_sems, local_sem):
        start, wait = _exchange_with_all(s_ref, o_ref, send_sems, recv_sems, local_sem)
        start()
        wait()

    vmem = pl.BlockSpec(memory_space=pltpu.VMEM)
    return pl.pallas_call(
        body, name="small_all_gather",
        out_shape=jax.ShapeDtypeStruct((N_DEV, rows, n), small.dtype),
        in_specs=[vmem], out_specs=vmem,
        scratch_shapes=[pltpu.SemaphoreType.DMA((N_DEV - 1,)), pltpu.SemaphoreType.DMA((N_DEV - 1,)),
                        pltpu.SemaphoreType.DMA],
    )(small)


GATHER_SPLIT = 2
GATHER_STEPS = ([("own", 0, p) for p in range(GATHER_SPLIT)] + [("sibling", 0, p) for p in range(GATHER_SPLIT)]
                + [(kind, j, p) for p in range(GATHER_SPLIT) for kind in ("ici", "passed") for j in range(3)])


def _gather_order():
    x, y, c = _mesh_pos()
    chips = [(1 - x, y), (x, 1 - y), (1 - x, 1 - y)]
    owner = {"own": lambda j: (x, y, c), "sibling": lambda j: (x, y, 1 - c),
             "ici": lambda j: (*chips[j], c), "passed": lambda j: (*chips[j], 1 - c)}
    blocks = [4 * px + 2 * py + pc for px, py, pc in (owner[kind](j) for kind, j, _ in GATHER_STEPS)]
    return (jnp.stack(blocks).astype(jnp.int32), jnp.array([p for _, _, p in GATHER_STEPS], jnp.int32))


def _in_proj_gather(x, gain, w_shard, cos, sin, order):
    S, D = x.shape
    W = w_shard.shape[1]
    wp = W // GATHER_SPLIT
    tm = min(1024, S)
    ni = S // tm
    n_steps = len(GATHER_STEPS)

    def body(blk_ref, piece_ref, x_ref, g_ref, w_ref, cos_ref, sin_ref, o_ref, wall_ref, ht_ref, h_scr, wbuf,
             send_sems, recv_sems, local_sem, load_sem):
        step, i = pl.program_id(0), pl.program_id(1)
        mx, my, c = _mesh_pos()
        me, sibling = (mx, my, c), (mx, my, 1 - c)
        chips = [(1 - mx, my), (mx, 1 - my), (1 - mx, 1 - my)]

        def piece_of(dev, p):
            px, py, pc = dev
            return wall_ref.at[4 * px + 2 * py + pc, :, pl.ds(p * wp, wp)]

        def copy(k, p, block, to, own=False):
            dst = piece_of(block, p)
            return pltpu.make_async_remote_copy(
                src_ref=w_ref.at[:, pl.ds(p * wp, wp)] if own else dst, dst_ref=dst,
                send_sem=send_sems.at[k, p], recv_sem=recv_sems.at[k, p], device_id=to, device_id_type=MESH)

        def load(src):
            cp = pltpu.make_async_copy(src, wbuf, load_sem)
            cp.start()
            cp.wait()

        pieces = range(GATHER_SPLIT)
        first = [cp for p in pieces for cp in
                 [copy(0, p, me, sibling, own=True)] + [copy(1 + j, p, me, (*chip, c), own=True)
                                                        for j, chip in enumerate(chips)]]
        passed = {(j, p): copy(4 + j, p, (*chip, c), sibling) for j, chip in enumerate(chips) for p in pieces}
        mine = pltpu.make_async_copy(w_ref, wall_ref.at[4 * mx + 2 * my + c], local_sem)

        @pl.when(i == 0)
        def _():
            for s, (kind, j, p) in enumerate(GATHER_STEPS):
                @pl.when(step == s)
                def _(s=s, kind=kind, j=j, p=p):
                    if s == 0:
                        for cp in first:
                            cp.start()
                        mine.start()
                    if kind == "own":
                        load(w_ref.at[:, pl.ds(p * wp, wp)])
                    elif kind == "sibling":
                        copy(0, p, sibling, me).wait_recv()
                        load(piece_of(sibling, p))
                    elif kind == "ici":
                        copy(1 + j, p, (*chips[j], c), me).wait_recv()
                        passed[j, p].start()
                        load(piece_of((*chips[j], c), p))
                    else:
                        copy(4 + j, p, (*chips[j], 1 - c), me).wait_recv()
                        load(piece_of((*chips[j], 1 - c), p))

        rows = pl.ds(pl.multiple_of(i * tm, tm), tm)

        @pl.when(step == 0)
        def _():
            xv = x_ref[...]
            r = lax.rsqrt(jnp.mean(xv * xv, axis=-1, keepdims=True) + EPS)
            hv = xv * r * g_ref[...]
            h_scr[rows, :] = hv.astype(BF16)
            ht_ref[...] = hv.T.astype(BF16)

        acc = _dot(h_scr[rows, :], wbuf[...])
        b = blk_ref[step]

        @pl.when(b >= 2)
        def _():
            o_ref[...] = acc

        @pl.when(b < 2)
        def _():
            scale = jnp.where(b == 1, HEAD_DIM ** -0.5, 1.0).astype(F32)
            cs, sn = cos_ref[...], sin_ref[...]
            for hh in range(wp // HEAD_DIM):
                cols = slice(hh * HEAD_DIM, (hh + 1) * HEAD_DIM)
                o_ref[:, cols] = _rot(acc[:, cols], cs, sn) * scale

        @pl.when((step == n_steps - 1) & (i == ni - 1))
        def _():
            for cp in first + list(passed.values()):
                cp.wait_send()
            mine.wait()

    hbm = pl.BlockSpec(memory_space=pltpu.HBM)
    rope = pl.BlockSpec((tm, HEAD_DIM), lambda s, i, blk, piece: (i, 0))
    first_pass = lambda s, i: jnp.where(s == 0, i, ni - 1)
    return pl.pallas_call(
        body, name="in_proj_gather",
        out_shape=(jax.ShapeDtypeStruct((N_DEV, S, W), F32), jax.ShapeDtypeStruct((N_DEV, D, W), BF16),
                   jax.ShapeDtypeStruct((D, S), BF16)),
        grid_spec=pltpu.PrefetchScalarGridSpec(
            num_scalar_prefetch=2, grid=(n_steps, ni),
            in_specs=[pl.BlockSpec((tm, D), lambda s, i, blk, piece: (first_pass(s, i), 0)),
                      pl.BlockSpec((1, D), lambda s, i, blk, piece: (0, 0)), hbm, rope, rope],
            out_specs=(pl.BlockSpec((None, tm, wp), lambda s, i, blk, piece: (blk[s], i, piece[s])), hbm,
                       pl.BlockSpec((D, tm), lambda s, i, blk, piece: (0, first_pass(s, i)))),
            scratch_shapes=[pltpu.VMEM((S, D), BF16), pltpu.VMEM((D, wp), BF16),
                            pltpu.SemaphoreType.DMA((7, GATHER_SPLIT)), pltpu.SemaphoreType.DMA((7, GATHER_SPLIT)),
                            pltpu.SemaphoreType.DMA, pltpu.SemaphoreType.DMA]),
        compiler_params=_params("arbitrary", "arbitrary"),
    )(*order, x, gain, w_shard, cos, sin)


def _head_spec(S, j):
    return pl.BlockSpec((None, S, HEAD_DIM), lambda h, *_: (j, 0, h))


def _bdot(a, b):
    return lax.dot_general(a, b, (((2,), (1,)), ((0,), (0,))), preferred_element_type=F32)


def _bdot_nt(a, b):
    return lax.dot_general(a, b, (((2,), (2,)), ((0,), (0,))), preferred_element_type=F32)


def _bdot_tn(a, b):
    return lax.dot_general(a, b, (((1,), (1,)), ((0,), (0,))), preferred_element_type=F32)


def _chunks(a):
    return a.reshape(a.shape[0] // CHUNK, CHUNK, a.shape[1])


def _ret_group(q, k, vb, states_b, dec, xi, ze):
    qb, kb = q.astype(BF16), k.astype(BF16)
    sb = (_bdot_nt(qb, kb) * dec).astype(BF16)
    qx = (q * xi).astype(BF16)
    out = _bdot(sb, vb) + _bdot(qx, states_b)
    return out, (qb, kb, sb, qx)


def _ret_states(kz, vb, gam, state, states_ref):
    kv = _bdot_tn(kz, vb)
    for u in range(RET_GROUP):
        states_ref[u] = state
        state = gam * state + kv[u]
    return state


def _table_specs():
    return [pl.BlockSpec((None, CHUNK, HEAD_DIM), lambda h, *_: (h, 0, 0))] * 4


def _ret_fwd(proj, tabs, gn_gain, gn_bias):
    _, S, W = proj.shape
    H, nc = W // HEAD_DIM, S // CHUNK
    assert nc % RET_GROUP == 0
    rows_per_group = RET_GROUP * CHUNK

    def body(q_ref, k_ref, v_ref, g_ref, dec_ref, xi_ref, ze_ref, gam_ref, gain_ref, bias_ref, o_ref, states_ref):
        dec, xi, ze, gam = dec_ref[...], xi_ref[...], ze_ref[...], gam_ref[...]
        gain, bias = gain_ref[...], bias_ref[...]

        def group(i, state):
            rows = pl.ds(pl.multiple_of(i * rows_per_group, rows_per_group), rows_per_group)
            q, k, vb = _chunks(q_ref[rows, :]), _chunks(k_ref[rows, :]), _chunks(v_ref[rows, :]).astype(BF16)
            state = _ret_states((k * ze).astype(BF16), vb, gam, state, states_ref)
            out, _ = _ret_group(q, k, vb, states_ref[...].astype(BF16), dec, xi, ze)
            mu = jnp.mean(out, axis=-1, keepdims=True)
            d = out - mu
            yn = d * lax.rsqrt(jnp.mean(d * d, axis=-1, keepdims=True) + EPS)
            g = _chunks(g_ref[rows, :])
            mix = g * _sigmoid(g) * (yn * gain + bias)
            o_ref[rows, :] = mix.reshape(rows_per_group, HEAD_DIM).astype(BF16)
            return state

        lax.fori_loop(0, nc // RET_GROUP, group, jnp.zeros((HEAD_DIM, HEAD_DIM), F32))

    vec = pl.BlockSpec((1, HEAD_DIM), lambda h: (0, h))
    return pl.pallas_call(
        body, name="ret_fwd", out_shape=jax.ShapeDtypeStruct((S, W), BF16), grid=(H,),
        in_specs=[_head_spec(S, 0), _head_spec(S, 1), _head_spec(S, 2), _head_spec(S, 3)] + _table_specs() + [vec, vec],
        out_specs=pl.BlockSpec((S, HEAD_DIM), lambda h: (0, h)),
        scratch_shapes=[pltpu.VMEM((RET_GROUP, HEAD_DIM, HEAD_DIM), F32)],
        compiler_params=_params("parallel"),
    )(proj, proj, proj, proj, *tabs, gn_gain, gn_bias)


def _sb_scores(qb, kk, masked, causal, upper):
    z = _dot_nt(qb, kk) * (HEAD_DIM ** -0.5)
    e = jnp.exp(-jnp.abs(z))
    l1p = jnp.log(1.0 + e)
    log_beta = jnp.minimum(z, 0.0) - l1p
    lk = jnp.minimum(-z, 0.0) - l1p
    if masked:
        lk = jnp.where(causal, lk, 0.0)
    hi = lk.astype(BF16)
    lo = (lk - hi.astype(F32)).astype(BF16)
    cs = _dot(hi, upper) + _dot(lo, upper)
    return log_beta, lk, cs


def _tri(B, kind):
    r = lax.broadcasted_iota(jnp.int32, (B, B), 0)
    c = lax.broadcasted_iota(jnp.int32, (B, B), 1)
    return {"gt": r > c, "lt": r < c}[kind]


def _ones_where(mask):
    return jnp.where(mask, 1.0, 0.0).astype(BF16)


def _exchange_with_all(src_ref, out_ref, send_sems, recv_sems, local_sem):
    x, y, c = _mesh_pos()
    peers = [(1 - x if k & 4 else x, 1 - y if k & 2 else y, 1 - c if k & 1 else c) for k in range(1, N_DEV)]

    def copy(k, owner, to):
        px, py, pc = owner
        return pltpu.make_async_remote_copy(
            src_ref=src_ref, dst_ref=out_ref.at[4 * px + 2 * py + pc], send_sem=send_sems.at[k],
            recv_sem=recv_sems.at[k], device_id=to, device_id_type=MESH)

    sends = [copy(k, (x, y, c), p) for k, p in enumerate(peers)]
    mine = pltpu.make_async_copy(src_ref, out_ref.at[4 * x + 2 * y + c], local_sem)

    def start():
        for cp in sends:
            cp.start()
        mine.start()

    def wait():
        for k, p in enumerate(peers):
            copy(k, p, p).wait_recv()
        for cp in sends:
            cp.wait_send()
        mine.wait()

    return start, wait


def _sb_fwd(proj, gain, wo_shard):
    _, S, W = proj.shape
    H = W // HEAD_DIM
    B = min(SB_BLOCK, S)
    nq = S // B
    assert nq <= HEAD_DIM and nq % SB_PER_STEP == 0
    ns = nq // SB_PER_STEP

    def body(q_ref, k_ref, v_ref, g_ref, gain_ref, wo_ref, mix_ref, raw_ref, car_ref, woall_ref, kb_ref, vb_ref,
             send_sems, recv_sems, local_sem):
        hd, si = pl.program_id(0), pl.program_id(1)
        start_gather, wait_gather = _exchange_with_all(wo_ref, woall_ref, send_sems, recv_sems, local_sem)
        pl.when((hd == 0) & (si == 0))(start_gather)

        @pl.when(si == 0)
        def _():
            kb_ref[...] = k_ref[...].astype(BF16)
            vb_ref[...] = v_ref[...].astype(BF16)

        causal = _tri(B, "gt")
        upper = _ones_where(causal)
        lane = lax.broadcasted_iota(jnp.int32, (B, HEAD_DIM), 1)

        def block(qb, kb, carry, acc, saved, masked):
            rows = pl.ds(pl.multiple_of(kb * B, B), B)
            log_beta, lk, cs = _sb_scores(qb, kb_ref[rows, :], masked, causal, upper)
            a = jnp.exp(log_beta + cs + carry)
            if masked:
                a = jnp.where(causal, a, 0.0)
            acc = acc + _dot(a.astype(BF16), vb_ref[rows, :])
            return carry + jnp.sum(lk, axis=1, keepdims=True), acc, jnp.where(lane == kb, carry, saved)

        init = (jnp.zeros((B, 1), F32), jnp.zeros((B, HEAD_DIM), F32), jnp.full((B, HEAD_DIM), NOT_VISITED, F32))

        def live(st):
            return (st[0] >= 0) & (jnp.max(st[1]) >= EXP_IS_ZERO_BELOW)

        def finish(u, acc, saved):
            rows = slice(u * B, (u + 1) * B)
            raw_ref[rows, :] = acc
            car_ref[rows, :] = saved
            yn = acc * lax.rsqrt(jnp.mean(acc * acc, axis=-1, keepdims=True) + EPS)
            g = g_ref[rows, :]
            mix_ref[rows, :] = (g * _sigmoid(g) * (yn * gain_ref[...])).astype(BF16)

        def whole(first_step):
            heads = []
            for u in range(SB_PER_STEP):
                qi = si * SB_PER_STEP + u
                qb = q_ref[u * B:(u + 1) * B, :].astype(BF16)
                state = block(qb, qi, *init, True)
                if not (first_step and u == 0):
                    state = block(qb, qi - 1, *state, False)
                heads.append((qi, qb, state))
            for u, (qi, qb, state) in enumerate(heads):
                if not (first_step and u == 0):
                    state = lax.while_loop(
                        live, lambda st, qb=qb: (st[0] - 1,) + block(qb, st[0], st[1], st[2], st[3], False),
                        (qi - 2,) + state)[1:]
                finish(u, state[1], state[2])

        pl.when(si == 0)(lambda: whole(True))
        pl.when(si > 0)(lambda: whole(False))
        pl.when((hd == H - 1) & (si == ns - 1))(wait_gather)

    tq = SB_PER_STEP * B
    tile = lambda j: pl.BlockSpec((None, tq, HEAD_DIM), lambda h, i: (j, i, h))
    out_tile = pl.BlockSpec((tq, HEAD_DIM), lambda h, i: (i, h))
    hbm = pl.BlockSpec(memory_space=pltpu.HBM)
    return pl.pallas_call(
        body, name="sb_fwd",
        out_shape=(jax.ShapeDtypeStruct((S, W), BF16), jax.ShapeDtypeStruct((S, W), F32),
                   jax.ShapeDtypeStruct((S, W), F32), jax.ShapeDtypeStruct((N_DEV,) + wo_shard.shape, BF16)),
        grid=(H, ns),
        in_specs=[tile(4), _head_spec(S, 5), _head_spec(S, 6), tile(7),
                  pl.BlockSpec((1, HEAD_DIM), lambda h, i: (0, h)), hbm],
        out_specs=(out_tile, out_tile, out_tile, hbm),
        scratch_shapes=[pltpu.VMEM((S, HEAD_DIM), BF16), pltpu.VMEM((S, HEAD_DIM), BF16),
                        pltpu.SemaphoreType.DMA((N_DEV - 1,)), pltpu.SemaphoreType.DMA((N_DEV - 1,)),
                        pltpu.SemaphoreType.DMA],
        compiler_params=_params("arbitrary", "arbitrary"),
    )(proj, proj, proj, proj, gain, wo_shard)


def _out_proj_loss(mix_r, mix_s, w_out, x, tgt, gf):
    S, W = mix_r.shape
    D = x.shape[1]
    tm = min(256, S)

    def body(mr_ref, ms_ref, wo_ref, x_ref, t_ref, gf_ref, dx2_ref, dx2b_ref, dmix_ref, loss_ref, gfn_ref):
        @pl.when(pl.program_id(0) == 0)
        def _():
            loss_ref[...] = jnp.zeros_like(loss_ref)
            gfn_ref[...] = jnp.zeros_like(gfn_ref)

        gfv = gf_ref[...]
        x2 = x_ref[...] + (_dot(mr_ref[...], wo_ref[:W, :]) + _dot(ms_ref[...], wo_ref[W:, :]))
        r2 = lax.rsqrt(jnp.mean(x2 * x2, axis=-1, keepdims=True) + EPS)
        n = x2 * r2
        err = n * gfv - t_ref[...]
        loss_ref[...] += 0.5 * jnp.sum(jnp.mean(err * err, axis=-1, keepdims=True))
        dy = err * (1.0 / D)
        gfn_ref[...] += jnp.sum(dy * n, axis=0, keepdims=True)
        dn = dy * gfv
        dx2 = r2 * (dn - n * jnp.mean(dn * n, axis=-1, keepdims=True))
        dx2_ref[...] = dx2
        b = dx2.astype(BF16)
        dx2b_ref[...] = b
        dmix_ref[:, :W] = _dot_nt(b, wo_ref[:W, :])
        dmix_ref[:, W:] = _dot_nt(b, wo_ref[W:, :])

    row = lambda width: pl.BlockSpec((tm, width), lambda i: (i, 0))
    return pl.pallas_call(
        body, name="out_proj_loss",
        out_shape=(jax.ShapeDtypeStruct((S, D), F32), jax.ShapeDtypeStruct((S, D), BF16),
                   jax.ShapeDtypeStruct((S, 2 * W), F32), jax.ShapeDtypeStruct((SUBLANES, LANES), F32),
                   jax.ShapeDtypeStruct((1, D), F32)),
        grid=(S // tm,),
        in_specs=[row(W), row(W), pl.BlockSpec((2 * W, D), lambda i: (0, 0)), row(D), row(D),
                  pl.BlockSpec((1, D), lambda i: (0, 0))],
        out_specs=(row(D), row(D), row(2 * W), pl.BlockSpec((SUBLANES, LANES), lambda i: (0, 0)),
                   pl.BlockSpec((1, D), lambda i: (0, 0))),
        compiler_params=_params("arbitrary"),
    )(mix_r, mix_s, w_out, x, tgt, gf)


def _silu_bwd(g, dm, normed):
    sig = _sigmoid(g)
    return dm * (g * sig), dm * normed * (sig * (1.0 + g * (1.0 - sig)))


def _ret_bwd(proj, dmix, tabs, gn_gain, gn_bias, cos, sin, gwo):
    _, S, W = proj.shape
    H, nc = W // HEAD_DIM, S // CHUNK
    assert nc % RET_GROUP == 0
    ng = nc // RET_GROUP
    rows_per_group = RET_GROUP * CHUNK

    def body(q_ref, k_ref, v_ref, g_ref, dm_ref, dec_ref, xi_ref, ze_ref, gam_ref, gain_ref, bias_ref, cos_ref,
             sin_ref, gwo_ref, dp_ref, dgain_ref, dbias_ref, rino_ref, rs_ref, dstates_ref, send_sems, recv_sems):
        dec, xi, ze, gam = dec_ref[...], xi_ref[...], ze_ref[...], gam_ref[...]
        gain, bias = gain_ref[...], bias_ref[...]
        hd = pl.program_id(0)
        other_core = 1 - lax.axis_index("c")
        copies = _to_sibling_copies([gwo_ref.at[2 * k + other_core] for k in range(4)], rino_ref, send_sems, recv_sems)

        @pl.when(hd == 0)
        def _():
            for cp in copies:
                cp.start()

        def group_rows(i):
            return pl.ds(pl.multiple_of(i * rows_per_group, rows_per_group), rows_per_group)

        def fwd_group(i, state):
            rows = group_rows(i)
            kz = (_chunks(k_ref[rows, :]) * ze).astype(BF16)
            return _ret_states(kz, _chunks(v_ref[rows, :]).astype(BF16), gam, state,
                               rs_ref.at[pl.ds(i * RET_GROUP, RET_GROUP)])

        lax.fori_loop(0, ng, fwd_group, jnp.zeros((HEAD_DIM, HEAD_DIM), F32))

        flat = lambda a: a.reshape(rows_per_group, HEAD_DIM)

        def bwd_group(t, carry):
            dgain, dbias, dstate = carry
            i = ng - 1 - t
            rows = group_rows(i)
            q, k, g = _chunks(q_ref[rows, :]), _chunks(k_ref[rows, :]), _chunks(g_ref[rows, :])
            vb = _chunks(v_ref[rows, :]).astype(BF16)
            rb = rs_ref[pl.ds(i * RET_GROUP, RET_GROUP)].astype(BF16)
            out, (qb, kb, sb, qx) = _ret_group(q, k, vb, rb, dec, xi, ze)
            kz = (k * ze).astype(BF16)
            mu = jnp.mean(out, axis=-1, keepdims=True)
            d = out - mu
            rstd = lax.rsqrt(jnp.mean(d * d, axis=-1, keepdims=True) + EPS)
            yn = d * rstd
            dgn, dg = _silu_bwd(g, _chunks(dm_ref[rows, :]), yn * gain + bias)
            dgain = dgain + jnp.sum(flat(dgn * yn), axis=0, keepdims=True)
            dbias = dbias + jnp.sum(flat(dgn), axis=0, keepdims=True)
            dyn = dgn * gain
            do = rstd * (dyn - jnp.mean(dyn, axis=-1, keepdims=True)
                         - yn * jnp.mean(dyn * yn, axis=-1, keepdims=True))
            dob = do.astype(BF16)
            dkv = _bdot_tn(qx, dob)
            for u in reversed(range(RET_GROUP)):
                dstates_ref[u] = dstate
                dstate = gam * dstate + dkv[u]
            drb = dstates_ref[...].astype(BF16)
            dv = _bdot_tn(sb, dob) + _bdot(kz, drb)
            dsb = (_bdot_nt(dob, vb) * dec).astype(BF16)
            dq = _bdot(dsb, kb) + _bdot_nt(dob, rb) * xi
            dk = _bdot_tn(dsb, qb) + _bdot_nt(vb, drb) * ze
            cs, sn = cos_ref[rows, :], -sin_ref[rows, :]
            dp_ref[0, rows, :] = _rot(flat(dq), cs, sn).astype(BF16)
            dp_ref[1, rows, :] = (_rot(flat(dk), cs, sn) * (HEAD_DIM ** -0.5)).astype(BF16)
            dp_ref[2, rows, :] = flat(dv).astype(BF16)
            dp_ref[3, rows, :] = flat(dg).astype(BF16)
            return dgain, dbias, dstate

        zero = jnp.zeros((1, HEAD_DIM), F32)
        dgain, dbias, _ = lax.fori_loop(0, ng, bwd_group, (zero, zero, jnp.zeros((HEAD_DIM, HEAD_DIM), F32)))
        dgain_ref[...] = dgain
        dbias_ref[...] = dbias

        @pl.when(hd == H - 1)
        def _():
            for cp in copies:
                cp.wait_recv()
            for cp in copies:
                cp.wait_send()

    vec = pl.BlockSpec((1, HEAD_DIM), lambda h: (0, h))
    full = pl.BlockSpec((S, HEAD_DIM), lambda h: (0, 0))
    hbm = pl.BlockSpec(memory_space=pltpu.HBM)
    return pl.pallas_call(
        body, name="ret_bwd",
        out_shape=(jax.ShapeDtypeStruct((4, S, W), BF16), jax.ShapeDtypeStruct((1, W), F32),
                   jax.ShapeDtypeStruct((1, W), F32), jax.ShapeDtypeStruct((4,) + gwo.shape[1:], gwo.dtype)),
        grid=(H,),
        in_specs=[_head_spec(S, 0), _head_spec(S, 1), _head_spec(S, 2), _head_spec(S, 3),
                  pl.BlockSpec((S, HEAD_DIM), lambda h: (0, h))] + _table_specs() + [vec, vec, full, full, hbm],
        out_specs=(pl.BlockSpec((4, S, HEAD_DIM), lambda h: (0, 0, h)), vec, vec, hbm),
        scratch_shapes=[pltpu.VMEM((nc, HEAD_DIM, HEAD_DIM), F32), pltpu.VMEM((RET_GROUP, HEAD_DIM, HEAD_DIM), F32),
                        pltpu.SemaphoreType.DMA((4,)), pltpu.SemaphoreType.DMA((4,))],
        compiler_params=_params("arbitrary"),
    )(proj, proj, proj, proj, dmix, *tabs, gn_gain, gn_bias, cos, sin, gwo)


def _sb_bwd(proj, raw, carries, dmix, gain, chip_sums_o):
    _, S, W = proj.shape
    H = W // HEAD_DIM
    B = min(SB_BLOCK, S)
    nq = S // B
    ns = nq // SB_PER_STEP

    def body(q_ref, k_ref, v_ref, g_ref, raw_ref, car_ref, dm_ref, gain_ref, so_ref, dp_ref, dgain_ref, ro_ref,
             kb_ref, vb_ref, dk_ref, dv_ref, send_sems, recv_sems):
        hd, si = pl.program_id(0), pl.program_id(1)
        start_exchange, wait_exchange = _exchange_chip_sums((so_ref,), (ro_ref,), send_sems, recv_sems)
        pl.when((hd == 0) & (si == 0))(start_exchange)

        @pl.when(si == 0)
        def _():
            kb_ref[...] = k_ref[...].astype(BF16)
            vb_ref[...] = v_ref[...].astype(BF16)
            dk_ref[...] = jnp.zeros_like(dk_ref)
            dv_ref[...] = jnp.zeros_like(dv_ref)
            dgain_ref[...] = jnp.zeros_like(dgain_ref)

        causal = _tri(B, "gt")
        upper = _ones_where(causal)
        before = _ones_where(_tri(B, "lt"))
        lane = lax.broadcasted_iota(jnp.int32, (B, HEAD_DIM), 1)
        gain_v = gain_ref[...]

        def prologue(u):
            qi = si * SB_PER_STEP + u
            rows = slice(u * B, (u + 1) * B)
            o = raw_ref[rows, :]
            rstd = lax.rsqrt(jnp.mean(o * o, axis=-1, keepdims=True) + EPS)
            yn = o * rstd
            dnrm, dg = _silu_bwd(g_ref[rows, :], dm_ref[rows, :], yn * gain_v)
            dp_ref[3, pl.ds(pl.multiple_of(qi * B, B), B), :] = dg.astype(BF16)
            dgain_ref[...] += jnp.sum(dnrm * yn, axis=0, keepdims=True)
            dyn = dnrm * gain_v
            do = rstd * (dyn - yn * jnp.mean(dyn * yn, axis=-1, keepdims=True))
            return qi, q_ref[rows, :].astype(BF16), do.astype(BF16), car_ref[rows, :]

        def block(ctx, kb, carry_g, dq, masked):
            _, qb, dob, saved = ctx
            rows = pl.ds(pl.multiple_of(kb * B, B), B)
            kk, vv = kb_ref[rows, :], vb_ref[rows, :]
            log_beta, _, cs = _sb_scores(qb, kk, masked, causal, upper)
            carry_lk = jnp.sum(jnp.where(lane == kb, saved, 0.0), axis=1, keepdims=True)
            a = jnp.exp(log_beta + cs + carry_lk)
            if masked:
                a = jnp.where(causal, a, 0.0)
            gmat = _dot_nt(dob, vv) * a
            dv_ref[rows, :] += _dot_tn(a.astype(BF16), dob)
            hi = gmat.astype(BF16)
            lo = (gmat - hi.astype(F32)).astype(BF16)
            dlk = carry_g + (_dot(hi, before) + _dot(lo, before))
            beta = jnp.exp(log_beta)
            dz = (gmat * (1.0 - beta) - dlk * beta) * (HEAD_DIM ** -0.5)
            if masked:
                dz = jnp.where(causal, dz, 0.0)
            dzb = dz.astype(BF16)
            dk_ref[rows, :] += _dot_tn(dzb, qb)
            return carry_g + jnp.sum(gmat, axis=1, keepdims=True), dq + _dot(dzb, kk)

        init = (jnp.zeros((B, 1), F32), jnp.zeros((B, HEAD_DIM), F32))

        def whole(first_step):
            ctxs = [prologue(u) for u in range(SB_PER_STEP)]
            states = []
            for u, ctx in enumerate(ctxs):
                state = init
                if not (first_step and u == 0):
                    visited = jnp.max(ctx[3], axis=0, keepdims=True) >= EXP_IS_ZERO_BELOW
                    first = jnp.min(jnp.where(visited, lane[:1, :], ctx[0]))
                    state = lax.fori_loop(first, ctx[0] - 1,
                                          lambda i, st, ctx=ctx: block(ctx, i, st[0], st[1], False), state)
                states.append(state)
            for u, (ctx, state) in enumerate(zip(ctxs, states)):
                if not (first_step and u == 0):
                    state = block(ctx, ctx[0] - 1, *state, False)
                state = block(ctx, ctx[0], *state, True)
                dp_ref[0, pl.ds(pl.multiple_of(ctx[0] * B, B), B), :] = state[1].astype(BF16)

        pl.when(si == 0)(lambda: whole(True))
        pl.when(si > 0)(lambda: whole(False))

        @pl.when(si == ns - 1)
        def _():
            dp_ref[1] = dk_ref[...].astype(BF16)
            dp_ref[2] = dv_ref[...].astype(BF16)

        pl.when((hd == H - 1) & (si == ns - 1))(wait_exchange)

    tq = SB_PER_STEP * B
    tile = lambda j: pl.BlockSpec((None, tq, HEAD_DIM), lambda h, i: (j, i, h))
    vec = pl.BlockSpec((1, HEAD_DIM), lambda h, i: (0, h))
    hbm = pl.BlockSpec(memory_space=pltpu.HBM)
    return pl.pallas_call(
        body, name="sb_bwd",
        out_shape=(jax.ShapeDtypeStruct((4, S, W), BF16), jax.ShapeDtypeStruct((1, W), F32),
                   jax.ShapeDtypeStruct((3,) + chip_sums_o.shape[1:], chip_sums_o.dtype)),
        grid=(H, ns),
        in_specs=[tile(4), _head_spec(S, 5), _head_spec(S, 6), tile(7),
                  pl.BlockSpec((tq, HEAD_DIM), lambda h, i: (i, h)),
                  pl.BlockSpec((tq, HEAD_DIM), lambda h, i: (i, h)),
                  pl.BlockSpec((tq, HEAD_DIM), lambda h, i: (i, H + h)), vec, hbm],
        out_specs=(pl.BlockSpec((4, S, HEAD_DIM), lambda h, i: (0, 0, h)), vec, hbm),
        scratch_shapes=[pltpu.VMEM((S, HEAD_DIM), BF16), pltpu.VMEM((S, HEAD_DIM), BF16),
                        pltpu.VMEM((S, HEAD_DIM), F32), pltpu.VMEM((S, HEAD_DIM), F32),
                        pltpu.SemaphoreType.DMA((1, 3)), pltpu.SemaphoreType.DMA((1, 3))],
        compiler_params=_params("arbitrary", "arbitrary"),
    )(proj, proj, proj, proj, raw, carries, dmix, gain, chip_sums_o)


def _grad_w_in_half(ht, dpr, dps, core, name, to_sibling=None):
    D, S = ht.shape
    _, _, W = dpr.shape
    tmm = min(512, D)
    nm = D // tmm

    def body(core_ref, ht_ref, r_ref, s_ref, *rest):
        o_ref = rest[1] if to_sibling is not None else rest[0]
        q, m = pl.program_id(0), pl.program_id(1)
        if to_sibling is not None:
            ga_ref, _, rin_ref, send_sems, recv_sems = rest
            copies = _to_sibling_copies([ga_ref.at[k] for k in range(4)], rin_ref, send_sems, recv_sems)

            @pl.when((q == 0) & (m == 0))
            def _():
                for cp in copies:
                    cp.start()

        @pl.when(q < 2)
        def _():
            o_ref[...] = _dot(ht_ref[...], r_ref[...])

        @pl.when(q >= 2)
        def _():
            o_ref[...] = _dot(ht_ref[...], s_ref[...])

        if to_sibling is not None:
            @pl.when((q == 3) & (m == nm - 1))
            def _():
                for cp in copies:
                    cp.wait_recv()
                for cp in copies:
                    cp.wait_send()

    hbm = pl.BlockSpec(memory_space=pltpu.HBM)
    gw_shape = jax.ShapeDtypeStruct((4, D, W), F32)
    out_shape, out_specs, extra_in, scratch = (gw_shape,), (pl.BlockSpec((None, tmm, W), lambda q, m, core: (q, m, 0)),), [], []
    if to_sibling is not None:
        out_shape += (gw_shape,)
        out_specs += (hbm,)
        extra_in = [hbm]
        scratch = [pltpu.SemaphoreType.DMA((4,)), pltpu.SemaphoreType.DMA((4,))]
    return pl.pallas_call(
        body, name=name, out_shape=out_shape,
        grid_spec=pltpu.PrefetchScalarGridSpec(
            num_scalar_prefetch=1, grid=(4, nm),
            in_specs=[pl.BlockSpec((tmm, S), lambda q, m, core: (m, 0)),
                      pl.BlockSpec((None, S, W), lambda q, m, core: (jnp.minimum(2 * q + core[0], 3), 0, 0)),
                      pl.BlockSpec((None, S, W), lambda q, m, core: (jnp.maximum(2 * q + core[0] - 4, 0), 0, 0))]
            + extra_in,
            out_specs=out_specs, scratch_shapes=scratch),
        compiler_params=_params("arbitrary", "arbitrary"),
    )(core, ht, dpr, dps, *(() if to_sibling is None else (to_sibling,)))


def _grad_w_out(mix_r, mix_s, dx2b):
    S, W = mix_r.shape
    D = dx2b.shape[1]
    tmm = min(512, W)
    tk = min(1024, S)

    def body(r_ref, s_ref, b_ref, o_ref):
        j, kk = pl.program_id(0), pl.program_id(2)

        def acc(a_ref):
            part = _dot_tn(a_ref[...], b_ref[...])

            @pl.when(kk == 0)
            def _():
                o_ref[...] = part

            @pl.when(kk > 0)
            def _():
                o_ref[...] += part

        pl.when(j == 0)(lambda: acc(r_ref))
        pl.when(j == 1)(lambda: acc(s_ref))

    return pl.pallas_call(
        body, name="grad_w_out", out_shape=jax.ShapeDtypeStruct((2, W, D), F32), grid=(2, W // tmm, S // tk),
        in_specs=[pl.BlockSpec((tk, tmm), lambda j, m, k: (k, m)),
                  pl.BlockSpec((tk, tmm), lambda j, m, k: (k, m)),
                  pl.BlockSpec((tk, D), lambda j, m, k: (k, 0))],
        out_specs=pl.BlockSpec((None, tmm, D), lambda j, m, k: (j, m, 0)),
        compiler_params=_params("parallel", "parallel", "arbitrary"),
    )(mix_r, mix_s, dx2b)


def _dh_matmul(dpr, dps, w_all, chip_sums):
    _, S, W = dpr.shape
    D = w_all.shape[1]
    tm = min(1024, S)
    ni = S // tm

    def body(r_ref, s_ref, w_ref, sa_ref, dh_ref, ra_ref, send_sems, recv_sems):
        i, j = pl.program_id(0), pl.program_id(1)
        start_exchange, wait_exchange = _exchange_chip_sums((sa_ref,), (ra_ref,), send_sems, recv_sems)
        pl.when((i == 0) & (j == 0))(start_exchange)

        def acc(b_ref):
            part = _dot_nt(b_ref[...], w_ref[...])

            @pl.when(j == 0)
            def _():
                dh_ref[...] = part

            @pl.when(j > 0)
            def _():
                dh_ref[...] += part

        pl.when(j < 4)(lambda: acc(r_ref))
        pl.when(j >= 4)(lambda: acc(s_ref))
        pl.when((i == ni - 1) & (j == 7))(wait_exchange)

    hbm = pl.BlockSpec(memory_space=pltpu.HBM)
    return pl.pallas_call(
        body, name="dh_matmul",
        out_shape=(jax.ShapeDtypeStruct((S, D), F32), jax.ShapeDtypeStruct((3,) + chip_sums.shape[1:], chip_sums.dtype)),
        grid=(ni, 8),
        in_specs=[pl.BlockSpec((None, tm, W), lambda i, j: (jnp.minimum(j, 3), i, 0)),
                  pl.BlockSpec((None, tm, W), lambda i, j: (jnp.maximum(j - 4, 0), i, 0)),
                  pl.BlockSpec((None, D, W), lambda i, j: (j, 0, 0)), hbm],
        out_specs=(pl.BlockSpec((tm, D), lambda i, j: (i, 0)), hbm),
        scratch_shapes=[pltpu.SemaphoreType.DMA((1, 3)), pltpu.SemaphoreType.DMA((1, 3))],
        compiler_params=_params("arbitrary", "arbitrary"),
    )(dpr, dps, w_all, chip_sums)


def _norm_bwd(x, dx2, dh, gain):
    S, D = x.shape
    tm = min(256, S)

    def body(x_ref, dx2_ref, dh_ref, g_ref, gx_ref, dgain_ref):
        @pl.when(pl.program_id(0) == 0)
        def _():
            dgain_ref[...] = jnp.zeros_like(dgain_ref)

        xv, dh_v = x_ref[...], dh_ref[...]
        r1 = lax.rsqrt(jnp.mean(xv * xv, axis=-1, keepdims=True) + EPS)
        n = xv * r1
        dgain_ref[...] += jnp.sum(dh_v * n, axis=0, keepdims=True)
        dn = dh_v * g_ref[...]
        gx_ref[...] = dx2_ref[...] + r1 * (dn - n * jnp.mean(dn * n, axis=-1, keepdims=True))

    row = pl.BlockSpec((tm, D), lambda i: (i, 0))
    one = pl.BlockSpec((1, D), lambda i: (0, 0))
    return pl.pallas_call(
        body, name="norm_bwd", out_shape=(jax.ShapeDtypeStruct((S, D), F32), jax.ShapeDtypeStruct((1, D), F32)),
        grid=(S // tm,), in_specs=[row, row, row, one], out_specs=(row, one),
        compiler_params=_params("arbitrary"),
    )(x, dx2, dh, gain)


def _own_block(gw, pos, q):
    return q if gw.shape[0] == 4 else 2 * q + pos[0]


def _rs_local_sum(gw, rin, pos):
    _, R, C = gw.shape
    tr = min(256, R)
    other = lambda k, pos: (pos[1] + 1 + k) % 4

    def body(pos_ref, a_ref, b_ref, o_ref):
        o_ref[...] = (a_ref[...] + b_ref[...]).astype(BF16)

    return pl.pallas_call(
        body, name="rs_local_sum", out_shape=jax.ShapeDtypeStruct((4, R, C), BF16),
        grid_spec=pltpu.PrefetchScalarGridSpec(
            num_scalar_prefetch=1, grid=(3, R // tr),
            in_specs=[pl.BlockSpec((None, tr, C), lambda k, i, pos: (_own_block(gw, pos, other(k, pos)), i, 0)),
                      pl.BlockSpec((None, tr, C), lambda k, i, pos: (other(k, pos), i, 0))],
            out_specs=pl.BlockSpec((None, tr, C), lambda k, i, pos: (other(k, pos), i, 0))),
        compiler_params=_params("parallel", "parallel"),
    )(pos, gw, rin)


def _adamw(w, g, m, v):
    m2 = ADAM_B1 * m + (1.0 - ADAM_B1) * g
    v2 = ADAM_B2 * v + (1.0 - ADAM_B2) * (g * g)
    m_hat = m2 / (1.0 - ADAM_B1 ** ADAM_STEP)
    v_hat = v2 / (1.0 - ADAM_B2 ** ADAM_STEP)
    delta = -ADAM_LR * (m_hat / (jnp.sqrt(v_hat) + ADAM_EPS) + ADAM_WD * w)
    return delta, m2, v2


def _adamw_shard(gw, rin, rb, w, m, v, pos):
    _, R, C = gw.shape
    tr = min(256, R)

    def body(pos_ref, a_ref, b_ref, rb_ref, w_ref, m_ref, v_ref, g_ref, d_ref, m2_ref, v2_ref):
        g = a_ref[...] + b_ref[...]
        for k in range(3):
            g = g + rb_ref[k].astype(F32)
        g_ref[...] = g
        d_ref[...], m2_ref[...], v2_ref[...] = _adamw(w_ref[...], g, m_ref[...], v_ref[...])

    plain = pl.BlockSpec((tr, C), lambda i, pos: (i, 0))
    shape = jax.ShapeDtypeStruct((R, C), F32)
    return pl.pallas_call(
        body, name="adamw_shard", out_shape=(shape,) * 4,
        grid_spec=pltpu.PrefetchScalarGridSpec(
            num_scalar_prefetch=1, grid=(R // tr,),
            in_specs=[pl.BlockSpec((None, tr, C), lambda i, pos: (_own_block(gw, pos, pos[1]), i, 0)),
                      pl.BlockSpec((None, tr, C), lambda i, pos: (pos[1], i, 0)),
                      pl.BlockSpec((3, tr, C), lambda i, pos: (0, i, 0)), plain, plain, plain],
            out_specs=(plain,) * 4),
        compiler_params=_params("parallel"),
    )(pos, gw, rin, rb, w, m, v)


def _adamw_small(parts, w, m, v):
    _, rows, n = parts.shape

    def body(p_ref, w_ref, m_ref, v_ref, g_ref, d_ref, m2_ref, v2_ref):
        g = p_ref[0]
        for d in range(1, N_DEV):
            g = g + p_ref[d]
        g_ref[...] = g
        d_ref[...], m2_ref[...], v2_ref[...] = _adamw(w_ref[...], g, m_ref[...], v_ref[...])

    shape = jax.ShapeDtypeStruct((rows, n), F32)
    return pl.pallas_call(body, name="adamw_small", out_shape=(shape,) * 4)(parts, w, m, v)


def _rope_tables(S):
    half = HEAD_DIM // 2
    inv = ROPE_THETA ** (-jnp.arange(half, dtype=F32) / half)
    ang = jnp.arange(S, dtype=F32)[:, None] * inv[None, :]
    cos, sin = jnp.cos(ang), jnp.sin(ang)
    return jnp.concatenate([cos, cos], axis=1), jnp.concatenate([-sin, sin], axis=1)


def _retention_tables(H):
    lg = jnp.log1p(-jnp.exp2(-5.0 - jnp.arange(H, dtype=F32)))
    n = jnp.arange(CHUNK, dtype=F32)
    rel = n[:, None] - n[None, :]
    decay = jnp.where(rel >= 0, jnp.exp(lg[:, None, None] * jnp.maximum(rel, 0.0)), 0.0)
    shape = (H, CHUNK, HEAD_DIM)
    xi = jnp.broadcast_to(jnp.exp(lg[:, None] * (n + 1.0))[:, :, None], shape)
    zeta = jnp.broadcast_to(jnp.exp(lg[:, None] * (CHUNK - 1.0 - n))[:, :, None], shape)
    gamma_c = jnp.broadcast_to(jnp.exp(lg * CHUNK)[:, None, None], shape)
    return decay, xi, zeta, gamma_c


def _pack_small(parts):
    flat = []
    for p in parts:
        p = p.reshape(-1)
        flat.append(jnp.pad(p, (0, -p.shape[0] % LANES)))
    flat = jnp.concatenate(flat)
    return jnp.pad(flat, (0, SMALL_N - flat.shape[0])).reshape(SUBLANES, SMALL_N // SUBLANES)


def _unpack_small(packed, shapes):
    flat = packed.reshape(-1)
    out, at = [], 0
    for shp in shapes:
        size = 1
        for s in shp:
            size *= s
        out.append(flat[at:at + size].reshape(shp))
        at += size + (-size % LANES)
    return out


def kernel(x, norm_gain, w_in, ret_gn_gain, ret_gn_bias, sb_norm_gain, w_out, final_norm_gain, loss_target, m_norm_gain, m_w_in, m_ret_gn_gain, m_ret_gn_bias, m_sb_norm_gain, m_w_out, m_final_norm_gain, v_norm_gain, v_w_in, v_ret_gn_gain, v_ret_gn_bias, v_sb_norm_gain, v_w_out, v_final_norm_gain):
    S, D = x.shape[1], x.shape[2]
    W = w_in.shape[2]
    wo_rows = w_out.shape[1]
    H = W // HEAD_DIM
    xs, tgt = x[0], loss_target[0]
    mx, my, mc = _mesh_pos()
    pos = jnp.stack([mc, 2 * mx + my]).astype(jnp.int32)

    cos, sin = _rope_tables(S)
    tabs = _retention_tables(H)

    proj, w_all, ht = _in_proj_gather(xs, norm_gain, w_in[0].astype(BF16), cos, sin, _gather_order())
    mix_r = _ret_fwd(proj, tabs, ret_gn_gain, ret_gn_bias)
    mix_s, raw_s, carries, wo_all = _sb_fwd(proj, sb_norm_gain, w_out[0].astype(BF16))
    wo_full = wo_all.reshape(N_DEV * wo_rows, D)
    dx2, dx2b, dmix, loss_p, d_gf = _out_proj_loss(mix_r, mix_s, wo_full, xs, tgt, final_norm_gain[None])

    gwo = _grad_w_out(mix_r, mix_s, dx2b).reshape(N_DEV, wo_rows, D)
    dpr, d_rgain, d_rbias, rino = _ret_bwd(proj, dmix, tabs, ret_gn_gain, ret_gn_bias, cos, sin, gwo)
    dps, d_sgain, rbo = _sb_bwd(proj, raw_s, carries, dmix, sb_norm_gain, _rs_local_sum(gwo, rino, pos))
    gw_sibling, = _grad_w_in_half(ht, dpr, dps, (1 - mc).reshape(1).astype(jnp.int32), "grad_w_in_sibling")
    gw, rin = _grad_w_in_half(ht, dpr, dps, mc.reshape(1).astype(jnp.int32), "grad_w_in_own", to_sibling=gw_sibling)
    dh, rb = _dh_matmul(dpr, dps, w_all, _rs_local_sum(gw, rin, pos))
    grad_x, d_gain = _norm_bwd(xs, dx2, dh, norm_gain)
    g_in, d_in, m_in, v_in = _adamw_shard(gw, rin, rb, w_in[0], m_w_in[0], v_w_in[0], pos)
    g_out, d_out, m_out, v_out = _adamw_shard(gwo, rino, rbo, w_out[0], m_w_out[0], v_w_out[0], pos)

    small_w = [norm_gain, ret_gn_gain, ret_gn_bias, sb_norm_gain, final_norm_gain]
    small_m = [m_norm_gain, m_ret_gn_gain, m_ret_gn_bias, m_sb_norm_gain, m_final_norm_gain]
    small_v = [v_norm_gain, v_ret_gn_gain, v_ret_gn_bias, v_sb_norm_gain, v_final_norm_gain]
    shapes = [()] + [w.shape for w in small_w]
    zero = jnp.zeros((), F32)
    parts = _small_all_gather(_pack_small([loss_p[0, 0], d_gain, d_rgain, d_rbias, d_sgain, d_gf]))
    packed = _adamw_small(parts, _pack_small([zero] + small_w), _pack_small([zero] + small_m),
                          _pack_small([zero] + small_v))
    g_s, d_s, m_s, v_s = (_unpack_small(p, shapes) for p in packed)

    grads = [g_s[1], g_in[None], g_s[2], g_s[3], g_s[4], g_out[None], g_s[5]]
    deltas = [d_s[1], d_in[None], d_s[2], d_s[3], d_s[4], d_out[None], d_s[5]]
    new_m = [m_s[1], m_in[None], m_s[2], m_s[3], m_s[4], m_out[None], m_s[5]]
    new_v = [v_s[1], v_in[None], v_s[2], v_s[3], v_s[4], v_out[None], v_s[5]]
    return (g_s[0], grad_x[None], *grads, *deltas, *new_m, *new_v)
```

```python
import functools

import jax
import jax.numpy as jnp
from jax import lax
from jax.experimental import pallas as pl
from jax.experimental.pallas import tpu as pltpu

F32 = jnp.float32
BF16 = jnp.bfloat16

HEAD_DIM = 128
CHUNK = 128
RET_GROUP = 16
ROPE_THETA = 10000.0
EPS = 1e-6
ADAM_LR = 0.001
ADAM_B1 = 0.9
ADAM_B2 = 0.999
ADAM_EPS = 1e-08
ADAM_WD = 0.01
ADAM_STEP = 10

N_DEV = 8
LANES = 128
SUBLANES = 8
VMEM_LIMIT = 56 * 1024 * 1024
SB_BLOCK = 256
SB_PER_STEP = 4
SMALL_N = 8192
EXP_IS_ZERO_BELOW = -104.0
NOT_VISITED = -1e30
MESH = pl.DeviceIdType.MESH

NT = (((1,), (1,)), ((), ()))
TN = (((0,), (0,)), ((), ()))


def _params(*sem):
    return pltpu.CompilerParams(dimension_semantics=sem if sem else None, vmem_limit_bytes=VMEM_LIMIT)


def _dot(a, b):
    return jnp.dot(a, b, preferred_element_type=F32)


def _dot_nt(a, b):
    return lax.dot_general(a, b, NT, preferred_element_type=F32)


def _dot_tn(a, b):
    return lax.dot_general(a, b, TN, preferred_element_type=F32)


def _sigmoid(g):
    return 1.0 / (1.0 + jnp.exp(-g))


def _rot(a, cos, sin_signed):
    return a * cos + pltpu.roll(a, HEAD_DIM // 2, 1) * sin_signed


def _mesh_pos():
    return lax.axis_index("x"), lax.axis_index("y"), lax.axis_index("c")


def _to_sibling_copies(blocks, out_ref, send_sems, recv_sems):
    x, y, c = _mesh_pos()
    return [pltpu.make_async_remote_copy(
        src_ref=block, dst_ref=out_ref.at[k], send_sem=send_sems.at[k], recv_sem=recv_sems.at[k],
        device_id=(x, y, 1 - c), device_id_type=MESH) for k, block in enumerate(blocks)]


def _exchange_chip_sums(srcs, outs, send_sems, recv_sems):
    x, y, c = _mesh_pos()
    copies = []
    for arr, (src, out) in enumerate(zip(srcs, outs)):
        for k in range(1, 4):
            px = 1 - x if k & 2 else x
            py = 1 - y if k & 1 else y
            copies.append(pltpu.make_async_remote_copy(
                src_ref=src.at[2 * px + py], dst_ref=out.at[k - 1],
                send_sem=send_sems.at[arr, k - 1], recv_sem=recv_sems.at[arr, k - 1],
                device_id=(px, py, c), device_id_type=MESH))

    def start():
        for cp in copies:
            cp.start()

    def wait():
        for cp in copies:
            cp.wait_recv()
        for cp in copies:
            cp.wait_send()

    return start, wait


def _small_all_gather(small):
    rows, n = small.shape

    def body(s_ref, o_ref, send_sems, recv_sems, local_sem):
        start, wait = _exchange_with_all(s_ref, o_ref, send_sems, recv_sems, local_sem)
        start()
        wait()

    vmem = pl.BlockSpec(memory_space=pltpu.VMEM)
    return pl.pallas_call(
        body, name="small_all_gather",
        out_shape=jax.ShapeDtypeStruct((N_DEV, rows, n), small.dtype),
        in_specs=[vmem], out_specs=vmem,
        scratch_shapes=[pltpu.SemaphoreType.DMA((N_DEV - 1,)), pltpu.SemaphoreType.DMA((N_DEV - 1,)),
                        pltpu.SemaphoreType.DMA],
    )(small)


GATHER_SPLIT = 4
GATHER_STEPS = ([("own", 0, p) for p in range(GATHER_SPLIT)] + [("sibling", 0, p) for p in range(GATHER_SPLIT)]
                + [(kind, j, p) for p in range(GATHER_SPLIT) for kind in ("ici", "passed") for j in range(3)])


def _gather_order():
    x, y, c = _mesh_pos()
    chips = [(1 - x, y), (x, 1 - y), (1 - x, 1 - y)]
    owner = {"own": lambda j: (x, y, c), "sibling": lambda j: (x, y, 1 - c),
             "ici": lambda j: (*chips[j], c), "passed": lambda j: (*chips[j], 1 - c)}
    blocks = [4 * px + 2 * py + pc for px, py, pc in (owner[kind](j) for kind, j, _ in GATHER_STEPS)]
    return (jnp.stack(blocks).astype(jnp.int32), jnp.array([p for _, _, p in GATHER_STEPS], jnp.int32))


def _in_proj_gather(x, gain, w_shard, cos, sin, order):
    S, D = x.shape
    W = w_shard.shape[1]
    wp = W // GATHER_SPLIT
    tm = min(1024, S)
    ni = S // tm
    n_steps = len(GATHER_STEPS)

    def body(blk_ref, piece_ref, x_ref, g_ref, w_ref, cos_ref, sin_ref, o_ref, wall_ref, ht_ref, h_scr, wbuf,
             send_sems, recv_sems, local_sem, load_sem):
        step, i = pl.program_id(0), pl.program_id(1)
        mx, my, c = _mesh_pos()
        me, sibling = (mx, my, c), (mx, my, 1 - c)
        chips = [(1 - mx, my), (mx, 1 - my), (1 - mx, 1 - my)]

        def piece_of(dev, p):
            px, py, pc = dev
            return wall_ref.at[4 * px + 2 * py + pc, :, pl.ds(p * wp, wp)]

        def copy(k, p, block, to, own=False):
            dst = piece_of(block, p)
            return pltpu.make_async_remote_copy(
                src_ref=w_ref.at[:, pl.ds(p * wp, wp)] if own else dst, dst_ref=dst,
                send_sem=send_sems.at[k, p], recv_sem=recv_sems.at[k, p], device_id=to, device_id_type=MESH)

        def load(src):
            cp = pltpu.make_async_copy(src, wbuf, load_sem)
            cp.start()
            cp.wait()

        pieces = range(GATHER_SPLIT)
        first = [cp for p in pieces for cp in
                 [copy(0, p, me, sibling, own=True)] + [copy(1 + j, p, me, (*chip, c), own=True)
                                                        for j, chip in enumerate(chips)]]
        passed = {(j, p): copy(4 + j, p, (*chip, c), sibling) for j, chip in enumerate(chips) for p in pieces}
        mine = pltpu.make_async_copy(w_ref, wall_ref.at[4 * mx + 2 * my + c], local_sem)

        @pl.when(i == 0)
        def _():
            for s, (kind, j, p) in enumerate(GATHER_STEPS):
                @pl.when(step == s)
                def _(s=s, kind=kind, j=j, p=p):
                    if s == 0:
                        for cp in first:
                            cp.start()
                        mine.start()
                    if kind == "own":
                        load(w_ref.at[:, pl.ds(p * wp, wp)])
                    elif kind == "sibling":
                        copy(0, p, sibling, me).wait_recv()
                        load(piece_of(sibling, p))
                    elif kind == "ici":
                        copy(1 + j, p, (*chips[j], c), me).wait_recv()
                        passed[j, p].start()
                        load(piece_of((*chips[j], c), p))
                    else:
                        copy(4 + j, p, (*chips[j], 1 - c), me).wait_recv()
                        load(piece_of((*chips[j], 1 - c), p))

        rows = pl.ds(pl.multiple_of(i * tm, tm), tm)

        @pl.when(step == 0)
        def _():
            xv = x_ref[...]
            r = lax.rsqrt(jnp.mean(xv * xv, axis=-1, keepdims=True) + EPS)
            hv = xv * r * g_ref[...]
            h_scr[rows, :] = hv.astype(BF16)
            ht_ref[...] = hv.T.astype(BF16)

        acc = _dot(h_scr[rows, :], wbuf[...])
        b = blk_ref[step]

        @pl.when(b >= 2)
        def _():
            o_ref[...] = acc

        @pl.when(b < 2)
        def _():
            scale = jnp.where(b == 1, HEAD_DIM ** -0.5, 1.0).astype(F32)
            cs, sn = cos_ref[...], sin_ref[...]
            for hh in range(wp // HEAD_DIM):
                cols = slice(hh * HEAD_DIM, (hh + 1) * HEAD_DIM)
                o_ref[:, cols] = _rot(acc[:, cols], cs, sn) * scale

        @pl.when((step == n_steps - 1) & (i == ni - 1))
        def _():
            for cp in first + list(passed.values()):
                cp.wait_send()
            mine.wait()

    hbm = pl.BlockSpec(memory_space=pltpu.HBM)
    rope = pl.BlockSpec((tm, HEAD_DIM), lambda s, i, blk, piece: (i, 0))
    first_pass = lambda s, i: jnp.where(s == 0, i, ni - 1)
    return pl.pallas_call(
        body, name="in_proj_gather",
        out_shape=(jax.ShapeDtypeStruct((N_DEV, S, W), F32), jax.ShapeDtypeStruct((N_DEV, D, W), BF16),
                   jax.ShapeDtypeStruct((D, S), BF16)),
        grid_spec=pltpu.PrefetchScalarGridSpec(
            num_scalar_prefetch=2, grid=(n_steps, ni),
            in_specs=[pl.BlockSpec((tm, D), lambda s, i, blk, piece: (first_pass(s, i), 0)),
                      pl.BlockSpec((1, D), lambda s, i, blk, piece: (0, 0)), hbm, rope, rope],
            out_specs=(pl.BlockSpec((None, tm, wp), lambda s, i, blk, piece: (blk[s], i, piece[s])), hbm,
                       pl.BlockSpec((D, tm), lambda s, i, blk, piece: (0, first_pass(s, i)))),
            scratch_shapes=[pltpu.VMEM((S, D), BF16), pltpu.VMEM((D, wp), BF16),
                            pltpu.SemaphoreType.DMA((7, GATHER_SPLIT)), pltpu.SemaphoreType.DMA((7, GATHER_SPLIT)),
                            pltpu.SemaphoreType.DMA, pltpu.SemaphoreType.DMA]),
        compiler_params=_params("arbitrary", "arbitrary"),
    )(*order, x, gain, w_shard, cos, sin)


def _head_spec(S, j):
    return pl.BlockSpec((None, S, HEAD_DIM), lambda h, *_: (j, 0, h))


def _bdot(a, b):
    return lax.dot_general(a, b, (((2,), (1,)), ((0,), (0,))), preferred_element_type=F32)


def _bdot_nt(a, b):
    return lax.dot_general(a, b, (((2,), (2,)), ((0,), (0,))), preferred_element_type=F32)


def _bdot_tn(a, b):
    return lax.dot_general(a, b, (((1,), (1,)), ((0,), (0,))), preferred_element_type=F32)


def _chunks(a):
    return a.reshape(a.shape[0] // CHUNK, CHUNK, a.shape[1])


def _ret_group(q, k, vb, states_b, dec, xi, ze):
    qb, kb = q.astype(BF16), k.astype(BF16)
    sb = (_bdot_nt(qb, kb) * dec).astype(BF16)
    qx = (q * xi).astype(BF16)
    out = _bdot(sb, vb) + _bdot(qx, states_b)
    return out, (qb, kb, sb, qx)


def _ret_states(kz, vb, gam, state, states_ref):
    kv = _bdot_tn(kz, vb)
    for u in range(RET_GROUP):
        states_ref[u] = state
        state = gam * state + kv[u]
    return state


def _table_specs():
    return [pl.BlockSpec((None, CHUNK, HEAD_DIM), lambda h, *_: (h, 0, 0))] * 4


def _ret_fwd(proj, tabs, gn_gain, gn_bias):
    _, S, W = proj.shape
    H, nc = W // HEAD_DIM, S // CHUNK
    assert nc % RET_GROUP == 0
    rows_per_group = RET_GROUP * CHUNK

    def body(q_ref, k_ref, v_ref, g_ref, dec_ref, xi_ref, ze_ref, gam_ref, gain_ref, bias_ref, o_ref, states_ref):
        dec, xi, ze, gam = dec_ref[...], xi_ref[...], ze_ref[...], gam_ref[...]
        gain, bias = gain_ref[...], bias_ref[...]

        def group(i, state):
            rows = pl.ds(pl.multiple_of(i * rows_per_group, rows_per_group), rows_per_group)
            q, k, vb = _chunks(q_ref[rows, :]), _chunks(k_ref[rows, :]), _chunks(v_ref[rows, :]).astype(BF16)
            state = _ret_states((k * ze).astype(BF16), vb, gam, state, states_ref)
            out, _ = _ret_group(q, k, vb, states_ref[...].astype(BF16), dec, xi, ze)
            mu = jnp.mean(out, axis=-1, keepdims=True)
            d = out - mu
            yn = d * lax.rsqrt(jnp.mean(d * d, axis=-1, keepdims=True) + EPS)
            g = _chunks(g_ref[rows, :])
            mix = g * _sigmoid(g) * (yn * gain + bias)
            o_ref[rows, :] = mix.reshape(rows_per_group, HEAD_DIM).astype(BF16)
            return state

        lax.fori_loop(0, nc // RET_GROUP, group, jnp.zeros((HEAD_DIM, HEAD_DIM), F32))

    vec = pl.BlockSpec((1, HEAD_DIM), lambda h: (0, h))
    return pl.pallas_call(
        body, name="ret_fwd", out_shape=jax.ShapeDtypeStruct((S, W), BF16), grid=(H,),
        in_specs=[_head_spec(S, 0), _head_spec(S, 1), _head_spec(S, 2), _head_spec(S, 3)] + _table_specs() + [vec, vec],
        out_specs=pl.BlockSpec((S, HEAD_DIM), lambda h: (0, h)),
        scratch_shapes=[pltpu.VMEM((RET_GROUP, HEAD_DIM, HEAD_DIM), F32)],
        compiler_params=_params("parallel"),
    )(proj, proj, proj, proj, *tabs, gn_gain, gn_bias)


def _sb_scores(qb, kk, masked, causal, upper):
    z = _dot_nt(qb, kk) * (HEAD_DIM ** -0.5)
    e = jnp.exp(-jnp.abs(z))
    l1p = jnp.log(1.0 + e)
    log_beta = jnp.minimum(z, 0.0) - l1p
    lk = jnp.minimum(-z, 0.0) - l1p
    if masked:
        lk = jnp.where(causal, lk, 0.0)
    hi = lk.astype(BF16)
    lo = (lk - hi.astype(F32)).astype(BF16)
    cs = _dot(hi, upper) + _dot(lo, upper)
    return log_beta, lk, cs


def _tri(B, kind):
    r = lax.broadcasted_iota(jnp.int32, (B, B), 0)
    c = lax.broadcasted_iota(jnp.int32, (B, B), 1)
    return {"gt": r > c, "lt": r < c}[kind]


def _ones_where(mask):
    return jnp.where(mask, 1.0, 0.0).astype(BF16)


def _exchange_with_all(src_ref, out_ref, send_sems, recv_sems, local_sem):
    x, y, c = _mesh_pos()
    peers = [(1 - x if k & 4 else x, 1 - y if k & 2 else y, 1 - c if k & 1 else c) for k in range(1, N_DEV)]

    def copy(k, owner, to):
        px, py, pc = owner
        return pltpu.make_async_remote_copy(
            src_ref=src_ref, dst_ref=out_ref.at[4 * px + 2 * py + pc], send_sem=send_sems.at[k],
            recv_sem=recv_sems.at[k], device_id=to, device_id_type=MESH)

    sends = [copy(k, (x, y, c), p) for k, p in enumerate(peers)]
    mine = pltpu.make_async_copy(src_ref, out_ref.at[4 * x + 2 * y + c], local_sem)

    def start():
        for cp in sends:
            cp.start()
        mine.start()

    def wait():
        for k, p in enumerate(peers):
            copy(k, p, p).wait_recv()
        for cp in sends:
            cp.wait_send()
        mine.wait()

    return start, wait


def _sb_fwd(proj, gain, wo_shard):
    _, S, W = proj.shape
    H = W // HEAD_DIM
    B = min(SB_BLOCK, S)
    nq = S // B
    assert nq <= HEAD_DIM and nq % SB_PER_STEP == 0
    ns = nq // SB_PER_STEP

    def body(q_ref, k_ref, v_ref, g_ref, gain_ref, wo_ref, mix_ref, raw_ref, car_ref, woall_ref, kb_ref, vb_ref,
             send_sems, recv_sems, local_sem):
        hd, si = pl.program_id(0), pl.program_id(1)
        start_gather, wait_gather = _exchange_with_all(wo_ref, woall_ref, send_sems, recv_sems, local_sem)
        pl.when((hd == 0) & (si == 0))(start_gather)

        @pl.when(si == 0)
        def _():
            kb_ref[...] = k_ref[...].astype(BF16)
            vb_ref[...] = v_ref[...].astype(BF16)

        causal = _tri(B, "gt")
        upper = _ones_where(causal)
        lane = lax.broadcasted_iota(jnp.int32, (B, HEAD_DIM), 1)

        def block(qb, kb, carry, acc, saved, masked):
            rows = pl.ds(pl.multiple_of(kb * B, B), B)
            log_beta, lk, cs = _sb_scores(qb, kb_ref[rows, :], masked, causal, upper)
            a = jnp.exp(log_beta + cs + carry)
            if masked:
                a = jnp.where(causal, a, 0.0)
            acc = acc + _dot(a.astype(BF16), vb_ref[rows, :])
            return carry + jnp.sum(lk, axis=1, keepdims=True), acc, jnp.where(lane == kb, carry, saved)

        init = (jnp.zeros((B, 1), F32), jnp.zeros((B, HEAD_DIM), F32), jnp.full((B, HEAD_DIM), NOT_VISITED, F32))

        def live(st):
            return (st[0] >= 0) & (jnp.max(st[1]) >= EXP_IS_ZERO_BELOW)

        def finish(u, acc, saved):
            rows = slice(u * B, (u + 1) * B)
            raw_ref[rows, :] = acc
            car_ref[rows, :] = saved
            yn = acc * lax.rsqrt(jnp.mean(acc * acc, axis=-1, keepdims=True) + EPS)
            g = g_ref[rows, :]
            mix_ref[rows, :] = (g * _sigmoid(g) * (yn * gain_ref[...])).astype(BF16)

        def whole(first_step):
            heads = []
            for u in range(SB_PER_STEP):
                qi = si * SB_PER_STEP + u
                qb = q_ref[u * B:(u + 1) * B, :].astype(BF16)
                state = block(qb, qi, *init, True)
                if not (first_step and u == 0):
                    state = block(qb, qi - 1, *state, False)
                heads.append((qi, qb, state))
            for u, (qi, qb, state) in enumerate(heads):
                if not (first_step and u == 0):
                    state = lax.while_loop(
                        live, lambda st, qb=qb: (st[0] - 1,) + block(qb, st[0], st[1], st[2], st[3], False),
                        (qi - 2,) + state)[1:]
                finish(u, state[1], state[2])

        pl.when(si == 0)(lambda: whole(True))
        pl.when(si > 0)(lambda: whole(False))
        pl.when((hd == H - 1) & (si == ns - 1))(wait_gather)

    tq = SB_PER_STEP * B
    tile = lambda j: pl.BlockSpec((None, tq, HEAD_DIM), lambda h, i: (j, i, h))
    out_tile = pl.BlockSpec((tq, HEAD_DIM), lambda h, i: (i, h))
    hbm = pl.BlockSpec(memory_space=pltpu.HBM)
    return pl.pallas_call(
        body, name="sb_fwd",
        out_shape=(jax.ShapeDtypeStruct((S, W), BF16), jax.ShapeDtypeStruct((S, W), F32),
                   jax.ShapeDtypeStruct((S, W), F32), jax.ShapeDtypeStruct((N_DEV,) + wo_shard.shape, BF16)),
        grid=(H, ns),
        in_specs=[tile(4), _head_spec(S, 5), _head_spec(S, 6), tile(7),
                  pl.BlockSpec((1, HEAD_DIM), lambda h, i: (0, h)), hbm],
        out_specs=(out_tile, out_tile, out_tile, hbm),
        scratch_shapes=[pltpu.VMEM((S, HEAD_DIM), BF16), pltpu.VMEM((S, HEAD_DIM), BF16),
                        pltpu.SemaphoreType.DMA((N_DEV - 1,)), pltpu.SemaphoreType.DMA((N_DEV - 1,)),
                        pltpu.SemaphoreType.DMA],
        compiler_params=_params("arbitrary", "arbitrary"),
    )(proj, proj, proj, proj, gain, wo_shard)


def _out_proj_loss(mix_r, mix_s, w_out, x, tgt, gf):
    S, W = mix_r.shape
    D = x.shape[1]
    tm = min(256, S)

    def body(mr_ref, ms_ref, wo_ref, x_ref, t_ref, gf_ref, dx2_ref, dx2b_ref, dmix_ref, loss_ref, gfn_ref):
        @pl.when(pl.program_id(0) == 0)
        def _():
            loss_ref[...] = jnp.zeros_like(loss_ref)
            gfn_ref[...] = jnp.zeros_like(gfn_ref)

        gfv = gf_ref[...]
        x2 = x_ref[...] + (_dot(mr_ref[...], wo_ref[:W, :]) + _dot(ms_ref[...], wo_ref[W:, :]))
        r2 = lax.rsqrt(jnp.mean(x2 * x2, axis=-1, keepdims=True) + EPS)
        n = x2 * r2
        err = n * gfv - t_ref[...]
        loss_ref[...] += 0.5 * jnp.sum(jnp.mean(err * err, axis=-1, keepdims=True))
        dy = err * (1.0 / D)
        gfn_ref[...] += jnp.sum(dy * n, axis=0, keepdims=True)
        dn = dy * gfv
        dx2 = r2 * (dn - n * jnp.mean(dn * n, axis=-1, keepdims=True))
        dx2_ref[...] = dx2
        b = dx2.astype(BF16)
        dx2b_ref[...] = b
        dmix_ref[:, :W] = _dot_nt(b, wo_ref[:W, :])
        dmix_ref[:, W:] = _dot_nt(b, wo_ref[W:, :])

    row = lambda width: pl.BlockSpec((tm, width), lambda i: (i, 0))
    return pl.pallas_call(
        body, name="out_proj_loss",
        out_shape=(jax.ShapeDtypeStruct((S, D), F32), jax.ShapeDtypeStruct((S, D), BF16),
                   jax.ShapeDtypeStruct((S, 2 * W), F32), jax.ShapeDtypeStruct((SUBLANES, LANES), F32),
                   jax.ShapeDtypeStruct((1, D), F32)),
        grid=(S // tm,),
        in_specs=[row(W), row(W), pl.BlockSpec((2 * W, D), lambda i: (0, 0)), row(D), row(D),
                  pl.BlockSpec((1, D), lambda i: (0, 0))],
        out_specs=(row(D), row(D), row(2 * W), pl.BlockSpec((SUBLANES, LANES), lambda i: (0, 0)),
                   pl.BlockSpec((1, D), lambda i: (0, 0))),
        compiler_params=_params("arbitrary"),
    )(mix_r, mix_s, w_out, x, tgt, gf)


def _silu_bwd(g, dm, normed):
    sig = _sigmoid(g)
    return dm * (g * sig), dm * normed * (sig * (1.0 + g * (1.0 - sig)))


def _ret_bwd(proj, dmix, tabs, gn_gain, gn_bias, cos, sin, gwo):
    _, S, W = proj.shape
    H, nc = W // HEAD_DIM, S // CHUNK
    assert nc % RET_GROUP == 0
    ng = nc // RET_GROUP
    rows_per_group = RET_GROUP * CHUNK

    def body(q_ref, k_ref, v_ref, g_ref, dm_ref, dec_ref, xi_ref, ze_ref, gam_ref, gain_ref, bias_ref, cos_ref,
             sin_ref, gwo_ref, dp_ref, dgain_ref, dbias_ref, rino_ref, rs_ref, dstates_ref, send_sems, recv_sems):
        dec, xi, ze, gam = dec_ref[...], xi_ref[...], ze_ref[...], gam_ref[...]
        gain, bias = gain_ref[...], bias_ref[...]
        hd = pl.program_id(0)
        other_core = 1 - lax.axis_index("c")
        copies = _to_sibling_copies([gwo_ref.at[2 * k + other_core] for k in range(4)], rino_ref, send_sems, recv_sems)

        @pl.when(hd == 0)
        def _():
            for cp in copies:
                cp.start()

        def group_rows(i):
            return pl.ds(pl.multiple_of(i * rows_per_group, rows_per_group), rows_per_group)

        def fwd_group(i, state):
            rows = group_rows(i)
            kz = (_chunks(k_ref[rows, :]) * ze).astype(BF16)
            return _ret_states(kz, _chunks(v_ref[rows, :]).astype(BF16), gam, state,
                               rs_ref.at[pl.ds(i * RET_GROUP, RET_GROUP)])

        lax.fori_loop(0, ng, fwd_group, jnp.zeros((HEAD_DIM, HEAD_DIM), F32))

        flat = lambda a: a.reshape(rows_per_group, HEAD_DIM)

        def bwd_group(t, carry):
            dgain, dbias, dstate = carry
            i = ng - 1 - t
            rows = group_rows(i)
            q, k, g = _chunks(q_ref[rows, :]), _chunks(k_ref[rows, :]), _chunks(g_ref[rows, :])
            vb = _chunks(v_ref[rows, :]).astype(BF16)
            rb = rs_ref[pl.ds(i * RET_GROUP, RET_GROUP)].astype(BF16)
            out, (qb, kb, sb, qx) = _ret_group(q, k, vb, rb, dec, xi, ze)
            kz = (k * ze).astype(BF16)
            mu = jnp.mean(out, axis=-1, keepdims=True)
            d = out - mu
            rstd = lax.rsqrt(jnp.mean(d * d, axis=-1, keepdims=True) + EPS)
            yn = d * rstd
            dgn, dg = _silu_bwd(g, _chunks(dm_ref[rows, :]), yn * gain + bias)
            dgain = dgain + jnp.sum(flat(dgn * yn), axis=0, keepdims=True)
            dbias = dbias + jnp.sum(flat(dgn), axis=0, keepdims=True)
            dyn = dgn * gain
            do = rstd * (dyn - jnp.mean(dyn, axis=-1, keepdims=True)
                         - yn * jnp.mean(dyn * yn, axis=-1, keepdims=True))
            dob = do.astype(BF16)
            dkv = _bdot_tn(qx, dob)
            for u in reversed(range(RET_GROUP)):
                dstates_ref[u] = dstate
                dstate = gam * dstate + dkv[u]
            drb = dstates_ref[...].astype(BF16)
            dv = _bdot_tn(sb, dob) + _bdot(kz, drb)
            dsb = (_bdot_nt(dob, vb) * dec).astype(BF16)
            dq = _bdot(dsb, kb) + _bdot_nt(dob, rb) * xi
            dk = _bdot_tn(dsb, qb) + _bdot_nt(vb, drb) * ze
            cs, sn = cos_ref[rows, :], -sin_ref[rows, :]
            dp_ref[0, rows, :] = _rot(flat(dq), cs, sn).astype(BF16)
            dp_ref[1, rows, :] = (_rot(flat(dk), cs, sn) * (HEAD_DIM ** -0.5)).astype(BF16)
            dp_ref[2, rows, :] = flat(dv).astype(BF16)
            dp_ref[3, rows, :] = flat(dg).astype(BF16)
            return dgain, dbias, dstate

        zero = jnp.zeros((1, HEAD_DIM), F32)
        dgain, dbias, _ = lax.fori_loop(0, ng, bwd_group, (zero, zero, jnp.zeros((HEAD_DIM, HEAD_DIM), F32)))
        dgain_ref[...] = dgain
        dbias_ref[...] = dbias

        @pl.when(hd == H - 1)
        def _():
            for cp in copies:
                cp.wait_recv()
            for cp in copies:
                cp.wait_send()

    vec = pl.BlockSpec((1, HEAD_DIM), lambda h: (0, h))
    full = pl.BlockSpec((S, HEAD_DIM), lambda h: (0, 0))
    hbm = pl.BlockSpec(memory_space=pltpu.HBM)
    return pl.pallas_call(
        body, name="ret_bwd",
        out_shape=(jax.ShapeDtypeStruct((4, S, W), BF16), jax.ShapeDtypeStruct((1, W), F32),
                   jax.ShapeDtypeStruct((1, W), F32), jax.ShapeDtypeStruct((4,) + gwo.shape[1:], gwo.dtype)),
        grid=(H,),
        in_specs=[_head_spec(S, 0), _head_spec(S, 1), _head_spec(S, 2), _head_spec(S, 3),
                  pl.BlockSpec((S, HEAD_DIM), lambda h: (0, h))] + _table_specs() + [vec, vec, full, full, hbm],
        out_specs=(pl.BlockSpec((4, S, HEAD_DIM), lambda h: (0, 0, h)), vec, vec, hbm),
        scratch_shapes=[pltpu.VMEM((nc, HEAD_DIM, HEAD_DIM), F32), pltpu.VMEM((RET_GROUP, HEAD_DIM, HEAD_DIM), F32),
                        pltpu.SemaphoreType.DMA((4,)), pltpu.SemaphoreType.DMA((4,))],
        compiler_params=_params("arbitrary"),
    )(proj, proj, proj, proj, dmix, *tabs, gn_gain, gn_bias, cos, sin, gwo)


def _sb_bwd(proj, raw, carries, dmix, gain, chip_sums_o):
    _, S, W = proj.shape
    H = W // HEAD_DIM
    B = min(SB_BLOCK, S)
    nq = S // B
    ns = nq // SB_PER_STEP

    def body(q_ref, k_ref, v_ref, g_ref, raw_ref, car_ref, dm_ref, gain_ref, so_ref, dp_ref, dgain_ref, ro_ref,
             kb_ref, vb_ref, dk_ref, dv_ref, send_sems, recv_sems):
        hd, si = pl.program_id(0), pl.program_id(1)
        start_exchange, wait_exchange = _exchange_chip_sums((so_ref,), (ro_ref,), send_sems, recv_sems)
        pl.when((hd == 0) & (si == 0))(start_exchange)

        @pl.when(si == 0)
        def _():
            kb_ref[...] = k_ref[...].astype(BF16)
            vb_ref[...] = v_ref[...].astype(BF16)
            dk_ref[...] = jnp.zeros_like(dk_ref)
            dv_ref[...] = jnp.zeros_like(dv_ref)
            dgain_ref[...] = jnp.zeros_like(dgain_ref)

        causal = _tri(B, "gt")
        upper = _ones_where(causal)
        before = _ones_where(_tri(B, "lt"))
        lane = lax.broadcasted_iota(jnp.int32, (B, HEAD_DIM), 1)
        gain_v = gain_ref[...]

        def prologue(u):
            qi = si * SB_PER_STEP + u
            rows = slice(u * B, (u + 1) * B)
            o = raw_ref[rows, :]
            rstd = lax.rsqrt(jnp.mean(o * o, axis=-1, keepdims=True) + EPS)
            yn = o * rstd
            dnrm, dg = _silu_bwd(g_ref[rows, :], dm_ref[rows, :], yn * gain_v)
            dp_ref[3, pl.ds(pl.multiple_of(qi * B, B), B), :] = dg.astype(BF16)
            dgain_ref[...] += jnp.sum(dnrm * yn, axis=0, keepdims=True)
            dyn = dnrm * gain_v
            do = rstd * (dyn - yn * jnp.mean(dyn * yn, axis=-1, keepdims=True))
            return qi, q_ref[rows, :].astype(BF16), do.astype(BF16), car_ref[rows, :]

        def block(ctx, kb, carry_g, dq, masked):
            _, qb, dob, saved = ctx
            rows = pl.ds(pl.multiple_of(kb * B, B), B)
            kk, vv = kb_ref[rows, :], vb_ref[rows, :]
            log_beta, _, cs = _sb_scores(qb, kk, masked, causal, upper)
            carry_lk = jnp.sum(jnp.where(lane == kb, saved, 0.0), axis=1, keepdims=True)
            a = jnp.exp(log_beta + cs + carry_lk)
            if masked:
                a = jnp.where(causal, a, 0.0)
            gmat = _dot_nt(dob, vv) * a
            dv_ref[rows, :] += _dot_tn(a.astype(BF16), dob)
            hi = gmat.astype(BF16)
            lo = (gmat - hi.astype(F32)).astype(BF16)
            dlk = carry_g + (_dot(hi, before) + _dot(lo, before))
            beta = jnp.exp(log_beta)
            dz = (gmat * (1.0 - beta) - dlk * beta) * (HEAD_DIM ** -0.5)
            if masked:
                dz = jnp.where(causal, dz, 0.0)
            dzb = dz.astype(BF16)
            dk_ref[rows, :] += _dot_tn(dzb, qb)
            return carry_g + jnp.sum(gmat, axis=1, keepdims=True), dq + _dot(dzb, kk)

        init = (jnp.zeros((B, 1), F32), jnp.zeros((B, HEAD_DIM), F32))

        def whole(first_step):
            ctxs = [prologue(u) for u in range(SB_PER_STEP)]
            states = []
            for u, ctx in enumerate(ctxs):
                state = init
                if not (first_step and u == 0):
                    visited = jnp.max(ctx[3], axis=0, keepdims=True) >= EXP_IS_ZERO_BELOW
                    first = jnp.min(jnp.where(visited, lane[:1, :], ctx[0]))
                    state = lax.fori_loop(first, ctx[0] - 1,
                                          lambda i, st, ctx=ctx: block(ctx, i, st[0], st[1], False), state)
                states.append(state)
            for u, (ctx, state) in enumerate(zip(ctxs, states)):
                if not (first_step and u == 0):
                    state = block(ctx, ctx[0] - 1, *state, False)
                state = block(ctx, ctx[0], *state, True)
                dp_ref[0, pl.ds(pl.multiple_of(ctx[0] * B, B), B), :] = state[1].astype(BF16)

        pl.when(si == 0)(lambda: whole(True))
        pl.when(si > 0)(lambda: whole(False))

        @pl.when(si == ns - 1)
        def _():
            dp_ref[1] = dk_ref[...].astype(BF16)
            dp_ref[2] = dv_ref[...].astype(BF16)

        pl.when((hd == H - 1) & (si == ns - 1))(wait_exchange)

    tq = SB_PER_STEP * B
    tile = lambda j: pl.BlockSpec((None, tq, HEAD_DIM), lambda h, i: (j, i, h))
    vec = pl.BlockSpec((1, HEAD_DIM), lambda h, i: (0, h))
    hbm = pl.BlockSpec(memory_space=pltpu.HBM)
    return pl.pallas_call(
        body, name="sb_bwd",
        out_shape=(jax.ShapeDtypeStruct((4, S, W), BF16), jax.ShapeDtypeStruct((1, W), F32),
                   jax.ShapeDtypeStruct((3,) + chip_sums_o.shape[1:], chip_sums_o.dtype)),
        grid=(H, ns),
        in_specs=[tile(4), _head_spec(S, 5), _head_spec(S, 6), tile(7),
                  pl.BlockSpec((tq, HEAD_DIM), lambda h, i: (i, h)),
                  pl.BlockSpec((tq, HEAD_DIM), lambda h, i: (i, h)),
                  pl.BlockSpec((tq, HEAD_DIM), lambda h, i: (i, H + h)), vec, hbm],
        out_specs=(pl.BlockSpec((4, S, HEAD_DIM), lambda h, i: (0, 0, h)), vec, hbm),
        scratch_shapes=[pltpu.VMEM((S, HEAD_DIM), BF16), pltpu.VMEM((S, HEAD_DIM), BF16),
                        pltpu.VMEM((S, HEAD_DIM), F32), pltpu.VMEM((S, HEAD_DIM), F32),
                        pltpu.SemaphoreType.DMA((1, 3)), pltpu.SemaphoreType.DMA((1, 3))],
        compiler_params=_params("arbitrary", "arbitrary"),
    )(proj, proj, proj, proj, raw, carries, dmix, gain, chip_sums_o)


def _grad_w_in_half(ht, dpr, dps, core, name, to_sibling=None):
    D, S = ht.shape
    _, _, W = dpr.shape
    tmm = min(512, D)
    nm = D // tmm

    def body(core_ref, ht_ref, r_ref, s_ref, *rest):
        o_ref = rest[1] if to_sibling is not None else rest[0]
        q, m = pl.program_id(0), pl.program_id(1)
        if to_sibling is not None:
            ga_ref, _, rin_ref, send_sems, recv_sems = rest
            copies = _to_sibling_copies([ga_ref.at[k] for k in range(4)], rin_ref, send_sems, recv_sems)

            @pl.when((q == 0) & (m == 0))
            def _():
                for cp in copies:
                    cp.start()

        @pl.when(q < 2)
        def _():
            o_ref[...] = _dot(ht_ref[...], r_ref[...])

        @pl.when(q >= 2)
        def _():
            o_ref[...] = _dot(ht_ref[...], s_ref[...])

        if to_sibling is not None:
            @pl.when((q == 3) & (m == nm - 1))
            def _():
                for cp in copies:
                    cp.wait_recv()
                for cp in copies:
                    cp.wait_send()

    hbm = pl.BlockSpec(memory_space=pltpu.HBM)
    gw_shape = jax.ShapeDtypeStruct((4, D, W), F32)
    out_shape, out_specs, extra_in, scratch = (gw_shape,), (pl.BlockSpec((None, tmm, W), lambda q, m, core: (q, m, 0)),), [], []
    if to_sibling is not None:
        out_shape += (gw_shape,)
        out_specs += (hbm,)
        extra_in = [hbm]
        scratch = [pltpu.SemaphoreType.DMA((4,)), pltpu.SemaphoreType.DMA((4,))]
    return pl.pallas_call(
        body, name=name, out_shape=out_shape,
        grid_spec=pltpu.PrefetchScalarGridSpec(
            num_scalar_prefetch=1, grid=(4, nm),
            in_specs=[pl.BlockSpec((tmm, S), lambda q, m, core: (m, 0)),
                      pl.BlockSpec((None, S, W), lambda q, m, core: (jnp.minimum(2 * q + core[0], 3), 0, 0)),
                      pl.BlockSpec((None, S, W), lambda q, m, core: (jnp.maximum(2 * q + core[0] - 4, 0), 0, 0))]
            + extra_in,
            out_specs=out_specs, scratch_shapes=scratch),
        compiler_params=_params("arbitrary", "arbitrary"),
    )(core, ht, dpr, dps, *(() if to_sibling is None else (to_sibling,)))


def _grad_w_out(mix_r, mix_s, dx2b):
    S, W = mix_r.shape
    D = dx2b.shape[1]
    tmm = min(512, W)
    tk = min(1024, S)

    def body(r_ref, s_ref, b_ref, o_ref):
        j, kk = pl.program_id(0), pl.program_id(2)

        def acc(a_ref):
            part = _dot_tn(a_ref[...], b_ref[...])

            @pl.when(kk == 0)
            def _():
                o_ref[...] = part

            @pl.when(kk > 0)
            def _():
                o_ref[...] += part

        pl.when(j == 0)(lambda: acc(r_ref))
        pl.when(j == 1)(lambda: acc(s_ref))

    return pl.pallas_call(
        body, name="grad_w_out", out_shape=jax.ShapeDtypeStruct((2, W, D), F32), grid=(2, W // tmm, S // tk),
        in_specs=[pl.BlockSpec((tk, tmm), lambda j, m, k: (k, m)),
                  pl.BlockSpec((tk, tmm), lambda j, m, k: (k, m)),
                  pl.BlockSpec((tk, D), lambda j, m, k: (k, 0))],
        out_specs=pl.BlockSpec((None, tmm, D), lambda j, m, k: (j, m, 0)),
        compiler_params=_params("parallel", "parallel", "arbitrary"),
    )(mix_r, mix_s, dx2b)


def _dh_matmul(dpr, dps, w_all, chip_sums):
    _, S, W = dpr.shape
    D = w_all.shape[1]
    tm = min(1024, S)
    ni = S // tm

    def body(r_ref, s_ref, w_ref, sa_ref, dh_ref, ra_ref, send_sems, recv_sems):
        i, j = pl.program_id(0), pl.program_id(1)
        start_exchange, wait_exchange = _exchange_chip_sums((sa_ref,), (ra_ref,), send_sems, recv_sems)
        pl.when((i == 0) & (j == 0))(start_exchange)

        def acc(b_ref):
            part = _dot_nt(b_ref[...], w_ref[...])

            @pl.when(j == 0)
            def _():
                dh_ref[...] = part

            @pl.when(j > 0)
            def _():
                dh_ref[...] += part

        pl.when(j < 4)(lambda: acc(r_ref))
        pl.when(j >= 4)(lambda: acc(s_ref))
        pl.when((i == ni - 1) & (j == 7))(wait_exchange)

    hbm = pl.BlockSpec(memory_space=pltpu.HBM)
    return pl.pallas_call(
        body, name="dh_matmul",
        out_shape=(jax.ShapeDtypeStruct((S, D), F32), jax.ShapeDtypeStruct((3,) + chip_sums.shape[1:], chip_sums.dtype)),
        grid=(ni, 8),
        in_specs=[pl.BlockSpec((None, tm, W), lambda i, j: (jnp.minimum(j, 3), i, 0)),
                  pl.BlockSpec((None, tm, W), lambda i, j: (jnp.maximum(j - 4, 0), i, 0)),
                  pl.BlockSpec((None, D, W), lambda i, j: (j, 0, 0)), hbm],
        out_specs=(pl.BlockSpec((tm, D), lambda i, j: (i, 0)), hbm),
        scratch_shapes=[pltpu.SemaphoreType.DMA((1, 3)), pltpu.SemaphoreType.DMA((1, 3))],
        compiler_params=_params("arbitrary", "arbitrary"),
    )(dpr, dps, w_all, chip_sums)


def _norm_bwd(x, dx2, dh, gain):
    S, D = x.shape
    tm = min(256, S)

    def body(x_ref, dx2_ref, dh_ref, g_ref, gx_ref, dgain_ref):
        @pl.when(pl.program_id(0) == 0)
        def _():
            dgain_ref[...] = jnp.zeros_like(dgain_ref)

        xv, dh_v = x_ref[...], dh_ref[...]
        r1 = lax.rsqrt(jnp.mean(xv * xv, axis=-1, keepdims=True) + EPS)
        n = xv * r1
        dgain_ref[...] += jnp.sum(dh_v * n, axis=0, keepdims=True)
        dn = dh_v * g_ref[...]
        gx_ref[...] = dx2_ref[...] + r1 * (dn - n * jnp.mean(dn * n, axis=-1, keepdims=True))

    row = pl.BlockSpec((tm, D), lambda i: (i, 0))
    one = pl.BlockSpec((1, D), lambda i: (0, 0))
    return pl.pallas_call(
        body, name="norm_bwd", out_shape=(jax.ShapeDtypeStruct((S, D), F32), jax.ShapeDtypeStruct((1, D), F32)),
        grid=(S // tm,), in_specs=[row, row, row, one], out_specs=(row, one),
        compiler_params=_params("arbitrary"),
    )(x, dx2, dh, gain)


def _own_block(gw, pos, q):
    return q if gw.shape[0] == 4 else 2 * q + pos[0]


def _rs_local_sum(gw, rin, pos):
    _, R, C = gw.shape
    tr = min(256, R)
    other = lambda k, pos: (pos[1] + 1 + k) % 4

    def body(pos_ref, a_ref, b_ref, o_ref):
        o_ref[...] = (a_ref[...] + b_ref[...]).astype(BF16)

    return pl.pallas_call(
        body, name="rs_local_sum", out_shape=jax.ShapeDtypeStruct((4, R, C), BF16),
        grid_spec=pltpu.PrefetchScalarGridSpec(
            num_scalar_prefetch=1, grid=(3, R // tr),
            in_specs=[pl.BlockSpec((None, tr, C), lambda k, i, pos: (_own_block(gw, pos, other(k, pos)), i, 0)),
                      pl.BlockSpec((None, tr, C), lambda k, i, pos: (other(k, pos), i, 0))],
            out_specs=pl.BlockSpec((None, tr, C), lambda k, i, pos: (other(k, pos), i, 0))),
        compiler_params=_params("parallel", "parallel"),
    )(pos, gw, rin)


def _adamw(w, g, m, v):
    m2 = ADAM_B1 * m + (1.0 - ADAM_B1) * g
    v2 = ADAM_B2 * v + (1.0 - ADAM_B2) * (g * g)
    m_hat = m2 / (1.0 - ADAM_B1 ** ADAM_STEP)
    v_hat = v2 / (1.0 - ADAM_B2 ** ADAM_STEP)
    delta = -ADAM_LR * (m_hat / (jnp.sqrt(v_hat) + ADAM_EPS) + ADAM_WD * w)
    return delta, m2, v2


def _adamw_shard(gw, rin, rb, w, m, v, pos):
    _, R, C = gw.shape
    tr = min(256, R)

    def body(pos_ref, a_ref, b_ref, rb_ref, w_ref, m_ref, v_ref, g_ref, d_ref, m2_ref, v2_ref):
        g = a_ref[...] + b_ref[...]
        for k in range(3):
            g = g + rb_ref[k].astype(F32)
        g_ref[...] = g
        d_ref[...], m2_ref[...], v2_ref[...] = _adamw(w_ref[...], g, m_ref[...], v_ref[...])

    plain = pl.BlockSpec((tr, C), lambda i, pos: (i, 0))
    shape = jax.ShapeDtypeStruct((R, C), F32)
    return pl.pallas_call(
        body, name="adamw_shard", out_shape=(shape,) * 4,
        grid_spec=pltpu.PrefetchScalarGridSpec(
            num_scalar_prefetch=1, grid=(R // tr,),
            in_specs=[pl.BlockSpec((None, tr, C), lambda i, pos: (_own_block(gw, pos, pos[1]), i, 0)),
                      pl.BlockSpec((None, tr, C), lambda i, pos: (pos[1], i, 0)),
                      pl.BlockSpec((3, tr, C), lambda i, pos: (0, i, 0)), plain, plain, plain],
            out_specs=(plain,) * 4),
        compiler_params=_params("parallel"),
    )(pos, gw, rin, rb, w, m, v)


def _adamw_small(parts, w, m, v):
    _, rows, n = parts.shape

    def body(p_ref, w_ref, m_ref, v_ref, g_ref, d_ref, m2_ref, v2_ref):
        g = p_ref[0]
        for d in range(1, N_DEV):
            g = g + p_ref[d]
        g_ref[...] = g
        d_ref[...], m2_ref[...], v2_ref[...] = _adamw(w_ref[...], g, m_ref[...], v_ref[...])

    shape = jax.ShapeDtypeStruct((rows, n), F32)
    return pl.pallas_call(body, name="adamw_small", out_shape=(shape,) * 4)(parts, w, m, v)


def _rope_tables(S):
    half = HEAD_DIM // 2
    inv = ROPE_THETA ** (-jnp.arange(half, dtype=F32) / half)
    ang = jnp.arange(S, dtype=F32)[:, None] * inv[None, :]
    cos, sin = jnp.cos(ang), jnp.sin(ang)
    return jnp.concatenate([cos, cos], axis=1), jnp.concatenate([-sin, sin], axis=1)


def _retention_tables(H):
    lg = jnp.log1p(-jnp.exp2(-5.0 - jnp.arange(H, dtype=F32)))
    n = jnp.arange(CHUNK, dtype=F32)
    rel = n[:, None] - n[None, :]
    decay = jnp.where(rel >= 0, jnp.exp(lg[:, None, None] * jnp.maximum(rel, 0.0)), 0.0)
    shape = (H, CHUNK, HEAD_DIM)
    xi = jnp.broadcast_to(jnp.exp(lg[:, None] * (n + 1.0))[:, :, None], shape)
    zeta = jnp.broadcast_to(jnp.exp(lg[:, None] * (CHUNK - 1.0 - n))[:, :, None], shape)
    gamma_c = jnp.broadcast_to(jnp.exp(lg * CHUNK)[:, None, None], shape)
    return decay, xi, zeta, gamma_c


def _pack_small(parts):
    flat = []
    for p in parts:
        p = p.reshape(-1)
        flat.append(jnp.pad(p, (0, -p.shape[0] % LANES)))
    flat = jnp.concatenate(flat)
    return jnp.pad(flat, (0, SMALL_N - flat.shape[0])).reshape(SUBLANES, SMALL_N // SUBLANES)


def _unpack_small(packed, shapes):
    flat = packed.reshape(-1)
    out, at = [], 0
    for shp in shapes:
        size = 1
        for s in shp:
            size *= s
        out.append(flat[at:at + size].reshape(shp))
        at += size + (-size % LANES)
    return out


def kernel(x, norm_gain, w_in, ret_gn_gain, ret_gn_bias, sb_norm_gain, w_out, final_norm_gain, loss_target, m_norm_gain, m_w_in, m_ret_gn_gain, m_ret_gn_bias, m_sb_norm_gain, m_w_out, m_final_norm_gain, v_norm_gain, v_w_in, v_ret_gn_gain, v_ret_gn_bias, v_sb_norm_gain, v_w_out, v_final_norm_gain):
    S, D = x.shape[1], x.shape[2]
    W = w_in.shape[2]
    wo_rows = w_out.shape[1]
    H = W // HEAD_DIM
    xs, tgt = x[0], loss_target[0]
    mx, my, mc = _mesh_pos()
    pos = jnp.stack([mc, 2 * mx + my]).astype(jnp.int32)

    cos, sin = _rope_tables(S)
    tabs = _retention_tables(H)

    proj, w_all, ht = _in_proj_gather(xs, norm_gain, w_in[0].astype(BF16), cos, sin, _gather_order())
    mix_r = _ret_fwd(proj, tabs, ret_gn_gain, ret_gn_bias)
    mix_s, raw_s, carries, wo_all = _sb_fwd(proj, sb_norm_gain, w_out[0].astype(BF16))
    wo_full = wo_all.reshape(N_DEV * wo_rows, D)
    dx2, dx2b, dmix, loss_p, d_gf = _out_proj_loss(mix_r, mix_s, wo_full, xs, tgt, final_norm_gain[None])

    gwo = _grad_w_out(mix_r, mix_s, dx2b).reshape(N_DEV, wo_rows, D)
    dpr, d_rgain, d_rbias, rino = _ret_bwd(proj, dmix, tabs, ret_gn_gain, ret_gn_bias, cos, sin, gwo)
    dps, d_sgain, rbo = _sb_bwd(proj, raw_s, carries, dmix, sb_norm_gain, _rs_local_sum(gwo, rino, pos))
    gw_sibling, = _grad_w_in_half(ht, dpr, dps, (1 - mc).reshape(1).astype(jnp.int32), "grad_w_in_sibling")
    gw, rin = _grad_w_in_half(ht, dpr, dps, mc.reshape(1).astype(jnp.int32), "grad_w_in_own", to_sibling=gw_sibling)
    dh, rb = _dh_matmul(dpr, dps, w_all, _rs_local_sum(gw, rin, pos))
    grad_x, d_gain = _norm_bwd(xs, dx2, dh, norm_gain)
    g_in, d_in, m_in, v_in = _adamw_shard(gw, rin, rb, w_in[0], m_w_in[0], v_w_in[0], pos)
    g_out, d_out, m_out, v_out = _adamw_shard(gwo, rino, rbo, w_out[0], m_w_out[0], v_w_out[0], pos)

    small_w = [norm_gain, ret_gn_gain, ret_gn_bias, sb_norm_gain, final_norm_gain]
    small_m = [m_norm_gain, m_ret_gn_gain, m_ret_gn_bias, m_sb_norm_gain, m_final_norm_gain]
    small_v = [v_norm_gain, v_ret_gn_gain, v_ret_gn_bias, v_sb_norm_gain, v_final_norm_gain]
    shapes = [()] + [w.shape for w in small_w]
    zero = jnp.zeros((), F32)
    parts = _small_all_gather(_pack_small([loss_p[0, 0], d_gain, d_rgain, d_rbias, d_sgain, d_gf]))
    packed = _adamw_small(parts, _pack_small([zero] + small_w), _pack_small([zero] + small_m),
                          _pack_small([zero] + small_v))
    g_s, d_s, m_s, v_s = (_unpack_small(p, shapes) for p in packed)

    grads = [g_s[1], g_in[None], g_s[2], g_s[3], g_s[4], g_out[None], g_s[5]]
    deltas = [d_s[1], d_in[None], d_s[2], d_s[3], d_s[4], d_out[None], d_s[5]]
    new_m = [m_s[1], m_in[None], m_s[2], m_s[3], m_s[4], m_out[None], m_s[5]]
    new_v = [v_s[1], v_in[None], v_s[2], v_s[3], v_s[4], v_out[None], v_s[5]]
    return (g_s[0], grad_x[None], *grads, *deltas, *new_m, *new_v)
```

```python
import functools

import jax
import jax.numpy as jnp
from jax import lax
from jax.experimental import pallas as pl
from jax.experimental.pallas import tpu as pltpu

F32 = jnp.float32
BF16 = jnp.bfloat16

HEAD_DIM = 128
CHUNK = 128
RET_GROUP = 16
ROPE_THETA = 10000.0
EPS = 1e-6
ADAM_LR = 0.001
ADAM_B1 = 0.9
ADAM_B2 = 0.999
ADAM_EPS = 1e-08
ADAM_WD = 0.01
ADAM_STEP = 10

N_DEV = 8
LANES = 128
SUBLANES = 8
VMEM_LIMIT = 56 * 1024 * 1024
SB_BLOCK = 256
SB_PER_STEP = 4
SMALL_N = 8192
EXP_IS_ZERO_BELOW = -104.0
NOT_VISITED = -1e30
MESH = pl.DeviceIdType.MESH

NT = (((1,), (1,)), ((), ()))
TN = (((0,), (0,)), ((), ()))


def _params(*sem):
    return pltpu.CompilerParams(dimension_semantics=sem if sem else None, vmem_limit_bytes=VMEM_LIMIT)


def _dot(a, b):
    return jnp.dot(a, b, preferred_element_type=F32)


def _dot_nt(a, b):
    return lax.dot_general(a, b, NT, preferred_element_type=F32)


def _dot_tn(a, b):
    return lax.dot_general(a, b, TN, preferred_element_type=F32)


def _sigmoid(g):
    return 1.0 / (1.0 + jnp.exp(-g))


def _rot(a, cos, sin_signed):
    return a * cos + pltpu.roll(a, HEAD_DIM // 2, 1) * sin_signed


def _mesh_pos():
    return lax.axis_index("x"), lax.axis_index("y"), lax.axis_index("c")


def _to_sibling_copies(blocks, out_ref, send_sems, recv_sems):
    x, y, c = _mesh_pos()
    return [pltpu.make_async_remote_copy(
        src_ref=block, dst_ref=out_ref.at[k], send_sem=send_sems.at[k], recv_sem=recv_sems.at[k],
        device_id=(x, y, 1 - c), device_id_type=MESH) for k, block in enumerate(blocks)]


def _exchange_chip_sums(srcs, outs, send_sems, recv_sems):
    x, y, c = _mesh_pos()
    copies = []
    for arr, (src, out) in enumerate(zip(srcs, outs)):
        for k in range(1, 4):
            px = 1 - x if k & 2 else x
            py = 1 - y if k & 1 else y
            copies.append(pltpu.make_async_remote_copy(
                src_ref=src.at[2 * px + py], dst_ref=out.at[k - 1],
                send_sem=send_sems.at[arr, k - 1], recv_sem=recv_sems.at[arr, k - 1],
                device_id=(px, py, c), device_id_type=MESH))

    def start():
        for cp in copies:
            cp.start()

    def wait():
        for cp in copies:
            cp.wait_recv()
        for cp in copies:
            cp.wait_send()

    return start, wait


def _small_all_gather(small):
    rows, n = small.shape

    def body(s_ref, o_ref, send_sems, recv_sems, local_sem):
        start, wait = _exchange_with_all(s_ref, o_ref, send_sems, recv_sems, local_sem)
        start()
        wait()

    vmem = pl.BlockSpec(memory_space=pltpu.VMEM)
    return pl.pallas_call(
        body, name="small_all_gather",
        out_shape=jax.ShapeDtypeStruct((N_DEV, rows, n), small.dtype),
        in_specs=[vmem], out_specs=vmem,
        scratch_shapes=[pltpu.SemaphoreType.DMA((N_DEV - 1,)), pltpu.SemaphoreType.DMA((N_DEV - 1,)),
                        pltpu.SemaphoreType.DMA],
    )(small)


GATHER_SPLIT = 2
GATHER_STEPS = ([("own", 0, p) for p in range(GATHER_SPLIT)] + [("sibling", 0, p) for p in range(GATHER_SPLIT)]
                + [step for p in range(GATHER_SPLIT) for step in
                   (("ici", 0, p), ("ici", 1, p), ("passed", 0, p), ("passed", 1, p), ("ici", 2, p), ("passed", 2, p))])


def _gather_order():
    x, y, c = _mesh_pos()
    chips = [(1 - x, y), (x, 1 - y), (1 - x, 1 - y)]
    owner = {"own": lambda j: (x, y, c), "sibling": lambda j: (x, y, 1 - c),
             "ici": lambda j: (*chips[j], c), "passed": lambda j: (*chips[j], 1 - c)}
    blocks = [4 * px + 2 * py + pc for px, py, pc in (owner[kind](j) for kind, j, _ in GATHER_STEPS)]
    return (jnp.stack(blocks).astype(jnp.int32), jnp.array([p for _, _, p in GATHER_STEPS], jnp.int32))


def _in_proj_gather(x, gain, w_shard, cos, sin, order):
    S, D = x.shape
    W = w_shard.shape[1]
    wp = W // GATHER_SPLIT
    tm = min(1024, S)
    ni = S // tm
    n_steps = len(GATHER_STEPS)

    def body(blk_ref, piece_ref, x_ref, g_ref, w_ref, cos_ref, sin_ref, o_ref, wall_ref, ht_ref, h_scr, wbuf,
             send_sems, recv_sems, local_sem, load_sem):
        step, i = pl.program_id(0), pl.program_id(1)
        mx, my, c = _mesh_pos()
        me, sibling = (mx, my, c), (mx, my, 1 - c)
        chips = [(1 - mx, my), (mx, 1 - my), (1 - mx, 1 - my)]

        def piece_of(dev, p):
            px, py, pc = dev
            return wall_ref.at[4 * px + 2 * py + pc, :, pl.ds(p * wp, wp)]

        def copy(k, p, block, to, own=False):
            dst = piece_of(block, p)
            return pltpu.make_async_remote_copy(
                src_ref=w_ref.at[:, pl.ds(p * wp, wp)] if own else dst, dst_ref=dst,
                send_sem=send_sems.at[k, p], recv_sem=recv_sems.at[k, p], device_id=to, device_id_type=MESH)

        def load(src):
            cp = pltpu.make_async_copy(src, wbuf, load_sem)
            cp.start()
            cp.wait()

        pieces = range(GATHER_SPLIT)
        first = [cp for p in pieces for cp in
                 [copy(0, p, me, sibling, own=True)] + [copy(1 + j, p, me, (*chip, c), own=True)
                                                        for j, chip in enumerate(chips)]]
        passed = {(j, p): copy(4 + j, p, (*chip, c), sibling) for j, chip in enumerate(chips) for p in pieces}
        mine = pltpu.make_async_copy(w_ref, wall_ref.at[4 * mx + 2 * my + c], local_sem)

        @pl.when(i == 0)
        def _():
            for s, (kind, j, p) in enumerate(GATHER_STEPS):
                @pl.when(step == s)
                def _(s=s, kind=kind, j=j, p=p):
                    if s == 0:
                        for cp in first:
                            cp.start()
                        mine.start()
                    if kind == "own":
                        load(w_ref.at[:, pl.ds(p * wp, wp)])
                    elif kind == "sibling":
                        copy(0, p, sibling, me).wait_recv()
                        load(piece_of(sibling, p))
                    elif kind == "ici":
                        copy(1 + j, p, (*chips[j], c), me).wait_recv()
                        passed[j, p].start()
                        load(piece_of((*chips[j], c), p))
                    else:
                        copy(4 + j, p, (*chips[j], 1 - c), me).wait_recv()
                        load(piece_of((*chips[j], 1 - c), p))

        rows = pl.ds(pl.multiple_of(i * tm, tm), tm)

        @pl.when(step == 0)
        def _():
            xv = x_ref[...]
            r = lax.rsqrt(jnp.mean(xv * xv, axis=-1, keepdims=True) + EPS)
            hv = xv * r * g_ref[...]
            h_scr[rows, :] = hv.astype(BF16)
            ht_ref[...] = hv.T.astype(BF16)

        acc = _dot(h_scr[rows, :], wbuf[...])
        b = blk_ref[step]

        @pl.when(b >= 2)
        def _():
            o_ref[...] = acc

        @pl.when(b < 2)
        def _():
            scale = jnp.where(b == 1, HEAD_DIM ** -0.5, 1.0).astype(F32)
            cs, sn = cos_ref[...], sin_ref[...]
            for hh in range(wp // HEAD_DIM):
                cols = slice(hh * HEAD_DIM, (hh + 1) * HEAD_DIM)
                o_ref[:, cols] = _rot(acc[:, cols], cs, sn) * scale

        @pl.when((step == n_steps - 1) & (i == ni - 1))
        def _():
            for cp in first + list(passed.values()):
                cp.wait_send()
            mine.wait()

    hbm = pl.BlockSpec(memory_space=pltpu.HBM)
    rope = pl.BlockSpec((tm, HEAD_DIM), lambda s, i, blk, piece: (i, 0))
    first_pass = lambda s, i: jnp.where(s == 0, i, ni - 1)
    return pl.pallas_call(
        body, name="in_proj_gather",
        out_shape=(jax.ShapeDtypeStruct((N_DEV, S, W), F32), jax.ShapeDtypeStruct((N_DEV, D, W), BF16),
                   jax.ShapeDtypeStruct((D, S), BF16)),
        grid_spec=pltpu.PrefetchScalarGridSpec(
            num_scalar_prefetch=2, grid=(n_steps, ni),
            in_specs=[pl.BlockSpec((tm, D), lambda s, i, blk, piece: (first_pass(s, i), 0)),
                      pl.BlockSpec((1, D), lambda s, i, blk, piece: (0, 0)), hbm, rope, rope],
            out_specs=(pl.BlockSpec((None, tm, wp), lambda s, i, blk, piece: (blk[s], i, piece[s])), hbm,
                       pl.BlockSpec((D, tm), lambda s, i, blk, piece: (0, first_pass(s, i)))),
            scratch_shapes=[pltpu.VMEM((S, D), BF16), pltpu.VMEM((D, wp), BF16),
                            pltpu.SemaphoreType.DMA((7, GATHER_SPLIT)), pltpu.SemaphoreType.DMA((7, GATHER_SPLIT)),
                            pltpu.SemaphoreType.DMA, pltpu.SemaphoreType.DMA]),
        compiler_params=_params("arbitrary", "arbitrary"),
    )(*order, x, gain, w_shard, cos, sin)


def _head_spec(S, j):
    return pl.BlockSpec((None, S, HEAD_DIM), lambda h, *_: (j, 0, h))


def _bdot(a, b):
    return lax.dot_general(a, b, (((2,), (1,)), ((0,), (0,))), preferred_element_type=F32)


def _bdot_nt(a, b):
    return lax.dot_general(a, b, (((2,), (2,)), ((0,), (0,))), preferred_element_type=F32)


def _bdot_tn(a, b):
    return lax.dot_general(a, b, (((1,), (1,)), ((0,), (0,))), preferred_element_type=F32)


def _chunks(a):
    return a.reshape(a.shape[0] // CHUNK, CHUNK, a.shape[1])


def _ret_group(q, k, vb, states_b, dec, xi, ze):
    qb, kb = q.astype(BF16), k.astype(BF16)
    sb = (_bdot_nt(qb, kb) * dec).astype(BF16)
    qx = (q * xi).astype(BF16)
    out = _bdot(sb, vb) + _bdot(qx, states_b)
    return out, (qb, kb, sb, qx)


def _ret_states(kz, vb, gam, state, states_ref):
    kv = _bdot_tn(kz, vb)
    for u in range(RET_GROUP):
        states_ref[u] = state
        state = gam * state + kv[u]
    return state


def _table_specs():
    return [pl.BlockSpec((None, CHUNK, HEAD_DIM), lambda h, *_: (h, 0, 0))] * 4


def _ret_fwd(proj, tabs, gn_gain, gn_bias):
    _, S, W = proj.shape
    H, nc = W // HEAD_DIM, S // CHUNK
    assert nc % RET_GROUP == 0
    rows_per_group = RET_GROUP * CHUNK

    def body(q_ref, k_ref, v_ref, g_ref, dec_ref, xi_ref, ze_ref, gam_ref, gain_ref, bias_ref, o_ref, states_ref):
        dec, xi, ze, gam = dec_ref[...], xi_ref[...], ze_ref[...], gam_ref[...]
        gain, bias = gain_ref[...], bias_ref[...]

        def group(i, state):
            rows = pl.ds(pl.multiple_of(i * rows_per_group, rows_per_group), rows_per_group)
            q, k, vb = _chunks(q_ref[rows, :]), _chunks(k_ref[rows, :]), _chunks(v_ref[rows, :]).astype(BF16)
            state = _ret_states((k * ze).astype(BF16), vb, gam, state, states_ref)
            out, _ = _ret_group(q, k, vb, states_ref[...].astype(BF16), dec, xi, ze)
            mu = jnp.mean(out, axis=-1, keepdims=True)
            d = out - mu
            yn = d * lax.rsqrt(jnp.mean(d * d, axis=-1, keepdims=True) + EPS)
            g = _chunks(g_ref[rows, :])
            mix = g * _sigmoid(g) * (yn * gain + bias)
            o_ref[rows, :] = mix.reshape(rows_per_group, HEAD_DIM).astype(BF16)
            return state

        lax.fori_loop(0, nc // RET_GROUP, group, jnp.zeros((HEAD_DIM, HEAD_DIM), F32))

    vec = pl.BlockSpec((1, HEAD_DIM), lambda h: (0, h))
    return pl.pallas_call(
        body, name="ret_fwd", out_shape=jax.ShapeDtypeStruct((S, W), BF16), grid=(H,),
        in_specs=[_head_spec(S, 0), _head_spec(S, 1), _head_spec(S, 2), _head_spec(S, 3)] + _table_specs() + [vec, vec],
        out_specs=pl.BlockSpec((S, HEAD_DIM), lambda h: (0, h)),
        scratch_shapes=[pltpu.VMEM((RET_GROUP, HEAD_DIM, HEAD_DIM), F32)],
        compiler_params=_params("parallel"),
    )(proj, proj, proj, proj, *tabs, gn_gain, gn_bias)


def _sb_scores(qb, kk, masked, causal, upper):
    z = _dot_nt(qb, kk) * (HEAD_DIM ** -0.5)
    e = jnp.exp(-jnp.abs(z))
    l1p = jnp.log(1.0 + e)
    log_beta = jnp.minimum(z, 0.0) - l1p
    lk = jnp.minimum(-z, 0.0) - l1p
    if masked:
        lk = jnp.where(causal, lk, 0.0)
    hi = lk.astype(BF16)
    lo = (lk - hi.astype(F32)).astype(BF16)
    cs = _dot(hi, upper) + _dot(lo, upper)
    return log_beta, lk, cs


def _tri(B, kind):
    r = lax.broadcasted_iota(jnp.int32, (B, B), 0)
    c = lax.broadcasted_iota(jnp.int32, (B, B), 1)
    return {"gt": r > c, "lt": r < c}[kind]


def _ones_where(mask):
    return jnp.where(mask, 1.0, 0.0).astype(BF16)


def _exchange_with_all(src_ref, out_ref, send_sems, recv_sems, local_sem):
    x, y, c = _mesh_pos()
    peers = [(1 - x if k & 4 else x, 1 - y if k & 2 else y, 1 - c if k & 1 else c) for k in range(1, N_DEV)]

    def copy(k, owner, to):
        px, py, pc = owner
        return pltpu.make_async_remote_copy(
            src_ref=src_ref, dst_ref=out_ref.at[4 * px + 2 * py + pc], send_sem=send_sems.at[k],
            recv_sem=recv_sems.at[k], device_id=to, device_id_type=MESH)

    sends = [copy(k, (x, y, c), p) for k, p in enumerate(peers)]
    mine = pltpu.make_async_copy(src_ref, out_ref.at[4 * x + 2 * y + c], local_sem)

    def start():
        for cp in sends:
            cp.start()
        mine.start()

    def wait():
        for k, p in enumerate(peers):
            copy(k, p, p).wait_recv()
        for cp in sends:
            cp.wait_send()
        mine.wait()

    return start, wait


def _sb_fwd(proj, gain, wo_shard):
    _, S, W = proj.shape
    H = W // HEAD_DIM
    B = min(SB_BLOCK, S)
    nq = S // B
    assert nq <= HEAD_DIM and nq % SB_PER_STEP == 0
    ns = nq // SB_PER_STEP

    def body(q_ref, k_ref, v_ref, g_ref, gain_ref, wo_ref, mix_ref, raw_ref, car_ref, woall_ref, kb_ref, vb_ref,
             send_sems, recv_sems, local_sem):
        hd, si = pl.program_id(0), pl.program_id(1)
        start_gather, wait_gather = _exchange_with_all(wo_ref, woall_ref, send_sems, recv_sems, local_sem)
        pl.when((hd == 0) & (si == 0))(start_gather)

        @pl.when(si == 0)
        def _():
            kb_ref[...] = k_ref[...].astype(BF16)
            vb_ref[...] = v_ref[...].astype(BF16)

        causal = _tri(B, "gt")
        upper = _ones_where(causal)
        lane = lax.broadcasted_iota(jnp.int32, (B, HEAD_DIM), 1)

        def block(qb, kb, carry, acc, saved, masked):
            rows = pl.ds(pl.multiple_of(kb * B, B), B)
            log_beta, lk, cs = _sb_scores(qb, kb_ref[rows, :], masked, causal, upper)
            a = jnp.exp(log_beta + cs + carry)
            if masked:
                a = jnp.where(causal, a, 0.0)
            acc = acc + _dot(a.astype(BF16), vb_ref[rows, :])
            return carry + jnp.sum(lk, axis=1, keepdims=True), acc, jnp.where(lane == kb, carry, saved)

        init = (jnp.zeros((B, 1), F32), jnp.zeros((B, HEAD_DIM), F32), jnp.full((B, HEAD_DIM), NOT_VISITED, F32))

        def live(st):
            return (st[0] >= 0) & (jnp.max(st[1]) >= EXP_IS_ZERO_BELOW)

        def finish(u, acc, saved):
            rows = slice(u * B, (u + 1) * B)
            raw_ref[rows, :] = acc
            car_ref[rows, :] = saved
            yn = acc * lax.rsqrt(jnp.mean(acc * acc, axis=-1, keepdims=True) + EPS)
            g = g_ref[rows, :]
            mix_ref[rows, :] = (g * _sigmoid(g) * (yn * gain_ref[...])).astype(BF16)

        def whole(first_step):
            heads = []
            for u in range(SB_PER_STEP):
                qi = si * SB_PER_STEP + u
                qb = q_ref[u * B:(u + 1) * B, :].astype(BF16)
                state = block(qb, qi, *init, True)
                if not (first_step and u == 0):
                    state = block(qb, qi - 1, *state, False)
                heads.append((qi, qb, state))
            for u, (qi, qb, state) in enumerate(heads):
                if not (first_step and u == 0):
                    state = lax.while_loop(
                        live, lambda st, qb=qb: (st[0] - 1,) + block(qb, st[0], st[1], st[2], st[3], False),
                        (qi - 2,) + state)[1:]
                finish(u, state[1], state[2])

        pl.when(si == 0)(lambda: whole(True))
        pl.when(si > 0)(lambda: whole(False))
        pl.when((hd == H - 1) & (si == ns - 1))(wait_gather)

    tq = SB_PER_STEP * B
    tile = lambda j: pl.BlockSpec((None, tq, HEAD_DIM), lambda h, i: (j, i, h))
    out_tile = pl.BlockSpec((tq, HEAD_DIM), lambda h, i: (i, h))
    hbm = pl.BlockSpec(memory_space=pltpu.HBM)
    return pl.pallas_call(
        body, name="sb_fwd",
        out_shape=(jax.ShapeDtypeStruct((S, W), BF16), jax.ShapeDtypeStruct((S, W), F32),
                   jax.ShapeDtypeStruct((S, W), F32), jax.ShapeDtypeStruct((N_DEV,) + wo_shard.shape, BF16)),
        grid=(H, ns),
        in_specs=[tile(4), _head_spec(S, 5), _head_spec(S, 6), tile(7),
                  pl.BlockSpec((1, HEAD_DIM), lambda h, i: (0, h)), hbm],
        out_specs=(out_tile, out_tile, out_tile, hbm),
        scratch_shapes=[pltpu.VMEM((S, HEAD_DIM), BF16), pltpu.VMEM((S, HEAD_DIM), BF16),
                        pltpu.SemaphoreType.DMA((N_DEV - 1,)), pltpu.SemaphoreType.DMA((N_DEV - 1,)),
                        pltpu.SemaphoreType.DMA],
        compiler_params=_params("arbitrary", "arbitrary"),
    )(proj, proj, proj, proj, gain, wo_shard)


def _out_proj_loss(mix_r, mix_s, w_out, x, tgt, gf):
    S, W = mix_r.shape
    D = x.shape[1]
    tm = min(256, S)

    def body(mr_ref, ms_ref, wo_ref, x_ref, t_ref, gf_ref, dx2_ref, dx2b_ref, dmix_ref, loss_ref, gfn_ref):
        @pl.when(pl.program_id(0) == 0)
        def _():
            loss_ref[...] = jnp.zeros_like(loss_ref)
            gfn_ref[...] = jnp.zeros_like(gfn_ref)

        gfv = gf_ref[...]
        x2 = x_ref[...] + (_dot(mr_ref[...], wo_ref[:W, :]) + _dot(ms_ref[...], wo_ref[W:, :]))
        r2 = lax.rsqrt(jnp.mean(x2 * x2, axis=-1, keepdims=True) + EPS)
        n = x2 * r2
        err = n * gfv - t_ref[...]
        loss_ref[...] += 0.5 * jnp.sum(jnp.mean(err * err, axis=-1, keepdims=True))
        dy = err * (1.0 / D)
        gfn_ref[...] += jnp.sum(dy * n, axis=0, keepdims=True)
        dn = dy * gfv
        dx2 = r2 * (dn - n * jnp.mean(dn * n, axis=-1, keepdims=True))
        dx2_ref[...] = dx2
        b = dx2.astype(BF16)
        dx2b_ref[...] = b
        dmix_ref[:, :W] = _dot_nt(b, wo_ref[:W, :])
        dmix_ref[:, W:] = _dot_nt(b, wo_ref[W:, :])

    row = lambda width: pl.BlockSpec((tm, width), lambda i: (i, 0))
    return pl.pallas_call(
        body, name="out_proj_loss",
        out_shape=(jax.ShapeDtypeStruct((S, D), F32), jax.ShapeDtypeStruct((S, D), BF16),
                   jax.ShapeDtypeStruct((S, 2 * W), F32), jax.ShapeDtypeStruct((SUBLANES, LANES), F32),
                   jax.ShapeDtypeStruct((1, D), F32)),
        grid=(S // tm,),
        in_specs=[row(W), row(W), pl.BlockSpec((2 * W, D), lambda i: (0, 0)), row(D), row(D),
                  pl.BlockSpec((1, D), lambda i: (0, 0))],
        out_specs=(row(D), row(D), row(2 * W), pl.BlockSpec((SUBLANES, LANES), lambda i: (0, 0)),
                   pl.BlockSpec((1, D), lambda i: (0, 0))),
        compiler_params=_params("arbitrary"),
    )(mix_r, mix_s, w_out, x, tgt, gf)


def _silu_bwd(g, dm, normed):
    sig = _sigmoid(g)
    return dm * (g * sig), dm * normed * (sig * (1.0 + g * (1.0 - sig)))


def _ret_bwd(proj, dmix, tabs, gn_gain, gn_bias, cos, sin, gwo):
    _, S, W = proj.shape
    H, nc = W // HEAD_DIM, S // CHUNK
    assert nc % RET_GROUP == 0
    ng = nc // RET_GROUP
    rows_per_group = RET_GROUP * CHUNK

    def body(q_ref, k_ref, v_ref, g_ref, dm_ref, dec_ref, xi_ref, ze_ref, gam_ref, gain_ref, bias_ref, cos_ref,
             sin_ref, gwo_ref, dp_ref, dgain_ref, dbias_ref, rino_ref, rs_ref, dstates_ref, send_sems, recv_sems):
        dec, xi, ze, gam = dec_ref[...], xi_ref[...], ze_ref[...], gam_ref[...]
        gain, bias = gain_ref[...], bias_ref[...]
        hd = pl.program_id(0)
        other_core = 1 - lax.axis_index("c")
        copies = _to_sibling_copies([gwo_ref.at[2 * k + other_core] for k in range(4)], rino_ref, send_sems, recv_sems)

        @pl.when(hd == 0)
        def _():
            for cp in copies:
                cp.start()

        def group_rows(i):
            return pl.ds(pl.multiple_of(i * rows_per_group, rows_per_group), rows_per_group)

        def fwd_group(i, state):
            rows = group_rows(i)
            kz = (_chunks(k_ref[rows, :]) * ze).astype(BF16)
            return _ret_states(kz, _chunks(v_ref[rows, :]).astype(BF16), gam, state,
                               rs_ref.at[pl.ds(i * RET_GROUP, RET_GROUP)])

        lax.fori_loop(0, ng, fwd_group, jnp.zeros((HEAD_DIM, HEAD_DIM), F32))

        flat = lambda a: a.reshape(rows_per_group, HEAD_DIM)

        def bwd_group(t, carry):
            dgain, dbias, dstate = carry
            i = ng - 1 - t
            rows = group_rows(i)
            q, k, g = _chunks(q_ref[rows, :]), _chunks(k_ref[rows, :]), _chunks(g_ref[rows, :])
            vb = _chunks(v_ref[rows, :]).astype(BF16)
            rb = rs_ref[pl.ds(i * RET_GROUP, RET_GROUP)].astype(BF16)
            out, (qb, kb, sb, qx) = _ret_group(q, k, vb, rb, dec, xi, ze)
            kz = (k * ze).astype(BF16)
            mu = jnp.mean(out, axis=-1, keepdims=True)
            d = out - mu
            rstd = lax.rsqrt(jnp.mean(d * d, axis=-1, keepdims=True) + EPS)
            yn = d * rstd
            dgn, dg = _silu_bwd(g, _chunks(dm_ref[rows, :]), yn * gain + bias)
            dgain = dgain + jnp.sum(flat(dgn * yn), axis=0, keepdims=True)
            dbias = dbias + jnp.sum(flat(dgn), axis=0, keepdims=True)
            dyn = dgn * gain
            do = rstd * (dyn - jnp.mean(dyn, axis=-1, keepdims=True)
                         - yn * jnp.mean(dyn * yn, axis=-1, keepdims=True))
            dob = do.astype(BF16)
            dkv = _bdot_tn(qx, dob)
            for u in reversed(range(RET_GROUP)):
                dstates_ref[u] = dstate
                dstate = gam * dstate + dkv[u]
            drb = dstates_ref[...].astype(BF16)
            dv = _bdot_tn(sb, dob) + _bdot(kz, drb)
            dsb = (_bdot_nt(dob, vb) * dec).astype(BF16)
            dq = _bdot(dsb, kb) + _bdot_nt(dob, rb) * xi
            dk = _bdot_tn(dsb, qb) + _bdot_nt(vb, drb) * ze
            cs, sn = cos_ref[rows, :], -sin_ref[rows, :]
            dp_ref[0, rows, :] = _rot(flat(dq), cs, sn).astype(BF16)
            dp_ref[1, rows, :] = (_rot(flat(dk), cs, sn) * (HEAD_DIM ** -0.5)).astype(BF16)
            dp_ref[2, rows, :] = flat(dv).astype(BF16)
            dp_ref[3, rows, :] = flat(dg).astype(BF16)
            return dgain, dbias, dstate

        zero = jnp.zeros((1, HEAD_DIM), F32)
        dgain, dbias, _ = lax.fori_loop(0, ng, bwd_group, (zero, zero, jnp.zeros((HEAD_DIM, HEAD_DIM), F32)))
        dgain_ref[...] = dgain
        dbias_ref[...] = dbias

        @pl.when(hd == H - 1)
        def _():
            for cp in copies:
                cp.wait_recv()
            for cp in copies:
                cp.wait_send()

    vec = pl.BlockSpec((1, HEAD_DIM), lambda h: (0, h))
    full = pl.BlockSpec((S, HEAD_DIM), lambda h: (0, 0))
    hbm = pl.BlockSpec(memory_space=pltpu.HBM)
    return pl.pallas_call(
        body, name="ret_bwd",
        out_shape=(jax.ShapeDtypeStruct((4, S, W), BF16), jax.ShapeDtypeStruct((1, W), F32),
                   jax.ShapeDtypeStruct((1, W), F32), jax.ShapeDtypeStruct((4,) + gwo.shape[1:], gwo.dtype)),
        grid=(H,),
        in_specs=[_head_spec(S, 0), _head_spec(S, 1), _head_spec(S, 2), _head_spec(S, 3),
                  pl.BlockSpec((S, HEAD_DIM), lambda h: (0, h))] + _table_specs() + [vec, vec, full, full, hbm],
        out_specs=(pl.BlockSpec((4, S, HEAD_DIM), lambda h: (0, 0, h)), vec, vec, hbm),
        scratch_shapes=[pltpu.VMEM((nc, HEAD_DIM, HEAD_DIM), F32), pltpu.VMEM((RET_GROUP, HEAD_DIM, HEAD_DIM), F32),
                        pltpu.SemaphoreType.DMA((4,)), pltpu.SemaphoreType.DMA((4,))],
        compiler_params=_params("arbitrary"),
    )(proj, proj, proj, proj, dmix, *tabs, gn_gain, gn_bias, cos, sin, gwo)


def _sb_bwd(proj, raw, carries, dmix, gain, chip_sums_o):
    _, S, W = proj.shape
    H = W // HEAD_DIM
    B = min(SB_BLOCK, S)
    nq = S // B
    ns = nq // SB_PER_STEP

    def body(q_ref, k_ref, v_ref, g_ref, raw_ref, car_ref, dm_ref, gain_ref, so_ref, dp_ref, dgain_ref, ro_ref,
             kb_ref, vb_ref, dk_ref, dv_ref, send_sems, recv_sems):
        hd, si = pl.program_id(0), pl.program_id(1)
        start_exchange, wait_exchange = _exchange_chip_sums((so_ref,), (ro_ref,), send_sems, recv_sems)
        pl.when((hd == 0) & (si == 0))(start_exchange)

        @pl.when(si == 0)
        def _():
            kb_ref[...] = k_ref[...].astype(BF16)
            vb_ref[...] = v_ref[...].astype(BF16)
            dk_ref[...] = jnp.zeros_like(dk_ref)
            dv_ref[...] = jnp.zeros_like(dv_ref)
            dgain_ref[...] = jnp.zeros_like(dgain_ref)

        causal = _tri(B, "gt")
        upper = _ones_where(causal)
        before = _ones_where(_tri(B, "lt"))
        lane = lax.broadcasted_iota(jnp.int32, (B, HEAD_DIM), 1)
        gain_v = gain_ref[...]

        def prologue(u):
            qi = si * SB_PER_STEP + u
            rows = slice(u * B, (u + 1) * B)
            o = raw_ref[rows, :]
            rstd = lax.rsqrt(jnp.mean(o * o, axis=-1, keepdims=True) + EPS)
            yn = o * rstd
            dnrm, dg = _silu_bwd(g_ref[rows, :], dm_ref[rows, :], yn * gain_v)
            dp_ref[3, pl.ds(pl.multiple_of(qi * B, B), B), :] = dg.astype(BF16)
            dgain_ref[...] += jnp.sum(dnrm * yn, axis=0, keepdims=True)
            dyn = dnrm * gain_v
            do = rstd * (dyn - yn * jnp.mean(dyn * yn, axis=-1, keepdims=True))
            return qi, q_ref[rows, :].astype(BF16), do.astype(BF16), car_ref[rows, :]

        def block(ctx, kb, carry_g, dq, masked):
            _, qb, dob, saved = ctx
            rows = pl.ds(pl.multiple_of(kb * B, B), B)
            kk, vv = kb_ref[rows, :], vb_ref[rows, :]
            log_beta, _, cs = _sb_scores(qb, kk, masked, causal, upper)
            carry_lk = jnp.sum(jnp.where(lane == kb, saved, 0.0), axis=1, keepdims=True)
            a = jnp.exp(log_beta + cs + carry_lk)
            if masked:
                a = jnp.where(causal, a, 0.0)
            gmat = _dot_nt(dob, vv) * a
            dv_ref[rows, :] += _dot_tn(a.astype(BF16), dob)
            hi = gmat.astype(BF16)
            lo = (gmat - hi.astype(F32)).astype(BF16)
            dlk = carry_g + (_dot(hi, before) + _dot(lo, before))
            beta = jnp.exp(log_beta)
            dz = (gmat * (1.0 - beta) - dlk * beta) * (HEAD_DIM ** -0.5)
            if masked:
                dz = jnp.where(causal, dz, 0.0)
            dzb = dz.astype(BF16)
            dk_ref[rows, :] += _dot_tn(dzb, qb)
            return carry_g + jnp.sum(gmat, axis=1, keepdims=True), dq + _dot(dzb, kk)

        init = (jnp.zeros((B, 1), F32), jnp.zeros((B, HEAD_DIM), F32))

        def whole(first_step):
            ctxs = [prologue(u) for u in range(SB_PER_STEP)]
            states = []
            for u, ctx in enumerate(ctxs):
                state = init
                if not (first_step and u == 0):
                    visited = jnp.max(ctx[3], axis=0, keepdims=True) >= EXP_IS_ZERO_BELOW
                    first = jnp.min(jnp.where(visited, lane[:1, :], ctx[0]))
                    state = lax.fori_loop(first, ctx[0] - 1,
                                          lambda i, st, ctx=ctx: block(ctx, i, st[0], st[1], False), state)
                states.append(state)
            for u, (ctx, state) in enumerate(zip(ctxs, states)):
                if not (first_step and u == 0):
                    state = block(ctx, ctx[0] - 1, *state, False)
                state = block(ctx, ctx[0], *state, True)
                dp_ref[0, pl.ds(pl.multiple_of(ctx[0] * B, B), B), :] = state[1].astype(BF16)

        pl.when(si == 0)(lambda: whole(True))
        pl.when(si > 0)(lambda: whole(False))

        @pl.when(si == ns - 1)
        def _():
            dp_ref[1] = dk_ref[...].astype(BF16)
            dp_ref[2] = dv_ref[...].astype(BF16)

        pl.when((hd == H - 1) & (si == ns - 1))(wait_exchange)

    tq = SB_PER_STEP * B
    tile = lambda j: pl.BlockSpec((None, tq, HEAD_DIM), lambda h, i: (j, i, h))
    vec = pl.BlockSpec((1, HEAD_DIM), lambda h, i: (0, h))
    hbm = pl.BlockSpec(memory_space=pltpu.HBM)
    return pl.pallas_call(
        body, name="sb_bwd",
        out_shape=(jax.ShapeDtypeStruct((4, S, W), BF16), jax.ShapeDtypeStruct((1, W), F32),
                   jax.ShapeDtypeStruct((3,) + chip_sums_o.shape[1:], chip_sums_o.dtype)),
        grid=(H, ns),
        in_specs=[tile(4), _head_spec(S, 5), _head_spec(S, 6), tile(7),
                  pl.BlockSpec((tq, HEAD_DIM), lambda h, i: (i, h)),
                  pl.BlockSpec((tq, HEAD_DIM), lambda h, i: (i, h)),
                  pl.BlockSpec((tq, HEAD_DIM), lambda h, i: (i, H + h)), vec, hbm],
        out_specs=(pl.BlockSpec((4, S, HEAD_DIM), lambda h, i: (0, 0, h)), vec, hbm),
        scratch_shapes=[pltpu.VMEM((S, HEAD_DIM), BF16), pltpu.VMEM((S, HEAD_DIM), BF16),
                        pltpu.VMEM((S, HEAD_DIM), F32), pltpu.VMEM((S, HEAD_DIM), F32),
                        pltpu.SemaphoreType.DMA((1, 3)), pltpu.SemaphoreType.DMA((1, 3))],
        compiler_params=_params("arbitrary", "arbitrary"),
    )(proj, proj, proj, proj, raw, carries, dmix, gain, chip_sums_o)


def _grad_w_in_half(ht, dpr, dps, core, name, to_sibling=None):
    D, S = ht.shape
    _, _, W = dpr.shape
    tmm = min(512, D)
    nm = D // tmm

    def body(core_ref, ht_ref, r_ref, s_ref, *rest):
        o_ref = rest[1] if to_sibling is not None else rest[0]
        q, m = pl.program_id(0), pl.program_id(1)
        if to_sibling is not None:
            ga_ref, _, rin_ref, send_sems, recv_sems = rest
            copies = _to_sibling_copies([ga_ref.at[k] for k in range(4)], rin_ref, send_sems, recv_sems)

            @pl.when((q == 0) & (m == 0))
            def _():
                for cp in copies:
                    cp.start()

        @pl.when(q < 2)
        def _():
            o_ref[...] = _dot(ht_ref[...], r_ref[...])

        @pl.when(q >= 2)
        def _():
            o_ref[...] = _dot(ht_ref[...], s_ref[...])

        if to_sibling is not None:
            @pl.when((q == 3) & (m == nm - 1))
            def _():
                for cp in copies:
                    cp.wait_recv()
                for cp in copies:
                    cp.wait_send()

    hbm = pl.BlockSpec(memory_space=pltpu.HBM)
    gw_shape = jax.ShapeDtypeStruct((4, D, W), F32)
    out_shape, out_specs, extra_in, scratch = (gw_shape,), (pl.BlockSpec((None, tmm, W), lambda q, m, core: (q, m, 0)),), [], []
    if to_sibling is not None:
        out_shape += (gw_shape,)
        out_specs += (hbm,)
        extra_in = [hbm]
        scratch = [pltpu.SemaphoreType.DMA((4,)), pltpu.SemaphoreType.DMA((4,))]
    return pl.pallas_call(
        body, name=name, out_shape=out_shape,
        grid_spec=pltpu.PrefetchScalarGridSpec(
            num_scalar_prefetch=1, grid=(4, nm),
            in_specs=[pl.BlockSpec((tmm, S), lambda q, m, core: (m, 0)),
                      pl.BlockSpec((None, S, W), lambda q, m, core: (jnp.minimum(2 * q + core[0], 3), 0, 0)),
                      pl.BlockSpec((None, S, W), lambda q, m, core: (jnp.maximum(2 * q + core[0] - 4, 0), 0, 0))]
            + extra_in,
            out_specs=out_specs, scratch_shapes=scratch),
        compiler_params=_params("arbitrary", "arbitrary"),
    )(core, ht, dpr, dps, *(() if to_sibling is None else (to_sibling,)))


def _grad_w_out(mix_r, mix_s, dx2b):
    S, W = mix_r.shape
    D = dx2b.shape[1]
    tmm = min(512, W)
    tk = min(1024, S)

    def body(r_ref, s_ref, b_ref, o_ref):
        j, kk = pl.program_id(0), pl.program_id(2)

        def acc(a_ref):
            part = _dot_tn(a_ref[...], b_ref[...])

            @pl.when(kk == 0)
            def _():
                o_ref[...] = part

            @pl.when(kk > 0)
            def _():
                o_ref[...] += part

        pl.when(j == 0)(lambda: acc(r_ref))
        pl.when(j == 1)(lambda: acc(s_ref))

    return pl.pallas_call(
        body, name="grad_w_out", out_shape=jax.ShapeDtypeStruct((2, W, D), F32), grid=(2, W // tmm, S // tk),
        in_specs=[pl.BlockSpec((tk, tmm), lambda j, m, k: (k, m)),
                  pl.BlockSpec((tk, tmm), lambda j, m, k: (k, m)),
                  pl.BlockSpec((tk, D), lambda j, m, k: (k, 0))],
        out_specs=pl.BlockSpec((None, tmm, D), lambda j, m, k: (j, m, 0)),
        compiler_params=_params("parallel", "parallel", "arbitrary"),
    )(mix_r, mix_s, dx2b)


def _dh_matmul(dpr, dps, w_all, chip_sums):
    _, S, W = dpr.shape
    D = w_all.shape[1]
    tm = min(1024, S)
    ni = S // tm

    def body(r_ref, s_ref, w_ref, sa_ref, dh_ref, ra_ref, send_sems, recv_sems):
        i, j = pl.program_id(0), pl.program_id(1)
        start_exchange, wait_exchange = _exchange_chip_sums((sa_ref,), (ra_ref,), send_sems, recv_sems)
        pl.when((i == 0) & (j == 0))(start_exchange)

        def acc(b_ref):
            part = _dot_nt(b_ref[...], w_ref[...])

            @pl.when(j == 0)
            def _():
                dh_ref[...] = part

            @pl.when(j > 0)
            def _():
                dh_ref[...] += part

        pl.when(j < 4)(lambda: acc(r_ref))
        pl.when(j >= 4)(lambda: acc(s_ref))
        pl.when((i == ni - 1) & (j == 7))(wait_exchange)

    hbm = pl.BlockSpec(memory_space=pltpu.HBM)
    return pl.pallas_call(
        body, name="dh_matmul",
        out_shape=(jax.ShapeDtypeStruct((S, D), F32), jax.ShapeDtypeStruct((3,) + chip_sums.shape[1:], chip_sums.dtype)),
        grid=(ni, 8),
        in_specs=[pl.BlockSpec((None, tm, W), lambda i, j: (jnp.minimum(j, 3), i, 0)),
                  pl.BlockSpec((None, tm, W), lambda i, j: (jnp.maximum(j - 4, 0), i, 0)),
                  pl.BlockSpec((None, D, W), lambda i, j: (j, 0, 0)), hbm],
        out_specs=(pl.BlockSpec((tm, D), lambda i, j: (i, 0)), hbm),
        scratch_shapes=[pltpu.SemaphoreType.DMA((1, 3)), pltpu.SemaphoreType.DMA((1, 3))],
        compiler_params=_params("arbitrary", "arbitrary"),
    )(dpr, dps, w_all, chip_sums)


def _norm_bwd(x, dx2, dh, gain):
    S, D = x.shape
    tm = min(256, S)

    def body(x_ref, dx2_ref, dh_ref, g_ref, gx_ref, dgain_ref):
        @pl.when(pl.program_id(0) == 0)
        def _():
            dgain_ref[...] = jnp.zeros_like(dgain_ref)

        xv, dh_v = x_ref[...], dh_ref[...]
        r1 = lax.rsqrt(jnp.mean(xv * xv, axis=-1, keepdims=True) + EPS)
        n = xv * r1
        dgain_ref[...] += jnp.sum(dh_v * n, axis=0, keepdims=True)
        dn = dh_v * g_ref[...]
        gx_ref[...] = dx2_ref[...] + r1 * (dn - n * jnp.mean(dn * n, axis=-1, keepdims=True))

    row = pl.BlockSpec((tm, D), lambda i: (i, 0))
    one = pl.BlockSpec((1, D), lambda i: (0, 0))
    return pl.pallas_call(
        body, name="norm_bwd", out_shape=(jax.ShapeDtypeStruct((S, D), F32), jax.ShapeDtypeStruct((1, D), F32)),
        grid=(S // tm,), in_specs=[row, row, row, one], out_specs=(row, one),
        compiler_params=_params("arbitrary"),
    )(x, dx2, dh, gain)


def _own_block(gw, pos, q):
    return q if gw.shape[0] == 4 else 2 * q + pos[0]


def _rs_local_sum(gw, rin, pos):
    _, R, C = gw.shape
    tr = min(256, R)
    other = lambda k, pos: (pos[1] + 1 + k) % 4

    def body(pos_ref, a_ref, b_ref, o_ref):
        o_ref[...] = (a_ref[...] + b_ref[...]).astype(BF16)

    return pl.pallas_call(
        body, name="rs_local_sum", out_shape=jax.ShapeDtypeStruct((4, R, C), BF16),
        grid_spec=pltpu.PrefetchScalarGridSpec(
            num_scalar_prefetch=1, grid=(3, R // tr),
            in_specs=[pl.BlockSpec((None, tr, C), lambda k, i, pos: (_own_block(gw, pos, other(k, pos)), i, 0)),
                      pl.BlockSpec((None, tr, C), lambda k, i, pos: (other(k, pos), i, 0))],
            out_specs=pl.BlockSpec((None, tr, C), lambda k, i, pos: (other(k, pos), i, 0))),
        compiler_params=_params("parallel", "parallel"),
    )(pos, gw, rin)


def _adamw(w, g, m, v):
    m2 = ADAM_B1 * m + (1.0 - ADAM_B1) * g
    v2 = ADAM_B2 * v + (1.0 - ADAM_B2) * (g * g)
    m_hat = m2 / (1.0 - ADAM_B1 ** ADAM_STEP)
    v_hat = v2 / (1.0 - ADAM_B2 ** ADAM_STEP)
    delta = -ADAM_LR * (m_hat / (jnp.sqrt(v_hat) + ADAM_EPS) + ADAM_WD * w)
    return delta, m2, v2


def _adamw_shard(gw, rin, rb, w, m, v, pos):
    _, R, C = gw.shape
    tr = min(256, R)

    def body(pos_ref, a_ref, b_ref, rb_ref, w_ref, m_ref, v_ref, g_ref, d_ref, m2_ref, v2_ref):
        g = a_ref[...] + b_ref[...]
        for k in range(3):
            g = g + rb_ref[k].astype(F32)
        g_ref[...] = g
        d_ref[...], m2_ref[...], v2_ref[...] = _adamw(w_ref[...], g, m_ref[...], v_ref[...])

    plain = pl.BlockSpec((tr, C), lambda i, pos: (i, 0))
    shape = jax.ShapeDtypeStruct((R, C), F32)
    return pl.pallas_call(
        body, name="adamw_shard", out_shape=(shape,) * 4,
        grid_spec=pltpu.PrefetchScalarGridSpec(
            num_scalar_prefetch=1, grid=(R // tr,),
            in_specs=[pl.BlockSpec((None, tr, C), lambda i, pos: (_own_block(gw, pos, pos[1]), i, 0)),
                      pl.BlockSpec((None, tr, C), lambda i, pos: (pos[1], i, 0)),
                      pl.BlockSpec((3, tr, C), lambda i, pos: (0, i, 0)), plain, plain, plain],
            out_specs=(plain,) * 4),
        compiler_params=_params("parallel"),
    )(pos, gw, rin, rb, w, m, v)


def _adamw_small(parts, w, m, v):
    _, rows, n = parts.shape

    def body(p_ref, w_ref, m_ref, v_ref, g_ref, d_ref, m2_ref, v2_ref):
        g = p_ref[0]
        for d in range(1, N_DEV):
            g = g + p_ref[d]
        g_ref[...] = g
        d_ref[...], m2_ref[...], v2_ref[...] = _adamw(w_ref[...], g, m_ref[...], v_ref[...])

    shape = jax.ShapeDtypeStruct((rows, n), F32)
    return pl.pallas_call(body, name="adamw_small", out_shape=(shape,) * 4)(parts, w, m, v)


def _rope_tables(S):
    half = HEAD_DIM // 2
    inv = ROPE_THETA ** (-jnp.arange(half, dtype=F32) / half)
    ang = jnp.arange(S, dtype=F32)[:, None] * inv[None, :]
    cos, sin = jnp.cos(ang), jnp.sin(ang)
    return jnp.concatenate([cos, cos], axis=1), jnp.concatenate([-sin, sin], axis=1)


def _retention_tables(H):
    lg = jnp.log1p(-jnp.exp2(-5.0 - jnp.arange(H, dtype=F32)))
    n = jnp.arange(CHUNK, dtype=F32)
    rel = n[:, None] - n[None, :]
    decay = jnp.where(rel >= 0, jnp.exp(lg[:, None, None] * jnp.maximum(rel, 0.0)), 0.0)
    shape = (H, CHUNK, HEAD_DIM)
    xi = jnp.broadcast_to(jnp.exp(lg[:, None] * (n + 1.0))[:, :, None], shape)
    zeta = jnp.broadcast_to(jnp.exp(lg[:, None] * (CHUNK - 1.0 - n))[:, :, None], shape)
    gamma_c = jnp.broadcast_to(jnp.exp(lg * CHUNK)[:, None, None], shape)
    return decay, xi, zeta, gamma_c


def _pack_small(parts):
    flat = []
    for p in parts:
        p = p.reshape(-1)
        flat.append(jnp.pad(p, (0, -p.shape[0] % LANES)))
    flat = jnp.concatenate(flat)
    return jnp.pad(flat, (0, SMALL_N - flat.shape[0])).reshape(SUBLANES, SMALL_N // SUBLANES)


def _unpack_small(packed, shapes):
    flat = packed.reshape(-1)
    out, at = [], 0
    for shp in shapes:
        size = 1
        for s in shp:
            size *= s
        out.append(flat[at:at + size].reshape(shp))
        at += size + (-size % LANES)
    return out


def kernel(x, norm_gain, w_in, ret_gn_gain, ret_gn_bias, sb_norm_gain, w_out, final_norm_gain, loss_target, m_norm_gain, m_w_in, m_ret_gn_gain, m_ret_gn_bias, m_sb_norm_gain, m_w_out, m_final_norm_gain, v_norm_gain, v_w_in, v_ret_gn_gain, v_ret_gn_bias, v_sb_norm_gain, v_w_out, v_final_norm_gain):
    S, D = x.shape[1], x.shape[2]
    W = w_in.shape[2]
    wo_rows = w_out.shape[1]
    H = W // HEAD_DIM
    xs, tgt = x[0], loss_target[0]
    mx, my, mc = _mesh_pos()
    pos = jnp.stack([mc, 2 * mx + my]).astype(jnp.int32)

    cos, sin = _rope_tables(S)
    tabs = _retention_tables(H)

    proj, w_all, ht = _in_proj_gather(xs, norm_gain, w_in[0].astype(BF16), cos, sin, _gather_order())
    mix_r = _ret_fwd(proj, tabs, ret_gn_gain, ret_gn_bias)
    mix_s, raw_s, carries, wo_all = _sb_fwd(proj, sb_norm_gain, w_out[0].astype(BF16))
    wo_full = wo_all.reshape(N_DEV * wo_rows, D)
    dx2, dx2b, dmix, loss_p, d_gf = _out_proj_loss(mix_r, mix_s, wo_full, xs, tgt, final_norm_gain[None])

    gwo = _grad_w_out(mix_r, mix_s, dx2b).reshape(N_DEV, wo_rows, D)
    dpr, d_rgain, d_rbias, rino = _ret_bwd(proj, dmix, tabs, ret_gn_gain, ret_gn_bias, cos, sin, gwo)
    dps, d_sgain, rbo = _sb_bwd(proj, raw_s, carries, dmix, sb_norm_gain, _rs_local_sum(gwo, rino, pos))
    gw_sibling, = _grad_w_in_half(ht, dpr, dps, (1 - mc).reshape(1).astype(jnp.int32), "grad_w_in_sibling")
    gw, rin = _grad_w_in_half(ht, dpr, dps, mc.reshape(1).astype(jnp.int32), "grad_w_in_own", to_sibling=gw_sibling)
    dh, rb = _dh_matmul(dpr, dps, w_all, _rs_local_sum(gw, rin, pos))
    grad_x, d_gain = _norm_bwd(xs, dx2, dh, norm_gain)
    g_in, d_in, m_in, v_in = _adamw_shard(gw, rin, rb, w_in[0], m_w_in[0], v_w_in[0], pos)
    g_out, d_out, m_out, v_out = _adamw_shard(gwo, rino, rbo, w_out[0], m_w_out[0], v_w_out[0], pos)

    small_w = [norm_gain, ret_gn_gain, ret_gn_bias, sb_norm_gain, final_norm_gain]
    small_m = [m_norm_gain, m_ret_gn_gain, m_ret_gn_bias, m_sb_norm_gain, m_final_norm_gain]
    small_v = [v_norm_gain, v_ret_gn_gain, v_ret_gn_bias, v_sb_norm_gain, v_final_norm_gain]
    shapes = [()] + [w.shape for w in small_w]
    zero = jnp.zeros((), F32)
    parts = _small_all_gather(_pack_small([loss_p[0, 0], d_gain, d_rgain, d_rbias, d_sgain, d_gf]))
    packed = _adamw_small(parts, _pack_small([zero] + small_w), _pack_small([zero] + small_m),
                          _pack_small([zero] + small_v))
    g_s, d_s, m_s, v_s = (_unpack_small(p, shapes) for p in packed)

    grads = [g_s[1], g_in[None], g_s[2], g_s[3], g_s[4], g_out[None], g_s[5]]
    deltas = [d_s[1], d_in[None], d_s[2], d_s[3], d_s[4], d_out[None], d_s[5]]
    new_m = [m_s[1], m_in[None], m_s[2], m_s[3], m_s[4], m_out[None], m_s[5]]
    new_v = [v_s[1], v_in[None], v_s[2], v_s[3], v_s[4], v_out[None], v_s[5]]
    return (g_s[0], grad_x[None], *grads, *deltas, *new_m, *new_v)
```

```python
import functools

import jax
import jax.numpy as jnp
from jax import lax
from jax.experimental import pallas as pl
from jax.experimental.pallas import tpu as pltpu

F32 = jnp.float32
BF16 = jnp.bfloat16

HEAD_DIM = 128
CHUNK = 128
RET_GROUP = 16
ROPE_THETA = 10000.0
EPS = 1e-6
ADAM_LR = 0.001
ADAM_B1 = 0.9
ADAM_B2 = 0.999
ADAM_EPS = 1e-08
ADAM_WD = 0.01
ADAM_STEP = 10

N_DEV = 8
LANES = 128
SUBLANES = 8
VMEM_LIMIT = 56 * 1024 * 1024
SB_BLOCK = 256
SB_PER_STEP = 4
SMALL_N = 8192
EXP_IS_ZERO_BELOW = -104.0
NOT_VISITED = -1e30
MESH = pl.DeviceIdType.MESH

NT = (((1,), (1,)), ((), ()))
TN = (((0,), (0,)), ((), ()))


def _params(*sem):
    return pltpu.CompilerParams(dimension_semantics=sem if sem else None, vmem_limit_bytes=VMEM_LIMIT)


def _dot(a, b):
    return jnp.dot(a, b, preferred_element_type=F32)


def _dot_nt(a, b):
    return lax.dot_general(a, b, NT, preferred_element_type=F32)


def _dot_tn(a, b):
    return lax.dot_general(a, b, TN, preferred_element_type=F32)


def _sigmoid(g):
    return 1.0 / (1.0 + jnp.exp(-g))


def _rot(a, cos, sin_signed):
    return a * cos + pltpu.roll(a, HEAD_DIM // 2, 1) * sin_signed


def _mesh_pos():
    return lax.axis_index("x"), lax.axis_index("y"), lax.axis_index("c")


def _to_sibling_copies(blocks, out_ref, send_sems, recv_sems):
    x, y, c = _mesh_pos()
    return [pltpu.make_async_remote_copy(
        src_ref=block, dst_ref=out_ref.at[k], send_sem=send_sems.at[k], recv_sem=recv_sems.at[k],
        device_id=(x, y, 1 - c), device_id_type=MESH) for k, block in enumerate(blocks)]


def _exchange_chip_sums(srcs, outs, send_sems, recv_sems):
    x, y, c = _mesh_pos()
    copies = []
    for arr, (src, out) in enumerate(zip(srcs, outs)):
        for k in range(1, 4):
            px = 1 - x if k & 2 else x
            py = 1 - y if k & 1 else y
            copies.append(pltpu.make_async_remote_copy(
                src_ref=src.at[2 * px + py], dst_ref=out.at[k - 1],
                send_sem=send_sems.at[arr, k - 1], recv_sem=recv_sems.at[arr, k - 1],
                device_id=(px, py, c), device_id_type=MESH))

    def start():
        for cp in copies:
            cp.start()

    def wait():
        for cp in copies:
            cp.wait_recv()
        for cp in copies:
            cp.wait_send()

    return start, wait


def _small_all_gather(small):
    rows, n = small.shape

    def body(s_ref, o_ref, send_sems, recv_sems, local_sem):
        start, wait = _exchange_with_all(s_ref, o_ref, send_sems, recv_sems, local_sem)
        start()
        wait()

    vmem = pl.BlockSpec(memory_space=pltpu.VMEM)
    return pl.pallas_call(
        body, name="small_all_gather",
        out_shape=jax.ShapeDtypeStruct((N_DEV, rows, n), small.dtype),
        in_specs=[vmem], out_specs=vmem,
        scratch_shapes=[pltpu.SemaphoreType.DMA((N_DEV - 1,)), pltpu.SemaphoreType.DMA((N_DEV - 1,)),
                        pltpu.SemaphoreType.DMA],
    )(small)


GATHER_SPLIT = 2
GATHER_STEPS = ([("own", 0, p) for p in range(GATHER_SPLIT)] + [("sibling", 0, p) for p in range(GATHER_SPLIT)]
                + [step for p in range(GATHER_SPLIT) for step in
                   (("ici", 0, p), ("ici", 1, p), ("passed", 0, p), ("passed", 1, p))]
                + [step for p in range(GATHER_SPLIT) for step in (("ici", 2, p), ("passed", 2, p))])


def _via_x(p):
    return p % 2 == 0


def _gather_order():
    x, y, c = _mesh_pos()
    chips = [(1 - x, y), (x, 1 - y), (1 - x, 1 - y)]
    owner = {"own": lambda j: (x, y, c), "sibling": lambda j: (x, y, 1 - c),
             "ici": lambda j: (*chips[j], c), "passed": lambda j: (*chips[j], 1 - c)}
    blocks = [4 * px + 2 * py + pc for px, py, pc in (owner[kind](j) for kind, j, _ in GATHER_STEPS)]
    return (jnp.stack(blocks).astype(jnp.int32), jnp.array([p for _, _, p in GATHER_STEPS], jnp.int32))


def _in_proj_gather(x, gain, w_shard, cos, sin, order):
    S, D = x.shape
    W = w_shard.shape[1]
    wp = W // GATHER_SPLIT
    tm = min(1024, S)
    ni = S // tm
    n_steps = len(GATHER_STEPS)

    def body(blk_ref, piece_ref, x_ref, g_ref, w_ref, cos_ref, sin_ref, o_ref, wall_ref, ht_ref, h_scr, wbuf,
             send_sems, recv_sems, local_sem, load_sem):
        step, i = pl.program_id(0), pl.program_id(1)
        mx, my, c = _mesh_pos()
        me, sibling = (mx, my, c), (mx, my, 1 - c)
        chips = [(1 - mx, my), (mx, 1 - my), (1 - mx, 1 - my)]

        def piece_of(dev, p):
            px, py, pc = dev
            return wall_ref.at[4 * px + 2 * py + pc, :, pl.ds(p * wp, wp)]

        def copy(k, p, block, to, own=False):
            dst = piece_of(block, p)
            return pltpu.make_async_remote_copy(
                src_ref=w_ref.at[:, pl.ds(p * wp, wp)] if own else dst, dst_ref=dst,
                send_sem=send_sems.at[k, p], recv_sem=recv_sems.at[k, p], device_id=to, device_id_type=MESH)

        def load(src):
            cp = pltpu.make_async_copy(src, wbuf, load_sem)
            cp.start()
            cp.wait()

        pieces = range(GATHER_SPLIT)
        first = [cp for p in pieces for cp in
                 [copy(0, p, me, sibling, own=True)] + [copy(1 + j, p, me, (*chips[j], c), own=True) for j in (0, 1)]]
        passed = {(j, p): copy(4 + j, p, (*chip, c), sibling) for j, chip in enumerate(chips) for p in pieces}
        onward = {p: copy(3, p, (*chips[0 if _via_x(p) else 1], c), (*chips[1 if _via_x(p) else 0], c)) for p in pieces}
        mine = pltpu.make_async_copy(w_ref, wall_ref.at[4 * mx + 2 * my + c], local_sem)

        @pl.when(i == 0)
        def _():
            for s, (kind, j, p) in enumerate(GATHER_STEPS):
                @pl.when(step == s)
                def _(s=s, kind=kind, j=j, p=p):
                    if s == 0:
                        for cp in first:
                            cp.start()
                        mine.start()
                    if kind == "own":
                        load(w_ref.at[:, pl.ds(p * wp, wp)])
                    elif kind == "sibling":
                        copy(0, p, sibling, me).wait_recv()
                        load(piece_of(sibling, p))
                    elif kind == "ici":
                        copy(1 + j, p, (*chips[j], c), me).wait_recv()
                        if j == (0 if _via_x(p) else 1):
                            onward[p].start()
                        passed[j, p].start()
                        load(piece_of((*chips[j], c), p))
                    else:
                        copy(4 + j, p, (*chips[j], 1 - c), me).wait_recv()
                        load(piece_of((*chips[j], 1 - c), p))

        rows = pl.ds(pl.multiple_of(i * tm, tm), tm)

        @pl.when(step == 0)
        def _():
            xv = x_ref[...]
            r = lax.rsqrt(jnp.mean(xv * xv, axis=-1, keepdims=True) + EPS)
            hv = xv * r * g_ref[...]
            h_scr[rows, :] = hv.astype(BF16)
            ht_ref[...] = hv.T.astype(BF16)

        acc = _dot(h_scr[rows, :], wbuf[...])
        b = blk_ref[step]

        @pl.when(b >= 2)
        def _():
            o_ref[...] = acc

        @pl.when(b < 2)
        def _():
            scale = jnp.where(b == 1, HEAD_DIM ** -0.5, 1.0).astype(F32)
            cs, sn = cos_ref[...], sin_ref[...]
            for hh in range(wp // HEAD_DIM):
                cols = slice(hh * HEAD_DIM, (hh + 1) * HEAD_DIM)
                o_ref[:, cols] = _rot(acc[:, cols], cs, sn) * scale

        @pl.when((step == n_steps - 1) & (i == ni - 1))
        def _():
            for cp in first + list(passed.values()) + list(onward.values()):
                cp.wait_send()
            mine.wait()

    hbm = pl.BlockSpec(memory_space=pltpu.HBM)
    rope = pl.BlockSpec((tm, HEAD_DIM), lambda s, i, blk, piece: (i, 0))
    first_pass = lambda s, i: jnp.where(s == 0, i, ni - 1)
    return pl.pallas_call(
        body, name="in_proj_gather",
        out_shape=(jax.ShapeDtypeStruct((N_DEV, S, W), F32), jax.ShapeDtypeStruct((N_DEV, D, W), BF16),
                   jax.ShapeDtypeStruct((D, S), BF16)),
        grid_spec=pltpu.PrefetchScalarGridSpec(
            num_scalar_prefetch=2, grid=(n_steps, ni),
            in_specs=[pl.BlockSpec((tm, D), lambda s, i, blk, piece: (first_pass(s, i), 0)),
                      pl.BlockSpec((1, D), lambda s, i, blk, piece: (0, 0)), hbm, rope, rope],
            out_specs=(pl.BlockSpec((None, tm, wp), lambda s, i, blk, piece: (blk[s], i, piece[s])), hbm,
                       pl.BlockSpec((D, tm), lambda s, i, blk, piece: (0, first_pass(s, i)))),
            scratch_shapes=[pltpu.VMEM((S, D), BF16), pltpu.VMEM((D, wp), BF16),
                            pltpu.SemaphoreType.DMA((7, GATHER_SPLIT)), pltpu.SemaphoreType.DMA((7, GATHER_SPLIT)),
                            pltpu.SemaphoreType.DMA, pltpu.SemaphoreType.DMA]),
        compiler_params=_params("arbitrary", "arbitrary"),
    )(*order, x, gain, w_shard, cos, sin)


def _head_spec(S, j):
    return pl.BlockSpec((None, S, HEAD_DIM), lambda h, *_: (j, 0, h))


def _bdot(a, b):
    return lax.dot_general(a, b, (((2,), (1,)), ((0,), (0,))), preferred_element_type=F32)


def _bdot_nt(a, b):
    return lax.dot_general(a, b, (((2,), (2,)), ((0,), (0,))), preferred_element_type=F32)


def _bdot_tn(a, b):
    return lax.dot_general(a, b, (((1,), (1,)), ((0,), (0,))), preferred_element_type=F32)


def _chunks(a):
    return a.reshape(a.shape[0] // CHUNK, CHUNK, a.shape[1])


def _ret_group(q, k, vb, states_b, dec, xi, ze):
    qb, kb = q.astype(BF16), k.astype(BF16)
    sb = (_bdot_nt(qb, kb) * dec).astype(BF16)
    qx = (q * xi).astype(BF16)
    out = _bdot(sb, vb) + _bdot(qx, states_b)
    return out, (qb, kb, sb, qx)


def _ret_states(kz, vb, gam, state, states_ref):
    kv = _bdot_tn(kz, vb)
    for u in range(RET_GROUP):
        states_ref[u] = state
        state = gam * state + kv[u]
    return state


def _table_specs():
    return [pl.BlockSpec((None, CHUNK, HEAD_DIM), lambda h, *_: (h, 0, 0))] * 4


def _ret_fwd(proj, tabs, gn_gain, gn_bias):
    _, S, W = proj.shape
    H, nc = W // HEAD_DIM, S // CHUNK
    assert nc % RET_GROUP == 0
    rows_per_group = RET_GROUP * CHUNK

    def body(q_ref, k_ref, v_ref, g_ref, dec_ref, xi_ref, ze_ref, gam_ref, gain_ref, bias_ref, o_ref, states_ref):
        dec, xi, ze, gam = dec_ref[...], xi_ref[...], ze_ref[...], gam_ref[...]
        gain, bias = gain_ref[...], bias_ref[...]

        def group(i, state):
            rows = pl.ds(pl.multiple_of(i * rows_per_group, rows_per_group), rows_per_group)
            q, k, vb = _chunks(q_ref[rows, :]), _chunks(k_ref[rows, :]), _chunks(v_ref[rows, :]).astype(BF16)
            state = _ret_states((k * ze).astype(BF16), vb, gam, state, states_ref)
            out, _ = _ret_group(q, k, vb, states_ref[...].astype(BF16), dec, xi, ze)
            mu = jnp.mean(out, axis=-1, keepdims=True)
            d = out - mu
            yn = d * lax.rsqrt(jnp.mean(d * d, axis=-1, keepdims=True) + EPS)
            g = _chunks(g_ref[rows, :])
            mix = g * _sigmoid(g) * (yn * gain + bias)
            o_ref[rows, :] = mix.reshape(rows_per_group, HEAD_DIM).astype(BF16)
            return state

        lax.fori_loop(0, nc // RET_GROUP, group, jnp.zeros((HEAD_DIM, HEAD_DIM), F32))

    vec = pl.BlockSpec((1, HEAD_DIM), lambda h: (0, h))
    return pl.pallas_call(
        body, name="ret_fwd", out_shape=jax.ShapeDtypeStruct((S, W), BF16), grid=(H,),
        in_specs=[_head_spec(S, 0), _head_spec(S, 1), _head_spec(S, 2), _head_spec(S, 3)] + _table_specs() + [vec, vec],
        out_specs=pl.BlockSpec((S, HEAD_DIM), lambda h: (0, h)),
        scratch_shapes=[pltpu.VMEM((RET_GROUP, HEAD_DIM, HEAD_DIM), F32)],
        compiler_params=_params("parallel"),
    )(proj, proj, proj, proj, *tabs, gn_gain, gn_bias)


def _sb_scores(qb, kk, masked, causal, upper):
    z = _dot_nt(qb, kk) * (HEAD_DIM ** -0.5)
    e = jnp.exp(-jnp.abs(z))
    l1p = jnp.log(1.0 + e)
    log_beta = jnp.minimum(z, 0.0) - l1p
    lk = jnp.minimum(-z, 0.0) - l1p
    if masked:
        lk = jnp.where(causal, lk, 0.0)
    hi = lk.astype(BF16)
    lo = (lk - hi.astype(F32)).astype(BF16)
    cs = _dot(hi, upper) + _dot(lo, upper)
    return log_beta, lk, cs


def _tri(B, kind):
    r = lax.broadcasted_iota(jnp.int32, (B, B), 0)
    c = lax.broadcasted_iota(jnp.int32, (B, B), 1)
    return {"gt": r > c, "lt": r < c}[kind]


def _ones_where(mask):
    return jnp.where(mask, 1.0, 0.0).astype(BF16)


def _exchange_with_all(src_ref, out_ref, send_sems, recv_sems, local_sem):
    x, y, c = _mesh_pos()
    peers = [(1 - x if k & 4 else x, 1 - y if k & 2 else y, 1 - c if k & 1 else c) for k in range(1, N_DEV)]

    def copy(k, owner, to):
        px, py, pc = owner
        return pltpu.make_async_remote_copy(
            src_ref=src_ref, dst_ref=out_ref.at[4 * px + 2 * py + pc], send_sem=send_sems.at[k],
            recv_sem=recv_sems.at[k], device_id=to, device_id_type=MESH)

    sends = [copy(k, (x, y, c), p) for k, p in enumerate(peers)]
    mine = pltpu.make_async_copy(src_ref, out_ref.at[4 * x + 2 * y + c], local_sem)

    def start():
        for cp in sends:
            cp.start()
        mine.start()

    def wait():
        for k, p in enumerate(peers):
            copy(k, p, p).wait_recv()
        for cp in sends:
            cp.wait_send()
        mine.wait()

    return start, wait


def _sb_fwd(proj, gain, wo_shard):
    _, S, W = proj.shape
    H = W // HEAD_DIM
    B = min(SB_BLOCK, S)
    nq = S // B
    assert nq <= HEAD_DIM and nq % SB_PER_STEP == 0
    ns = nq // SB_PER_STEP

    def body(q_ref, k_ref, v_ref, g_ref, gain_ref, wo_ref, mix_ref, raw_ref, car_ref, woall_ref, kb_ref, vb_ref,
             send_sems, recv_sems, local_sem):
        hd, si = pl.program_id(0), pl.program_id(1)
        start_gather, wait_gather = _exchange_with_all(wo_ref, woall_ref, send_sems, recv_sems, local_sem)
        pl.when((hd == 0) & (si == 0))(start_gather)

        @pl.when(si == 0)
        def _():
            kb_ref[...] = k_ref[...].astype(BF16)
            vb_ref[...] = v_ref[...].astype(BF16)

        causal = _tri(B, "gt")
        upper = _ones_where(causal)
        lane = lax.broadcasted_iota(jnp.int32, (B, HEAD_DIM), 1)

        def block(qb, kb, carry, acc, saved, masked):
            rows = pl.ds(pl.multiple_of(kb * B, B), B)
            log_beta, lk, cs = _sb_scores(qb, kb_ref[rows, :], masked, causal, upper)
            a = jnp.exp(log_beta + cs + carry)
            if masked:
                a = jnp.where(causal, a, 0.0)
            acc = acc + _dot(a.astype(BF16), vb_ref[rows, :])
            return carry + jnp.sum(lk, axis=1, keepdims=True), acc, jnp.where(lane == kb, carry, saved)

        init = (jnp.zeros((B, 1), F32), jnp.zeros((B, HEAD_DIM), F32), jnp.full((B, HEAD_DIM), NOT_VISITED, F32))

        def live(st):
            return (st[0] >= 0) & (jnp.max(st[1]) >= EXP_IS_ZERO_BELOW)

        def finish(u, acc, saved):
            rows = slice(u * B, (u + 1) * B)
            raw_ref[rows, :] = acc
            car_ref[rows, :] = saved
            yn = acc * lax.rsqrt(jnp.mean(acc * acc, axis=-1, keepdims=True) + EPS)
            g = g_ref[rows, :]
            mix_ref[rows, :] = (g * _sigmoid(g) * (yn * gain_ref[...])).astype(BF16)

        def whole(first_step):
            heads = []
            for u in range(SB_PER_STEP):
                qi = si * SB_PER_STEP + u
                qb = q_ref[u * B:(u + 1) * B, :].astype(BF16)
                state = block(qb, qi, *init, True)
                if not (first_step and u == 0):
                    state = block(qb, qi - 1, *state, False)
                heads.append((qi, qb, state))
            for u, (qi, qb, state) in enumerate(heads):
                if not (first_step and u == 0):
                    state = lax.while_loop(
                        live, lambda st, qb=qb: (st[0] - 1,) + block(qb, st[0], st[1], st[2], st[3], False),
                        (qi - 2,) + state)[1:]
                finish(u, state[1], state[2])

        pl.when(si == 0)(lambda: whole(True))
        pl.when(si > 0)(lambda: whole(False))
        pl.when((hd == H - 1) & (si == ns - 1))(wait_gather)

    tq = SB_PER_STEP * B
    tile = lambda j: pl.BlockSpec((None, tq, HEAD_DIM), lambda h, i: (j, i, h))
    out_tile = pl.BlockSpec((tq, HEAD_DIM), lambda h, i: (i, h))
    hbm = pl.BlockSpec(memory_space=pltpu.HBM)
    return pl.pallas_call(
        body, name="sb_fwd",
        out_shape=(jax.ShapeDtypeStruct((S, W), BF16), jax.ShapeDtypeStruct((S, W), F32),
                   jax.ShapeDtypeStruct((S, W), F32), jax.ShapeDtypeStruct((N_DEV,) + wo_shard.shape, BF16)),
        grid=(H, ns),
        in_specs=[tile(4), _head_spec(S, 5), _head_spec(S, 6), tile(7),
                  pl.BlockSpec((1, HEAD_DIM), lambda h, i: (0, h)), hbm],
        out_specs=(out_tile, out_tile, out_tile, hbm),
        scratch_shapes=[pltpu.VMEM((S, HEAD_DIM), BF16), pltpu.VMEM((S, HEAD_DIM), BF16),
                        pltpu.SemaphoreType.DMA((N_DEV - 1,)), pltpu.SemaphoreType.DMA((N_DEV - 1,)),
                        pltpu.SemaphoreType.DMA],
        compiler_params=_params("arbitrary", "arbitrary"),
    )(proj, proj, proj, proj, gain, wo_shard)


def _out_proj_loss(mix_r, mix_s, w_out, x, tgt, gf):
    S, W = mix_r.shape
    D = x.shape[1]
    tm = min(256, S)

    def body(mr_ref, ms_ref, wo_ref, x_ref, t_ref, gf_ref, dx2_ref, dx2b_ref, dmix_ref, loss_ref, gfn_ref):
        @pl.when(pl.program_id(0) == 0)
        def _():
            loss_ref[...] = jnp.zeros_like(loss_ref)
            gfn_ref[...] = jnp.zeros_like(gfn_ref)

        gfv = gf_ref[...]
        x2 = x_ref[...] + (_dot(mr_ref[...], wo_ref[:W, :]) + _dot(ms_ref[...], wo_ref[W:, :]))
        r2 = lax.rsqrt(jnp.mean(x2 * x2, axis=-1, keepdims=True) + EPS)
        n = x2 * r2
        err = n * gfv - t_ref[...]
        loss_ref[...] += 0.5 * jnp.sum(jnp.mean(err * err, axis=-1, keepdims=True))
        dy = err * (1.0 / D)
        gfn_ref[...] += jnp.sum(dy * n, axis=0, keepdims=True)
        dn = dy * gfv
        dx2 = r2 * (dn - n * jnp.mean(dn * n, axis=-1, keepdims=True))
        dx2_ref[...] = dx2
        b = dx2.astype(BF16)
        dx2b_ref[...] = b
        dmix_ref[:, :W] = _dot_nt(b, wo_ref[:W, :])
        dmix_ref[:, W:] = _dot_nt(b, wo_ref[W:, :])

    row = lambda width: pl.BlockSpec((tm, width), lambda i: (i, 0))
    return pl.pallas_call(
        body, name="out_proj_loss",
        out_shape=(jax.ShapeDtypeStruct((S, D), F32), jax.ShapeDtypeStruct((S, D), BF16),
                   jax.ShapeDtypeStruct((S, 2 * W), F32), jax.ShapeDtypeStruct((SUBLANES, LANES), F32),
                   jax.ShapeDtypeStruct((1, D), F32)),
        grid=(S // tm,),
        in_specs=[row(W), row(W), pl.BlockSpec((2 * W, D), lambda i: (0, 0)), row(D), row(D),
                  pl.BlockSpec((1, D), lambda i: (0, 0))],
        out_specs=(row(D), row(D), row(2 * W), pl.BlockSpec((SUBLANES, LANES), lambda i: (0, 0)),
                   pl.BlockSpec((1, D), lambda i: (0, 0))),
        compiler_params=_params("arbitrary"),
    )(mix_r, mix_s, w_out, x, tgt, gf)


def _silu_bwd(g, dm, normed):
    sig = _sigmoid(g)
    return dm * (g * sig), dm * normed * (sig * (1.0 + g * (1.0 - sig)))


def _ret_bwd(proj, dmix, tabs, gn_gain, gn_bias, cos, sin, gwo):
    _, S, W = proj.shape
    H, nc = W // HEAD_DIM, S // CHUNK
    assert nc % RET_GROUP == 0
    ng = nc // RET_GROUP
    rows_per_group = RET_GROUP * CHUNK

    def body(q_ref, k_ref, v_ref, g_ref, dm_ref, dec_ref, xi_ref, ze_ref, gam_ref, gain_ref, bias_ref, cos_ref,
             sin_ref, gwo_ref, dp_ref, dgain_ref, dbias_ref, rino_ref, rs_ref, dstates_ref, send_sems, recv_sems):
        dec, xi, ze, gam = dec_ref[...], xi_ref[...], ze_ref[...], gam_ref[...]
        gain, bias = gain_ref[...], bias_ref[...]
        hd = pl.program_id(0)
        other_core = 1 - lax.axis_index("c")
        copies = _to_sibling_copies([gwo_ref.at[2 * k + other_core] for k in range(4)], rino_ref, send_sems, recv_sems)

        @pl.when(hd == 0)
        def _():
            for cp in copies:
                cp.start()

        def group_rows(i):
            return pl.ds(pl.multiple_of(i * rows_per_group, rows_per_group), rows_per_group)

        def fwd_group(i, state):
            rows = group_rows(i)
            kz = (_chunks(k_ref[rows, :]) * ze).astype(BF16)
            return _ret_states(kz, _chunks(v_ref[rows, :]).astype(BF16), gam, state,
                               rs_ref.at[pl.ds(i * RET_GROUP, RET_GROUP)])

        lax.fori_loop(0, ng, fwd_group, jnp.zeros((HEAD_DIM, HEAD_DIM), F32))

        flat = lambda a: a.reshape(rows_per_group, HEAD_DIM)

        def bwd_group(t, carry):
            dgain, dbias, dstate = carry
            i = ng - 1 - t
            rows = group_rows(i)
            q, k, g = _chunks(q_ref[rows, :]), _chunks(k_ref[rows, :]), _chunks(g_ref[rows, :])
            vb = _chunks(v_ref[rows, :]).astype(BF16)
            rb = rs_ref[pl.ds(i * RET_GROUP, RET_GROUP)].astype(BF16)
            out, (qb, kb, sb, qx) = _ret_group(q, k, vb, rb, dec, xi, ze)
            kz = (k * ze).astype(BF16)
            mu = jnp.mean(out, axis=-1, keepdims=True)
            d = out - mu
            rstd = lax.rsqrt(jnp.mean(d * d, axis=-1, keepdims=True) + EPS)
            yn = d * rstd
            dgn, dg = _silu_bwd(g, _chunks(dm_ref[rows, :]), yn * gain + bias)
            dgain = dgain + jnp.sum(flat(dgn * yn), axis=0, keepdims=True)
            dbias = dbias + jnp.sum(flat(dgn), axis=0, keepdims=True)
            dyn = dgn * gain
            do = rstd * (dyn - jnp.mean(dyn, axis=-1, keepdims=True)
                         - yn * jnp.mean(dyn * yn, axis=-1, keepdims=True))
            dob = do.astype(BF16)
            dkv = _bdot_tn(qx, dob)
            for u in reversed(range(RET_GROUP)):
                dstates_ref[u] = dstate
                dstate = gam * dstate + dkv[u]
            drb = dstates_ref[...].astype(BF16)
            dv = _bdot_tn(sb, dob) + _bdot(kz, drb)
            dsb = (_bdot_nt(dob, vb) * dec).astype(BF16)
            dq = _bdot(dsb, kb) + _bdot_nt(dob, rb) * xi
            dk = _bdot_tn(dsb, qb) + _bdot_nt(vb, drb) * ze
            cs, sn = cos_ref[rows, :], -sin_ref[rows, :]
            dp_ref[0, rows, :] = _rot(flat(dq), cs, sn).astype(BF16)
            dp_ref[1, rows, :] = (_rot(flat(dk), cs, sn) * (HEAD_DIM ** -0.5)).astype(BF16)
            dp_ref[2, rows, :] = flat(dv).astype(BF16)
            dp_ref[3, rows, :] = flat(dg).astype(BF16)
            return dgain, dbias, dstate

        zero = jnp.zeros((1, HEAD_DIM), F32)
        dgain, dbias, _ = lax.fori_loop(0, ng, bwd_group, (zero, zero, jnp.zeros((HEAD_DIM, HEAD_DIM), F32)))
        dgain_ref[...] = dgain
        dbias_ref[...] = dbias

        @pl.when(hd == H - 1)
        def _():
            for cp in copies:
                cp.wait_recv()
            for cp in copies:
                cp.wait_send()

    vec = pl.BlockSpec((1, HEAD_DIM), lambda h: (0, h))
    full = pl.BlockSpec((S, HEAD_DIM), lambda h: (0, 0))
    hbm = pl.BlockSpec(memory_space=pltpu.HBM)
    return pl.pallas_call(
        body, name="ret_bwd",
        out_shape=(jax.ShapeDtypeStruct((4, S, W), BF16), jax.ShapeDtypeStruct((1, W), F32),
                   jax.ShapeDtypeStruct((1, W), F32), jax.ShapeDtypeStruct((4,) + gwo.shape[1:], gwo.dtype)),
        grid=(H,),
        in_specs=[_head_spec(S, 0), _head_spec(S, 1), _head_spec(S, 2), _head_spec(S, 3),
                  pl.BlockSpec((S, HEAD_DIM), lambda h: (0, h))] + _table_specs() + [vec, vec, full, full, hbm],
        out_specs=(pl.BlockSpec((4, S, HEAD_DIM), lambda h: (0, 0, h)), vec, vec, hbm),
        scratch_shapes=[pltpu.VMEM((nc, HEAD_DIM, HEAD_DIM), F32), pltpu.VMEM((RET_GROUP, HEAD_DIM, HEAD_DIM), F32),
                        pltpu.SemaphoreType.DMA((4,)), pltpu.SemaphoreType.DMA((4,))],
        compiler_params=_params("arbitrary"),
    )(proj, proj, proj, proj, dmix, *tabs, gn_gain, gn_bias, cos, sin, gwo)


def _sb_bwd(proj, raw, carries, dmix, gain, chip_sums_o):
    _, S, W = proj.shape
    H = W // HEAD_DIM
    B = min(SB_BLOCK, S)
    nq = S // B
    ns = nq // SB_PER_STEP

    def body(q_ref, k_ref, v_ref, g_ref, raw_ref, car_ref, dm_ref, gain_ref, so_ref, dp_ref, dgain_ref, ro_ref,
             kb_ref, vb_ref, dk_ref, dv_ref, send_sems, recv_sems):
        hd, si = pl.program_id(0), pl.program_id(1)
        start_exchange, wait_exchange = _exchange_chip_sums((so_ref,), (ro_ref,), send_sems, recv_sems)
        pl.when((hd == 0) & (si == 0))(start_exchange)

        @pl.when(si == 0)
        def _():
            kb_ref[...] = k_ref[...].astype(BF16)
            vb_ref[...] = v_ref[...].astype(BF16)
            dk_ref[...] = jnp.zeros_like(dk_ref)
            dv_ref[...] = jnp.zeros_like(dv_ref)
            dgain_ref[...] = jnp.zeros_like(dgain_ref)

        causal = _tri(B, "gt")
        upper = _ones_where(causal)
        before = _ones_where(_tri(B, "lt"))
        lane = lax.broadcasted_iota(jnp.int32, (B, HEAD_DIM), 1)
        gain_v = gain_ref[...]

        def prologue(u):
            qi = si * SB_PER_STEP + u
            rows = slice(u * B, (u + 1) * B)
            o = raw_ref[rows, :]
            rstd = lax.rsqrt(jnp.mean(o * o, axis=-1, keepdims=True) + EPS)
            yn = o * rstd
            dnrm, dg = _silu_bwd(g_ref[rows, :], dm_ref[rows, :], yn * gain_v)
            dp_ref[3, pl.ds(pl.multiple_of(qi * B, B), B), :] = dg.astype(BF16)
            dgain_ref[...] += jnp.sum(dnrm * yn, axis=0, keepdims=True)
            dyn = dnrm * gain_v
            do = rstd * (dyn - yn * jnp.mean(dyn * yn, axis=-1, keepdims=True))
            return qi, q_ref[rows, :].astype(BF16), do.astype(BF16), car_ref[rows, :]

        def block(ctx, kb, carry_g, dq, masked):
            _, qb, dob, saved = ctx
            rows = pl.ds(pl.multiple_of(kb * B, B), B)
            kk, vv = kb_ref[rows, :], vb_ref[rows, :]
            log_beta, _, cs = _sb_scores(qb, kk, masked, causal, upper)
            carry_lk = jnp.sum(jnp.where(lane == kb, saved, 0.0), axis=1, keepdims=True)
            a = jnp.exp(log_beta + cs + carry_lk)
            if masked:
                a = jnp.where(causal, a, 0.0)
            gmat = _dot_nt(dob, vv) * a
            dv_ref[rows, :] += _dot_tn(a.astype(BF16), dob)
            hi = gmat.astype(BF16)
            lo = (gmat - hi.astype(F32)).astype(BF16)
            dlk = carry_g + (_dot(hi, before) + _dot(lo, before))
            beta = jnp.exp(log_beta)
            dz = (gmat * (1.0 - beta) - dlk * beta) * (HEAD_DIM ** -0.5)
            if masked:
                dz = jnp.where(causal, dz, 0.0)
            dzb = dz.astype(BF16)
            dk_ref[rows, :] += _dot_tn(dzb, qb)
            return carry_g + jnp.sum(gmat, axis=1, keepdims=True), dq + _dot(dzb, kk)

        init = (jnp.zeros((B, 1), F32), jnp.zeros((B, HEAD_DIM), F32))

        def whole(first_step):
            ctxs = [prologue(u) for u in range(SB_PER_STEP)]
            states = []
            for u, ctx in enumerate(ctxs):
                state = init
                if not (first_step and u == 0):
                    visited = jnp.max(ctx[3], axis=0, keepdims=True) >= EXP_IS_ZERO_BELOW
                    first = jnp.min(jnp.where(visited, lane[:1, :], ctx[0]))
                    state = lax.fori_loop(first, ctx[0] - 1,
                                          lambda i, st, ctx=ctx: block(ctx, i, st[0], st[1], False), state)
                states.append(state)
            for u, (ctx, state) in enumerate(zip(ctxs, states)):
                if not (first_step and u == 0):
                    state = block(ctx, ctx[0] - 1, *state, False)
                state = block(ctx, ctx[0], *state, True)
                dp_ref[0, pl.ds(pl.multiple_of(ctx[0] * B, B), B), :] = state[1].astype(BF16)

        pl.when(si == 0)(lambda: whole(True))
        pl.when(si > 0)(lambda: whole(False))

        @pl.when(si == ns - 1)
        def _():
            dp_ref[1] = dk_ref[...].astype(BF16)
            dp_ref[2] = dv_ref[...].astype(BF16)

        pl.when((hd == H - 1) & (si == ns - 1))(wait_exchange)

    tq = SB_PER_STEP * B
    tile = lambda j: pl.BlockSpec((None, tq, HEAD_DIM), lambda h, i: (j, i, h))
    vec = pl.BlockSpec((1, HEAD_DIM), lambda h, i: (0, h))
    hbm = pl.BlockSpec(memory_space=pltpu.HBM)
    return pl.pallas_call(
        body, name="sb_bwd",
        out_shape=(jax.ShapeDtypeStruct((4, S, W), BF16), jax.ShapeDtypeStruct((1, W), F32),
                   jax.ShapeDtypeStruct((3,) + chip_sums_o.shape[1:], chip_sums_o.dtype)),
        grid=(H, ns),
        in_specs=[tile(4), _head_spec(S, 5), _head_spec(S, 6), tile(7),
                  pl.BlockSpec((tq, HEAD_DIM), lambda h, i: (i, h)),
                  pl.BlockSpec((tq, HEAD_DIM), lambda h, i: (i, h)),
                  pl.BlockSpec((tq, HEAD_DIM), lambda h, i: (i, H + h)), vec, hbm],
        out_specs=(pl.BlockSpec((4, S, HEAD_DIM), lambda h, i: (0, 0, h)), vec, hbm),
        scratch_shapes=[pltpu.VMEM((S, HEAD_DIM), BF16), pltpu.VMEM((S, HEAD_DIM), BF16),
                        pltpu.VMEM((S, HEAD_DIM), F32), pltpu.VMEM((S, HEAD_DIM), F32),
                        pltpu.SemaphoreType.DMA((1, 3)), pltpu.SemaphoreType.DMA((1, 3))],
        compiler_params=_params("arbitrary", "arbitrary"),
    )(proj, proj, proj, proj, raw, carries, dmix, gain, chip_sums_o)


def _grad_w_in_half(ht, dpr, dps, core, name, to_sibling=None):
    D, S = ht.shape
    _, _, W = dpr.shape
    tmm = min(512, D)
    nm = D // tmm

    def body(core_ref, ht_ref, r_ref, s_ref, *rest):
        o_ref = rest[1] if to_sibling is not None else rest[0]
        q, m = pl.program_id(0), pl.program_id(1)
        if to_sibling is not None:
            ga_ref, _, rin_ref, send_sems, recv_sems = rest
            copies = _to_sibling_copies([ga_ref.at[k] for k in range(4)], rin_ref, send_sems, recv_sems)

            @pl.when((q == 0) & (m == 0))
            def _():
                for cp in copies:
                    cp.start()

        @pl.when(q < 2)
        def _():
            o_ref[...] = _dot(ht_ref[...], r_ref[...])

        @pl.when(q >= 2)
        def _():
            o_ref[...] = _dot(ht_ref[...], s_ref[...])

        if to_sibling is not None:
            @pl.when((q == 3) & (m == nm - 1))
            def _():
                for cp in copies:
                    cp.wait_recv()
                for cp in copies:
                    cp.wait_send()

    hbm = pl.BlockSpec(memory_space=pltpu.HBM)
    gw_shape = jax.ShapeDtypeStruct((4, D, W), F32)
    out_shape, out_specs, extra_in, scratch = (gw_shape,), (pl.BlockSpec((None, tmm, W), lambda q, m, core: (q, m, 0)),), [], []
    if to_sibling is not None:
        out_shape += (gw_shape,)
        out_specs += (hbm,)
        extra_in = [hbm]
        scratch = [pltpu.SemaphoreType.DMA((4,)), pltpu.SemaphoreType.DMA((4,))]
    return pl.pallas_call(
        body, name=name, out_shape=out_shape,
        grid_spec=pltpu.PrefetchScalarGridSpec(
            num_scalar_prefetch=1, grid=(4, nm),
            in_specs=[pl.BlockSpec((tmm, S), lambda q, m, core: (m, 0)),
                      pl.BlockSpec((None, S, W), lambda q, m, core: (jnp.minimum(2 * q + core[0], 3), 0, 0)),
                      pl.BlockSpec((None, S, W), lambda q, m, core: (jnp.maximum(2 * q + core[0] - 4, 0), 0, 0))]
            + extra_in,
            out_specs=out_specs, scratch_shapes=scratch),
        compiler_params=_params("arbitrary", "arbitrary"),
    )(core, ht, dpr, dps, *(() if to_sibling is None else (to_sibling,)))


def _grad_w_out(mix_r, mix_s, dx2b):
    S, W = mix_r.shape
    D = dx2b.shape[1]
    tmm = min(512, W)
    tk = min(1024, S)

    def body(r_ref, s_ref, b_ref, o_ref):
        j, kk = pl.program_id(0), pl.program_id(2)

        def acc(a_ref):
            part = _dot_tn(a_ref[...], b_ref[...])

            @pl.when(kk == 0)
            def _():
                o_ref[...] = part

            @pl.when(kk > 0)
            def _():
                o_ref[...] += part

        pl.when(j == 0)(lambda: acc(r_ref))
        pl.when(j == 1)(lambda: acc(s_ref))

    return pl.pallas_call(
        body, name="grad_w_out", out_shape=jax.ShapeDtypeStruct((2, W, D), F32), grid=(2, W // tmm, S // tk),
        in_specs=[pl.BlockSpec((tk, tmm), lambda j, m, k: (k, m)),
                  pl.BlockSpec((tk, tmm), lambda j, m, k: (k, m)),
                  pl.BlockSpec((tk, D), lambda j, m, k: (k, 0))],
        out_specs=pl.BlockSpec((None, tmm, D), lambda j, m, k: (j, m, 0)),
        compiler_params=_params("parallel", "parallel", "arbitrary"),
    )(mix_r, mix_s, dx2b)


def _dh_matmul(dpr, dps, w_all, chip_sums):
    _, S, W = dpr.shape
    D = w_all.shape[1]
    tm = min(1024, S)
    ni = S // tm

    def body(r_ref, s_ref, w_ref, sa_ref, dh_ref, ra_ref, send_sems, recv_sems):
        i, j = pl.program_id(0), pl.program_id(1)
        start_exchange, wait_exchange = _exchange_chip_sums((sa_ref,), (ra_ref,), send_sems, recv_sems)
        pl.when((i == 0) & (j == 0))(start_exchange)

        def acc(b_ref):
            part = _dot_nt(b_ref[...], w_ref[...])

            @pl.when(j == 0)
            def _():
                dh_ref[...] = part

            @pl.when(j > 0)
            def _():
                dh_ref[...] += part

        pl.when(j < 4)(lambda: acc(r_ref))
        pl.when(j >= 4)(lambda: acc(s_ref))
        pl.when((i == ni - 1) & (j == 7))(wait_exchange)

    hbm = pl.BlockSpec(memory_space=pltpu.HBM)
    return pl.pallas_call(
        body, name="dh_matmul",
        out_shape=(jax.ShapeDtypeStruct((S, D), F32), jax.ShapeDtypeStruct((3,) + chip_sums.shape[1:], chip_sums.dtype)),
        grid=(ni, 8),
        in_specs=[pl.BlockSpec((None, tm, W), lambda i, j: (jnp.minimum(j, 3), i, 0)),
                  pl.BlockSpec((None, tm, W), lambda i, j: (jnp.maximum(j - 4, 0), i, 0)),
                  pl.BlockSpec((None, D, W), lambda i, j: (j, 0, 0)), hbm],
        out_specs=(pl.BlockSpec((tm, D), lambda i, j: (i, 0)), hbm),
        scratch_shapes=[pltpu.SemaphoreType.DMA((1, 3)), pltpu.SemaphoreType.DMA((1, 3))],
        compiler_params=_params("arbitrary", "arbitrary"),
    )(dpr, dps, w_all, chip_sums)


def _norm_bwd(x, dx2, dh, gain):
    S, D = x.shape
    tm = min(256, S)

    def body(x_ref, dx2_ref, dh_ref, g_ref, gx_ref, dgain_ref):
        @pl.when(pl.program_id(0) == 0)
        def _():
            dgain_ref[...] = jnp.zeros_like(dgain_ref)

        xv, dh_v = x_ref[...], dh_ref[...]
        r1 = lax.rsqrt(jnp.mean(xv * xv, axis=-1, keepdims=True) + EPS)
        n = xv * r1
        dgain_ref[...] += jnp.sum(dh_v * n, axis=0, keepdims=True)
        dn = dh_v * g_ref[...]
        gx_ref[...] = dx2_ref[...] + r1 * (dn - n * jnp.mean(dn * n, axis=-1, keepdims=True))

    row = pl.BlockSpec((tm, D), lambda i: (i, 0))
    one = pl.BlockSpec((1, D), lambda i: (0, 0))
    return pl.pallas_call(
        body, name="norm_bwd", out_shape=(jax.ShapeDtypeStruct((S, D), F32), jax.ShapeDtypeStruct((1, D), F32)),
        grid=(S // tm,), in_specs=[row, row, row, one], out_specs=(row, one),
        compiler_params=_params("arbitrary"),
    )(x, dx2, dh, gain)


def _own_block(gw, pos, q):
    return q if gw.shape[0] == 4 else 2 * q + pos[0]


def _rs_local_sum(gw, rin, pos):
    _, R, C = gw.shape
    tr = min(256, R)
    other = lambda k, pos: (pos[1] + 1 + k) % 4

    def body(pos_ref, a_ref, b_ref, o_ref):
        o_ref[...] = (a_ref[...] + b_ref[...]).astype(BF16)

    return pl.pallas_call(
        body, name="rs_local_sum", out_shape=jax.ShapeDtypeStruct((4, R, C), BF16),
        grid_spec=pltpu.PrefetchScalarGridSpec(
            num_scalar_prefetch=1, grid=(3, R // tr),
            in_specs=[pl.BlockSpec((None, tr, C), lambda k, i, pos: (_own_block(gw, pos, other(k, pos)), i, 0)),
                      pl.BlockSpec((None, tr, C), lambda k, i, pos: (other(k, pos), i, 0))],
            out_specs=pl.BlockSpec((None, tr, C), lambda k, i, pos: (other(k, pos), i, 0))),
        compiler_params=_params("parallel", "parallel"),
    )(pos, gw, rin)


def _adamw(w, g, m, v):
    m2 = ADAM_B1 * m + (1.0 - ADAM_B1) * g
    v2 = ADAM_B2 * v + (1.0 - ADAM_B2) * (g * g)
    m_hat = m2 / (1.0 - ADAM_B1 ** ADAM_STEP)
    v_hat = v2 / (1.0 - ADAM_B2 ** ADAM_STEP)
    delta = -ADAM_LR * (m_hat / (jnp.sqrt(v_hat) + ADAM_EPS) + ADAM_WD * w)
    return delta, m2, v2


def _adamw_shard(gw, rin, rb, w, m, v, pos):
    _, R, C = gw.shape
    tr = min(256, R)

    def body(pos_ref, a_ref, b_ref, rb_ref, w_ref, m_ref, v_ref, g_ref, d_ref, m2_ref, v2_ref):
        g = a_ref[...] + b_ref[...]
        for k in range(3):
            g = g + rb_ref[k].astype(F32)
        g_ref[...] = g
        d_ref[...], m2_ref[...], v2_ref[...] = _adamw(w_ref[...], g, m_ref[...], v_ref[...])

    plain = pl.BlockSpec((tr, C), lambda i, pos: (i, 0))
    shape = jax.ShapeDtypeStruct((R, C), F32)
    return pl.pallas_call(
        body, name="adamw_shard", out_shape=(shape,) * 4,
        grid_spec=pltpu.PrefetchScalarGridSpec(
            num_scalar_prefetch=1, grid=(R // tr,),
            in_specs=[pl.BlockSpec((None, tr, C), lambda i, pos: (_own_block(gw, pos, pos[1]), i, 0)),
                      pl.BlockSpec((None, tr, C), lambda i, pos: (pos[1], i, 0)),
                      pl.BlockSpec((3, tr, C), lambda i, pos: (0, i, 0)), plain, plain, plain],
            out_specs=(plain,) * 4),
        compiler_params=_params("parallel"),
    )(pos, gw, rin, rb, w, m, v)


def _adamw_small(parts, w, m, v):
    _, rows, n = parts.shape

    def body(p_ref, w_ref, m_ref, v_ref, g_ref, d_ref, m2_ref, v2_ref):
        g = p_ref[0]
        for d in range(1, N_DEV):
            g = g + p_ref[d]
        g_ref[...] = g
        d_ref[...], m2_ref[...], v2_ref[...] = _adamw(w_ref[...], g, m_ref[...], v_ref[...])

    shape = jax.ShapeDtypeStruct((rows, n), F32)
    return pl.pallas_call(body, name="adamw_small", out_shape=(shape,) * 4)(parts, w, m, v)


def _rope_tables(S):
    half = HEAD_DIM // 2
    inv = ROPE_THETA ** (-jnp.arange(half, dtype=F32) / half)
    ang = jnp.arange(S, dtype=F32)[:, None] * inv[None, :]
    cos, sin = jnp.cos(ang), jnp.sin(ang)
    return jnp.concatenate([cos, cos], axis=1), jnp.concatenate([-sin, sin], axis=1)


def _retention_tables(H):
    lg = jnp.log1p(-jnp.exp2(-5.0 - jnp.arange(H, dtype=F32)))
    n = jnp.arange(CHUNK, dtype=F32)
    rel = n[:, None] - n[None, :]
    decay = jnp.where(rel >= 0, jnp.exp(lg[:, None, None] * jnp.maximum(rel, 0.0)), 0.0)
    shape = (H, CHUNK, HEAD_DIM)
    xi = jnp.broadcast_to(jnp.exp(lg[:, None] * (n + 1.0))[:, :, None], shape)
    zeta = jnp.broadcast_to(jnp.exp(lg[:, None] * (CHUNK - 1.0 - n))[:, :, None], shape)
    gamma_c = jnp.broadcast_to(jnp.exp(lg * CHUNK)[:, None, None], shape)
    return decay, xi, zeta, gamma_c


def _pack_small(parts):
    flat = []
    for p in parts:
        p = p.reshape(-1)
        flat.append(jnp.pad(p, (0, -p.shape[0] % LANES)))
    flat = jnp.concatenate(flat)
    return jnp.pad(flat, (0, SMALL_N - flat.shape[0])).reshape(SUBLANES, SMALL_N // SUBLANES)


def _unpack_small(packed, shapes):
    flat = packed.reshape(-1)
    out, at = [], 0
    for shp in shapes:
        size = 1
        for s in shp:
            size *= s
        out.append(flat[at:at + size].reshape(shp))
        at += size + (-size % LANES)
    return out


def kernel(x, norm_gain, w_in, ret_gn_gain, ret_gn_bias, sb_norm_gain, w_out, final_norm_gain, loss_target, m_norm_gain, m_w_in, m_ret_gn_gain, m_ret_gn_bias, m_sb_norm_gain, m_w_out, m_final_norm_gain, v_norm_gain, v_w_in, v_ret_gn_gain, v_ret_gn_bias, v_sb_norm_gain, v_w_out, v_final_norm_gain):
    S, D = x.shape[1], x.shape[2]
    W = w_in.shape[2]
    wo_rows = w_out.shape[1]
    H = W // HEAD_DIM
    xs, tgt = x[0], loss_target[0]
    mx, my, mc = _mesh_pos()
    pos = jnp.stack([mc, 2 * mx + my]).astype(jnp.int32)

    cos, sin = _rope_tables(S)
    tabs = _retention_tables(H)

    proj, w_all, ht = _in_proj_gather(xs, norm_gain, w_in[0].astype(BF16), cos, sin, _gather_order())
    mix_r = _ret_fwd(proj, tabs, ret_gn_gain, ret_gn_bias)
    mix_s, raw_s, carries, wo_all = _sb_fwd(proj, sb_norm_gain, w_out[0].astype(BF16))
    wo_full = wo_all.reshape(N_DEV * wo_rows, D)
    dx2, dx2b, dmix, loss_p, d_gf = _out_proj_loss(mix_r, mix_s, wo_full, xs, tgt, final_norm_gain[None])

    gwo = _grad_w_out(mix_r, mix_s, dx2b).reshape(N_DEV, wo_rows, D)
    dpr, d_rgain, d_rbias, rino = _ret_bwd(proj, dmix, tabs, ret_gn_gain, ret_gn_bias, cos, sin, gwo)
    dps, d_sgain, rbo = _sb_bwd(proj, raw_s, carries, dmix, sb_norm_gain, _rs_local_sum(gwo, rino, pos))
    gw_sibling, = _grad_w_in_half(ht, dpr, dps, (1 - mc).reshape(1).astype(jnp.int32), "grad_w_in_sibling")
    gw, rin = _grad_w_in_half(ht, dpr, dps, mc.reshape(1).astype(jnp.int32), "grad_w_in_own", to_sibling=gw_sibling)
    dh, rb = _dh_matmul(dpr, dps, w_all, _rs_local_sum(gw, rin, pos))
    grad_x, d_gain = _norm_bwd(xs, dx2, dh, norm_gain)
    g_in, d_in, m_in, v_in = _adamw_shard(gw, rin, rb, w_in[0], m_w_in[0], v_w_in[0], pos)
    g_out, d_out, m_out, v_out = _adamw_shard(gwo, rino, rbo, w_out[0], m_w_out[0], v_w_out[0], pos)

    small_w = [norm_gain, ret_gn_gain, ret_gn_bias, sb_norm_gain, final_norm_gain]
    small_m = [m_norm_gain, m_ret_gn_gain, m_ret_gn_bias, m_sb_norm_gain, m_final_norm_gain]
    small_v = [v_norm_gain, v_ret_gn_gain, v_ret_gn_bias, v_sb_norm_gain, v_final_norm_gain]
    shapes = [()] + [w.shape for w in small_w]
    zero = jnp.zeros((), F32)
    parts = _small_all_gather(_pack_small([loss_p[0, 0], d_gain, d_rgain, d_rbias, d_sgain, d_gf]))
    packed = _adamw_small(parts, _pack_small([zero] + small_w), _pack_small([zero] + small_m),
                          _pack_small([zero] + small_v))
    g_s, d_s, m_s, v_s = (_unpack_small(p, shapes) for p in packed)

    grads = [g_s[1], g_in[None], g_s[2], g_s[3], g_s[4], g_out[None], g_s[5]]
    deltas = [d_s[1], d_in[None], d_s[2], d_s[3], d_s[4], d_out[None], d_s[5]]
    new_m = [m_s[1], m_in[None], m_s[2], m_s[3], m_s[4], m_out[None], m_s[5]]
    new_v = [v_s[1], v_in[None], v_s[2], v_s[3], v_s[4], v_out[None], v_s[5]]
    return (g_s[0], grad_x[None], *grads, *deltas, *new_m, *new_v)
```

```python
import functools

import jax
import jax.numpy as jnp
from jax import lax
from jax.experimental import pallas as pl
from jax.experimental.pallas import tpu as pltpu

F32 = jnp.float32
BF16 = jnp.bfloat16

HEAD_DIM = 128
CHUNK = 128
RET_GROUP = 16
ROPE_THETA = 10000.0
EPS = 1e-6
ADAM_LR = 0.001
ADAM_B1 = 0.9
ADAM_B2 = 0.999
ADAM_EPS = 1e-08
ADAM_WD = 0.01
ADAM_STEP = 10

N_DEV = 8
LANES = 128
SUBLANES = 8
VMEM_LIMIT = 56 * 1024 * 1024
SB_BLOCK = 256
SB_PER_STEP = 4
SMALL_N = 8192
EXP_IS_ZERO_BELOW = -104.0
NOT_VISITED = -1e30
MESH = pl.DeviceIdType.MESH

NT = (((1,), (1,)), ((), ()))
TN = (((0,), (0,)), ((), ()))


def _params(*sem):
    return pltpu.CompilerParams(dimension_semantics=sem if sem else None, vmem_limit_bytes=VMEM_LIMIT)


def _dot(a, b):
    return jnp.dot(a, b, preferred_element_type=F32)


def _dot_nt(a, b):
    return lax.dot_general(a, b, NT, preferred_element_type=F32)


def _dot_tn(a, b):
    return lax.dot_general(a, b, TN, preferred_element_type=F32)


def _sigmoid(g):
    return 1.0 / (1.0 + jnp.exp(-g))


def _rot(a, cos, sin_signed):
    return a * cos + pltpu.roll(a, HEAD_DIM // 2, 1) * sin_signed


def _mesh_pos():
    return lax.axis_index("x"), lax.axis_index("y"), lax.axis_index("c")


def _to_sibling_copies(blocks, out_ref, send_sems, recv_sems):
    x, y, c = _mesh_pos()
    return [pltpu.make_async_remote_copy(
        src_ref=block, dst_ref=out_ref.at[k], send_sem=send_sems.at[k], recv_sem=recv_sems.at[k],
        device_id=(x, y, 1 - c), device_id_type=MESH) for k, block in enumerate(blocks)]


def _exchange_chip_sums(srcs, outs, send_sems, recv_sems):
    x, y, c = _mesh_pos()
    copies = []
    for arr, (src, out) in enumerate(zip(srcs, outs)):
        for k in range(1, 4):
            px = 1 - x if k & 2 else x
            py = 1 - y if k & 1 else y
            copies.append(pltpu.make_async_remote_copy(
                src_ref=src.at[2 * px + py], dst_ref=out.at[k - 1],
                send_sem=send_sems.at[arr, k - 1], recv_sem=recv_sems.at[arr, k - 1],
                device_id=(px, py, c), device_id_type=MESH))

    def start():
        for cp in copies:
            cp.start()

    def wait():
        for cp in copies:
            cp.wait_recv()
        for cp in copies:
            cp.wait_send()

    return start, wait


def _small_all_gather(small):
    rows, n = small.shape

    def body(s_ref, o_ref, send_sems, recv_sems, local_sem):
        start, wait = _exchange_with_all(s_ref, o_ref, send_sems, recv_sems, local_sem)
        start()
        wait()

    vmem = pl.BlockSpec(memory_space=pltpu.VMEM)
    return pl.pallas_call(
        body, name="small_all_gather",
        out_shape=jax.ShapeDtypeStruct((N_DEV, rows, n), small.dtype),
        in_specs=[vmem], out_specs=vmem,
        scratch_shapes=[pltpu.SemaphoreType.DMA((N_DEV - 1,)), pltpu.SemaphoreType.DMA((N_DEV - 1,)),
                        pltpu.SemaphoreType.DMA],
    )(small)


GATHER_SPLIT = 2
GATHER_STEPS = ([("own", 0, p) for p in range(GATHER_SPLIT)] + [("sibling", 0, p) for p in range(GATHER_SPLIT)]
                + [step for p in range(GATHER_SPLIT) for step in
                   (("ici", 0, p), ("ici", 1, p), ("passed", 0, p), ("passed", 1, p))]
                + [step for p in range(GATHER_SPLIT) for step in (("ici", 2, p), ("passed", 2, p))])


def _via_x(p):
    return p % 2 == 0


def _gather_order():
    x, y, c = _mesh_pos()
    chips = [(1 - x, y), (x, 1 - y), (1 - x, 1 - y)]
    owner = {"own": lambda j: (x, y, c), "sibling": lambda j: (x, y, 1 - c),
             "ici": lambda j: (*chips[j], c), "passed": lambda j: (*chips[j], 1 - c)}
    blocks = [4 * px + 2 * py + pc for px, py, pc in (owner[kind](j) for kind, j, _ in GATHER_STEPS)]
    return (jnp.stack(blocks).astype(jnp.int32), jnp.array([p for _, _, p in GATHER_STEPS], jnp.int32))


def _in_proj_gather(x, gain, w_shard, cos, sin, order):
    S, D = x.shape
    W = w_shard.shape[1]
    wp = W // GATHER_SPLIT
    tm = min(1024, S)
    ni = S // tm
    n_steps = len(GATHER_STEPS)

    def body(blk_ref, piece_ref, x_ref, g_ref, w_ref, cos_ref, sin_ref, o_ref, wall_ref, ht_ref, h_scr, wbuf,
             send_sems, recv_sems, local_sem, load_sem):
        step, i = pl.program_id(0), pl.program_id(1)
        mx, my, c = _mesh_pos()
        me, sibling = (mx, my, c), (mx, my, 1 - c)
        chips = [(1 - mx, my), (mx, 1 - my), (1 - mx, 1 - my)]

        def piece_of(dev, p):
            px, py, pc = dev
            return wall_ref.at[4 * px + 2 * py + pc, :, pl.ds(p * wp, wp)]

        def copy(k, p, block, to, own=False):
            dst = piece_of(block, p)
            return pltpu.make_async_remote_copy(
                src_ref=w_ref.at[:, pl.ds(p * wp, wp)] if own else dst, dst_ref=dst,
                send_sem=send_sems.at[k, p], recv_sem=recv_sems.at[k, p], device_id=to, device_id_type=MESH)

        def load(src):
            cp = pltpu.make_async_copy(src, wbuf, load_sem)
            cp.start()
            cp.wait()

        pieces = range(GATHER_SPLIT)
        first = [cp for p in pieces for cp in
                 [copy(0, p, me, sibling, own=True)] + [copy(1 + j, p, me, (*chips[j], c), own=True) for j in (0, 1)]]
        passed = {(j, p): copy(4 + j, p, (*chip, c), sibling) for j, chip in enumerate(chips) for p in pieces}
        onward = {p: copy(3, p, (*chips[0 if _via_x(p) else 1], c), (*chips[1 if _via_x(p) else 0], c)) for p in pieces}
        mine = pltpu.make_async_copy(w_ref, wall_ref.at[4 * mx + 2 * my + c], local_sem)

        @pl.when(i == 0)
        def _():
            for s, (kind, j, p) in enumerate(GATHER_STEPS):
                @pl.when(step == s)
                def _(s=s, kind=kind, j=j, p=p):
                    if s == 0:
                        for cp in first:
                            cp.start()
                        mine.start()
                    if kind == "own":
                        load(w_ref.at[:, pl.ds(p * wp, wp)])
                    elif kind == "sibling":
                        copy(0, p, sibling, me).wait_recv()
                        load(piece_of(sibling, p))
                    elif kind == "ici":
                        copy(1 + j, p, (*chips[j], c), me).wait_recv()
                        if j == (0 if _via_x(p) else 1):
                            onward[p].start()
                        passed[j, p].start()
                        load(piece_of((*chips[j], c), p))
                    else:
                        copy(4 + j, p, (*chips[j], 1 - c), me).wait_recv()
                        load(piece_of((*chips[j], 1 - c), p))

        rows = pl.ds(pl.multiple_of(i * tm, tm), tm)

        @pl.when(step == 0)
        def _():
            xv = x_ref[...]
            r = lax.rsqrt(jnp.mean(xv * xv, axis=-1, keepdims=True) + EPS)
            hv = xv * r * g_ref[...]
            h_scr[rows, :] = hv.astype(BF16)
            ht_ref[...] = hv.T.astype(BF16)

        acc = _dot(h_scr[rows, :], wbuf[...])
        b = blk_ref[step]

        @pl.when(b >= 2)
        def _():
            o_ref[...] = acc

        @pl.when(b < 2)
        def _():
            scale = jnp.where(b == 1, HEAD_DIM ** -0.5, 1.0).astype(F32)
            cs, sn = cos_ref[...], sin_ref[...]
            for hh in range(wp // HEAD_DIM):
                cols = slice(hh * HEAD_DIM, (hh + 1) * HEAD_DIM)
                o_ref[:, cols] = _rot(acc[:, cols], cs, sn) * scale

        @pl.when((step == n_steps - 1) & (i == ni - 1))
        def _():
            for cp in first + list(passed.values()) + list(onward.values()):
                cp.wait_send()
            mine.wait()

    hbm = pl.BlockSpec(memory_space=pltpu.HBM)
    rope = pl.BlockSpec((tm, HEAD_DIM), lambda s, i, blk, piece: (i, 0))
    first_pass = lambda s, i: jnp.where(s == 0, i, ni - 1)
    return pl.pallas_call(
        body, name="in_proj_gather",
        out_shape=(jax.ShapeDtypeStruct((N_DEV, S, W), F32), jax.ShapeDtypeStruct((N_DEV, D, W), BF16),
                   jax.ShapeDtypeStruct((D, S), BF16)),
        grid_spec=pltpu.PrefetchScalarGridSpec(
            num_scalar_prefetch=2, grid=(n_steps, ni),
            in_specs=[pl.BlockSpec((tm, D), lambda s, i, blk, piece: (first_pass(s, i), 0)),
                      pl.BlockSpec((1, D), lambda s, i, blk, piece: (0, 0)), hbm, rope, rope],
            out_specs=(pl.BlockSpec((None, tm, wp), lambda s, i, blk, piece: (blk[s], i, piece[s])), hbm,
                       pl.BlockSpec((D, tm), lambda s, i, blk, piece: (0, first_pass(s, i)))),
            scratch_shapes=[pltpu.VMEM((S, D), BF16), pltpu.VMEM((D, wp), BF16),
                            pltpu.SemaphoreType.DMA((7, GATHER_SPLIT)), pltpu.SemaphoreType.DMA((7, GATHER_SPLIT)),
                            pltpu.SemaphoreType.DMA, pltpu.SemaphoreType.DMA]),
        compiler_params=_params("arbitrary", "arbitrary"),
    )(*order, x, gain, w_shard, cos, sin)


def _head_spec(S, j):
    return pl.BlockSpec((None, S, HEAD_DIM), lambda h, *_: (j, 0, h))


def _bdot(a, b):
    return lax.dot_general(a, b, (((2,), (1,)), ((0,), (0,))), preferred_element_type=F32)


def _bdot_nt(a, b):
    return lax.dot_general(a, b, (((2,), (2,)), ((0,), (0,))), preferred_element_type=F32)


def _bdot_tn(a, b):
    return lax.dot_general(a, b, (((1,), (1,)), ((0,), (0,))), preferred_element_type=F32)


def _chunks(a):
    return a.reshape(a.shape[0] // CHUNK, CHUNK, a.shape[1])


def _ret_group(q, k, vb, states_b, dec, xi, ze):
    qb, kb = q.astype(BF16), k.astype(BF16)
    sb = (_bdot_nt(qb, kb) * dec).astype(BF16)
    qx = (q * xi).astype(BF16)
    out = _bdot(sb, vb) + _bdot(qx, states_b)
    return out, (qb, kb, sb, qx)


def _ret_states(kz, vb, gam, state, states_ref):
    kv = _bdot_tn(kz, vb)
    for u in range(RET_GROUP):
        states_ref[u] = state
        state = gam * state + kv[u]
    return state


def _table_specs():
    return [pl.BlockSpec((None, CHUNK, HEAD_DIM), lambda h, *_: (h, 0, 0))] * 4


def _ret_fwd(proj, tabs, gn_gain, gn_bias):
    _, S, W = proj.shape
    H, nc = W // HEAD_DIM, S // CHUNK
    assert nc % RET_GROUP == 0
    rows_per_group = RET_GROUP * CHUNK

    def body(q_ref, k_ref, v_ref, g_ref, dec_ref, xi_ref, ze_ref, gam_ref, gain_ref, bias_ref, o_ref, states_ref):
        dec, xi, ze, gam = dec_ref[...], xi_ref[...], ze_ref[...], gam_ref[...]
        gain, bias = gain_ref[...], bias_ref[...]

        def group(i, state):
            rows = pl.ds(pl.multiple_of(i * rows_per_group, rows_per_group), rows_per_group)
            q, k, vb = _chunks(q_ref[rows, :]), _chunks(k_ref[rows, :]), _chunks(v_ref[rows, :]).astype(BF16)
            state = _ret_states((k * ze).astype(BF16), vb, gam, state, states_ref)
            out, _ = _ret_group(q, k, vb, states_ref[...].astype(BF16), dec, xi, ze)
            mu = jnp.mean(out, axis=-1, keepdims=True)
            d = out - mu
            yn = d * lax.rsqrt(jnp.mean(d * d, axis=-1, keepdims=True) + EPS)
            g = _chunks(g_ref[rows, :])
            mix = g * _sigmoid(g) * (yn * gain + bias)
            o_ref[rows, :] = mix.reshape(rows_per_group, HEAD_DIM).astype(BF16)
            return state

        lax.fori_loop(0, nc // RET_GROUP, group, jnp.zeros((HEAD_DIM, HEAD_DIM), F32))

    vec = pl.BlockSpec((1, HEAD_DIM), lambda h: (0, h))
    return pl.pallas_call(
        body, name="ret_fwd", out_shape=jax.ShapeDtypeStruct((S, W), BF16), grid=(H,),
        in_specs=[_head_spec(S, 0), _head_spec(S, 1), _head_spec(S, 2), _head_spec(S, 3)] + _table_specs() + [vec, vec],
        out_specs=pl.BlockSpec((S, HEAD_DIM), lambda h: (0, h)),
        scratch_shapes=[pltpu.VMEM((RET_GROUP, HEAD_DIM, HEAD_DIM), F32)],
        compiler_params=_params("parallel"),
    )(proj, proj, proj, proj, *tabs, gn_gain, gn_bias)


def _sb_scores(qb, kk, masked, causal, upper):
    z = _dot_nt(qb, kk) * (HEAD_DIM ** -0.5)
    e = jnp.exp(-jnp.abs(z))
    l1p = jnp.log(1.0 + e)
    log_beta = jnp.minimum(z, 0.0) - l1p
    lk = jnp.minimum(-z, 0.0) - l1p
    if masked:
        lk = jnp.where(causal, lk, 0.0)
    hi = lk.astype(BF16)
    lo = (lk - hi.astype(F32)).astype(BF16)
    cs = _dot(hi, upper) + _dot(lo, upper)
    return log_beta, lk, cs


def _tri(B, kind):
    r = lax.broadcasted_iota(jnp.int32, (B, B), 0)
    c = lax.broadcasted_iota(jnp.int32, (B, B), 1)
    return {"gt": r > c, "lt": r < c}[kind]


def _ones_where(mask):
    return jnp.where(mask, 1.0, 0.0).astype(BF16)


def _exchange_with_all(src_ref, out_ref, send_sems, recv_sems, local_sem):
    x, y, c = _mesh_pos()
    peers = [(1 - x if k & 4 else x, 1 - y if k & 2 else y, 1 - c if k & 1 else c) for k in range(1, N_DEV)]

    def copy(k, owner, to):
        px, py, pc = owner
        return pltpu.make_async_remote_copy(
            src_ref=src_ref, dst_ref=out_ref.at[4 * px + 2 * py + pc], send_sem=send_sems.at[k],
            recv_sem=recv_sems.at[k], device_id=to, device_id_type=MESH)

    sends = [copy(k, (x, y, c), p) for k, p in enumerate(peers)]
    mine = pltpu.make_async_copy(src_ref, out_ref.at[4 * x + 2 * y + c], local_sem)

    def start():
        for cp in sends:
            cp.start()
        mine.start()

    def wait():
        for k, p in enumerate(peers):
            copy(k, p, p).wait_recv()
        for cp in sends:
            cp.wait_send()
        mine.wait()

    return start, wait


def _sb_fwd(proj, gain, wo_shard):
    _, S, W = proj.shape
    H = W // HEAD_DIM
    B = min(SB_BLOCK, S)
    nq = S // B
    assert nq <= HEAD_DIM and nq % SB_PER_STEP == 0
    ns = nq // SB_PER_STEP

    def body(q_ref, k_ref, v_ref, g_ref, gain_ref, wo_ref, mix_ref, raw_ref, car_ref, woall_ref, kb_ref, vb_ref,
             send_sems, recv_sems, local_sem):
        hd, si = pl.program_id(0), pl.program_id(1)
        start_gather, wait_gather = _exchange_with_all(wo_ref, woall_ref, send_sems, recv_sems, local_sem)
        pl.when((hd == 0) & (si == 0))(start_gather)

        @pl.when(si == 0)
        def _():
            kb_ref[...] = k_ref[...].astype(BF16)
            vb_ref[...] = v_ref[...].astype(BF16)

        causal = _tri(B, "gt")
        upper = _ones_where(causal)
        lane = lax.broadcasted_iota(jnp.int32, (B, HEAD_DIM), 1)

        def block(qb, kb, carry, acc, saved, masked):
            rows = pl.ds(pl.multiple_of(kb * B, B), B)
            log_beta, lk, cs = _sb_scores(qb, kb_ref[rows, :], masked, causal, upper)
            a = jnp.exp(log_beta + cs + carry)
            if masked:
                a = jnp.where(causal, a, 0.0)
            acc = acc + _dot(a.astype(BF16), vb_ref[rows, :])
            return carry + jnp.sum(lk, axis=1, keepdims=True), acc, jnp.where(lane == kb, carry, saved)

        init = (jnp.zeros((B, 1), F32), jnp.zeros((B, HEAD_DIM), F32), jnp.full((B, HEAD_DIM), NOT_VISITED, F32))

        def live(st):
            return (st[0] >= 0) & (jnp.max(st[1]) >= EXP_IS_ZERO_BELOW)

        def finish(u, acc, saved):
            rows = slice(u * B, (u + 1) * B)
            raw_ref[rows, :] = acc
            car_ref[rows, :] = saved
            yn = acc * lax.rsqrt(jnp.mean(acc * acc, axis=-1, keepdims=True) + EPS)
            g = g_ref[rows, :]
            mix_ref[rows, :] = (g * _sigmoid(g) * (yn * gain_ref[...])).astype(BF16)

        def whole(first_step):
            heads = []
            for u in range(SB_PER_STEP):
                qi = si * SB_PER_STEP + u
                qb = q_ref[u * B:(u + 1) * B, :].astype(BF16)
                state = block(qb, qi, *init, True)
                if not (first_step and u == 0):
                    state = block(qb, qi - 1, *state, False)
                heads.append((qi, qb, state))
            for u, (qi, qb, state) in enumerate(heads):
                if not (first_step and u == 0):
                    state = lax.while_loop(
                        live, lambda st, qb=qb: (st[0] - 1,) + block(qb, st[0], st[1], st[2], st[3], False),
                        (qi - 2,) + state)[1:]
                finish(u, state[1], state[2])

        pl.when(si == 0)(lambda: whole(True))
        pl.when(si > 0)(lambda: whole(False))
        pl.when((hd == H - 1) & (si == ns - 1))(wait_gather)

    tq = SB_PER_STEP * B
    tile = lambda j: pl.BlockSpec((None, tq, HEAD_DIM), lambda h, i: (j, i, h))
    out_tile = pl.BlockSpec((tq, HEAD_DIM), lambda h, i: (i, h))
    hbm = pl.BlockSpec(memory_space=pltpu.HBM)
    return pl.pallas_call(
        body, name="sb_fwd",
        out_shape=(jax.ShapeDtypeStruct((S, W), BF16), jax.ShapeDtypeStruct((S, W), F32),
                   jax.ShapeDtypeStruct((S, W), F32), jax.ShapeDtypeStruct((N_DEV,) + wo_shard.shape, BF16)),
        grid=(H, ns),
        in_specs=[tile(4), _head_spec(S, 5), _head_spec(S, 6), tile(7),
                  pl.BlockSpec((1, HEAD_DIM), lambda h, i: (0, h)), hbm],
        out_specs=(out_tile, out_tile, out_tile, hbm),
        scratch_shapes=[pltpu.VMEM((S, HEAD_DIM), BF16), pltpu.VMEM((S, HEAD_DIM), BF16),
                        pltpu.SemaphoreType.DMA((N_DEV - 1,)), pltpu.SemaphoreType.DMA((N_DEV - 1,)),
                        pltpu.SemaphoreType.DMA],
        compiler_params=_params("arbitrary", "arbitrary"),
    )(proj, proj, proj, proj, gain, wo_shard)


def _out_proj_loss(mix_r, mix_s, w_out, x, tgt, gf):
    S, W = mix_r.shape
    D = x.shape[1]
    tm = min(256, S)

    def body(mr_ref, ms_ref, wo_ref, x_ref, t_ref, gf_ref, dx2_ref, dx2b_ref, dmix_ref, loss_ref, gfn_ref):
        @pl.when(pl.program_id(0) == 0)
        def _():
            loss_ref[...] = jnp.zeros_like(loss_ref)
            gfn_ref[...] = jnp.zeros_like(gfn_ref)

        gfv = gf_ref[...]
        x2 = x_ref[...] + (_dot(mr_ref[...], wo_ref[:W, :]) + _dot(ms_ref[...], wo_ref[W:, :]))
        r2 = lax.rsqrt(jnp.mean(x2 * x2, axis=-1, keepdims=True) + EPS)
        n = x2 * r2
        err = n * gfv - t_ref[...]
        loss_ref[...] += 0.5 * jnp.sum(jnp.mean(err * err, axis=-1, keepdims=True))
        dy = err * (1.0 / D)
        gfn_ref[...] += jnp.sum(dy * n, axis=0, keepdims=True)
        dn = dy * gfv
        dx2 = r2 * (dn - n * jnp.mean(dn * n, axis=-1, keepdims=True))
        dx2_ref[...] = dx2
        b = dx2.astype(BF16)
        dx2b_ref[...] = b
        dmix_ref[:, :W] = _dot_nt(b, wo_ref[:W, :])
        dmix_ref[:, W:] = _dot_nt(b, wo_ref[W:, :])

    row = lambda width: pl.BlockSpec((tm, width), lambda i: (i, 0))
    return pl.pallas_call(
        body, name="out_proj_loss",
        out_shape=(jax.ShapeDtypeStruct((S, D), F32), jax.ShapeDtypeStruct((S, D), BF16),
                   jax.ShapeDtypeStruct((S, 2 * W), F32), jax.ShapeDtypeStruct((SUBLANES, LANES), F32),
                   jax.ShapeDtypeStruct((1, D), F32)),
        grid=(S // tm,),
        in_specs=[row(W), row(W), pl.BlockSpec((2 * W, D), lambda i: (0, 0)), row(D), row(D),
                  pl.BlockSpec((1, D), lambda i: (0, 0))],
        out_specs=(row(D), row(D), row(2 * W), pl.BlockSpec((SUBLANES, LANES), lambda i: (0, 0)),
                   pl.BlockSpec((1, D), lambda i: (0, 0))),
        compiler_params=_params("arbitrary"),
    )(mix_r, mix_s, w_out, x, tgt, gf)


def _silu_bwd(g, dm, normed):
    sig = _sigmoid(g)
    return dm * (g * sig), dm * normed * (sig * (1.0 + g * (1.0 - sig)))


def _ret_bwd(proj, dmix, tabs, gn_gain, gn_bias, cos, sin, gwo):
    _, S, W = proj.shape
    H, nc = W // HEAD_DIM, S // CHUNK
    assert nc % RET_GROUP == 0
    ng = nc // RET_GROUP
    rows_per_group = RET_GROUP * CHUNK

    def body(q_ref, k_ref, v_ref, g_ref, dm_ref, dec_ref, xi_ref, ze_ref, gam_ref, gain_ref, bias_ref, cos_ref,
             sin_ref, gwo_ref, dp_ref, dgain_ref, dbias_ref, rino_ref, rs_ref, dstates_ref, send_sems, recv_sems):
        dec, xi, ze, gam = dec_ref[...], xi_ref[...], ze_ref[...], gam_ref[...]
        gain, bias = gain_ref[...], bias_ref[...]
        hd = pl.program_id(0)
        other_core = 1 - lax.axis_index("c")
        copies = _to_sibling_copies([gwo_ref.at[2 * k + other_core] for k in range(4)], rino_ref, send_sems, recv_sems)

        @pl.when(hd == 0)
        def _():
            for cp in copies:
                cp.start()

        def group_rows(i):
            return pl.ds(pl.multiple_of(i * rows_per_group, rows_per_group), rows_per_group)

        def fwd_group(i, state):
            rows = group_rows(i)
            kz = (_chunks(k_ref[rows, :]) * ze).astype(BF16)
            return _ret_states(kz, _chunks(v_ref[rows, :]).astype(BF16), gam, state,
                               rs_ref.at[pl.ds(i * RET_GROUP, RET_GROUP)])

        lax.fori_loop(0, ng, fwd_group, jnp.zeros((HEAD_DIM, HEAD_DIM), F32))

        flat = lambda a: a.reshape(rows_per_group, HEAD_DIM)

        def bwd_group(t, carry):
            dgain, dbias, dstate = carry
            i = ng - 1 - t
            rows = group_rows(i)
            q, k, g = _chunks(q_ref[rows, :]), _chunks(k_ref[rows, :]), _chunks(g_ref[rows, :])
            vb = _chunks(v_ref[rows, :]).astype(BF16)
            rb = rs_ref[pl.ds(i * RET_GROUP, RET_GROUP)].astype(BF16)
            out, (qb, kb, sb, qx) = _ret_group(q, k, vb, rb, dec, xi, ze)
            kz = (k * ze).astype(BF16)
            mu = jnp.mean(out, axis=-1, keepdims=True)
            d = out - mu
            rstd = lax.rsqrt(jnp.mean(d * d, axis=-1, keepdims=True) + EPS)
            yn = d * rstd
            dgn, dg = _silu_bwd(g, _chunks(dm_ref[rows, :]), yn * gain + bias)
            dgain = dgain + jnp.sum(flat(dgn * yn), axis=0, keepdims=True)
            dbias = dbias + jnp.sum(flat(dgn), axis=0, keepdims=True)
            dyn = dgn * gain
            do = rstd * (dyn - jnp.mean(dyn, axis=-1, keepdims=True)
                         - yn * jnp.mean(dyn * yn, axis=-1, keepdims=True))
            dob = do.astype(BF16)
            dkv = _bdot_tn(qx, dob)
            for u in reversed(range(RET_GROUP)):
                dstates_ref[u] = dstate
                dstate = gam * dstate + dkv[u]
            drb = dstates_ref[...].astype(BF16)
            dv = _bdot_tn(sb, dob) + _bdot(kz, drb)
            dsb = (_bdot_nt(dob, vb) * dec).astype(BF16)
            dq = _bdot(dsb, kb) + _bdot_nt(dob, rb) * xi
            dk = _bdot_tn(dsb, qb) + _bdot_nt(vb, drb) * ze
            cs, sn = cos_ref[rows, :], -sin_ref[rows, :]
            dp_ref[0, rows, :] = _rot(flat(dq), cs, sn).astype(BF16)
            dp_ref[1, rows, :] = (_rot(flat(dk), cs, sn) * (HEAD_DIM ** -0.5)).astype(BF16)
            dp_ref[2, rows, :] = flat(dv).astype(BF16)
            dp_ref[3, rows, :] = flat(dg).astype(BF16)
            return dgain, dbias, dstate

        zero = jnp.zeros((1, HEAD_DIM), F32)
        dgain, dbias, _ = lax.fori_loop(0, ng, bwd_group, (zero, zero, jnp.zeros((HEAD_DIM, HEAD_DIM), F32)))
        dgain_ref[...] = dgain
        dbias_ref[...] = dbias

        @pl.when(hd == H - 1)
        def _():
            for cp in copies:
                cp.wait_recv()
            for cp in copies:
                cp.wait_send()

    vec = pl.BlockSpec((1, HEAD_DIM), lambda h: (0, h))
    full = pl.BlockSpec((S, HEAD_DIM), lambda h: (0, 0))
    hbm = pl.BlockSpec(memory_space=pltpu.HBM)
    return pl.pallas_call(
        body, name="ret_bwd",
        out_shape=(jax.ShapeDtypeStruct((4, S, W), BF16), jax.ShapeDtypeStruct((1, W), F32),
                   jax.ShapeDtypeStruct((1, W), F32), jax.ShapeDtypeStruct((4,) + gwo.shape[1:], gwo.dtype)),
        grid=(H,),
        in_specs=[_head_spec(S, 0), _head_spec(S, 1), _head_spec(S, 2), _head_spec(S, 3),
                  pl.BlockSpec((S, HEAD_DIM), lambda h: (0, h))] + _table_specs() + [vec, vec, full, full, hbm],
        out_specs=(pl.BlockSpec((4, S, HEAD_DIM), lambda h: (0, 0, h)), vec, vec, hbm),
        scratch_shapes=[pltpu.VMEM((nc, HEAD_DIM, HEAD_DIM), F32), pltpu.VMEM((RET_GROUP, HEAD_DIM, HEAD_DIM), F32),
                        pltpu.SemaphoreType.DMA((4,)), pltpu.SemaphoreType.DMA((4,))],
        compiler_params=_params("arbitrary"),
    )(proj, proj, proj, proj, dmix, *tabs, gn_gain, gn_bias, cos, sin, gwo)


def _sb_bwd(proj, raw, carries, dmix, gain, chip_sums_o):
    _, S, W = proj.shape
    H = W // HEAD_DIM
    B = min(SB_BLOCK, S)
    nq = S // B
    ns = nq // SB_PER_STEP

    def body(q_ref, k_ref, v_ref, g_ref, raw_ref, car_ref, dm_ref, gain_ref, so_ref, dp_ref, dgain_ref, ro_ref,
             kb_ref, vb_ref, dk_ref, dv_ref, send_sems, recv_sems):
        hd, si = pl.program_id(0), pl.program_id(1)
        start_exchange, wait_exchange = _exchange_chip_sums((so_ref,), (ro_ref,), send_sems, recv_sems)
        pl.when((hd == 0) & (si == 0))(start_exchange)

        @pl.when(si == 0)
        def _():
            kb_ref[...] = k_ref[...].astype(BF16)
            vb_ref[...] = v_ref[...].astype(BF16)
            dk_ref[...] = jnp.zeros_like(dk_ref)
            dv_ref[...] = jnp.zeros_like(dv_ref)
            dgain_ref[...] = jnp.zeros_like(dgain_ref)

        causal = _tri(B, "gt")
        upper = _ones_where(causal)
        before = _ones_where(_tri(B, "lt"))
        lane = lax.broadcasted_iota(jnp.int32, (B, HEAD_DIM), 1)
        gain_v = gain_ref[...]

        def prologue(u):
            qi = si * SB_PER_STEP + u
            rows = slice(u * B, (u + 1) * B)
            o = raw_ref[rows, :]
            rstd = lax.rsqrt(jnp.mean(o * o, axis=-1, keepdims=True) + EPS)
            yn = o * rstd
            dnrm, dg = _silu_bwd(g_ref[rows, :], dm_ref[rows, :], yn * gain_v)
            dp_ref[3, pl.ds(pl.multiple_of(qi * B, B), B), :] = dg.astype(BF16)
            dgain_ref[...] += jnp.sum(dnrm * yn, axis=0, keepdims=True)
            dyn = dnrm * gain_v
            do = rstd * (dyn - yn * jnp.mean(dyn * yn, axis=-1, keepdims=True))
            return qi, q_ref[rows, :].astype(BF16), do.astype(BF16), car_ref[rows, :]

        def block(ctx, kb, carry_g, dq, masked):
            _, qb, dob, saved = ctx
            rows = pl.ds(pl.multiple_of(kb * B, B), B)
            kk, vv = kb_ref[rows, :], vb_ref[rows, :]
            log_beta, _, cs = _sb_scores(qb, kk, masked, causal, upper)
            carry_lk = jnp.sum(jnp.where(lane == kb, saved, 0.0), axis=1, keepdims=True)
            a = jnp.exp(log_beta + cs + carry_lk)
            if masked:
                a = jnp.where(causal, a, 0.0)
            gmat = _dot_nt(dob, vv) * a
            dv_ref[rows, :] += _dot_tn(a.astype(BF16), dob)
            hi = gmat.astype(BF16)
            lo = (gmat - hi.astype(F32)).astype(BF16)
            dlk = carry_g + (_dot(hi, before) + _dot(lo, before))
            beta = jnp.exp(log_beta)
            dz = (gmat * (1.0 - beta) - dlk * beta) * (HEAD_DIM ** -0.5)
            if masked:
                dz = jnp.where(causal, dz, 0.0)
            dzb = dz.astype(BF16)
            dk_ref[rows, :] += _dot_tn(dzb, qb)
            return carry_g + jnp.sum(gmat, axis=1, keepdims=True), dq + _dot(dzb, kk)

        init = (jnp.zeros((B, 1), F32), jnp.zeros((B, HEAD_DIM), F32))

        def whole(first_step):
            ctxs = [prologue(u) for u in range(SB_PER_STEP)]
            states = []
            for u, ctx in enumerate(ctxs):
                state = init
                if not (first_step and u == 0):
                    visited = jnp.max(ctx[3], axis=0, keepdims=True) >= EXP_IS_ZERO_BELOW
                    first = jnp.min(jnp.where(visited, lane[:1, :], ctx[0]))
                    state = lax.fori_loop(first, ctx[0] - 1,
                                          lambda i, st, ctx=ctx: block(ctx, i, st[0], st[1], False), state)
                states.append(state)
            for u, (ctx, state) in enumerate(zip(ctxs, states)):
                if not (first_step and u == 0):
                    state = block(ctx, ctx[0] - 1, *state, False)
                state = block(ctx, ctx[0], *state, True)
                dp_ref[0, pl.ds(pl.multiple_of(ctx[0] * B, B), B), :] = state[1].astype(BF16)

        pl.when(si == 0)(lambda: whole(True))
        pl.when(si > 0)(lambda: whole(False))

        @pl.when(si == ns - 1)
        def _():
            dp_ref[1] = dk_ref[...].astype(BF16)
            dp_ref[2] = dv_ref[...].astype(BF16)

        pl.when((hd == H - 1) & (si == ns - 1))(wait_exchange)

    tq = SB_PER_STEP * B
    tile = lambda j: pl.BlockSpec((None, tq, HEAD_DIM), lambda h, i: (j, i, h))
    vec = pl.BlockSpec((1, HEAD_DIM), lambda h, i: (0, h))
    hbm = pl.BlockSpec(memory_space=pltpu.HBM)
    return pl.pallas_call(
        body, name="sb_bwd",
        out_shape=(jax.ShapeDtypeStruct((4, S, W), BF16), jax.ShapeDtypeStruct((1, W), F32),
                   jax.ShapeDtypeStruct((3,) + chip_sums_o.shape[1:], chip_sums_o.dtype)),
        grid=(H, ns),
        in_specs=[tile(4), _head_spec(S, 5), _head_spec(S, 6), tile(7),
                  pl.BlockSpec((tq, HEAD_DIM), lambda h, i: (i, h)),
                  pl.BlockSpec((tq, HEAD_DIM), lambda h, i: (i, h)),
                  pl.BlockSpec((tq, HEAD_DIM), lambda h, i: (i, H + h)), vec, hbm],
        out_specs=(pl.BlockSpec((4, S, HEAD_DIM), lambda h, i: (0, 0, h)), vec, hbm),
        scratch_shapes=[pltpu.VMEM((S, HEAD_DIM), BF16), pltpu.VMEM((S, HEAD_DIM), BF16),
                        pltpu.VMEM((S, HEAD_DIM), F32), pltpu.VMEM((S, HEAD_DIM), F32),
                        pltpu.SemaphoreType.DMA((1, 3)), pltpu.SemaphoreType.DMA((1, 3))],
        compiler_params=_params("arbitrary", "arbitrary"),
    )(proj, proj, proj, proj, raw, carries, dmix, gain, chip_sums_o)


def _grad_w_in_half(ht, dpr, dps, core, name, to_sibling=None):
    D, S = ht.shape
    _, _, W = dpr.shape
    tmm = min(512, D)
    nm = D // tmm

    def body(core_ref, ht_ref, r_ref, s_ref, *rest):
        o_ref = rest[1] if to_sibling is not None else rest[0]
        q, m = pl.program_id(0), pl.program_id(1)
        if to_sibling is not None:
            ga_ref, _, rin_ref, send_sems, recv_sems = rest
            copies = _to_sibling_copies([ga_ref.at[k] for k in range(4)], rin_ref, send_sems, recv_sems)

            @pl.when((q == 0) & (m == 0))
            def _():
                for cp in copies:
                    cp.start()

        @pl.when(q < 2)
        def _():
            o_ref[...] = _dot(ht_ref[...], r_ref[...])

        @pl.when(q >= 2)
        def _():
            o_ref[...] = _dot(ht_ref[...], s_ref[...])

        if to_sibling is not None:
            @pl.when((q == 3) & (m == nm - 1))
            def _():
                for cp in copies:
                    cp.wait_recv()
                for cp in copies:
                    cp.wait_send()

    hbm = pl.BlockSpec(memory_space=pltpu.HBM)
    gw_shape = jax.ShapeDtypeStruct((4, D, W), F32)
    out_shape, out_specs, extra_in, scratch = (gw_shape,), (pl.BlockSpec((None, tmm, W), lambda q, m, core: (q, m, 0)),), [], []
    if to_sibling is not None:
        out_shape += (gw_shape,)
        out_specs += (hbm,)
        extra_in = [hbm]
        scratch = [pltpu.SemaphoreType.DMA((4,)), pltpu.SemaphoreType.DMA((4,))]
    return pl.pallas_call(
        body, name=name, out_shape=out_shape,
        grid_spec=pltpu.PrefetchScalarGridSpec(
            num_scalar_prefetch=1, grid=(4, nm),
            in_specs=[pl.BlockSpec((tmm, S), lambda q, m, core: (m, 0)),
                      pl.BlockSpec((None, S, W), lambda q, m, core: (jnp.minimum(2 * q + core[0], 3), 0, 0)),
                      pl.BlockSpec((None, S, W), lambda q, m, core: (jnp.maximum(2 * q + core[0] - 4, 0), 0, 0))]
            + extra_in,
            out_specs=out_specs, scratch_shapes=scratch),
        compiler_params=_params("arbitrary", "arbitrary"),
    )(core, ht, dpr, dps, *(() if to_sibling is None else (to_sibling,)))


def _grad_w_out(mix_r, mix_s, dx2b):
    S, W = mix_r.shape
    D = dx2b.shape[1]
    tmm = min(512, W)
    tk = min(2048, S)

    def body(r_ref, s_ref, b_ref, o_ref):
        j, kk = pl.program_id(0), pl.program_id(2)

        def acc(a_ref):
            part = _dot_tn(a_ref[...], b_ref[...])

            @pl.when(kk == 0)
            def _():
                o_ref[...] = part

            @pl.when(kk > 0)
            def _():
                o_ref[...] += part

        pl.when(j == 0)(lambda: acc(r_ref))
        pl.when(j == 1)(lambda: acc(s_ref))

    return pl.pallas_call(
        body, name="grad_w_out", out_shape=jax.ShapeDtypeStruct((2, W, D), F32), grid=(2, W // tmm, S // tk),
        in_specs=[pl.BlockSpec((tk, tmm), lambda j, m, k: (k, m)),
                  pl.BlockSpec((tk, tmm), lambda j, m, k: (k, m)),
                  pl.BlockSpec((tk, D), lambda j, m, k: (k, 0))],
        out_specs=pl.BlockSpec((None, tmm, D), lambda j, m, k: (j, m, 0)),
        compiler_params=_params("parallel", "parallel", "arbitrary"),
    )(mix_r, mix_s, dx2b)


def _dh_matmul(dpr, dps, w_all, chip_sums):
    _, S, W = dpr.shape
    D = w_all.shape[1]
    tm = min(1024, S)
    ni = S // tm

    def body(r_ref, s_ref, w_ref, sa_ref, dh_ref, ra_ref, send_sems, recv_sems):
        i, j = pl.program_id(0), pl.program_id(1)
        start_exchange, wait_exchange = _exchange_chip_sums((sa_ref,), (ra_ref,), send_sems, recv_sems)
        pl.when((i == 0) & (j == 0))(start_exchange)

        def acc(b_ref):
            part = _dot_nt(b_ref[...], w_ref[...])

            @pl.when(j == 0)
            def _():
                dh_ref[...] = part

            @pl.when(j > 0)
            def _():
                dh_ref[...] += part

        pl.when(j < 4)(lambda: acc(r_ref))
        pl.when(j >= 4)(lambda: acc(s_ref))
        pl.when((i == ni - 1) & (j == 7))(wait_exchange)

    hbm = pl.BlockSpec(memory_space=pltpu.HBM)
    return pl.pallas_call(
        body, name="dh_matmul",
        out_shape=(jax.ShapeDtypeStruct((S, D), F32), jax.ShapeDtypeStruct((3,) + chip_sums.shape[1:], chip_sums.dtype)),
        grid=(ni, 8),
        in_specs=[pl.BlockSpec((None, tm, W), lambda i, j: (jnp.minimum(j, 3), i, 0)),
                  pl.BlockSpec((None, tm, W), lambda i, j: (jnp.maximum(j - 4, 0), i, 0)),
                  pl.BlockSpec((None, D, W), lambda i, j: (j, 0, 0)), hbm],
        out_specs=(pl.BlockSpec((tm, D), lambda i, j: (i, 0)), hbm),
        scratch_shapes=[pltpu.SemaphoreType.DMA((1, 3)), pltpu.SemaphoreType.DMA((1, 3))],
        compiler_params=_params("arbitrary", "arbitrary"),
    )(dpr, dps, w_all, chip_sums)


def _norm_bwd(x, dx2, dh, gain):
    S, D = x.shape
    tm = min(256, S)

    def body(x_ref, dx2_ref, dh_ref, g_ref, gx_ref, dgain_ref):
        @pl.when(pl.program_id(0) == 0)
        def _():
            dgain_ref[...] = jnp.zeros_like(dgain_ref)

        xv, dh_v = x_ref[...], dh_ref[...]
        r1 = lax.rsqrt(jnp.mean(xv * xv, axis=-1, keepdims=True) + EPS)
        n = xv * r1
        dgain_ref[...] += jnp.sum(dh_v * n, axis=0, keepdims=True)
        dn = dh_v * g_ref[...]
        gx_ref[...] = dx2_ref[...] + r1 * (dn - n * jnp.mean(dn * n, axis=-1, keepdims=True))

    row = pl.BlockSpec((tm, D), lambda i: (i, 0))
    one = pl.BlockSpec((1, D), lambda i: (0, 0))
    return pl.pallas_call(
        body, name="norm_bwd", out_shape=(jax.ShapeDtypeStruct((S, D), F32), jax.ShapeDtypeStruct((1, D), F32)),
        grid=(S // tm,), in_specs=[row, row, row, one], out_specs=(row, one),
        compiler_params=_params("arbitrary"),
    )(x, dx2, dh, gain)


def _own_block(gw, pos, q):
    return q if gw.shape[0] == 4 else 2 * q + pos[0]


def _rs_local_sum(gw, rin, pos):
    _, R, C = gw.shape
    tr = min(1024, R)
    other = lambda k, pos: (pos[1] + 1 + k) % 4

    def body(pos_ref, a_ref, b_ref, o_ref):
        o_ref[...] = (a_ref[...] + b_ref[...]).astype(BF16)

    return pl.pallas_call(
        body, name="rs_local_sum", out_shape=jax.ShapeDtypeStruct((4, R, C), BF16),
        grid_spec=pltpu.PrefetchScalarGridSpec(
            num_scalar_prefetch=1, grid=(3, R // tr),
            in_specs=[pl.BlockSpec((None, tr, C), lambda k, i, pos: (_own_block(gw, pos, other(k, pos)), i, 0)),
                      pl.BlockSpec((None, tr, C), lambda k, i, pos: (other(k, pos), i, 0))],
            out_specs=pl.BlockSpec((None, tr, C), lambda k, i, pos: (other(k, pos), i, 0))),
        compiler_params=_params("parallel", "parallel"),
    )(pos, gw, rin)


def _adamw(w, g, m, v):
    m2 = ADAM_B1 * m + (1.0 - ADAM_B1) * g
    v2 = ADAM_B2 * v + (1.0 - ADAM_B2) * (g * g)
    m_hat = m2 / (1.0 - ADAM_B1 ** ADAM_STEP)
    v_hat = v2 / (1.0 - ADAM_B2 ** ADAM_STEP)
    delta = -ADAM_LR * (m_hat / (jnp.sqrt(v_hat) + ADAM_EPS) + ADAM_WD * w)
    return delta, m2, v2


def _adamw_shard(gw, rin, rb, w, m, v, pos):
    _, R, C = gw.shape
    tr = min(256, R)

    def body(pos_ref, a_ref, b_ref, rb_ref, w_ref, m_ref, v_ref, g_ref, d_ref, m2_ref, v2_ref):
        g = a_ref[...] + b_ref[...]
        for k in range(3):
            g = g + rb_ref[k].astype(F32)
        g_ref[...] = g
        d_ref[...], m2_ref[...], v2_ref[...] = _adamw(w_ref[...], g, m_ref[...], v_ref[...])

    plain = pl.BlockSpec((tr, C), lambda i, pos: (i, 0))
    shape = jax.ShapeDtypeStruct((R, C), F32)
    return pl.pallas_call(
        body, name="adamw_shard", out_shape=(shape,) * 4,
        grid_spec=pltpu.PrefetchScalarGridSpec(
            num_scalar_prefetch=1, grid=(R // tr,),
            in_specs=[pl.BlockSpec((None, tr, C), lambda i, pos: (_own_block(gw, pos, pos[1]), i, 0)),
                      pl.BlockSpec((None, tr, C), lambda i, pos: (pos[1], i, 0)),
                      pl.BlockSpec((3, tr, C), lambda i, pos: (0, i, 0)), plain, plain, plain],
            out_specs=(plain,) * 4),
        compiler_params=_params("parallel"),
    )(pos, gw, rin, rb, w, m, v)


def _adamw_small(parts, w, m, v):
    _, rows, n = parts.shape

    def body(p_ref, w_ref, m_ref, v_ref, g_ref, d_ref, m2_ref, v2_ref):
        g = p_ref[0]
        for d in range(1, N_DEV):
            g = g + p_ref[d]
        g_ref[...] = g
        d_ref[...], m2_ref[...], v2_ref[...] = _adamw(w_ref[...], g, m_ref[...], v_ref[...])

    shape = jax.ShapeDtypeStruct((rows, n), F32)
    return pl.pallas_call(body, name="adamw_small", out_shape=(shape,) * 4)(parts, w, m, v)


def _rope_tables(S):
    half = HEAD_DIM // 2
    inv = ROPE_THETA ** (-jnp.arange(half, dtype=F32) / half)
    ang = jnp.arange(S, dtype=F32)[:, None] * inv[None, :]
    cos, sin = jnp.cos(ang), jnp.sin(ang)
    return jnp.concatenate([cos, cos], axis=1), jnp.concatenate([-sin, sin], axis=1)


def _retention_tables(H):
    lg = jnp.log1p(-jnp.exp2(-5.0 - jnp.arange(H, dtype=F32)))
    n = jnp.arange(CHUNK, dtype=F32)
    rel = n[:, None] - n[None, :]
    decay = jnp.where(rel >= 0, jnp.exp(lg[:, None, None] * jnp.maximum(rel, 0.0)), 0.0)
    shape = (H, CHUNK, HEAD_DIM)
    xi = jnp.broadcast_to(jnp.exp(lg[:, None] * (n + 1.0))[:, :, None], shape)
    zeta = jnp.broadcast_to(jnp.exp(lg[:, None] * (CHUNK - 1.0 - n))[:, :, None], shape)
    gamma_c = jnp.broadcast_to(jnp.exp(lg * CHUNK)[:, None, None], shape)
    return decay, xi, zeta, gamma_c


def _pack_small(parts):
    flat = []
    for p in parts:
        p = p.reshape(-1)
        flat.append(jnp.pad(p, (0, -p.shape[0] % LANES)))
    flat = jnp.concatenate(flat)
    return jnp.pad(flat, (0, SMALL_N - flat.shape[0])).reshape(SUBLANES, SMALL_N // SUBLANES)


def _unpack_small(packed, shapes):
    flat = packed.reshape(-1)
    out, at = [], 0
    for shp in shapes:
        size = 1
        for s in shp:
            size *= s
        out.append(flat[at:at + size].reshape(shp))
        at += size + (-size % LANES)
    return out


def kernel(x, norm_gain, w_in, ret_gn_gain, ret_gn_bias, sb_norm_gain, w_out, final_norm_gain, loss_target, m_norm_gain, m_w_in, m_ret_gn_gain, m_ret_gn_bias, m_sb_norm_gain, m_w_out, m_final_norm_gain, v_norm_gain, v_w_in, v_ret_gn_gain, v_ret_gn_bias, v_sb_norm_gain, v_w_out, v_final_norm_gain):
    S, D = x.shape[1], x.shape[2]
    W = w_in.shape[2]
    wo_rows = w_out.shape[1]
    H = W // HEAD_DIM
    xs, tgt = x[0], loss_target[0]
    mx, my, mc = _mesh_pos()
    pos = jnp.stack([mc, 2 * mx + my]).astype(jnp.int32)

    cos, sin = _rope_tables(S)
    tabs = _retention_tables(H)

    proj, w_all, ht = _in_proj_gather(xs, norm_gain, w_in[0].astype(BF16), cos, sin, _gather_order())
    mix_r = _ret_fwd(proj, tabs, ret_gn_gain, ret_gn_bias)
    mix_s, raw_s, carries, wo_all = _sb_fwd(proj, sb_norm_gain, w_out[0].astype(BF16))
    wo_full = wo_all.reshape(N_DEV * wo_rows, D)
    dx2, dx2b, dmix, loss_p, d_gf = _out_proj_loss(mix_r, mix_s, wo_full, xs, tgt, final_norm_gain[None])

    gwo = _grad_w_out(mix_r, mix_s, dx2b).reshape(N_DEV, wo_rows, D)
    dpr, d_rgain, d_rbias, rino = _ret_bwd(proj, dmix, tabs, ret_gn_gain, ret_gn_bias, cos, sin, gwo)
    dps, d_sgain, rbo = _sb_bwd(proj, raw_s, carries, dmix, sb_norm_gain, _rs_local_sum(gwo, rino, pos))
    gw_sibling, = _grad_w_in_half(ht, dpr, dps, (1 - mc).reshape(1).astype(jnp.int32), "grad_w_in_sibling")
    gw, rin = _grad_w_in_half(ht, dpr, dps, mc.reshape(1).astype(jnp.int32), "grad_w_in_own", to_sibling=gw_sibling)
    dh, rb = _dh_matmul(dpr, dps, w_all, _rs_local_sum(gw, rin, pos))
    grad_x, d_gain = _norm_bwd(xs, dx2, dh, norm_gain)
    g_in, d_in, m_in, v_in = _adamw_shard(gw, rin, rb, w_in[0], m_w_in[0], v_w_in[0], pos)
    g_out, d_out, m_out, v_out = _adamw_shard(gwo, rino, rbo, w_out[0], m_w_out[0], v_w_out[0], pos)

    small_w = [norm_gain, ret_gn_gain, ret_gn_bias, sb_norm_gain, final_norm_gain]
    small_m = [m_norm_gain, m_ret_gn_gain, m_ret_gn_bias, m_sb_norm_gain, m_final_norm_gain]
    small_v = [v_norm_gain, v_ret_gn_gain, v_ret_gn_bias, v_sb_norm_gain, v_final_norm_gain]
    shapes = [()] + [w.shape for w in small_w]
    zero = jnp.zeros((), F32)
    parts = _small_all_gather(_pack_small([loss_p[0, 0], d_gain, d_rgain, d_rbias, d_sgain, d_gf]))
    packed = _adamw_small(parts, _pack_small([zero] + small_w), _pack_small([zero] + small_m),
                          _pack_small([zero] + small_v))
    g_s, d_s, m_s, v_s = (_unpack_small(p, shapes) for p in packed)

    grads = [g_s[1], g_in[None], g_s[2], g_s[3], g_s[4], g_out[None], g_s[5]]
    deltas = [d_s[1], d_in[None], d_s[2], d_s[3], d_s[4], d_out[None], d_s[5]]
    new_m = [m_s[1], m_in[None], m_s[2], m_s[3], m_s[4], m_out[None], m_s[5]]
    new_v = [v_s[1], v_in[None], v_s[2], v_s[3], v_s[4], v_out[None], v_s[5]]
    return (g_s[0], grad_x[None], *grads, *deltas, *new_m, *new_v)
```

```python
import functools

import jax
import jax.numpy as jnp
from jax import lax
from jax.experimental import pallas as pl
from jax.experimental.pallas import tpu as pltpu

F32 = jnp.float32
BF16 = jnp.bfloat16

HEAD_DIM = 128
CHUNK = 128
RET_GROUP = 16
ROPE_THETA = 10000.0
EPS = 1e-6
ADAM_LR = 0.001
ADAM_B1 = 0.9
ADAM_B2 = 0.999
ADAM_EPS = 1e-08
ADAM_WD = 0.01
ADAM_STEP = 10

N_DEV = 8
LANES = 128
SUBLANES = 8
VMEM_LIMIT = 56 * 1024 * 1024
SB_BLOCK = 256
SB_PER_STEP = 4
SMALL_N = 8192
EXP_IS_ZERO_BELOW = -104.0
NOT_VISITED = -1e30
MESH = pl.DeviceIdType.MESH

NT = (((1,), (1,)), ((), ()))
TN = (((0,), (0,)), ((), ()))


def _params(*sem):
    return pltpu.CompilerParams(dimension_semantics=sem if sem else None, vmem_limit_bytes=VMEM_LIMIT)


def _dot(a, b):
    return jnp.dot(a, b, preferred_element_type=F32)


def _dot_nt(a, b):
    return lax.dot_general(a, b, NT, preferred_element_type=F32)


def _dot_tn(a, b):
    return lax.dot_general(a, b, TN, preferred_element_type=F32)


def _sigmoid(g):
    return 1.0 / (1.0 + jnp.exp(-g))


def _rot(a, cos, sin_signed):
    return a * cos + pltpu.roll(a, HEAD_DIM // 2, 1) * sin_signed


def _mesh_pos():
    return lax.axis_index("x"), lax.axis_index("y"), lax.axis_index("c")


def _to_sibling_copies(blocks, out_ref, send_sems, recv_sems):
    x, y, c = _mesh_pos()
    return [pltpu.make_async_remote_copy(
        src_ref=block, dst_ref=out_ref.at[k], send_sem=send_sems.at[k], recv_sem=recv_sems.at[k],
        device_id=(x, y, 1 - c), device_id_type=MESH) for k, block in enumerate(blocks)]


def _exchange_chip_sums(srcs, outs, send_sems, recv_sems):
    x, y, c = _mesh_pos()
    copies = []
    for arr, (src, out) in enumerate(zip(srcs, outs)):
        for k in range(1, 4):
            px = 1 - x if k & 2 else x
            py = 1 - y if k & 1 else y
            copies.append(pltpu.make_async_remote_copy(
                src_ref=src.at[2 * px + py], dst_ref=out.at[k - 1],
                send_sem=send_sems.at[arr, k - 1], recv_sem=recv_sems.at[arr, k - 1],
                device_id=(px, py, c), device_id_type=MESH))

    def start():
        for cp in copies:
            cp.start()

    def wait():
        for cp in copies:
            cp.wait_recv()
        for cp in copies:
            cp.wait_send()

    return start, wait


def _small_all_gather(small):
    rows, n = small.shape

    def body(s_ref, o_ref, send_sems, recv_sems, local_sem):
        start, wait = _exchange_with_all(s_ref, o_ref, send_sems, recv_sems, local_sem)
        start()
        wait()

    vmem = pl.BlockSpec(memory_space=pltpu.VMEM)
    return pl.pallas_call(
        body, name="small_all_gather",
        out_shape=jax.ShapeDtypeStruct((N_DEV, rows, n), small.dtype),
        in_specs=[vmem], out_specs=vmem,
        scratch_shapes=[pltpu.SemaphoreType.DMA((N_DEV - 1,)), pltpu.SemaphoreType.DMA((N_DEV - 1,)),
                        pltpu.SemaphoreType.DMA],
    )(small)


GATHER_SPLIT = 2
GATHER_STEPS = ([("own", 0, p) for p in range(GATHER_SPLIT)] + [("sibling", 0, p) for p in range(GATHER_SPLIT)]
                + [step for p in range(GATHER_SPLIT) for step in
                   (("ici", 0, p), ("ici", 1, p), ("passed", 0, p), ("passed", 1, p))]
                + [step for p in range(GATHER_SPLIT) for step in (("ici", 2, p), ("passed", 2, p))])


def _via_x(p):
    return p % 2 == 0


def _gather_order():
    x, y, c = _mesh_pos()
    chips = [(1 - x, y), (x, 1 - y), (1 - x, 1 - y)]
    owner = {"own": lambda j: (x, y, c), "sibling": lambda j: (x, y, 1 - c),
             "ici": lambda j: (*chips[j], c), "passed": lambda j: (*chips[j], 1 - c)}
    blocks = [4 * px + 2 * py + pc for px, py, pc in (owner[kind](j) for kind, j, _ in GATHER_STEPS)]
    return (jnp.stack(blocks).astype(jnp.int32), jnp.array([p for _, _, p in GATHER_STEPS], jnp.int32))


def _in_proj_gather(x, gain, w_shard, cos, sin, order):
    S, D = x.shape
    W = w_shard.shape[1]
    wp = W // GATHER_SPLIT
    tm = min(1024, S)
    ni = S // tm
    n_steps = len(GATHER_STEPS)

    def body(blk_ref, piece_ref, x_ref, g_ref, w_ref, cos_ref, sin_ref, o_ref, wall_ref, ht_ref, h_scr, wbuf,
             send_sems, recv_sems, local_sem, load_sems):
        step, i = pl.program_id(0), pl.program_id(1)
        mx, my, c = _mesh_pos()
        me, sibling = (mx, my, c), (mx, my, 1 - c)
        chips = [(1 - mx, my), (mx, 1 - my), (1 - mx, 1 - my)]

        def piece_of(dev, p):
            px, py, pc = dev
            return wall_ref.at[4 * px + 2 * py + pc, :, pl.ds(p * wp, wp)]

        def copy(k, p, block, to, own=False):
            dst = piece_of(block, p)
            return pltpu.make_async_remote_copy(
                src_ref=w_ref.at[:, pl.ds(p * wp, wp)] if own else dst, dst_ref=dst,
                send_sem=send_sems.at[k, p], recv_sem=recv_sems.at[k, p], device_id=to, device_id_type=MESH)

        pieces = range(GATHER_SPLIT)
        first = [cp for p in pieces for cp in
                 [copy(0, p, me, sibling, own=True)] + [copy(1 + j, p, me, (*chips[j], c), own=True) for j in (0, 1)]]
        passed = {(j, p): copy(4 + j, p, (*chip, c), sibling) for j, chip in enumerate(chips) for p in pieces}
        onward = {p: copy(3, p, (*chips[0 if _via_x(p) else 1], c), (*chips[1 if _via_x(p) else 0], c)) for p in pieces}
        mine = pltpu.make_async_copy(w_ref, wall_ref.at[4 * mx + 2 * my + c], local_sem)

        def load(s):
            kind, j, p = GATHER_STEPS[s]
            if kind == "own":
                src = w_ref.at[:, pl.ds(p * wp, wp)]
            elif kind == "sibling":
                copy(0, p, sibling, me).wait_recv()
                src = piece_of(sibling, p)
            elif kind == "ici":
                copy(1 + j, p, (*chips[j], c), me).wait_recv()
                if j == (0 if _via_x(p) else 1):
                    onward[p].start()
                passed[j, p].start()
                src = piece_of((*chips[j], c), p)
            else:
                copy(4 + j, p, (*chips[j], 1 - c), me).wait_recv()
                src = piece_of((*chips[j], 1 - c), p)
            return pltpu.make_async_copy(src, wbuf.at[s % 2], load_sems.at[s % 2])

        for s in range(n_steps):
            @pl.when((step == s) & (i == 0))
            def _(s=s):
                if s == 0:
                    for cp in first:
                        cp.start()
                    mine.start()
                    load(0).start()
                pltpu.make_async_copy(w_ref.at[:, pl.ds(0, wp)], wbuf.at[s % 2], load_sems.at[s % 2]).wait()

            if s + 1 < n_steps:
                @pl.when((step == s) & (i == ni - 1))
                def _(s=s):
                    load(s + 1).start()

        rows = pl.ds(pl.multiple_of(i * tm, tm), tm)

        @pl.when(step == 0)
        def _():
            xv = x_ref[...]
            r = lax.rsqrt(jnp.mean(xv * xv, axis=-1, keepdims=True) + EPS)
            hv = xv * r * g_ref[...]
            h_scr[rows, :] = hv.astype(BF16)
            ht_ref[...] = hv.T.astype(BF16)

        acc = _dot(h_scr[rows, :], wbuf[step % 2])
        b = blk_ref[step]

        @pl.when(b >= 2)
        def _():
            o_ref[...] = acc

        @pl.when(b < 2)
        def _():
            scale = jnp.where(b == 1, HEAD_DIM ** -0.5, 1.0).astype(F32)
            cs, sn = cos_ref[...], sin_ref[...]
            for hh in range(wp // HEAD_DIM):
                cols = slice(hh * HEAD_DIM, (hh + 1) * HEAD_DIM)
                o_ref[:, cols] = _rot(acc[:, cols], cs, sn) * scale

        @pl.when((step == n_steps - 1) & (i == ni - 1))
        def _():
            for cp in first + list(passed.values()) + list(onward.values()):
                cp.wait_send()
            mine.wait()

    hbm = pl.BlockSpec(memory_space=pltpu.HBM)
    rope = pl.BlockSpec((tm, HEAD_DIM), lambda s, i, blk, piece: (i, 0))
    first_pass = lambda s, i: jnp.where(s == 0, i, ni - 1)
    return pl.pallas_call(
        body, name="in_proj_gather",
        out_shape=(jax.ShapeDtypeStruct((N_DEV, S, W), F32), jax.ShapeDtypeStruct((N_DEV, D, W), BF16),
                   jax.ShapeDtypeStruct((D, S), BF16)),
        grid_spec=pltpu.PrefetchScalarGridSpec(
            num_scalar_prefetch=2, grid=(n_steps, ni),
            in_specs=[pl.BlockSpec((tm, D), lambda s, i, blk, piece: (first_pass(s, i), 0)),
                      pl.BlockSpec((1, D), lambda s, i, blk, piece: (0, 0)), hbm, rope, rope],
            out_specs=(pl.BlockSpec((None, tm, wp), lambda s, i, blk, piece: (blk[s], i, piece[s])), hbm,
                       pl.BlockSpec((D, tm), lambda s, i, blk, piece: (0, first_pass(s, i)))),
            scratch_shapes=[pltpu.VMEM((S, D), BF16), pltpu.VMEM((2, D, wp), BF16),
                            pltpu.SemaphoreType.DMA((7, GATHER_SPLIT)), pltpu.SemaphoreType.DMA((7, GATHER_SPLIT)),
                            pltpu.SemaphoreType.DMA, pltpu.SemaphoreType.DMA((2,))]),
        compiler_params=_params("arbitrary", "arbitrary"),
    )(*order, x, gain, w_shard, cos, sin)


def _head_spec(S, j):
    return pl.BlockSpec((None, S, HEAD_DIM), lambda h, *_: (j, 0, h))


def _bdot(a, b):
    return lax.dot_general(a, b, (((2,), (1,)), ((0,), (0,))), preferred_element_type=F32)


def _bdot_nt(a, b):
    return lax.dot_general(a, b, (((2,), (2,)), ((0,), (0,))), preferred_element_type=F32)


def _bdot_tn(a, b):
    return lax.dot_general(a, b, (((1,), (1,)), ((0,), (0,))), preferred_element_type=F32)


def _chunks(a):
    return a.reshape(a.shape[0] // CHUNK, CHUNK, a.shape[1])


def _ret_group(q, k, vb, states_b, dec, xi, ze):
    qb, kb = q.astype(BF16), k.astype(BF16)
    sb = (_bdot_nt(qb, kb) * dec).astype(BF16)
    qx = (q * xi).astype(BF16)
    out = _bdot(sb, vb) + _bdot(qx, states_b)
    return out, (qb, kb, sb, qx)


def _ret_states(kz, vb, gam, state, states_ref):
    kv = _bdot_tn(kz, vb)
    for u in range(RET_GROUP):
        states_ref[u] = state
        state = gam * state + kv[u]
    return state


def _table_specs():
    return [pl.BlockSpec((None, CHUNK, HEAD_DIM), lambda h, *_: (h, 0, 0))] * 4


def _ret_fwd(proj, tabs, gn_gain, gn_bias):
    _, S, W = proj.shape
    H, nc = W // HEAD_DIM, S // CHUNK
    assert nc % RET_GROUP == 0
    rows_per_group = RET_GROUP * CHUNK

    def body(q_ref, k_ref, v_ref, g_ref, dec_ref, xi_ref, ze_ref, gam_ref, gain_ref, bias_ref, o_ref, states_ref):
        dec, xi, ze, gam = dec_ref[...], xi_ref[...], ze_ref[...], gam_ref[...]
        gain, bias = gain_ref[...], bias_ref[...]

        def group(i, state):
            rows = pl.ds(pl.multiple_of(i * rows_per_group, rows_per_group), rows_per_group)
            q, k, vb = _chunks(q_ref[rows, :]), _chunks(k_ref[rows, :]), _chunks(v_ref[rows, :]).astype(BF16)
            state = _ret_states((k * ze).astype(BF16), vb, gam, state, states_ref)
            out, _ = _ret_group(q, k, vb, states_ref[...].astype(BF16), dec, xi, ze)
            mu = jnp.mean(out, axis=-1, keepdims=True)
            d = out - mu
            yn = d * lax.rsqrt(jnp.mean(d * d, axis=-1, keepdims=True) + EPS)
            g = _chunks(g_ref[rows, :])
            mix = g * _sigmoid(g) * (yn * gain + bias)
            o_ref[rows, :] = mix.reshape(rows_per_group, HEAD_DIM).astype(BF16)
            return state

        lax.fori_loop(0, nc // RET_GROUP, group, jnp.zeros((HEAD_DIM, HEAD_DIM), F32))

    vec = pl.BlockSpec((1, HEAD_DIM), lambda h: (0, h))
    return pl.pallas_call(
        body, name="ret_fwd", out_shape=jax.ShapeDtypeStruct((S, W), BF16), grid=(H,),
        in_specs=[_head_spec(S, 0), _head_spec(S, 1), _head_spec(S, 2), _head_spec(S, 3)] + _table_specs() + [vec, vec],
        out_specs=pl.BlockSpec((S, HEAD_DIM), lambda h: (0, h)),
        scratch_shapes=[pltpu.VMEM((RET_GROUP, HEAD_DIM, HEAD_DIM), F32)],
        compiler_params=_params("parallel"),
    )(proj, proj, proj, proj, *tabs, gn_gain, gn_bias)


def _sb_scores(qb, kk, masked, causal, upper):
    z = _dot_nt(qb, kk) * (HEAD_DIM ** -0.5)
    e = jnp.exp(-jnp.abs(z))
    l1p = jnp.log(1.0 + e)
    log_beta = jnp.minimum(z, 0.0) - l1p
    lk = jnp.minimum(-z, 0.0) - l1p
    if masked:
        lk = jnp.where(causal, lk, 0.0)
    hi = lk.astype(BF16)
    lo = (lk - hi.astype(F32)).astype(BF16)
    cs = _dot(hi, upper) + _dot(lo, upper)
    return log_beta, lk, cs


def _tri(B, kind):
    r = lax.broadcasted_iota(jnp.int32, (B, B), 0)
    c = lax.broadcasted_iota(jnp.int32, (B, B), 1)
    return {"gt": r > c, "lt": r < c}[kind]


def _ones_where(mask):
    return jnp.where(mask, 1.0, 0.0).astype(BF16)


def _exchange_with_all(src_ref, out_ref, send_sems, recv_sems, local_sem):
    x, y, c = _mesh_pos()
    peers = [(1 - x if k & 4 else x, 1 - y if k & 2 else y, 1 - c if k & 1 else c) for k in range(1, N_DEV)]

    def copy(k, owner, to):
        px, py, pc = owner
        return pltpu.make_async_remote_copy(
            src_ref=src_ref, dst_ref=out_ref.at[4 * px + 2 * py + pc], send_sem=send_sems.at[k],
            recv_sem=recv_sems.at[k], device_id=to, device_id_type=MESH)

    sends = [copy(k, (x, y, c), p) for k, p in enumerate(peers)]
    mine = pltpu.make_async_copy(src_ref, out_ref.at[4 * x + 2 * y + c], local_sem)

    def start():
        for cp in sends:
            cp.start()
        mine.start()

    def wait():
        for k, p in enumerate(peers):
            copy(k, p, p).wait_recv()
        for cp in sends:
            cp.wait_send()
        mine.wait()

    return start, wait


def _sb_fwd(proj, gain, wo_shard):
    _, S, W = proj.shape
    H = W // HEAD_DIM
    B = min(SB_BLOCK, S)
    nq = S // B
    assert nq <= HEAD_DIM and nq % SB_PER_STEP == 0
    ns = nq // SB_PER_STEP

    def body(q_ref, k_ref, v_ref, g_ref, gain_ref, wo_ref, mix_ref, raw_ref, car_ref, woall_ref, kb_ref, vb_ref,
             send_sems, recv_sems, local_sem):
        hd, si = pl.program_id(0), pl.program_id(1)
        start_gather, wait_gather = _exchange_with_all(wo_ref, woall_ref, send_sems, recv_sems, local_sem)
        pl.when((hd == 0) & (si == 0))(start_gather)

        @pl.when(si == 0)
        def _():
            kb_ref[...] = k_ref[...].astype(BF16)
            vb_ref[...] = v_ref[...].astype(BF16)

        causal = _tri(B, "gt")
        upper = _ones_where(causal)
        lane = lax.broadcasted_iota(jnp.int32, (B, HEAD_DIM), 1)

        def block(qb, kb, carry, acc, saved, masked):
            rows = pl.ds(pl.multiple_of(kb * B, B), B)
            log_beta, lk, cs = _sb_scores(qb, kb_ref[rows, :], masked, causal, upper)
            a = jnp.exp(log_beta + cs + carry)
            if masked:
                a = jnp.where(causal, a, 0.0)
            acc = acc + _dot(a.astype(BF16), vb_ref[rows, :])
            return carry + jnp.sum(lk, axis=1, keepdims=True), acc, jnp.where(lane == kb, carry, saved)

        init = (jnp.zeros((B, 1), F32), jnp.zeros((B, HEAD_DIM), F32), jnp.full((B, HEAD_DIM), NOT_VISITED, F32))

        def live(st):
            return (st[0] >= 0) & (jnp.max(st[1]) >= EXP_IS_ZERO_BELOW)

        def finish(u, acc, saved):
            rows = slice(u * B, (u + 1) * B)
            raw_ref[rows, :] = acc
            car_ref[rows, :] = saved
            yn = acc * lax.rsqrt(jnp.mean(acc * acc, axis=-1, keepdims=True) + EPS)
            g = g_ref[rows, :]
            mix_ref[rows, :] = (g * _sigmoid(g) * (yn * gain_ref[...])).astype(BF16)

        def whole(first_step):
            heads = []
            for u in range(SB_PER_STEP):
                qi = si * SB_PER_STEP + u
                qb = q_ref[u * B:(u + 1) * B, :].astype(BF16)
                state = block(qb, qi, *init, True)
                if not (first_step and u == 0):
                    state = block(qb, qi - 1, *state, False)
                heads.append((qi, qb, state))
            for u, (qi, qb, state) in enumerate(heads):
                if not (first_step and u == 0):
                    state = lax.while_loop(
                        live, lambda st, qb=qb: (st[0] - 1,) + block(qb, st[0], st[1], st[2], st[3], False),
                        (qi - 2,) + state)[1:]
                finish(u, state[1], state[2])

        pl.when(si == 0)(lambda: whole(True))
        pl.when(si > 0)(lambda: whole(False))
        pl.when((hd == H - 1) & (si == ns - 1))(wait_gather)

    tq = SB_PER_STEP * B
    tile = lambda j: pl.BlockSpec((None, tq, HEAD_DIM), lambda h, i: (j, i, h))
    out_tile = pl.BlockSpec((tq, HEAD_DIM), lambda h, i: (i, h))
    hbm = pl.BlockSpec(memory_space=pltpu.HBM)
    return pl.pallas_call(
        body, name="sb_fwd",
        out_shape=(jax.ShapeDtypeStruct((S, W), BF16), jax.ShapeDtypeStruct((S, W), F32),
                   jax.ShapeDtypeStruct((S, W), F32), jax.ShapeDtypeStruct((N_DEV,) + wo_shard.shape, BF16)),
        grid=(H, ns),
        in_specs=[tile(4), _head_spec(S, 5), _head_spec(S, 6), tile(7),
                  pl.BlockSpec((1, HEAD_DIM), lambda h, i: (0, h)), hbm],
        out_specs=(out_tile, out_tile, out_tile, hbm),
        scratch_shapes=[pltpu.VMEM((S, HEAD_DIM), BF16), pltpu.VMEM((S, HEAD_DIM), BF16),
                        pltpu.SemaphoreType.DMA((N_DEV - 1,)), pltpu.SemaphoreType.DMA((N_DEV - 1,)),
                        pltpu.SemaphoreType.DMA],
        compiler_params=_params("arbitrary", "arbitrary"),
    )(proj, proj, proj, proj, gain, wo_shard)


def _out_proj_loss(mix_r, mix_s, w_out, x, tgt, gf):
    S, W = mix_r.shape
    D = x.shape[1]
    tm = min(256, S)

    def body(mr_ref, ms_ref, wo_ref, x_ref, t_ref, gf_ref, dx2_ref, dx2b_ref, dmix_ref, loss_ref, gfn_ref):
        @pl.when(pl.program_id(0) == 0)
        def _():
            loss_ref[...] = jnp.zeros_like(loss_ref)
            gfn_ref[...] = jnp.zeros_like(gfn_ref)

        gfv = gf_ref[...]
        x2 = x_ref[...] + (_dot(mr_ref[...], wo_ref[:W, :]) + _dot(ms_ref[...], wo_ref[W:, :]))
        r2 = lax.rsqrt(jnp.mean(x2 * x2, axis=-1, keepdims=True) + EPS)
        n = x2 * r2
        err = n * gfv - t_ref[...]
        loss_ref[...] += 0.5 * jnp.sum(jnp.mean(err * err, axis=-1, keepdims=True))
        dy = err * (1.0 / D)
        gfn_ref[...] += jnp.sum(dy * n, axis=0, keepdims=True)
        dn = dy * gfv
        dx2 = r2 * (dn - n * jnp.mean(dn * n, axis=-1, keepdims=True))
        dx2_ref[...] = dx2
        b = dx2.astype(BF16)
        dx2b_ref[...] = b
        dmix_ref[:, :W] = _dot_nt(b, wo_ref[:W, :])
        dmix_ref[:, W:] = _dot_nt(b, wo_ref[W:, :])

    row = lambda width: pl.BlockSpec((tm, width), lambda i: (i, 0))
    return pl.pallas_call(
        body, name="out_proj_loss",
        out_shape=(jax.ShapeDtypeStruct((S, D), F32), jax.ShapeDtypeStruct((S, D), BF16),
                   jax.ShapeDtypeStruct((S, 2 * W), F32), jax.ShapeDtypeStruct((SUBLANES, LANES), F32),
                   jax.ShapeDtypeStruct((1, D), F32)),
        grid=(S // tm,),
        in_specs=[row(W), row(W), pl.BlockSpec((2 * W, D), lambda i: (0, 0)), row(D), row(D),
                  pl.BlockSpec((1, D), lambda i: (0, 0))],
        out_specs=(row(D), row(D), row(2 * W), pl.BlockSpec((SUBLANES, LANES), lambda i: (0, 0)),
                   pl.BlockSpec((1, D), lambda i: (0, 0))),
        compiler_params=_params("arbitrary"),
    )(mix_r, mix_s, w_out, x, tgt, gf)


def _silu_bwd(g, dm, normed):
    sig = _sigmoid(g)
    return dm * (g * sig), dm * normed * (sig * (1.0 + g * (1.0 - sig)))


def _ret_bwd(proj, dmix, tabs, gn_gain, gn_bias, cos, sin, gwo):
    _, S, W = proj.shape
    H, nc = W // HEAD_DIM, S // CHUNK
    assert nc % RET_GROUP == 0
    ng = nc // RET_GROUP
    rows_per_group = RET_GROUP * CHUNK

    def body(q_ref, k_ref, v_ref, g_ref, dm_ref, dec_ref, xi_ref, ze_ref, gam_ref, gain_ref, bias_ref, cos_ref,
             sin_ref, gwo_ref, dp_ref, dgain_ref, dbias_ref, rino_ref, rs_ref, dstates_ref, send_sems, recv_sems):
        dec, xi, ze, gam = dec_ref[...], xi_ref[...], ze_ref[...], gam_ref[...]
        gain, bias = gain_ref[...], bias_ref[...]
        hd = pl.program_id(0)
        other_core = 1 - lax.axis_index("c")
        copies = _to_sibling_copies([gwo_ref.at[2 * k + other_core] for k in range(4)], rino_ref, send_sems, recv_sems)

        @pl.when(hd == 0)
        def _():
            for cp in copies:
                cp.start()

        def group_rows(i):
            return pl.ds(pl.multiple_of(i * rows_per_group, rows_per_group), rows_per_group)

        def fwd_group(i, state):
            rows = group_rows(i)
            kz = (_chunks(k_ref[rows, :]) * ze).astype(BF16)
            return _ret_states(kz, _chunks(v_ref[rows, :]).astype(BF16), gam, state,
                               rs_ref.at[pl.ds(i * RET_GROUP, RET_GROUP)])

        lax.fori_loop(0, ng, fwd_group, jnp.zeros((HEAD_DIM, HEAD_DIM), F32))

        flat = lambda a: a.reshape(rows_per_group, HEAD_DIM)

        def bwd_group(t, carry):
            dgain, dbias, dstate = carry
            i = ng - 1 - t
            rows = group_rows(i)
            q, k, g = _chunks(q_ref[rows, :]), _chunks(k_ref[rows, :]), _chunks(g_ref[rows, :])
            vb = _chunks(v_ref[rows, :]).astype(BF16)
            rb = rs_ref[pl.ds(i * RET_GROUP, RET_GROUP)].astype(BF16)
            out, (qb, kb, sb, qx) = _ret_group(q, k, vb, rb, dec, xi, ze)
            kz = (k * ze).astype(BF16)
            mu = jnp.mean(out, axis=-1, keepdims=True)
            d = out - mu
            rstd = lax.rsqrt(jnp.mean(d * d, axis=-1, keepdims=True) + EPS)
            yn = d * rstd
            dgn, dg = _silu_bwd(g, _chunks(dm_ref[rows, :]), yn * gain + bias)
            dgain = dgain + jnp.sum(flat(dgn * yn), axis=0, keepdims=True)
            dbias = dbias + jnp.sum(flat(dgn), axis=0, keepdims=True)
            dyn = dgn * gain
            do = rstd * (dyn - jnp.mean(dyn, axis=-1, keepdims=True)
                         - yn * jnp.mean(dyn * yn, axis=-1, keepdims=True))
            dob = do.astype(BF16)
            dkv = _bdot_tn(qx, dob)
            for u in reversed(range(RET_GROUP)):
                dstates_ref[u] = dstate
                dstate = gam * dstate + dkv[u]
            drb = dstates_ref[...].astype(BF16)
            dv = _bdot_tn(sb, dob) + _bdot(kz, drb)
            dsb = (_bdot_nt(dob, vb) * dec).astype(BF16)
            dq = _bdot(dsb, kb) + _bdot_nt(dob, rb) * xi
            dk = _bdot_tn(dsb, qb) + _bdot_nt(vb, drb) * ze
            cs, sn = cos_ref[rows, :], -sin_ref[rows, :]
            dp_ref[0, rows, :] = _rot(flat(dq), cs, sn).astype(BF16)
            dp_ref[1, rows, :] = (_rot(flat(dk), cs, sn) * (HEAD_DIM ** -0.5)).astype(BF16)
            dp_ref[2, rows, :] = flat(dv).astype(BF16)
            dp_ref[3, rows, :] = flat(dg).astype(BF16)
            return dgain, dbias, dstate

        zero = jnp.zeros((1, HEAD_DIM), F32)
        dgain, dbias, _ = lax.fori_loop(0, ng, bwd_group, (zero, zero, jnp.zeros((HEAD_DIM, HEAD_DIM), F32)))
        dgain_ref[...] = dgain
        dbias_ref[...] = dbias

        @pl.when(hd == H - 1)
        def _():
            for cp in copies:
                cp.wait_recv()
            for cp in copies:
                cp.wait_send()

    vec = pl.BlockSpec((1, HEAD_DIM), lambda h: (0, h))
    full = pl.BlockSpec((S, HEAD_DIM), lambda h: (0, 0))
    hbm = pl.BlockSpec(memory_space=pltpu.HBM)
    return pl.pallas_call(
        body, name="ret_bwd",
        out_shape=(jax.ShapeDtypeStruct((4, S, W), BF16), jax.ShapeDtypeStruct((1, W), F32),
                   jax.ShapeDtypeStruct((1, W), F32), jax.ShapeDtypeStruct((4,) + gwo.shape[1:], gwo.dtype)),
        grid=(H,),
        in_specs=[_head_spec(S, 0), _head_spec(S, 1), _head_spec(S, 2), _head_spec(S, 3),
                  pl.BlockSpec((S, HEAD_DIM), lambda h: (0, h))] + _table_specs() + [vec, vec, full, full, hbm],
        out_specs=(pl.BlockSpec((4, S, HEAD_DIM), lambda h: (0, 0, h)), vec, vec, hbm),
        scratch_shapes=[pltpu.VMEM((nc, HEAD_DIM, HEAD_DIM), F32), pltpu.VMEM((RET_GROUP, HEAD_DIM, HEAD_DIM), F32),
                        pltpu.SemaphoreType.DMA((4,)), pltpu.SemaphoreType.DMA((4,))],
        compiler_params=_params("arbitrary"),
    )(proj, proj, proj, proj, dmix, *tabs, gn_gain, gn_bias, cos, sin, gwo)


def _sb_bwd(proj, raw, carries, dmix, gain, chip_sums_o):
    _, S, W = proj.shape
    H = W // HEAD_DIM
    B = min(SB_BLOCK, S)
    nq = S // B
    ns = nq // SB_PER_STEP

    def body(q_ref, k_ref, v_ref, g_ref, raw_ref, car_ref, dm_ref, gain_ref, so_ref, dp_ref, dgain_ref, ro_ref,
             kb_ref, vb_ref, dk_ref, dv_ref, send_sems, recv_sems):
        hd, si = pl.program_id(0), pl.program_id(1)
        start_exchange, wait_exchange = _exchange_chip_sums((so_ref,), (ro_ref,), send_sems, recv_sems)
        pl.when((hd == 0) & (si == 0))(start_exchange)

        @pl.when(si == 0)
        def _():
            kb_ref[...] = k_ref[...].astype(BF16)
            vb_ref[...] = v_ref[...].astype(BF16)
            dk_ref[...] = jnp.zeros_like(dk_ref)
            dv_ref[...] = jnp.zeros_like(dv_ref)
            dgain_ref[...] = jnp.zeros_like(dgain_ref)

        causal = _tri(B, "gt")
        upper = _ones_where(causal)
        before = _ones_where(_tri(B, "lt"))
        lane = lax.broadcasted_iota(jnp.int32, (B, HEAD_DIM), 1)
        gain_v = gain_ref[...]

        def prologue(u):
            qi = si * SB_PER_STEP + u
            rows = slice(u * B, (u + 1) * B)
            o = raw_ref[rows, :]
            rstd = lax.rsqrt(jnp.mean(o * o, axis=-1, keepdims=True) + EPS)
            yn = o * rstd
            dnrm, dg = _silu_bwd(g_ref[rows, :], dm_ref[rows, :], yn * gain_v)
            dp_ref[3, pl.ds(pl.multiple_of(qi * B, B), B), :] = dg.astype(BF16)
            dgain_ref[...] += jnp.sum(dnrm * yn, axis=0, keepdims=True)
            dyn = dnrm * gain_v
            do = rstd * (dyn - yn * jnp.mean(dyn * yn, axis=-1, keepdims=True))
            return qi, q_ref[rows, :].astype(BF16), do.astype(BF16), car_ref[rows, :]

        def block(ctx, kb, carry_g, dq, masked):
            _, qb, dob, saved = ctx
            rows = pl.ds(pl.multiple_of(kb * B, B), B)
            kk, vv = kb_ref[rows, :], vb_ref[rows, :]
            log_beta, _, cs = _sb_scores(qb, kk, masked, causal, upper)
            carry_lk = jnp.sum(jnp.where(lane == kb, saved, 0.0), axis=1, keepdims=True)
            a = jnp.exp(log_beta + cs + carry_lk)
            if masked:
                a = jnp.where(causal, a, 0.0)
            gmat = _dot_nt(dob, vv) * a
            dv_ref[rows, :] += _dot_tn(a.astype(BF16), dob)
            hi = gmat.astype(BF16)
            lo = (gmat - hi.astype(F32)).astype(BF16)
            dlk = carry_g + (_dot(hi, before) + _dot(lo, before))
            beta = jnp.exp(log_beta)
            dz = (gmat * (1.0 - beta) - dlk * beta) * (HEAD_DIM ** -0.5)
            if masked:
                dz = jnp.where(causal, dz, 0.0)
            dzb = dz.astype(BF16)
            dk_ref[rows, :] += _dot_tn(dzb, qb)
            return carry_g + jnp.sum(gmat, axis=1, keepdims=True), dq + _dot(dzb, kk)

        init = (jnp.zeros((B, 1), F32), jnp.zeros((B, HEAD_DIM), F32))

        def whole(first_step):
            ctxs = [prologue(u) for u in range(SB_PER_STEP)]
            states = []
            for u, ctx in enumerate(ctxs):
                state = init
                if not (first_step and u == 0):
                    visited = jnp.max(ctx[3], axis=0, keepdims=True) >= EXP_IS_ZERO_BELOW
                    first = jnp.min(jnp.where(visited, lane[:1, :], ctx[0]))
                    state = lax.fori_loop(first, ctx[0] - 1,
                                          lambda i, st, ctx=ctx: block(ctx, i, st[0], st[1], False), state)
                states.append(state)
            for u, (ctx, state) in enumerate(zip(ctxs, states)):
                if not (first_step and u == 0):
                    state = block(ctx, ctx[0] - 1, *state, False)
                state = block(ctx, ctx[0], *state, True)
                dp_ref[0, pl.ds(pl.multiple_of(ctx[0] * B, B), B), :] = state[1].astype(BF16)

        pl.when(si == 0)(lambda: whole(True))
        pl.when(si > 0)(lambda: whole(False))

        @pl.when(si == ns - 1)
        def _():
            dp_ref[1] = dk_ref[...].astype(BF16)
            dp_ref[2] = dv_ref[...].astype(BF16)

        pl.when((hd == H - 1) & (si == ns - 1))(wait_exchange)

    tq = SB_PER_STEP * B
    tile = lambda j: pl.BlockSpec((None, tq, HEAD_DIM), lambda h, i: (j, i, h))
    vec = pl.BlockSpec((1, HEAD_DIM), lambda h, i: (0, h))
    hbm = pl.BlockSpec(memory_space=pltpu.HBM)
    return pl.pallas_call(
        body, name="sb_bwd",
        out_shape=(jax.ShapeDtypeStruct((4, S, W), BF16), jax.ShapeDtypeStruct((1, W), F32),
                   jax.ShapeDtypeStruct((3,) + chip_sums_o.shape[1:], chip_sums_o.dtype)),
        grid=(H, ns),
        in_specs=[tile(4), _head_spec(S, 5), _head_spec(S, 6), tile(7),
                  pl.BlockSpec((tq, HEAD_DIM), lambda h, i: (i, h)),
                  pl.BlockSpec((tq, HEAD_DIM), lambda h, i: (i, h)),
                  pl.BlockSpec((tq, HEAD_DIM), lambda h, i: (i, H + h)), vec, hbm],
        out_specs=(pl.BlockSpec((4, S, HEAD_DIM), lambda h, i: (0, 0, h)), vec, hbm),
        scratch_shapes=[pltpu.VMEM((S, HEAD_DIM), BF16), pltpu.VMEM((S, HEAD_DIM), BF16),
                        pltpu.VMEM((S, HEAD_DIM), F32), pltpu.VMEM((S, HEAD_DIM), F32),
                        pltpu.SemaphoreType.DMA((1, 3)), pltpu.SemaphoreType.DMA((1, 3))],
        compiler_params=_params("arbitrary", "arbitrary"),
    )(proj, proj, proj, proj, raw, carries, dmix, gain, chip_sums_o)


def _grad_w_in_half(ht, dpr, dps, core, name, to_sibling=None):
    D, S = ht.shape
    _, _, W = dpr.shape
    tmm = min(512, D)
    nm = D // tmm

    def body(core_ref, ht_ref, r_ref, s_ref, *rest):
        o_ref = rest[1] if to_sibling is not None else rest[0]
        q, m = pl.program_id(0), pl.program_id(1)
        if to_sibling is not None:
            ga_ref, _, rin_ref, send_sems, recv_sems = rest
            copies = _to_sibling_copies([ga_ref.at[k] for k in range(4)], rin_ref, send_sems, recv_sems)

            @pl.when((q == 0) & (m == 0))
            def _():
                for cp in copies:
                    cp.start()

        @pl.when(q < 2)
        def _():
            o_ref[...] = _dot(ht_ref[...], r_ref[...])

        @pl.when(q >= 2)
        def _():
            o_ref[...] = _dot(ht_ref[...], s_ref[...])

        if to_sibling is not None:
            @pl.when((q == 3) & (m == nm - 1))
            def _():
                for cp in copies:
                    cp.wait_recv()
                for cp in copies:
                    cp.wait_send()

    hbm = pl.BlockSpec(memory_space=pltpu.HBM)
    gw_shape = jax.ShapeDtypeStruct((4, D, W), F32)
    out_shape, out_specs, extra_in, scratch = (gw_shape,), (pl.BlockSpec((None, tmm, W), lambda q, m, core: (q, m, 0)),), [], []
    if to_sibling is not None:
        out_shape += (gw_shape,)
        out_specs += (hbm,)
        extra_in = [hbm]
        scratch = [pltpu.SemaphoreType.DMA((4,)), pltpu.SemaphoreType.DMA((4,))]
    return pl.pallas_call(
        body, name=name, out_shape=out_shape,
        grid_spec=pltpu.PrefetchScalarGridSpec(
            num_scalar_prefetch=1, grid=(4, nm),
            in_specs=[pl.BlockSpec((tmm, S), lambda q, m, core: (m, 0)),
                      pl.BlockSpec((None, S, W), lambda q, m, core: (jnp.minimum(2 * q + core[0], 3), 0, 0)),
                      pl.BlockSpec((None, S, W), lambda q, m, core: (jnp.maximum(2 * q + core[0] - 4, 0), 0, 0))]
            + extra_in,
            out_specs=out_specs, scratch_shapes=scratch),
        compiler_params=_params("arbitrary", "arbitrary"),
    )(core, ht, dpr, dps, *(() if to_sibling is None else (to_sibling,)))


def _grad_w_out(mix_r, mix_s, dx2b):
    S, W = mix_r.shape
    D = dx2b.shape[1]
    tmm = min(512, W)
    tk = min(2048, S)

    def body(r_ref, s_ref, b_ref, o_ref):
        j, kk = pl.program_id(0), pl.program_id(2)

        def acc(a_ref):
            part = _dot_tn(a_ref[...], b_ref[...])

            @pl.when(kk == 0)
            def _():
                o_ref[...] = part

            @pl.when(kk > 0)
            def _():
                o_ref[...] += part

        pl.when(j == 0)(lambda: acc(r_ref))
        pl.when(j == 1)(lambda: acc(s_ref))

    return pl.pallas_call(
        body, name="grad_w_out", out_shape=jax.ShapeDtypeStruct((2, W, D), F32), grid=(2, W // tmm, S // tk),
        in_specs=[pl.BlockSpec((tk, tmm), lambda j, m, k: (k, m)),
                  pl.BlockSpec((tk, tmm), lambda j, m, k: (k, m)),
                  pl.BlockSpec((tk, D), lambda j, m, k: (k, 0))],
        out_specs=pl.BlockSpec((None, tmm, D), lambda j, m, k: (j, m, 0)),
        compiler_params=_params("parallel", "parallel", "arbitrary"),
    )(mix_r, mix_s, dx2b)


def _dh_matmul(dpr, dps, w_all, chip_sums):
    _, S, W = dpr.shape
    D = w_all.shape[1]
    tm = min(1024, S)
    ni = S // tm

    def body(r_ref, s_ref, w_ref, sa_ref, dh_ref, ra_ref, send_sems, recv_sems):
        i, j = pl.program_id(0), pl.program_id(1)
        start_exchange, wait_exchange = _exchange_chip_sums((sa_ref,), (ra_ref,), send_sems, recv_sems)
        pl.when((i == 0) & (j == 0))(start_exchange)

        def acc(b_ref):
            part = _dot_nt(b_ref[...], w_ref[...])

            @pl.when(j == 0)
            def _():
                dh_ref[...] = part

            @pl.when(j > 0)
            def _():
                dh_ref[...] += part

        pl.when(j < 4)(lambda: acc(r_ref))
        pl.when(j >= 4)(lambda: acc(s_ref))
        pl.when((i == ni - 1) & (j == 7))(wait_exchange)

    hbm = pl.BlockSpec(memory_space=pltpu.HBM)
    return pl.pallas_call(
        body, name="dh_matmul",
        out_shape=(jax.ShapeDtypeStruct((S, D), F32), jax.ShapeDtypeStruct((3,) + chip_sums.shape[1:], chip_sums.dtype)),
        grid=(ni, 8),
        in_specs=[pl.BlockSpec((None, tm, W), lambda i, j: (jnp.minimum(j, 3), i, 0)),
                  pl.BlockSpec((None, tm, W), lambda i, j: (jnp.maximum(j - 4, 0), i, 0)),
                  pl.BlockSpec((None, D, W), lambda i, j: (j, 0, 0)), hbm],
        out_specs=(pl.BlockSpec((tm, D), lambda i, j: (i, 0)), hbm),
        scratch_shapes=[pltpu.SemaphoreType.DMA((1, 3)), pltpu.SemaphoreType.DMA((1, 3))],
        compiler_params=_params("arbitrary", "arbitrary"),
    )(dpr, dps, w_all, chip_sums)


def _norm_bwd(x, dx2, dh, gain):
    S, D = x.shape
    tm = min(256, S)

    def body(x_ref, dx2_ref, dh_ref, g_ref, gx_ref, dgain_ref):
        @pl.when(pl.program_id(0) == 0)
        def _():
            dgain_ref[...] = jnp.zeros_like(dgain_ref)

        xv, dh_v = x_ref[...], dh_ref[...]
        r1 = lax.rsqrt(jnp.mean(xv * xv, axis=-1, keepdims=True) + EPS)
        n = xv * r1
        dgain_ref[...] += jnp.sum(dh_v * n, axis=0, keepdims=True)
        dn = dh_v * g_ref[...]
        gx_ref[...] = dx2_ref[...] + r1 * (dn - n * jnp.mean(dn * n, axis=-1, keepdims=True))

    row = pl.BlockSpec((tm, D), lambda i: (i, 0))
    one = pl.BlockSpec((1, D), lambda i: (0, 0))
    return pl.pallas_call(
        body, name="norm_bwd", out_shape=(jax.ShapeDtypeStruct((S, D), F32), jax.ShapeDtypeStruct((1, D), F32)),
        grid=(S // tm,), in_specs=[row, row, row, one], out_specs=(row, one),
        compiler_params=_params("arbitrary"),
    )(x, dx2, dh, gain)


def _own_block(gw, pos, q):
    return q if gw.shape[0] == 4 else 2 * q + pos[0]


def _rs_local_sum(gw, rin, pos):
    _, R, C = gw.shape
    tr = min(1024, R)
    other = lambda k, pos: (pos[1] + 1 + k) % 4

    def body(pos_ref, a_ref, b_ref, o_ref):
        o_ref[...] = (a_ref[...] + b_ref[...]).astype(BF16)

    return pl.pallas_call(
        body, name="rs_local_sum", out_shape=jax.ShapeDtypeStruct((4, R, C), BF16),
        grid_spec=pltpu.PrefetchScalarGridSpec(
            num_scalar_prefetch=1, grid=(3, R // tr),
            in_specs=[pl.BlockSpec((None, tr, C), lambda k, i, pos: (_own_block(gw, pos, other(k, pos)), i, 0)),
                      pl.BlockSpec((None, tr, C), lambda k, i, pos: (other(k, pos), i, 0))],
            out_specs=pl.BlockSpec((None, tr, C), lambda k, i, pos: (other(k, pos), i, 0))),
        compiler_params=_params("parallel", "parallel"),
    )(pos, gw, rin)


def _adamw(w, g, m, v):
    m2 = ADAM_B1 * m + (1.0 - ADAM_B1) * g
    v2 = ADAM_B2 * v + (1.0 - ADAM_B2) * (g * g)
    m_hat = m2 / (1.0 - ADAM_B1 ** ADAM_STEP)
    v_hat = v2 / (1.0 - ADAM_B2 ** ADAM_STEP)
    delta = -ADAM_LR * (m_hat / (jnp.sqrt(v_hat) + ADAM_EPS) + ADAM_WD * w)
    return delta, m2, v2


def _adamw_shard(gw, rin, rb, w, m, v, pos):
    _, R, C = gw.shape
    tr = min(256, R)

    def body(pos_ref, a_ref, b_ref, rb_ref, w_ref, m_ref, v_ref, g_ref, d_ref, m2_ref, v2_ref):
        g = a_ref[...] + b_ref[...]
        for k in range(3):
            g = g + rb_ref[k].astype(F32)
        g_ref[...] = g
        d_ref[...], m2_ref[...], v2_ref[...] = _adamw(w_ref[...], g, m_ref[...], v_ref[...])

    plain = pl.BlockSpec((tr, C), lambda i, pos: (i, 0))
    shape = jax.ShapeDtypeStruct((R, C), F32)
    return pl.pallas_call(
        body, name="adamw_shard", out_shape=(shape,) * 4,
        grid_spec=pltpu.PrefetchScalarGridSpec(
            num_scalar_prefetch=1, grid=(R // tr,),
            in_specs=[pl.BlockSpec((None, tr, C), lambda i, pos: (_own_block(gw, pos, pos[1]), i, 0)),
                      pl.BlockSpec((None, tr, C), lambda i, pos: (pos[1], i, 0)),
                      pl.BlockSpec((3, tr, C), lambda i, pos: (0, i, 0)), plain, plain, plain],
            out_specs=(plain,) * 4),
        compiler_params=_params("parallel"),
    )(pos, gw, rin, rb, w, m, v)


def _adamw_small(parts, w, m, v):
    _, rows, n = parts.shape

    def body(p_ref, w_ref, m_ref, v_ref, g_ref, d_ref, m2_ref, v2_ref):
        g = p_ref[0]
        for d in range(1, N_DEV):
            g = g + p_ref[d]
        g_ref[...] = g
        d_ref[...], m2_ref[...], v2_ref[...] = _adamw(w_ref[...], g, m_ref[...], v_ref[...])

    shape = jax.ShapeDtypeStruct((rows, n), F32)
    return pl.pallas_call(body, name="adamw_small", out_shape=(shape,) * 4)(parts, w, m, v)


def _rope_tables(S):
    half = HEAD_DIM // 2
    inv = ROPE_THETA ** (-jnp.arange(half, dtype=F32) / half)
    ang = jnp.arange(S, dtype=F32)[:, None] * inv[None, :]
    cos, sin = jnp.cos(ang), jnp.sin(ang)
    return jnp.concatenate([cos, cos], axis=1), jnp.concatenate([-sin, sin], axis=1)


def _retention_tables(H):
    lg = jnp.log1p(-jnp.exp2(-5.0 - jnp.arange(H, dtype=F32)))
    n = jnp.arange(CHUNK, dtype=F32)
    rel = n[:, None] - n[None, :]
    decay = jnp.where(rel >= 0, jnp.exp(lg[:, None, None] * jnp.maximum(rel, 0.0)), 0.0)
    shape = (H, CHUNK, HEAD_DIM)
    xi = jnp.broadcast_to(jnp.exp(lg[:, None] * (n + 1.0))[:, :, None], shape)
    zeta = jnp.broadcast_to(jnp.exp(lg[:, None] * (CHUNK - 1.0 - n))[:, :, None], shape)
    gamma_c = jnp.broadcast_to(jnp.exp(lg * CHUNK)[:, None, None], shape)
    return decay, xi, zeta, gamma_c


def _pack_small(parts):
    flat = []
    for p in parts:
        p = p.reshape(-1)
        flat.append(jnp.pad(p, (0, -p.shape[0] % LANES)))
    flat = jnp.concatenate(flat)
    return jnp.pad(flat, (0, SMALL_N - flat.shape[0])).reshape(SUBLANES, SMALL_N // SUBLANES)


def _unpack_small(packed, shapes):
    flat = packed.reshape(-1)
    out, at = [], 0
    for shp in shapes:
        size = 1
        for s in shp:
            size *= s
        out.append(flat[at:at + size].reshape(shp))
        at += size + (-size % LANES)
    return out


def kernel(x, norm_gain, w_in, ret_gn_gain, ret_gn_bias, sb_norm_gain, w_out, final_norm_gain, loss_target, m_norm_gain, m_w_in, m_ret_gn_gain, m_ret_gn_bias, m_sb_norm_gain, m_w_out, m_final_norm_gain, v_norm_gain, v_w_in, v_ret_gn_gain, v_ret_gn_bias, v_sb_norm_gain, v_w_out, v_final_norm_gain):
    S, D = x.shape[1], x.shape[2]
    W = w_in.shape[2]
    wo_rows = w_out.shape[1]
    H = W // HEAD_DIM
    xs, tgt = x[0], loss_target[0]
    mx, my, mc = _mesh_pos()
    pos = jnp.stack([mc, 2 * mx + my]).astype(jnp.int32)

    cos, sin = _rope_tables(S)
    tabs = _retention_tables(H)

    proj, w_all, ht = _in_proj_gather(xs, norm_gain, w_in[0].astype(BF16), cos, sin, _gather_order())
    mix_r = _ret_fwd(proj, tabs, ret_gn_gain, ret_gn_bias)
    mix_s, raw_s, carries, wo_all = _sb_fwd(proj, sb_norm_gain, w_out[0].astype(BF16))
    wo_full = wo_all.reshape(N_DEV * wo_rows, D)
    dx2, dx2b, dmix, loss_p, d_gf = _out_proj_loss(mix_r, mix_s, wo_full, xs, tgt, final_norm_gain[None])

    gwo = _grad_w_out(mix_r, mix_s, dx2b).reshape(N_DEV, wo_rows, D)
    dpr, d_rgain, d_rbias, rino = _ret_bwd(proj, dmix, tabs, ret_gn_gain, ret_gn_bias, cos, sin, gwo)
    dps, d_sgain, rbo = _sb_bwd(proj, raw_s, carries, dmix, sb_norm_gain, _rs_local_sum(gwo, rino, pos))
    gw_sibling, = _grad_w_in_half(ht, dpr, dps, (1 - mc).reshape(1).astype(jnp.int32), "grad_w_in_sibling")
    gw, rin = _grad_w_in_half(ht, dpr, dps, mc.reshape(1).astype(jnp.int32), "grad_w_in_own", to_sibling=gw_sibling)
    dh, rb = _dh_matmul(dpr, dps, w_all, _rs_local_sum(gw, rin, pos))
    grad_x, d_gain = _norm_bwd(xs, dx2, dh, norm_gain)
    g_in, d_in, m_in, v_in = _adamw_shard(gw, rin, rb, w_in[0], m_w_in[0], v_w_in[0], pos)
    g_out, d_out, m_out, v_out = _adamw_shard(gwo, rino, rbo, w_out[0], m_w_out[0], v_w_out[0], pos)

    small_w = [norm_gain, ret_gn_gain, ret_gn_bias, sb_norm_gain, final_norm_gain]
    small_m = [m_norm_gain, m_ret_gn_gain, m_ret_gn_bias, m_sb_norm_gain, m_final_norm_gain]
    small_v = [v_norm_gain, v_ret_gn_gain, v_ret_gn_bias, v_sb_norm_gain, v_final_norm_gain]
    shapes = [()] + [w.shape for w in small_w]
    zero = jnp.zeros((), F32)
    parts = _small_all_gather(_pack_small([loss_p[0, 0], d_gain, d_rgain, d_rbias, d_sgain, d_gf]))
    packed = _adamw_small(parts, _pack_small([zero] + small_w), _pack_small([zero] + small_m),
                          _pack_small([zero] + small_v))
    g_s, d_s, m_s, v_s = (_unpack_small(p, shapes) for p in packed)

    grads = [g_s[1], g_in[None], g_s[2], g_s[3], g_s[4], g_out[None], g_s[5]]
    deltas = [d_s[1], d_in[None], d_s[2], d_s[3], d_s[4], d_out[None], d_s[5]]
    new_m = [m_s[1], m_in[None], m_s[2], m_s[3], m_s[4], m_out[None], m_s[5]]
    new_v = [v_s[1], v_in[None], v_s[2], v_s[3], v_s[4], v_out[None], v_s[5]]
    return (g_s[0], grad_x[None], *grads, *deltas, *new_m, *new_v)
```

```python
import jax
import jax.numpy as jnp
from jax import lax
from jax.experimental import pallas as pl
from jax.experimental.pallas import tpu as pltpu

F32 = jnp.float32
BF16 = jnp.bfloat16

HEAD_DIM = 128
CHUNK = 128
RET_GROUP = 16
ROPE_THETA = 10000.0
EPS = 1e-6
ADAM_LR = 0.001
ADAM_B1 = 0.9
ADAM_B2 = 0.999
ADAM_EPS = 1e-08
ADAM_WD = 0.01
ADAM_STEP = 10

N_DEV = 8
LANES = 128
SUBLANES = 8
VMEM_LIMIT = 56 * 1024 * 1024
SB_BLOCK = 256
SB_PER_STEP = 8
SMALL_N = 8192
EXP_IS_ZERO_BELOW = -104.0
NOT_VISITED = -1e30
MESH = pl.DeviceIdType.MESH

NT = (((1,), (1,)), ((), ()))
TN = (((0,), (0,)), ((), ()))


def _params(*sem):
    return pltpu.CompilerParams(dimension_semantics=sem if sem else None, vmem_limit_bytes=VMEM_LIMIT)


def _dot(a, b):
    return jnp.dot(a, b, preferred_element_type=F32)


def _dot_nt(a, b):
    return lax.dot_general(a, b, NT, preferred_element_type=F32)


def _dot_tn(a, b):
    return lax.dot_general(a, b, TN, preferred_element_type=F32)


def _sigmoid(g):
    return 1.0 / (1.0 + jnp.exp(-g))


def _rot(a, cos, sin_signed):
    return a * cos + pltpu.roll(a, HEAD_DIM // 2, 1) * sin_signed


def _mesh_pos():
    return lax.axis_index("x"), lax.axis_index("y"), lax.axis_index("c")


def _to_sibling_copies(blocks, out_ref, send_sems, recv_sems):
    x, y, c = _mesh_pos()
    return [pltpu.make_async_remote_copy(
        src_ref=block, dst_ref=out_ref.at[k], send_sem=send_sems.at[k], recv_sem=recv_sems.at[k],
        device_id=(x, y, 1 - c), device_id_type=MESH) for k, block in enumerate(blocks)]


def _exchange_chip_sums(srcs, outs, send_sems, recv_sems):
    x, y, c = _mesh_pos()
    copies = []
    for arr, (src, out) in enumerate(zip(srcs, outs)):
        for k in range(1, 4):
            px = 1 - x if k & 2 else x
            py = 1 - y if k & 1 else y
            copies.append(pltpu.make_async_remote_copy(
                src_ref=src.at[2 * px + py], dst_ref=out.at[k - 1],
                send_sem=send_sems.at[arr, k - 1], recv_sem=recv_sems.at[arr, k - 1],
                device_id=(px, py, c), device_id_type=MESH))

    def start():
        for cp in copies:
            cp.start()

    def wait():
        for cp in copies:
            cp.wait_recv()
        for cp in copies:
            cp.wait_send()

    return start, wait


def _small_all_gather(small):
    rows, n = small.shape

    def body(s_ref, o_ref, send_sems, recv_sems, local_sem):
        start, wait = _exchange_with_all(s_ref, o_ref, send_sems, recv_sems, local_sem)
        start()
        wait()

    vmem = pl.BlockSpec(memory_space=pltpu.VMEM)
    return pl.pallas_call(
        body, name="small_all_gather",
        out_shape=jax.ShapeDtypeStruct((N_DEV, rows, n), small.dtype),
        in_specs=[vmem], out_specs=vmem,
        scratch_shapes=[pltpu.SemaphoreType.DMA((N_DEV - 1,)), pltpu.SemaphoreType.DMA((N_DEV - 1,)),
                        pltpu.SemaphoreType.DMA],
    )(small)


GATHER_SPLIT = 2
GATHER_STEPS = ([("own", 0, p) for p in range(GATHER_SPLIT)] + [("sibling", 0, p) for p in range(GATHER_SPLIT)]
                + [step for p in range(GATHER_SPLIT) for step in
                   (("ici", 0, p), ("ici", 1, p), ("passed", 0, p), ("passed", 1, p))]
                + [step for p in range(GATHER_SPLIT) for step in (("ici", 2, p), ("passed", 2, p))])


def _via_x(p):
    return p % 2 == 0


def _gather_order():
    x, y, c = _mesh_pos()
    chips = [(1 - x, y), (x, 1 - y), (1 - x, 1 - y)]
    owner = {"own": lambda j: (x, y, c), "sibling": lambda j: (x, y, 1 - c),
             "ici": lambda j: (*chips[j], c), "passed": lambda j: (*chips[j], 1 - c)}
    blocks = [4 * px + 2 * py + pc for px, py, pc in (owner[kind](j) for kind, j, _ in GATHER_STEPS)]
    return (jnp.stack(blocks).astype(jnp.int32), jnp.array([p for _, _, p in GATHER_STEPS], jnp.int32))


def _in_proj_gather(x, gain, w_shard, cos, sin, order):
    S, D = x.shape
    W = w_shard.shape[1]
    wp = W // GATHER_SPLIT
    tm = min(1024, S)
    ni = S // tm
    n_steps = len(GATHER_STEPS)

    def body(blk_ref, piece_ref, x_ref, g_ref, w_ref, cos_ref, sin_ref, o_ref, wall_ref, ht_ref, h_scr, wbuf,
             send_sems, recv_sems, local_sem, load_sems):
        step, i = pl.program_id(0), pl.program_id(1)
        mx, my, c = _mesh_pos()
        me, sibling = (mx, my, c), (mx, my, 1 - c)
        chips = [(1 - mx, my), (mx, 1 - my), (1 - mx, 1 - my)]

        def piece_of(dev, p):
            px, py, pc = dev
            return wall_ref.at[4 * px + 2 * py + pc, :, pl.ds(p * wp, wp)]

        def copy(k, p, block, to, own=False):
            dst = piece_of(block, p)
            return pltpu.make_async_remote_copy(
                src_ref=w_ref.at[:, pl.ds(p * wp, wp)] if own else dst, dst_ref=dst,
                send_sem=send_sems.at[k, p], recv_sem=recv_sems.at[k, p], device_id=to, device_id_type=MESH)

        pieces = range(GATHER_SPLIT)
        first = [cp for p in pieces for cp in
                 [copy(0, p, me, sibling, own=True)] + [copy(1 + j, p, me, (*chips[j], c), own=True) for j in (0, 1)]]
        passed = {(j, p): copy(4 + j, p, (*chip, c), sibling) for j, chip in enumerate(chips) for p in pieces}
        onward = {p: copy(3, p, (*chips[0 if _via_x(p) else 1], c), (*chips[1 if _via_x(p) else 0], c)) for p in pieces}
        mine = pltpu.make_async_copy(w_ref, wall_ref.at[4 * mx + 2 * my + c], local_sem)

        def load(s):
            kind, j, p = GATHER_STEPS[s]
            if kind == "own":
                src = w_ref.at[:, pl.ds(p * wp, wp)]
            elif kind == "sibling":
                copy(0, p, sibling, me).wait_recv()
                src = piece_of(sibling, p)
            elif kind == "ici":
                copy(1 + j, p, (*chips[j], c), me).wait_recv()
                if j == (0 if _via_x(p) else 1):
                    onward[p].start()
                passed[j, p].start()
                src = piece_of((*chips[j], c), p)
            else:
                copy(4 + j, p, (*chips[j], 1 - c), me).wait_recv()
                src = piece_of((*chips[j], 1 - c), p)
            return pltpu.make_async_copy(src, wbuf.at[s % 2], load_sems.at[s % 2])

        for s in range(n_steps):
            @pl.when((step == s) & (i == 0))
            def _(s=s):
                if s == 0:
                    for cp in first:
                        cp.start()
                    mine.start()
                    load(0).start()
                pltpu.make_async_copy(w_ref.at[:, pl.ds(0, wp)], wbuf.at[s % 2], load_sems.at[s % 2]).wait()

            if s + 1 < n_steps:
                @pl.when((step == s) & (i == ni - 1))
                def _(s=s):
                    load(s + 1).start()

        rows = pl.ds(pl.multiple_of(i * tm, tm), tm)

        @pl.when(step == 0)
        def _():
            xv = x_ref[...]
            r = lax.rsqrt(jnp.mean(xv * xv, axis=-1, keepdims=True) + EPS)
            hv = xv * r * g_ref[...]
            h_scr[rows, :] = hv.astype(BF16)
            ht_ref[...] = hv.T.astype(BF16)

        acc = _dot(h_scr[rows, :], wbuf[step % 2])
        b = blk_ref[step]

        @pl.when(b >= 2)
        def _():
            o_ref[...] = acc

        @pl.when(b < 2)
        def _():
            scale = jnp.where(b == 1, HEAD_DIM ** -0.5, 1.0).astype(F32)
            cs, sn = cos_ref[...], sin_ref[...]
            for hh in range(wp // HEAD_DIM):
                cols = slice(hh * HEAD_DIM, (hh + 1) * HEAD_DIM)
                o_ref[:, cols] = _rot(acc[:, cols], cs, sn) * scale

        @pl.when((step == n_steps - 1) & (i == ni - 1))
        def _():
            for cp in first + list(passed.values()) + list(onward.values()):
                cp.wait_send()
            mine.wait()

    hbm = pl.BlockSpec(memory_space=pltpu.HBM)
    rope = pl.BlockSpec((tm, HEAD_DIM), lambda s, i, blk, piece: (i, 0))
    first_pass = lambda s, i: jnp.where(s == 0, i, ni - 1)
    return pl.pallas_call(
        body, name="in_proj_gather",
        out_shape=(jax.ShapeDtypeStruct((N_DEV, S, W), F32), jax.ShapeDtypeStruct((N_DEV, D, W), BF16),
                   jax.ShapeDtypeStruct((D, S), BF16)),
        grid_spec=pltpu.PrefetchScalarGridSpec(
            num_scalar_prefetch=2, grid=(n_steps, ni),
            in_specs=[pl.BlockSpec((tm, D), lambda s, i, blk, piece: (first_pass(s, i), 0)),
                      pl.BlockSpec((1, D), lambda s, i, blk, piece: (0, 0)), hbm, rope, rope],
            out_specs=(pl.BlockSpec((None, tm, wp), lambda s, i, blk, piece: (blk[s], i, piece[s])), hbm,
                       pl.BlockSpec((D, tm), lambda s, i, blk, piece: (0, first_pass(s, i)))),
            scratch_shapes=[pltpu.VMEM((S, D), BF16), pltpu.VMEM((2, D, wp), BF16),
                            pltpu.SemaphoreType.DMA((7, GATHER_SPLIT)), pltpu.SemaphoreType.DMA((7, GATHER_SPLIT)),
                            pltpu.SemaphoreType.DMA, pltpu.SemaphoreType.DMA((2,))]),
        compiler_params=_params("arbitrary", "arbitrary"),
    )(*order, x, gain, w_shard, cos, sin)


def _head_spec(S, j):
    return pl.BlockSpec((None, S, HEAD_DIM), lambda h, *_: (j, 0, h))


def _bdot(a, b):
    return lax.dot_general(a, b, (((2,), (1,)), ((0,), (0,))), preferred_element_type=F32)


def _bdot_nt(a, b):
    return lax.dot_general(a, b, (((2,), (2,)), ((0,), (0,))), preferred_element_type=F32)


def _bdot_tn(a, b):
    return lax.dot_general(a, b, (((1,), (1,)), ((0,), (0,))), preferred_element_type=F32)


def _chunks(a):
    return a.reshape(a.shape[0] // CHUNK, CHUNK, a.shape[1])


def _ret_group(q, k, vb, states_b, dec, xi, ze):
    qb, kb = q.astype(BF16), k.astype(BF16)
    sb = (_bdot_nt(qb, kb) * dec).astype(BF16)
    qx = (q * xi).astype(BF16)
    out = _bdot(sb, vb) + _bdot(qx, states_b)
    return out, (qb, kb, sb, qx)


def _ret_states(kz, vb, gam, state, states_ref):
    kv = _bdot_tn(kz, vb)
    for u in range(RET_GROUP):
        states_ref[u] = state
        state = gam * state + kv[u]
    return state


def _table_specs():
    return [pl.BlockSpec((None, CHUNK, HEAD_DIM), lambda h, *_: (h, 0, 0))] * 4


def _ret_fwd(proj, tabs, gn_gain, gn_bias):
    _, S, W = proj.shape
    H, nc = W // HEAD_DIM, S // CHUNK
    assert nc % RET_GROUP == 0
    rows_per_group = RET_GROUP * CHUNK

    def body(q_ref, k_ref, v_ref, g_ref, dec_ref, xi_ref, ze_ref, gam_ref, gain_ref, bias_ref, o_ref, states_ref):
        dec, xi, ze, gam = dec_ref[...], xi_ref[...], ze_ref[...], gam_ref[...]
        gain, bias = gain_ref[...], bias_ref[...]

        def group(i, state):
            rows = pl.ds(pl.multiple_of(i * rows_per_group, rows_per_group), rows_per_group)
            q, k, vb = _chunks(q_ref[rows, :]), _chunks(k_ref[rows, :]), _chunks(v_ref[rows, :]).astype(BF16)
            state = _ret_states((k * ze).astype(BF16), vb, gam, state, states_ref)
            out, _ = _ret_group(q, k, vb, states_ref[...].astype(BF16), dec, xi, ze)
            mu = jnp.mean(out, axis=-1, keepdims=True)
            d = out - mu
            yn = d * lax.rsqrt(jnp.mean(d * d, axis=-1, keepdims=True) + EPS)
            g = _chunks(g_ref[rows, :])
            mix = g * _sigmoid(g) * (yn * gain + bias)
            o_ref[rows, :] = mix.reshape(rows_per_group, HEAD_DIM).astype(BF16)
            return state

        lax.fori_loop(0, nc // RET_GROUP, group, jnp.zeros((HEAD_DIM, HEAD_DIM), F32))

    vec = pl.BlockSpec((1, HEAD_DIM), lambda h: (0, h))
    return pl.pallas_call(
        body, name="ret_fwd", out_shape=jax.ShapeDtypeStruct((S, W), BF16), grid=(H,),
        in_specs=[_head_spec(S, 0), _head_spec(S, 1), _head_spec(S, 2), _head_spec(S, 3)] + _table_specs() + [vec, vec],
        out_specs=pl.BlockSpec((S, HEAD_DIM), lambda h: (0, h)),
        scratch_shapes=[pltpu.VMEM((RET_GROUP, HEAD_DIM, HEAD_DIM), F32)],
        compiler_params=_params("parallel"),
    )(proj, proj, proj, proj, *tabs, gn_gain, gn_bias)


def _sb_scores(qb, kk, masked, causal, upper):
    z = _dot_nt(qb, kk) * (HEAD_DIM ** -0.5)
    e = jnp.exp(-jnp.abs(z))
    l1p = jnp.log(1.0 + e)
    log_beta = jnp.minimum(z, 0.0) - l1p
    lk = jnp.minimum(-z, 0.0) - l1p
    if masked:
        lk = jnp.where(causal, lk, 0.0)
    hi = lk.astype(BF16)
    lo = (lk - hi.astype(F32)).astype(BF16)
    cs = _dot(hi, upper) + _dot(lo, upper)
    return log_beta, lk, cs


def _tri(B, kind):
    r = lax.broadcasted_iota(jnp.int32, (B, B), 0)
    c = lax.broadcasted_iota(jnp.int32, (B, B), 1)
    return {"gt": r > c, "lt": r < c}[kind]


def _ones_where(mask):
    return jnp.where(mask, 1.0, 0.0).astype(BF16)


def _exchange_with_all(src_ref, out_ref, send_sems, recv_sems, local_sem):
    x, y, c = _mesh_pos()
    peers = [(1 - x if k & 4 else x, 1 - y if k & 2 else y, 1 - c if k & 1 else c) for k in range(1, N_DEV)]

    def copy(k, owner, to):
        px, py, pc = owner
        return pltpu.make_async_remote_copy(
            src_ref=src_ref, dst_ref=out_ref.at[4 * px + 2 * py + pc], send_sem=send_sems.at[k],
            recv_sem=recv_sems.at[k], device_id=to, device_id_type=MESH)

    sends = [copy(k, (x, y, c), p) for k, p in enumerate(peers)]
    mine = pltpu.make_async_copy(src_ref, out_ref.at[4 * x + 2 * y + c], local_sem)

    def start():
        for cp in sends:
            cp.start()
        mine.start()

    def wait():
        for k, p in enumerate(peers):
            copy(k, p, p).wait_recv()
        for cp in sends:
            cp.wait_send()
        mine.wait()

    return start, wait


def _sb_fwd(proj, gain, wo_shard):
    _, S, W = proj.shape
    H = W // HEAD_DIM
    B = min(SB_BLOCK, S)
    nq = S // B
    assert nq <= HEAD_DIM and nq % SB_PER_STEP == 0
    ns = nq // SB_PER_STEP

    def body(q_ref, k_ref, v_ref, g_ref, gain_ref, wo_ref, mix_ref, raw_ref, car_ref, woall_ref, kb_ref, vb_ref,
             send_sems, recv_sems, local_sem):
        hd, si = pl.program_id(0), pl.program_id(1)
        start_gather, wait_gather = _exchange_with_all(wo_ref, woall_ref, send_sems, recv_sems, local_sem)
        pl.when((hd == 0) & (si == 0))(start_gather)

        @pl.when(si == 0)
        def _():
            kb_ref[...] = k_ref[...].astype(BF16)
            vb_ref[...] = v_ref[...].astype(BF16)

        causal = _tri(B, "gt")
        upper = _ones_where(causal)
        lane = lax.broadcasted_iota(jnp.int32, (B, HEAD_DIM), 1)

        def block(qb, kb, carry, acc, saved, masked):
            rows = pl.ds(pl.multiple_of(kb * B, B), B)
            log_beta, lk, cs = _sb_scores(qb, kb_ref[rows, :], masked, causal, upper)
            a = jnp.exp(log_beta + cs + carry)
            if masked:
                a = jnp.where(causal, a, 0.0)
            acc = acc + _dot(a.astype(BF16), vb_ref[rows, :])
            return carry + jnp.sum(lk, axis=1, keepdims=True), acc, jnp.where(lane == kb, carry, saved)

        init = (jnp.zeros((B, 1), F32), jnp.zeros((B, HEAD_DIM), F32), jnp.full((B, HEAD_DIM), NOT_VISITED, F32))

        def live(st):
            return (st[0] >= 0) & (jnp.max(st[1]) >= EXP_IS_ZERO_BELOW)

        def finish(u, acc, saved):
            rows = slice(u * B, (u + 1) * B)
            raw_ref[rows, :] = acc
            car_ref[rows, :] = saved
            yn = acc * lax.rsqrt(jnp.mean(acc * acc, axis=-1, keepdims=True) + EPS)
            g = g_ref[rows, :]
            mix_ref[rows, :] = (g * _sigmoid(g) * (yn * gain_ref[...])).astype(BF16)

        def whole(first_step):
            heads = []
            for u in range(SB_PER_STEP):
                qi = si * SB_PER_STEP + u
                qb = q_ref[u * B:(u + 1) * B, :].astype(BF16)
                state = block(qb, qi, *init, True)
                if not (first_step and u == 0):
                    state = block(qb, qi - 1, *state, False)
                heads.append((qi, qb, state))
            for u, (qi, qb, state) in enumerate(heads):
                if not (first_step and u == 0):
                    state = lax.while_loop(
                        live, lambda st, qb=qb: (st[0] - 1,) + block(qb, st[0], st[1], st[2], st[3], False),
                        (qi - 2,) + state)[1:]
                finish(u, state[1], state[2])

        pl.when(si == 0)(lambda: whole(True))
        pl.when(si > 0)(lambda: whole(False))
        pl.when((hd == H - 1) & (si == ns - 1))(wait_gather)

    tq = SB_PER_STEP * B
    tile = lambda j: pl.BlockSpec((None, tq, HEAD_DIM), lambda h, i: (j, i, h))
    out_tile = pl.BlockSpec((tq, HEAD_DIM), lambda h, i: (i, h))
    hbm = pl.BlockSpec(memory_space=pltpu.HBM)
    return pl.pallas_call(
        body, name="sb_fwd",
        out_shape=(jax.ShapeDtypeStruct((S, W), BF16), jax.ShapeDtypeStruct((S, W), F32),
                   jax.ShapeDtypeStruct((S, W), F32), jax.ShapeDtypeStruct((N_DEV,) + wo_shard.shape, BF16)),
        grid=(H, ns),
        in_specs=[tile(4), _head_spec(S, 5), _head_spec(S, 6), tile(7),
                  pl.BlockSpec((1, HEAD_DIM), lambda h, i: (0, h)), hbm],
        out_specs=(out_tile, out_tile, out_tile, hbm),
        scratch_shapes=[pltpu.VMEM((S, HEAD_DIM), BF16), pltpu.VMEM((S, HEAD_DIM), BF16),
                        pltpu.SemaphoreType.DMA((N_DEV - 1,)), pltpu.SemaphoreType.DMA((N_DEV - 1,)),
                        pltpu.SemaphoreType.DMA],
        compiler_params=_params("arbitrary", "arbitrary"),
    )(proj, proj, proj, proj, gain, wo_shard)


def _out_proj_loss(mix_r, mix_s, w_out, x, tgt, gf):
    S, W = mix_r.shape
    D = x.shape[1]
    tm = min(256, S)

    def body(mr_ref, ms_ref, wo_ref, x_ref, t_ref, gf_ref, dx2_ref, dx2b_ref, dmix_ref, loss_ref, gfn_ref):
        @pl.when(pl.program_id(0) == 0)
        def _():
            loss_ref[...] = jnp.zeros_like(loss_ref)
            gfn_ref[...] = jnp.zeros_like(gfn_ref)

        gfv = gf_ref[...]
        x2 = x_ref[...] + (_dot(mr_ref[...], wo_ref[:W, :]) + _dot(ms_ref[...], wo_ref[W:, :]))
        r2 = lax.rsqrt(jnp.mean(x2 * x2, axis=-1, keepdims=True) + EPS)
        n = x2 * r2
        err = n * gfv - t_ref[...]
        loss_ref[...] += 0.5 * jnp.sum(jnp.mean(err * err, axis=-1, keepdims=True))
        dy = err * (1.0 / D)
        gfn_ref[...] += jnp.sum(dy * n, axis=0, keepdims=True)
        dn = dy * gfv
        dx2 = r2 * (dn - n * jnp.mean(dn * n, axis=-1, keepdims=True))
        dx2_ref[...] = dx2
        b = dx2.astype(BF16)
        dx2b_ref[...] = b
        dmix_ref[:, :W] = _dot_nt(b, wo_ref[:W, :])
        dmix_ref[:, W:] = _dot_nt(b, wo_ref[W:, :])

    row = lambda width: pl.BlockSpec((tm, width), lambda i: (i, 0))
    return pl.pallas_call(
        body, name="out_proj_loss",
        out_shape=(jax.ShapeDtypeStruct((S, D), F32), jax.ShapeDtypeStruct((S, D), BF16),
                   jax.ShapeDtypeStruct((S, 2 * W), F32), jax.ShapeDtypeStruct((SUBLANES, LANES), F32),
                   jax.ShapeDtypeStruct((1, D), F32)),
        grid=(S // tm,),
        in_specs=[row(W), row(W), pl.BlockSpec((2 * W, D), lambda i: (0, 0)), row(D), row(D),
                  pl.BlockSpec((1, D), lambda i: (0, 0))],
        out_specs=(row(D), row(D), row(2 * W), pl.BlockSpec((SUBLANES, LANES), lambda i: (0, 0)),
                   pl.BlockSpec((1, D), lambda i: (0, 0))),
        compiler_params=_params("arbitrary"),
    )(mix_r, mix_s, w_out, x, tgt, gf)


def _silu_bwd(g, dm, normed):
    sig = _sigmoid(g)
    return dm * (g * sig), dm * normed * (sig * (1.0 + g * (1.0 - sig)))


def _ret_bwd(proj, dmix, tabs, gn_gain, gn_bias, cos, sin, gwo):
    _, S, W = proj.shape
    H, nc = W // HEAD_DIM, S // CHUNK
    assert nc % RET_GROUP == 0
    ng = nc // RET_GROUP
    rows_per_group = RET_GROUP * CHUNK

    def body(q_ref, k_ref, v_ref, g_ref, dm_ref, dec_ref, xi_ref, ze_ref, gam_ref, gain_ref, bias_ref, cos_ref,
             sin_ref, gwo_ref, dp_ref, dgain_ref, dbias_ref, rino_ref, rs_ref, dstates_ref, send_sems, recv_sems):
        dec, xi, ze, gam = dec_ref[...], xi_ref[...], ze_ref[...], gam_ref[...]
        gain, bias = gain_ref[...], bias_ref[...]
        hd = pl.program_id(0)
        other_core = 1 - lax.axis_index("c")
        copies = _to_sibling_copies([gwo_ref.at[2 * k + other_core] for k in range(4)], rino_ref, send_sems, recv_sems)

        @pl.when(hd == 0)
        def _():
            for cp in copies:
                cp.start()

        def group_rows(i):
            return pl.ds(pl.multiple_of(i * rows_per_group, rows_per_group), rows_per_group)

        def fwd_group(i, state):
            rows = group_rows(i)
            kz = (_chunks(k_ref[rows, :]) * ze).astype(BF16)
            return _ret_states(kz, _chunks(v_ref[rows, :]).astype(BF16), gam, state,
                               rs_ref.at[pl.ds(i * RET_GROUP, RET_GROUP)])

        lax.fori_loop(0, ng, fwd_group, jnp.zeros((HEAD_DIM, HEAD_DIM), F32))

        flat = lambda a: a.reshape(rows_per_group, HEAD_DIM)

        def bwd_group(t, carry):
            dgain, dbias, dstate = carry
            i = ng - 1 - t
            rows = group_rows(i)
            q, k, g = _chunks(q_ref[rows, :]), _chunks(k_ref[rows, :]), _chunks(g_ref[rows, :])
            vb = _chunks(v_ref[rows, :]).astype(BF16)
            rb = rs_ref[pl.ds(i * RET_GROUP, RET_GROUP)].astype(BF16)
            out, (qb, kb, sb, qx) = _ret_group(q, k, vb, rb, dec, xi, ze)
            kz = (k * ze).astype(BF16)
            mu = jnp.mean(out, axis=-1, keepdims=True)
            d = out - mu
            rstd = lax.rsqrt(jnp.mean(d * d, axis=-1, keepdims=True) + EPS)
            yn = d * rstd
            dgn, dg = _silu_bwd(g, _chunks(dm_ref[rows, :]), yn * gain + bias)
            dgain = dgain + jnp.sum(flat(dgn * yn), axis=0, keepdims=True)
            dbias = dbias + jnp.sum(flat(dgn), axis=0, keepdims=True)
            dyn = dgn * gain
            do = rstd * (dyn - jnp.mean(dyn, axis=-1, keepdims=True)
                         - yn * jnp.mean(dyn * yn, axis=-1, keepdims=True))
            dob = do.astype(BF16)
            dkv = _bdot_tn(qx, dob)
            for u in reversed(range(RET_GROUP)):
                dstates_ref[u] = dstate
                dstate = gam * dstate + dkv[u]
            drb = dstates_ref[...].astype(BF16)
            dv = _bdot_tn(sb, dob) + _bdot(kz, drb)
            dsb = (_bdot_nt(dob, vb) * dec).astype(BF16)
            dq = _bdot(dsb, kb) + _bdot_nt(dob, rb) * xi
            dk = _bdot_tn(dsb, qb) + _bdot_nt(vb, drb) * ze
            cs, sn = cos_ref[rows, :], -sin_ref[rows, :]
            dp_ref[0, rows, :] = _rot(flat(dq), cs, sn).astype(BF16)
            dp_ref[1, rows, :] = (_rot(flat(dk), cs, sn) * (HEAD_DIM ** -0.5)).astype(BF16)
            dp_ref[2, rows, :] = flat(dv).astype(BF16)
            dp_ref[3, rows, :] = flat(dg).astype(BF16)
            return dgain, dbias, dstate

        zero = jnp.zeros((1, HEAD_DIM), F32)
        dgain, dbias, _ = lax.fori_loop(0, ng, bwd_group, (zero, zero, jnp.zeros((HEAD_DIM, HEAD_DIM), F32)))
        dgain_ref[...] = dgain
        dbias_ref[...] = dbias

        @pl.when(hd == H - 1)
        def _():
            for cp in copies:
                cp.wait_recv()
            for cp in copies:
                cp.wait_send()

    vec = pl.BlockSpec((1, HEAD_DIM), lambda h: (0, h))
    full = pl.BlockSpec((S, HEAD_DIM), lambda h: (0, 0))
    hbm = pl.BlockSpec(memory_space=pltpu.HBM)
    return pl.pallas_call(
        body, name="ret_bwd",
        out_shape=(jax.ShapeDtypeStruct((4, S, W), BF16), jax.ShapeDtypeStruct((1, W), F32),
                   jax.ShapeDtypeStruct((1, W), F32), jax.ShapeDtypeStruct((4,) + gwo.shape[1:], gwo.dtype)),
        grid=(H,),
        in_specs=[_head_spec(S, 0), _head_spec(S, 1), _head_spec(S, 2), _head_spec(S, 3),
                  pl.BlockSpec((S, HEAD_DIM), lambda h: (0, h))] + _table_specs() + [vec, vec, full, full, hbm],
        out_specs=(pl.BlockSpec((4, S, HEAD_DIM), lambda h: (0, 0, h)), vec, vec, hbm),
        scratch_shapes=[pltpu.VMEM((nc, HEAD_DIM, HEAD_DIM), F32), pltpu.VMEM((RET_GROUP, HEAD_DIM, HEAD_DIM), F32),
                        pltpu.SemaphoreType.DMA((4,)), pltpu.SemaphoreType.DMA((4,))],
        compiler_params=_params("arbitrary"),
    )(proj, proj, proj, proj, dmix, *tabs, gn_gain, gn_bias, cos, sin, gwo)


def _sb_bwd(proj, raw, carries, dmix, gain, chip_sums_o):
    _, S, W = proj.shape
    H = W // HEAD_DIM
    B = min(SB_BLOCK, S)
    nq = S // B
    ns = nq // SB_PER_STEP

    def body(q_ref, k_ref, v_ref, g_ref, raw_ref, car_ref, dm_ref, gain_ref, so_ref, dp_ref, dgain_ref, ro_ref,
             kb_ref, vb_ref, dk_ref, dv_ref, send_sems, recv_sems):
        hd, si = pl.program_id(0), pl.program_id(1)
        start_exchange, wait_exchange = _exchange_chip_sums((so_ref,), (ro_ref,), send_sems, recv_sems)
        pl.when((hd == 0) & (si == 0))(start_exchange)

        @pl.when(si == 0)
        def _():
            kb_ref[...] = k_ref[...].astype(BF16)
            vb_ref[...] = v_ref[...].astype(BF16)
            dk_ref[...] = jnp.zeros_like(dk_ref)
            dv_ref[...] = jnp.zeros_like(dv_ref)
            dgain_ref[...] = jnp.zeros_like(dgain_ref)

        causal = _tri(B, "gt")
        upper = _ones_where(causal)
        before = _ones_where(_tri(B, "lt"))
        lane = lax.broadcasted_iota(jnp.int32, (B, HEAD_DIM), 1)
        gain_v = gain_ref[...]

        def prologue(u):
            qi = si * SB_PER_STEP + u
            rows = slice(u * B, (u + 1) * B)
            o = raw_ref[rows, :]
            rstd = lax.rsqrt(jnp.mean(o * o, axis=-1, keepdims=True) + EPS)
            yn = o * rstd
            dnrm, dg = _silu_bwd(g_ref[rows, :], dm_ref[rows, :], yn * gain_v)
            dp_ref[3, pl.ds(pl.multiple_of(qi * B, B), B), :] = dg.astype(BF16)
            dgain_ref[...] += jnp.sum(dnrm * yn, axis=0, keepdims=True)
            dyn = dnrm * gain_v
            do = rstd * (dyn - yn * jnp.mean(dyn * yn, axis=-1, keepdims=True))
            return qi, q_ref[rows, :].astype(BF16), do.astype(BF16), car_ref[rows, :]

        def block(ctx, kb, carry_g, dq, masked):
            _, qb, dob, saved = ctx
            rows = pl.ds(pl.multiple_of(kb * B, B), B)
            kk, vv = kb_ref[rows, :], vb_ref[rows, :]
            log_beta, _, cs = _sb_scores(qb, kk, masked, causal, upper)
            carry_lk = jnp.sum(jnp.where(lane == kb, saved, 0.0), axis=1, keepdims=True)
            a = jnp.exp(log_beta + cs + carry_lk)
            if masked:
                a = jnp.where(causal, a, 0.0)
            gmat = _dot_nt(dob, vv) * a
            dv_ref[rows, :] += _dot_tn(a.astype(BF16), dob)
            hi = gmat.astype(BF16)
            lo = (gmat - hi.astype(F32)).astype(BF16)
            dlk = carry_g + (_dot(hi, before) + _dot(lo, before))
            beta = jnp.exp(log_beta)
            dz = (gmat * (1.0 - beta) - dlk * beta) * (HEAD_DIM ** -0.5)
            if masked:
                dz = jnp.where(causal, dz, 0.0)
            dzb = dz.astype(BF16)
            dk_ref[rows, :] += _dot_tn(dzb, qb)
            return carry_g + jnp.sum(gmat, axis=1, keepdims=True), dq + _dot(dzb, kk)

        init = (jnp.zeros((B, 1), F32), jnp.zeros((B, HEAD_DIM), F32))

        def whole(first_step):
            ctxs = [prologue(u) for u in range(SB_PER_STEP)]
            states = []
            for u, ctx in enumerate(ctxs):
                state = init
                if not (first_step and u == 0):
                    visited = jnp.max(ctx[3], axis=0, keepdims=True) >= EXP_IS_ZERO_BELOW
                    first = jnp.min(jnp.where(visited, lane[:1, :], ctx[0]))
                    state = lax.fori_loop(first, ctx[0] - 1,
                                          lambda i, st, ctx=ctx: block(ctx, i, st[0], st[1], False), state)
                states.append(state)
            for u, (ctx, state) in enumerate(zip(ctxs, states)):
                if not (first_step and u == 0):
                    state = block(ctx, ctx[0] - 1, *state, False)
                state = block(ctx, ctx[0], *state, True)
                dp_ref[0, pl.ds(pl.multiple_of(ctx[0] * B, B), B), :] = state[1].astype(BF16)

        pl.when(si == 0)(lambda: whole(True))
        pl.when(si > 0)(lambda: whole(False))

        @pl.when(si == ns - 1)
        def _():
            dp_ref[1] = dk_ref[...].astype(BF16)
            dp_ref[2] = dv_ref[...].astype(BF16)

        pl.when((hd == H - 1) & (si == ns - 1))(wait_exchange)

    tq = SB_PER_STEP * B
    tile = lambda j: pl.BlockSpec((None, tq, HEAD_DIM), lambda h, i: (j, i, h))
    vec = pl.BlockSpec((1, HEAD_DIM), lambda h, i: (0, h))
    hbm = pl.BlockSpec(memory_space=pltpu.HBM)
    return pl.pallas_call(
        body, name="sb_bwd",
        out_shape=(jax.ShapeDtypeStruct((4, S, W), BF16), jax.ShapeDtypeStruct((1, W), F32),
                   jax.ShapeDtypeStruct((3,) + chip_sums_o.shape[1:], chip_sums_o.dtype)),
        grid=(H, ns),
        in_specs=[tile(4), _head_spec(S, 5), _head_spec(S, 6), tile(7),
                  pl.BlockSpec((tq, HEAD_DIM), lambda h, i: (i, h)),
                  pl.BlockSpec((tq, HEAD_DIM), lambda h, i: (i, h)),
                  pl.BlockSpec((tq, HEAD_DIM), lambda h, i: (i, H + h)), vec, hbm],
        out_specs=(pl.BlockSpec((4, S, HEAD_DIM), lambda h, i: (0, 0, h)), vec, hbm),
        scratch_shapes=[pltpu.VMEM((S, HEAD_DIM), BF16), pltpu.VMEM((S, HEAD_DIM), BF16),
                        pltpu.VMEM((S, HEAD_DIM), F32), pltpu.VMEM((S, HEAD_DIM), F32),
                        pltpu.SemaphoreType.DMA((1, 3)), pltpu.SemaphoreType.DMA((1, 3))],
        compiler_params=_params("arbitrary", "arbitrary"),
    )(proj, proj, proj, proj, raw, carries, dmix, gain, chip_sums_o)


def _grad_w_in_half(ht, dpr, dps, core, name, to_sibling=None):
    D, S = ht.shape
    _, _, W = dpr.shape
    tmm = min(512, D)
    nm = D // tmm

    def body(core_ref, ht_ref, r_ref, s_ref, *rest):
        o_ref = rest[1] if to_sibling is not None else rest[0]
        q, m = pl.program_id(0), pl.program_id(1)
        if to_sibling is not None:
            ga_ref, _, rin_ref, send_sems, recv_sems = rest
            copies = _to_sibling_copies([ga_ref.at[k] for k in range(4)], rin_ref, send_sems, recv_sems)

            @pl.when((q == 0) & (m == 0))
            def _():
                for cp in copies:
                    cp.start()

        @pl.when(q < 2)
        def _():
            o_ref[...] = _dot(ht_ref[...], r_ref[...])

        @pl.when(q >= 2)
        def _():
            o_ref[...] = _dot(ht_ref[...], s_ref[...])

        if to_sibling is not None:
            @pl.when((q == 3) & (m == nm - 1))
            def _():
                for cp in copies:
                    cp.wait_recv()
                for cp in copies:
                    cp.wait_send()

    hbm = pl.BlockSpec(memory_space=pltpu.HBM)
    gw_shape = jax.ShapeDtypeStruct((4, D, W), F32)
    out_shape, out_specs, extra_in, scratch = (gw_shape,), (pl.BlockSpec((None, tmm, W), lambda q, m, core: (q, m, 0)),), [], []
    if to_sibling is not None:
        out_shape += (gw_shape,)
        out_specs += (hbm,)
        extra_in = [hbm]
        scratch = [pltpu.SemaphoreType.DMA((4,)), pltpu.SemaphoreType.DMA((4,))]
    return pl.pallas_call(
        body, name=name, out_shape=out_shape,
        grid_spec=pltpu.PrefetchScalarGridSpec(
            num_scalar_prefetch=1, grid=(4, nm),
            in_specs=[pl.BlockSpec((tmm, S), lambda q, m, core: (m, 0)),
                      pl.BlockSpec((None, S, W), lambda q, m, core: (jnp.minimum(2 * q + core[0], 3), 0, 0)),
                      pl.BlockSpec((None, S, W), lambda q, m, core: (jnp.maximum(2 * q + core[0] - 4, 0), 0, 0))]
            + extra_in,
            out_specs=out_specs, scratch_shapes=scratch),
        compiler_params=_params("arbitrary", "arbitrary"),
    )(core, ht, dpr, dps, *(() if to_sibling is None else (to_sibling,)))


def _grad_w_out(mix_r, mix_s, dx2b):
    S, W = mix_r.shape
    D = dx2b.shape[1]
    tmm = min(512, W)
    tk = min(2048, S)

    def body(r_ref, s_ref, b_ref, o_ref):
        j, kk = pl.program_id(0), pl.program_id(2)

        def acc(a_ref):
            part = _dot_tn(a_ref[...], b_ref[...])

            @pl.when(kk == 0)
            def _():
                o_ref[...] = part

            @pl.when(kk > 0)
            def _():
                o_ref[...] += part

        pl.when(j == 0)(lambda: acc(r_ref))
        pl.when(j == 1)(lambda: acc(s_ref))

    return pl.pallas_call(
        body, name="grad_w_out", out_shape=jax.ShapeDtypeStruct((2, W, D), F32), grid=(2, W // tmm, S // tk),
        in_specs=[pl.BlockSpec((tk, tmm), lambda j, m, k: (k, m)),
                  pl.BlockSpec((tk, tmm), lambda j, m, k: (k, m)),
                  pl.BlockSpec((tk, D), lambda j, m, k: (k, 0))],
        out_specs=pl.BlockSpec((None, tmm, D), lambda j, m, k: (j, m, 0)),
        compiler_params=_params("parallel", "parallel", "arbitrary"),
    )(mix_r, mix_s, dx2b)


def _dh_matmul(dpr, dps, w_all, chip_sums):
    _, S, W = dpr.shape
    D = w_all.shape[1]
    tm = min(1024, S)
    ni = S // tm

    def body(r_ref, s_ref, w_ref, sa_ref, dh_ref, ra_ref, send_sems, recv_sems):
        i, j = pl.program_id(0), pl.program_id(1)
        start_exchange, wait_exchange = _exchange_chip_sums((sa_ref,), (ra_ref,), send_sems, recv_sems)
        pl.when((i == 0) & (j == 0))(start_exchange)

        def acc(b_ref):
            part = _dot_nt(b_ref[...], w_ref[...])

            @pl.when(j == 0)
            def _():
                dh_ref[...] = part

            @pl.when(j > 0)
            def _():
                dh_ref[...] += part

        pl.when(j < 4)(lambda: acc(r_ref))
        pl.when(j >= 4)(lambda: acc(s_ref))
        pl.when((i == ni - 1) & (j == 7))(wait_exchange)

    hbm = pl.BlockSpec(memory_space=pltpu.HBM)
    return pl.pallas_call(
        body, name="dh_matmul",
        out_shape=(jax.ShapeDtypeStruct((S, D), F32), jax.ShapeDtypeStruct((3,) + chip_sums.shape[1:], chip_sums.dtype)),
        grid=(ni, 8),
        in_specs=[pl.BlockSpec((None, tm, W), lambda i, j: (jnp.minimum(j, 3), i, 0)),
                  pl.BlockSpec((None, tm, W), lambda i, j: (jnp.maximum(j - 4, 0), i, 0)),
                  pl.BlockSpec((None, D, W), lambda i, j: (j, 0, 0)), hbm],
        out_specs=(pl.BlockSpec((tm, D), lambda i, j: (i, 0)), hbm),
        scratch_shapes=[pltpu.SemaphoreType.DMA((1, 3)), pltpu.SemaphoreType.DMA((1, 3))],
        compiler_params=_params("arbitrary", "arbitrary"),
    )(dpr, dps, w_all, chip_sums)


def _norm_bwd(x, dx2, dh, gain):
    S, D = x.shape
    tm = min(256, S)

    def body(x_ref, dx2_ref, dh_ref, g_ref, gx_ref, dgain_ref):
        @pl.when(pl.program_id(0) == 0)
        def _():
            dgain_ref[...] = jnp.zeros_like(dgain_ref)

        xv, dh_v = x_ref[...], dh_ref[...]
        r1 = lax.rsqrt(jnp.mean(xv * xv, axis=-1, keepdims=True) + EPS)
        n = xv * r1
        dgain_ref[...] += jnp.sum(dh_v * n, axis=0, keepdims=True)
        dn = dh_v * g_ref[...]
        gx_ref[...] = dx2_ref[...] + r1 * (dn - n * jnp.mean(dn * n, axis=-1, keepdims=True))

    row = pl.BlockSpec((tm, D), lambda i: (i, 0))
    one = pl.BlockSpec((1, D), lambda i: (0, 0))
    return pl.pallas_call(
        body, name="norm_bwd", out_shape=(jax.ShapeDtypeStruct((S, D), F32), jax.ShapeDtypeStruct((1, D), F32)),
        grid=(S // tm,), in_specs=[row, row, row, one], out_specs=(row, one),
        compiler_params=_params("arbitrary"),
    )(x, dx2, dh, gain)


def _own_block(gw, pos, q):
    return q if gw.shape[0] == 4 else 2 * q + pos[0]


def _rs_local_sum(gw, rin, pos):
    _, R, C = gw.shape
    tr = min(1024, R)
    other = lambda k, pos: (pos[1] + 1 + k) % 4

    def body(pos_ref, a_ref, b_ref, o_ref):
        o_ref[...] = (a_ref[...] + b_ref[...]).astype(BF16)

    return pl.pallas_call(
        body, name="rs_local_sum", out_shape=jax.ShapeDtypeStruct((4, R, C), BF16),
        grid_spec=pltpu.PrefetchScalarGridSpec(
            num_scalar_prefetch=1, grid=(3, R // tr),
            in_specs=[pl.BlockSpec((None, tr, C), lambda k, i, pos: (_own_block(gw, pos, other(k, pos)), i, 0)),
                      pl.BlockSpec((None, tr, C), lambda k, i, pos: (other(k, pos), i, 0))],
            out_specs=pl.BlockSpec((None, tr, C), lambda k, i, pos: (other(k, pos), i, 0))),
        compiler_params=_params("parallel", "parallel"),
    )(pos, gw, rin)


def _adamw(w, g, m, v):
    m2 = ADAM_B1 * m + (1.0 - ADAM_B1) * g
    v2 = ADAM_B2 * v + (1.0 - ADAM_B2) * (g * g)
    m_hat = m2 / (1.0 - ADAM_B1 ** ADAM_STEP)
    v_hat = v2 / (1.0 - ADAM_B2 ** ADAM_STEP)
    delta = -ADAM_LR * (m_hat / (jnp.sqrt(v_hat) + ADAM_EPS) + ADAM_WD * w)
    return delta, m2, v2


def _adamw_shard(gw, rin, rb, w, m, v, pos):
    _, R, C = gw.shape
    tr = min(256, R)

    def body(pos_ref, a_ref, b_ref, rb_ref, w_ref, m_ref, v_ref, g_ref, d_ref, m2_ref, v2_ref):
        g = a_ref[...] + b_ref[...]
        for k in range(3):
            g = g + rb_ref[k].astype(F32)
        g_ref[...] = g
        d_ref[...], m2_ref[...], v2_ref[...] = _adamw(w_ref[...], g, m_ref[...], v_ref[...])

    plain = pl.BlockSpec((tr, C), lambda i, pos: (i, 0))
    shape = jax.ShapeDtypeStruct((R, C), F32)
    return pl.pallas_call(
        body, name="adamw_shard", out_shape=(shape,) * 4,
        grid_spec=pltpu.PrefetchScalarGridSpec(
            num_scalar_prefetch=1, grid=(R // tr,),
            in_specs=[pl.BlockSpec((None, tr, C), lambda i, pos: (_own_block(gw, pos, pos[1]), i, 0)),
                      pl.BlockSpec((None, tr, C), lambda i, pos: (pos[1], i, 0)),
                      pl.BlockSpec((3, tr, C), lambda i, pos: (0, i, 0)), plain, plain, plain],
            out_specs=(plain,) * 4),
        compiler_params=_params("parallel"),
    )(pos, gw, rin, rb, w, m, v)


def _adamw_small(parts, w, m, v):
    _, rows, n = parts.shape

    def body(p_ref, w_ref, m_ref, v_ref, g_ref, d_ref, m2_ref, v2_ref):
        g = p_ref[0]
        for d in range(1, N_DEV):
            g = g + p_ref[d]
        g_ref[...] = g
        d_ref[...], m2_ref[...], v2_ref[...] = _adamw(w_ref[...], g, m_ref[...], v_ref[...])

    shape = jax.ShapeDtypeStruct((rows, n), F32)
    return pl.pallas_call(body, name="adamw_small", out_shape=(shape,) * 4)(parts, w, m, v)


def _rope_tables(S):
    half = HEAD_DIM // 2
    inv = ROPE_THETA ** (-jnp.arange(half, dtype=F32) / half)
    ang = jnp.arange(S, dtype=F32)[:, None] * inv[None, :]
    cos, sin = jnp.cos(ang), jnp.sin(ang)
    return jnp.concatenate([cos, cos], axis=1), jnp.concatenate([-sin, sin], axis=1)


def _retention_tables(H):
    lg = jnp.log1p(-jnp.exp2(-5.0 - jnp.arange(H, dtype=F32)))
    n = jnp.arange(CHUNK, dtype=F32)
    rel = n[:, None] - n[None, :]
    decay = jnp.where(rel >= 0, jnp.exp(lg[:, None, None] * jnp.maximum(rel, 0.0)), 0.0)
    shape = (H, CHUNK, HEAD_DIM)
    xi = jnp.broadcast_to(jnp.exp(lg[:, None] * (n + 1.0))[:, :, None], shape)
    zeta = jnp.broadcast_to(jnp.exp(lg[:, None] * (CHUNK - 1.0 - n))[:, :, None], shape)
    gamma_c = jnp.broadcast_to(jnp.exp(lg * CHUNK)[:, None, None], shape)
    return decay, xi, zeta, gamma_c


def _pack_small(parts):
    flat = []
    for p in parts:
        p = p.reshape(-1)
        flat.append(jnp.pad(p, (0, -p.shape[0] % LANES)))
    flat = jnp.concatenate(flat)
    return jnp.pad(flat, (0, SMALL_N - flat.shape[0])).reshape(SUBLANES, SMALL_N // SUBLANES)


def _unpack_small(packed, shapes):
    flat = packed.reshape(-1)
    out, at = [], 0
    for shp in shapes:
        size = 1
        for s in shp:
            size *= s
        out.append(flat[at:at + size].reshape(shp))
        at += size + (-size % LANES)
    return out


def kernel(x, norm_gain, w_in, ret_gn_gain, ret_gn_bias, sb_norm_gain, w_out, final_norm_gain, loss_target, m_norm_gain, m_w_in, m_ret_gn_gain, m_ret_gn_bias, m_sb_norm_gain, m_w_out, m_final_norm_gain, v_norm_gain, v_w_in, v_ret_gn_gain, v_ret_gn_bias, v_sb_norm_gain, v_w_out, v_final_norm_gain):
    S, D = x.shape[1], x.shape[2]
    W = w_in.shape[2]
    wo_rows = w_out.shape[1]
    H = W // HEAD_DIM
    xs, tgt = x[0], loss_target[0]
    mx, my, mc = _mesh_pos()
    pos = jnp.stack([mc, 2 * mx + my]).astype(jnp.int32)

    cos, sin = _rope_tables(S)
    tabs = _retention_tables(H)

    proj, w_all, ht = _in_proj_gather(xs, norm_gain, w_in[0].astype(BF16), cos, sin, _gather_order())
    mix_r = _ret_fwd(proj, tabs, ret_gn_gain, ret_gn_bias)
    mix_s, raw_s, carries, wo_all = _sb_fwd(proj, sb_norm_gain, w_out[0].astype(BF16))
    wo_full = wo_all.reshape(N_DEV * wo_rows, D)
    dx2, dx2b, dmix, loss_p, d_gf = _out_proj_loss(mix_r, mix_s, wo_full, xs, tgt, final_norm_gain[None])

    gwo = _grad_w_out(mix_r, mix_s, dx2b).reshape(N_DEV, wo_rows, D)
    dpr, d_rgain, d_rbias, rino = _ret_bwd(proj, dmix, tabs, ret_gn_gain, ret_gn_bias, cos, sin, gwo)
    dps, d_sgain, rbo = _sb_bwd(proj, raw_s, carries, dmix, sb_norm_gain, _rs_local_sum(gwo, rino, pos))
    gw_sibling, = _grad_w_in_half(ht, dpr, dps, (1 - mc).reshape(1).astype(jnp.int32), "grad_w_in_sibling")
    gw, rin = _grad_w_in_half(ht, dpr, dps, mc.reshape(1).astype(jnp.int32), "grad_w_in_own", to_sibling=gw_sibling)
    dh, rb = _dh_matmul(dpr, dps, w_all, _rs_local_sum(gw, rin, pos))
    grad_x, d_gain = _norm_bwd(xs, dx2, dh, norm_gain)
    g_in, d_in, m_in, v_in = _adamw_shard(gw, rin, rb, w_in[0], m_w_in[0], v_w_in[0], pos)
    g_out, d_out, m_out, v_out = _adamw_shard(gwo, rino, rbo, w_out[0], m_w_out[0], v_w_out[0], pos)

    small_w = [norm_gain, ret_gn_gain, ret_gn_bias, sb_norm_gain, final_norm_gain]
    small_m = [m_norm_gain, m_ret_gn_gain, m_ret_gn_bias, m_sb_norm_gain, m_final_norm_gain]
    small_v = [v_norm_gain, v_ret_gn_gain, v_ret_gn_bias, v_sb_norm_gain, v_final_norm_gain]
    shapes = [()] + [w.shape for w in small_w]
    zero = jnp.zeros((), F32)
    parts = _small_all_gather(_pack_small([loss_p[0, 0], d_gain, d_rgain, d_rbias, d_sgain, d_gf]))
    packed = _adamw_small(parts, _pack_small([zero] + small_w), _pack_small([zero] + small_m),
                          _pack_small([zero] + small_v))
    g_s, d_s, m_s, v_s = (_unpack_small(p, shapes) for p in packed)

    grads = [g_s[1], g_in[None], g_s[2], g_s[3], g_s[4], g_out[None], g_s[5]]
    deltas = [d_s[1], d_in[None], d_s[2], d_s[3], d_s[4], d_out[None], d_s[5]]
    new_m = [m_s[1], m_in[None], m_s[2], m_s[3], m_s[4], m_out[None], m_s[5]]
    new_v = [v_s[1], v_in[None], v_s[2], v_s[3], v_s[4], v_out[None], v_s[5]]
    return (g_s[0], grad_x[None], *grads, *deltas, *new_m, *new_v)
```

```python
import jax
import jax.numpy as jnp
from jax import lax
from jax.experimental import pallas as pl
from jax.experimental.pallas import tpu as pltpu

F32 = jnp.float32
BF16 = jnp.bfloat16

HEAD_DIM = 128
CHUNK = 128
RET_GROUP = 16
ROPE_THETA = 10000.0
EPS = 1e-6
ADAM_LR = 0.001
ADAM_B1 = 0.9
ADAM_B2 = 0.999
ADAM_EPS = 1e-08
ADAM_WD = 0.01
ADAM_STEP = 10

N_DEV = 8
LANES = 128
SUBLANES = 8
VMEM_LIMIT = 56 * 1024 * 1024
SB_BLOCK = 256
SB_PER_STEP = 4
SB_FWD_PER_STEP = 8
SMALL_N = 8192
EXP_IS_ZERO_BELOW = -104.0
NOT_VISITED = -1e30
MESH = pl.DeviceIdType.MESH

NT = (((1,), (1,)), ((), ()))
TN = (((0,), (0,)), ((), ()))


def _params(*sem):
    return pltpu.CompilerParams(dimension_semantics=sem if sem else None, vmem_limit_bytes=VMEM_LIMIT)


def _dot(a, b):
    return jnp.dot(a, b, preferred_element_type=F32)


def _dot_nt(a, b):
    return lax.dot_general(a, b, NT, preferred_element_type=F32)


def _dot_tn(a, b):
    return lax.dot_general(a, b, TN, preferred_element_type=F32)


def _sigmoid(g):
    return 1.0 / (1.0 + jnp.exp(-g))


def _rot(a, cos, sin_signed):
    return a * cos + pltpu.roll(a, HEAD_DIM // 2, 1) * sin_signed


def _mesh_pos():
    return lax.axis_index("x"), lax.axis_index("y"), lax.axis_index("c")


def _to_sibling_copies(blocks, out_ref, send_sems, recv_sems):
    x, y, c = _mesh_pos()
    return [pltpu.make_async_remote_copy(
        src_ref=block, dst_ref=out_ref.at[k], send_sem=send_sems.at[k], recv_sem=recv_sems.at[k],
        device_id=(x, y, 1 - c), device_id_type=MESH) for k, block in enumerate(blocks)]


def _exchange_chip_sums(srcs, outs, send_sems, recv_sems):
    x, y, c = _mesh_pos()
    copies = []
    for arr, (src, out) in enumerate(zip(srcs, outs)):
        for k in range(1, 4):
            px = 1 - x if k & 2 else x
            py = 1 - y if k & 1 else y
            copies.append(pltpu.make_async_remote_copy(
                src_ref=src.at[2 * px + py], dst_ref=out.at[k - 1],
                send_sem=send_sems.at[arr, k - 1], recv_sem=recv_sems.at[arr, k - 1],
                device_id=(px, py, c), device_id_type=MESH))

    def start():
        for cp in copies:
            cp.start()

    def wait():
        for cp in copies:
            cp.wait_recv()
        for cp in copies:
            cp.wait_send()

    return start, wait


def _small_all_gather(small):
    rows, n = small.shape

    def body(s_ref, o_ref, send_sems, recv_sems, local_sem):
        start, wait = _exchange_with_all(s_ref, o_ref, send_sems, recv_sems, local_sem)
        start()
        wait()

    vmem = pl.BlockSpec(memory_space=pltpu.VMEM)
    return pl.pallas_call(
        body, name="small_all_gather",
        out_shape=jax.ShapeDtypeStruct((N_DEV, rows, n), small.dtype),
        in_specs=[vmem], out_specs=vmem,
        scratch_shapes=[pltpu.SemaphoreType.DMA((N_DEV - 1,)), pltpu.SemaphoreType.DMA((N_DEV - 1,)),
                        pltpu.SemaphoreType.DMA],
    )(small)


GATHER_SPLIT = 2
GATHER_STEPS = ([("own", 0, p) for p in range(GATHER_SPLIT)] + [("sibling", 0, p) for p in range(GATHER_SPLIT)]
                + [step for p in range(GATHER_SPLIT) for step in
                   (("ici", 0, p), ("ici", 1, p), ("passed", 0, p), ("passed", 1, p))]
                + [step for p in range(GATHER_SPLIT) for step in (("ici", 2, p), ("passed", 2, p))])


def _via_x(p):
    return p % 2 == 0


def _gather_order():
    x, y, c = _mesh_pos()
    chips = [(1 - x, y), (x, 1 - y), (1 - x, 1 - y)]
    owner = {"own": lambda j: (x, y, c), "sibling": lambda j: (x, y, 1 - c),
             "ici": lambda j: (*chips[j], c), "passed": lambda j: (*chips[j], 1 - c)}
    blocks = [4 * px + 2 * py + pc for px, py, pc in (owner[kind](j) for kind, j, _ in GATHER_STEPS)]
    return (jnp.stack(blocks).astype(jnp.int32), jnp.array([p for _, _, p in GATHER_STEPS], jnp.int32))


def _in_proj_gather(x, gain, w_shard, cos, sin, order):
    S, D = x.shape
    W = w_shard.shape[1]
    wp = W // GATHER_SPLIT
    tm = min(1024, S)
    ni = S // tm
    n_steps = len(GATHER_STEPS)

    def body(blk_ref, piece_ref, x_ref, g_ref, w_ref, cos_ref, sin_ref, o_ref, wall_ref, ht_ref, h_scr, wbuf,
             send_sems, recv_sems, local_sem, load_sems):
        step, i = pl.program_id(0), pl.program_id(1)
        mx, my, c = _mesh_pos()
        me, sibling = (mx, my, c), (mx, my, 1 - c)
        chips = [(1 - mx, my), (mx, 1 - my), (1 - mx, 1 - my)]

        def piece_of(dev, p):
            px, py, pc = dev
            return wall_ref.at[4 * px + 2 * py + pc, :, pl.ds(p * wp, wp)]

        def copy(k, p, block, to, own=False):
            dst = piece_of(block, p)
            return pltpu.make_async_remote_copy(
                src_ref=w_ref.at[:, pl.ds(p * wp, wp)] if own else dst, dst_ref=dst,
                send_sem=send_sems.at[k, p], recv_sem=recv_sems.at[k, p], device_id=to, device_id_type=MESH)

        pieces = range(GATHER_SPLIT)
        first = [cp for p in pieces for cp in
                 [copy(0, p, me, sibling, own=True)] + [copy(1 + j, p, me, (*chips[j], c), own=True) for j in (0, 1)]]
        passed = {(j, p): copy(4 + j, p, (*chip, c), sibling) for j, chip in enumerate(chips) for p in pieces}
        onward = {p: copy(3, p, (*chips[0 if _via_x(p) else 1], c), (*chips[1 if _via_x(p) else 0], c)) for p in pieces}
        mine = pltpu.make_async_copy(w_ref, wall_ref.at[4 * mx + 2 * my + c], local_sem)

        def load(s):
            kind, j, p = GATHER_STEPS[s]
            if kind == "own":
                src = w_ref.at[:, pl.ds(p * wp, wp)]
            elif kind == "sibling":
                copy(0, p, sibling, me).wait_recv()
                src = piece_of(sibling, p)
            elif kind == "ici":
                copy(1 + j, p, (*chips[j], c), me).wait_recv()
                if j == (0 if _via_x(p) else 1):
                    onward[p].start()
                passed[j, p].start()
                src = piece_of((*chips[j], c), p)
            else:
                copy(4 + j, p, (*chips[j], 1 - c), me).wait_recv()
                src = piece_of((*chips[j], 1 - c), p)
            return pltpu.make_async_copy(src, wbuf.at[s % 2], load_sems.at[s % 2])

        for s in range(n_steps):
            @pl.when((step == s) & (i == 0))
            def _(s=s):
                if s == 0:
                    for cp in first:
                        cp.start()
                    mine.start()
                    load(0).start()
                pltpu.make_async_copy(w_ref.at[:, pl.ds(0, wp)], wbuf.at[s % 2], load_sems.at[s % 2]).wait()

            if s + 1 < n_steps:
                @pl.when((step == s) & (i == ni - 1))
                def _(s=s):
                    load(s + 1).start()

        rows = pl.ds(pl.multiple_of(i * tm, tm), tm)

        @pl.when(step == 0)
        def _():
            xv = x_ref[...]
            r = lax.rsqrt(jnp.mean(xv * xv, axis=-1, keepdims=True) + EPS)
            hv = xv * r * g_ref[...]
            h_scr[rows, :] = hv.astype(BF16)
            ht_ref[...] = hv.T.astype(BF16)

        acc = _dot(h_scr[rows, :], wbuf[step % 2])
        b = blk_ref[step]

        @pl.when(b >= 2)
        def _():
            o_ref[...] = acc

        @pl.when(b < 2)
        def _():
            scale = jnp.where(b == 1, HEAD_DIM ** -0.5, 1.0).astype(F32)
            cs, sn = cos_ref[...], sin_ref[...]
            for hh in range(wp // HEAD_DIM):
                cols = slice(hh * HEAD_DIM, (hh + 1) * HEAD_DIM)
                o_ref[:, cols] = _rot(acc[:, cols], cs, sn) * scale

        @pl.when((step == n_steps - 1) & (i == ni - 1))
        def _():
            for cp in first + list(passed.values()) + list(onward.values()):
                cp.wait_send()
            mine.wait()

    hbm = pl.BlockSpec(memory_space=pltpu.HBM)
    rope = pl.BlockSpec((tm, HEAD_DIM), lambda s, i, blk, piece: (i, 0))
    first_pass = lambda s, i: jnp.where(s == 0, i, ni - 1)
    return pl.pallas_call(
        body, name="in_proj_gather",
        out_shape=(jax.ShapeDtypeStruct((N_DEV, S, W), F32), jax.ShapeDtypeStruct((N_DEV, D, W), BF16),
                   jax.ShapeDtypeStruct((D, S), BF16)),
        grid_spec=pltpu.PrefetchScalarGridSpec(
            num_scalar_prefetch=2, grid=(n_steps, ni),
            in_specs=[pl.BlockSpec((tm, D), lambda s, i, blk, piece: (first_pass(s, i), 0)),
                      pl.BlockSpec((1, D), lambda s, i, blk, piece: (0, 0)), hbm, rope, rope],
            out_specs=(pl.BlockSpec((None, tm, wp), lambda s, i, blk, piece: (blk[s], i, piece[s])), hbm,
                       pl.BlockSpec((D, tm), lambda s, i, blk, piece: (0, first_pass(s, i)))),
            scratch_shapes=[pltpu.VMEM((S, D), BF16), pltpu.VMEM((2, D, wp), BF16),
                            pltpu.SemaphoreType.DMA((7, GATHER_SPLIT)), pltpu.SemaphoreType.DMA((7, GATHER_SPLIT)),
                            pltpu.SemaphoreType.DMA, pltpu.SemaphoreType.DMA((2,))]),
        compiler_params=_params("arbitrary", "arbitrary"),
    )(*order, x, gain, w_shard, cos, sin)


def _head_spec(S, j):
    return pl.BlockSpec((None, S, HEAD_DIM), lambda h, *_: (j, 0, h))


def _bdot(a, b):
    return lax.dot_general(a, b, (((2,), (1,)), ((0,), (0,))), preferred_element_type=F32)


def _bdot_nt(a, b):
    return lax.dot_general(a, b, (((2,), (2,)), ((0,), (0,))), preferred_element_type=F32)


def _bdot_tn(a, b):
    return lax.dot_general(a, b, (((1,), (1,)), ((0,), (0,))), preferred_element_type=F32)


def _chunks(a):
    return a.reshape(a.shape[0] // CHUNK, CHUNK, a.shape[1])


def _ret_group(q, k, vb, states_b, dec, xi, ze):
    qb, kb = q.astype(BF16), k.astype(BF16)
    sb = (_bdot_nt(qb, kb) * dec).astype(BF16)
    qx = (q * xi).astype(BF16)
    out = _bdot(sb, vb) + _bdot(qx, states_b)
    return out, (qb, kb, sb, qx)


def _ret_states(kz, vb, gam, state, states_ref):
    kv = _bdot_tn(kz, vb)
    for u in range(RET_GROUP):
        states_ref[u] = state
        state = gam * state + kv[u]
    return state


def _table_specs():
    return [pl.BlockSpec((None, CHUNK, HEAD_DIM), lambda h, *_: (h, 0, 0))] * 4


def _ret_fwd(proj, tabs, gn_gain, gn_bias):
    _, S, W = proj.shape
    H, nc = W // HEAD_DIM, S // CHUNK
    assert nc % RET_GROUP == 0
    rows_per_group = RET_GROUP * CHUNK

    def body(q_ref, k_ref, v_ref, g_ref, dec_ref, xi_ref, ze_ref, gam_ref, gain_ref, bias_ref, o_ref, states_ref):
        dec, xi, ze, gam = dec_ref[...], xi_ref[...], ze_ref[...], gam_ref[...]
        gain, bias = gain_ref[...], bias_ref[...]

        def group(i, state):
            rows = pl.ds(pl.multiple_of(i * rows_per_group, rows_per_group), rows_per_group)
            q, k, vb = _chunks(q_ref[rows, :]), _chunks(k_ref[rows, :]), _chunks(v_ref[rows, :]).astype(BF16)
            state = _ret_states((k * ze).astype(BF16), vb, gam, state, states_ref)
            out, _ = _ret_group(q, k, vb, states_ref[...].astype(BF16), dec, xi, ze)
            mu = jnp.mean(out, axis=-1, keepdims=True)
            d = out - mu
            yn = d * lax.rsqrt(jnp.mean(d * d, axis=-1, keepdims=True) + EPS)
            g = _chunks(g_ref[rows, :])
            mix = g * _sigmoid(g) * (yn * gain + bias)
            o_ref[rows, :] = mix.reshape(rows_per_group, HEAD_DIM).astype(BF16)
            return state

        lax.fori_loop(0, nc // RET_GROUP, group, jnp.zeros((HEAD_DIM, HEAD_DIM), F32))

    vec = pl.BlockSpec((1, HEAD_DIM), lambda h: (0, h))
    return pl.pallas_call(
        body, name="ret_fwd", out_shape=jax.ShapeDtypeStruct((S, W), BF16), grid=(H,),
        in_specs=[_head_spec(S, 0), _head_spec(S, 1), _head_spec(S, 2), _head_spec(S, 3)] + _table_specs() + [vec, vec],
        out_specs=pl.BlockSpec((S, HEAD_DIM), lambda h: (0, h)),
        scratch_shapes=[pltpu.VMEM((RET_GROUP, HEAD_DIM, HEAD_DIM), F32)],
        compiler_params=_params("parallel"),
    )(proj, proj, proj, proj, *tabs, gn_gain, gn_bias)


def _sb_scores(qb, kk, masked, causal, upper):
    z = _dot_nt(qb, kk) * (HEAD_DIM ** -0.5)
    e = jnp.exp(-jnp.abs(z))
    l1p = jnp.log(1.0 + e)
    log_beta = jnp.minimum(z, 0.0) - l1p
    lk = jnp.minimum(-z, 0.0) - l1p
    if masked:
        lk = jnp.where(causal, lk, 0.0)
    hi = lk.astype(BF16)
    lo = (lk - hi.astype(F32)).astype(BF16)
    cs = _dot(hi, upper) + _dot(lo, upper)
    return log_beta, lk, cs


def _tri(B, kind):
    r = lax.broadcasted_iota(jnp.int32, (B, B), 0)
    c = lax.broadcasted_iota(jnp.int32, (B, B), 1)
    return {"gt": r > c, "lt": r < c}[kind]


def _ones_where(mask):
    return jnp.where(mask, 1.0, 0.0).astype(BF16)


def _exchange_with_all(src_ref, out_ref, send_sems, recv_sems, local_sem):
    x, y, c = _mesh_pos()
    peers = [(1 - x if k & 4 else x, 1 - y if k & 2 else y, 1 - c if k & 1 else c) for k in range(1, N_DEV)]

    def copy(k, owner, to):
        px, py, pc = owner
        return pltpu.make_async_remote_copy(
            src_ref=src_ref, dst_ref=out_ref.at[4 * px + 2 * py + pc], send_sem=send_sems.at[k],
            recv_sem=recv_sems.at[k], device_id=to, device_id_type=MESH)

    sends = [copy(k, (x, y, c), p) for k, p in enumerate(peers)]
    mine = pltpu.make_async_copy(src_ref, out_ref.at[4 * x + 2 * y + c], local_sem)

    def start():
        for cp in sends:
            cp.start()
        mine.start()

    def wait():
        for k, p in enumerate(peers):
            copy(k, p, p).wait_recv()
        for cp in sends:
            cp.wait_send()
        mine.wait()

    return start, wait


def _sb_fwd(proj, gain, wo_shard):
    _, S, W = proj.shape
    H = W // HEAD_DIM
    B = min(SB_BLOCK, S)
    nq = S // B
    assert nq <= HEAD_DIM and nq % SB_FWD_PER_STEP == 0
    ns = nq // SB_FWD_PER_STEP

    def body(q_ref, k_ref, v_ref, g_ref, gain_ref, wo_ref, mix_ref, raw_ref, car_ref, woall_ref, kb_ref, vb_ref,
             send_sems, recv_sems, local_sem):
        hd, si = pl.program_id(0), pl.program_id(1)
        start_gather, wait_gather = _exchange_with_all(wo_ref, woall_ref, send_sems, recv_sems, local_sem)
        pl.when((hd == 0) & (si == 0))(start_gather)

        @pl.when(si == 0)
        def _():
            kb_ref[...] = k_ref[...].astype(BF16)
            vb_ref[...] = v_ref[...].astype(BF16)

        causal = _tri(B, "gt")
        upper = _ones_where(causal)
        lane = lax.broadcasted_iota(jnp.int32, (B, HEAD_DIM), 1)

        def block(qb, kb, carry, acc, saved, masked):
            rows = pl.ds(pl.multiple_of(kb * B, B), B)
            log_beta, lk, cs = _sb_scores(qb, kb_ref[rows, :], masked, causal, upper)
            a = jnp.exp(log_beta + cs + carry)
            if masked:
                a = jnp.where(causal, a, 0.0)
            acc = acc + _dot(a.astype(BF16), vb_ref[rows, :])
            return carry + jnp.sum(lk, axis=1, keepdims=True), acc, jnp.where(lane == kb, carry, saved)

        init = (jnp.zeros((B, 1), F32), jnp.zeros((B, HEAD_DIM), F32), jnp.full((B, HEAD_DIM), NOT_VISITED, F32))

        def live(st):
            return (st[0] >= 0) & (jnp.max(st[1]) >= EXP_IS_ZERO_BELOW)

        def finish(u, acc, saved):
            rows = slice(u * B, (u + 1) * B)
            raw_ref[rows, :] = acc
            car_ref[rows, :] = saved
            yn = acc * lax.rsqrt(jnp.mean(acc * acc, axis=-1, keepdims=True) + EPS)
            g = g_ref[rows, :]
            mix_ref[rows, :] = (g * _sigmoid(g) * (yn * gain_ref[...])).astype(BF16)

        def whole(first_step):
            heads = []
            for u in range(SB_FWD_PER_STEP):
                qi = si * SB_FWD_PER_STEP + u
                qb = q_ref[u * B:(u + 1) * B, :].astype(BF16)
                state = block(qb, qi, *init, True)
                if not (first_step and u == 0):
                    state = block(qb, qi - 1, *state, False)
                heads.append((qi, qb, state))
            for u, (qi, qb, state) in enumerate(heads):
                if not (first_step and u == 0):
                    state = lax.while_loop(
                        live, lambda st, qb=qb: (st[0] - 1,) + block(qb, st[0], st[1], st[2], st[3], False),
                        (qi - 2,) + state)[1:]
                finish(u, state[1], state[2])

        pl.when(si == 0)(lambda: whole(True))
        pl.when(si > 0)(lambda: whole(False))
        pl.when((hd == H - 1) & (si == ns - 1))(wait_gather)

    tq = SB_FWD_PER_STEP * B
    tile = lambda j: pl.BlockSpec((None, tq, HEAD_DIM), lambda h, i: (j, i, h))
    out_tile = pl.BlockSpec((tq, HEAD_DIM), lambda h, i: (i, h))
    hbm = pl.BlockSpec(memory_space=pltpu.HBM)
    return pl.pallas_call(
        body, name="sb_fwd",
        out_shape=(jax.ShapeDtypeStruct((S, W), BF16), jax.ShapeDtypeStruct((S, W), F32),
                   jax.ShapeDtypeStruct((S, W), F32), jax.ShapeDtypeStruct((N_DEV,) + wo_shard.shape, BF16)),
        grid=(H, ns),
        in_specs=[tile(4), _head_spec(S, 5), _head_spec(S, 6), tile(7),
                  pl.BlockSpec((1, HEAD_DIM), lambda h, i: (0, h)), hbm],
        out_specs=(out_tile, out_tile, out_tile, hbm),
        scratch_shapes=[pltpu.VMEM((S, HEAD_DIM), BF16), pltpu.VMEM((S, HEAD_DIM), BF16),
                        pltpu.SemaphoreType.DMA((N_DEV - 1,)), pltpu.SemaphoreType.DMA((N_DEV - 1,)),
                        pltpu.SemaphoreType.DMA],
        compiler_params=_params("arbitrary", "arbitrary"),
    )(proj, proj, proj, proj, gain, wo_shard)


def _out_proj_loss(mix_r, mix_s, w_out, x, tgt, gf):
    S, W = mix_r.shape
    D = x.shape[1]
    tm = min(256, S)

    def body(mr_ref, ms_ref, wo_ref, x_ref, t_ref, gf_ref, dx2_ref, dx2b_ref, dmix_ref, loss_ref, gfn_ref):
        @pl.when(pl.program_id(0) == 0)
        def _():
            loss_ref[...] = jnp.zeros_like(loss_ref)
            gfn_ref[...] = jnp.zeros_like(gfn_ref)

        gfv = gf_ref[...]
        x2 = x_ref[...] + (_dot(mr_ref[...], wo_ref[:W, :]) + _dot(ms_ref[...], wo_ref[W:, :]))
        r2 = lax.rsqrt(jnp.mean(x2 * x2, axis=-1, keepdims=True) + EPS)
        n = x2 * r2
        err = n * gfv - t_ref[...]
        loss_ref[...] += 0.5 * jnp.sum(jnp.mean(err * err, axis=-1, keepdims=True))
        dy = err * (1.0 / D)
        gfn_ref[...] += jnp.sum(dy * n, axis=0, keepdims=True)
        dn = dy * gfv
        dx2 = r2 * (dn - n * jnp.mean(dn * n, axis=-1, keepdims=True))
        dx2_ref[...] = dx2
        b = dx2.astype(BF16)
        dx2b_ref[...] = b
        dmix_ref[:, :W] = _dot_nt(b, wo_ref[:W, :])
        dmix_ref[:, W:] = _dot_nt(b, wo_ref[W:, :])

    row = lambda width: pl.BlockSpec((tm, width), lambda i: (i, 0))
    return pl.pallas_call(
        body, name="out_proj_loss",
        out_shape=(jax.ShapeDtypeStruct((S, D), F32), jax.ShapeDtypeStruct((S, D), BF16),
                   jax.ShapeDtypeStruct((S, 2 * W), F32), jax.ShapeDtypeStruct((SUBLANES, LANES), F32),
                   jax.ShapeDtypeStruct((1, D), F32)),
        grid=(S // tm,),
        in_specs=[row(W), row(W), pl.BlockSpec((2 * W, D), lambda i: (0, 0)), row(D), row(D),
                  pl.BlockSpec((1, D), lambda i: (0, 0))],
        out_specs=(row(D), row(D), row(2 * W), pl.BlockSpec((SUBLANES, LANES), lambda i: (0, 0)),
                   pl.BlockSpec((1, D), lambda i: (0, 0))),
        compiler_params=_params("arbitrary"),
    )(mix_r, mix_s, w_out, x, tgt, gf)


def _silu_bwd(g, dm, normed):
    sig = _sigmoid(g)
    return dm * (g * sig), dm * normed * (sig * (1.0 + g * (1.0 - sig)))


def _ret_bwd(proj, dmix, tabs, gn_gain, gn_bias, cos, sin, gwo):
    _, S, W = proj.shape
    H, nc = W // HEAD_DIM, S // CHUNK
    assert nc % RET_GROUP == 0
    ng = nc // RET_GROUP
    rows_per_group = RET_GROUP * CHUNK

    def body(q_ref, k_ref, v_ref, g_ref, dm_ref, dec_ref, xi_ref, ze_ref, gam_ref, gain_ref, bias_ref, cos_ref,
             sin_ref, gwo_ref, dp_ref, dgain_ref, dbias_ref, rino_ref, rs_ref, dstates_ref, send_sems, recv_sems):
        dec, xi, ze, gam = dec_ref[...], xi_ref[...], ze_ref[...], gam_ref[...]
        gain, bias = gain_ref[...], bias_ref[...]
        hd = pl.program_id(0)
        other_core = 1 - lax.axis_index("c")
        copies = _to_sibling_copies([gwo_ref.at[2 * k + other_core] for k in range(4)], rino_ref, send_sems, recv_sems)

        @pl.when(hd == 0)
        def _():
            for cp in copies:
                cp.start()

        def group_rows(i):
            return pl.ds(pl.multiple_of(i * rows_per_group, rows_per_group), rows_per_group)

        def fwd_group(i, state):
            rows = group_rows(i)
            kz = (_chunks(k_ref[rows, :]) * ze).astype(BF16)
            return _ret_states(kz, _chunks(v_ref[rows, :]).astype(BF16), gam, state,
                               rs_ref.at[pl.ds(i * RET_GROUP, RET_GROUP)])

        lax.fori_loop(0, ng, fwd_group, jnp.zeros((HEAD_DIM, HEAD_DIM), F32))

        flat = lambda a: a.reshape(rows_per_group, HEAD_DIM)

        def bwd_group(t, carry):
            dgain, dbias, dstate = carry
            i = ng - 1 - t
            rows = group_rows(i)
            q, k, g = _chunks(q_ref[rows, :]), _chunks(k_ref[rows, :]), _chunks(g_ref[rows, :])
            vb = _chunks(v_ref[rows, :]).astype(BF16)
            rb = rs_ref[pl.ds(i * RET_GROUP, RET_GROUP)].astype(BF16)
            out, (qb, kb, sb, qx) = _ret_group(q, k, vb, rb, dec, xi, ze)
            kz = (k * ze).astype(BF16)
            mu = jnp.mean(out, axis=-1, keepdims=True)
            d = out - mu
            rstd = lax.rsqrt(jnp.mean(d * d, axis=-1, keepdims=True) + EPS)
            yn = d * rstd
            dgn, dg = _silu_bwd(g, _chunks(dm_ref[rows, :]), yn * gain + bias)
            dgain = dgain + jnp.sum(flat(dgn * yn), axis=0, keepdims=True)
            dbias = dbias + jnp.sum(flat(dgn), axis=0, keepdims=True)
            dyn = dgn * gain
            do = rstd * (dyn - jnp.mean(dyn, axis=-1, keepdims=True)
                         - yn * jnp.mean(dyn * yn, axis=-1, keepdims=True))
            dob = do.astype(BF16)
            dkv = _bdot_tn(qx, dob)
            for u in reversed(range(RET_GROUP)):
                dstates_ref[u] = dstate
                dstate = gam * dstate + dkv[u]
            drb = dstates_ref[...].astype(BF16)
            dv = _bdot_tn(sb, dob) + _bdot(kz, drb)
            dsb = (_bdot_nt(dob, vb) * dec).astype(BF16)
            dq = _bdot(dsb, kb) + _bdot_nt(dob, rb) * xi
            dk = _bdot_tn(dsb, qb) + _bdot_nt(vb, drb) * ze
            cs, sn = cos_ref[rows, :], -sin_ref[rows, :]
            dp_ref[0, rows, :] = _rot(flat(dq), cs, sn).astype(BF16)
            dp_ref[1, rows, :] = (_rot(flat(dk), cs, sn) * (HEAD_DIM ** -0.5)).astype(BF16)
            dp_ref[2, rows, :] = flat(dv).astype(BF16)
            dp_ref[3, rows, :] = flat(dg).astype(BF16)
            return dgain, dbias, dstate

        zero = jnp.zeros((1, HEAD_DIM), F32)
        dgain, dbias, _ = lax.fori_loop(0, ng, bwd_group, (zero, zero, jnp.zeros((HEAD_DIM, HEAD_DIM), F32)))
        dgain_ref[...] = dgain
        dbias_ref[...] = dbias

        @pl.when(hd == H - 1)
        def _():
            for cp in copies:
                cp.wait_recv()
            for cp in copies:
                cp.wait_send()

    vec = pl.BlockSpec((1, HEAD_DIM), lambda h: (0, h))
    full = pl.BlockSpec((S, HEAD_DIM), lambda h: (0, 0))
    hbm = pl.BlockSpec(memory_space=pltpu.HBM)
    return pl.pallas_call(
        body, name="ret_bwd",
        out_shape=(jax.ShapeDtypeStruct((4, S, W), BF16), jax.ShapeDtypeStruct((1, W), F32),
                   jax.ShapeDtypeStruct((1, W), F32), jax.ShapeDtypeStruct((4,) + gwo.shape[1:], gwo.dtype)),
        grid=(H,),
        in_specs=[_head_spec(S, 0), _head_spec(S, 1), _head_spec(S, 2), _head_spec(S, 3),
                  pl.BlockSpec((S, HEAD_DIM), lambda h: (0, h))] + _table_specs() + [vec, vec, full, full, hbm],
        out_specs=(pl.BlockSpec((4, S, HEAD_DIM), lambda h: (0, 0, h)), vec, vec, hbm),
        scratch_shapes=[pltpu.VMEM((nc, HEAD_DIM, HEAD_DIM), F32), pltpu.VMEM((RET_GROUP, HEAD_DIM, HEAD_DIM), F32),
                        pltpu.SemaphoreType.DMA((4,)), pltpu.SemaphoreType.DMA((4,))],
        compiler_params=_params("arbitrary"),
    )(proj, proj, proj, proj, dmix, *tabs, gn_gain, gn_bias, cos, sin, gwo)


def _sb_bwd(proj, raw, carries, dmix, gain, chip_sums_o):
    _, S, W = proj.shape
    H = W // HEAD_DIM
    B = min(SB_BLOCK, S)
    nq = S // B
    ns = nq // SB_PER_STEP

    def body(q_ref, k_ref, v_ref, g_ref, raw_ref, car_ref, dm_ref, gain_ref, so_ref, dp_ref, dgain_ref, ro_ref,
             kb_ref, vb_ref, dk_ref, dv_ref, send_sems, recv_sems):
        hd, si = pl.program_id(0), pl.program_id(1)
        start_exchange, wait_exchange = _exchange_chip_sums((so_ref,), (ro_ref,), send_sems, recv_sems)
        pl.when((hd == 0) & (si == 0))(start_exchange)

        @pl.when(si == 0)
        def _():
            kb_ref[...] = k_ref[...].astype(BF16)
            vb_ref[...] = v_ref[...].astype(BF16)
            dk_ref[...] = jnp.zeros_like(dk_ref)
            dv_ref[...] = jnp.zeros_like(dv_ref)
            dgain_ref[...] = jnp.zeros_like(dgain_ref)

        causal = _tri(B, "gt")
        upper = _ones_where(causal)
        before = _ones_where(_tri(B, "lt"))
        lane = lax.broadcasted_iota(jnp.int32, (B, HEAD_DIM), 1)
        gain_v = gain_ref[...]

        def prologue(u):
            qi = si * SB_PER_STEP + u
            rows = slice(u * B, (u + 1) * B)
            o = raw_ref[rows, :]
            rstd = lax.rsqrt(jnp.mean(o * o, axis=-1, keepdims=True) + EPS)
            yn = o * rstd
            dnrm, dg = _silu_bwd(g_ref[rows, :], dm_ref[rows, :], yn * gain_v)
            dp_ref[3, pl.ds(pl.multiple_of(qi * B, B), B), :] = dg.astype(BF16)
            dgain_ref[...] += jnp.sum(dnrm * yn, axis=0, keepdims=True)
            dyn = dnrm * gain_v
            do = rstd * (dyn - yn * jnp.mean(dyn * yn, axis=-1, keepdims=True))
            return qi, q_ref[rows, :].astype(BF16), do.astype(BF16), car_ref[rows, :]

        def block(ctx, kb, carry_g, dq, masked):
            _, qb, dob, saved = ctx
            rows = pl.ds(pl.multiple_of(kb * B, B), B)
            kk, vv = kb_ref[rows, :], vb_ref[rows, :]
            log_beta, _, cs = _sb_scores(qb, kk, masked, causal, upper)
            carry_lk = jnp.sum(jnp.where(lane == kb, saved, 0.0), axis=1, keepdims=True)
            a = jnp.exp(log_beta + cs + carry_lk)
            if masked:
                a = jnp.where(causal, a, 0.0)
            gmat = _dot_nt(dob, vv) * a
            dv_ref[rows, :] += _dot_tn(a.astype(BF16), dob)
            hi = gmat.astype(BF16)
            lo = (gmat - hi.astype(F32)).astype(BF16)
            dlk = carry_g + (_dot(hi, before) + _dot(lo, before))
            beta = jnp.exp(log_beta)
            dz = (gmat * (1.0 - beta) - dlk * beta) * (HEAD_DIM ** -0.5)
            if masked:
                dz = jnp.where(causal, dz, 0.0)
            dzb = dz.astype(BF16)
            dk_ref[rows, :] += _dot_tn(dzb, qb)
            return carry_g + jnp.sum(gmat, axis=1, keepdims=True), dq + _dot(dzb, kk)

        init = (jnp.zeros((B, 1), F32), jnp.zeros((B, HEAD_DIM), F32))

        def whole(first_step):
            ctxs = [prologue(u) for u in range(SB_PER_STEP)]
            states = []
            for u, ctx in enumerate(ctxs):
                state = init
                if not (first_step and u == 0):
                    visited = jnp.max(ctx[3], axis=0, keepdims=True) >= EXP_IS_ZERO_BELOW
                    first = jnp.min(jnp.where(visited, lane[:1, :], ctx[0]))
                    state = lax.fori_loop(first, ctx[0] - 1,
                                          lambda i, st, ctx=ctx: block(ctx, i, st[0], st[1], False), state)
                states.append(state)
            for u, (ctx, state) in enumerate(zip(ctxs, states)):
                if not (first_step and u == 0):
                    state = block(ctx, ctx[0] - 1, *state, False)
                state = block(ctx, ctx[0], *state, True)
                dp_ref[0, pl.ds(pl.multiple_of(ctx[0] * B, B), B), :] = state[1].astype(BF16)

        pl.when(si == 0)(lambda: whole(True))
        pl.when(si > 0)(lambda: whole(False))

        @pl.when(si == ns - 1)
        def _():
            dp_ref[1] = dk_ref[...].astype(BF16)
            dp_ref[2] = dv_ref[...].astype(BF16)

        pl.when((hd == H - 1) & (si == ns - 1))(wait_exchange)

    tq = SB_PER_STEP * B
    tile = lambda j: pl.BlockSpec((None, tq, HEAD_DIM), lambda h, i: (j, i, h))
    vec = pl.BlockSpec((1, HEAD_DIM), lambda h, i: (0, h))
    hbm = pl.BlockSpec(memory_space=pltpu.HBM)
    return pl.pallas_call(
        body, name="sb_bwd",
        out_shape=(jax.ShapeDtypeStruct((4, S, W), BF16), jax.ShapeDtypeStruct((1, W), F32),
                   jax.ShapeDtypeStruct((3,) + chip_sums_o.shape[1:], chip_sums_o.dtype)),
        grid=(H, ns),
        in_specs=[tile(4), _head_spec(S, 5), _head_spec(S, 6), tile(7),
                  pl.BlockSpec((tq, HEAD_DIM), lambda h, i: (i, h)),
                  pl.BlockSpec((tq, HEAD_DIM), lambda h, i: (i, h)),
                  pl.BlockSpec((tq, HEAD_DIM), lambda h, i: (i, H + h)), vec, hbm],
        out_specs=(pl.BlockSpec((4, S, HEAD_DIM), lambda h, i: (0, 0, h)), vec, hbm),
        scratch_shapes=[pltpu.VMEM((S, HEAD_DIM), BF16), pltpu.VMEM((S, HEAD_DIM), BF16),
                        pltpu.VMEM((S, HEAD_DIM), F32), pltpu.VMEM((S, HEAD_DIM), F32),
                        pltpu.SemaphoreType.DMA((1, 3)), pltpu.SemaphoreType.DMA((1, 3))],
        compiler_params=_params("arbitrary", "arbitrary"),
    )(proj, proj, proj, proj, raw, carries, dmix, gain, chip_sums_o)


def _grad_w_in_half(ht, dpr, dps, core, name, to_sibling=None):
    D, S = ht.shape
    _, _, W = dpr.shape
    tmm = min(512, D)
    nm = D // tmm

    def body(core_ref, ht_ref, r_ref, s_ref, *rest):
        o_ref = rest[1] if to_sibling is not None else rest[0]
        q, m = pl.program_id(0), pl.program_id(1)
        if to_sibling is not None:
            ga_ref, _, rin_ref, send_sems, recv_sems = rest
            copies = _to_sibling_copies([ga_ref.at[k] for k in range(4)], rin_ref, send_sems, recv_sems)

            @pl.when((q == 0) & (m == 0))
            def _():
                for cp in copies:
                    cp.start()

        @pl.when(q < 2)
        def _():
            o_ref[...] = _dot(ht_ref[...], r_ref[...])

        @pl.when(q >= 2)
        def _():
            o_ref[...] = _dot(ht_ref[...], s_ref[...])

        if to_sibling is not None:
            @pl.when((q == 3) & (m == nm - 1))
            def _():
                for cp in copies:
                    cp.wait_recv()
                for cp in copies:
                    cp.wait_send()

    hbm = pl.BlockSpec(memory_space=pltpu.HBM)
    gw_shape = jax.ShapeDtypeStruct((4, D, W), F32)
    out_shape, out_specs, extra_in, scratch = (gw_shape,), (pl.BlockSpec((None, tmm, W), lambda q, m, core: (q, m, 0)),), [], []
    if to_sibling is not None:
        out_shape += (gw_shape,)
        out_specs += (hbm,)
        extra_in = [hbm]
        scratch = [pltpu.SemaphoreType.DMA((4,)), pltpu.SemaphoreType.DMA((4,))]
    return pl.pallas_call(
        body, name=name, out_shape=out_shape,
        grid_spec=pltpu.PrefetchScalarGridSpec(
            num_scalar_prefetch=1, grid=(4, nm),
            in_specs=[pl.BlockSpec((tmm, S), lambda q, m, core: (m, 0)),
                      pl.BlockSpec((None, S, W), lambda q, m, core: (jnp.minimum(2 * q + core[0], 3), 0, 0)),
                      pl.BlockSpec((None, S, W), lambda q, m, core: (jnp.maximum(2 * q + core[0] - 4, 0), 0, 0))]
            + extra_in,
            out_specs=out_specs, scratch_shapes=scratch),
        compiler_params=_params("arbitrary", "arbitrary"),
    )(core, ht, dpr, dps, *(() if to_sibling is None else (to_sibling,)))


def _grad_w_out(mix_r, mix_s, dx2b):
    S, W = mix_r.shape
    D = dx2b.shape[1]
    tmm = min(512, W)
    tk = min(2048, S)

    def body(r_ref, s_ref, b_ref, o_ref):
        j, kk = pl.program_id(0), pl.program_id(2)

        def acc(a_ref):
            part = _dot_tn(a_ref[...], b_ref[...])

            @pl.when(kk == 0)
            def _():
                o_ref[...] = part

            @pl.when(kk > 0)
            def _():
                o_ref[...] += part

        pl.when(j == 0)(lambda: acc(r_ref))
        pl.when(j == 1)(lambda: acc(s_ref))

    return pl.pallas_call(
        body, name="grad_w_out", out_shape=jax.ShapeDtypeStruct((2, W, D), F32), grid=(2, W // tmm, S // tk),
        in_specs=[pl.BlockSpec((tk, tmm), lambda j, m, k: (k, m)),
                  pl.BlockSpec((tk, tmm), lambda j, m, k: (k, m)),
                  pl.BlockSpec((tk, D), lambda j, m, k: (k, 0))],
        out_specs=pl.BlockSpec((None, tmm, D), lambda j, m, k: (j, m, 0)),
        compiler_params=_params("parallel", "parallel", "arbitrary"),
    )(mix_r, mix_s, dx2b)


def _dh_matmul(dpr, dps, w_all, chip_sums):
    _, S, W = dpr.shape
    D = w_all.shape[1]
    tm = min(1024, S)
    ni = S // tm

    def body(r_ref, s_ref, w_ref, sa_ref, dh_ref, ra_ref, send_sems, recv_sems):
        i, j = pl.program_id(0), pl.program_id(1)
        start_exchange, wait_exchange = _exchange_chip_sums((sa_ref,), (ra_ref,), send_sems, recv_sems)
        pl.when((i == 0) & (j == 0))(start_exchange)

        def acc(b_ref):
            part = _dot_nt(b_ref[...], w_ref[...])

            @pl.when(j == 0)
            def _():
                dh_ref[...] = part

            @pl.when(j > 0)
            def _():
                dh_ref[...] += part

        pl.when(j < 4)(lambda: acc(r_ref))
        pl.when(j >= 4)(lambda: acc(s_ref))
        pl.when((i == ni - 1) & (j == 7))(wait_exchange)

    hbm = pl.BlockSpec(memory_space=pltpu.HBM)
    return pl.pallas_call(
        body, name="dh_matmul",
        out_shape=(jax.ShapeDtypeStruct((S, D), F32), jax.ShapeDtypeStruct((3,) + chip_sums.shape[1:], chip_sums.dtype)),
        grid=(ni, 8),
        in_specs=[pl.BlockSpec((None, tm, W), lambda i, j: (jnp.minimum(j, 3), i, 0)),
                  pl.BlockSpec((None, tm, W), lambda i, j: (jnp.maximum(j - 4, 0), i, 0)),
                  pl.BlockSpec((None, D, W), lambda i, j: (j, 0, 0)), hbm],
        out_specs=(pl.BlockSpec((tm, D), lambda i, j: (i, 0)), hbm),
        scratch_shapes=[pltpu.SemaphoreType.DMA((1, 3)), pltpu.SemaphoreType.DMA((1, 3))],
        compiler_params=_params("arbitrary", "arbitrary"),
    )(dpr, dps, w_all, chip_sums)


def _norm_bwd(x, dx2, dh, gain):
    S, D = x.shape
    tm = min(256, S)

    def body(x_ref, dx2_ref, dh_ref, g_ref, gx_ref, dgain_ref):
        @pl.when(pl.program_id(0) == 0)
        def _():
            dgain_ref[...] = jnp.zeros_like(dgain_ref)

        xv, dh_v = x_ref[...], dh_ref[...]
        r1 = lax.rsqrt(jnp.mean(xv * xv, axis=-1, keepdims=True) + EPS)
        n = xv * r1
        dgain_ref[...] += jnp.sum(dh_v * n, axis=0, keepdims=True)
        dn = dh_v * g_ref[...]
        gx_ref[...] = dx2_ref[...] + r1 * (dn - n * jnp.mean(dn * n, axis=-1, keepdims=True))

    row = pl.BlockSpec((tm, D), lambda i: (i, 0))
    one = pl.BlockSpec((1, D), lambda i: (0, 0))
    return pl.pallas_call(
        body, name="norm_bwd", out_shape=(jax.ShapeDtypeStruct((S, D), F32), jax.ShapeDtypeStruct((1, D), F32)),
        grid=(S // tm,), in_specs=[row, row, row, one], out_specs=(row, one),
        compiler_params=_params("arbitrary"),
    )(x, dx2, dh, gain)


def _own_block(gw, pos, q):
    return q if gw.shape[0] == 4 else 2 * q + pos[0]


def _rs_local_sum(gw, rin, pos):
    _, R, C = gw.shape
    tr = min(1024, R)
    other = lambda k, pos: (pos[1] + 1 + k) % 4

    def body(pos_ref, a_ref, b_ref, o_ref):
        o_ref[...] = (a_ref[...] + b_ref[...]).astype(BF16)

    return pl.pallas_call(
        body, name="rs_local_sum", out_shape=jax.ShapeDtypeStruct((4, R, C), BF16),
        grid_spec=pltpu.PrefetchScalarGridSpec(
            num_scalar_prefetch=1, grid=(3, R // tr),
            in_specs=[pl.BlockSpec((None, tr, C), lambda k, i, pos: (_own_block(gw, pos, other(k, pos)), i, 0)),
                      pl.BlockSpec((None, tr, C), lambda k, i, pos: (other(k, pos), i, 0))],
            out_specs=pl.BlockSpec((None, tr, C), lambda k, i, pos: (other(k, pos), i, 0))),
        compiler_params=_params("parallel", "parallel"),
    )(pos, gw, rin)


def _adamw(w, g, m, v):
    m2 = ADAM_B1 * m + (1.0 - ADAM_B1) * g
    v2 = ADAM_B2 * v + (1.0 - ADAM_B2) * (g * g)
    m_hat = m2 / (1.0 - ADAM_B1 ** ADAM_STEP)
    v_hat = v2 / (1.0 - ADAM_B2 ** ADAM_STEP)
    delta = -ADAM_LR * (m_hat / (jnp.sqrt(v_hat) + ADAM_EPS) + ADAM_WD * w)
    return delta, m2, v2


def _adamw_shard(gw, rin, rb, w, m, v, pos):
    _, R, C = gw.shape
    tr = min(256, R)

    def body(pos_ref, a_ref, b_ref, rb_ref, w_ref, m_ref, v_ref, g_ref, d_ref, m2_ref, v2_ref):
        g = a_ref[...] + b_ref[...]
        for k in range(3):
            g = g + rb_ref[k].astype(F32)
        g_ref[...] = g
        d_ref[...], m2_ref[...], v2_ref[...] = _adamw(w_ref[...], g, m_ref[...], v_ref[...])

    plain = pl.BlockSpec((tr, C), lambda i, pos: (i, 0))
    shape = jax.ShapeDtypeStruct((R, C), F32)
    return pl.pallas_call(
        body, name="adamw_shard", out_shape=(shape,) * 4,
        grid_spec=pltpu.PrefetchScalarGridSpec(
            num_scalar_prefetch=1, grid=(R // tr,),
            in_specs=[pl.BlockSpec((None, tr, C), lambda i, pos: (_own_block(gw, pos, pos[1]), i, 0)),
                      pl.BlockSpec((None, tr, C), lambda i, pos: (pos[1], i, 0)),
                      pl.BlockSpec((3, tr, C), lambda i, pos: (0, i, 0)), plain, plain, plain],
            out_specs=(plain,) * 4),
        compiler_params=_params("parallel"),
    )(pos, gw, rin, rb, w, m, v)


def _adamw_small(parts, w, m, v):
    _, rows, n = parts.shape

    def body(p_ref, w_ref, m_ref, v_ref, g_ref, d_ref, m2_ref, v2_ref):
        g = p_ref[0]
        for d in range(1, N_DEV):
            g = g + p_ref[d]
        g_ref[...] = g
        d_ref[...], m2_ref[...], v2_ref[...] = _adamw(w_ref[...], g, m_ref[...], v_ref[...])

    shape = jax.ShapeDtypeStruct((rows, n), F32)
    return pl.pallas_call(body, name="adamw_small", out_shape=(shape,) * 4)(parts, w, m, v)


def _rope_tables(S):
    half = HEAD_DIM // 2
    inv = ROPE_THETA ** (-jnp.arange(half, dtype=F32) / half)
    ang = jnp.arange(S, dtype=F32)[:, None] * inv[None, :]
    cos, sin = jnp.cos(ang), jnp.sin(ang)
    return jnp.concatenate([cos, cos], axis=1), jnp.concatenate([-sin, sin], axis=1)


def _retention_tables(H):
    lg = jnp.log1p(-jnp.exp2(-5.0 - jnp.arange(H, dtype=F32)))
    n = jnp.arange(CHUNK, dtype=F32)
    rel = n[:, None] - n[None, :]
    decay = jnp.where(rel >= 0, jnp.exp(lg[:, None, None] * jnp.maximum(rel, 0.0)), 0.0)
    shape = (H, CHUNK, HEAD_DIM)
    xi = jnp.broadcast_to(jnp.exp(lg[:, None] * (n + 1.0))[:, :, None], shape)
    zeta = jnp.broadcast_to(jnp.exp(lg[:, None] * (CHUNK - 1.0 - n))[:, :, None], shape)
    gamma_c = jnp.broadcast_to(jnp.exp(lg * CHUNK)[:, None, None], shape)
    return decay, xi, zeta, gamma_c


def _pack_small(parts):
    flat = []
    for p in parts:
        p = p.reshape(-1)
        flat.append(jnp.pad(p, (0, -p.shape[0] % LANES)))
    flat = jnp.concatenate(flat)
    return jnp.pad(flat, (0, SMALL_N - flat.shape[0])).reshape(SUBLANES, SMALL_N // SUBLANES)


def _unpack_small(packed, shapes):
    flat = packed.reshape(-1)
    out, at = [], 0
    for shp in shapes:
        size = 1
        for s in shp:
            size *= s
        out.append(flat[at:at + size].reshape(shp))
        at += size + (-size % LANES)
    return out


def kernel(x, norm_gain, w_in, ret_gn_gain, ret_gn_bias, sb_norm_gain, w_out, final_norm_gain, loss_target, m_norm_gain, m_w_in, m_ret_gn_gain, m_ret_gn_bias, m_sb_norm_gain, m_w_out, m_final_norm_gain, v_norm_gain, v_w_in, v_ret_gn_gain, v_ret_gn_bias, v_sb_norm_gain, v_w_out, v_final_norm_gain):
    S, D = x.shape[1], x.shape[2]
    W = w_in.shape[2]
    wo_rows = w_out.shape[1]
    H = W // HEAD_DIM
    xs, tgt = x[0], loss_target[0]
    mx, my, mc = _mesh_pos()
    pos = jnp.stack([mc, 2 * mx + my]).astype(jnp.int32)

    cos, sin = _rope_tables(S)
    tabs = _retention_tables(H)

    proj, w_all, ht = _in_proj_gather(xs, norm_gain, w_in[0].astype(BF16), cos, sin, _gather_order())
    mix_r = _ret_fwd(proj, tabs, ret_gn_gain, ret_gn_bias)
    mix_s, raw_s, carries, wo_all = _sb_fwd(proj, sb_norm_gain, w_out[0].astype(BF16))
    wo_full = wo_all.reshape(N_DEV * wo_rows, D)
    dx2, dx2b, dmix, loss_p, d_gf = _out_proj_loss(mix_r, mix_s, wo_full, xs, tgt, final_norm_gain[None])

    gwo = _grad_w_out(mix_r, mix_s, dx2b).reshape(N_DEV, wo_rows, D)
    dpr, d_rgain, d_rbias, rino = _ret_bwd(proj, dmix, tabs, ret_gn_gain, ret_gn_bias, cos, sin, gwo)
    dps, d_sgain, rbo = _sb_bwd(proj, raw_s, carries, dmix, sb_norm_gain, _rs_local_sum(gwo, rino, pos))
    gw_sibling, = _grad_w_in_half(ht, dpr, dps, (1 - mc).reshape(1).astype(jnp.int32), "grad_w_in_sibling")
    gw, rin = _grad_w_in_half(ht, dpr, dps, mc.reshape(1).astype(jnp.int32), "grad_w_in_own", to_sibling=gw_sibling)
    dh, rb = _dh_matmul(dpr, dps, w_all, _rs_local_sum(gw, rin, pos))
    grad_x, d_gain = _norm_bwd(xs, dx2, dh, norm_gain)
    g_in, d_in, m_in, v_in = _adamw_shard(gw, rin, rb, w_in[0], m_w_in[0], v_w_in[0], pos)
    g_out, d_out, m_out, v_out = _adamw_shard(gwo, rino, rbo, w_out[0], m_w_out[0], v_w_out[0], pos)

    small_w = [norm_gain, ret_gn_gain, ret_gn_bias, sb_norm_gain, final_norm_gain]
    small_m = [m_norm_gain, m_ret_gn_gain, m_ret_gn_bias, m_sb_norm_gain, m_final_norm_gain]
    small_v = [v_norm_gain, v_ret_gn_gain, v_ret_gn_bias, v_sb_norm_gain, v_final_norm_gain]
    shapes = [()] + [w.shape for w in small_w]
    zero = jnp.zeros((), F32)
    parts = _small_all_gather(_pack_small([loss_p[0, 0], d_gain, d_rgain, d_rbias, d_sgain, d_gf]))
    packed = _adamw_small(parts, _pack_small([zero] + small_w), _pack_small([zero] + small_m),
                          _pack_small([zero] + small_v))
    g_s, d_s, m_s, v_s = (_unpack_small(p, shapes) for p in packed)

    grads = [g_s[1], g_in[None], g_s[2], g_s[3], g_s[4], g_out[None], g_s[5]]
    deltas = [d_s[1], d_in[None], d_s[2], d_s[3], d_s[4], d_out[None], d_s[5]]
    new_m = [m_s[1], m_in[None], m_s[2], m_s[3], m_s[4], m_out[None], m_s[5]]
    new_v = [v_s[1], v_in[None], v_s[2], v_s[3], v_s[4], v_out[None], v_s[5]]
    return (g_s[0], grad_x[None], *grads, *deltas, *new_m, *new_v)
```

```python
import jax
import jax.numpy as jnp
from jax import lax
from jax.experimental import pallas as pl
from jax.experimental.pallas import tpu as pltpu

F32 = jnp.float32
BF16 = jnp.bfloat16

HEAD_DIM = 128
CHUNK = 128
RET_GROUP = 16
ROPE_THETA = 10000.0
EPS = 1e-6
ADAM_LR = 0.001
ADAM_B1 = 0.9
ADAM_B2 = 0.999
ADAM_EPS = 1e-08
ADAM_WD = 0.01
ADAM_STEP = 10

N_DEV = 8
LANES = 128
SUBLANES = 8
VMEM_LIMIT = 56 * 1024 * 1024
SB_BLOCK = 256
SB_PER_STEP = 4
SB_ROW_SPLIT = 2
SB_FWD_PER_STEP = 8
SMALL_N = 8192
EXP_IS_ZERO_BELOW = -104.0
NOT_VISITED = -1e30
MESH = pl.DeviceIdType.MESH

NT = (((1,), (1,)), ((), ()))
TN = (((0,), (0,)), ((), ()))


def _params(*sem):
    return pltpu.CompilerParams(dimension_semantics=sem if sem else None, vmem_limit_bytes=VMEM_LIMIT)


def _dot(a, b):
    return jnp.dot(a, b, preferred_element_type=F32)


def _dot_nt(a, b):
    return lax.dot_general(a, b, NT, preferred_element_type=F32)


def _dot_tn(a, b):
    return lax.dot_general(a, b, TN, preferred_element_type=F32)


def _sigmoid(g):
    return 1.0 / (1.0 + jnp.exp(-g))


def _rot(a, cos, sin_signed):
    return a * cos + pltpu.roll(a, HEAD_DIM // 2, 1) * sin_signed


def _mesh_pos():
    return lax.axis_index("x"), lax.axis_index("y"), lax.axis_index("c")


def _to_sibling_copies(blocks, out_ref, send_sems, recv_sems):
    x, y, c = _mesh_pos()
    return [pltpu.make_async_remote_copy(
        src_ref=block, dst_ref=out_ref.at[k], send_sem=send_sems.at[k], recv_sem=recv_sems.at[k],
        device_id=(x, y, 1 - c), device_id_type=MESH) for k, block in enumerate(blocks)]


def _exchange_chip_sums(srcs, outs, send_sems, recv_sems):
    x, y, c = _mesh_pos()
    copies = []
    for arr, (src, out) in enumerate(zip(srcs, outs)):
        for k in range(1, 4):
            px = 1 - x if k & 2 else x
            py = 1 - y if k & 1 else y
            copies.append(pltpu.make_async_remote_copy(
                src_ref=src.at[2 * px + py], dst_ref=out.at[k - 1],
                send_sem=send_sems.at[arr, k - 1], recv_sem=recv_sems.at[arr, k - 1],
                device_id=(px, py, c), device_id_type=MESH))

    def start():
        for cp in copies:
            cp.start()

    def wait():
        for cp in copies:
            cp.wait_recv()
        for cp in copies:
            cp.wait_send()

    return start, wait


def _small_all_gather(small):
    rows, n = small.shape

    def body(s_ref, o_ref, send_sems, recv_sems, local_sem):
        start, wait = _exchange_with_all(s_ref, o_ref, send_sems, recv_sems, local_sem)
        start()
        wait()

    vmem = pl.BlockSpec(memory_space=pltpu.VMEM)
    return pl.pallas_call(
        body, name="small_all_gather",
        out_shape=jax.ShapeDtypeStruct((N_DEV, rows, n), small.dtype),
        in_specs=[vmem], out_specs=vmem,
        scratch_shapes=[pltpu.SemaphoreType.DMA((N_DEV - 1,)), pltpu.SemaphoreType.DMA((N_DEV - 1,)),
                        pltpu.SemaphoreType.DMA],
    )(small)


GATHER_SPLIT = 2
GATHER_STEPS = ([("own", 0, p) for p in range(GATHER_SPLIT)] + [("sibling", 0, p) for p in range(GATHER_SPLIT)]
                + [step for p in range(GATHER_SPLIT) for step in
                   (("ici", 0, p), ("ici", 1, p), ("passed", 0, p), ("passed", 1, p))]
                + [step for p in range(GATHER_SPLIT) for step in (("ici", 2, p), ("passed", 2, p))])


def _via_x(p):
    return p % 2 == 0


def _gather_order():
    x, y, c = _mesh_pos()
    chips = [(1 - x, y), (x, 1 - y), (1 - x, 1 - y)]
    owner = {"own": lambda j: (x, y, c), "sibling": lambda j: (x, y, 1 - c),
             "ici": lambda j: (*chips[j], c), "passed": lambda j: (*chips[j], 1 - c)}
    blocks = [4 * px + 2 * py + pc for px, py, pc in (owner[kind](j) for kind, j, _ in GATHER_STEPS)]
    return (jnp.stack(blocks).astype(jnp.int32), jnp.array([p for _, _, p in GATHER_STEPS], jnp.int32))


def _in_proj_gather(x, gain, w_shard, cos, sin, order):
    S, D = x.shape
    W = w_shard.shape[1]
    wp = W // GATHER_SPLIT
    tm = min(1024, S)
    ni = S // tm
    n_steps = len(GATHER_STEPS)

    def body(blk_ref, piece_ref, x_ref, g_ref, w_ref, cos_ref, sin_ref, o_ref, wall_ref, ht_ref, h_scr, wbuf,
             send_sems, recv_sems, local_sem, load_sems):
        step, i = pl.program_id(0), pl.program_id(1)
        mx, my, c = _mesh_pos()
        me, sibling = (mx, my, c), (mx, my, 1 - c)
        chips = [(1 - mx, my), (mx, 1 - my), (1 - mx, 1 - my)]

        def piece_of(dev, p):
            px, py, pc = dev
            return wall_ref.at[4 * px + 2 * py + pc, :, pl.ds(p * wp, wp)]

        def copy(k, p, block, to, own=False):
            dst = piece_of(block, p)
            return pltpu.make_async_remote_copy(
                src_ref=w_ref.at[:, pl.ds(p * wp, wp)] if own else dst, dst_ref=dst,
                send_sem=send_sems.at[k, p], recv_sem=recv_sems.at[k, p], device_id=to, device_id_type=MESH)

        pieces = range(GATHER_SPLIT)
        first = [cp for p in pieces for cp in
                 [copy(0, p, me, sibling, own=True)] + [copy(1 + j, p, me, (*chips[j], c), own=True) for j in (0, 1)]]
        passed = {(j, p): copy(4 + j, p, (*chip, c), sibling) for j, chip in enumerate(chips) for p in pieces}
        onward = {p: copy(3, p, (*chips[0 if _via_x(p) else 1], c), (*chips[1 if _via_x(p) else 0], c)) for p in pieces}
        mine = pltpu.make_async_copy(w_ref, wall_ref.at[4 * mx + 2 * my + c], local_sem)

        def load(s):
            kind, j, p = GATHER_STEPS[s]
            if kind == "own":
                src = w_ref.at[:, pl.ds(p * wp, wp)]
            elif kind == "sibling":
                copy(0, p, sibling, me).wait_recv()
                src = piece_of(sibling, p)
            elif kind == "ici":
                copy(1 + j, p, (*chips[j], c), me).wait_recv()
                if j == (0 if _via_x(p) else 1):
                    onward[p].start()
                passed[j, p].start()
                src = piece_of((*chips[j], c), p)
            else:
                copy(4 + j, p, (*chips[j], 1 - c), me).wait_recv()
                src = piece_of((*chips[j], 1 - c), p)
            return pltpu.make_async_copy(src, wbuf.at[s % 2], load_sems.at[s % 2])

        for s in range(n_steps):
            @pl.when((step == s) & (i == 0))
            def _(s=s):
                if s == 0:
                    for cp in first:
                        cp.start()
                    mine.start()
                    load(0).start()
                pltpu.make_async_copy(w_ref.at[:, pl.ds(0, wp)], wbuf.at[s % 2], load_sems.at[s % 2]).wait()

            if s + 1 < n_steps:
                @pl.when((step == s) & (i == ni - 1))
                def _(s=s):
                    load(s + 1).start()

        rows = pl.ds(pl.multiple_of(i * tm, tm), tm)

        @pl.when(step == 0)
        def _():
            xv = x_ref[...]
            r = lax.rsqrt(jnp.mean(xv * xv, axis=-1, keepdims=True) + EPS)
            hv = xv * r * g_ref[...]
            h_scr[rows, :] = hv.astype(BF16)
            ht_ref[...] = hv.T.astype(BF16)

        acc = _dot(h_scr[rows, :], wbuf[step % 2])
        b = blk_ref[step]

        @pl.when(b >= 2)
        def _():
            o_ref[...] = acc

        @pl.when(b < 2)
        def _():
            scale = jnp.where(b == 1, HEAD_DIM ** -0.5, 1.0).astype(F32)
            cs, sn = cos_ref[...], sin_ref[...]
            for hh in range(wp // HEAD_DIM):
                cols = slice(hh * HEAD_DIM, (hh + 1) * HEAD_DIM)
                o_ref[:, cols] = _rot(acc[:, cols], cs, sn) * scale

        @pl.when((step == n_steps - 1) & (i == ni - 1))
        def _():
            for cp in first + list(passed.values()) + list(onward.values()):
                cp.wait_send()
            mine.wait()

    hbm = pl.BlockSpec(memory_space=pltpu.HBM)
    rope = pl.BlockSpec((tm, HEAD_DIM), lambda s, i, blk, piece: (i, 0))
    first_pass = lambda s, i: jnp.where(s == 0, i, ni - 1)
    return pl.pallas_call(
        body, name="in_proj_gather",
        out_shape=(jax.ShapeDtypeStruct((N_DEV, S, W), F32), jax.ShapeDtypeStruct((N_DEV, D, W), BF16),
                   jax.ShapeDtypeStruct((D, S), BF16)),
        grid_spec=pltpu.PrefetchScalarGridSpec(
            num_scalar_prefetch=2, grid=(n_steps, ni),
            in_specs=[pl.BlockSpec((tm, D), lambda s, i, blk, piece: (first_pass(s, i), 0)),
                      pl.BlockSpec((1, D), lambda s, i, blk, piece: (0, 0)), hbm, rope, rope],
            out_specs=(pl.BlockSpec((None, tm, wp), lambda s, i, blk, piece: (blk[s], i, piece[s])), hbm,
                       pl.BlockSpec((D, tm), lambda s, i, blk, piece: (0, first_pass(s, i)))),
            scratch_shapes=[pltpu.VMEM((S, D), BF16), pltpu.VMEM((2, D, wp), BF16),
                            pltpu.SemaphoreType.DMA((7, GATHER_SPLIT)), pltpu.SemaphoreType.DMA((7, GATHER_SPLIT)),
                            pltpu.SemaphoreType.DMA, pltpu.SemaphoreType.DMA((2,))]),
        compiler_params=_params("arbitrary", "arbitrary"),
    )(*order, x, gain, w_shard, cos, sin)


def _head_spec(S, j):
    return pl.BlockSpec((None, S, HEAD_DIM), lambda h, *_: (j, 0, h))


def _bdot(a, b):
    return lax.dot_general(a, b, (((2,), (1,)), ((0,), (0,))), preferred_element_type=F32)


def _bdot_nt(a, b):
    return lax.dot_general(a, b, (((2,), (2,)), ((0,), (0,))), preferred_element_type=F32)


def _bdot_tn(a, b):
    return lax.dot_general(a, b, (((1,), (1,)), ((0,), (0,))), preferred_element_type=F32)


def _chunks(a):
    return a.reshape(a.shape[0] // CHUNK, CHUNK, a.shape[1])


def _ret_group(q, k, vb, states_b, dec, xi, ze):
    qb, kb = q.astype(BF16), k.astype(BF16)
    sb = (_bdot_nt(qb, kb) * dec).astype(BF16)
    qx = (q * xi).astype(BF16)
    out = _bdot(sb, vb) + _bdot(qx, states_b)
    return out, (qb, kb, sb, qx)


def _ret_states(kz, vb, gam, state, states_ref):
    kv = _bdot_tn(kz, vb)
    for u in range(RET_GROUP):
        states_ref[u] = state
        state = gam * state + kv[u]
    return state


def _table_specs():
    return [pl.BlockSpec((None, CHUNK, HEAD_DIM), lambda h, *_: (h, 0, 0))] * 4


def _ret_fwd(proj, tabs, gn_gain, gn_bias):
    _, S, W = proj.shape
    H, nc = W // HEAD_DIM, S // CHUNK
    assert nc % RET_GROUP == 0
    rows_per_group = RET_GROUP * CHUNK

    def body(q_ref, k_ref, v_ref, g_ref, dec_ref, xi_ref, ze_ref, gam_ref, gain_ref, bias_ref, o_ref, states_ref):
        dec, xi, ze, gam = dec_ref[...], xi_ref[...], ze_ref[...], gam_ref[...]
        gain, bias = gain_ref[...], bias_ref[...]

        def group(i, state):
            rows = pl.ds(pl.multiple_of(i * rows_per_group, rows_per_group), rows_per_group)
            q, k, vb = _chunks(q_ref[rows, :]), _chunks(k_ref[rows, :]), _chunks(v_ref[rows, :]).astype(BF16)
            state = _ret_states((k * ze).astype(BF16), vb, gam, state, states_ref)
            out, _ = _ret_group(q, k, vb, states_ref[...].astype(BF16), dec, xi, ze)
            mu = jnp.mean(out, axis=-1, keepdims=True)
            d = out - mu
            yn = d * lax.rsqrt(jnp.mean(d * d, axis=-1, keepdims=True) + EPS)
            g = _chunks(g_ref[rows, :])
            mix = g * _sigmoid(g) * (yn * gain + bias)
            o_ref[rows, :] = mix.reshape(rows_per_group, HEAD_DIM).astype(BF16)
            return state

        lax.fori_loop(0, nc // RET_GROUP, group, jnp.zeros((HEAD_DIM, HEAD_DIM), F32))

    vec = pl.BlockSpec((1, HEAD_DIM), lambda h: (0, h))
    return pl.pallas_call(
        body, name="ret_fwd", out_shape=jax.ShapeDtypeStruct((S, W), BF16), grid=(H,),
        in_specs=[_head_spec(S, 0), _head_spec(S, 1), _head_spec(S, 2), _head_spec(S, 3)] + _table_specs() + [vec, vec],
        out_specs=pl.BlockSpec((S, HEAD_DIM), lambda h: (0, h)),
        scratch_shapes=[pltpu.VMEM((RET_GROUP, HEAD_DIM, HEAD_DIM), F32)],
        compiler_params=_params("parallel"),
    )(proj, proj, proj, proj, *tabs, gn_gain, gn_bias)


def _sb_scores(qb, kk, masked, causal, upper):
    z = _dot_nt(qb, kk) * (HEAD_DIM ** -0.5)
    e = jnp.exp(-jnp.abs(z))
    l1p = jnp.log(1.0 + e)
    log_beta = jnp.minimum(z, 0.0) - l1p
    lk = jnp.minimum(-z, 0.0) - l1p
    if masked:
        lk = jnp.where(causal, lk, 0.0)
    hi = lk.astype(BF16)
    lo = (lk - hi.astype(F32)).astype(BF16)
    cs = _dot(hi, upper) + _dot(lo, upper)
    return log_beta, lk, cs


def _tri(B, kind):
    r = lax.broadcasted_iota(jnp.int32, (B, B), 0)
    c = lax.broadcasted_iota(jnp.int32, (B, B), 1)
    return {"gt": r > c, "lt": r < c}[kind]


def _ones_where(mask):
    return jnp.where(mask, 1.0, 0.0).astype(BF16)


def _exchange_with_all(src_ref, out_ref, send_sems, recv_sems, local_sem):
    x, y, c = _mesh_pos()
    peers = [(1 - x if k & 4 else x, 1 - y if k & 2 else y, 1 - c if k & 1 else c) for k in range(1, N_DEV)]

    def copy(k, owner, to):
        px, py, pc = owner
        return pltpu.make_async_remote_copy(
            src_ref=src_ref, dst_ref=out_ref.at[4 * px + 2 * py + pc], send_sem=send_sems.at[k],
            recv_sem=recv_sems.at[k], device_id=to, device_id_type=MESH)

    sends = [copy(k, (x, y, c), p) for k, p in enumerate(peers)]
    mine = pltpu.make_async_copy(src_ref, out_ref.at[4 * x + 2 * y + c], local_sem)

    def start():
        for cp in sends:
            cp.start()
        mine.start()

    def wait():
        for k, p in enumerate(peers):
            copy(k, p, p).wait_recv()
        for cp in sends:
            cp.wait_send()
        mine.wait()

    return start, wait


def _sb_fwd(proj, gain, wo_shard):
    _, S, W = proj.shape
    H = W // HEAD_DIM
    B = min(SB_BLOCK, S)
    nq = S // B
    assert nq <= HEAD_DIM and nq % SB_FWD_PER_STEP == 0
    ns = nq // SB_FWD_PER_STEP

    def body(q_ref, k_ref, v_ref, g_ref, gain_ref, wo_ref, mix_ref, raw_ref, car_ref, woall_ref, kb_ref, vb_ref,
             send_sems, recv_sems, local_sem):
        hd, si = pl.program_id(0), pl.program_id(1)
        start_gather, wait_gather = _exchange_with_all(wo_ref, woall_ref, send_sems, recv_sems, local_sem)
        pl.when((hd == 0) & (si == 0))(start_gather)

        @pl.when(si == 0)
        def _():
            kb_ref[...] = k_ref[...].astype(BF16)
            vb_ref[...] = v_ref[...].astype(BF16)

        causal = _tri(B, "gt")
        upper = _ones_where(causal)
        lane = lax.broadcasted_iota(jnp.int32, (B, HEAD_DIM), 1)

        def block(qb, kb, carry, acc, saved, masked):
            rows = pl.ds(pl.multiple_of(kb * B, B), B)
            log_beta, lk, cs = _sb_scores(qb, kb_ref[rows, :], masked, causal, upper)
            a = jnp.exp(log_beta + cs + carry)
            if masked:
                a = jnp.where(causal, a, 0.0)
            acc = acc + _dot(a.astype(BF16), vb_ref[rows, :])
            return carry + jnp.sum(lk, axis=1, keepdims=True), acc, jnp.where(lane == kb, carry, saved)

        init = (jnp.zeros((B, 1), F32), jnp.zeros((B, HEAD_DIM), F32), jnp.full((B, HEAD_DIM), NOT_VISITED, F32))

        def live(st):
            return (st[0] >= 0) & (jnp.max(st[1]) >= EXP_IS_ZERO_BELOW)

        def finish(u, acc, saved):
            rows = slice(u * B, (u + 1) * B)
            raw_ref[rows, :] = acc
            car_ref[rows, :] = saved
            yn = acc * lax.rsqrt(jnp.mean(acc * acc, axis=-1, keepdims=True) + EPS)
            g = g_ref[rows, :]
            mix_ref[rows, :] = (g * _sigmoid(g) * (yn * gain_ref[...])).astype(BF16)

        def whole(first_step):
            heads = []
            for u in range(SB_FWD_PER_STEP):
                qi = si * SB_FWD_PER_STEP + u
                qb = q_ref[u * B:(u + 1) * B, :].astype(BF16)
                state = block(qb, qi, *init, True)
                if not (first_step and u == 0):
                    state = block(qb, qi - 1, *state, False)
                heads.append((qi, qb, state))
            for u, (qi, qb, state) in enumerate(heads):
                if not (first_step and u == 0):
                    state = lax.while_loop(
                        live, lambda st, qb=qb: (st[0] - 1,) + block(qb, st[0], st[1], st[2], st[3], False),
                        (qi - 2,) + state)[1:]
                finish(u, state[1], state[2])

        pl.when(si == 0)(lambda: whole(True))
        pl.when(si > 0)(lambda: whole(False))
        pl.when((hd == H - 1) & (si == ns - 1))(wait_gather)

    tq = SB_FWD_PER_STEP * B
    tile = lambda j: pl.BlockSpec((None, tq, HEAD_DIM), lambda h, i: (j, i, h))
    out_tile = pl.BlockSpec((tq, HEAD_DIM), lambda h, i: (i, h))
    hbm = pl.BlockSpec(memory_space=pltpu.HBM)
    return pl.pallas_call(
        body, name="sb_fwd",
        out_shape=(jax.ShapeDtypeStruct((S, W), BF16), jax.ShapeDtypeStruct((S, W), F32),
                   jax.ShapeDtypeStruct((S, W), F32), jax.ShapeDtypeStruct((N_DEV,) + wo_shard.shape, BF16)),
        grid=(H, ns),
        in_specs=[tile(4), _head_spec(S, 5), _head_spec(S, 6), tile(7),
                  pl.BlockSpec((1, HEAD_DIM), lambda h, i: (0, h)), hbm],
        out_specs=(out_tile, out_tile, out_tile, hbm),
        scratch_shapes=[pltpu.VMEM((S, HEAD_DIM), BF16), pltpu.VMEM((S, HEAD_DIM), BF16),
                        pltpu.SemaphoreType.DMA((N_DEV - 1,)), pltpu.SemaphoreType.DMA((N_DEV - 1,)),
                        pltpu.SemaphoreType.DMA],
        compiler_params=_params("arbitrary", "arbitrary"),
    )(proj, proj, proj, proj, gain, wo_shard)


def _out_proj_loss(mix_r, mix_s, w_out, x, tgt, gf):
    S, W = mix_r.shape
    D = x.shape[1]
    tm = min(256, S)

    def body(mr_ref, ms_ref, wo_ref, x_ref, t_ref, gf_ref, dx2_ref, dx2b_ref, dmix_ref, loss_ref, gfn_ref):
        @pl.when(pl.program_id(0) == 0)
        def _():
            loss_ref[...] = jnp.zeros_like(loss_ref)
            gfn_ref[...] = jnp.zeros_like(gfn_ref)

        gfv = gf_ref[...]
        x2 = x_ref[...] + (_dot(mr_ref[...], wo_ref[:W, :]) + _dot(ms_ref[...], wo_ref[W:, :]))
        r2 = lax.rsqrt(jnp.mean(x2 * x2, axis=-1, keepdims=True) + EPS)
        n = x2 * r2
        err = n * gfv - t_ref[...]
        loss_ref[...] += 0.5 * jnp.sum(jnp.mean(err * err, axis=-1, keepdims=True))
        dy = err * (1.0 / D)
        gfn_ref[...] += jnp.sum(dy * n, axis=0, keepdims=True)
        dn = dy * gfv
        dx2 = r2 * (dn - n * jnp.mean(dn * n, axis=-1, keepdims=True))
        dx2_ref[...] = dx2
        b = dx2.astype(BF16)
        dx2b_ref[...] = b
        dmix_ref[:, :W] = _dot_nt(b, wo_ref[:W, :])
        dmix_ref[:, W:] = _dot_nt(b, wo_ref[W:, :])

    row = lambda width: pl.BlockSpec((tm, width), lambda i: (i, 0))
    return pl.pallas_call(
        body, name="out_proj_loss",
        out_shape=(jax.ShapeDtypeStruct((S, D), F32), jax.ShapeDtypeStruct((S, D), BF16),
                   jax.ShapeDtypeStruct((S, 2 * W), F32), jax.ShapeDtypeStruct((SUBLANES, LANES), F32),
                   jax.ShapeDtypeStruct((1, D), F32)),
        grid=(S // tm,),
        in_specs=[row(W), row(W), pl.BlockSpec((2 * W, D), lambda i: (0, 0)), row(D), row(D),
                  pl.BlockSpec((1, D), lambda i: (0, 0))],
        out_specs=(row(D), row(D), row(2 * W), pl.BlockSpec((SUBLANES, LANES), lambda i: (0, 0)),
                   pl.BlockSpec((1, D), lambda i: (0, 0))),
        compiler_params=_params("arbitrary"),
    )(mix_r, mix_s, w_out, x, tgt, gf)


def _silu_bwd(g, dm, normed):
    sig = _sigmoid(g)
    return dm * (g * sig), dm * normed * (sig * (1.0 + g * (1.0 - sig)))


def _ret_bwd(proj, dmix, tabs, gn_gain, gn_bias, cos, sin, gwo):
    _, S, W = proj.shape
    H, nc = W // HEAD_DIM, S // CHUNK
    assert nc % RET_GROUP == 0
    ng = nc // RET_GROUP
    rows_per_group = RET_GROUP * CHUNK

    def body(q_ref, k_ref, v_ref, g_ref, dm_ref, dec_ref, xi_ref, ze_ref, gam_ref, gain_ref, bias_ref, cos_ref,
             sin_ref, gwo_ref, dp_ref, dgain_ref, dbias_ref, rino_ref, rs_ref, dstates_ref, send_sems, recv_sems):
        dec, xi, ze, gam = dec_ref[...], xi_ref[...], ze_ref[...], gam_ref[...]
        gain, bias = gain_ref[...], bias_ref[...]
        hd = pl.program_id(0)
        other_core = 1 - lax.axis_index("c")
        copies = _to_sibling_copies([gwo_ref.at[2 * k + other_core] for k in range(4)], rino_ref, send_sems, recv_sems)

        @pl.when(hd == 0)
        def _():
            for cp in copies:
                cp.start()

        def group_rows(i):
            return pl.ds(pl.multiple_of(i * rows_per_group, rows_per_group), rows_per_group)

        def fwd_group(i, state):
            rows = group_rows(i)
            kz = (_chunks(k_ref[rows, :]) * ze).astype(BF16)
            return _ret_states(kz, _chunks(v_ref[rows, :]).astype(BF16), gam, state,
                               rs_ref.at[pl.ds(i * RET_GROUP, RET_GROUP)])

        lax.fori_loop(0, ng, fwd_group, jnp.zeros((HEAD_DIM, HEAD_DIM), F32))

        flat = lambda a: a.reshape(rows_per_group, HEAD_DIM)

        def bwd_group(t, carry):
            dgain, dbias, dstate = carry
            i = ng - 1 - t
            rows = group_rows(i)
            q, k, g = _chunks(q_ref[rows, :]), _chunks(k_ref[rows, :]), _chunks(g_ref[rows, :])
            vb = _chunks(v_ref[rows, :]).astype(BF16)
            rb = rs_ref[pl.ds(i * RET_GROUP, RET_GROUP)].astype(BF16)
            out, (qb, kb, sb, qx) = _ret_group(q, k, vb, rb, dec, xi, ze)
            kz = (k * ze).astype(BF16)
            mu = jnp.mean(out, axis=-1, keepdims=True)
            d = out - mu
            rstd = lax.rsqrt(jnp.mean(d * d, axis=-1, keepdims=True) + EPS)
            yn = d * rstd
            dgn, dg = _silu_bwd(g, _chunks(dm_ref[rows, :]), yn * gain + bias)
            dgain = dgain + jnp.sum(flat(dgn * yn), axis=0, keepdims=True)
            dbias = dbias + jnp.sum(flat(dgn), axis=0, keepdims=True)
            dyn = dgn * gain
            do = rstd * (dyn - jnp.mean(dyn, axis=-1, keepdims=True)
                         - yn * jnp.mean(dyn * yn, axis=-1, keepdims=True))
            dob = do.astype(BF16)
            dkv = _bdot_tn(qx, dob)
            for u in reversed(range(RET_GROUP)):
                dstates_ref[u] = dstate
                dstate = gam * dstate + dkv[u]
            drb = dstates_ref[...].astype(BF16)
            dv = _bdot_tn(sb, dob) + _bdot(kz, drb)
            dsb = (_bdot_nt(dob, vb) * dec).astype(BF16)
            dq = _bdot(dsb, kb) + _bdot_nt(dob, rb) * xi
            dk = _bdot_tn(dsb, qb) + _bdot_nt(vb, drb) * ze
            cs, sn = cos_ref[rows, :], -sin_ref[rows, :]
            dp_ref[0, rows, :] = _rot(flat(dq), cs, sn).astype(BF16)
            dp_ref[1, rows, :] = (_rot(flat(dk), cs, sn) * (HEAD_DIM ** -0.5)).astype(BF16)
            dp_ref[2, rows, :] = flat(dv).astype(BF16)
            dp_ref[3, rows, :] = flat(dg).astype(BF16)
            return dgain, dbias, dstate

        zero = jnp.zeros((1, HEAD_DIM), F32)
        dgain, dbias, _ = lax.fori_loop(0, ng, bwd_group, (zero, zero, jnp.zeros((HEAD_DIM, HEAD_DIM), F32)))
        dgain_ref[...] = dgain
        dbias_ref[...] = dbias

        @pl.when(hd == H - 1)
        def _():
            for cp in copies:
                cp.wait_recv()
            for cp in copies:
                cp.wait_send()

    vec = pl.BlockSpec((1, HEAD_DIM), lambda h: (0, h))
    full = pl.BlockSpec((S, HEAD_DIM), lambda h: (0, 0))
    hbm = pl.BlockSpec(memory_space=pltpu.HBM)
    return pl.pallas_call(
        body, name="ret_bwd",
        out_shape=(jax.ShapeDtypeStruct((4, S, W), BF16), jax.ShapeDtypeStruct((1, W), F32),
                   jax.ShapeDtypeStruct((1, W), F32), jax.ShapeDtypeStruct((4,) + gwo.shape[1:], gwo.dtype)),
        grid=(H,),
        in_specs=[_head_spec(S, 0), _head_spec(S, 1), _head_spec(S, 2), _head_spec(S, 3),
                  pl.BlockSpec((S, HEAD_DIM), lambda h: (0, h))] + _table_specs() + [vec, vec, full, full, hbm],
        out_specs=(pl.BlockSpec((4, S, HEAD_DIM), lambda h: (0, 0, h)), vec, vec, hbm),
        scratch_shapes=[pltpu.VMEM((nc, HEAD_DIM, HEAD_DIM), F32), pltpu.VMEM((RET_GROUP, HEAD_DIM, HEAD_DIM), F32),
                        pltpu.SemaphoreType.DMA((4,)), pltpu.SemaphoreType.DMA((4,))],
        compiler_params=_params("arbitrary"),
    )(proj, proj, proj, proj, dmix, *tabs, gn_gain, gn_bias, cos, sin, gwo)


def _sb_bwd(proj, raw, carries, dmix, gain, chip_sums_o):
    _, S, W = proj.shape
    H = W // HEAD_DIM
    B = min(SB_BLOCK, S)
    nq = S // B
    ns = nq // SB_PER_STEP

    def body(q_ref, k_ref, v_ref, g_ref, raw_ref, car_ref, dm_ref, gain_ref, so_ref, dp_ref, dgain_ref, ro_ref,
             kb_ref, vb_ref, dk_ref, dv_ref, send_sems, recv_sems):
        hd, si = pl.program_id(0), pl.program_id(1)
        start_exchange, wait_exchange = _exchange_chip_sums((so_ref,), (ro_ref,), send_sems, recv_sems)
        pl.when((hd == 0) & (si == 0))(start_exchange)

        @pl.when(si == 0)
        def _():
            kb_ref[...] = k_ref[...].astype(BF16)
            vb_ref[...] = v_ref[...].astype(BF16)
            dk_ref[...] = jnp.zeros_like(dk_ref)
            dv_ref[...] = jnp.zeros_like(dv_ref)
            dgain_ref[...] = jnp.zeros_like(dgain_ref)

        causal = _tri(B, "gt")
        upper = _ones_where(causal)
        before = _ones_where(_tri(B, "lt"))
        lane = lax.broadcasted_iota(jnp.int32, (B, HEAD_DIM), 1)
        gain_v = gain_ref[...]

        def prologue(u):
            qi = si * SB_PER_STEP + u
            rows = slice(u * B, (u + 1) * B)
            o = raw_ref[rows, :]
            rstd = lax.rsqrt(jnp.mean(o * o, axis=-1, keepdims=True) + EPS)
            yn = o * rstd
            dnrm, dg = _silu_bwd(g_ref[rows, :], dm_ref[rows, :], yn * gain_v)
            dp_ref[3, pl.ds(pl.multiple_of(qi * B, B), B), :] = dg.astype(BF16)
            dgain_ref[...] += jnp.sum(dnrm * yn, axis=0, keepdims=True)
            dyn = dnrm * gain_v
            do = rstd * (dyn - yn * jnp.mean(dyn * yn, axis=-1, keepdims=True))
            return qi, q_ref[rows, :].astype(BF16), do.astype(BF16), car_ref[rows, :]

        def block(ctx, kb, carry_g, dq, masked):
            _, qb, dob, saved = ctx
            rows = pl.ds(pl.multiple_of(kb * B, B), B)
            kk, vv = kb_ref[rows, :], vb_ref[rows, :]
            part = B // SB_ROW_SPLIT
            new_carry, new_dq, dv_sum, dk_sum = [], [], None, None
            for r in range(SB_ROW_SPLIT):
                sl = slice(r * part, (r + 1) * part)
                mask = causal[sl, :]
                log_beta, _, cs = _sb_scores(qb[sl], kk, masked, mask, upper)
                carry_lk = jnp.sum(jnp.where(lane[sl] == kb, saved[sl], 0.0), axis=1, keepdims=True)
                a = jnp.exp(log_beta + cs + carry_lk)
                if masked:
                    a = jnp.where(mask, a, 0.0)
                gmat = _dot_nt(dob[sl], vv) * a
                dv_part = _dot_tn(a.astype(BF16), dob[sl])
                hi = gmat.astype(BF16)
                lo = (gmat - hi.astype(F32)).astype(BF16)
                dlk = carry_g[sl] + (_dot(hi, before) + _dot(lo, before))
                beta = jnp.exp(log_beta)
                dz = (gmat * (1.0 - beta) - dlk * beta) * (HEAD_DIM ** -0.5)
                if masked:
                    dz = jnp.where(mask, dz, 0.0)
                dzb = dz.astype(BF16)
                dk_part = _dot_tn(dzb, qb[sl])
                dv_sum = dv_part if dv_sum is None else dv_sum + dv_part
                dk_sum = dk_part if dk_sum is None else dk_sum + dk_part
                new_carry.append(carry_g[sl] + jnp.sum(gmat, axis=1, keepdims=True))
                new_dq.append(dq[sl] + _dot(dzb, kk))
            dv_ref[rows, :] += dv_sum
            dk_ref[rows, :] += dk_sum
            return jnp.concatenate(new_carry, axis=0), jnp.concatenate(new_dq, axis=0)

        init = (jnp.zeros((B, 1), F32), jnp.zeros((B, HEAD_DIM), F32))

        def whole(first_step):
            ctxs = [prologue(u) for u in range(SB_PER_STEP)]
            states = []
            for u, ctx in enumerate(ctxs):
                state = init
                if not (first_step and u == 0):
                    visited = jnp.max(ctx[3], axis=0, keepdims=True) >= EXP_IS_ZERO_BELOW
                    first = jnp.min(jnp.where(visited, lane[:1, :], ctx[0]))
                    state = lax.fori_loop(first, ctx[0] - 1,
                                          lambda i, st, ctx=ctx: block(ctx, i, st[0], st[1], False), state)
                states.append(state)
            for u, (ctx, state) in enumerate(zip(ctxs, states)):
                if not (first_step and u == 0):
                    state = block(ctx, ctx[0] - 1, *state, False)
                state = block(ctx, ctx[0], *state, True)
                dp_ref[0, pl.ds(pl.multiple_of(ctx[0] * B, B), B), :] = state[1].astype(BF16)

        pl.when(si == 0)(lambda: whole(True))
        pl.when(si > 0)(lambda: whole(False))

        @pl.when(si == ns - 1)
        def _():
            dp_ref[1] = dk_ref[...].astype(BF16)
            dp_ref[2] = dv_ref[...].astype(BF16)

        pl.when((hd == H - 1) & (si == ns - 1))(wait_exchange)

    tq = SB_PER_STEP * B
    tile = lambda j: pl.BlockSpec((None, tq, HEAD_DIM), lambda h, i: (j, i, h))
    vec = pl.BlockSpec((1, HEAD_DIM), lambda h, i: (0, h))
    hbm = pl.BlockSpec(memory_space=pltpu.HBM)
    return pl.pallas_call(
        body, name="sb_bwd",
        out_shape=(jax.ShapeDtypeStruct((4, S, W), BF16), jax.ShapeDtypeStruct((1, W), F32),
                   jax.ShapeDtypeStruct((3,) + chip_sums_o.shape[1:], chip_sums_o.dtype)),
        grid=(H, ns),
        in_specs=[tile(4), _head_spec(S, 5), _head_spec(S, 6), tile(7),
                  pl.BlockSpec((tq, HEAD_DIM), lambda h, i: (i, h)),
                  pl.BlockSpec((tq, HEAD_DIM), lambda h, i: (i, h)),
                  pl.BlockSpec((tq, HEAD_DIM), lambda h, i: (i, H + h)), vec, hbm],
        out_specs=(pl.BlockSpec((4, S, HEAD_DIM), lambda h, i: (0, 0, h)), vec, hbm),
        scratch_shapes=[pltpu.VMEM((S, HEAD_DIM), BF16), pltpu.VMEM((S, HEAD_DIM), BF16),
                        pltpu.VMEM((S, HEAD_DIM), F32), pltpu.VMEM((S, HEAD_DIM), F32),
                        pltpu.SemaphoreType.DMA((1, 3)), pltpu.SemaphoreType.DMA((1, 3))],
        compiler_params=_params("arbitrary", "arbitrary"),
    )(proj, proj, proj, proj, raw, carries, dmix, gain, chip_sums_o)


def _grad_w_in_half(ht, dpr, dps, core, name, to_sibling=None):
    D, S = ht.shape
    _, _, W = dpr.shape
    tmm = min(512, D)
    nm = D // tmm

    def body(core_ref, ht_ref, r_ref, s_ref, *rest):
        o_ref = rest[1] if to_sibling is not None else rest[0]
        q, m = pl.program_id(0), pl.program_id(1)
        if to_sibling is not None:
            ga_ref, _, rin_ref, send_sems, recv_sems = rest
            copies = _to_sibling_copies([ga_ref.at[k] for k in range(4)], rin_ref, send_sems, recv_sems)

            @pl.when((q == 0) & (m == 0))
            def _():
                for cp in copies:
                    cp.start()

        @pl.when(q < 2)
        def _():
            o_ref[...] = _dot(ht_ref[...], r_ref[...])

        @pl.when(q >= 2)
        def _():
            o_ref[...] = _dot(ht_ref[...], s_ref[...])

        if to_sibling is not None:
            @pl.when((q == 3) & (m == nm - 1))
            def _():
                for cp in copies:
                    cp.wait_recv()
                for cp in copies:
                    cp.wait_send()

    hbm = pl.BlockSpec(memory_space=pltpu.HBM)
    gw_shape = jax.ShapeDtypeStruct((4, D, W), F32)
    out_shape, out_specs, extra_in, scratch = (gw_shape,), (pl.BlockSpec((None, tmm, W), lambda q, m, core: (q, m, 0)),), [], []
    if to_sibling is not None:
        out_shape += (gw_shape,)
        out_specs += (hbm,)
        extra_in = [hbm]
        scratch = [pltpu.SemaphoreType.DMA((4,)), pltpu.SemaphoreType.DMA((4,))]
    return pl.pallas_call(
        body, name=name, out_shape=out_shape,
        grid_spec=pltpu.PrefetchScalarGridSpec(
            num_scalar_prefetch=1, grid=(4, nm),
            in_specs=[pl.BlockSpec((tmm, S), lambda q, m, core: (m, 0)),
                      pl.BlockSpec((None, S, W), lambda q, m, core: (jnp.minimum(2 * q + core[0], 3), 0, 0)),
                      pl.BlockSpec((None, S, W), lambda q, m, core: (jnp.maximum(2 * q + core[0] - 4, 0), 0, 0))]
            + extra_in,
            out_specs=out_specs, scratch_shapes=scratch),
        compiler_params=_params("arbitrary", "arbitrary"),
    )(core, ht, dpr, dps, *(() if to_sibling is None else (to_sibling,)))


def _grad_w_out(mix_r, mix_s, dx2b):
    S, W = mix_r.shape
    D = dx2b.shape[1]
    tmm = min(512, W)
    tk = min(2048, S)

    def body(r_ref, s_ref, b_ref, o_ref):
        j, kk = pl.program_id(0), pl.program_id(2)

        def acc(a_ref):
            part = _dot_tn(a_ref[...], b_ref[...])

            @pl.when(kk == 0)
            def _():
                o_ref[...] = part

            @pl.when(kk > 0)
            def _():
                o_ref[...] += part

        pl.when(j == 0)(lambda: acc(r_ref))
        pl.when(j == 1)(lambda: acc(s_ref))

    return pl.pallas_call(
        body, name="grad_w_out", out_shape=jax.ShapeDtypeStruct((2, W, D), F32), grid=(2, W // tmm, S // tk),
        in_specs=[pl.BlockSpec((tk, tmm), lambda j, m, k: (k, m)),
                  pl.BlockSpec((tk, tmm), lambda j, m, k: (k, m)),
                  pl.BlockSpec((tk, D), lambda j, m, k: (k, 0))],
        out_specs=pl.BlockSpec((None, tmm, D), lambda j, m, k: (j, m, 0)),
        compiler_params=_params("parallel", "parallel", "arbitrary"),
    )(mix_r, mix_s, dx2b)


def _dh_matmul(dpr, dps, w_all, chip_sums):
    _, S, W = dpr.shape
    D = w_all.shape[1]
    tm = min(1024, S)
    ni = S // tm

    def body(r_ref, s_ref, w_ref, sa_ref, dh_ref, ra_ref, send_sems, recv_sems):
        i, j = pl.program_id(0), pl.program_id(1)
        start_exchange, wait_exchange = _exchange_chip_sums((sa_ref,), (ra_ref,), send_sems, recv_sems)
        pl.when((i == 0) & (j == 0))(start_exchange)

        def acc(b_ref):
            part = _dot_nt(b_ref[...], w_ref[...])

            @pl.when(j == 0)
            def _():
                dh_ref[...] = part

            @pl.when(j > 0)
            def _():
                dh_ref[...] += part

        pl.when(j < 4)(lambda: acc(r_ref))
        pl.when(j >= 4)(lambda: acc(s_ref))
        pl.when((i == ni - 1) & (j == 7))(wait_exchange)

    hbm = pl.BlockSpec(memory_space=pltpu.HBM)
    return pl.pallas_call(
        body, name="dh_matmul",
        out_shape=(jax.ShapeDtypeStruct((S, D), F32), jax.ShapeDtypeStruct((3,) + chip_sums.shape[1:], chip_sums.dtype)),
        grid=(ni, 8),
        in_specs=[pl.BlockSpec((None, tm, W), lambda i, j: (jnp.minimum(j, 3), i, 0)),
                  pl.BlockSpec((None, tm, W), lambda i, j: (jnp.maximum(j - 4, 0), i, 0)),
                  pl.BlockSpec((None, D, W), lambda i, j: (j, 0, 0)), hbm],
        out_specs=(pl.BlockSpec((tm, D), lambda i, j: (i, 0)), hbm),
        scratch_shapes=[pltpu.SemaphoreType.DMA((1, 3)), pltpu.SemaphoreType.DMA((1, 3))],
        compiler_params=_params("arbitrary", "arbitrary"),
    )(dpr, dps, w_all, chip_sums)


def _norm_bwd(x, dx2, dh, gain):
    S, D = x.shape
    tm = min(256, S)

    def body(x_ref, dx2_ref, dh_ref, g_ref, gx_ref, dgain_ref):
        @pl.when(pl.program_id(0) == 0)
        def _():
            dgain_ref[...] = jnp.zeros_like(dgain_ref)

        xv, dh_v = x_ref[...], dh_ref[...]
        r1 = lax.rsqrt(jnp.mean(xv * xv, axis=-1, keepdims=True) + EPS)
        n = xv * r1
        dgain_ref[...] += jnp.sum(dh_v * n, axis=0, keepdims=True)
        dn = dh_v * g_ref[...]
        gx_ref[...] = dx2_ref[...] + r1 * (dn - n * jnp.mean(dn * n, axis=-1, keepdims=True))

    row = pl.BlockSpec((tm, D), lambda i: (i, 0))
    one = pl.BlockSpec((1, D), lambda i: (0, 0))
    return pl.pallas_call(
        body, name="norm_bwd", out_shape=(jax.ShapeDtypeStruct((S, D), F32), jax.ShapeDtypeStruct((1, D), F32)),
        grid=(S // tm,), in_specs=[row, row, row, one], out_specs=(row, one),
        compiler_params=_params("arbitrary"),
    )(x, dx2, dh, gain)


def _own_block(gw, pos, q):
    return q if gw.shape[0] == 4 else 2 * q + pos[0]


def _rs_local_sum(gw, rin, pos):
    _, R, C = gw.shape
    tr = min(1024, R)
    other = lambda k, pos: (pos[1] + 1 + k) % 4

    def body(pos_ref, a_ref, b_ref, o_ref):
        o_ref[...] = (a_ref[...] + b_ref[...]).astype(BF16)

    return pl.pallas_call(
        body, name="rs_local_sum", out_shape=jax.ShapeDtypeStruct((4, R, C), BF16),
        grid_spec=pltpu.PrefetchScalarGridSpec(
            num_scalar_prefetch=1, grid=(3, R // tr),
            in_specs=[pl.BlockSpec((None, tr, C), lambda k, i, pos: (_own_block(gw, pos, other(k, pos)), i, 0)),
                      pl.BlockSpec((None, tr, C), lambda k, i, pos: (other(k, pos), i, 0))],
            out_specs=pl.BlockSpec((None, tr, C), lambda k, i, pos: (other(k, pos), i, 0))),
        compiler_params=_params("parallel", "parallel"),
    )(pos, gw, rin)


def _adamw(w, g, m, v):
    m2 = ADAM_B1 * m + (1.0 - ADAM_B1) * g
    v2 = ADAM_B2 * v + (1.0 - ADAM_B2) * (g * g)
    m_hat = m2 / (1.0 - ADAM_B1 ** ADAM_STEP)
    v_hat = v2 / (1.0 - ADAM_B2 ** ADAM_STEP)
    delta = -ADAM_LR * (m_hat / (jnp.sqrt(v_hat) + ADAM_EPS) + ADAM_WD * w)
    return delta, m2, v2


def _adamw_shard(gw, rin, rb, w, m, v, pos):
    _, R, C = gw.shape
    tr = min(256, R)

    def body(pos_ref, a_ref, b_ref, rb_ref, w_ref, m_ref, v_ref, g_ref, d_ref, m2_ref, v2_ref):
        g = a_ref[...] + b_ref[...]
        for k in range(3):
            g = g + rb_ref[k].astype(F32)
        g_ref[...] = g
        d_ref[...], m2_ref[...], v2_ref[...] = _adamw(w_ref[...], g, m_ref[...], v_ref[...])

    plain = pl.BlockSpec((tr, C), lambda i, pos: (i, 0))
    shape = jax.ShapeDtypeStruct((R, C), F32)
    return pl.pallas_call(
        body, name="adamw_shard", out_shape=(shape,) * 4,
        grid_spec=pltpu.PrefetchScalarGridSpec(
            num_scalar_prefetch=1, grid=(R // tr,),
            in_specs=[pl.BlockSpec((None, tr, C), lambda i, pos: (_own_block(gw, pos, pos[1]), i, 0)),
                      pl.BlockSpec((None, tr, C), lambda i, pos: (pos[1], i, 0)),
                      pl.BlockSpec((3, tr, C), lambda i, pos: (0, i, 0)), plain, plain, plain],
            out_specs=(plain,) * 4),
        compiler_params=_params("parallel"),
    )(pos, gw, rin, rb, w, m, v)


def _adamw_small(parts, w, m, v):
    _, rows, n = parts.shape

    def body(p_ref, w_ref, m_ref, v_ref, g_ref, d_ref, m2_ref, v2_ref):
        g = p_ref[0]
        for d in range(1, N_DEV):
            g = g + p_ref[d]
        g_ref[...] = g
        d_ref[...], m2_ref[...], v2_ref[...] = _adamw(w_ref[...], g, m_ref[...], v_ref[...])

    shape = jax.ShapeDtypeStruct((rows, n), F32)
    return pl.pallas_call(body, name="adamw_small", out_shape=(shape,) * 4)(parts, w, m, v)


def _rope_tables(S):
    half = HEAD_DIM // 2
    inv = ROPE_THETA ** (-jnp.arange(half, dtype=F32) / half)
    ang = jnp.arange(S, dtype=F32)[:, None] * inv[None, :]
    cos, sin = jnp.cos(ang), jnp.sin(ang)
    return jnp.concatenate([cos, cos], axis=1), jnp.concatenate([-sin, sin], axis=1)


def _retention_tables(H):
    lg = jnp.log1p(-jnp.exp2(-5.0 - jnp.arange(H, dtype=F32)))
    n = jnp.arange(CHUNK, dtype=F32)
    rel = n[:, None] - n[None, :]
    decay = jnp.where(rel >= 0, jnp.exp(lg[:, None, None] * jnp.maximum(rel, 0.0)), 0.0)
    shape = (H, CHUNK, HEAD_DIM)
    xi = jnp.broadcast_to(jnp.exp(lg[:, None] * (n + 1.0))[:, :, None], shape)
    zeta = jnp.broadcast_to(jnp.exp(lg[:, None] * (CHUNK - 1.0 - n))[:, :, None], shape)
    gamma_c = jnp.broadcast_to(jnp.exp(lg * CHUNK)[:, None, None], shape)
    return decay, xi, zeta, gamma_c


def _pack_small(parts):
    flat = []
    for p in parts:
        p = p.reshape(-1)
        flat.append(jnp.pad(p, (0, -p.shape[0] % LANES)))
    flat = jnp.concatenate(flat)
    return jnp.pad(flat, (0, SMALL_N - flat.shape[0])).reshape(SUBLANES, SMALL_N // SUBLANES)


def _unpack_small(packed, shapes):
    flat = packed.reshape(-1)
    out, at = [], 0
    for shp in shapes:
        size = 1
        for s in shp:
            size *= s
        out.append(flat[at:at + size].reshape(shp))
        at += size + (-size % LANES)
    return out


def kernel(x, norm_gain, w_in, ret_gn_gain, ret_gn_bias, sb_norm_gain, w_out, final_norm_gain, loss_target, m_norm_gain, m_w_in, m_ret_gn_gain, m_ret_gn_bias, m_sb_norm_gain, m_w_out, m_final_norm_gain, v_norm_gain, v_w_in, v_ret_gn_gain, v_ret_gn_bias, v_sb_norm_gain, v_w_out, v_final_norm_gain):
    S, D = x.shape[1], x.shape[2]
    W = w_in.shape[2]
    wo_rows = w_out.shape[1]
    H = W // HEAD_DIM
    xs, tgt = x[0], loss_target[0]
    mx, my, mc = _mesh_pos()
    pos = jnp.stack([mc, 2 * mx + my]).astype(jnp.int32)

    cos, sin = _rope_tables(S)
    tabs = _retention_tables(H)

    proj, w_all, ht = _in_proj_gather(xs, norm_gain, w_in[0].astype(BF16), cos, sin, _gather_order())
    mix_r = _ret_fwd(proj, tabs, ret_gn_gain, ret_gn_bias)
    mix_s, raw_s, carries, wo_all = _sb_fwd(proj, sb_norm_gain, w_out[0].astype(BF16))
    wo_full = wo_all.reshape(N_DEV * wo_rows, D)
    dx2, dx2b, dmix, loss_p, d_gf = _out_proj_loss(mix_r, mix_s, wo_full, xs, tgt, final_norm_gain[None])

    gwo = _grad_w_out(mix_r, mix_s, dx2b).reshape(N_DEV, wo_rows, D)
    dpr, d_rgain, d_rbias, rino = _ret_bwd(proj, dmix, tabs, ret_gn_gain, ret_gn_bias, cos, sin, gwo)
    dps, d_sgain, rbo = _sb_bwd(proj, raw_s, carries, dmix, sb_norm_gain, _rs_local_sum(gwo, rino, pos))
    gw_sibling, = _grad_w_in_half(ht, dpr, dps, (1 - mc).reshape(1).astype(jnp.int32), "grad_w_in_sibling")
    gw, rin = _grad_w_in_half(ht, dpr, dps, mc.reshape(1).astype(jnp.int32), "grad_w_in_own", to_sibling=gw_sibling)
    dh, rb = _dh_matmul(dpr, dps, w_all, _rs_local_sum(gw, rin, pos))
    grad_x, d_gain = _norm_bwd(xs, dx2, dh, norm_gain)
    g_in, d_in, m_in, v_in = _adamw_shard(gw, rin, rb, w_in[0], m_w_in[0], v_w_in[0], pos)
    g_out, d_out, m_out, v_out = _adamw_shard(gwo, rino, rbo, w_out[0], m_w_out[0], v_w_out[0], pos)

    small_w = [norm_gain, ret_gn_gain, ret_gn_bias, sb_norm_gain, final_norm_gain]
    small_m = [m_norm_gain, m_ret_gn_gain, m_ret_gn_bias, m_sb_norm_gain, m_final_norm_gain]
    small_v = [v_norm_gain, v_ret_gn_gain, v_ret_gn_bias, v_sb_norm_gain, v_final_norm_gain]
    shapes = [()] + [w.shape for w in small_w]
    zero = jnp.zeros((), F32)
    parts = _small_all_gather(_pack_small([loss_p[0, 0], d_gain, d_rgain, d_rbias, d_sgain, d_gf]))
    packed = _adamw_small(parts, _pack_small([zero] + small_w), _pack_small([zero] + small_m),
                          _pack_small([zero] + small_v))
    g_s, d_s, m_s, v_s = (_unpack_small(p, shapes) for p in packed)

    grads = [g_s[1], g_in[None], g_s[2], g_s[3], g_s[4], g_out[None], g_s[5]]
    deltas = [d_s[1], d_in[None], d_s[2], d_s[3], d_s[4], d_out[None], d_s[5]]
    new_m = [m_s[1], m_in[None], m_s[2], m_s[3], m_s[4], m_out[None], m_s[5]]
    new_v = [v_s[1], v_in[None], v_s[2], v_s[3], v_s[4], v_out[None], v_s[5]]
    return (g_s[0], grad_x[None], *grads, *deltas, *new_m, *new_v)
```
